```python
import jax
import jax.numpy as jnp
from jax import lax
import numpy as np

D_MODEL = 1024
BATCH = 32
SEQ = 256
DEPTH = 2
DEC_BATCH = 2
DEC_SEQ = 1024
PAST_LEN = 512

GRID_W = 64
CHUNK = 64
Q_BLOCK = 128
EPS = 1e-6

A_HEADS = 4
A_DK = 64
A_DV = 128
A_GATE_RANK = 16
A_GATE_TEMP = 16.0
B_HEADS = 4
B_DK = 64
B_DV = 128
C_HEADS = 8
C_Q_RANK = 384
C_KV_RANK = 256
C_NOPE = 128
C_ROPE = 64
C_DV = 128
ROPE_THETA = 10000.0
D_FF = 2816
CONV_W = 3
N_MOD = 6

L0_SIZES = (A_HEADS * A_DK, A_HEADS * A_DK, A_HEADS * A_DV, A_HEADS * A_DV, 2 * A_GATE_RANK,
            B_HEADS * B_DK, B_HEADS * B_DK, B_HEADS * B_DV, B_HEADS * B_DV, 4 * B_HEADS)
L0_IN = sum(L0_SIZES)
L0_MIX = A_HEADS * A_DV + B_HEADS * B_DV
L1_SIZES = (C_Q_RANK, C_KV_RANK, C_ROPE)
L1_IN = sum(L1_SIZES)

kernel_name = 'hybrid_gla_mlstm_mla_flow_step'


def split_cols(t, sizes):
    idx = np.cumsum(sizes)[:-1].tolist()
    return jnp.split(t, idx, axis=-1)


def rmsnorm(x, g):
    xf = x.astype(jnp.float32)
    y = xf * lax.rsqrt(jnp.mean(jnp.square(xf), axis=-1, keepdims=True) + EPS)
    return (y * g.astype(jnp.float32)).astype(x.dtype)


def modulation(cond, w_mod, b_mod):
    m = (jax.nn.silu(cond) @ w_mod + b_mod)[:, None, :]
    return jnp.split(m, N_MOD, axis=-1)


def to_chunks(t):
    b, n = t.shape[0], t.shape[1]
    t = t.reshape((b, n // CHUNK, CHUNK) + t.shape[2:])
    perm = (1, 0, 3, 2) + tuple(range(4, t.ndim))
    return t.transpose(perm)


def from_chunks(t):
    nc, b, h, l, d = t.shape
    return t.transpose(1, 0, 3, 2, 4).reshape(b, nc * l, h, d)


def gla_chunked(q, k, v, log_a, s0):
    qc, kc, vc, ac = (to_chunks(t) for t in (q, k, v, log_a))
    causal = jnp.tril(jnp.ones((CHUNK, CHUNK), dtype=bool))

    def step(s, inp):
        qi, ki, vi, ai = inp
        b = jnp.cumsum(ai, axis=2)
        inter = jnp.einsum('bhtk,bhkv->bhtv', qi * jnp.exp(b), s)
        rel = b[:, :, :, None, :] - b[:, :, None, :, :]
        decay = jnp.exp(jnp.where(causal[:, :, None], rel, -jnp.inf))
        scores = jnp.einsum('bhtk,bhsk,bhtsk->bhts', qi, ki, decay)
        out = inter + jnp.einsum('bhts,bhsv->bhtv', scores, vi)
        b_end = b[:, :, -1, :]
        s_new = (jnp.exp(b_end)[..., None] * s
                 + jnp.einsum('bhsk,bhsv->bhkv', ki * jnp.exp(b_end[:, :, None, :] - b), vi))
        return s_new, out

    s_fin, out = lax.scan(step, s0, (qc, kc, vc, ac))
    return from_chunks(out), s_fin


def mlstm_chunked(q, k, v, log_i, log_f, state):
    qc, kc, vc, ic, fc = (to_chunks(t) for t in (q, k, v, log_i, log_f))
    causal = jnp.tril(jnp.ones((CHUNK, CHUNK), dtype=bool))

    def step(carry, inp):
        c_mat, n_vec, m = carry
        qi, ki, vi, ii, fi = inp
        b = jnp.cumsum(fi, axis=-1)
        log_inter = b + m[..., None]
        log_intra = jnp.where(causal, b[..., :, None] - b[..., None, :] + ii[..., None, :], -jnp.inf)
        m_out = jnp.maximum(log_inter, jnp.max(log_intra, axis=-1))
        w_inter = jnp.exp(log_inter - m_out)
        w_intra = jnp.exp(log_intra - m_out[..., None])
        qk = jnp.einsum('bhtk,bhsk->bhts', qi, ki) * w_intra
        num = (w_inter[..., None] * jnp.einsum('bhtk,bhkv->bhtv', qi, c_mat)
               + jnp.einsum('bhts,bhsv->bhtv', qk, vi))
        den = w_inter * jnp.einsum('bhtk,bhk->bht', qi, n_vec) + jnp.sum(qk, axis=-1)
        h = num / jnp.maximum(jnp.abs(den), jnp.exp(-m_out))[..., None]
        b_end = b[..., -1]
        log_src = b_end[..., None] - b + ii
        m_new = jnp.maximum(b_end + m, jnp.max(log_src, axis=-1))
        carry_w = jnp.exp(b_end + m - m_new)
        src_w = jnp.exp(log_src - m_new[..., None])
        c_new = carry_w[..., None, None] * c_mat + jnp.einsum('bhs,bhsk,bhsv->bhkv', src_w, ki, vi)
        n_new = carry_w[..., None] * n_vec + jnp.einsum('bhs,bhsk->bhk', src_w, ki)
        return (c_new, n_new, m_new), h

    (c_f, n_f, m_f), h = lax.scan(step, state, (qc, kc, vc, ic, fc))
    return from_chunks(h), (c_f, n_f, m_f)


def recurrent_mixer(h, w_in, gla_w_gate_f, gla_b_gate_f, gla_w_gate_b, gla_b_gate_b, gla_g_norm,
                    mlstm_b_gates, mlstm_g_norm, w_out, init):
    bsz, n, _ = h.shape
    f32 = jnp.float32
    qa, ka, va, ga, lra, qb, kb, vb, ob, gates = split_cols(h @ w_in, L0_SIZES)

    def heads(t, d):
        return t.reshape(bsz, n, -1, d).astype(f32)

    def flip(t):
        return jnp.flip(t, axis=1)

    if init is None:
        za = jnp.zeros((bsz, A_HEADS, A_DK, A_DV), f32)
        zc = jnp.zeros((bsz, B_HEADS, B_DK, B_DV), f32)
        zn = jnp.zeros((bsz, B_HEADS, B_DK), f32)
        zm = jnp.zeros((bsz, B_HEADS), f32)
        init = (za, za, zc, zn, zm, zc, zn, zm)
    sa_f0, sa_b0, cf0, nf0, mf0, cb0, nb0, mb0 = (t.astype(f32) for t in init)

    qa = heads(qa, A_DK) * A_DK ** -0.5
    ka = heads(ka, A_DK)
    va = heads(va, A_DV)
    lr_f, lr_b = jnp.split(lra, 2, axis=-1)
    loga_f = jax.nn.log_sigmoid((lr_f @ gla_w_gate_f + gla_b_gate_f).astype(f32)).reshape(
        bsz, n, A_HEADS, A_DK) / A_GATE_TEMP
    loga_b = jax.nn.log_sigmoid((lr_b @ gla_w_gate_b + gla_b_gate_b).astype(f32)).reshape(
        bsz, n, A_HEADS, A_DK) / A_GATE_TEMP
    oa_f, sa_f = gla_chunked(qa, ka, va, loga_f, sa_f0)
    oa_b, sa_b = gla_chunked(flip(qa), flip(ka), flip(va), flip(loga_b), sa_b0)
    oa = rmsnorm(oa_f + flip(oa_b), gla_g_norm) * jax.nn.silu(heads(ga, A_DV))

    qb = heads(qb, B_DK)
    kb = heads(kb, B_DK) * B_DK ** -0.5
    vb = heads(vb, B_DV)
    g = (gates + mlstm_b_gates).astype(f32)
    i_f, i_b, f_f, f_b = jnp.split(g, 4, axis=-1)
    hb_f, st_f = mlstm_chunked(qb, kb, vb, i_f, jax.nn.log_sigmoid(f_f), (cf0, nf0, mf0))
    hb_b, st_b = mlstm_chunked(flip(qb), flip(kb), flip(vb), flip(i_b),
                               flip(jax.nn.log_sigmoid(f_b)), (cb0, nb0, mb0))
    om = jax.nn.sigmoid(heads(ob, B_DV)) * rmsnorm(hb_f + flip(hb_b), mlstm_g_norm)

    mixed = jnp.concatenate([oa.reshape(bsz, n, -1), om.reshape(bsz, n, -1)], axis=-1).astype(h.dtype)
    return mixed @ w_out, (sa_f, sa_b, st_f[0], st_f[1], st_f[2], st_b[0], st_b[1], st_b[2])


def axial_rope_tables(n):
    rows = n // GRID_W
    pos = jnp.arange(rows * GRID_W)
    row = (pos // GRID_W).astype(jnp.float32)
    col = (pos % GRID_W).astype(jnp.float32)
    half = C_ROPE // 2
    inv = 1.0 / (ROPE_THETA ** (jnp.arange(0, half, 2, dtype=jnp.float32) / half))
    ang_r = row[:, None] * inv[None, :]
    ang_c = col[:, None] * inv[None, :]
    return jnp.cos(ang_r), jnp.sin(ang_r), jnp.cos(ang_c), jnp.sin(ang_c)


def rotate_half(x, cos, sin):
    x1, x2 = jnp.split(x, 2, axis=-1)
    cos = cos[:, None, :]
    sin = sin[:, None, :]
    return jnp.concatenate([x1 * cos - x2 * sin, x1 * sin + x2 * cos], axis=-1)


def apply_axial_rope(x, tables):
    cr, sr, cc, sc = tables
    xr, xc = jnp.split(x.astype(jnp.float32), 2, axis=-1)
    return jnp.concatenate([rotate_half(xr, cr, sr), rotate_half(xc, cc, sc)], axis=-1).astype(x.dtype)


def blocked_attention(q, k, v, scale):
    bsz, n, h, dq = q.shape
    nb = n // Q_BLOCK
    qb = q.reshape(bsz, nb, Q_BLOCK, h, dq).transpose(1, 0, 2, 3, 4)

    def one_block(qi):
        s = jnp.einsum('bqhd,bkhd->bhqk', qi, k).astype(jnp.float32) * scale
        p = jax.nn.softmax(s, axis=-1).astype(v.dtype)
        return jnp.einsum('bhqk,bkhd->bqhd', p, v)

    out = lax.map(one_block, qb)
    return out.transpose(1, 0, 2, 3, 4).reshape(bsz, n, h, v.shape[-1])


def mla_project(h, w_in, g_q_norm, w_qb, g_kv_norm):
    bsz, n, _ = h.shape
    q_a, kv_a, k_r = split_cols(h @ w_in, L1_SIZES)
    q = (rmsnorm(q_a, g_q_norm) @ w_qb).reshape(bsz, n, C_HEADS, C_NOPE + C_ROPE)
    q_nope, q_rope = jnp.split(q, [C_NOPE], axis=-1)
    return q_nope, q_rope, rmsnorm(kv_a, g_kv_norm), k_r


def mla_attend(q_nope, q_rope, ckv, k_rope, w_kvb, w_out):
    bsz, n = q_nope.shape[0], q_nope.shape[1]
    nk = ckv.shape[1]
    kv = (ckv @ w_kvb).reshape(bsz, nk, C_HEADS, C_NOPE + C_DV)
    k_nope, v = jnp.split(kv, [C_NOPE], axis=-1)
    k = jnp.concatenate([k_nope, jnp.broadcast_to(k_rope[:, :, None, :], (bsz, nk, C_HEADS, C_ROPE))], axis=-1)
    q = jnp.concatenate([q_nope, q_rope], axis=-1)
    o = blocked_attention(q, k, v, (C_NOPE + C_ROPE) ** -0.5)
    return o.reshape(bsz, n, C_HEADS * C_DV) @ w_out


def mla_context(h, w_in, g_q_norm, w_qb, g_kv_norm, w_kvb, w_out):
    q_nope, q_rope, ckv, k_r = mla_project(h, w_in, g_q_norm, w_qb, g_kv_norm)
    return mla_attend(q_nope, q_rope, ckv, k_r, w_kvb, w_out), (ckv, k_r)


def mla_latent(h, w_in, g_q_norm, w_qb, g_kv_norm, w_kvb, w_out, ctx_ckv, ctx_krope):
    q_nope, q_rope, ckv, k_r = mla_project(h, w_in, g_q_norm, w_qb, g_kv_norm)
    tables = axial_rope_tables(h.shape[1])
    q_rope = apply_axial_rope(q_rope, tables)
    k_r = apply_axial_rope(k_r[:, :, None, :], tables)[:, :, 0, :]
    ckv_all = jnp.concatenate([ctx_ckv.astype(ckv.dtype), ckv], axis=1)
    kr_all = jnp.concatenate([ctx_krope.astype(k_r.dtype), k_r], axis=1)
    return mla_attend(q_nope, q_rope, ckv_all, kr_all, w_kvb, w_out)


def conv_ffn(h, w_up, conv_w, conv_b, w_down):
    u = h @ w_up
    u = lax.conv_general_dilated(
        u, conv_w[:, None, :].astype(u.dtype), window_strides=(1,),
        padding=[(CONV_W // 2, CONV_W // 2)], dimension_numbers=('NWC', 'WIO', 'NWC'),
        feature_group_count=u.shape[-1]) + conv_b
    a, g = jnp.split(u, 2, axis=-1)
    return (jax.nn.silu(g) * a) @ w_down


def setup_inputs(seed: int = 0) -> dict:
    key = jax.random.key(seed)
    ks = iter(jax.random.split(key, 96))
    D = D_MODEL

    def nrm(shape, scale=1.0):
        return scale * jax.random.normal(next(ks), shape, jnp.float32)

    def gain(n):
        return 1.0 + nrm((n,), 0.02)

    inp = {}
    inp['x_prompt'] = nrm((BATCH, SEQ, D))
    inp['x_sample'] = nrm((DEC_BATCH, DEC_SEQ, D))
    inp['state_l0_gla_fwd'] = nrm((DEC_BATCH, A_HEADS, A_DK, A_DV), 0.5)
    inp['state_l0_gla_bwd'] = nrm((DEC_BATCH, A_HEADS, A_DK, A_DV), 0.5)
    inp['state_l0_mlstm_c_fwd'] = nrm((DEC_BATCH, B_HEADS, B_DK, B_DV), 0.5)
    inp['state_l0_mlstm_n_fwd'] = nrm((DEC_BATCH, B_HEADS, B_DK), 0.5)
    inp['state_l0_mlstm_m_fwd'] = nrm((DEC_BATCH, B_HEADS), 0.5)
    inp['state_l0_mlstm_c_bwd'] = nrm((DEC_BATCH, B_HEADS, B_DK, B_DV), 0.5)
    inp['state_l0_mlstm_n_bwd'] = nrm((DEC_BATCH, B_HEADS, B_DK), 0.5)
    inp['state_l0_mlstm_m_bwd'] = nrm((DEC_BATCH, B_HEADS), 0.5)
    inp['cache_l1_ckv'] = nrm((DEC_BATCH, PAST_LEN, C_KV_RANK))
    inp['cache_l1_krope'] = nrm((DEC_BATCH, PAST_LEN, C_ROPE))
    inp['c'] = nrm((DEC_BATCH, D))
    inp['c_ctx'] = nrm((D,))
    inp['l0_w_mod'] = nrm((D, N_MOD * D), 0.5 * D ** -0.5)
    inp['l0_b_mod'] = nrm((N_MOD * D,), 0.02)
    inp['l0_g_pre_mix'] = gain(D)
    inp['l0_g_post_mix'] = gain(D)
    inp['l0_g_pre_ffn'] = gain(D)
    inp['l0_g_post_ffn'] = gain(D)
    inp['l0_w_in'] = nrm((D, L0_IN), D ** -0.5)
    inp['l0_gla_w_gate_f'] = nrm((A_GATE_RANK, A_HEADS * A_DK), A_GATE_RANK ** -0.5)
    inp['l0_gla_b_gate_f'] = nrm((A_HEADS * A_DK,), 0.02)
    inp['l0_gla_w_gate_b'] = nrm((A_GATE_RANK, A_HEADS * A_DK), A_GATE_RANK ** -0.5)
    inp['l0_gla_b_gate_b'] = nrm((A_HEADS * A_DK,), 0.02)
    inp['l0_gla_g_norm'] = gain(A_DV)
    inp['l0_mlstm_b_gates'] = jnp.concatenate([nrm((2 * B_HEADS,), 0.1), 3.0 + nrm((2 * B_HEADS,), 0.1)])
    inp['l0_mlstm_g_norm'] = gain(B_DV)
    inp['l0_w_out'] = nrm((L0_MIX, D), L0_MIX ** -0.5)
    inp['l0_ffn_w_up'] = nrm((D, 2 * D_FF), D ** -0.5)
    inp['l0_ffn_conv_w'] = nrm((CONV_W, 2 * D_FF), CONV_W ** -0.5)
    inp['l0_ffn_conv_b'] = nrm((2 * D_FF,), 0.02)
    inp['l0_ffn_w_down'] = nrm((D_FF, D), D_FF ** -0.5)
    inp['l1_w_mod'] = nrm((D, N_MOD * D), 0.5 * D ** -0.5)
    inp['l1_b_mod'] = nrm((N_MOD * D,), 0.02)
    inp['l1_g_pre_mix'] = gain(D)
    inp['l1_g_post_mix'] = gain(D)
    inp['l1_g_pre_ffn'] = gain(D)
    inp['l1_g_post_ffn'] = gain(D)
    inp['l1_w_in'] = nrm((D, L1_IN), D ** -0.5)
    inp['l1_g_q_norm'] = gain(C_Q_RANK)
    inp['l1_w_qb'] = nrm((C_Q_RANK, C_HEADS * (C_NOPE + C_ROPE)), C_Q_RANK ** -0.5)
    inp['l1_g_kv_norm'] = gain(C_KV_RANK)
    inp['l1_w_kvb'] = nrm((C_KV_RANK, C_HEADS * (C_NOPE + C_DV)), C_KV_RANK ** -0.5)
    inp['l1_w_out'] = nrm((C_HEADS * C_DV, D), (C_HEADS * C_DV) ** -0.5)
    inp['l1_ffn_w_up'] = nrm((D, 2 * D_FF), D ** -0.5)
    inp['l1_ffn_conv_w'] = nrm((CONV_W, 2 * D_FF), CONV_W ** -0.5)
    inp['l1_ffn_conv_b'] = nrm((2 * D_FF,), 0.02)
    inp['l1_ffn_w_down'] = nrm((D_FF, D), D_FF ** -0.5)
    return inp


def reference(x_prompt, x_sample, state_l0_gla_fwd, state_l0_gla_bwd, state_l0_mlstm_c_fwd,
              state_l0_mlstm_n_fwd, state_l0_mlstm_m_fwd, state_l0_mlstm_c_bwd, state_l0_mlstm_n_bwd,
              state_l0_mlstm_m_bwd, cache_l1_ckv, cache_l1_krope, c, c_ctx,
              l0_w_mod, l0_b_mod, l0_g_pre_mix, l0_g_post_mix, l0_g_pre_ffn, l0_g_post_ffn, l0_w_in,
              l0_gla_w_gate_f, l0_gla_b_gate_f, l0_gla_w_gate_b, l0_gla_b_gate_b, l0_gla_g_norm,
              l0_mlstm_b_gates, l0_mlstm_g_norm, l0_w_out, l0_ffn_w_up, l0_ffn_conv_w, l0_ffn_conv_b,
              l0_ffn_w_down,
              l1_w_mod, l1_b_mod, l1_g_pre_mix, l1_g_post_mix, l1_g_pre_ffn, l1_g_post_ffn, l1_w_in,
              l1_g_q_norm, l1_w_qb, l1_g_kv_norm, l1_w_kvb, l1_w_out, l1_ffn_w_up, l1_ffn_conv_w,
              l1_ffn_conv_b, l1_ffn_w_down):
    ctx_cond = c_ctx[None, :]
    shared = (
        (l0_w_mod, l0_b_mod, l0_g_pre_mix, l0_g_post_mix, l0_g_pre_ffn, l0_g_post_ffn,
         l0_ffn_w_up, l0_ffn_conv_w, l0_ffn_conv_b, l0_ffn_w_down),
        (l1_w_mod, l1_b_mod, l1_g_pre_mix, l1_g_post_mix, l1_g_pre_ffn, l1_g_post_ffn,
         l1_ffn_w_up, l1_ffn_conv_w, l1_ffn_conv_b, l1_ffn_w_down),
    )
    mixers = (
        (l0_w_in, l0_gla_w_gate_f, l0_gla_b_gate_f, l0_gla_w_gate_b, l0_gla_b_gate_b, l0_gla_g_norm,
         l0_mlstm_b_gates, l0_mlstm_g_norm, l0_w_out),
        (l1_w_in, l1_g_q_norm, l1_w_qb, l1_g_kv_norm, l1_w_kvb, l1_w_out),
    )
    caches = (
        (state_l0_gla_fwd, state_l0_gla_bwd, state_l0_mlstm_c_fwd, state_l0_mlstm_n_fwd,
         state_l0_mlstm_m_fwd, state_l0_mlstm_c_bwd, state_l0_mlstm_n_bwd, state_l0_mlstm_m_bwd),
        (cache_l1_ckv, cache_l1_krope),
    )
    xp, xs = x_prompt, x_sample
    ctx_states = []
    for li in range(DEPTH):
        (w_mod, b_mod, g_pre_mix, g_post_mix, g_pre_ffn, g_post_ffn,
         w_up, conv_w, conv_b, w_down) = shared[li]
        sh1p, sc1p, gt1p, sh2p, sc2p, gt2p = modulation(ctx_cond, w_mod, b_mod)
        sh1s, sc1s, gt1s, sh2s, sc2s, gt2s = modulation(c, w_mod, b_mod)
        hp = rmsnorm(xp, g_pre_mix) * (1.0 + sc1p) + sh1p
        hs = rmsnorm(xs, g_pre_mix) * (1.0 + sc1s) + sh1s
        if li % 2 == 0:
            mp, st = recurrent_mixer(hp, *mixers[li], None)
            ms, _ = recurrent_mixer(hs, *mixers[li], caches[li])
        else:
            mp, st = mla_context(hp, *mixers[li])
            ms = mla_latent(hs, *mixers[li], *caches[li])
        ctx_states.append(st)
        xp = xp + gt1p * rmsnorm(mp, g_post_mix)
        xs = xs + gt1s * rmsnorm(ms, g_post_mix)
        hp = rmsnorm(xp, g_pre_ffn) * (1.0 + sc2p) + sh2p
        hs = rmsnorm(xs, g_pre_ffn) * (1.0 + sc2s) + sh2s
        xp = xp + gt2p * rmsnorm(conv_ffn(hp, w_up, conv_w, conv_b, w_down), g_post_ffn)
        xs = xs + gt2s * rmsnorm(conv_ffn(hs, w_up, conv_w, conv_b, w_down), g_post_ffn)
    (gla_f, gla_b, mc_f, mn_f, mm_f, mc_b, mn_b, mm_b) = ctx_states[0]
    (ckv, krope) = ctx_states[1]
    return (xp, xs, gla_f, gla_b, mc_f, mn_f, mm_f, mc_b, mn_b, mm_b, ckv, krope)
```

```python
import functools

import numpy as np
import jax
import jax.numpy as jnp
from jax import lax
from jax.experimental import pallas as pl
from jax.experimental.pallas import tpu as pltpu

F32 = jnp.float32
BF16 = jnp.bfloat16

D_MODEL = 1024
EPS = 1e-6
N_MOD = 6
GRID_W = 64
ROPE_THETA = 10000.0

A_HEADS, A_DK, A_DV, A_GATE_RANK, A_GATE_TEMP = 4, 64, 128, 16, 16.0
B_HEADS, B_DK, B_DV = 4, 64, 128
C_HEADS, C_Q_RANK, C_KV_RANK, C_NOPE, C_ROPE, C_DV = 8, 384, 256, 128, 64, 128
D_FF = 2816
CONV_W = 3

HEADS = 4
HK = HEADS * 64
HV = HEADS * 128
ZA_W = 1536
ZG_W = 128
GATE_COL = 32
RCHUNK = 128
L1_IN_PAD = 768
QCAT = 256

VMEM_LIMIT = 56 * 1024 * 1024


def _cparams(sem):
    return pltpu.CompilerParams(dimension_semantics=sem, vmem_limit_bytes=VMEM_LIMIT)


def _dot(a, b):
    return jnp.dot(a, b, preferred_element_type=F32)


def _dot_nt(a, b):
    return lax.dot_general(a, b, (((1,), (1,)), ((), ())), preferred_element_type=F32)


def _dot_tn(a, b):
    return lax.dot_general(a, b, (((0,), (0,)), ((), ())), preferred_element_type=F32)


def _split3(x):
    hi = x.astype(BF16)
    r1 = x - hi.astype(F32)
    mid = r1.astype(BF16)
    lo = (r1 - mid.astype(F32)).astype(BF16)
    return hi, mid, lo


def _tri_dot(tri, x):
    hi, mid, lo = _split3(x)
    return _dot(tri, hi) + _dot(tri, mid) + _dot(tri, lo)


def _dot_tri(x, tri):
    hi, mid, lo = _split3(x)
    return _dot(hi, tri) + _dot(mid, tri) + _dot(lo, tri)


def _rms(x, g):
    return x * lax.rsqrt(jnp.mean(x * x, axis=-1, keepdims=True) + EPS) * g


def _sigmoid(x):
    return 1.0 / (1.0 + jnp.exp(-x))


def _silu(x):
    return x * _sigmoid(x)


def _log_sigmoid(x):
    return jnp.minimum(x, 0.0) - jnp.log(1.0 + jnp.exp(-jnp.abs(x)))


def _tri_masks(n):
    r = lax.broadcasted_iota(jnp.int32, (n, n), 0)
    c = lax.broadcasted_iota(jnp.int32, (n, n), 1)
    return c <= r, c >= r


def _mod_kernel(cond_ref, w_ref, b_ref, o_ref):
    s = _silu(cond_ref[...])
    o_ref[...] = _dot(s.astype(BF16), w_ref[...].astype(BF16)) + b_ref[...]


def _modulation(cond8, w_mod, b_mod):
    n = w_mod.shape[1]
    tn = 1536
    return pl.pallas_call(
        _mod_kernel,
        out_shape=jax.ShapeDtypeStruct((8, n), F32),
        grid=(n // tn,),
        in_specs=[pl.BlockSpec((8, D_MODEL), lambda j: (0, 0)),
                  pl.BlockSpec((D_MODEL, tn), lambda j: (0, j)),
                  pl.BlockSpec((1, tn), lambda j: (0, j))],
        out_specs=pl.BlockSpec((8, tn), lambda j: (0, j)),
        compiler_params=_cparams(("arbitrary",)),
        name="modulation",
    )(cond8, w_mod, b_mod.reshape(1, n))


def _p0_kernel(x_ref, g_ref, sc_ref, sh_ref, w_ref, za_ref, zb_ref, zg_ref):
    h = _rms(x_ref[...], g_ref[...]) * (1.0 + sc_ref[0]) + sh_ref[0]
    hb = h.astype(BF16)
    za_ref[...] = _dot(hb, w_ref[:, 0:ZA_W])
    zb_ref[...] = _dot(hb, w_ref[:, ZA_W:2 * ZA_W])
    zg_ref[...] = _dot(hb, w_ref[:, 2 * ZA_W:2 * ZA_W + ZG_W])


def _proj_l0(x, g, sc, sh, w, tm):
    t = x.shape[0]
    groups = sc.shape[0]
    per = (t // tm) // groups
    nw = w.shape[1]
    row = lambda i: (i, 0)
    grp = lambda i: (i // per, 0, 0)
    fixed = lambda i: (0, 0)
    return pl.pallas_call(
        _p0_kernel,
        out_shape=(jax.ShapeDtypeStruct((t, ZA_W), F32),
                   jax.ShapeDtypeStruct((t, ZA_W), F32),
                   jax.ShapeDtypeStruct((t, ZG_W), F32)),
        grid=(t // tm,),
        in_specs=[pl.BlockSpec((tm, D_MODEL), row),
                  pl.BlockSpec((1, D_MODEL), fixed),
                  pl.BlockSpec((1, 1, D_MODEL), grp),
                  pl.BlockSpec((1, 1, D_MODEL), grp),
                  pl.BlockSpec((D_MODEL, nw), fixed)],
        out_specs=(pl.BlockSpec((tm, ZA_W), row),
                   pl.BlockSpec((tm, ZA_W), row),
                   pl.BlockSpec((tm, ZG_W), row)),
        compiler_params=_cparams(("arbitrary",)),
        name="proj_l0",
    )(x, g, sc, sh, w)


def _head_lane_mask(shape, h):
    lane = lax.broadcasted_iota(jnp.int32, shape, len(shape) - 1)
    return (lane >= h * 64) & (lane < (h + 1) * 64)


def _gla_chunk(za_ref, la_ref, st_ref, o_ref, r0, rev, tri_lo, tri_up, m_lo, m_up):
    L = RCHUNK
    rows = pl.ds(r0, L)
    la = la_ref[rows, :]
    b = _tri_dot(tri_up if rev else tri_lo, la)
    mid = L // 2
    ref = b[mid:mid + 1, :]
    b_end = b[0:1, :] if rev else b[L - 1:L, :]
    q = za_ref[rows, 0:HK]
    k = za_ref[rows, HK:2 * HK]
    v = za_ref[rows, 2 * HK:2 * HK + HV].astype(BF16)
    qf = q * (jnp.exp(b - ref) * (A_DK ** -0.5))
    ke = (k * jnp.exp(ref - b)).astype(BF16)
    st = st_ref[...]
    st_in = (st * jnp.exp(ref)).astype(BF16)
    mask = m_up if rev else m_lo
    for h in range(HEADS):
        qm = jnp.where(_head_lane_mask(qf.shape, h), qf, 0.0).astype(BF16)
        sc = jnp.where(mask, _dot_nt(qm, ke), 0.0).astype(BF16)
        o_ref[rows, h * 128:(h + 1) * 128] = (
            _dot_nt(qm, st_in) + _dot(sc, v[:, h * 128:(h + 1) * 128]))
    rt = _dot_tn(v, ke)
    upd = jnp.zeros_like(st)
    for h in range(HEADS):
        blk = rt[h * 128:(h + 1) * 128, :]
        upd = upd + jnp.where(_head_lane_mask(blk.shape, h), blk, 0.0)
    st_ref[...] = st * jnp.exp(b_end) + upd * jnp.exp(b_end - ref)


def _gla_kernel(*refs, n, has_init):
    if has_init:
        (za_ref, zg_ref, wgf_ref, wgb_ref, bgf_ref, bgb_ref, gn_ref, sf0_ref, sb0_ref,
         o_ref, sf_ref, sb_ref, laf_ref, lab_ref, of_ref, ob_ref, stf_ref, stb_ref) = refs
    else:
        (za_ref, zg_ref, wgf_ref, wgb_ref, bgf_ref, bgb_ref, gn_ref,
         o_ref, sf_ref, sb_ref, laf_ref, lab_ref, of_ref, ob_ref, stf_ref, stb_ref) = refs
    L = RCHUNK
    nc = n // L
    zg = zg_ref[...]
    hi = lax.Precision.HIGHEST
    xf = jnp.dot(zg, wgf_ref[...], precision=hi, preferred_element_type=F32) + bgf_ref[...]
    xb = jnp.dot(zg, wgb_ref[...], precision=hi, preferred_element_type=F32) + bgb_ref[...]
    laf_ref[...] = _log_sigmoid(xf) / A_GATE_TEMP
    lab_ref[...] = _log_sigmoid(xb) / A_GATE_TEMP
    if has_init:
        stf_ref[...] = sf0_ref[...]
        stb_ref[...] = sb0_ref[...]
    else:
        stf_ref[...] = jnp.zeros_like(stf_ref)
        stb_ref[...] = jnp.zeros_like(stb_ref)

    m_lo, m_up = _tri_masks(L)
    tri_lo = jnp.where(m_lo, 1.0, 0.0).astype(BF16)
    tri_up = jnp.where(m_up, 1.0, 0.0).astype(BF16)

    def scan_body(c, carry):
        rf = pl.multiple_of(c * L, L)
        rb = pl.multiple_of((nc - 1 - c) * L, L)
        _gla_chunk(za_ref, laf_ref, stf_ref, of_ref, rf, False, tri_lo, tri_up, m_lo, m_up)
        _gla_chunk(za_ref, lab_ref, stb_ref, ob_ref, rb, True, tri_lo, tri_up, m_lo, m_up)
        return carry

    lax.fori_loop(0, nc, scan_body, 0)

    gn = gn_ref[...]

    def out_body(c, carry):
        rows = pl.ds(pl.multiple_of(c * L, L), L)
        for h in range(HEADS):
            cols = slice(h * 128, (h + 1) * 128)
            o = of_ref[rows, cols] + ob_ref[rows, cols]
            gate = za_ref[rows, 2 * HK + HV + h * 128:2 * HK + HV + (h + 1) * 128]
            o_ref[rows, cols] = (_rms(o, gn) * _silu(gate)).astype(o_ref.dtype)
        return carry

    lax.fori_loop(0, nc, out_body, 0)

    stf_t = stf_ref[...].T
    stb_t = stb_ref[...].T
    for h in range(HEADS):
        sf_ref[h] = stf_t[h * 64:(h + 1) * 64, :]
        sb_ref[h] = stb_t[h * 64:(h + 1) * 64, :]


def _gla(za, zg, wgf, wgb, bgf, bgb, gn, init):
    bsz, n, _ = za.shape
    has_init = init is not None
    per_b3 = lambda b: (b, 0, 0)
    per_b4 = lambda b: (b, 0, 0, 0)
    fixed = lambda b: (0, 0)
    in_specs = [pl.BlockSpec((None, n, ZA_W), per_b3),
                pl.BlockSpec((None, n, ZG_W), per_b3),
                pl.BlockSpec((ZG_W, HK), fixed),
                pl.BlockSpec((ZG_W, HK), fixed),
                pl.BlockSpec((1, HK), fixed),
                pl.BlockSpec((1, HK), fixed),
                pl.BlockSpec((1, 128), fixed)]
    args = [za, zg, wgf, wgb, bgf, bgb, gn]
    if has_init:
        in_specs += [pl.BlockSpec((None, 128, HK), per_b3)] * 2
        args += list(init)
    return pl.pallas_call(
        functools.partial(_gla_kernel, n=n, has_init=has_init),
        out_shape=(jax.ShapeDtypeStruct((bsz, n, HV), BF16),
                   jax.ShapeDtypeStruct((bsz, HEADS, 64, 128), F32),
                   jax.ShapeDtypeStruct((bsz, HEADS, 64, 128), F32)),
        grid=(bsz,),
        in_specs=in_specs,
        out_specs=(pl.BlockSpec((None, n, HV), per_b3),
                   pl.BlockSpec((None, HEADS, 64, 128), per_b4),
                   pl.BlockSpec((None, HEADS, 64, 128), per_b4)),
        scratch_shapes=[pltpu.VMEM((n, HK), F32), pltpu.VMEM((n, HK), F32),
                        pltpu.VMEM((n, HV), F32), pltpu.VMEM((n, HV), F32),
                        pltpu.VMEM((128, HK), F32), pltpu.VMEM((128, HK), F32)],
        compiler_params=_cparams(("arbitrary",)),
        name="gla",
    )(*args)


def _mlstm_chunk(zb_ref, gc_ref, gt_ref, ct_ref, nm_ref, h_ref, ci_chunk, d,
                 tri_lo, tri_up, m_lo, m_up):
    L = RCHUNK
    rev = d == 1
    rows = pl.ds(pl.multiple_of(ci_chunk * L, L), L)
    gcc = gc_ref[rows, :]
    gtc = gt_ref[ci_chunk, GATE_COL:GATE_COL + 16, :]
    b_cols = _tri_dot(tri_up if rev else tri_lo, gcc)
    b_rows = _dot_tri(gtc, tri_lo if rev else tri_up)
    mask = m_up if rev else m_lo
    q = zb_ref[rows, 0:HK]
    k = zb_ref[rows, HK:2 * HK] * (B_DK ** -0.5)
    kb = k.astype(BF16)
    v = zb_ref[rows, 2 * HK:2 * HK + HV].astype(BF16)
    ct = ct_ref[...]
    ctb = ct.astype(BF16)
    n_row = nm_ref[0:1, :]
    k_scale = jnp.zeros((L, HK), F32)
    carry_row = jnp.zeros((1, HK), F32)
    for h in range(HEADS):
        ci = GATE_COL + 4 * d + h
        cf = GATE_COL + 8 + 4 * d + h
        i_col = gcc[:, ci:ci + 1]
        b_col = b_cols[:, cf:cf + 1]
        i_row = gtc[4 * d + h:4 * d + h + 1, :]
        b_row = b_rows[8 + 4 * d + h:8 + 4 * d + h + 1, :]
        b_end = b_col[0:1, :] if rev else b_col[L - 1:L, :]
        m_prev = nm_ref[1 + h:2 + h, 0:1]
        log_intra = jnp.where(mask, b_col + (i_row - b_row), -jnp.inf)
        log_inter = b_col + m_prev
        m_out = jnp.maximum(log_inter, jnp.max(log_intra, axis=1, keepdims=True))
        w_inter = jnp.exp(log_inter - m_out)
        w_intra = jnp.exp(log_intra - m_out)
        hmask = _head_lane_mask(q.shape, h)
        qm = jnp.where(hmask, q, 0.0)
        qmb = qm.astype(BF16)
        qk = _dot_nt(qmb, kb) * w_intra
        num = w_inter * _dot_nt(qmb, ctb) + _dot(qk.astype(BF16), v[:, h * 128:(h + 1) * 128])
        den = (w_inter * jnp.sum(qm * n_row, axis=1, keepdims=True)
               + jnp.sum(qk, axis=1, keepdims=True))
        h_ref[rows, h * 128:(h + 1) * 128] = num / jnp.maximum(jnp.abs(den), jnp.exp(-m_out))
        log_src = b_end - b_col + i_col
        m_new = jnp.maximum(b_end + m_prev, jnp.max(log_src, axis=0, keepdims=True))
        carry_w = jnp.exp(b_end + m_prev - m_new)
        src_w = jnp.exp(log_src - m_new)
        k_scale = k_scale + jnp.where(hmask, src_w, 0.0)
        carry_row = carry_row + jnp.where(_head_lane_mask((1, HK), h), carry_w, 0.0)
        nm_ref[1 + h:2 + h, :] = jnp.broadcast_to(m_new, (1, HK))
    ks = k * k_scale
    rt = _dot_tn(v, ks.astype(BF16))
    upd = jnp.zeros_like(ct)
    for h in range(HEADS):
        blk = rt[h * 128:(h + 1) * 128, :]
        upd = upd + jnp.where(_head_lane_mask(blk.shape, h), blk, 0.0)
    ct_ref[...] = carry_row * ct + upd
    nm_ref[0:1, :] = carry_row * n_row + jnp.sum(ks, axis=0, keepdims=True)


def _mlstm_kernel(*refs, n, has_init):
    if has_init:
        (zb_ref, zg_ref, bg_ref, gn_ref, cf0_ref, cb0_ref, nmf0_ref, nmb0_ref,
         o_ref, cf_ref, cb_ref, nmf_out_ref, nmb_out_ref,
         gc_ref, gt_ref, hf_ref, hb_ref, ctf_ref, ctb_ref, nmf_ref, nmb_ref) = refs
    else:
        (zb_ref, zg_ref, bg_ref, gn_ref,
         o_ref, cf_ref, cb_ref, nmf_out_ref, nmb_out_ref,
         gc_ref, gt_ref, hf_ref, hb_ref, ctf_ref, ctb_ref, nmf_ref, nmb_ref) = refs
    L = RCHUNK
    nc = n // L
    g = zg_ref[...] + bg_ref[...]
    lane = lax.broadcasted_iota(jnp.int32, g.shape, 1)
    is_forget = (lane >= GATE_COL + 8) & (lane < GATE_COL + 16)
    g = jnp.where(is_forget, _log_sigmoid(g), g)
    gc_ref[...] = g
    for ci in range(nc):
        gt_ref[ci] = g[ci * L:(ci + 1) * L, :].T
    if has_init:
        ctf_ref[...] = cf0_ref[...]
        ctb_ref[...] = cb0_ref[...]
        nmf_ref[...] = nmf0_ref[...]
        nmb_ref[...] = nmb0_ref[...]
    else:
        ctf_ref[...] = jnp.zeros_like(ctf_ref)
        ctb_ref[...] = jnp.zeros_like(ctb_ref)
        nmf_ref[...] = jnp.zeros_like(nmf_ref)
        nmb_ref[...] = jnp.zeros_like(nmb_ref)

    m_lo, m_up = _tri_masks(L)
    tri_lo = jnp.where(m_lo, 1.0, 0.0).astype(BF16)
    tri_up = jnp.where(m_up, 1.0, 0.0).astype(BF16)

    def scan_body(c, carry):
        _mlstm_chunk(zb_ref, gc_ref, gt_ref, ctf_ref, nmf_ref, hf_ref, c, 0,
                     tri_lo, tri_up, m_lo, m_up)
        _mlstm_chunk(zb_ref, gc_ref, gt_ref, ctb_ref, nmb_ref, hb_ref, nc - 1 - c, 1,
                     tri_lo, tri_up, m_lo, m_up)
        return carry

    lax.fori_loop(0, nc, scan_body, 0)

    gn = gn_ref[...]

    def out_body(c, carry):
        rows = pl.ds(pl.multiple_of(c * L, L), L)
        for h in range(HEADS):
            cols = slice(h * 128, (h + 1) * 128)
            hh = hf_ref[rows, cols] + hb_ref[rows, cols]
            gate = zb_ref[rows, 2 * HK + HV + h * 128:2 * HK + HV + (h + 1) * 128]
            o_ref[rows, cols] = (_sigmoid(gate) * _rms(hh, gn)).astype(o_ref.dtype)
        return carry

    lax.fori_loop(0, nc, out_body, 0)

    ctf_t = ctf_ref[...].T
    ctb_t = ctb_ref[...].T
    for h in range(HEADS):
        cf_ref[h] = ctf_t[h * 64:(h + 1) * 64, :]
        cb_ref[h] = ctb_t[h * 64:(h + 1) * 64, :]
    nmf_out_ref[...] = nmf_ref[...]
    nmb_out_ref[...] = nmb_ref[...]


def _mlstm(zb, zg, bg, gn, init):
    bsz, n, _ = zb.shape
    has_init = init is not None
    per_b3 = lambda b: (b, 0, 0)
    per_b4 = lambda b: (b, 0, 0, 0)
    fixed = lambda b: (0, 0)
    in_specs = [pl.BlockSpec((None, n, ZA_W), per_b3),
                pl.BlockSpec((None, n, ZG_W), per_b3),
                pl.BlockSpec((1, ZG_W), fixed),
                pl.BlockSpec((1, 128), fixed)]
    args = [zb, zg, bg, gn]
    if has_init:
        in_specs += [pl.BlockSpec((None, 128, HK), per_b3)] * 2
        in_specs += [pl.BlockSpec((None, 8, HK), per_b3)] * 2
        args += list(init)
    return pl.pallas_call(
        functools.partial(_mlstm_kernel, n=n, has_init=has_init),
        out_shape=(jax.ShapeDtypeStruct((bsz, n, HV), BF16),
                   jax.ShapeDtypeStruct((bsz, HEADS, 64, 128), F32),
                   jax.ShapeDtypeStruct((bsz, HEADS, 64, 128), F32),
                   jax.ShapeDtypeStruct((bsz, 8, HK), F32),
                   jax.ShapeDtypeStruct((bsz, 8, HK), F32)),
        grid=(bsz,),
        in_specs=in_specs,
        out_specs=(pl.BlockSpec((None, n, HV), per_b3),
                   pl.BlockSpec((None, HEADS, 64, 128), per_b4),
                   pl.BlockSpec((None, HEADS, 64, 128), per_b4),
                   pl.BlockSpec((None, 8, HK), per_b3),
                   pl.BlockSpec((None, 8, HK), per_b3)),
        scratch_shapes=[pltpu.VMEM((n, ZG_W), F32), pltpu.VMEM((n // RCHUNK, ZG_W, RCHUNK), F32),
                        pltpu.VMEM((n, HV), F32), pltpu.VMEM((n, HV), F32),
                        pltpu.VMEM((128, HK), F32), pltpu.VMEM((128, HK), F32),
                        pltpu.VMEM((8, HK), F32), pltpu.VMEM((8, HK), F32)],
        compiler_params=_cparams(("arbitrary",)),
        name="mlstm",
    )(*args)


def _out_kernel(*refs, n_parts):
    parts = refs[:n_parts]
    (w_ref, x_ref, gt_ref, gpost_ref, gpre_ref, sc_ref, sh_ref, xo_ref, ho_ref) = refs[n_parts:]
    m = None
    off = 0
    for p in parts:
        kp = p.shape[1]
        t = _dot(p[...], w_ref[off:off + kp, :])
        m = t if m is None else m + t
        off += kp
    x1 = x_ref[...] + gt_ref[0] * _rms(m, gpost_ref[...])
    xo_ref[...] = x1
    ho_ref[...] = (_rms(x1, gpre_ref[...]) * (1.0 + sc_ref[0]) + sh_ref[0]).astype(ho_ref.dtype)


def _mix_out(parts, w, x, gt, gpost, gpre, sc, sh, tm):
    t = x.shape[0]
    groups = gt.shape[0]
    per = (t // tm) // groups
    row = lambda i: (i, 0)
    grp = lambda i: (i // per, 0, 0)
    fixed = lambda i: (0, 0)
    vec = pl.BlockSpec((1, D_MODEL), fixed)
    mod = pl.BlockSpec((1, 1, D_MODEL), grp)
    in_specs = [pl.BlockSpec((tm, p.shape[1]), row) for p in parts]
    in_specs += [pl.BlockSpec(w.shape, fixed), pl.BlockSpec((tm, D_MODEL), row),
                 mod, vec, vec, mod, mod]
    return pl.pallas_call(
        functools.partial(_out_kernel, n_parts=len(parts)),
        out_shape=(jax.ShapeDtypeStruct((t, D_MODEL), F32),
                   jax.ShapeDtypeStruct((t, D_MODEL), BF16)),
        grid=(t // tm,),
        in_specs=in_specs,
        out_specs=(pl.BlockSpec((tm, D_MODEL), row), pl.BlockSpec((tm, D_MODEL), row)),
        compiler_params=_cparams(("arbitrary",)),
        name="mix_out",
    )(*parts, w, x, gt, gpost, gpre, sc, sh)


def _ffn_kernel(h_ref, wa_ref, wg_ref, cwa_ref, cwg_ref, cba_ref, cbg_ref, wd_ref, x_ref,
                gt_ref, gpost_ref, o_ref, acc_ref, *, seq):
    j = pl.program_id(1)
    tm = h_ref.shape[0]

    @pl.when(j == 0)
    def _():
        acc_ref[...] = jnp.zeros_like(acc_ref)

    hb = h_ref[...]
    pos = lax.broadcasted_iota(jnp.int32, (tm, 1), 0) % seq
    first = pos == 0
    last = pos == seq - 1

    def conv(u, cw_ref, cb_ref):
        prev = jnp.where(first, 0.0, pltpu.roll(u, 1, axis=0))
        nxt = jnp.where(last, 0.0, pltpu.roll(u, tm - 1, axis=0))
        return cw_ref[0:1, :] * prev + cw_ref[1:2, :] * u + cw_ref[2:3, :] * nxt + cb_ref[...]

    a = conv(_dot(hb, wa_ref[...]), cwa_ref, cba_ref)
    g = conv(_dot(hb, wg_ref[...]), cwg_ref, cbg_ref)
    acc_ref[...] += _dot((_silu(g) * a).astype(BF16), wd_ref[...])

    @pl.when(j == pl.num_programs(1) - 1)
    def _():
        o_ref[...] = x_ref[...] + gt_ref[0] * _rms(acc_ref[...], gpost_ref[...])


def _conv_ffn(h, w_up, conv_w, conv_b, w_down, x, gt, gpost, tm, seq, tf=256):
    t = h.shape[0]
    groups = gt.shape[0]
    per = (t // tm) // groups
    nf = D_FF // tf
    row = lambda i, j: (i, 0)
    return pl.pallas_call(
        functools.partial(_ffn_kernel, seq=seq),
        out_shape=jax.ShapeDtypeStruct((t, D_MODEL), F32),
        grid=(t // tm, nf),
        in_specs=[pl.BlockSpec((tm, D_MODEL), row),
                  pl.BlockSpec((D_MODEL, tf), lambda i, j: (0, j)),
                  pl.BlockSpec((D_MODEL, tf), lambda i, j: (0, j + nf)),
                  pl.BlockSpec((CONV_W, tf), lambda i, j: (0, j)),
                  pl.BlockSpec((CONV_W, tf), lambda i, j: (0, j + nf)),
                  pl.BlockSpec((1, tf), lambda i, j: (0, j)),
                  pl.BlockSpec((1, tf), lambda i, j: (0, j + nf)),
                  pl.BlockSpec((tf, D_MODEL), lambda i, j: (j, 0)),
                  pl.BlockSpec((tm, D_MODEL), row),
                  pl.BlockSpec((1, 1, D_MODEL), lambda i, j: (i // per, 0, 0)),
                  pl.BlockSpec((1, D_MODEL), lambda i, j: (0, 0))],
        out_specs=pl.BlockSpec((tm, D_MODEL), row),
        scratch_shapes=[pltpu.VMEM((tm, D_MODEL), F32)],
        compiler_params=_cparams(("arbitrary", "arbitrary")),
        name="conv_ffn",
    )(h, w_up, w_up, conv_w, conv_w, conv_b, conv_b, w_down, x, gt, gpost)


def _rope(x, cos, sin_signed):
    lane = lax.broadcasted_iota(jnp.int32, x.shape, 1)
    lower = (lane % 32) < 16
    partner = jnp.where(lower, pltpu.roll(x, 128 - 16, axis=1), pltpu.roll(x, 16, axis=1))
    return x * cos + partner * sin_signed


def _p1_kernel(*refs, rope):
    if rope:
        (x_ref, g_ref, sc_ref, sh_ref, w_ref, gq_ref, wq_ref, gkv_ref, cos_ref, sin_ref,
         q_ref, ckv_ref, kr_ref) = refs
    else:
        (x_ref, g_ref, sc_ref, sh_ref, w_ref, gq_ref, wq_ref, gkv_ref,
         q_ref, ckv_ref, kr_ref) = refs
    h = _rms(x_ref[...], g_ref[...]) * (1.0 + sc_ref[0]) + sh_ref[0]
    z = _dot(h.astype(BF16), w_ref[...])
    qa = _rms(z[:, 0:C_Q_RANK], gq_ref[...])
    q = _dot(qa.astype(BF16), wq_ref[...])
    ckv_ref[...] = _rms(z[:, C_Q_RANK:C_Q_RANK + C_KV_RANK], gkv_ref[...])
    kr = z[:, C_Q_RANK + C_KV_RANK:L1_IN_PAD]
    if rope:
        cos = cos_ref[...]
        sin = sin_ref[...]
        kr = _rope(kr, cos, sin)
    kr_ref[...] = kr[:, 0:C_ROPE]
    for hd in range(C_HEADS):
        c0 = hd * QCAT
        q_ref[:, c0:c0 + 128] = q[:, c0:c0 + 128].astype(q_ref.dtype)
        qr = q[:, c0 + 128:c0 + 256]
        if rope:
            qr = _rope(qr, cos, sin)
        q_ref[:, c0 + 128:c0 + 256] = qr.astype(q_ref.dtype)


def _proj_l1(x, g, sc, sh, w, gq, wq, gkv, tables, tm):
    t = x.shape[0]
    groups = sc.shape[0]
    per = (t // tm) // groups
    row = lambda i: (i, 0)
    grp = lambda i: (i // per, 0, 0)
    fixed = lambda i: (0, 0)
    in_specs = [pl.BlockSpec((tm, D_MODEL), row),
                pl.BlockSpec((1, D_MODEL), fixed),
                pl.BlockSpec((1, 1, D_MODEL), grp),
                pl.BlockSpec((1, 1, D_MODEL), grp),
                pl.BlockSpec(w.shape, fixed),
                pl.BlockSpec((1, C_Q_RANK), fixed),
                pl.BlockSpec(wq.shape, fixed),
                pl.BlockSpec((1, C_KV_RANK), fixed)]
    args = [x, g, sc, sh, w, gq, wq, gkv]
    rope = tables is not None
    if rope:
        n = tables[0].shape[0]
        pos = lambda i: (i % (n // tm), 0)
        in_specs += [pl.BlockSpec((tm, 128), pos)] * 2
        args += list(tables)
    return pl.pallas_call(
        functools.partial(_p1_kernel, rope=rope),
        out_shape=(jax.ShapeDtypeStruct((t, C_HEADS * QCAT), BF16),
                   jax.ShapeDtypeStruct((t, C_KV_RANK), F32),
                   jax.ShapeDtypeStruct((t, C_ROPE), F32)),
        grid=(t // tm,),
        in_specs=in_specs,
        out_specs=(pl.BlockSpec((tm, C_HEADS * QCAT), row),
                   pl.BlockSpec((tm, C_KV_RANK), row),
                   pl.BlockSpec((tm, C_ROPE), row)),
        compiler_params=_cparams(("arbitrary",)),
        name="proj_l1",
    )(*args)


def _attn_kernel(*refs, n_parts, scale):
    q_ref = refs[0]
    kv_parts = refs[1:1 + 2 * n_parts]
    w_ref, o_ref, k_sc, v_sc = refs[1 + 2 * n_parts:]

    @pl.when(pl.program_id(1) == 0)
    def _():
        off = 0
        for p in range(n_parts):
            ckv_ref, kr_ref = kv_parts[2 * p], kv_parts[2 * p + 1]
            nk = ckv_ref.shape[0]
            kv = _dot(ckv_ref[...].astype(BF16), w_ref[...])
            kr = kr_ref[...].astype(BF16)
            kr2 = jnp.concatenate([kr, kr], axis=1)
            for hd in range(C_HEADS):
                k_sc[hd, off:off + nk, 0:128] = kv[:, hd * 128:(hd + 1) * 128].astype(BF16)
                k_sc[hd, off:off + nk, 128:256] = kr2
            v_sc[off:off + nk, :] = kv[:, C_HEADS * C_NOPE:].astype(BF16)
            off += nk

    for hd in range(C_HEADS):
        s = _dot_nt(q_ref[:, hd * QCAT:(hd + 1) * QCAT], k_sc[hd]) * scale
        p = jnp.exp(s - jnp.max(s, axis=1, keepdims=True))
        l = jnp.sum(p, axis=1, keepdims=True)
        o = _dot(p.astype(BF16), v_sc[:, hd * C_DV:(hd + 1) * C_DV])
        o_ref[:, hd * C_DV:(hd + 1) * C_DV] = (o / l).astype(o_ref.dtype)


def _attention(q, kv_parts, w_kvb, tq):
    bsz, n, _ = q.shape
    nk = sum(p[0].shape[1] for p in kv_parts)
    in_specs = [pl.BlockSpec((None, tq, C_HEADS * QCAT), lambda b, i: (b, i, 0))]
    args = [q]
    for ckv, kr in kv_parts:
        in_specs.append(pl.BlockSpec((None, ckv.shape[1], C_KV_RANK), lambda b, i: (b, 0, 0)))
        in_specs.append(pl.BlockSpec((None, kr.shape[1], C_ROPE), lambda b, i: (b, 0, 0)))
        args += [ckv, kr]
    in_specs.append(pl.BlockSpec(w_kvb.shape, lambda b, i: (0, 0)))
    args.append(w_kvb)
    return pl.pallas_call(
        functools.partial(_attn_kernel, n_parts=len(kv_parts),
                          scale=(C_NOPE + C_ROPE) ** -0.5),
        out_shape=jax.ShapeDtypeStruct((bsz, n, C_HEADS * C_DV), BF16),
        grid=(bsz, n // tq),
        in_specs=in_specs,
        out_specs=pl.BlockSpec((None, tq, C_HEADS * C_DV), lambda b, i: (b, i, 0)),
        scratch_shapes=[pltpu.VMEM((C_HEADS, nk, QCAT), BF16),
                        pltpu.VMEM((nk, C_HEADS * C_DV), BF16)],
        compiler_params=_cparams(("arbitrary", "arbitrary")),
        name="mla_attention",
    )(*args)


def _l0_w_in_layout(w_in):
    sizes = (256, 256, 512, 512, 32, 256, 256, 512, 512, 16)
    offs = np.concatenate([[0], np.cumsum(sizes)])
    seg = [w_in[:, offs[i]:offs[i + 1]] for i in range(len(sizes))]
    pad = jnp.zeros((w_in.shape[0], ZG_W - 48), w_in.dtype)
    cols = seg[0:4] + seg[5:9] + [seg[4], seg[9], pad]
    return jnp.concatenate(cols, axis=1).astype(BF16)


def _l1_wq_layout(w_qb):
    blocks = []
    zero = jnp.zeros((w_qb.shape[0], C_ROPE), w_qb.dtype)
    for hd in range(C_HEADS):
        base = hd * (C_NOPE + C_ROPE)
        nope = w_qb[:, base:base + C_NOPE]
        rope = w_qb[:, base + C_NOPE:base + C_NOPE + C_ROPE]
        blocks += [nope, rope, zero] if hd % 2 == 0 else [nope, zero, rope]
    return jnp.concatenate(blocks, axis=1).astype(BF16)


def _l1_wkvb_layout(w_kvb):
    w = w_kvb.reshape(C_KV_RANK, C_HEADS, C_NOPE + C_DV)
    k = w[:, :, :C_NOPE].reshape(C_KV_RANK, C_HEADS * C_NOPE)
    v = w[:, :, C_NOPE:].reshape(C_KV_RANK, C_HEADS * C_DV)
    return jnp.concatenate([k, v], axis=1).astype(BF16)


def _rope_tables(n):
    pos = jnp.arange(n)
    row = (pos // GRID_W).astype(F32)
    col = (pos % GRID_W).astype(F32)
    half = C_ROPE // 2
    inv = 1.0 / (ROPE_THETA ** (jnp.arange(0, half, 2, dtype=F32) / half))
    ang_r = row[:, None] * inv[None, :]
    ang_c = col[:, None] * inv[None, :]
    cos64 = jnp.concatenate([jnp.cos(ang_r)] * 2 + [jnp.cos(ang_c)] * 2, axis=1)
    sin64 = jnp.concatenate([-jnp.sin(ang_r), jnp.sin(ang_r), -jnp.sin(ang_c), jnp.sin(ang_c)], axis=1)
    return jnp.concatenate([cos64] * 2, axis=1), jnp.concatenate([sin64] * 2, axis=1)


def _gate_weight(w, first_row):
    return jnp.zeros((ZG_W, HK), F32).at[first_row:first_row + A_GATE_RANK].set(w)


def _state_t(s):
    b = s.shape[0]
    return s.transpose(0, 3, 1, 2).reshape(b, 128, HK)


def _nm_rows(n_vec, m):
    b = n_vec.shape[0]
    rows = jnp.zeros((b, 8, HK), F32)
    rows = rows.at[:, 0, :].set(n_vec.reshape(b, HK))
    return rows.at[:, 1:1 + HEADS, :].set(jnp.broadcast_to(m[:, :, None], (b, HEADS, HK)))


def kernel(x_prompt, x_sample, state_l0_gla_fwd, state_l0_gla_bwd, state_l0_mlstm_c_fwd, state_l0_mlstm_n_fwd, state_l0_mlstm_m_fwd, state_l0_mlstm_c_bwd, state_l0_mlstm_n_bwd, state_l0_mlstm_m_bwd, cache_l1_ckv, cache_l1_krope, c, c_ctx, l0_w_mod, l0_b_mod, l0_g_pre_mix, l0_g_post_mix, l0_g_pre_ffn, l0_g_post_ffn, l0_w_in, l0_gla_w_gate_f, l0_gla_b_gate_f, l0_gla_w_gate_b, l0_gla_b_gate_b, l0_gla_g_norm, l0_mlstm_b_gates, l0_mlstm_g_norm, l0_w_out, l0_ffn_w_up, l0_ffn_conv_w, l0_ffn_conv_b, l0_ffn_w_down, l1_w_mod, l1_b_mod, l1_g_pre_mix, l1_g_post_mix, l1_g_pre_ffn, l1_g_post_ffn, l1_w_in, l1_g_q_norm, l1_w_qb, l1_g_kv_norm, l1_w_kvb, l1_w_out, l1_ffn_w_up, l1_ffn_conv_w, l1_ffn_conv_b, l1_ffn_w_down):
    bp, sp, d = x_prompt.shape
    bs, ss, _ = x_sample.shape
    row = lambda v: v.reshape(1, -1)

    cond8 = jnp.zeros((8, d), F32).at[0].set(c_ctx).at[1:1 + bs].set(c)
    mods = (_modulation(cond8, l0_w_mod, l0_b_mod), _modulation(cond8, l1_w_mod, l1_b_mod))

    def mod_rows(layer, which, lo, hi):
        m = mods[layer][lo:hi, which * d:(which + 1) * d]
        return m.reshape(hi - lo, 1, d)

    paths = {
        "ctx": dict(x=x_prompt.reshape(bp * sp, d), b=bp, n=sp, rows=(0, 1), tm=512),
        "lat": dict(x=x_sample.reshape(bs * ss, d), b=bs, n=ss, rows=(1, 1 + bs), tm=512),
    }

    w_in0 = _l0_w_in_layout(l0_w_in)
    wgf = _gate_weight(l0_gla_w_gate_f, 0)
    wgb = _gate_weight(l0_gla_w_gate_b, A_GATE_RANK)
    bg = jnp.zeros((1, ZG_W), F32).at[0, GATE_COL:GATE_COL + 16].set(l0_mlstm_b_gates)
    w_out0 = l0_w_out.astype(BF16)
    w_in1 = jnp.pad(l1_w_in, ((0, 0), (0, L1_IN_PAD - l1_w_in.shape[1]))).astype(BF16)
    w_q1 = _l1_wq_layout(l1_w_qb)
    w_kvb1 = _l1_wkvb_layout(l1_w_kvb)
    w_out1 = l1_w_out.astype(BF16)
    ffn = ((l0_ffn_w_up.astype(BF16), l0_ffn_conv_w, row(l0_ffn_conv_b), l0_ffn_w_down.astype(BF16)),
           (l1_ffn_w_up.astype(BF16), l1_ffn_conv_w, row(l1_ffn_conv_b), l1_ffn_w_down.astype(BF16)))
    norms = ((row(l0_g_pre_mix), row(l0_g_post_mix), row(l0_g_pre_ffn), row(l0_g_post_ffn)),
             (row(l1_g_pre_mix), row(l1_g_post_mix), row(l1_g_pre_ffn), row(l1_g_post_ffn)))

    gla_init = (_state_t(state_l0_gla_fwd), _state_t(state_l0_gla_bwd))
    mlstm_init = (_state_t(state_l0_mlstm_c_fwd), _state_t(state_l0_mlstm_c_bwd),
                  _nm_rows(state_l0_mlstm_n_fwd, state_l0_mlstm_m_fwd),
                  _nm_rows(state_l0_mlstm_n_bwd, state_l0_mlstm_m_bwd))
    tables = _rope_tables(ss)

    results = {}
    for name, p in paths.items():
        x, b, n, (lo, hi), tm = p["x"], p["b"], p["n"], p["rows"], p["tm"]
        is_ctx = name == "ctx"
        ffn_tm = tm if is_ctx else n
        sh1, sc1, gt1, sh2, sc2, gt2 = (mod_rows(0, i, lo, hi) for i in range(N_MOD))
        g_pre, g_post, g_pre2, g_post2 = norms[0]

        za, zb, zg = _proj_l0(x, g_pre, sc1, sh1, w_in0, tm)
        za, zb, zg = (t.reshape(b, n, -1) for t in (za, zb, zg))
        oa, sa_f, sa_b = _gla(za, zg, wgf, wgb, row(l0_gla_b_gate_f), row(l0_gla_b_gate_b),
                              row(l0_gla_g_norm), None if is_ctx else gla_init)
        om, c_f, c_b, nm_f, nm_b = _mlstm(zb, zg, bg, row(l0_mlstm_g_norm),
                                          None if is_ctx else mlstm_init)
        x, h2 = _mix_out([oa.reshape(b * n, HV), om.reshape(b * n, HV)], w_out0, x,
                         gt1, g_post, g_pre2, sc2, sh2, tm)
        x = _conv_ffn(h2, *ffn[0], x, gt2, g_post2, ffn_tm, n)

        sh1, sc1, gt1, sh2, sc2, gt2 = (mod_rows(1, i, lo, hi) for i in range(N_MOD))
        g_pre, g_post, g_pre2, g_post2 = norms[1]
        q, ckv, kr = _proj_l1(x, g_pre, sc1, sh1, w_in1, row(l1_g_q_norm), w_q1,
                              row(l1_g_kv_norm), None if is_ctx else tables, tm)
        ckv3 = ckv.reshape(b, n, C_KV_RANK)
        kr3 = kr.reshape(b, n, C_ROPE)
        kv_parts = [(ckv3, kr3)] if is_ctx else [(cache_l1_ckv, cache_l1_krope), (ckv3, kr3)]
        o = _attention(q.reshape(b, n, -1), kv_parts, w_kvb1, 256)
        x, h2 = _mix_out([o.reshape(b * n, C_HEADS * C_DV)], w_out1, x,
                         gt1, g_post, g_pre2, sc2, sh2, tm)
        x = _conv_ffn(h2, *ffn[1], x, gt2, g_post2, ffn_tm, n)

        results[name] = dict(y=x.reshape(b, n, d), gla=(sa_f, sa_b), c=(c_f, c_b),
                             nm=(nm_f, nm_b), ckv=ckv3, kr=kr3)

    r = results["ctx"]
    nm_f, nm_b = r["nm"]

    def n_of(nm):
        return nm[:, 0, :].reshape(bp, B_HEADS, B_DK)

    def m_of(nm):
        return nm[:, 1:1 + B_HEADS, 0]

    return (r["y"], results["lat"]["y"], r["gla"][0], r["gla"][1],
            r["c"][0], n_of(nm_f), m_of(nm_f), r["c"][1], n_of(nm_b), m_of(nm_b),
            r["ckv"], r["kr"])
```

```python
import functools

import numpy as np
import jax
import jax.numpy as jnp
from jax import lax
from jax.experimental import pallas as pl
from jax.experimental.pallas import tpu as pltpu

F32 = jnp.float32
BF16 = jnp.bfloat16

D_MODEL = 1024
EPS = 1e-6
N_MOD = 6
GRID_W = 64
ROPE_THETA = 10000.0

A_HEADS, A_DK, A_DV, A_GATE_RANK, A_GATE_TEMP = 4, 64, 128, 16, 16.0
B_HEADS, B_DK, B_DV = 4, 64, 128
C_HEADS, C_Q_RANK, C_KV_RANK, C_NOPE, C_ROPE, C_DV = 8, 384, 256, 128, 64, 128
D_FF = 2816
CONV_W = 3

HEADS = 4
HK = HEADS * 64
HV = HEADS * 128
ZA_W = 1536
ZG_W = 128
GATE_COL = 32
RCHUNK = 128
L1_IN_PAD = 768
QCAT = 256

VMEM_LIMIT = 56 * 1024 * 1024


def _cparams(sem):
    return pltpu.CompilerParams(dimension_semantics=sem, vmem_limit_bytes=VMEM_LIMIT)


def _dot(a, b):
    return jnp.dot(a, b, preferred_element_type=F32)


def _dot_nt(a, b):
    return lax.dot_general(a, b, (((1,), (1,)), ((), ())), preferred_element_type=F32)


def _dot_tn(a, b):
    return lax.dot_general(a, b, (((0,), (0,)), ((), ())), preferred_element_type=F32)


def _split3(x):
    hi = x.astype(BF16)
    r1 = x - hi.astype(F32)
    mid = r1.astype(BF16)
    lo = (r1 - mid.astype(F32)).astype(BF16)
    return hi, mid, lo


def _tri_dot(tri, x):
    hi, mid, lo = _split3(x)
    return _dot(tri, hi) + _dot(tri, mid) + _dot(tri, lo)


def _dot_tri(x, tri):
    hi, mid, lo = _split3(x)
    return _dot(hi, tri) + _dot(mid, tri) + _dot(lo, tri)


def _rms(x, g):
    return x * lax.rsqrt(jnp.mean(x * x, axis=-1, keepdims=True) + EPS) * g


def _sigmoid(x):
    return 1.0 / (1.0 + jnp.exp(-x))


def _silu(x):
    return x * _sigmoid(x)


def _log_sigmoid(x):
    return jnp.minimum(x, 0.0) - jnp.log(1.0 + jnp.exp(-jnp.abs(x)))


def _tri_masks(n):
    r = lax.broadcasted_iota(jnp.int32, (n, n), 0)
    c = lax.broadcasted_iota(jnp.int32, (n, n), 1)
    return c <= r, c >= r


def _mod_kernel(cond_ref, w_ref, b_ref, o_ref):
    s = _silu(cond_ref[...])
    o_ref[...] = _dot(s.astype(BF16), w_ref[...].astype(BF16)) + b_ref[...]


def _modulation(cond8, w_mod, b_mod):
    n = w_mod.shape[1]
    tn = 1536
    return pl.pallas_call(
        _mod_kernel,
        out_shape=jax.ShapeDtypeStruct((8, n), F32),
        grid=(n // tn,),
        in_specs=[pl.BlockSpec((8, D_MODEL), lambda j: (0, 0)),
                  pl.BlockSpec((D_MODEL, tn), lambda j: (0, j)),
                  pl.BlockSpec((1, tn), lambda j: (0, j))],
        out_specs=pl.BlockSpec((8, tn), lambda j: (0, j)),
        compiler_params=_cparams(("arbitrary",)),
        name="modulation",
    )(cond8, w_mod, b_mod.reshape(1, n))


def _p0_kernel(x_ref, g_ref, sc_ref, sh_ref, w_ref, za_ref, zb_ref, zg_ref):
    h = _rms(x_ref[...], g_ref[...]) * (1.0 + sc_ref[0]) + sh_ref[0]
    hb = h.astype(BF16)
    za_ref[...] = _dot(hb, w_ref[:, 0:ZA_W])
    zb_ref[...] = _dot(hb, w_ref[:, ZA_W:2 * ZA_W])
    zg_ref[...] = _dot(hb, w_ref[:, 2 * ZA_W:2 * ZA_W + ZG_W])


def _proj_l0(x, g, sc, sh, w, tm):
    t = x.shape[0]
    groups = sc.shape[0]
    per = (t // tm) // groups
    nw = w.shape[1]
    row = lambda i: (i, 0)
    grp = lambda i: (i // per, 0, 0)
    fixed = lambda i: (0, 0)
    return pl.pallas_call(
        _p0_kernel,
        out_shape=(jax.ShapeDtypeStruct((t, ZA_W), F32),
                   jax.ShapeDtypeStruct((t, ZA_W), F32),
                   jax.ShapeDtypeStruct((t, ZG_W), F32)),
        grid=(t // tm,),
        in_specs=[pl.BlockSpec((tm, D_MODEL), row),
                  pl.BlockSpec((1, D_MODEL), fixed),
                  pl.BlockSpec((1, 1, D_MODEL), grp),
                  pl.BlockSpec((1, 1, D_MODEL), grp),
                  pl.BlockSpec((D_MODEL, nw), fixed)],
        out_specs=(pl.BlockSpec((tm, ZA_W), row),
                   pl.BlockSpec((tm, ZA_W), row),
                   pl.BlockSpec((tm, ZG_W), row)),
        compiler_params=_cparams(("arbitrary",)),
        name="proj_l0",
    )(x, g, sc, sh, w)


def _head_lane_mask(shape, h):
    lane = lax.broadcasted_iota(jnp.int32, shape, len(shape) - 1)
    return (lane >= h * 64) & (lane < (h + 1) * 64)


def _gla_chunk(za_ref, la_ref, st_ref, o_ref, r0, rev, tri_lo, tri_up, m_lo, m_up):
    L = RCHUNK
    rows = pl.ds(r0, L)
    la = la_ref[rows, :]
    b = _tri_dot(tri_up if rev else tri_lo, la)
    mid = L // 2
    ref = b[mid:mid + 1, :]
    b_end = b[0:1, :] if rev else b[L - 1:L, :]
    q = za_ref[rows, 0:HK]
    k = za_ref[rows, HK:2 * HK]
    v = za_ref[rows, 2 * HK:2 * HK + HV].astype(BF16)
    qf = q * (jnp.exp(b - ref) * (A_DK ** -0.5))
    ke = (k * jnp.exp(ref - b)).astype(BF16)
    st = st_ref[...]
    st_in = (st * jnp.exp(ref)).astype(BF16)
    mask = m_up if rev else m_lo
    for h in range(HEADS):
        qm = jnp.where(_head_lane_mask(qf.shape, h), qf, 0.0).astype(BF16)
        sc = jnp.where(mask, _dot_nt(qm, ke), 0.0).astype(BF16)
        o_ref[rows, h * 128:(h + 1) * 128] = (
            _dot_nt(qm, st_in) + _dot(sc, v[:, h * 128:(h + 1) * 128]))
    rt = _dot_tn(v, ke)
    upd = jnp.zeros_like(st)
    for h in range(HEADS):
        blk = rt[h * 128:(h + 1) * 128, :]
        upd = upd + jnp.where(_head_lane_mask(blk.shape, h), blk, 0.0)
    st_ref[...] = st * jnp.exp(b_end) + upd * jnp.exp(b_end - ref)


def _gla_kernel(*refs, n, has_init):
    if has_init:
        (za_ref, zg_ref, wgf_ref, wgb_ref, bgf_ref, bgb_ref, gn_ref, sf0_ref, sb0_ref,
         o_ref, sf_ref, sb_ref, laf_ref, lab_ref, of_ref, ob_ref, stf_ref, stb_ref) = refs
    else:
        (za_ref, zg_ref, wgf_ref, wgb_ref, bgf_ref, bgb_ref, gn_ref,
         o_ref, sf_ref, sb_ref, laf_ref, lab_ref, of_ref, ob_ref, stf_ref, stb_ref) = refs
    L = RCHUNK
    nc = n // L
    zg = zg_ref[...]
    hi = lax.Precision.HIGHEST
    xf = jnp.dot(zg, wgf_ref[...], precision=hi, preferred_element_type=F32) + bgf_ref[...]
    xb = jnp.dot(zg, wgb_ref[...], precision=hi, preferred_element_type=F32) + bgb_ref[...]
    laf_ref[...] = _log_sigmoid(xf) / A_GATE_TEMP
    lab_ref[...] = _log_sigmoid(xb) / A_GATE_TEMP
    if has_init:
        stf_ref[...] = sf0_ref[...]
        stb_ref[...] = sb0_ref[...]
    else:
        stf_ref[...] = jnp.zeros_like(stf_ref)
        stb_ref[...] = jnp.zeros_like(stb_ref)

    m_lo, m_up = _tri_masks(L)
    tri_lo = jnp.where(m_lo, 1.0, 0.0).astype(BF16)
    tri_up = jnp.where(m_up, 1.0, 0.0).astype(BF16)

    def scan_body(c, carry):
        rf = pl.multiple_of(c * L, L)
        rb = pl.multiple_of((nc - 1 - c) * L, L)
        _gla_chunk(za_ref, laf_ref, stf_ref, of_ref, rf, False, tri_lo, tri_up, m_lo, m_up)
        _gla_chunk(za_ref, lab_ref, stb_ref, ob_ref, rb, True, tri_lo, tri_up, m_lo, m_up)
        return carry

    lax.fori_loop(0, nc, scan_body, 0)

    gn = gn_ref[...]

    def out_body(c, carry):
        rows = pl.ds(pl.multiple_of(c * L, L), L)
        for h in range(HEADS):
            cols = slice(h * 128, (h + 1) * 128)
            o = of_ref[rows, cols] + ob_ref[rows, cols]
            gate = za_ref[rows, 2 * HK + HV + h * 128:2 * HK + HV + (h + 1) * 128]
            o_ref[rows, cols] = (_rms(o, gn) * _silu(gate)).astype(o_ref.dtype)
        return carry

    lax.fori_loop(0, nc, out_body, 0)

    stf_t = stf_ref[...].T
    stb_t = stb_ref[...].T
    for h in range(HEADS):
        sf_ref[h] = stf_t[h * 64:(h + 1) * 64, :]
        sb_ref[h] = stb_t[h * 64:(h + 1) * 64, :]


def _gla(za, zg, wgf, wgb, bgf, bgb, gn, init):
    bsz, n, _ = za.shape
    has_init = init is not None
    per_b3 = lambda b: (b, 0, 0)
    per_b4 = lambda b: (b, 0, 0, 0)
    fixed = lambda b: (0, 0)
    in_specs = [pl.BlockSpec((None, n, ZA_W), per_b3),
                pl.BlockSpec((None, n, ZG_W), per_b3),
                pl.BlockSpec((ZG_W, HK), fixed),
                pl.BlockSpec((ZG_W, HK), fixed),
                pl.BlockSpec((1, HK), fixed),
                pl.BlockSpec((1, HK), fixed),
                pl.BlockSpec((1, 128), fixed)]
    args = [za, zg, wgf, wgb, bgf, bgb, gn]
    if has_init:
        in_specs += [pl.BlockSpec((None, 128, HK), per_b3)] * 2
        args += list(init)
    return pl.pallas_call(
        functools.partial(_gla_kernel, n=n, has_init=has_init),
        out_shape=(jax.ShapeDtypeStruct((bsz, n, HV), BF16),
                   jax.ShapeDtypeStruct((bsz, HEADS, 64, 128), F32),
                   jax.ShapeDtypeStruct((bsz, HEADS, 64, 128), F32)),
        grid=(bsz,),
        in_specs=in_specs,
        out_specs=(pl.BlockSpec((None, n, HV), per_b3),
                   pl.BlockSpec((None, HEADS, 64, 128), per_b4),
                   pl.BlockSpec((None, HEADS, 64, 128), per_b4)),
        scratch_shapes=[pltpu.VMEM((n, HK), F32), pltpu.VMEM((n, HK), F32),
                        pltpu.VMEM((n, HV), F32), pltpu.VMEM((n, HV), F32),
                        pltpu.VMEM((128, HK), F32), pltpu.VMEM((128, HK), F32)],
        compiler_params=_cparams(("arbitrary",)),
        name="gla",
    )(*args)


def _mlstm_chunk(zb_ref, gc_ref, gt_ref, ct_ref, nm_ref, h_ref, ci_chunk, d,
                 tri_lo, tri_up, m_lo, m_up):
    L = RCHUNK
    rev = d == 1
    rows = pl.ds(pl.multiple_of(ci_chunk * L, L), L)
    gcc = gc_ref[rows, :]
    gtc = gt_ref[ci_chunk, GATE_COL:GATE_COL + 16, :]
    b_cols = _tri_dot(tri_up if rev else tri_lo, gcc)
    b_rows = _dot_tri(gtc, tri_lo if rev else tri_up)
    mask = m_up if rev else m_lo
    q = zb_ref[rows, 0:HK]
    k = zb_ref[rows, HK:2 * HK] * (B_DK ** -0.5)
    kb = k.astype(BF16)
    v = zb_ref[rows, 2 * HK:2 * HK + HV].astype(BF16)
    ct = ct_ref[...]
    ctb = ct.astype(BF16)
    n_row = nm_ref[0:1, :]
    k_scale = jnp.zeros((L, HK), F32)
    carry_row = jnp.zeros((1, HK), F32)
    for h in range(HEADS):
        ci = GATE_COL + 4 * d + h
        cf = GATE_COL + 8 + 4 * d + h
        i_col = gcc[:, ci:ci + 1]
        b_col = b_cols[:, cf:cf + 1]
        i_row = gtc[4 * d + h:4 * d + h + 1, :]
        b_row = b_rows[8 + 4 * d + h:8 + 4 * d + h + 1, :]
        b_end = b_col[0:1, :] if rev else b_col[L - 1:L, :]
        m_prev = nm_ref[1 + h:2 + h, 0:1]
        log_intra = jnp.where(mask, b_col + (i_row - b_row), -jnp.inf)
        log_inter = b_col + m_prev
        m_out = jnp.maximum(log_inter, jnp.max(log_intra, axis=1, keepdims=True))
        w_inter = jnp.exp(log_inter - m_out)
        w_intra = jnp.exp(log_intra - m_out)
        hmask = _head_lane_mask(q.shape, h)
        qm = jnp.where(hmask, q, 0.0)
        qmb = qm.astype(BF16)
        qk = _dot_nt(qmb, kb) * w_intra
        num = w_inter * _dot_nt(qmb, ctb) + _dot(qk.astype(BF16), v[:, h * 128:(h + 1) * 128])
        den = (w_inter * jnp.sum(qm * n_row, axis=1, keepdims=True)
               + jnp.sum(qk, axis=1, keepdims=True))
        h_ref[rows, h * 128:(h + 1) * 128] = num / jnp.maximum(jnp.abs(den), jnp.exp(-m_out))
        log_src = b_end - b_col + i_col
        m_new = jnp.maximum(b_end + m_prev, jnp.max(log_src, axis=0, keepdims=True))
        carry_w = jnp.exp(b_end + m_prev - m_new)
        src_w = jnp.exp(log_src - m_new)
        k_scale = k_scale + jnp.where(hmask, src_w, 0.0)
        carry_row = carry_row + jnp.where(_head_lane_mask((1, HK), h), carry_w, 0.0)
        nm_ref[1 + h:2 + h, :] = jnp.broadcast_to(m_new, (1, HK))
    ks = k * k_scale
    rt = _dot_tn(v, ks.astype(BF16))
    upd = jnp.zeros_like(ct)
    for h in range(HEADS):
        blk = rt[h * 128:(h + 1) * 128, :]
        upd = upd + jnp.where(_head_lane_mask(blk.shape, h), blk, 0.0)
    ct_ref[...] = carry_row * ct + upd
    nm_ref[0:1, :] = carry_row * n_row + jnp.sum(ks, axis=0, keepdims=True)


def _mlstm_kernel(*refs, n, has_init):
    if has_init:
        (zb_ref, zg_ref, bg_ref, gn_ref, cf0_ref, cb0_ref, nmf0_ref, nmb0_ref,
         o_ref, cf_ref, cb_ref, nmf_out_ref, nmb_out_ref,
         gc_ref, gt_ref, hf_ref, hb_ref, ctf_ref, ctb_ref, nmf_ref, nmb_ref) = refs
    else:
        (zb_ref, zg_ref, bg_ref, gn_ref,
         o_ref, cf_ref, cb_ref, nmf_out_ref, nmb_out_ref,
         gc_ref, gt_ref, hf_ref, hb_ref, ctf_ref, ctb_ref, nmf_ref, nmb_ref) = refs
    L = RCHUNK
    nc = n // L
    g = zg_ref[...] + bg_ref[...]
    lane = lax.broadcasted_iota(jnp.int32, g.shape, 1)
    is_forget = (lane >= GATE_COL + 8) & (lane < GATE_COL + 16)
    g = jnp.where(is_forget, _log_sigmoid(g), g)
    gc_ref[...] = g
    for ci in range(nc):
        gt_ref[ci] = g[ci * L:(ci + 1) * L, :].T
    if has_init:
        ctf_ref[...] = cf0_ref[...]
        ctb_ref[...] = cb0_ref[...]
        nmf_ref[...] = nmf0_ref[...]
        nmb_ref[...] = nmb0_ref[...]
    else:
        ctf_ref[...] = jnp.zeros_like(ctf_ref)
        ctb_ref[...] = jnp.zeros_like(ctb_ref)
        nmf_ref[...] = jnp.zeros_like(nmf_ref)
        nmb_ref[...] = jnp.zeros_like(nmb_ref)

    m_lo, m_up = _tri_masks(L)
    tri_lo = jnp.where(m_lo, 1.0, 0.0).astype(BF16)
    tri_up = jnp.where(m_up, 1.0, 0.0).astype(BF16)

    def scan_body(c, carry):
        _mlstm_chunk(zb_ref, gc_ref, gt_ref, ctf_ref, nmf_ref, hf_ref, c, 0,
                     tri_lo, tri_up, m_lo, m_up)
        _mlstm_chunk(zb_ref, gc_ref, gt_ref, ctb_ref, nmb_ref, hb_ref, nc - 1 - c, 1,
                     tri_lo, tri_up, m_lo, m_up)
        return carry

    lax.fori_loop(0, nc, scan_body, 0)

    gn = gn_ref[...]

    def out_body(c, carry):
        rows = pl.ds(pl.multiple_of(c * L, L), L)
        for h in range(HEADS):
            cols = slice(h * 128, (h + 1) * 128)
            hh = hf_ref[rows, cols] + hb_ref[rows, cols]
            gate = zb_ref[rows, 2 * HK + HV + h * 128:2 * HK + HV + (h + 1) * 128]
            o_ref[rows, cols] = (_sigmoid(gate) * _rms(hh, gn)).astype(o_ref.dtype)
        return carry

    lax.fori_loop(0, nc, out_body, 0)

    ctf_t = ctf_ref[...].T
    ctb_t = ctb_ref[...].T
    for h in range(HEADS):
        cf_ref[h] = ctf_t[h * 64:(h + 1) * 64, :]
        cb_ref[h] = ctb_t[h * 64:(h + 1) * 64, :]
    nmf_out_ref[...] = nmf_ref[...]
    nmb_out_ref[...] = nmb_ref[...]


def _mlstm(zb, zg, bg, gn, init):
    bsz, n, _ = zb.shape
    has_init = init is not None
    per_b3 = lambda b: (b, 0, 0)
    per_b4 = lambda b: (b, 0, 0, 0)
    fixed = lambda b: (0, 0)
    in_specs = [pl.BlockSpec((None, n, ZA_W), per_b3),
                pl.BlockSpec((None, n, ZG_W), per_b3),
                pl.BlockSpec((1, ZG_W), fixed),
                pl.BlockSpec((1, 128), fixed)]
    args = [zb, zg, bg, gn]
    if has_init:
        in_specs += [pl.BlockSpec((None, 128, HK), per_b3)] * 2
        in_specs += [pl.BlockSpec((None, 8, HK), per_b3)] * 2
        args += list(init)
    return pl.pallas_call(
        functools.partial(_mlstm_kernel, n=n, has_init=has_init),
        out_shape=(jax.ShapeDtypeStruct((bsz, n, HV), BF16),
                   jax.ShapeDtypeStruct((bsz, HEADS, 64, 128), F32),
                   jax.ShapeDtypeStruct((bsz, HEADS, 64, 128), F32),
                   jax.ShapeDtypeStruct((bsz, 8, HK), F32),
                   jax.ShapeDtypeStruct((bsz, 8, HK), F32)),
        grid=(bsz,),
        in_specs=in_specs,
        out_specs=(pl.BlockSpec((None, n, HV), per_b3),
                   pl.BlockSpec((None, HEADS, 64, 128), per_b4),
                   pl.BlockSpec((None, HEADS, 64, 128), per_b4),
                   pl.BlockSpec((None, 8, HK), per_b3),
                   pl.BlockSpec((None, 8, HK), per_b3)),
        scratch_shapes=[pltpu.VMEM((n, ZG_W), F32), pltpu.VMEM((n // RCHUNK, ZG_W, RCHUNK), F32),
                        pltpu.VMEM((n, HV), F32), pltpu.VMEM((n, HV), F32),
                        pltpu.VMEM((128, HK), F32), pltpu.VMEM((128, HK), F32),
                        pltpu.VMEM((8, HK), F32), pltpu.VMEM((8, HK), F32)],
        compiler_params=_cparams(("arbitrary",)),
        name="mlstm",
    )(*args)


def _out_kernel(*refs, n_parts):
    parts = refs[:n_parts]
    (w_ref, x_ref, gt_ref, gpost_ref, gpre_ref, sc_ref, sh_ref, xo_ref, ho_ref) = refs[n_parts:]
    m = None
    off = 0
    for p in parts:
        kp = p.shape[1]
        t = _dot(p[...], w_ref[off:off + kp, :])
        m = t if m is None else m + t
        off += kp
    x1 = x_ref[...] + gt_ref[0] * _rms(m, gpost_ref[...])
    xo_ref[...] = x1
    ho_ref[...] = (_rms(x1, gpre_ref[...]) * (1.0 + sc_ref[0]) + sh_ref[0]).astype(ho_ref.dtype)


def _mix_out(parts, w, x, gt, gpost, gpre, sc, sh, tm):
    t = x.shape[0]
    groups = gt.shape[0]
    per = (t // tm) // groups
    row = lambda i: (i, 0)
    grp = lambda i: (i // per, 0, 0)
    fixed = lambda i: (0, 0)
    vec = pl.BlockSpec((1, D_MODEL), fixed)
    mod = pl.BlockSpec((1, 1, D_MODEL), grp)
    in_specs = [pl.BlockSpec((tm, p.shape[1]), row) for p in parts]
    in_specs += [pl.BlockSpec(w.shape, fixed), pl.BlockSpec((tm, D_MODEL), row),
                 mod, vec, vec, mod, mod]
    return pl.pallas_call(
        functools.partial(_out_kernel, n_parts=len(parts)),
        out_shape=(jax.ShapeDtypeStruct((t, D_MODEL), F32),
                   jax.ShapeDtypeStruct((t, D_MODEL), BF16)),
        grid=(t // tm,),
        in_specs=in_specs,
        out_specs=(pl.BlockSpec((tm, D_MODEL), row), pl.BlockSpec((tm, D_MODEL), row)),
        compiler_params=_cparams(("arbitrary",)),
        name="mix_out",
    )(*parts, w, x, gt, gpost, gpre, sc, sh)


def _ffn_kernel(h_ref, wu_ref, cw_ref, cb_ref, wd_ref, x_ref, gt_ref, gpost_ref, o_ref, act_ref,
                *, seq, tf):
    tm = h_ref.shape[0]
    hb = h_ref[...]
    pos = lax.broadcasted_iota(jnp.int32, (tm, 1), 0) % seq
    first = pos == 0
    last = pos == seq - 1

    def conv(u, cols):
        prev = jnp.where(first, 0.0, pltpu.roll(u, 1, axis=0))
        nxt = jnp.where(last, 0.0, pltpu.roll(u, tm - 1, axis=0))
        return (cw_ref[0:1, cols] * prev + cw_ref[1:2, cols] * u + cw_ref[2:3, cols] * nxt
                + cb_ref[:, cols])

    for j in range(D_FF // tf):
        ca = slice(j * tf, (j + 1) * tf)
        cg = slice(D_FF + j * tf, D_FF + (j + 1) * tf)
        a = conv(_dot(hb, wu_ref[:, ca]), ca)
        g = conv(_dot(hb, wu_ref[:, cg]), cg)
        act_ref[:, ca] = (_silu(g) * a).astype(BF16)
    y = _dot(act_ref[...], wd_ref[...])
    o_ref[...] = x_ref[...] + gt_ref[0] * _rms(y, gpost_ref[...])


def _resident(shape):
    return pl.BlockSpec(shape, lambda *_: (0,) * len(shape), pipeline_mode=pl.Buffered(1))


def _conv_ffn(h, w_up, conv_w, conv_b, w_down, x, gt, gpost, tm, seq, tf=256):
    t = h.shape[0]
    groups = gt.shape[0]
    per = (t // tm) // groups
    row = lambda i: (i, 0)
    return pl.pallas_call(
        functools.partial(_ffn_kernel, seq=seq, tf=tf),
        out_shape=jax.ShapeDtypeStruct((t, D_MODEL), F32),
        grid=(t // tm,),
        in_specs=[pl.BlockSpec((tm, D_MODEL), row),
                  _resident(w_up.shape),
                  _resident(conv_w.shape),
                  _resident(conv_b.shape),
                  _resident(w_down.shape),
                  pl.BlockSpec((tm, D_MODEL), row),
                  pl.BlockSpec((1, 1, D_MODEL), lambda i: (i // per, 0, 0)),
                  _resident((1, D_MODEL))],
        out_specs=pl.BlockSpec((tm, D_MODEL), row),
        scratch_shapes=[pltpu.VMEM((tm, D_FF), BF16)],
        compiler_params=_cparams(("arbitrary",)),
        name="conv_ffn",
    )(h, w_up, conv_w, conv_b, w_down, x, gt, gpost)


def _rope(x, cos, sin_signed):
    lane = lax.broadcasted_iota(jnp.int32, x.shape, 1)
    lower = (lane % 32) < 16
    partner = jnp.where(lower, pltpu.roll(x, 128 - 16, axis=1), pltpu.roll(x, 16, axis=1))
    return x * cos + partner * sin_signed


def _p1_kernel(*refs, rope):
    if rope:
        (x_ref, g_ref, sc_ref, sh_ref, w_ref, gq_ref, wq_ref, gkv_ref, cos_ref, sin_ref,
         q_ref, ckv_ref, kr_ref) = refs
    else:
        (x_ref, g_ref, sc_ref, sh_ref, w_ref, gq_ref, wq_ref, gkv_ref,
         q_ref, ckv_ref, kr_ref) = refs
    h = _rms(x_ref[...], g_ref[...]) * (1.0 + sc_ref[0]) + sh_ref[0]
    z = _dot(h.astype(BF16), w_ref[...])
    qa = _rms(z[:, 0:C_Q_RANK], gq_ref[...])
    q = _dot(qa.astype(BF16), wq_ref[...])
    ckv_ref[...] = _rms(z[:, C_Q_RANK:C_Q_RANK + C_KV_RANK], gkv_ref[...])
    kr = z[:, C_Q_RANK + C_KV_RANK:L1_IN_PAD]
    if rope:
        cos = cos_ref[...]
        sin = sin_ref[...]
        kr = _rope(kr, cos, sin)
    kr_ref[...] = kr[:, 0:C_ROPE]
    for hd in range(C_HEADS):
        c0 = hd * QCAT
        q_ref[:, c0:c0 + 128] = q[:, c0:c0 + 128].astype(q_ref.dtype)
        qr = q[:, c0 + 128:c0 + 256]
        if rope:
            qr = _rope(qr, cos, sin)
        q_ref[:, c0 + 128:c0 + 256] = qr.astype(q_ref.dtype)


def _proj_l1(x, g, sc, sh, w, gq, wq, gkv, tables, tm):
    t = x.shape[0]
    groups = sc.shape[0]
    per = (t // tm) // groups
    row = lambda i: (i, 0)
    grp = lambda i: (i // per, 0, 0)
    fixed = lambda i: (0, 0)
    in_specs = [pl.BlockSpec((tm, D_MODEL), row),
                pl.BlockSpec((1, D_MODEL), fixed),
                pl.BlockSpec((1, 1, D_MODEL), grp),
                pl.BlockSpec((1, 1, D_MODEL), grp),
                pl.BlockSpec(w.shape, fixed),
                pl.BlockSpec((1, C_Q_RANK), fixed),
                pl.BlockSpec(wq.shape, fixed),
                pl.BlockSpec((1, C_KV_RANK), fixed)]
    args = [x, g, sc, sh, w, gq, wq, gkv]
    rope = tables is not None
    if rope:
        n = tables[0].shape[0]
        pos = lambda i: (i % (n // tm), 0)
        in_specs += [pl.BlockSpec((tm, 128), pos)] * 2
        args += list(tables)
    return pl.pallas_call(
        functools.partial(_p1_kernel, rope=rope),
        out_shape=(jax.ShapeDtypeStruct((t, C_HEADS * QCAT), BF16),
                   jax.ShapeDtypeStruct((t, C_KV_RANK), F32),
                   jax.ShapeDtypeStruct((t, C_ROPE), F32)),
        grid=(t // tm,),
        in_specs=in_specs,
        out_specs=(pl.BlockSpec((tm, C_HEADS * QCAT), row),
                   pl.BlockSpec((tm, C_KV_RANK), row),
                   pl.BlockSpec((tm, C_ROPE), row)),
        compiler_params=_cparams(("arbitrary",)),
        name="proj_l1",
    )(*args)


def _attn_kernel(*refs, n_parts, scale):
    q_ref = refs[0]
    kv_parts = refs[1:1 + 2 * n_parts]
    w_ref, o_ref, k_sc, v_sc = refs[1 + 2 * n_parts:]

    @pl.when(pl.program_id(1) == 0)
    def _():
        off = 0
        for p in range(n_parts):
            ckv_ref, kr_ref = kv_parts[2 * p], kv_parts[2 * p + 1]
            nk = ckv_ref.shape[0]
            kv = _dot(ckv_ref[...].astype(BF16), w_ref[...])
            kr = kr_ref[...].astype(BF16)
            kr2 = jnp.concatenate([kr, kr], axis=1)
            for hd in range(C_HEADS):
                k_sc[hd, off:off + nk, 0:128] = kv[:, hd * 128:(hd + 1) * 128].astype(BF16)
                k_sc[hd, off:off + nk, 128:256] = kr2
            v_sc[off:off + nk, :] = kv[:, C_HEADS * C_NOPE:].astype(BF16)
            off += nk

    for hd in range(C_HEADS):
        s = _dot_nt(q_ref[:, hd * QCAT:(hd + 1) * QCAT], k_sc[hd]) * scale
        p = jnp.exp(s - jnp.max(s, axis=1, keepdims=True))
        l = jnp.sum(p, axis=1, keepdims=True)
        o = _dot(p.astype(BF16), v_sc[:, hd * C_DV:(hd + 1) * C_DV])
        o_ref[:, hd * C_DV:(hd + 1) * C_DV] = (o / l).astype(o_ref.dtype)


def _attention(q, kv_parts, w_kvb, tq):
    bsz, n, _ = q.shape
    nk = sum(p[0].shape[1] for p in kv_parts)
    in_specs = [pl.BlockSpec((None, tq, C_HEADS * QCAT), lambda b, i: (b, i, 0))]
    args = [q]
    for ckv, kr in kv_parts:
        in_specs.append(pl.BlockSpec((None, ckv.shape[1], C_KV_RANK), lambda b, i: (b, 0, 0)))
        in_specs.append(pl.BlockSpec((None, kr.shape[1], C_ROPE), lambda b, i: (b, 0, 0)))
        args += [ckv, kr]
    in_specs.append(pl.BlockSpec(w_kvb.shape, lambda b, i: (0, 0)))
    args.append(w_kvb)
    return pl.pallas_call(
        functools.partial(_attn_kernel, n_parts=len(kv_parts),
                          scale=(C_NOPE + C_ROPE) ** -0.5),
        out_shape=jax.ShapeDtypeStruct((bsz, n, C_HEADS * C_DV), BF16),
        grid=(bsz, n // tq),
        in_specs=in_specs,
        out_specs=pl.BlockSpec((None, tq, C_HEADS * C_DV), lambda b, i: (b, i, 0)),
        scratch_shapes=[pltpu.VMEM((C_HEADS, nk, QCAT), BF16),
                        pltpu.VMEM((nk, C_HEADS * C_DV), BF16)],
        compiler_params=_cparams(("arbitrary", "arbitrary")),
        name="mla_attention",
    )(*args)


def _l0_w_in_layout(w_in):
    sizes = (256, 256, 512, 512, 32, 256, 256, 512, 512, 16)
    offs = np.concatenate([[0], np.cumsum(sizes)])
    seg = [w_in[:, offs[i]:offs[i + 1]] for i in range(len(sizes))]
    pad = jnp.zeros((w_in.shape[0], ZG_W - 48), w_in.dtype)
    cols = seg[0:4] + seg[5:9] + [seg[4], seg[9], pad]
    return jnp.concatenate(cols, axis=1).astype(BF16)


def _l1_wq_layout(w_qb):
    blocks = []
    zero = jnp.zeros((w_qb.shape[0], C_ROPE), w_qb.dtype)
    for hd in range(C_HEADS):
        base = hd * (C_NOPE + C_ROPE)
        nope = w_qb[:, base:base + C_NOPE]
        rope = w_qb[:, base + C_NOPE:base + C_NOPE + C_ROPE]
        blocks += [nope, rope, zero] if hd % 2 == 0 else [nope, zero, rope]
    return jnp.concatenate(blocks, axis=1).astype(BF16)


def _l1_wkvb_layout(w_kvb):
    w = w_kvb.reshape(C_KV_RANK, C_HEADS, C_NOPE + C_DV)
    k = w[:, :, :C_NOPE].reshape(C_KV_RANK, C_HEADS * C_NOPE)
    v = w[:, :, C_NOPE:].reshape(C_KV_RANK, C_HEADS * C_DV)
    return jnp.concatenate([k, v], axis=1).astype(BF16)


def _rope_tables(n):
    pos = jnp.arange(n)
    row = (pos // GRID_W).astype(F32)
    col = (pos % GRID_W).astype(F32)
    half = C_ROPE // 2
    inv = 1.0 / (ROPE_THETA ** (jnp.arange(0, half, 2, dtype=F32) / half))
    ang_r = row[:, None] * inv[None, :]
    ang_c = col[:, None] * inv[None, :]
    cos64 = jnp.concatenate([jnp.cos(ang_r)] * 2 + [jnp.cos(ang_c)] * 2, axis=1)
    sin64 = jnp.concatenate([-jnp.sin(ang_r), jnp.sin(ang_r), -jnp.sin(ang_c), jnp.sin(ang_c)], axis=1)
    return jnp.concatenate([cos64] * 2, axis=1), jnp.concatenate([sin64] * 2, axis=1)


def _gate_weight(w, first_row):
    return jnp.zeros((ZG_W, HK), F32).at[first_row:first_row + A_GATE_RANK].set(w)


def _state_t(s):
    b = s.shape[0]
    return s.transpose(0, 3, 1, 2).reshape(b, 128, HK)


def _nm_rows(n_vec, m):
    b = n_vec.shape[0]
    rows = jnp.zeros((b, 8, HK), F32)
    rows = rows.at[:, 0, :].set(n_vec.reshape(b, HK))
    return rows.at[:, 1:1 + HEADS, :].set(jnp.broadcast_to(m[:, :, None], (b, HEADS, HK)))


def kernel(x_prompt, x_sample, state_l0_gla_fwd, state_l0_gla_bwd, state_l0_mlstm_c_fwd, state_l0_mlstm_n_fwd, state_l0_mlstm_m_fwd, state_l0_mlstm_c_bwd, state_l0_mlstm_n_bwd, state_l0_mlstm_m_bwd, cache_l1_ckv, cache_l1_krope, c, c_ctx, l0_w_mod, l0_b_mod, l0_g_pre_mix, l0_g_post_mix, l0_g_pre_ffn, l0_g_post_ffn, l0_w_in, l0_gla_w_gate_f, l0_gla_b_gate_f, l0_gla_w_gate_b, l0_gla_b_gate_b, l0_gla_g_norm, l0_mlstm_b_gates, l0_mlstm_g_norm, l0_w_out, l0_ffn_w_up, l0_ffn_conv_w, l0_ffn_conv_b, l0_ffn_w_down, l1_w_mod, l1_b_mod, l1_g_pre_mix, l1_g_post_mix, l1_g_pre_ffn, l1_g_post_ffn, l1_w_in, l1_g_q_norm, l1_w_qb, l1_g_kv_norm, l1_w_kvb, l1_w_out, l1_ffn_w_up, l1_ffn_conv_w, l1_ffn_conv_b, l1_ffn_w_down):
    bp, sp, d = x_prompt.shape
    bs, ss, _ = x_sample.shape
    row = lambda v: v.reshape(1, -1)

    cond8 = jnp.zeros((8, d), F32).at[0].set(c_ctx).at[1:1 + bs].set(c)
    mods = (_modulation(cond8, l0_w_mod, l0_b_mod), _modulation(cond8, l1_w_mod, l1_b_mod))

    def mod_rows(layer, which, lo, hi):
        m = mods[layer][lo:hi, which * d:(which + 1) * d]
        return m.reshape(hi - lo, 1, d)

    paths = {
        "ctx": dict(x=x_prompt.reshape(bp * sp, d), b=bp, n=sp, rows=(0, 1), tm=512),
        "lat": dict(x=x_sample.reshape(bs * ss, d), b=bs, n=ss, rows=(1, 1 + bs), tm=512),
    }

    w_in0 = _l0_w_in_layout(l0_w_in)
    wgf = _gate_weight(l0_gla_w_gate_f, 0)
    wgb = _gate_weight(l0_gla_w_gate_b, A_GATE_RANK)
    bg = jnp.zeros((1, ZG_W), F32).at[0, GATE_COL:GATE_COL + 16].set(l0_mlstm_b_gates)
    w_out0 = l0_w_out.astype(BF16)
    w_in1 = jnp.pad(l1_w_in, ((0, 0), (0, L1_IN_PAD - l1_w_in.shape[1]))).astype(BF16)
    w_q1 = _l1_wq_layout(l1_w_qb)
    w_kvb1 = _l1_wkvb_layout(l1_w_kvb)
    w_out1 = l1_w_out.astype(BF16)
    ffn = ((l0_ffn_w_up.astype(BF16), l0_ffn_conv_w, row(l0_ffn_conv_b), l0_ffn_w_down.astype(BF16)),
           (l1_ffn_w_up.astype(BF16), l1_ffn_conv_w, row(l1_ffn_conv_b), l1_ffn_w_down.astype(BF16)))
    norms = ((row(l0_g_pre_mix), row(l0_g_post_mix), row(l0_g_pre_ffn), row(l0_g_post_ffn)),
             (row(l1_g_pre_mix), row(l1_g_post_mix), row(l1_g_pre_ffn), row(l1_g_post_ffn)))

    gla_init = (_state_t(state_l0_gla_fwd), _state_t(state_l0_gla_bwd))
    mlstm_init = (_state_t(state_l0_mlstm_c_fwd), _state_t(state_l0_mlstm_c_bwd),
                  _nm_rows(state_l0_mlstm_n_fwd, state_l0_mlstm_m_fwd),
                  _nm_rows(state_l0_mlstm_n_bwd, state_l0_mlstm_m_bwd))
    tables = _rope_tables(ss)

    results = {}
    for name, p in paths.items():
        x, b, n, (lo, hi), tm = p["x"], p["b"], p["n"], p["rows"], p["tm"]
        is_ctx = name == "ctx"
        ffn_tm = tm if is_ctx else n
        sh1, sc1, gt1, sh2, sc2, gt2 = (mod_rows(0, i, lo, hi) for i in range(N_MOD))
        g_pre, g_post, g_pre2, g_post2 = norms[0]

        za, zb, zg = _proj_l0(x, g_pre, sc1, sh1, w_in0, tm)
        za, zb, zg = (t.reshape(b, n, -1) for t in (za, zb, zg))
        oa, sa_f, sa_b = _gla(za, zg, wgf, wgb, row(l0_gla_b_gate_f), row(l0_gla_b_gate_b),
                              row(l0_gla_g_norm), None if is_ctx else gla_init)
        om, c_f, c_b, nm_f, nm_b = _mlstm(zb, zg, bg, row(l0_mlstm_g_norm),
                                          None if is_ctx else mlstm_init)
        x, h2 = _mix_out([oa.reshape(b * n, HV), om.reshape(b * n, HV)], w_out0, x,
                         gt1, g_post, g_pre2, sc2, sh2, tm)
        x = _conv_ffn(h2, *ffn[0], x, gt2, g_post2, ffn_tm, n)

        sh1, sc1, gt1, sh2, sc2, gt2 = (mod_rows(1, i, lo, hi) for i in range(N_MOD))
        g_pre, g_post, g_pre2, g_post2 = norms[1]
        q, ckv, kr = _proj_l1(x, g_pre, sc1, sh1, w_in1, row(l1_g_q_norm), w_q1,
                              row(l1_g_kv_norm), None if is_ctx else tables, tm)
        ckv3 = ckv.reshape(b, n, C_KV_RANK)
        kr3 = kr.reshape(b, n, C_ROPE)
        kv_parts = [(ckv3, kr3)] if is_ctx else [(cache_l1_ckv, cache_l1_krope), (ckv3, kr3)]
        o = _attention(q.reshape(b, n, -1), kv_parts, w_kvb1, 256)
        x, h2 = _mix_out([o.reshape(b * n, C_HEADS * C_DV)], w_out1, x,
                         gt1, g_post, g_pre2, sc2, sh2, tm)
        x = _conv_ffn(h2, *ffn[1], x, gt2, g_post2, ffn_tm, n)

        results[name] = dict(y=x.reshape(b, n, d), gla=(sa_f, sa_b), c=(c_f, c_b),
                             nm=(nm_f, nm_b), ckv=ckv3, kr=kr3)

    r = results["ctx"]
    nm_f, nm_b = r["nm"]

    def n_of(nm):
        return nm[:, 0, :].reshape(bp, B_HEADS, B_DK)

    def m_of(nm):
        return nm[:, 1:1 + B_HEADS, 0]

    return (r["y"], results["lat"]["y"], r["gla"][0], r["gla"][1],
            r["c"][0], n_of(nm_f), m_of(nm_f), r["c"][1], n_of(nm_b), m_of(nm_b),
            r["ckv"], r["kr"])
```

```python
import functools

import numpy as np
import jax
import jax.numpy as jnp
from jax import lax
from jax.experimental import pallas as pl
from jax.experimental.pallas import tpu as pltpu

F32 = jnp.float32
BF16 = jnp.bfloat16

D_MODEL = 1024
EPS = 1e-6
N_MOD = 6
GRID_W = 64
ROPE_THETA = 10000.0

A_HEADS, A_DK, A_DV, A_GATE_RANK, A_GATE_TEMP = 4, 64, 128, 16, 16.0
B_HEADS, B_DK, B_DV = 4, 64, 128
C_HEADS, C_Q_RANK, C_KV_RANK, C_NOPE, C_ROPE, C_DV = 8, 384, 256, 128, 64, 128
D_FF = 2816
CONV_W = 3

HEADS = 4
HK = HEADS * 64
HV = HEADS * 128
ZA_W = 1536
ZG_W = 128
GATE_COL = 32
RCHUNK = 128
L1_IN_PAD = 768
QCAT = 256

VMEM_LIMIT = 56 * 1024 * 1024


def _cparams(sem):
    return pltpu.CompilerParams(dimension_semantics=sem, vmem_limit_bytes=VMEM_LIMIT)


def _dot(a, b):
    return jnp.dot(a, b, preferred_element_type=F32)


def _dot_nt(a, b):
    return lax.dot_general(a, b, (((1,), (1,)), ((), ())), preferred_element_type=F32)


def _dot_tn(a, b):
    return lax.dot_general(a, b, (((0,), (0,)), ((), ())), preferred_element_type=F32)


def _split3(x):
    hi = x.astype(BF16)
    r1 = x - hi.astype(F32)
    mid = r1.astype(BF16)
    lo = (r1 - mid.astype(F32)).astype(BF16)
    return hi, mid, lo


def _tri_dot(tri, x):
    hi, mid, lo = _split3(x)
    return _dot(tri, hi) + _dot(tri, mid) + _dot(tri, lo)


def _dot_tri(x, tri):
    hi, mid, lo = _split3(x)
    return _dot(hi, tri) + _dot(mid, tri) + _dot(lo, tri)


def _dot_hl(a, b):
    ah = a.astype(BF16)
    al = (a - ah.astype(F32)).astype(BF16)
    bh = b.astype(BF16)
    bl = (b - bh.astype(F32)).astype(BF16)
    return _dot(ah, bh) + _dot(ah, bl) + _dot(al, bh)


def _rms(x, g):
    return x * lax.rsqrt(jnp.mean(x * x, axis=-1, keepdims=True) + EPS) * g


def _sigmoid(x):
    return 1.0 / (1.0 + jnp.exp(-x))


def _silu(x):
    return x * _sigmoid(x)


def _log_sigmoid(x):
    return jnp.minimum(x, 0.0) - jnp.log(1.0 + jnp.exp(-jnp.abs(x)))


def _tri_masks(n):
    r = lax.broadcasted_iota(jnp.int32, (n, n), 0)
    c = lax.broadcasted_iota(jnp.int32, (n, n), 1)
    return c <= r, c >= r


def _mod_kernel(cond_ref, w_ref, b_ref, o_ref):
    s = _silu(cond_ref[...])
    o_ref[...] = _dot(s.astype(BF16), w_ref[...].astype(BF16)) + b_ref[...]


def _modulation(cond8, w_mod, b_mod):
    n = w_mod.shape[1]
    tn = 1536
    return pl.pallas_call(
        _mod_kernel,
        out_shape=jax.ShapeDtypeStruct((8, n), F32),
        grid=(n // tn,),
        in_specs=[pl.BlockSpec((8, D_MODEL), lambda j: (0, 0)),
                  pl.BlockSpec((D_MODEL, tn), lambda j: (0, j)),
                  pl.BlockSpec((1, tn), lambda j: (0, j))],
        out_specs=pl.BlockSpec((8, tn), lambda j: (0, j)),
        compiler_params=_cparams(("arbitrary",)),
        name="modulation",
    )(cond8, w_mod, b_mod.reshape(1, n))


def _p0_kernel(x_ref, g_ref, sc_ref, sh_ref, w_ref, za_ref, zb_ref, zg_ref):
    h = _rms(x_ref[...], g_ref[...]) * (1.0 + sc_ref[0]) + sh_ref[0]
    hb = h.astype(BF16)
    za_ref[...] = _dot(hb, w_ref[:, 0:ZA_W])
    zb_ref[...] = _dot(hb, w_ref[:, ZA_W:2 * ZA_W])
    zg_ref[...] = _dot(hb, w_ref[:, 2 * ZA_W:2 * ZA_W + ZG_W])


def _proj_l0(x, g, sc, sh, w, tm):
    t = x.shape[0]
    groups = sc.shape[0]
    per = (t // tm) // groups
    nw = w.shape[1]
    row = lambda i: (i, 0)
    grp = lambda i: (i // per, 0, 0)
    fixed = lambda i: (0, 0)
    return pl.pallas_call(
        _p0_kernel,
        out_shape=(jax.ShapeDtypeStruct((t, ZA_W), F32),
                   jax.ShapeDtypeStruct((t, ZA_W), F32),
                   jax.ShapeDtypeStruct((t, ZG_W), F32)),
        grid=(t // tm,),
        in_specs=[pl.BlockSpec((tm, D_MODEL), row),
                  pl.BlockSpec((1, D_MODEL), fixed),
                  pl.BlockSpec((1, 1, D_MODEL), grp),
                  pl.BlockSpec((1, 1, D_MODEL), grp),
                  pl.BlockSpec((D_MODEL, nw), fixed)],
        out_specs=(pl.BlockSpec((tm, ZA_W), row),
                   pl.BlockSpec((tm, ZA_W), row),
                   pl.BlockSpec((tm, ZG_W), row)),
        compiler_params=_cparams(("arbitrary",)),
        name="proj_l0",
    )(x, g, sc, sh, w)


def _head_lane_mask(shape, h):
    lane = lax.broadcasted_iota(jnp.int32, shape, len(shape) - 1)
    return (lane >= h * 64) & (lane < (h + 1) * 64)


def _head_row(h):
    return jnp.where(_head_lane_mask((1, HK), h), 1.0, 0.0).astype(BF16)


def _gla_prepare(za_ref, la_ref, r0, rev, tri):
    L = RCHUNK
    rows = pl.ds(r0, L)
    b = _tri_dot(tri, la_ref[rows, :])
    mid = L // 2
    ref = b[mid:mid + 1, :]
    b_end = b[0:1, :] if rev else b[L - 1:L, :]
    q = za_ref[rows, 0:HK]
    k = za_ref[rows, HK:2 * HK]
    v = za_ref[rows, 2 * HK:2 * HK + HV].astype(BF16)
    qe = (q * (jnp.exp(b - ref) * (A_DK ** -0.5))).astype(BF16)
    ke = (k * jnp.exp(ref - b)).astype(BF16)
    qm = jnp.concatenate([qe * _head_row(h) for h in range(HEADS)], axis=0)
    rt = _dot_tn(v, ke)
    upd = None
    for h in range(HEADS):
        blk = rt[h * 128:(h + 1) * 128, :]
        blk = jnp.where(_head_lane_mask(blk.shape, h), blk, 0.0)
        upd = blk if upd is None else upd + blk
    return dict(rows=rows, qm=qm, ke=ke, v=v, upd=upd * jnp.exp(b_end - ref),
                decay=jnp.exp(b_end), eref=jnp.exp(ref))


def _gla_apply(items):
    L = RCHUNK
    sts = [st_ref[...] for _, st_ref, _, _ in items]
    o1s = []
    for (p, _, _, _), st in zip(items, sts):
        rhs = jnp.concatenate([p["ke"], (st * p["eref"]).astype(BF16)], axis=0)
        o1s.append(_dot_nt(p["qm"], rhs))
    for h in range(HEADS):
        for (p, _, o_ref, mask), o1 in zip(items, o1s):
            blk = o1[h * L:(h + 1) * L, :]
            sc = jnp.where(mask, blk[:, 0:L], 0.0).astype(BF16)
            o_ref[p["rows"], h * 128:(h + 1) * 128] = (
                blk[:, L:L + 128] + _dot(sc, p["v"][:, h * 128:(h + 1) * 128]))
    for (p, st_ref, _, _), st in zip(items, sts):
        st_ref[...] = st * p["decay"] + p["upd"]


def _gla_kernel(*refs, n, has_init):
    if has_init:
        (za_ref, zg_ref, wgf_ref, wgb_ref, bgf_ref, bgb_ref, gn_ref, sf0_ref, sb0_ref,
         o_ref, sf_ref, sb_ref, laf_ref, lab_ref, of_ref, ob_ref, stf_ref, stb_ref) = refs
    else:
        (za_ref, zg_ref, wgf_ref, wgb_ref, bgf_ref, bgb_ref, gn_ref,
         o_ref, sf_ref, sb_ref, laf_ref, lab_ref, of_ref, ob_ref, stf_ref, stb_ref) = refs
    L = RCHUNK
    nc = n // L
    zg = zg_ref[...]
    xf = _dot_hl(zg, wgf_ref[...]) + bgf_ref[...]
    xb = _dot_hl(zg, wgb_ref[...]) + bgb_ref[...]
    laf_ref[...] = _log_sigmoid(xf) / A_GATE_TEMP
    lab_ref[...] = _log_sigmoid(xb) / A_GATE_TEMP
    if has_init:
        stf_ref[...] = sf0_ref[...]
        stb_ref[...] = sb0_ref[...]
    else:
        stf_ref[...] = jnp.zeros_like(stf_ref)
        stb_ref[...] = jnp.zeros_like(stb_ref)

    m_lo, m_up = _tri_masks(L)
    tri_lo = jnp.where(m_lo, 1.0, 0.0).astype(BF16)
    tri_up = jnp.where(m_up, 1.0, 0.0).astype(BF16)

    def scan_body(g, carry):
        waves = []
        for w in range(2):
            c = 2 * g + w
            rf = pl.multiple_of(c * L, L)
            rb = pl.multiple_of((nc - 1 - c) * L, L)
            waves.append((_gla_prepare(za_ref, laf_ref, rf, False, tri_lo),
                          _gla_prepare(za_ref, lab_ref, rb, True, tri_up)))
        for pf, pb in waves:
            _gla_apply([(pf, stf_ref, of_ref, m_lo), (pb, stb_ref, ob_ref, m_up)])
        return carry

    if nc == 2:
        scan_body(0, 0)
    else:
        lax.fori_loop(0, nc // 2, scan_body, 0)

    gn = gn_ref[...]

    def out_body(c, carry):
        rows = pl.ds(pl.multiple_of(c * L, L), L)
        for h in range(HEADS):
            cols = slice(h * 128, (h + 1) * 128)
            o = of_ref[rows, cols] + ob_ref[rows, cols]
            gate = za_ref[rows, 2 * HK + HV + h * 128:2 * HK + HV + (h + 1) * 128]
            o_ref[rows, cols] = (_rms(o, gn) * _silu(gate)).astype(o_ref.dtype)
        return carry

    lax.fori_loop(0, nc, out_body, 0)

    stf_t = stf_ref[...].T
    stb_t = stb_ref[...].T
    for h in range(HEADS):
        sf_ref[h] = stf_t[h * 64:(h + 1) * 64, :]
        sb_ref[h] = stb_t[h * 64:(h + 1) * 64, :]


def _gla(za, zg, wgf, wgb, bgf, bgb, gn, init):
    bsz, n, _ = za.shape
    has_init = init is not None
    per_b3 = lambda b: (b, 0, 0)
    per_b4 = lambda b: (b, 0, 0, 0)
    fixed = lambda b: (0, 0)
    in_specs = [pl.BlockSpec((None, n, ZA_W), per_b3),
                pl.BlockSpec((None, n, ZG_W), per_b3),
                pl.BlockSpec((ZG_W, HK), fixed),
                pl.BlockSpec((ZG_W, HK), fixed),
                pl.BlockSpec((1, HK), fixed),
                pl.BlockSpec((1, HK), fixed),
                pl.BlockSpec((1, 128), fixed)]
    args = [za, zg, wgf, wgb, bgf, bgb, gn]
    if has_init:
        in_specs += [pl.BlockSpec((None, 128, HK), per_b3)] * 2
        args += list(init)
    return pl.pallas_call(
        functools.partial(_gla_kernel, n=n, has_init=has_init),
        out_shape=(jax.ShapeDtypeStruct((bsz, n, HV), BF16),
                   jax.ShapeDtypeStruct((bsz, HEADS, 64, 128), F32),
                   jax.ShapeDtypeStruct((bsz, HEADS, 64, 128), F32)),
        grid=(bsz,),
        in_specs=in_specs,
        out_specs=(pl.BlockSpec((None, n, HV), per_b3),
                   pl.BlockSpec((None, HEADS, 64, 128), per_b4),
                   pl.BlockSpec((None, HEADS, 64, 128), per_b4)),
        scratch_shapes=[pltpu.VMEM((n, HK), F32), pltpu.VMEM((n, HK), F32),
                        pltpu.VMEM((n, HV), F32), pltpu.VMEM((n, HV), F32),
                        pltpu.VMEM((128, HK), F32), pltpu.VMEM((128, HK), F32)],
        compiler_params=_cparams(("arbitrary",)),
        name="gla",
    )(*args)


T_M, T_MOUT, T_G, T_SRCW, T_MPREV, T_CARRY, T_USED = 0, 8, 16, 24, 32, 40, 48


def _lane_scan(x, op, fill, is_fwd, pos, n):
    k = 1
    while k < RCHUNK:
        pf = jnp.where(pos >= k, pltpu.roll(x, k, axis=1), fill)
        sf = jnp.where(pos < RCHUNK - k, pltpu.roll(x, n - k, axis=1), fill)
        x = op(x, jnp.where(is_fwd, pf, sf))
        k *= 2
    return x


def _mlstm_gate_tables(g, m0, gt_ref, tab_ref, col_ref, n):
    L = RCHUNK
    nc = n // L
    for c in range(nc):
        gt_ref[:, c * L:(c + 1) * L] = g[c * L:(c + 1) * L, :].T
        tab_ref[c, T_USED:, :] = jnp.zeros((ZG_W - T_USED, L), F32)
    gi = gt_ref[GATE_COL:GATE_COL + 8, :]
    lf = gt_ref[GATE_COL + 8:GATE_COL + 16, :]
    is_fwd = lax.broadcasted_iota(jnp.int32, (8, n), 0) < HEADS
    pos = lax.broadcasted_iota(jnp.int32, (8, n), 1) % L
    b = _lane_scan(lf, jnp.add, 0.0, is_fwd, pos, n)
    gg = gi - b
    cmax = _lane_scan(gg, jnp.maximum, -jnp.inf, is_fwd, pos, n)
    fwd_l = lax.broadcasted_iota(jnp.int32, (8, L), 0) < HEADS
    mp = m0
    for j in range(nc):
        cf, cb = j, nc - 1 - j

        def pick(x):
            return jnp.where(fwd_l, x[:, cf * L:(cf + 1) * L], x[:, cb * L:(cb + 1) * L])

        gj, bj, cj, lj = pick(gg), pick(b), pick(cmax), pick(lf)
        mx = jnp.maximum(mp, jnp.max(gj, axis=1, keepdims=True))
        mj = jnp.maximum(cj, mp)
        blocks = (mj, bj + mj, gj, jnp.exp(gj - mx),
                  jnp.broadcast_to(mp, (8, L)), jnp.broadcast_to(jnp.exp(mp - mx), (8, L)))
        for t, val in enumerate(blocks):
            tab_ref[cf, 8 * t:8 * t + HEADS, :] = val[0:HEADS]
            tab_ref[cb, 8 * t + HEADS:8 * t + 8, :] = val[HEADS:8]
        mp = jnp.sum(lj, axis=1, keepdims=True) + mx
    for c in range(nc):
        col_ref[c * L:(c + 1) * L, :] = tab_ref[c].T
    return mp


def _mlstm_prepare(zb_ref, tab_ref, col_ref, c, d, mask):
    L = RCHUNK
    rows = pl.ds(pl.multiple_of(c * L, L), L)
    qb = zb_ref[rows, 0:HK].astype(BF16)
    kb = (zb_ref[rows, HK:2 * HK] * (B_DK ** -0.5)).astype(BF16)
    v = zb_ref[rows, 2 * HK:2 * HK + HV]
    vb = v.astype(BF16)
    vt = v.T
    qm = jnp.concatenate([qb * _head_row(h) for h in range(HEADS)], axis=0)
    s = _dot_nt(qm, kb)
    ones = jnp.ones((L, 128), BF16)
    pv, w_inter, e_negm = [], [], []
    upd = None
    carry = None
    for h in range(HEADS):
        r = 4 * d + h
        m_rep = jnp.broadcast_to(col_ref[rows, T_M + r:T_M + r + 1], (L, 128))
        mout_rep = jnp.broadcast_to(col_ref[rows, T_MOUT + r:T_MOUT + r + 1], (L, 128))
        g_row = tab_ref[c, T_G + r:T_G + r + 1, :]
        w = jnp.exp(jnp.where(mask, g_row - m_rep, -jnp.inf))
        qk = (s[h * L:(h + 1) * L, :] * w).astype(BF16)
        v1 = jnp.concatenate([vb[:, h * 128:(h + 1) * 128], ones], axis=1)
        pv.append(_dot(qk, v1))
        w_inter.append(jnp.exp(tab_ref[c, T_MPREV + r:T_MPREV + r + 1, :] - m_rep))
        e_negm.append(jnp.exp(-mout_rep))
        sw = tab_ref[c, T_SRCW + r:T_SRCW + r + 1, :]
        lhs = jnp.concatenate([vt[h * 128:(h + 1) * 128, :] * sw, jnp.broadcast_to(sw, (128, L))],
                              axis=0).astype(BF16)
        blk = _dot(lhs, kb)
        blk = jnp.where(_head_lane_mask(blk.shape, h), blk, 0.0)
        upd = blk if upd is None else upd + blk
        cr = tab_ref[c, T_CARRY + r:T_CARRY + r + 1, :]
        cr = jnp.where(_head_lane_mask((1, HK), h), jnp.concatenate([cr, cr], axis=1), 0.0)
        carry = cr if carry is None else carry + cr
    return dict(rows=rows, qm=qm, pv=pv, w_inter=w_inter, e_negm=e_negm, upd=upd, carry=carry)


def _mlstm_apply(items):
    L = RCHUNK
    sts = [st_ref[...] for _, st_ref, _ in items]
    a_all = [_dot_nt(p["qm"], st.astype(BF16)) for (p, _, _), st in zip(items, sts)]
    for h in range(HEADS):
        for (p, _, h_ref), a in zip(items, a_all):
            blk = a[h * L:(h + 1) * L, :]
            pv, w, e = p["pv"][h], p["w_inter"][h], p["e_negm"][h]
            num = w * blk[:, 0:128] + pv[:, 0:128]
            den = w * blk[:, 128:256] + pv[:, 128:256]
            h_ref[p["rows"], h * 128:(h + 1) * 128] = num / jnp.maximum(jnp.abs(den), e)
    for (p, st_ref, _), st in zip(items, sts):
        st_ref[...] = p["carry"] * st + p["upd"]


def _mlstm_kernel(*refs, n, has_init):
    if has_init:
        (zb_ref, zg_ref, bg_ref, gn_ref, cf0_ref, cb0_ref, n0_ref, m0_ref,
         o_ref, cf_ref, cb_ref, n_out_ref, m_out_ref,
         gt_ref, tab_ref, col_ref, hf_ref, hb_ref, stf_ref, stb_ref) = refs
    else:
        (zb_ref, zg_ref, bg_ref, gn_ref,
         o_ref, cf_ref, cb_ref, n_out_ref, m_out_ref,
         gt_ref, tab_ref, col_ref, hf_ref, hb_ref, stf_ref, stb_ref) = refs
    L = RCHUNK
    nc = n // L
    g = zg_ref[...] + bg_ref[...]
    lane = lax.broadcasted_iota(jnp.int32, g.shape, 1)
    is_forget = (lane >= GATE_COL + 8) & (lane < GATE_COL + 16)
    g = jnp.where(is_forget, _log_sigmoid(g), g)
    if has_init:
        m0 = m0_ref[:, 0:1]
        stf_ref[...] = jnp.concatenate(
            [cf0_ref[...], jnp.broadcast_to(n0_ref[0:1, :], (128, HK))], axis=0)
        stb_ref[...] = jnp.concatenate(
            [cb0_ref[...], jnp.broadcast_to(n0_ref[1:2, :], (128, HK))], axis=0)
    else:
        m0 = jnp.zeros((8, 1), F32)
        stf_ref[...] = jnp.zeros_like(stf_ref)
        stb_ref[...] = jnp.zeros_like(stb_ref)
    m_fin = _mlstm_gate_tables(g, m0, gt_ref, tab_ref, col_ref, n)

    m_lo, m_up = _tri_masks(L)

    def scan_body(gidx, carry):
        waves = []
        for w in range(2):
            c = 2 * gidx + w
            waves.append((_mlstm_prepare(zb_ref, tab_ref, col_ref, c, 0, m_lo),
                          _mlstm_prepare(zb_ref, tab_ref, col_ref, nc - 1 - c, 1, m_up)))
        for pf, pb in waves:
            _mlstm_apply([(pf, stf_ref, hf_ref), (pb, stb_ref, hb_ref)])
        return carry

    if nc == 2:
        scan_body(0, 0)
    else:
        lax.fori_loop(0, nc // 2, scan_body, 0)

    gn = gn_ref[...]

    def out_body(c, carry):
        rows = pl.ds(pl.multiple_of(c * L, L), L)
        for h in range(HEADS):
            cols = slice(h * 128, (h + 1) * 128)
            hh = hf_ref[rows, cols] + hb_ref[rows, cols]
            gate = zb_ref[rows, 2 * HK + HV + h * 128:2 * HK + HV + (h + 1) * 128]
            o_ref[rows, cols] = (_sigmoid(gate) * _rms(hh, gn)).astype(o_ref.dtype)
        return carry

    lax.fori_loop(0, nc, out_body, 0)

    stf = stf_ref[...]
    stb = stb_ref[...]
    ctf_t = stf[0:128, :].T
    ctb_t = stb[0:128, :].T
    for h in range(HEADS):
        cf_ref[h] = ctf_t[h * 64:(h + 1) * 64, :]
        cb_ref[h] = ctb_t[h * 64:(h + 1) * 64, :]
    n_out_ref[...] = jnp.concatenate(
        [stf[128:129, :], stb[128:129, :], jnp.zeros((6, HK), F32)], axis=0)
    m_out_ref[...] = jnp.broadcast_to(m_fin, (8, 128))


def _mlstm(zb, zg, bg, gn, init):
    bsz, n, _ = zb.shape
    has_init = init is not None
    nc = n // RCHUNK
    per_b3 = lambda b: (b, 0, 0)
    per_b4 = lambda b: (b, 0, 0, 0)
    fixed = lambda b: (0, 0)
    in_specs = [pl.BlockSpec((None, n, ZA_W), per_b3),
                pl.BlockSpec((None, n, ZG_W), per_b3),
                pl.BlockSpec((1, ZG_W), fixed),
                pl.BlockSpec((1, 128), fixed)]
    args = [zb, zg, bg, gn]
    if has_init:
        in_specs += [pl.BlockSpec((None, 128, HK), per_b3)] * 2
        in_specs += [pl.BlockSpec((None, 8, HK), per_b3), pl.BlockSpec((None, 8, 128), per_b3)]
        args += list(init)
    return pl.pallas_call(
        functools.partial(_mlstm_kernel, n=n, has_init=has_init),
        out_shape=(jax.ShapeDtypeStruct((bsz, n, HV), BF16),
                   jax.ShapeDtypeStruct((bsz, HEADS, 64, 128), F32),
                   jax.ShapeDtypeStruct((bsz, HEADS, 64, 128), F32),
                   jax.ShapeDtypeStruct((bsz, 8, HK), F32),
                   jax.ShapeDtypeStruct((bsz, 8, 128), F32)),
        grid=(bsz,),
        in_specs=in_specs,
        out_specs=(pl.BlockSpec((None, n, HV), per_b3),
                   pl.BlockSpec((None, HEADS, 64, 128), per_b4),
                   pl.BlockSpec((None, HEADS, 64, 128), per_b4),
                   pl.BlockSpec((None, 8, HK), per_b3),
                   pl.BlockSpec((None, 8, 128), per_b3)),
        scratch_shapes=[pltpu.VMEM((ZG_W, n), F32), pltpu.VMEM((nc, ZG_W, RCHUNK), F32),
                        pltpu.VMEM((n, ZG_W), F32),
                        pltpu.VMEM((n, HV), F32), pltpu.VMEM((n, HV), F32),
                        pltpu.VMEM((2 * 128, HK), F32), pltpu.VMEM((2 * 128, HK), F32)],
        compiler_params=_cparams(("arbitrary",)),
        name="mlstm",
    )(*args)


def _out_kernel(*refs, n_parts):
    parts = refs[:n_parts]
    (w_ref, x_ref, gt_ref, gpost_ref, gpre_ref, sc_ref, sh_ref, xo_ref, ho_ref) = refs[n_parts:]
    m = None
    off = 0
    for p in parts:
        kp = p.shape[1]
        t = _dot(p[...], w_ref[off:off + kp, :])
        m = t if m is None else m + t
        off += kp
    x1 = x_ref[...] + gt_ref[0] * _rms(m, gpost_ref[...])
    xo_ref[...] = x1
    ho_ref[...] = (_rms(x1, gpre_ref[...]) * (1.0 + sc_ref[0]) + sh_ref[0]).astype(ho_ref.dtype)


def _mix_out(parts, w, x, gt, gpost, gpre, sc, sh, tm):
    t = x.shape[0]
    groups = gt.shape[0]
    per = (t // tm) // groups
    row = lambda i: (i, 0)
    grp = lambda i: (i // per, 0, 0)
    fixed = lambda i: (0, 0)
    vec = pl.BlockSpec((1, D_MODEL), fixed)
    mod = pl.BlockSpec((1, 1, D_MODEL), grp)
    in_specs = [pl.BlockSpec((tm, p.shape[1]), row) for p in parts]
    in_specs += [pl.BlockSpec(w.shape, fixed), pl.BlockSpec((tm, D_MODEL), row),
                 mod, vec, vec, mod, mod]
    return pl.pallas_call(
        functools.partial(_out_kernel, n_parts=len(parts)),
        out_shape=(jax.ShapeDtypeStruct((t, D_MODEL), F32),
                   jax.ShapeDtypeStruct((t, D_MODEL), BF16)),
        grid=(t // tm,),
        in_specs=in_specs,
        out_specs=(pl.BlockSpec((tm, D_MODEL), row), pl.BlockSpec((tm, D_MODEL), row)),
        compiler_params=_cparams(("arbitrary",)),
        name="mix_out",
    )(*parts, w, x, gt, gpost, gpre, sc, sh)


def _ffn_kernel(h_ref, wu_ref, cw_ref, cb_ref, wd_ref, x_ref, gt_ref, gpost_ref, o_ref, act_ref,
                *, seq, tf):
    tm = h_ref.shape[0]
    hb = h_ref[...]
    pos = lax.broadcasted_iota(jnp.int32, (tm, 1), 0) % seq
    first = pos == 0
    last = pos == seq - 1

    def conv(u, cols):
        prev = jnp.where(first, 0.0, pltpu.roll(u, 1, axis=0))
        nxt = jnp.where(last, 0.0, pltpu.roll(u, tm - 1, axis=0))
        return (cw_ref[0:1, cols] * prev + cw_ref[1:2, cols] * u + cw_ref[2:3, cols] * nxt
                + cb_ref[:, cols])

    for j in range(D_FF // tf):
        ca = slice(j * tf, (j + 1) * tf)
        cg = slice(D_FF + j * tf, D_FF + (j + 1) * tf)
        a = conv(_dot(hb, wu_ref[:, ca]), ca)
        g = conv(_dot(hb, wu_ref[:, cg]), cg)
        act_ref[:, ca] = (_silu(g) * a).astype(BF16)
    y = _dot(act_ref[...], wd_ref[...])
    o_ref[...] = x_ref[...] + gt_ref[0] * _rms(y, gpost_ref[...])


def _resident(shape):
    return pl.BlockSpec(shape, lambda *_: (0,) * len(shape), pipeline_mode=pl.Buffered(1))


def _conv_ffn(h, w_up, conv_w, conv_b, w_down, x, gt, gpost, tm, seq, tf=256):
    t = h.shape[0]
    groups = gt.shape[0]
    per = (t // tm) // groups
    row = lambda i: (i, 0)
    return pl.pallas_call(
        functools.partial(_ffn_kernel, seq=seq, tf=tf),
        out_shape=jax.ShapeDtypeStruct((t, D_MODEL), F32),
        grid=(t // tm,),
        in_specs=[pl.BlockSpec((tm, D_MODEL), row),
                  _resident(w_up.shape),
                  _resident(conv_w.shape),
                  _resident(conv_b.shape),
                  _resident(w_down.shape),
                  pl.BlockSpec((tm, D_MODEL), row),
                  pl.BlockSpec((1, 1, D_MODEL), lambda i: (i // per, 0, 0)),
                  _resident((1, D_MODEL))],
        out_specs=pl.BlockSpec((tm, D_MODEL), row),
        scratch_shapes=[pltpu.VMEM((tm, D_FF), BF16)],
        compiler_params=_cparams(("arbitrary",)),
        name="conv_ffn",
    )(h, w_up, conv_w, conv_b, w_down, x, gt, gpost)


def _rope(x, cos, sin_signed):
    lane = lax.broadcasted_iota(jnp.int32, x.shape, 1)
    lower = (lane % 32) < 16
    partner = jnp.where(lower, pltpu.roll(x, 128 - 16, axis=1), pltpu.roll(x, 16, axis=1))
    return x * cos + partner * sin_signed


def _p1_kernel(*refs, rope):
    if rope:
        (x_ref, g_ref, sc_ref, sh_ref, w_ref, gq_ref, wq_ref, gkv_ref, cos_ref, sin_ref,
         q_ref, ckv_ref, kr_ref) = refs
    else:
        (x_ref, g_ref, sc_ref, sh_ref, w_ref, gq_ref, wq_ref, gkv_ref,
         q_ref, ckv_ref, kr_ref) = refs
    h = _rms(x_ref[...], g_ref[...]) * (1.0 + sc_ref[0]) + sh_ref[0]
    z = _dot(h.astype(BF16), w_ref[...])
    qa = _rms(z[:, 0:C_Q_RANK], gq_ref[...])
    q = _dot(qa.astype(BF16), wq_ref[...])
    ckv_ref[...] = _rms(z[:, C_Q_RANK:C_Q_RANK + C_KV_RANK], gkv_ref[...])
    kr = z[:, C_Q_RANK + C_KV_RANK:L1_IN_PAD]
    if rope:
        cos = cos_ref[...]
        sin = sin_ref[...]
        kr = _rope(kr, cos, sin)
    kr_ref[...] = kr[:, 0:C_ROPE]
    for hd in range(C_HEADS):
        c0 = hd * QCAT
        q_ref[:, c0:c0 + 128] = q[:, c0:c0 + 128].astype(q_ref.dtype)
        qr = q[:, c0 + 128:c0 + 256]
        if rope:
            qr = _rope(qr, cos, sin)
        q_ref[:, c0 + 128:c0 + 256] = qr.astype(q_ref.dtype)


def _proj_l1(x, g, sc, sh, w, gq, wq, gkv, tables, tm):
    t = x.shape[0]
    groups = sc.shape[0]
    per = (t // tm) // groups
    row = lambda i: (i, 0)
    grp = lambda i: (i // per, 0, 0)
    fixed = lambda i: (0, 0)
    in_specs = [pl.BlockSpec((tm, D_MODEL), row),
                pl.BlockSpec((1, D_MODEL), fixed),
                pl.BlockSpec((1, 1, D_MODEL), grp),
                pl.BlockSpec((1, 1, D_MODEL), grp),
                pl.BlockSpec(w.shape, fixed),
                pl.BlockSpec((1, C_Q_RANK), fixed),
                pl.BlockSpec(wq.shape, fixed),
                pl.BlockSpec((1, C_KV_RANK), fixed)]
    args = [x, g, sc, sh, w, gq, wq, gkv]
    rope = tables is not None
    if rope:
        n = tables[0].shape[0]
        pos = lambda i: (i % (n // tm), 0)
        in_specs += [pl.BlockSpec((tm, 128), pos)] * 2
        args += list(tables)
    return pl.pallas_call(
        functools.partial(_p1_kernel, rope=rope),
        out_shape=(jax.ShapeDtypeStruct((t, C_HEADS * QCAT), BF16),
                   jax.ShapeDtypeStruct((t, C_KV_RANK), F32),
                   jax.ShapeDtypeStruct((t, C_ROPE), F32)),
        grid=(t // tm,),
        in_specs=in_specs,
        out_specs=(pl.BlockSpec((tm, C_HEADS * QCAT), row),
                   pl.BlockSpec((tm, C_KV_RANK), row),
                   pl.BlockSpec((tm, C_ROPE), row)),
        compiler_params=_cparams(("arbitrary",)),
        name="proj_l1",
    )(*args)


def _attn_kernel(*refs, n_parts, scale):
    q_ref = refs[0]
    kv_parts = refs[1:1 + 2 * n_parts]
    w_ref, o_ref, k_sc, v_sc = refs[1 + 2 * n_parts:]

    @pl.when(pl.program_id(1) == 0)
    def _():
        off = 0
        for p in range(n_parts):
            ckv_ref, kr_ref = kv_parts[2 * p], kv_parts[2 * p + 1]
            nk = ckv_ref.shape[0]
            kv = _dot(ckv_ref[...].astype(BF16), w_ref[...])
            kr = kr_ref[...].astype(BF16)
            kr2 = jnp.concatenate([kr, kr], axis=1)
            for hd in range(C_HEADS):
                k_sc[hd, off:off + nk, 0:128] = kv[:, hd * 128:(hd + 1) * 128].astype(BF16)
                k_sc[hd, off:off + nk, 128:256] = kr2
            v_sc[off:off + nk, :] = kv[:, C_HEADS * C_NOPE:].astype(BF16)
            off += nk

    for hd in range(C_HEADS):
        s = _dot_nt(q_ref[:, hd * QCAT:(hd + 1) * QCAT], k_sc[hd]) * scale
        p = jnp.exp(s - jnp.max(s, axis=1, keepdims=True))
        l = jnp.sum(p, axis=1, keepdims=True)
        o = _dot(p.astype(BF16), v_sc[:, hd * C_DV:(hd + 1) * C_DV])
        o_ref[:, hd * C_DV:(hd + 1) * C_DV] = (o / l).astype(o_ref.dtype)


def _attention(q, kv_parts, w_kvb, tq):
    bsz, n, _ = q.shape
    nk = sum(p[0].shape[1] for p in kv_parts)
    in_specs = [pl.BlockSpec((None, tq, C_HEADS * QCAT), lambda b, i: (b, i, 0))]
    args = [q]
    for ckv, kr in kv_parts:
        in_specs.append(pl.BlockSpec((None, ckv.shape[1], C_KV_RANK), lambda b, i: (b, 0, 0)))
        in_specs.append(pl.BlockSpec((None, kr.shape[1], C_ROPE), lambda b, i: (b, 0, 0)))
        args += [ckv, kr]
    in_specs.append(pl.BlockSpec(w_kvb.shape, lambda b, i: (0, 0)))
    args.append(w_kvb)
    return pl.pallas_call(
        functools.partial(_attn_kernel, n_parts=len(kv_parts),
                          scale=(C_NOPE + C_ROPE) ** -0.5),
        out_shape=jax.ShapeDtypeStruct((bsz, n, C_HEADS * C_DV), BF16),
        grid=(bsz, n // tq),
        in_specs=in_specs,
        out_specs=pl.BlockSpec((None, tq, C_HEADS * C_DV), lambda b, i: (b, i, 0)),
        scratch_shapes=[pltpu.VMEM((C_HEADS, nk, QCAT), BF16),
                        pltpu.VMEM((nk, C_HEADS * C_DV), BF16)],
        compiler_params=_cparams(("arbitrary", "arbitrary")),
        name="mla_attention",
    )(*args)


def _l0_w_in_layout(w_in):
    sizes = (256, 256, 512, 512, 32, 256, 256, 512, 512, 16)
    offs = np.concatenate([[0], np.cumsum(sizes)])
    seg = [w_in[:, offs[i]:offs[i + 1]] for i in range(len(sizes))]
    pad = jnp.zeros((w_in.shape[0], ZG_W - 48), w_in.dtype)
    cols = seg[0:4] + seg[5:9] + [seg[4], seg[9], pad]
    return jnp.concatenate(cols, axis=1).astype(BF16)


def _l1_wq_layout(w_qb):
    blocks = []
    zero = jnp.zeros((w_qb.shape[0], C_ROPE), w_qb.dtype)
    for hd in range(C_HEADS):
        base = hd * (C_NOPE + C_ROPE)
        nope = w_qb[:, base:base + C_NOPE]
        rope = w_qb[:, base + C_NOPE:base + C_NOPE + C_ROPE]
        blocks += [nope, rope, zero] if hd % 2 == 0 else [nope, zero, rope]
    return jnp.concatenate(blocks, axis=1).astype(BF16)


def _l1_wkvb_layout(w_kvb):
    w = w_kvb.reshape(C_KV_RANK, C_HEADS, C_NOPE + C_DV)
    k = w[:, :, :C_NOPE].reshape(C_KV_RANK, C_HEADS * C_NOPE)
    v = w[:, :, C_NOPE:].reshape(C_KV_RANK, C_HEADS * C_DV)
    return jnp.concatenate([k, v], axis=1).astype(BF16)


def _rope_tables(n):
    pos = jnp.arange(n)
    row = (pos // GRID_W).astype(F32)
    col = (pos % GRID_W).astype(F32)
    half = C_ROPE // 2
    inv = 1.0 / (ROPE_THETA ** (jnp.arange(0, half, 2, dtype=F32) / half))
    ang_r = row[:, None] * inv[None, :]
    ang_c = col[:, None] * inv[None, :]
    cos64 = jnp.concatenate([jnp.cos(ang_r)] * 2 + [jnp.cos(ang_c)] * 2, axis=1)
    sin64 = jnp.concatenate([-jnp.sin(ang_r), jnp.sin(ang_r), -jnp.sin(ang_c), jnp.sin(ang_c)], axis=1)
    return jnp.concatenate([cos64] * 2, axis=1), jnp.concatenate([sin64] * 2, axis=1)


def _gate_weight(w, first_row):
    return jnp.zeros((ZG_W, HK), F32).at[first_row:first_row + A_GATE_RANK].set(w)


def _state_t(s):
    b = s.shape[0]
    return s.transpose(0, 3, 1, 2).reshape(b, 128, HK)


def _n_rows(n_fwd, n_bwd):
    b = n_fwd.shape[0]
    rows = jnp.zeros((b, 8, HK), F32)
    return rows.at[:, 0, :].set(n_fwd.reshape(b, HK)).at[:, 1, :].set(n_bwd.reshape(b, HK))


def _m_rows(m_fwd, m_bwd):
    m = jnp.concatenate([m_fwd, m_bwd], axis=1)
    return jnp.broadcast_to(m[:, :, None], (m.shape[0], 8, 128))


def kernel(x_prompt, x_sample, state_l0_gla_fwd, state_l0_gla_bwd, state_l0_mlstm_c_fwd, state_l0_mlstm_n_fwd, state_l0_mlstm_m_fwd, state_l0_mlstm_c_bwd, state_l0_mlstm_n_bwd, state_l0_mlstm_m_bwd, cache_l1_ckv, cache_l1_krope, c, c_ctx, l0_w_mod, l0_b_mod, l0_g_pre_mix, l0_g_post_mix, l0_g_pre_ffn, l0_g_post_ffn, l0_w_in, l0_gla_w_gate_f, l0_gla_b_gate_f, l0_gla_w_gate_b, l0_gla_b_gate_b, l0_gla_g_norm, l0_mlstm_b_gates, l0_mlstm_g_norm, l0_w_out, l0_ffn_w_up, l0_ffn_conv_w, l0_ffn_conv_b, l0_ffn_w_down, l1_w_mod, l1_b_mod, l1_g_pre_mix, l1_g_post_mix, l1_g_pre_ffn, l1_g_post_ffn, l1_w_in, l1_g_q_norm, l1_w_qb, l1_g_kv_norm, l1_w_kvb, l1_w_out, l1_ffn_w_up, l1_ffn_conv_w, l1_ffn_conv_b, l1_ffn_w_down):
    bp, sp, d = x_prompt.shape
    bs, ss, _ = x_sample.shape
    row = lambda v: v.reshape(1, -1)

    cond8 = jnp.zeros((8, d), F32).at[0].set(c_ctx).at[1:1 + bs].set(c)
    mods = (_modulation(cond8, l0_w_mod, l0_b_mod), _modulation(cond8, l1_w_mod, l1_b_mod))

    def mod_rows(layer, which, lo, hi):
        m = mods[layer][lo:hi, which * d:(which + 1) * d]
        return m.reshape(hi - lo, 1, d)

    paths = {
        "ctx": dict(x=x_prompt.reshape(bp * sp, d), b=bp, n=sp, rows=(0, 1), tm=512),
        "lat": dict(x=x_sample.reshape(bs * ss, d), b=bs, n=ss, rows=(1, 1 + bs), tm=512),
    }

    w_in0 = _l0_w_in_layout(l0_w_in)
    wgf = _gate_weight(l0_gla_w_gate_f, 0)
    wgb = _gate_weight(l0_gla_w_gate_b, A_GATE_RANK)
    bg = jnp.zeros((1, ZG_W), F32).at[0, GATE_COL:GATE_COL + 16].set(l0_mlstm_b_gates)
    w_out0 = l0_w_out.astype(BF16)
    w_in1 = jnp.pad(l1_w_in, ((0, 0), (0, L1_IN_PAD - l1_w_in.shape[1]))).astype(BF16)
    w_q1 = _l1_wq_layout(l1_w_qb)
    w_kvb1 = _l1_wkvb_layout(l1_w_kvb)
    w_out1 = l1_w_out.astype(BF16)
    ffn = ((l0_ffn_w_up.astype(BF16), l0_ffn_conv_w, row(l0_ffn_conv_b), l0_ffn_w_down.astype(BF16)),
           (l1_ffn_w_up.astype(BF16), l1_ffn_conv_w, row(l1_ffn_conv_b), l1_ffn_w_down.astype(BF16)))
    norms = ((row(l0_g_pre_mix), row(l0_g_post_mix), row(l0_g_pre_ffn), row(l0_g_post_ffn)),
             (row(l1_g_pre_mix), row(l1_g_post_mix), row(l1_g_pre_ffn), row(l1_g_post_ffn)))

    gla_init = (_state_t(state_l0_gla_fwd), _state_t(state_l0_gla_bwd))
    mlstm_init = (_state_t(state_l0_mlstm_c_fwd), _state_t(state_l0_mlstm_c_bwd),
                  _n_rows(state_l0_mlstm_n_fwd, state_l0_mlstm_n_bwd),
                  _m_rows(state_l0_mlstm_m_fwd, state_l0_mlstm_m_bwd))
    tables = _rope_tables(ss)

    results = {}
    for name, p in paths.items():
        x, b, n, (lo, hi), tm = p["x"], p["b"], p["n"], p["rows"], p["tm"]
        is_ctx = name == "ctx"
        ffn_tm = tm if is_ctx else n
        sh1, sc1, gt1, sh2, sc2, gt2 = (mod_rows(0, i, lo, hi) for i in range(N_MOD))
        g_pre, g_post, g_pre2, g_post2 = norms[0]

        za, zb, zg = _proj_l0(x, g_pre, sc1, sh1, w_in0, tm)
        za, zb, zg = (t.reshape(b, n, -1) for t in (za, zb, zg))
        oa, sa_f, sa_b = _gla(za, zg, wgf, wgb, row(l0_gla_b_gate_f), row(l0_gla_b_gate_b),
                              row(l0_gla_g_norm), None if is_ctx else gla_init)
        om, c_f, c_b, n_rows, m_rows = _mlstm(zb, zg, bg, row(l0_mlstm_g_norm),
                                              None if is_ctx else mlstm_init)
        x, h2 = _mix_out([oa.reshape(b * n, HV), om.reshape(b * n, HV)], w_out0, x,
                         gt1, g_post, g_pre2, sc2, sh2, tm)
        x = _conv_ffn(h2, *ffn[0], x, gt2, g_post2, ffn_tm, n)

        sh1, sc1, gt1, sh2, sc2, gt2 = (mod_rows(1, i, lo, hi) for i in range(N_MOD))
        g_pre, g_post, g_pre2, g_post2 = norms[1]
        q, ckv, kr = _proj_l1(x, g_pre, sc1, sh1, w_in1, row(l1_g_q_norm), w_q1,
                              row(l1_g_kv_norm), None if is_ctx else tables, tm)
        ckv3 = ckv.reshape(b, n, C_KV_RANK)
        kr3 = kr.reshape(b, n, C_ROPE)
        kv_parts = [(ckv3, kr3)] if is_ctx else [(cache_l1_ckv, cache_l1_krope), (ckv3, kr3)]
        o = _attention(q.reshape(b, n, -1), kv_parts, w_kvb1, 256)
        x, h2 = _mix_out([o.reshape(b * n, C_HEADS * C_DV)], w_out1, x,
                         gt1, g_post, g_pre2, sc2, sh2, tm)
        x = _conv_ffn(h2, *ffn[1], x, gt2, g_post2, ffn_tm, n)

        results[name] = dict(y=x.reshape(b, n, d), gla=(sa_f, sa_b), c=(c_f, c_b),
                             n=n_rows, m=m_rows, ckv=ckv3, kr=kr3)

    r = results["ctx"]
    n_fwd = r["n"][:, 0, :].reshape(bp, B_HEADS, B_DK)
    n_bwd = r["n"][:, 1, :].reshape(bp, B_HEADS, B_DK)
    m_fwd = r["m"][:, 0:B_HEADS, 0]
    m_bwd = r["m"][:, B_HEADS:2 * B_HEADS, 0]
    return (r["y"], results["lat"]["y"], r["gla"][0], r["gla"][1],
            r["c"][0], n_fwd, m_fwd, r["c"][1], n_bwd, m_bwd, r["ckv"], r["kr"])
```

```python
import functools

import numpy as np
import jax
import jax.numpy as jnp
from jax import lax
from jax.experimental import pallas as pl
from jax.experimental.pallas import tpu as pltpu

F32 = jnp.float32
BF16 = jnp.bfloat16

D_MODEL = 1024
EPS = 1e-6
N_MOD = 6
GRID_W = 64
ROPE_THETA = 10000.0

A_HEADS, A_DK, A_DV, A_GATE_RANK, A_GATE_TEMP = 4, 64, 128, 16, 16.0
B_HEADS, B_DK, B_DV = 4, 64, 128
C_HEADS, C_Q_RANK, C_KV_RANK, C_NOPE, C_ROPE, C_DV = 8, 384, 256, 128, 64, 128
D_FF = 2816
CONV_W = 3

HEADS = 4
HK = HEADS * 64
HV = HEADS * 128
ZA_W = 1536
ZG_W = 128
GATE_COL = 32
RCHUNK = 128
L1_IN_PAD = 768
QCAT = 256

VMEM_LIMIT = 56 * 1024 * 1024


def _cparams(sem):
    return pltpu.CompilerParams(dimension_semantics=sem, vmem_limit_bytes=VMEM_LIMIT)


def _dot(a, b):
    return jnp.dot(a, b, preferred_element_type=F32)


def _dot_nt(a, b):
    return lax.dot_general(a, b, (((1,), (1,)), ((), ())), preferred_element_type=F32)


def _dot_tn(a, b):
    return lax.dot_general(a, b, (((0,), (0,)), ((), ())), preferred_element_type=F32)


def _split3(x):
    hi = x.astype(BF16)
    r1 = x - hi.astype(F32)
    mid = r1.astype(BF16)
    lo = (r1 - mid.astype(F32)).astype(BF16)
    return hi, mid, lo


def _tri_dot(tri, x):
    hi, mid, lo = _split3(x)
    return _dot(tri, hi) + _dot(tri, mid) + _dot(tri, lo)


def _dot_tri(x, tri):
    hi, mid, lo = _split3(x)
    return _dot(hi, tri) + _dot(mid, tri) + _dot(lo, tri)


def _dot_hl(a, b):
    ah = a.astype(BF16)
    al = (a - ah.astype(F32)).astype(BF16)
    bh = b.astype(BF16)
    bl = (b - bh.astype(F32)).astype(BF16)
    return _dot(ah, bh) + _dot(ah, bl) + _dot(al, bh)


def _rms(x, g):
    return x * lax.rsqrt(jnp.mean(x * x, axis=-1, keepdims=True) + EPS) * g


def _sigmoid(x):
    return 1.0 / (1.0 + jnp.exp(-x))


def _silu(x):
    return x * _sigmoid(x)


def _log_sigmoid(x):
    return jnp.minimum(x, 0.0) - jnp.log(1.0 + jnp.exp(-jnp.abs(x)))


def _tri_masks(n):
    r = lax.broadcasted_iota(jnp.int32, (n, n), 0)
    c = lax.broadcasted_iota(jnp.int32, (n, n), 1)
    return c <= r, c >= r


def _mod_kernel(cond_ref, w_ref, b_ref, o_ref):
    s = _silu(cond_ref[...])
    o_ref[...] = _dot(s.astype(BF16), w_ref[...].astype(BF16)) + b_ref[...]


def _modulation(cond8, w_mod, b_mod):
    n = w_mod.shape[1]
    tn = 1536
    return pl.pallas_call(
        _mod_kernel,
        out_shape=jax.ShapeDtypeStruct((8, n), F32),
        grid=(n // tn,),
        in_specs=[pl.BlockSpec((8, D_MODEL), lambda j: (0, 0)),
                  pl.BlockSpec((D_MODEL, tn), lambda j: (0, j)),
                  pl.BlockSpec((1, tn), lambda j: (0, j))],
        out_specs=pl.BlockSpec((8, tn), lambda j: (0, j)),
        compiler_params=_cparams(("arbitrary",)),
        name="modulation",
    )(cond8, w_mod, b_mod.reshape(1, n))


def _p0_kernel(x_ref, g_ref, sc_ref, sh_ref, w_ref, za_ref, zb_ref, zg_ref):
    h = _rms(x_ref[...], g_ref[...]) * (1.0 + sc_ref[0]) + sh_ref[0]
    hb = h.astype(BF16)
    za_ref[...] = _dot(hb, w_ref[:, 0:ZA_W])
    zb_ref[...] = _dot(hb, w_ref[:, ZA_W:2 * ZA_W])
    zg_ref[...] = _dot(hb, w_ref[:, 2 * ZA_W:2 * ZA_W + ZG_W])


def _proj_l0(x, g, sc, sh, w, tm):
    t = x.shape[0]
    groups = sc.shape[0]
    per = (t // tm) // groups
    nw = w.shape[1]
    row = lambda i: (i, 0)
    grp = lambda i: (i // per, 0, 0)
    fixed = lambda i: (0, 0)
    return pl.pallas_call(
        _p0_kernel,
        out_shape=(jax.ShapeDtypeStruct((t, ZA_W), F32),
                   jax.ShapeDtypeStruct((t, ZA_W), F32),
                   jax.ShapeDtypeStruct((t, ZG_W), F32)),
        grid=(t // tm,),
        in_specs=[pl.BlockSpec((tm, D_MODEL), row),
                  pl.BlockSpec((1, D_MODEL), fixed),
                  pl.BlockSpec((1, 1, D_MODEL), grp),
                  pl.BlockSpec((1, 1, D_MODEL), grp),
                  pl.BlockSpec((D_MODEL, nw), fixed)],
        out_specs=(pl.BlockSpec((tm, ZA_W), row),
                   pl.BlockSpec((tm, ZA_W), row),
                   pl.BlockSpec((tm, ZG_W), row)),
        compiler_params=_cparams(("arbitrary",)),
        name="proj_l0",
    )(x, g, sc, sh, w)


def _head_lane_mask(shape, h):
    lane = lax.broadcasted_iota(jnp.int32, shape, len(shape) - 1)
    return (lane >= h * 64) & (lane < (h + 1) * 64)


def _head_row(h):
    return jnp.where(_head_lane_mask((1, HK), h), 1.0, 0.0).astype(BF16)


def _gla_prepare(za_ref, la_ref, r0, rev, tri):
    L = RCHUNK
    rows = pl.ds(r0, L)
    b = _tri_dot(tri, la_ref[rows, :])
    mid = L // 2
    ref = b[mid:mid + 1, :]
    b_end = b[0:1, :] if rev else b[L - 1:L, :]
    q = za_ref[rows, 0:HK]
    k = za_ref[rows, HK:2 * HK]
    v = za_ref[rows, 2 * HK:2 * HK + HV].astype(BF16)
    qe = (q * (jnp.exp(b - ref) * (A_DK ** -0.5))).astype(BF16)
    ke = (k * jnp.exp(ref - b)).astype(BF16)
    qm = jnp.concatenate([qe * _head_row(h) for h in range(HEADS)], axis=0)
    rt = _dot_tn(v, ke)
    upd = None
    for h in range(HEADS):
        blk = rt[h * 128:(h + 1) * 128, :]
        blk = jnp.where(_head_lane_mask(blk.shape, h), blk, 0.0)
        upd = blk if upd is None else upd + blk
    return dict(rows=rows, qm=qm, ke=ke, v=v, upd=upd * jnp.exp(b_end - ref),
                decay=jnp.exp(b_end), eref=jnp.exp(ref))


def _gla_apply(items):
    L = RCHUNK
    sts = [st_ref[...] for _, st_ref, _, _ in items]
    o1s = []
    for (p, _, _, _), st in zip(items, sts):
        rhs = jnp.concatenate([p["ke"], (st * p["eref"]).astype(BF16)], axis=0)
        o1s.append(_dot_nt(p["qm"], rhs))
    for h in range(HEADS):
        for (p, _, o_ref, mask), o1 in zip(items, o1s):
            blk = o1[h * L:(h + 1) * L, :]
            sc = jnp.where(mask, blk[:, 0:L], 0.0).astype(BF16)
            o_ref[p["rows"], h * 128:(h + 1) * 128] = (
                blk[:, L:L + 128] + _dot(sc, p["v"][:, h * 128:(h + 1) * 128]))
    for (p, st_ref, _, _), st in zip(items, sts):
        st_ref[...] = st * p["decay"] + p["upd"]


def _gla_kernel(*refs, n, has_init):
    if has_init:
        (za_ref, zg_ref, wgf_ref, wgb_ref, bgf_ref, bgb_ref, gn_ref, sf0_ref, sb0_ref,
         o_ref, sf_ref, sb_ref, laf_ref, lab_ref, of_ref, ob_ref, stf_ref, stb_ref) = refs
    else:
        (za_ref, zg_ref, wgf_ref, wgb_ref, bgf_ref, bgb_ref, gn_ref,
         o_ref, sf_ref, sb_ref, laf_ref, lab_ref, of_ref, ob_ref, stf_ref, stb_ref) = refs
    L = RCHUNK
    nc = n // L
    zg = zg_ref[...]
    xf = _dot_hl(zg, wgf_ref[...]) + bgf_ref[...]
    xb = _dot_hl(zg, wgb_ref[...]) + bgb_ref[...]
    laf_ref[...] = _log_sigmoid(xf) / A_GATE_TEMP
    lab_ref[...] = _log_sigmoid(xb) / A_GATE_TEMP
    if has_init:
        stf_ref[...] = sf0_ref[...]
        stb_ref[...] = sb0_ref[...]
    else:
        stf_ref[...] = jnp.zeros_like(stf_ref)
        stb_ref[...] = jnp.zeros_like(stb_ref)

    m_lo, m_up = _tri_masks(L)
    tri_lo = jnp.where(m_lo, 1.0, 0.0).astype(BF16)
    tri_up = jnp.where(m_up, 1.0, 0.0).astype(BF16)

    def scan_body(g, carry):
        waves = []
        for w in range(2):
            c = 2 * g + w
            rf = pl.multiple_of(c * L, L)
            rb = pl.multiple_of((nc - 1 - c) * L, L)
            waves.append((_gla_prepare(za_ref, laf_ref, rf, False, tri_lo),
                          _gla_prepare(za_ref, lab_ref, rb, True, tri_up)))
        for pf, pb in waves:
            _gla_apply([(pf, stf_ref, of_ref, m_lo), (pb, stb_ref, ob_ref, m_up)])
        return carry

    if nc == 2:
        scan_body(0, 0)
    else:
        lax.fori_loop(0, nc // 2, scan_body, 0)

    gn = gn_ref[...]

    def out_body(c, carry):
        rows = pl.ds(pl.multiple_of(c * L, L), L)
        for h in range(HEADS):
            cols = slice(h * 128, (h + 1) * 128)
            o = of_ref[rows, cols] + ob_ref[rows, cols]
            gate = za_ref[rows, 2 * HK + HV + h * 128:2 * HK + HV + (h + 1) * 128]
            o_ref[rows, cols] = (_rms(o, gn) * _silu(gate)).astype(o_ref.dtype)
        return carry

    lax.fori_loop(0, nc, out_body, 0)

    stf_t = stf_ref[...].T
    stb_t = stb_ref[...].T
    for h in range(HEADS):
        sf_ref[h] = stf_t[h * 64:(h + 1) * 64, :]
        sb_ref[h] = stb_t[h * 64:(h + 1) * 64, :]


def _gla(za, zg, wgf, wgb, bgf, bgb, gn, init):
    bsz, n, _ = za.shape
    has_init = init is not None
    per_b3 = lambda b: (b, 0, 0)
    per_b4 = lambda b: (b, 0, 0, 0)
    fixed = lambda b: (0, 0)
    in_specs = [pl.BlockSpec((None, n, ZA_W), per_b3),
                pl.BlockSpec((None, n, ZG_W), per_b3),
                pl.BlockSpec((ZG_W, HK), fixed),
                pl.BlockSpec((ZG_W, HK), fixed),
                pl.BlockSpec((1, HK), fixed),
                pl.BlockSpec((1, HK), fixed),
                pl.BlockSpec((1, 128), fixed)]
    args = [za, zg, wgf, wgb, bgf, bgb, gn]
    if has_init:
        in_specs += [pl.BlockSpec((None, 128, HK), per_b3)] * 2
        args += list(init)
    return pl.pallas_call(
        functools.partial(_gla_kernel, n=n, has_init=has_init),
        out_shape=(jax.ShapeDtypeStruct((bsz, n, HV), BF16),
                   jax.ShapeDtypeStruct((bsz, HEADS, 64, 128), F32),
                   jax.ShapeDtypeStruct((bsz, HEADS, 64, 128), F32)),
        grid=(bsz,),
        in_specs=in_specs,
        out_specs=(pl.BlockSpec((None, n, HV), per_b3),
                   pl.BlockSpec((None, HEADS, 64, 128), per_b4),
                   pl.BlockSpec((None, HEADS, 64, 128), per_b4)),
        scratch_shapes=[pltpu.VMEM((n, HK), F32), pltpu.VMEM((n, HK), F32),
                        pltpu.VMEM((n, HV), F32), pltpu.VMEM((n, HV), F32),
                        pltpu.VMEM((128, HK), F32), pltpu.VMEM((128, HK), F32)],
        compiler_params=_cparams(("arbitrary",)),
        name="gla",
    )(*args)


T_M, T_MOUT, T_G, T_SRCW, T_MPREV, T_CARRY, T_USED = 0, 8, 16, 24, 32, 40, 48


def _lane_scan(x, op, fill, is_fwd, pos, n):
    k = 1
    while k < RCHUNK:
        pf = jnp.where(pos >= k, pltpu.roll(x, k, axis=1), fill)
        sf = jnp.where(pos < RCHUNK - k, pltpu.roll(x, n - k, axis=1), fill)
        x = op(x, jnp.where(is_fwd, pf, sf))
        k *= 2
    return x


def _mlstm_gate_tables(g, m0, gt_ref, tab_ref, col_ref, n):
    L = RCHUNK
    nc = n // L
    for c in range(nc):
        gt_ref[:, c * L:(c + 1) * L] = g[c * L:(c + 1) * L, :].T
        tab_ref[c, T_USED:, :] = jnp.zeros((ZG_W - T_USED, L), F32)
    gi = gt_ref[GATE_COL:GATE_COL + 8, :]
    lf = gt_ref[GATE_COL + 8:GATE_COL + 16, :]
    is_fwd = lax.broadcasted_iota(jnp.int32, (8, n), 0) < HEADS
    pos = lax.broadcasted_iota(jnp.int32, (8, n), 1) % L
    b = _lane_scan(lf, jnp.add, 0.0, is_fwd, pos, n)
    gg = gi - b
    cmax = _lane_scan(gg, jnp.maximum, -jnp.inf, is_fwd, pos, n)
    fwd_l = lax.broadcasted_iota(jnp.int32, (8, L), 0) < HEADS
    mp = m0
    for j in range(nc):
        cf, cb = j, nc - 1 - j

        def pick(x):
            return jnp.where(fwd_l, x[:, cf * L:(cf + 1) * L], x[:, cb * L:(cb + 1) * L])

        gj, bj, cj, lj = pick(gg), pick(b), pick(cmax), pick(lf)
        mx = jnp.maximum(mp, jnp.max(gj, axis=1, keepdims=True))
        mj = jnp.maximum(cj, mp)
        blocks = (mj, bj + mj, gj, jnp.exp(gj - mx),
                  jnp.broadcast_to(mp, (8, L)), jnp.broadcast_to(jnp.exp(mp - mx), (8, L)))
        for t, val in enumerate(blocks):
            tab_ref[cf, 8 * t:8 * t + HEADS, :] = val[0:HEADS]
            tab_ref[cb, 8 * t + HEADS:8 * t + 8, :] = val[HEADS:8]
        mp = jnp.sum(lj, axis=1, keepdims=True) + mx
    for c in range(nc):
        col_ref[c * L:(c + 1) * L, :] = tab_ref[c].T
    return mp


def _mlstm_prepare(zb_ref, tab_ref, col_ref, c, d, mask):
    L = RCHUNK
    rows = pl.ds(pl.multiple_of(c * L, L), L)
    qb = zb_ref[rows, 0:HK].astype(BF16)
    kb = (zb_ref[rows, HK:2 * HK] * (B_DK ** -0.5)).astype(BF16)
    v = zb_ref[rows, 2 * HK:2 * HK + HV]
    vb = v.astype(BF16)
    vt = v.T
    qm = jnp.concatenate([qb * _head_row(h) for h in range(HEADS)], axis=0)
    s = _dot_nt(qm, kb)
    ones = jnp.ones((L, 128), BF16)
    pv, w_inter, e_negm = [], [], []
    upd = None
    carry = None
    for h in range(HEADS):
        r = 4 * d + h
        m_rep = jnp.broadcast_to(col_ref[rows, T_M + r:T_M + r + 1], (L, 128))
        mout_rep = jnp.broadcast_to(col_ref[rows, T_MOUT + r:T_MOUT + r + 1], (L, 128))
        g_row = tab_ref[c, T_G + r:T_G + r + 1, :]
        w = jnp.exp(jnp.where(mask, g_row - m_rep, -jnp.inf))
        qk = (s[h * L:(h + 1) * L, :] * w).astype(BF16)
        v1 = jnp.concatenate([vb[:, h * 128:(h + 1) * 128], ones], axis=1)
        pv.append(_dot(qk, v1))
        w_inter.append(jnp.exp(tab_ref[c, T_MPREV + r:T_MPREV + r + 1, :] - m_rep))
        e_negm.append(jnp.exp(-mout_rep))
        sw = tab_ref[c, T_SRCW + r:T_SRCW + r + 1, :]
        lhs = jnp.concatenate([vt[h * 128:(h + 1) * 128, :] * sw, jnp.broadcast_to(sw, (128, L))],
                              axis=0).astype(BF16)
        blk = _dot(lhs, kb)
        blk = jnp.where(_head_lane_mask(blk.shape, h), blk, 0.0)
        upd = blk if upd is None else upd + blk
        cr = tab_ref[c, T_CARRY + r:T_CARRY + r + 1, :]
        cr = jnp.where(_head_lane_mask((1, HK), h), jnp.concatenate([cr, cr], axis=1), 0.0)
        carry = cr if carry is None else carry + cr
    return dict(rows=rows, qm=qm, pv=pv, w_inter=w_inter, e_negm=e_negm, upd=upd, carry=carry)


def _mlstm_apply(items):
    L = RCHUNK
    sts = [st_ref[...] for _, st_ref, _ in items]
    a_all = [_dot_nt(p["qm"], st.astype(BF16)) for (p, _, _), st in zip(items, sts)]
    for h in range(HEADS):
        for (p, _, h_ref), a in zip(items, a_all):
            blk = a[h * L:(h + 1) * L, :]
            pv, w, e = p["pv"][h], p["w_inter"][h], p["e_negm"][h]
            num = w * blk[:, 0:128] + pv[:, 0:128]
            den = w * blk[:, 128:256] + pv[:, 128:256]
            h_ref[p["rows"], h * 128:(h + 1) * 128] = num / jnp.maximum(jnp.abs(den), e)
    for (p, st_ref, _), st in zip(items, sts):
        st_ref[...] = p["carry"] * st + p["upd"]


def _mlstm_kernel(*refs, n, has_init):
    if has_init:
        (zb_ref, zg_ref, bg_ref, gn_ref, cf0_ref, cb0_ref, n0_ref, m0_ref,
         o_ref, cf_ref, cb_ref, n_out_ref, m_out_ref,
         gt_ref, tab_ref, col_ref, hf_ref, hb_ref, stf_ref, stb_ref) = refs
    else:
        (zb_ref, zg_ref, bg_ref, gn_ref,
         o_ref, cf_ref, cb_ref, n_out_ref, m_out_ref,
         gt_ref, tab_ref, col_ref, hf_ref, hb_ref, stf_ref, stb_ref) = refs
    L = RCHUNK
    nc = n // L
    g = zg_ref[...] + bg_ref[...]
    lane = lax.broadcasted_iota(jnp.int32, g.shape, 1)
    is_forget = (lane >= GATE_COL + 8) & (lane < GATE_COL + 16)
    g = jnp.where(is_forget, _log_sigmoid(g), g)
    if has_init:
        m0 = m0_ref[:, 0:1]
        stf_ref[...] = jnp.concatenate(
            [cf0_ref[...], jnp.broadcast_to(n0_ref[0:1, :], (128, HK))], axis=0)
        stb_ref[...] = jnp.concatenate(
            [cb0_ref[...], jnp.broadcast_to(n0_ref[1:2, :], (128, HK))], axis=0)
    else:
        m0 = jnp.zeros((8, 1), F32)
        stf_ref[...] = jnp.zeros_like(stf_ref)
        stb_ref[...] = jnp.zeros_like(stb_ref)
    m_fin = _mlstm_gate_tables(g, m0, gt_ref, tab_ref, col_ref, n)

    m_lo, m_up = _tri_masks(L)

    def scan_body(gidx, carry):
        waves = []
        for w in range(2):
            c = 2 * gidx + w
            waves.append((_mlstm_prepare(zb_ref, tab_ref, col_ref, c, 0, m_lo),
                          _mlstm_prepare(zb_ref, tab_ref, col_ref, nc - 1 - c, 1, m_up)))
        for pf, pb in waves:
            _mlstm_apply([(pf, stf_ref, hf_ref), (pb, stb_ref, hb_ref)])
        return carry

    if nc == 2:
        scan_body(0, 0)
    else:
        lax.fori_loop(0, nc // 2, scan_body, 0)

    gn = gn_ref[...]

    def out_body(c, carry):
        rows = pl.ds(pl.multiple_of(c * L, L), L)
        for h in range(HEADS):
            cols = slice(h * 128, (h + 1) * 128)
            hh = hf_ref[rows, cols] + hb_ref[rows, cols]
            gate = zb_ref[rows, 2 * HK + HV + h * 128:2 * HK + HV + (h + 1) * 128]
            o_ref[rows, cols] = (_sigmoid(gate) * _rms(hh, gn)).astype(o_ref.dtype)
        return carry

    lax.fori_loop(0, nc, out_body, 0)

    stf = stf_ref[...]
    stb = stb_ref[...]
    ctf_t = stf[0:128, :].T
    ctb_t = stb[0:128, :].T
    for h in range(HEADS):
        cf_ref[h] = ctf_t[h * 64:(h + 1) * 64, :]
        cb_ref[h] = ctb_t[h * 64:(h + 1) * 64, :]
    n_out_ref[...] = jnp.concatenate(
        [stf[128:129, :], stb[128:129, :], jnp.zeros((6, HK), F32)], axis=0)
    m_out_ref[...] = jnp.broadcast_to(m_fin, (8, 128))


def _mlstm(zb, zg, bg, gn, init):
    bsz, n, _ = zb.shape
    has_init = init is not None
    nc = n // RCHUNK
    per_b3 = lambda b: (b, 0, 0)
    per_b4 = lambda b: (b, 0, 0, 0)
    fixed = lambda b: (0, 0)
    in_specs = [pl.BlockSpec((None, n, ZA_W), per_b3),
                pl.BlockSpec((None, n, ZG_W), per_b3),
                pl.BlockSpec((1, ZG_W), fixed),
                pl.BlockSpec((1, 128), fixed)]
    args = [zb, zg, bg, gn]
    if has_init:
        in_specs += [pl.BlockSpec((None, 128, HK), per_b3)] * 2
        in_specs += [pl.BlockSpec((None, 8, HK), per_b3), pl.BlockSpec((None, 8, 128), per_b3)]
        args += list(init)
    return pl.pallas_call(
        functools.partial(_mlstm_kernel, n=n, has_init=has_init),
        out_shape=(jax.ShapeDtypeStruct((bsz, n, HV), BF16),
                   jax.ShapeDtypeStruct((bsz, HEADS, 64, 128), F32),
                   jax.ShapeDtypeStruct((bsz, HEADS, 64, 128), F32),
                   jax.ShapeDtypeStruct((bsz, 8, HK), F32),
                   jax.ShapeDtypeStruct((bsz, 8, 128), F32)),
        grid=(bsz,),
        in_specs=in_specs,
        out_specs=(pl.BlockSpec((None, n, HV), per_b3),
                   pl.BlockSpec((None, HEADS, 64, 128), per_b4),
                   pl.BlockSpec((None, HEADS, 64, 128), per_b4),
                   pl.BlockSpec((None, 8, HK), per_b3),
                   pl.BlockSpec((None, 8, 128), per_b3)),
        scratch_shapes=[pltpu.VMEM((ZG_W, n), F32), pltpu.VMEM((nc, ZG_W, RCHUNK), F32),
                        pltpu.VMEM((n, ZG_W), F32),
                        pltpu.VMEM((n, HV), F32), pltpu.VMEM((n, HV), F32),
                        pltpu.VMEM((2 * 128, HK), F32), pltpu.VMEM((2 * 128, HK), F32)],
        compiler_params=_cparams(("arbitrary",)),
        name="mlstm",
    )(*args)


def _mix_ffn_kernel(*refs, n_parts, seq, tf):
    parts = refs[:n_parts]
    (wo_ref, x_ref, gt1_ref, gpost1_ref, gpre2_ref, sc2_ref, sh2_ref,
     wu_ref, cw_ref, cb_ref, wd_ref, gt2_ref, gpost2_ref, o_ref, act_ref) = refs[n_parts:]
    tm = x_ref.shape[0]
    m = None
    off = 0
    for p in parts:
        kp = p.shape[1]
        t = _dot(p[...], wo_ref[off:off + kp, :])
        m = t if m is None else m + t
        off += kp
    x1 = x_ref[...] + gt1_ref[0] * _rms(m, gpost1_ref[...])
    o_ref[...] = x1
    hb = (_rms(x1, gpre2_ref[...]) * (1.0 + sc2_ref[0]) + sh2_ref[0]).astype(BF16)
    pos = lax.broadcasted_iota(jnp.int32, (tm, 1), 0) % seq
    first = pos == 0
    last = pos == seq - 1

    def conv(u, cols):
        prev = jnp.where(first, 0.0, pltpu.roll(u, 1, axis=0))
        nxt = jnp.where(last, 0.0, pltpu.roll(u, tm - 1, axis=0))
        return (cw_ref[0:1, cols] * prev + cw_ref[1:2, cols] * u + cw_ref[2:3, cols] * nxt
                + cb_ref[:, cols])

    for j in range(D_FF // tf):
        ca = slice(j * tf, (j + 1) * tf)
        cg = slice(D_FF + j * tf, D_FF + (j + 1) * tf)
        a = conv(_dot(hb, wu_ref[:, ca]), ca)
        g = conv(_dot(hb, wu_ref[:, cg]), cg)
        act_ref[:, ca] = (_silu(g) * a).astype(BF16)
    y = _dot(act_ref[...], wd_ref[...])
    o_ref[...] = o_ref[...] + gt2_ref[0] * _rms(y, gpost2_ref[...])


def _resident(shape):
    return pl.BlockSpec(shape, lambda *_: (0,) * len(shape), pipeline_mode=pl.Buffered(1))


def _mix_ffn(parts, w_out, x, gt1, gpost1, gpre2, sc2, sh2, w_up, conv_w, conv_b, w_down,
             gt2, gpost2, tm, seq, tf=256):
    t = x.shape[0]
    groups = gt1.shape[0]
    per = (t // tm) // groups
    row = lambda i: (i, 0)
    mod = pl.BlockSpec((1, 1, D_MODEL), lambda i: (i // per, 0, 0))
    vec = _resident((1, D_MODEL))
    in_specs = [pl.BlockSpec((tm, p.shape[1]), row) for p in parts]
    in_specs += [_resident(w_out.shape), pl.BlockSpec((tm, D_MODEL), row), mod, vec, vec, mod, mod,
                 _resident(w_up.shape), _resident(conv_w.shape), _resident(conv_b.shape),
                 _resident(w_down.shape), mod, vec]
    return pl.pallas_call(
        functools.partial(_mix_ffn_kernel, n_parts=len(parts), seq=seq, tf=tf),
        out_shape=jax.ShapeDtypeStruct((t, D_MODEL), F32),
        grid=(t // tm,),
        in_specs=in_specs,
        out_specs=pl.BlockSpec((tm, D_MODEL), row),
        scratch_shapes=[pltpu.VMEM((tm, D_FF), BF16)],
        compiler_params=_cparams(("arbitrary",)),
        name="mix_ffn",
    )(*parts, w_out, x, gt1, gpost1, gpre2, sc2, sh2, w_up, conv_w, conv_b, w_down, gt2, gpost2)


def _rope(x, cos, sin_signed):
    lane = lax.broadcasted_iota(jnp.int32, x.shape, 1)
    lower = (lane % 32) < 16
    partner = jnp.where(lower, pltpu.roll(x, 128 - 16, axis=1), pltpu.roll(x, 16, axis=1))
    return x * cos + partner * sin_signed


def _p1_kernel(*refs, rope):
    if rope:
        (x_ref, g_ref, sc_ref, sh_ref, w_ref, gq_ref, wq_ref, gkv_ref, cos_ref, sin_ref,
         q_ref, ckv_ref, kr_ref) = refs
    else:
        (x_ref, g_ref, sc_ref, sh_ref, w_ref, gq_ref, wq_ref, gkv_ref,
         q_ref, ckv_ref, kr_ref) = refs
    h = _rms(x_ref[...], g_ref[...]) * (1.0 + sc_ref[0]) + sh_ref[0]
    z = _dot(h.astype(BF16), w_ref[...])
    qa = _rms(z[:, 0:C_Q_RANK], gq_ref[...])
    q = _dot(qa.astype(BF16), wq_ref[...])
    ckv_ref[...] = _rms(z[:, C_Q_RANK:C_Q_RANK + C_KV_RANK], gkv_ref[...])
    kr = z[:, C_Q_RANK + C_KV_RANK:L1_IN_PAD]
    if rope:
        cos = cos_ref[...]
        sin = sin_ref[...]
        kr = _rope(kr, cos, sin)
    kr_ref[...] = kr[:, 0:C_ROPE]
    for hd in range(C_HEADS):
        c0 = hd * QCAT
        q_ref[:, c0:c0 + 128] = q[:, c0:c0 + 128].astype(q_ref.dtype)
        qr = q[:, c0 + 128:c0 + 256]
        if rope:
            qr = _rope(qr, cos, sin)
        q_ref[:, c0 + 128:c0 + 256] = qr.astype(q_ref.dtype)


def _proj_l1(x, g, sc, sh, w, gq, wq, gkv, tables, tm):
    t = x.shape[0]
    groups = sc.shape[0]
    per = (t // tm) // groups
    row = lambda i: (i, 0)
    grp = lambda i: (i // per, 0, 0)
    fixed = lambda i: (0, 0)
    in_specs = [pl.BlockSpec((tm, D_MODEL), row),
                pl.BlockSpec((1, D_MODEL), fixed),
                pl.BlockSpec((1, 1, D_MODEL), grp),
                pl.BlockSpec((1, 1, D_MODEL), grp),
                pl.BlockSpec(w.shape, fixed),
                pl.BlockSpec((1, C_Q_RANK), fixed),
                pl.BlockSpec(wq.shape, fixed),
                pl.BlockSpec((1, C_KV_RANK), fixed)]
    args = [x, g, sc, sh, w, gq, wq, gkv]
    rope = tables is not None
    if rope:
        n = tables[0].shape[0]
        pos = lambda i: (i % (n // tm), 0)
        in_specs += [pl.BlockSpec((tm, 128), pos)] * 2
        args += list(tables)
    return pl.pallas_call(
        functools.partial(_p1_kernel, rope=rope),
        out_shape=(jax.ShapeDtypeStruct((t, C_HEADS * QCAT), BF16),
                   jax.ShapeDtypeStruct((t, C_KV_RANK), F32),
                   jax.ShapeDtypeStruct((t, C_ROPE), F32)),
        grid=(t // tm,),
        in_specs=in_specs,
        out_specs=(pl.BlockSpec((tm, C_HEADS * QCAT), row),
                   pl.BlockSpec((tm, C_KV_RANK), row),
                   pl.BlockSpec((tm, C_ROPE), row)),
        compiler_params=_cparams(("arbitrary",)),
        name="proj_l1",
    )(*args)


def _attn_kernel(*refs, n_parts, scale):
    q_ref = refs[0]
    kv_parts = refs[1:1 + 2 * n_parts]
    w_ref, o_ref, k_sc, v_sc = refs[1 + 2 * n_parts:]

    @pl.when(pl.program_id(1) == 0)
    def _():
        off = 0
        for p in range(n_parts):
            ckv_ref, kr_ref = kv_parts[2 * p], kv_parts[2 * p + 1]
            nk = ckv_ref.shape[0]
            kv = _dot(ckv_ref[...].astype(BF16), w_ref[...])
            kr = kr_ref[...].astype(BF16)
            kr2 = jnp.concatenate([kr, kr], axis=1)
            for hd in range(C_HEADS):
                k_sc[hd, off:off + nk, 0:128] = kv[:, hd * 128:(hd + 1) * 128].astype(BF16)
                k_sc[hd, off:off + nk, 128:256] = kr2
            v_sc[off:off + nk, :] = kv[:, C_HEADS * C_NOPE:].astype(BF16)
            off += nk

    for hd in range(C_HEADS):
        s = _dot_nt(q_ref[:, hd * QCAT:(hd + 1) * QCAT], k_sc[hd]) * scale
        p = jnp.exp(s - jnp.max(s, axis=1, keepdims=True))
        l = jnp.sum(p, axis=1, keepdims=True)
        o = _dot(p.astype(BF16), v_sc[:, hd * C_DV:(hd + 1) * C_DV])
        o_ref[:, hd * C_DV:(hd + 1) * C_DV] = (o / l).astype(o_ref.dtype)


def _attention(q, kv_parts, w_kvb, tq):
    bsz, n, _ = q.shape
    nk = sum(p[0].shape[1] for p in kv_parts)
    in_specs = [pl.BlockSpec((None, tq, C_HEADS * QCAT), lambda b, i: (b, i, 0))]
    args = [q]
    for ckv, kr in kv_parts:
        in_specs.append(pl.BlockSpec((None, ckv.shape[1], C_KV_RANK), lambda b, i: (b, 0, 0)))
        in_specs.append(pl.BlockSpec((None, kr.shape[1], C_ROPE), lambda b, i: (b, 0, 0)))
        args += [ckv, kr]
    in_specs.append(pl.BlockSpec(w_kvb.shape, lambda b, i: (0, 0)))
    args.append(w_kvb)
    return pl.pallas_call(
        functools.partial(_attn_kernel, n_parts=len(kv_parts),
                          scale=(C_NOPE + C_ROPE) ** -0.5),
        out_shape=jax.ShapeDtypeStruct((bsz, n, C_HEADS * C_DV), BF16),
        grid=(bsz, n // tq),
        in_specs=in_specs,
        out_specs=pl.BlockSpec((None, tq, C_HEADS * C_DV), lambda b, i: (b, i, 0)),
        scratch_shapes=[pltpu.VMEM((C_HEADS, nk, QCAT), BF16),
                        pltpu.VMEM((nk, C_HEADS * C_DV), BF16)],
        compiler_params=_cparams(("arbitrary", "arbitrary")),
        name="mla_attention",
    )(*args)


def _l0_w_in_layout(w_in):
    sizes = (256, 256, 512, 512, 32, 256, 256, 512, 512, 16)
    offs = np.concatenate([[0], np.cumsum(sizes)])
    seg = [w_in[:, offs[i]:offs[i + 1]] for i in range(len(sizes))]
    pad = jnp.zeros((w_in.shape[0], ZG_W - 48), w_in.dtype)
    cols = seg[0:4] + seg[5:9] + [seg[4], seg[9], pad]
    return jnp.concatenate(cols, axis=1).astype(BF16)


def _l1_wq_layout(w_qb):
    blocks = []
    zero = jnp.zeros((w_qb.shape[0], C_ROPE), w_qb.dtype)
    for hd in range(C_HEADS):
        base = hd * (C_NOPE + C_ROPE)
        nope = w_qb[:, base:base + C_NOPE]
        rope = w_qb[:, base + C_NOPE:base + C_NOPE + C_ROPE]
        blocks += [nope, rope, zero] if hd % 2 == 0 else [nope, zero, rope]
    return jnp.concatenate(blocks, axis=1).astype(BF16)


def _l1_wkvb_layout(w_kvb):
    w = w_kvb.reshape(C_KV_RANK, C_HEADS, C_NOPE + C_DV)
    k = w[:, :, :C_NOPE].reshape(C_KV_RANK, C_HEADS * C_NOPE)
    v = w[:, :, C_NOPE:].reshape(C_KV_RANK, C_HEADS * C_DV)
    return jnp.concatenate([k, v], axis=1).astype(BF16)


def _rope_tables(n):
    pos = jnp.arange(n)
    row = (pos // GRID_W).astype(F32)
    col = (pos % GRID_W).astype(F32)
    half = C_ROPE // 2
    inv = 1.0 / (ROPE_THETA ** (jnp.arange(0, half, 2, dtype=F32) / half))
    ang_r = row[:, None] * inv[None, :]
    ang_c = col[:, None] * inv[None, :]
    cos64 = jnp.concatenate([jnp.cos(ang_r)] * 2 + [jnp.cos(ang_c)] * 2, axis=1)
    sin64 = jnp.concatenate([-jnp.sin(ang_r), jnp.sin(ang_r), -jnp.sin(ang_c), jnp.sin(ang_c)], axis=1)
    return jnp.concatenate([cos64] * 2, axis=1), jnp.concatenate([sin64] * 2, axis=1)


def _gate_weight(w, first_row):
    return jnp.zeros((ZG_W, HK), F32).at[first_row:first_row + A_GATE_RANK].set(w)


def _state_t(s):
    b = s.shape[0]
    return s.transpose(0, 3, 1, 2).reshape(b, 128, HK)


def _n_rows(n_fwd, n_bwd):
    b = n_fwd.shape[0]
    rows = jnp.zeros((b, 8, HK), F32)
    return rows.at[:, 0, :].set(n_fwd.reshape(b, HK)).at[:, 1, :].set(n_bwd.reshape(b, HK))


def _m_rows(m_fwd, m_bwd):
    m = jnp.concatenate([m_fwd, m_bwd], axis=1)
    return jnp.broadcast_to(m[:, :, None], (m.shape[0], 8, 128))


def kernel(x_prompt, x_sample, state_l0_gla_fwd, state_l0_gla_bwd, state_l0_mlstm_c_fwd, state_l0_mlstm_n_fwd, state_l0_mlstm_m_fwd, state_l0_mlstm_c_bwd, state_l0_mlstm_n_bwd, state_l0_mlstm_m_bwd, cache_l1_ckv, cache_l1_krope, c, c_ctx, l0_w_mod, l0_b_mod, l0_g_pre_mix, l0_g_post_mix, l0_g_pre_ffn, l0_g_post_ffn, l0_w_in, l0_gla_w_gate_f, l0_gla_b_gate_f, l0_gla_w_gate_b, l0_gla_b_gate_b, l0_gla_g_norm, l0_mlstm_b_gates, l0_mlstm_g_norm, l0_w_out, l0_ffn_w_up, l0_ffn_conv_w, l0_ffn_conv_b, l0_ffn_w_down, l1_w_mod, l1_b_mod, l1_g_pre_mix, l1_g_post_mix, l1_g_pre_ffn, l1_g_post_ffn, l1_w_in, l1_g_q_norm, l1_w_qb, l1_g_kv_norm, l1_w_kvb, l1_w_out, l1_ffn_w_up, l1_ffn_conv_w, l1_ffn_conv_b, l1_ffn_w_down):
    bp, sp, d = x_prompt.shape
    bs, ss, _ = x_sample.shape
    row = lambda v: v.reshape(1, -1)

    cond8 = jnp.zeros((8, d), F32).at[0].set(c_ctx).at[1:1 + bs].set(c)
    mods = (_modulation(cond8, l0_w_mod, l0_b_mod), _modulation(cond8, l1_w_mod, l1_b_mod))

    def mod_rows(layer, which, lo, hi):
        m = mods[layer][lo:hi, which * d:(which + 1) * d]
        return m.reshape(hi - lo, 1, d)

    paths = {
        "ctx": dict(x=x_prompt.reshape(bp * sp, d), b=bp, n=sp, rows=(0, 1), tm=512),
        "lat": dict(x=x_sample.reshape(bs * ss, d), b=bs, n=ss, rows=(1, 1 + bs), tm=512),
    }

    w_in0 = _l0_w_in_layout(l0_w_in)
    wgf = _gate_weight(l0_gla_w_gate_f, 0)
    wgb = _gate_weight(l0_gla_w_gate_b, A_GATE_RANK)
    bg = jnp.zeros((1, ZG_W), F32).at[0, GATE_COL:GATE_COL + 16].set(l0_mlstm_b_gates)
    w_out0 = l0_w_out.astype(BF16)
    w_in1 = jnp.pad(l1_w_in, ((0, 0), (0, L1_IN_PAD - l1_w_in.shape[1]))).astype(BF16)
    w_q1 = _l1_wq_layout(l1_w_qb)
    w_kvb1 = _l1_wkvb_layout(l1_w_kvb)
    w_out1 = l1_w_out.astype(BF16)
    ffn = ((l0_ffn_w_up.astype(BF16), l0_ffn_conv_w, row(l0_ffn_conv_b), l0_ffn_w_down.astype(BF16)),
           (l1_ffn_w_up.astype(BF16), l1_ffn_conv_w, row(l1_ffn_conv_b), l1_ffn_w_down.astype(BF16)))
    norms = ((row(l0_g_pre_mix), row(l0_g_post_mix), row(l0_g_pre_ffn), row(l0_g_post_ffn)),
             (row(l1_g_pre_mix), row(l1_g_post_mix), row(l1_g_pre_ffn), row(l1_g_post_ffn)))

    gla_init = (_state_t(state_l0_gla_fwd), _state_t(state_l0_gla_bwd))
    mlstm_init = (_state_t(state_l0_mlstm_c_fwd), _state_t(state_l0_mlstm_c_bwd),
                  _n_rows(state_l0_mlstm_n_fwd, state_l0_mlstm_n_bwd),
                  _m_rows(state_l0_mlstm_m_fwd, state_l0_mlstm_m_bwd))
    tables = _rope_tables(ss)

    results = {}
    for name, p in paths.items():
        x, b, n, (lo, hi), tm = p["x"], p["b"], p["n"], p["rows"], p["tm"]
        is_ctx = name == "ctx"
        ffn_tm = tm if is_ctx else n
        sh1, sc1, gt1, sh2, sc2, gt2 = (mod_rows(0, i, lo, hi) for i in range(N_MOD))
        g_pre, g_post, g_pre2, g_post2 = norms[0]

        za, zb, zg = _proj_l0(x, g_pre, sc1, sh1, w_in0, tm)
        za, zb, zg = (t.reshape(b, n, -1) for t in (za, zb, zg))
        oa, sa_f, sa_b = _gla(za, zg, wgf, wgb, row(l0_gla_b_gate_f), row(l0_gla_b_gate_b),
                              row(l0_gla_g_norm), None if is_ctx else gla_init)
        om, c_f, c_b, n_rows, m_rows = _mlstm(zb, zg, bg, row(l0_mlstm_g_norm),
                                              None if is_ctx else mlstm_init)
        x = _mix_ffn([oa.reshape(b * n, HV), om.reshape(b * n, HV)], w_out0, x,
                     gt1, g_post, g_pre2, sc2, sh2, *ffn[0], gt2, g_post2, ffn_tm, n)

        sh1, sc1, gt1, sh2, sc2, gt2 = (mod_rows(1, i, lo, hi) for i in range(N_MOD))
        g_pre, g_post, g_pre2, g_post2 = norms[1]
        q, ckv, kr = _proj_l1(x, g_pre, sc1, sh1, w_in1, row(l1_g_q_norm), w_q1,
                              row(l1_g_kv_norm), None if is_ctx else tables, tm)
        ckv3 = ckv.reshape(b, n, C_KV_RANK)
        kr3 = kr.reshape(b, n, C_ROPE)
        kv_parts = [(ckv3, kr3)] if is_ctx else [(cache_l1_ckv, cache_l1_krope), (ckv3, kr3)]
        o = _attention(q.reshape(b, n, -1), kv_parts, w_kvb1, 256)
        x = _mix_ffn([o.reshape(b * n, C_HEADS * C_DV)], w_out1, x,
                     gt1, g_post, g_pre2, sc2, sh2, *ffn[1], gt2, g_post2, ffn_tm, n)

        results[name] = dict(y=x.reshape(b, n, d), gla=(sa_f, sa_b), c=(c_f, c_b),
                             n=n_rows, m=m_rows, ckv=ckv3, kr=kr3)

    r = results["ctx"]
    n_fwd = r["n"][:, 0, :].reshape(bp, B_HEADS, B_DK)
    n_bwd = r["n"][:, 1, :].reshape(bp, B_HEADS, B_DK)
    m_fwd = r["m"][:, 0:B_HEADS, 0]
    m_bwd = r["m"][:, B_HEADS:2 * B_HEADS, 0]
    return (r["y"], results["lat"]["y"], r["gla"][0], r["gla"][1],
            r["c"][0], n_fwd, m_fwd, r["c"][1], n_bwd, m_bwd, r["ckv"], r["kr"])
```

```python
import functools

import numpy as np
import jax
import jax.numpy as jnp
from jax import lax
from jax.experimental import pallas as pl
from jax.experimental.pallas import tpu as pltpu

F32 = jnp.float32
BF16 = jnp.bfloat16

D_MODEL = 1024
EPS = 1e-6
N_MOD = 6
GRID_W = 64
ROPE_THETA = 10000.0

A_HEADS, A_DK, A_DV, A_GATE_RANK, A_GATE_TEMP = 4, 64, 128, 16, 16.0
B_HEADS, B_DK, B_DV = 4, 64, 128
C_HEADS, C_Q_RANK, C_KV_RANK, C_NOPE, C_ROPE, C_DV = 8, 384, 256, 128, 64, 128
D_FF = 2816
CONV_W = 3

HEADS = 4
HK = HEADS * 64
HV = HEADS * 128
ZA_W = 1536
ZG_W = 128
GATE_COL = 32
RCHUNK = 128
L1_IN_PAD = 768
QCAT = 256

VMEM_LIMIT = 56 * 1024 * 1024


def _cparams(sem):
    return pltpu.CompilerParams(dimension_semantics=sem, vmem_limit_bytes=VMEM_LIMIT)


def _dot(a, b):
    return jnp.dot(a, b, preferred_element_type=F32)


def _dot_nt(a, b):
    return lax.dot_general(a, b, (((1,), (1,)), ((), ())), preferred_element_type=F32)


def _dot_tn(a, b):
    return lax.dot_general(a, b, (((0,), (0,)), ((), ())), preferred_element_type=F32)


def _split3(x):
    hi = x.astype(BF16)
    r1 = x - hi.astype(F32)
    mid = r1.astype(BF16)
    lo = (r1 - mid.astype(F32)).astype(BF16)
    return hi, mid, lo


def _tri_dot(tri, x):
    hi, mid, lo = _split3(x)
    return _dot(tri, hi) + _dot(tri, mid) + _dot(tri, lo)


def _dot_tri(x, tri):
    hi, mid, lo = _split3(x)
    return _dot(hi, tri) + _dot(mid, tri) + _dot(lo, tri)


def _dot_hl(a, b):
    ah = a.astype(BF16)
    al = (a - ah.astype(F32)).astype(BF16)
    bh = b.astype(BF16)
    bl = (b - bh.astype(F32)).astype(BF16)
    return _dot(ah, bh) + _dot(ah, bl) + _dot(al, bh)


def _rms(x, g):
    return x * lax.rsqrt(jnp.mean(x * x, axis=-1, keepdims=True) + EPS) * g


def _sigmoid(x):
    return 1.0 / (1.0 + jnp.exp(-x))


def _silu(x):
    return x * _sigmoid(x)


def _log_sigmoid(x):
    return jnp.minimum(x, 0.0) - jnp.log(1.0 + jnp.exp(-jnp.abs(x)))


def _tri_masks(n):
    r = lax.broadcasted_iota(jnp.int32, (n, n), 0)
    c = lax.broadcasted_iota(jnp.int32, (n, n), 1)
    return c <= r, c >= r


def _mod_kernel(cond_ref, w_ref, b_ref, o_ref):
    s = _silu(cond_ref[...])
    o_ref[...] = _dot(s.astype(BF16), w_ref[...].astype(BF16)) + b_ref[...]


def _modulation(cond8, w_mod, b_mod):
    d = D_MODEL
    out = pl.pallas_call(
        _mod_kernel,
        out_shape=jax.ShapeDtypeStruct((N_MOD, 8, d), F32),
        grid=(N_MOD,),
        in_specs=[pl.BlockSpec((8, d), lambda j: (0, 0)),
                  pl.BlockSpec((d, d), lambda j: (0, j)),
                  pl.BlockSpec((1, d), lambda j: (0, j))],
        out_specs=pl.BlockSpec((None, 8, d), lambda j: (j, 0, 0)),
        compiler_params=_cparams(("arbitrary",)),
        name="modulation",
    )(cond8, w_mod, b_mod.reshape(1, N_MOD * d))
    return out.reshape(N_MOD, 8, 1, d)


MOD_SHIFT1, MOD_SCALE1, MOD_GATE1, MOD_SHIFT2, MOD_SCALE2, MOD_GATE2 = range(N_MOD)


def _mod_spec(which, first_row, per):
    return pl.BlockSpec((None, None, 1, D_MODEL), lambda i: (which, first_row + i // per, 0, 0))


def _p0_kernel(x_ref, g_ref, sc_ref, sh_ref, w_ref, za_ref, zb_ref, zg_ref):
    h = _rms(x_ref[...], g_ref[...]) * (1.0 + sc_ref[...]) + sh_ref[...]
    hb = h.astype(BF16)
    za_ref[...] = _dot(hb, w_ref[:, 0:ZA_W])
    zb_ref[...] = _dot(hb, w_ref[:, ZA_W:2 * ZA_W])
    zg_ref[...] = _dot(hb, w_ref[:, 2 * ZA_W:2 * ZA_W + ZG_W])


def _proj_l0(x, g, mod, first_row, groups, w, tm):
    t = x.shape[0]
    per = (t // tm) // groups
    nw = w.shape[1]
    row = lambda i: (i, 0)
    fixed = lambda i: (0, 0)
    return pl.pallas_call(
        _p0_kernel,
        out_shape=(jax.ShapeDtypeStruct((t, ZA_W), F32),
                   jax.ShapeDtypeStruct((t, ZA_W), F32),
                   jax.ShapeDtypeStruct((t, ZG_W), F32)),
        grid=(t // tm,),
        in_specs=[pl.BlockSpec((tm, D_MODEL), row),
                  pl.BlockSpec((1, D_MODEL), fixed),
                  _mod_spec(MOD_SCALE1, first_row, per),
                  _mod_spec(MOD_SHIFT1, first_row, per),
                  pl.BlockSpec((D_MODEL, nw), fixed)],
        out_specs=(pl.BlockSpec((tm, ZA_W), row),
                   pl.BlockSpec((tm, ZA_W), row),
                   pl.BlockSpec((tm, ZG_W), row)),
        compiler_params=_cparams(("arbitrary",)),
        name="proj_l0",
    )(x, g, mod, mod, w)


def _head_lane_mask(shape, h):
    lane = lax.broadcasted_iota(jnp.int32, shape, len(shape) - 1)
    return (lane >= h * 64) & (lane < (h + 1) * 64)


def _head_row(h):
    return jnp.where(_head_lane_mask((1, HK), h), 1.0, 0.0).astype(BF16)


def _gla_prepare(za_ref, la_ref, r0, rev, tri):
    L = RCHUNK
    rows = pl.ds(r0, L)
    b = _tri_dot(tri, la_ref[rows, :])
    mid = L // 2
    ref = b[mid:mid + 1, :]
    b_end = b[0:1, :] if rev else b[L - 1:L, :]
    q = za_ref[rows, 0:HK]
    k = za_ref[rows, HK:2 * HK]
    v = za_ref[rows, 2 * HK:2 * HK + HV].astype(BF16)
    qe = (q * (jnp.exp(b - ref) * (A_DK ** -0.5))).astype(BF16)
    ke = (k * jnp.exp(ref - b)).astype(BF16)
    qm = jnp.concatenate([qe * _head_row(h) for h in range(HEADS)], axis=0)
    rt = _dot_tn(v, ke)
    upd = None
    for h in range(HEADS):
        blk = rt[h * 128:(h + 1) * 128, :]
        blk = jnp.where(_head_lane_mask(blk.shape, h), blk, 0.0)
        upd = blk if upd is None else upd + blk
    return dict(rows=rows, qm=qm, ke=ke, v=v, upd=upd * jnp.exp(b_end - ref),
                decay=jnp.exp(b_end), eref=jnp.exp(ref))


def _gla_apply(items):
    L = RCHUNK
    sts = [st_ref[...] for _, st_ref, _, _ in items]
    o1s = []
    for (p, _, _, _), st in zip(items, sts):
        rhs = jnp.concatenate([p["ke"], (st * p["eref"]).astype(BF16)], axis=0)
        o1s.append(_dot_nt(p["qm"], rhs))
    for h in range(HEADS):
        for (p, _, o_ref, mask), o1 in zip(items, o1s):
            blk = o1[h * L:(h + 1) * L, :]
            sc = jnp.where(mask, blk[:, 0:L], 0.0).astype(BF16)
            o_ref[p["rows"], h * 128:(h + 1) * 128] = (
                blk[:, L:L + 128] + _dot(sc, p["v"][:, h * 128:(h + 1) * 128]))
    for (p, st_ref, _, _), st in zip(items, sts):
        st_ref[...] = st * p["decay"] + p["upd"]


def _gla_kernel(*refs, n, has_init):
    if has_init:
        (za_ref, zg_ref, wgf_ref, wgb_ref, bgf_ref, bgb_ref, gn_ref, sf0_ref, sb0_ref,
         o_ref, sf_ref, sb_ref, laf_ref, lab_ref, of_ref, ob_ref, stf_ref, stb_ref) = refs
    else:
        (za_ref, zg_ref, wgf_ref, wgb_ref, bgf_ref, bgb_ref, gn_ref,
         o_ref, sf_ref, sb_ref, laf_ref, lab_ref, of_ref, ob_ref, stf_ref, stb_ref) = refs
    L = RCHUNK
    nc = n // L
    zg = zg_ref[...]
    xf = _dot_hl(zg, wgf_ref[...]) + bgf_ref[...]
    xb = _dot_hl(zg, wgb_ref[...]) + bgb_ref[...]
    laf_ref[...] = _log_sigmoid(xf) / A_GATE_TEMP
    lab_ref[...] = _log_sigmoid(xb) / A_GATE_TEMP
    if has_init:
        stf_ref[...] = sf0_ref[...]
        stb_ref[...] = sb0_ref[...]
    else:
        stf_ref[...] = jnp.zeros_like(stf_ref)
        stb_ref[...] = jnp.zeros_like(stb_ref)

    m_lo, m_up = _tri_masks(L)
    tri_lo = jnp.where(m_lo, 1.0, 0.0).astype(BF16)
    tri_up = jnp.where(m_up, 1.0, 0.0).astype(BF16)

    def scan_body(g, carry):
        waves = []
        for w in range(2):
            c = 2 * g + w
            rf = pl.multiple_of(c * L, L)
            rb = pl.multiple_of((nc - 1 - c) * L, L)
            waves.append((_gla_prepare(za_ref, laf_ref, rf, False, tri_lo),
                          _gla_prepare(za_ref, lab_ref, rb, True, tri_up)))
        for pf, pb in waves:
            _gla_apply([(pf, stf_ref, of_ref, m_lo), (pb, stb_ref, ob_ref, m_up)])
        return carry

    if nc == 2:
        scan_body(0, 0)
    else:
        lax.fori_loop(0, nc // 2, scan_body, 0)

    gn = gn_ref[...]

    def out_body(c, carry):
        rows = pl.ds(pl.multiple_of(c * L, L), L)
        for h in range(HEADS):
            cols = slice(h * 128, (h + 1) * 128)
            o = of_ref[rows, cols] + ob_ref[rows, cols]
            gate = za_ref[rows, 2 * HK + HV + h * 128:2 * HK + HV + (h + 1) * 128]
            o_ref[rows, cols] = (_rms(o, gn) * _silu(gate)).astype(o_ref.dtype)
        return carry

    lax.fori_loop(0, nc, out_body, 0)

    stf_t = stf_ref[...].T
    stb_t = stb_ref[...].T
    for h in range(HEADS):
        sf_ref[h] = stf_t[h * 64:(h + 1) * 64, :]
        sb_ref[h] = stb_t[h * 64:(h + 1) * 64, :]


def _gla(za, zg, wgf, wgb, bgf, bgb, gn, init):
    bsz, n, _ = za.shape
    has_init = init is not None
    per_b3 = lambda b: (b, 0, 0)
    per_b4 = lambda b: (b, 0, 0, 0)
    fixed = lambda b: (0, 0)
    in_specs = [pl.BlockSpec((None, n, ZA_W), per_b3),
                pl.BlockSpec((None, n, ZG_W), per_b3),
                pl.BlockSpec((ZG_W, HK), fixed),
                pl.BlockSpec((ZG_W, HK), fixed),
                pl.BlockSpec((1, HK), fixed),
                pl.BlockSpec((1, HK), fixed),
                pl.BlockSpec((1, 128), fixed)]
    args = [za, zg, wgf, wgb, bgf, bgb, gn]
    if has_init:
        in_specs += [pl.BlockSpec((None, 128, HK), per_b3)] * 2
        args += list(init)
    return pl.pallas_call(
        functools.partial(_gla_kernel, n=n, has_init=has_init),
        out_shape=(jax.ShapeDtypeStruct((bsz, n, HV), BF16),
                   jax.ShapeDtypeStruct((bsz, HEADS, 64, 128), F32),
                   jax.ShapeDtypeStruct((bsz, HEADS, 64, 128), F32)),
        grid=(bsz,),
        in_specs=in_specs,
        out_specs=(pl.BlockSpec((None, n, HV), per_b3),
                   pl.BlockSpec((None, HEADS, 64, 128), per_b4),
                   pl.BlockSpec((None, HEADS, 64, 128), per_b4)),
        scratch_shapes=[pltpu.VMEM((n, HK), F32), pltpu.VMEM((n, HK), F32),
                        pltpu.VMEM((n, HV), F32), pltpu.VMEM((n, HV), F32),
                        pltpu.VMEM((128, HK), F32), pltpu.VMEM((128, HK), F32)],
        compiler_params=_cparams(("arbitrary",)),
        name="gla",
    )(*args)


T_M, T_MOUT, T_G, T_SRCW, T_MPREV, T_CARRY, T_USED = 0, 8, 16, 24, 32, 40, 48


def _lane_scan(x, op, fill, is_fwd, pos, n):
    k = 1
    while k < RCHUNK:
        pf = jnp.where(pos >= k, pltpu.roll(x, k, axis=1), fill)
        sf = jnp.where(pos < RCHUNK - k, pltpu.roll(x, n - k, axis=1), fill)
        x = op(x, jnp.where(is_fwd, pf, sf))
        k *= 2
    return x


def _mlstm_gate_tables(g, m0, gt_ref, tab_ref, col_ref, n):
    L = RCHUNK
    nc = n // L
    for c in range(nc):
        gt_ref[:, c * L:(c + 1) * L] = g[c * L:(c + 1) * L, :].T
        tab_ref[c, T_USED:, :] = jnp.zeros((ZG_W - T_USED, L), F32)
    gi = gt_ref[GATE_COL:GATE_COL + 8, :]
    lf = gt_ref[GATE_COL + 8:GATE_COL + 16, :]
    is_fwd = lax.broadcasted_iota(jnp.int32, (8, n), 0) < HEADS
    pos = lax.broadcasted_iota(jnp.int32, (8, n), 1) % L
    b = _lane_scan(lf, jnp.add, 0.0, is_fwd, pos, n)
    gg = gi - b
    cmax = _lane_scan(gg, jnp.maximum, -jnp.inf, is_fwd, pos, n)
    fwd_l = lax.broadcasted_iota(jnp.int32, (8, L), 0) < HEADS
    mp = m0
    for j in range(nc):
        cf, cb = j, nc - 1 - j

        def pick(x):
            return jnp.where(fwd_l, x[:, cf * L:(cf + 1) * L], x[:, cb * L:(cb + 1) * L])

        gj, bj, cj, lj = pick(gg), pick(b), pick(cmax), pick(lf)
        mx = jnp.maximum(mp, jnp.max(gj, axis=1, keepdims=True))
        mj = jnp.maximum(cj, mp)
        blocks = (mj, bj + mj, gj, jnp.exp(gj - mx),
                  jnp.broadcast_to(mp, (8, L)), jnp.broadcast_to(jnp.exp(mp - mx), (8, L)))
        for t, val in enumerate(blocks):
            tab_ref[cf, 8 * t:8 * t + HEADS, :] = val[0:HEADS]
            tab_ref[cb, 8 * t + HEADS:8 * t + 8, :] = val[HEADS:8]
        mp = jnp.sum(lj, axis=1, keepdims=True) + mx
    for c in range(nc):
        col_ref[c * L:(c + 1) * L, :] = tab_ref[c].T
    return mp


def _mlstm_prepare(zb_ref, tab_ref, col_ref, c, d, mask):
    L = RCHUNK
    rows = pl.ds(pl.multiple_of(c * L, L), L)
    qb = zb_ref[rows, 0:HK].astype(BF16)
    kb = (zb_ref[rows, HK:2 * HK] * (B_DK ** -0.5)).astype(BF16)
    v = zb_ref[rows, 2 * HK:2 * HK + HV]
    vb = v.astype(BF16)
    vt = v.T
    qm = jnp.concatenate([qb * _head_row(h) for h in range(HEADS)], axis=0)
    s = _dot_nt(qm, kb)
    ones = jnp.ones((L, 128), BF16)
    pv, w_inter, e_negm = [], [], []
    upd = None
    carry = None
    for h in range(HEADS):
        r = 4 * d + h
        m_rep = jnp.broadcast_to(col_ref[rows, T_M + r:T_M + r + 1], (L, 128))
        mout_rep = jnp.broadcast_to(col_ref[rows, T_MOUT + r:T_MOUT + r + 1], (L, 128))
        g_row = tab_ref[c, T_G + r:T_G + r + 1, :]
        w = jnp.exp(jnp.where(mask, g_row - m_rep, -jnp.inf))
        qk = (s[h * L:(h + 1) * L, :] * w).astype(BF16)
        v1 = jnp.concatenate([vb[:, h * 128:(h + 1) * 128], ones], axis=1)
        pv.append(_dot(qk, v1))
        w_inter.append(jnp.exp(tab_ref[c, T_MPREV + r:T_MPREV + r + 1, :] - m_rep))
        e_negm.append(jnp.exp(-mout_rep))
        sw = tab_ref[c, T_SRCW + r:T_SRCW + r + 1, :]
        lhs = jnp.concatenate([vt[h * 128:(h + 1) * 128, :] * sw, jnp.broadcast_to(sw, (128, L))],
                              axis=0).astype(BF16)
        blk = _dot(lhs, kb)
        blk = jnp.where(_head_lane_mask(blk.shape, h), blk, 0.0)
        upd = blk if upd is None else upd + blk
        cr = tab_ref[c, T_CARRY + r:T_CARRY + r + 1, :]
        cr = jnp.where(_head_lane_mask((1, HK), h), jnp.concatenate([cr, cr], axis=1), 0.0)
        carry = cr if carry is None else carry + cr
    return dict(rows=rows, qm=qm, pv=pv, w_inter=w_inter, e_negm=e_negm, upd=upd, carry=carry)


def _mlstm_apply(items):
    L = RCHUNK
    sts = [st_ref[...] for _, st_ref, _ in items]
    a_all = [_dot_nt(p["qm"], st.astype(BF16)) for (p, _, _), st in zip(items, sts)]
    for h in range(HEADS):
        for (p, _, h_ref), a in zip(items, a_all):
            blk = a[h * L:(h + 1) * L, :]
            pv, w, e = p["pv"][h], p["w_inter"][h], p["e_negm"][h]
            num = w * blk[:, 0:128] + pv[:, 0:128]
            den = w * blk[:, 128:256] + pv[:, 128:256]
            h_ref[p["rows"], h * 128:(h + 1) * 128] = num / jnp.maximum(jnp.abs(den), e)
    for (p, st_ref, _), st in zip(items, sts):
        st_ref[...] = p["carry"] * st + p["upd"]


def _mlstm_kernel(*refs, n, has_init):
    if has_init:
        (zb_ref, zg_ref, bg_ref, gn_ref, cf0_ref, cb0_ref, n0_ref, m0_ref,
         o_ref, cf_ref, cb_ref, n_out_ref, m_out_ref,
         gt_ref, tab_ref, col_ref, hf_ref, hb_ref, stf_ref, stb_ref) = refs
    else:
        (zb_ref, zg_ref, bg_ref, gn_ref,
         o_ref, cf_ref, cb_ref, n_out_ref, m_out_ref,
         gt_ref, tab_ref, col_ref, hf_ref, hb_ref, stf_ref, stb_ref) = refs
    L = RCHUNK
    nc = n // L
    g = zg_ref[...] + bg_ref[...]
    lane = lax.broadcasted_iota(jnp.int32, g.shape, 1)
    is_forget = (lane >= GATE_COL + 8) & (lane < GATE_COL + 16)
    g = jnp.where(is_forget, _log_sigmoid(g), g)
    if has_init:
        m0 = m0_ref[:, 0:1]
        stf_ref[...] = jnp.concatenate(
            [cf0_ref[...], jnp.broadcast_to(n0_ref[0:1, :], (128, HK))], axis=0)
        stb_ref[...] = jnp.concatenate(
            [cb0_ref[...], jnp.broadcast_to(n0_ref[1:2, :], (128, HK))], axis=0)
    else:
        m0 = jnp.zeros((8, 1), F32)
        stf_ref[...] = jnp.zeros_like(stf_ref)
        stb_ref[...] = jnp.zeros_like(stb_ref)
    m_fin = _mlstm_gate_tables(g, m0, gt_ref, tab_ref, col_ref, n)

    m_lo, m_up = _tri_masks(L)

    def scan_body(gidx, carry):
        waves = []
        for w in range(2):
            c = 2 * gidx + w
            waves.append((_mlstm_prepare(zb_ref, tab_ref, col_ref, c, 0, m_lo),
                          _mlstm_prepare(zb_ref, tab_ref, col_ref, nc - 1 - c, 1, m_up)))
        for pf, pb in waves:
            _mlstm_apply([(pf, stf_ref, hf_ref), (pb, stb_ref, hb_ref)])
        return carry

    if nc == 2:
        scan_body(0, 0)
    else:
        lax.fori_loop(0, nc // 2, scan_body, 0)

    gn = gn_ref[...]

    def out_body(c, carry):
        rows = pl.ds(pl.multiple_of(c * L, L), L)
        for h in range(HEADS):
            cols = slice(h * 128, (h + 1) * 128)
            hh = hf_ref[rows, cols] + hb_ref[rows, cols]
            gate = zb_ref[rows, 2 * HK + HV + h * 128:2 * HK + HV + (h + 1) * 128]
            o_ref[rows, cols] = (_sigmoid(gate) * _rms(hh, gn)).astype(o_ref.dtype)
        return carry

    lax.fori_loop(0, nc, out_body, 0)

    stf = stf_ref[...]
    stb = stb_ref[...]
    ctf_t = stf[0:128, :].T
    ctb_t = stb[0:128, :].T
    for h in range(HEADS):
        cf_ref[h] = ctf_t[h * 64:(h + 1) * 64, :]
        cb_ref[h] = ctb_t[h * 64:(h + 1) * 64, :]
    n_out_ref[...] = jnp.concatenate(
        [stf[128:129, :], stb[128:129, :], jnp.zeros((6, HK), F32)], axis=0)
    m_out_ref[...] = jnp.broadcast_to(m_fin, (8, 128))


def _mlstm(zb, zg, bg, gn, init):
    bsz, n, _ = zb.shape
    has_init = init is not None
    nc = n // RCHUNK
    per_b3 = lambda b: (b, 0, 0)
    per_b4 = lambda b: (b, 0, 0, 0)
    fixed = lambda b: (0, 0)
    in_specs = [pl.BlockSpec((None, n, ZA_W), per_b3),
                pl.BlockSpec((None, n, ZG_W), per_b3),
                pl.BlockSpec((1, ZG_W), fixed),
                pl.BlockSpec((1, 128), fixed)]
    args = [zb, zg, bg, gn]
    if has_init:
        in_specs += [pl.BlockSpec((None, 128, HK), per_b3)] * 2
        in_specs += [pl.BlockSpec((None, 8, HK), per_b3), pl.BlockSpec((None, 8, 128), per_b3)]
        args += list(init)
    return pl.pallas_call(
        functools.partial(_mlstm_kernel, n=n, has_init=has_init),
        out_shape=(jax.ShapeDtypeStruct((bsz, n, HV), BF16),
                   jax.ShapeDtypeStruct((bsz, HEADS, 64, 128), F32),
                   jax.ShapeDtypeStruct((bsz, HEADS, 64, 128), F32),
                   jax.ShapeDtypeStruct((bsz, 8, HK), F32),
                   jax.ShapeDtypeStruct((bsz, 8, 128), F32)),
        grid=(bsz,),
        in_specs=in_specs,
        out_specs=(pl.BlockSpec((None, n, HV), per_b3),
                   pl.BlockSpec((None, HEADS, 64, 128), per_b4),
                   pl.BlockSpec((None, HEADS, 64, 128), per_b4),
                   pl.BlockSpec((None, 8, HK), per_b3),
                   pl.BlockSpec((None, 8, 128), per_b3)),
        scratch_shapes=[pltpu.VMEM((ZG_W, n), F32), pltpu.VMEM((nc, ZG_W, RCHUNK), F32),
                        pltpu.VMEM((n, ZG_W), F32),
                        pltpu.VMEM((n, HV), F32), pltpu.VMEM((n, HV), F32),
                        pltpu.VMEM((2 * 128, HK), F32), pltpu.VMEM((2 * 128, HK), F32)],
        compiler_params=_cparams(("arbitrary",)),
        name="mlstm",
    )(*args)


def _mix_ffn_kernel(*refs, n_parts, seq, tf):
    parts = refs[:n_parts]
    (wo_ref, x_ref, gt1_ref, gpost1_ref, gpre2_ref, sc2_ref, sh2_ref,
     wu_ref, cw_ref, cb_ref, wd_ref, gt2_ref, gpost2_ref, o_ref, act_ref) = refs[n_parts:]
    tm = x_ref.shape[0]
    m = None
    off = 0
    for p in parts:
        kp = p.shape[1]
        t = _dot(p[...], wo_ref[off:off + kp, :])
        m = t if m is None else m + t
        off += kp
    x1 = x_ref[...] + gt1_ref[...] * _rms(m, gpost1_ref[...])
    o_ref[...] = x1
    hb = (_rms(x1, gpre2_ref[...]) * (1.0 + sc2_ref[...]) + sh2_ref[...]).astype(BF16)
    pos = lax.broadcasted_iota(jnp.int32, (tm, 1), 0) % seq
    first = pos == 0
    last = pos == seq - 1

    def conv(u, cols):
        prev = jnp.where(first, 0.0, pltpu.roll(u, 1, axis=0))
        nxt = jnp.where(last, 0.0, pltpu.roll(u, tm - 1, axis=0))
        return (cw_ref[0:1, cols] * prev + cw_ref[1:2, cols] * u + cw_ref[2:3, cols] * nxt
                + cb_ref[:, cols])

    for j in range(D_FF // tf):
        ca = slice(j * tf, (j + 1) * tf)
        cg = slice(D_FF + j * tf, D_FF + (j + 1) * tf)
        a = conv(_dot(hb, wu_ref[:, ca]), ca)
        g = conv(_dot(hb, wu_ref[:, cg]), cg)
        act_ref[:, ca] = (_silu(g) * a).astype(BF16)
    y = _dot(act_ref[...], wd_ref[...])
    o_ref[...] = o_ref[...] + gt2_ref[...] * _rms(y, gpost2_ref[...])


def _resident(shape):
    return pl.BlockSpec(shape, lambda *_: (0,) * len(shape), pipeline_mode=pl.Buffered(1))


def _mix_ffn(parts, w_out, x, mod, first_row, groups, gpost1, gpre2, w_up, conv_w, conv_b, w_down,
             gpost2, tm, seq, tf=256):
    t = x.shape[0]
    per = (t // tm) // groups
    row = lambda i: (i, 0)
    vec = _resident((1, D_MODEL))
    in_specs = [pl.BlockSpec((tm, p.shape[1]), row) for p in parts]
    in_specs += [_resident(w_out.shape), pl.BlockSpec((tm, D_MODEL), row),
                 _mod_spec(MOD_GATE1, first_row, per), vec, vec,
                 _mod_spec(MOD_SCALE2, first_row, per), _mod_spec(MOD_SHIFT2, first_row, per),
                 _resident(w_up.shape), _resident(conv_w.shape), _resident(conv_b.shape),
                 _resident(w_down.shape), _mod_spec(MOD_GATE2, first_row, per), vec]
    return pl.pallas_call(
        functools.partial(_mix_ffn_kernel, n_parts=len(parts), seq=seq, tf=tf),
        out_shape=jax.ShapeDtypeStruct((t, D_MODEL), F32),
        grid=(t // tm,),
        in_specs=in_specs,
        out_specs=pl.BlockSpec((tm, D_MODEL), row),
        scratch_shapes=[pltpu.VMEM((tm, D_FF), BF16)],
        compiler_params=_cparams(("arbitrary",)),
        name="mix_ffn",
    )(*parts, w_out, x, mod, gpost1, gpre2, mod, mod, w_up, conv_w, conv_b, w_down, mod, gpost2)


def _rope(x, cos, sin_signed):
    lane = lax.broadcasted_iota(jnp.int32, x.shape, 1)
    lower = (lane % 32) < 16
    partner = jnp.where(lower, pltpu.roll(x, 128 - 16, axis=1), pltpu.roll(x, 16, axis=1))
    return x * cos + partner * sin_signed


def _p1_kernel(*refs, rope):
    if rope:
        (x_ref, g_ref, sc_ref, sh_ref, w_ref, gq_ref, wq_ref, gkv_ref, cos_ref, sin_ref,
         q_ref, ckv_ref, kr_ref) = refs
    else:
        (x_ref, g_ref, sc_ref, sh_ref, w_ref, gq_ref, wq_ref, gkv_ref,
         q_ref, ckv_ref, kr_ref) = refs
    h = _rms(x_ref[...], g_ref[...]) * (1.0 + sc_ref[...]) + sh_ref[...]
    z = _dot(h.astype(BF16), w_ref[...])
    qa = _rms(z[:, 0:C_Q_RANK], gq_ref[...])
    q = _dot(qa.astype(BF16), wq_ref[...])
    ckv_ref[...] = _rms(z[:, C_Q_RANK:C_Q_RANK + C_KV_RANK], gkv_ref[...])
    kr = z[:, C_Q_RANK + C_KV_RANK:L1_IN_PAD]
    if rope:
        cos = cos_ref[...]
        sin = sin_ref[...]
        kr = _rope(kr, cos, sin)
    kr_ref[...] = kr[:, 0:C_ROPE]
    for hd in range(C_HEADS):
        c0 = hd * QCAT
        q_ref[:, c0:c0 + 128] = q[:, c0:c0 + 128].astype(q_ref.dtype)
        qr = q[:, c0 + 128:c0 + 256]
        if rope:
            qr = _rope(qr, cos, sin)
        q_ref[:, c0 + 128:c0 + 256] = qr.astype(q_ref.dtype)


def _proj_l1(x, g, mod, first_row, groups, w, gq, wq, gkv, tables, tm):
    t = x.shape[0]
    per = (t // tm) // groups
    row = lambda i: (i, 0)
    fixed = lambda i: (0, 0)
    in_specs = [pl.BlockSpec((tm, D_MODEL), row),
                pl.BlockSpec((1, D_MODEL), fixed),
                _mod_spec(MOD_SCALE1, first_row, per),
                _mod_spec(MOD_SHIFT1, first_row, per),
                pl.BlockSpec(w.shape, fixed),
                pl.BlockSpec((1, C_Q_RANK), fixed),
                pl.BlockSpec(wq.shape, fixed),
                pl.BlockSpec((1, C_KV_RANK), fixed)]
    args = [x, g, mod, mod, w, gq, wq, gkv]
    rope = tables is not None
    if rope:
        n = tables[0].shape[0]
        pos = lambda i: (i % (n // tm), 0)
        in_specs += [pl.BlockSpec((tm, 128), pos)] * 2
        args += list(tables)
    return pl.pallas_call(
        functools.partial(_p1_kernel, rope=rope),
        out_shape=(jax.ShapeDtypeStruct((t, C_HEADS * QCAT), BF16),
                   jax.ShapeDtypeStruct((t, C_KV_RANK), F32),
                   jax.ShapeDtypeStruct((t, C_ROPE), F32)),
        grid=(t // tm,),
        in_specs=in_specs,
        out_specs=(pl.BlockSpec((tm, C_HEADS * QCAT), row),
                   pl.BlockSpec((tm, C_KV_RANK), row),
                   pl.BlockSpec((tm, C_ROPE), row)),
        compiler_params=_cparams(("arbitrary",)),
        name="proj_l1",
    )(*args)


def _attn_kernel(*refs, n_parts, scale):
    q_ref = refs[0]
    kv_parts = refs[1:1 + 2 * n_parts]
    w_ref, o_ref, k_sc, v_sc = refs[1 + 2 * n_parts:]

    @pl.when(pl.program_id(1) == 0)
    def _():
        off = 0
        for p in range(n_parts):
            ckv_ref, kr_ref = kv_parts[2 * p], kv_parts[2 * p + 1]
            nk = ckv_ref.shape[0]
            kv = _dot(ckv_ref[...].astype(BF16), w_ref[...])
            kr = kr_ref[...].astype(BF16)
            kr2 = jnp.concatenate([kr, kr], axis=1)
            for hd in range(C_HEADS):
                k_sc[hd, off:off + nk, 0:128] = kv[:, hd * 128:(hd + 1) * 128].astype(BF16)
                k_sc[hd, off:off + nk, 128:256] = kr2
            v_sc[off:off + nk, :] = kv[:, C_HEADS * C_NOPE:].astype(BF16)
            off += nk

    for hd in range(C_HEADS):
        s = _dot_nt(q_ref[:, hd * QCAT:(hd + 1) * QCAT], k_sc[hd]) * scale
        p = jnp.exp(s - jnp.max(s, axis=1, keepdims=True))
        l = jnp.sum(p, axis=1, keepdims=True)
        o = _dot(p.astype(BF16), v_sc[:, hd * C_DV:(hd + 1) * C_DV])
        o_ref[:, hd * C_DV:(hd + 1) * C_DV] = (o / l).astype(o_ref.dtype)


def _attention(q, kv_parts, w_kvb, tq):
    bsz, n, _ = q.shape
    nk = sum(p[0].shape[1] for p in kv_parts)
    in_specs = [pl.BlockSpec((None, tq, C_HEADS * QCAT), lambda b, i: (b, i, 0))]
    args = [q]
    for ckv, kr in kv_parts:
        in_specs.append(pl.BlockSpec((None, ckv.shape[1], C_KV_RANK), lambda b, i: (b, 0, 0)))
        in_specs.append(pl.BlockSpec((None, kr.shape[1], C_ROPE), lambda b, i: (b, 0, 0)))
        args += [ckv, kr]
    in_specs.append(pl.BlockSpec(w_kvb.shape, lambda b, i: (0, 0)))
    args.append(w_kvb)
    return pl.pallas_call(
        functools.partial(_attn_kernel, n_parts=len(kv_parts),
                          scale=(C_NOPE + C_ROPE) ** -0.5),
        out_shape=jax.ShapeDtypeStruct((bsz, n, C_HEADS * C_DV), BF16),
        grid=(bsz, n // tq),
        in_specs=in_specs,
        out_specs=pl.BlockSpec((None, tq, C_HEADS * C_DV), lambda b, i: (b, i, 0)),
        scratch_shapes=[pltpu.VMEM((C_HEADS, nk, QCAT), BF16),
                        pltpu.VMEM((nk, C_HEADS * C_DV), BF16)],
        compiler_params=_cparams(("arbitrary", "arbitrary")),
        name="mla_attention",
    )(*args)


def _l0_w_in_layout(w_in):
    sizes = (256, 256, 512, 512, 32, 256, 256, 512, 512, 16)
    offs = np.concatenate([[0], np.cumsum(sizes)])
    seg = [w_in[:, offs[i]:offs[i + 1]] for i in range(len(sizes))]
    pad = jnp.zeros((w_in.shape[0], ZG_W - 48), w_in.dtype)
    cols = seg[0:4] + seg[5:9] + [seg[4], seg[9], pad]
    return jnp.concatenate(cols, axis=1).astype(BF16)


def _l1_wq_layout(w_qb):
    blocks = []
    zero = jnp.zeros((w_qb.shape[0], C_ROPE), w_qb.dtype)
    for hd in range(C_HEADS):
        base = hd * (C_NOPE + C_ROPE)
        nope = w_qb[:, base:base + C_NOPE]
        rope = w_qb[:, base + C_NOPE:base + C_NOPE + C_ROPE]
        blocks += [nope, rope, zero] if hd % 2 == 0 else [nope, zero, rope]
    return jnp.concatenate(blocks, axis=1).astype(BF16)


def _l1_wkvb_layout(w_kvb):
    w = w_kvb.reshape(C_KV_RANK, C_HEADS, C_NOPE + C_DV)
    k = w[:, :, :C_NOPE].reshape(C_KV_RANK, C_HEADS * C_NOPE)
    v = w[:, :, C_NOPE:].reshape(C_KV_RANK, C_HEADS * C_DV)
    return jnp.concatenate([k, v], axis=1).astype(BF16)


def _rope_tables(n):
    pos = np.arange(n)
    row = (pos // GRID_W).astype(np.float64)
    col = (pos % GRID_W).astype(np.float64)
    half = C_ROPE // 2
    inv = 1.0 / (ROPE_THETA ** (np.arange(0, half, 2, dtype=np.float64) / half))
    ang_r = row[:, None] * inv[None, :]
    ang_c = col[:, None] * inv[None, :]
    cos64 = np.concatenate([np.cos(ang_r)] * 2 + [np.cos(ang_c)] * 2, axis=1)
    sin64 = np.concatenate([-np.sin(ang_r), np.sin(ang_r), -np.sin(ang_c), np.sin(ang_c)], axis=1)
    return (jnp.asarray(np.concatenate([cos64] * 2, axis=1), F32),
            jnp.asarray(np.concatenate([sin64] * 2, axis=1), F32))


def _gate_weight(w, first_row):
    return jnp.pad(w, ((first_row, ZG_W - first_row - A_GATE_RANK), (0, 0)))


def _state_t(s):
    b = s.shape[0]
    return s.transpose(0, 3, 1, 2).reshape(b, 128, HK)


def _n_rows(n_fwd, n_bwd):
    b = n_fwd.shape[0]
    rows = jnp.zeros((b, 8, HK), F32)
    return rows.at[:, 0, :].set(n_fwd.reshape(b, HK)).at[:, 1, :].set(n_bwd.reshape(b, HK))


def _m_rows(m_fwd, m_bwd):
    m = jnp.concatenate([m_fwd, m_bwd], axis=1)
    return jnp.broadcast_to(m[:, :, None], (m.shape[0], 8, 128))


def kernel(x_prompt, x_sample, state_l0_gla_fwd, state_l0_gla_bwd, state_l0_mlstm_c_fwd, state_l0_mlstm_n_fwd, state_l0_mlstm_m_fwd, state_l0_mlstm_c_bwd, state_l0_mlstm_n_bwd, state_l0_mlstm_m_bwd, cache_l1_ckv, cache_l1_krope, c, c_ctx, l0_w_mod, l0_b_mod, l0_g_pre_mix, l0_g_post_mix, l0_g_pre_ffn, l0_g_post_ffn, l0_w_in, l0_gla_w_gate_f, l0_gla_b_gate_f, l0_gla_w_gate_b, l0_gla_b_gate_b, l0_gla_g_norm, l0_mlstm_b_gates, l0_mlstm_g_norm, l0_w_out, l0_ffn_w_up, l0_ffn_conv_w, l0_ffn_conv_b, l0_ffn_w_down, l1_w_mod, l1_b_mod, l1_g_pre_mix, l1_g_post_mix, l1_g_pre_ffn, l1_g_post_ffn, l1_w_in, l1_g_q_norm, l1_w_qb, l1_g_kv_norm, l1_w_kvb, l1_w_out, l1_ffn_w_up, l1_ffn_conv_w, l1_ffn_conv_b, l1_ffn_w_down):
    bp, sp, d = x_prompt.shape
    bs, ss, _ = x_sample.shape
    row = lambda v: v.reshape(1, -1)

    cond8 = jnp.concatenate([c_ctx[None, :], c, jnp.zeros((8 - 1 - bs, d), F32)], axis=0)
    mods = (_modulation(cond8, l0_w_mod, l0_b_mod), _modulation(cond8, l1_w_mod, l1_b_mod))

    paths = {
        "ctx": dict(x=x_prompt.reshape(bp * sp, d), b=bp, n=sp, first_row=0, groups=1, tm=512),
        "lat": dict(x=x_sample.reshape(bs * ss, d), b=bs, n=ss, first_row=1, groups=bs, tm=512),
    }

    w_in0 = _l0_w_in_layout(l0_w_in)
    wgf = _gate_weight(l0_gla_w_gate_f, 0)
    wgb = _gate_weight(l0_gla_w_gate_b, A_GATE_RANK)
    bg = jnp.pad(l0_mlstm_b_gates, (GATE_COL, ZG_W - GATE_COL - 16)).reshape(1, ZG_W)
    w_out0 = l0_w_out.astype(BF16)
    w_in1 = jnp.pad(l1_w_in, ((0, 0), (0, L1_IN_PAD - l1_w_in.shape[1]))).astype(BF16)
    w_q1 = _l1_wq_layout(l1_w_qb)
    w_kvb1 = _l1_wkvb_layout(l1_w_kvb)
    w_out1 = l1_w_out.astype(BF16)
    ffn = ((l0_ffn_w_up.astype(BF16), l0_ffn_conv_w, row(l0_ffn_conv_b), l0_ffn_w_down.astype(BF16)),
           (l1_ffn_w_up.astype(BF16), l1_ffn_conv_w, row(l1_ffn_conv_b), l1_ffn_w_down.astype(BF16)))
    norms = ((row(l0_g_pre_mix), row(l0_g_post_mix), row(l0_g_pre_ffn), row(l0_g_post_ffn)),
             (row(l1_g_pre_mix), row(l1_g_post_mix), row(l1_g_pre_ffn), row(l1_g_post_ffn)))

    gla_init = (_state_t(state_l0_gla_fwd), _state_t(state_l0_gla_bwd))
    mlstm_init = (_state_t(state_l0_mlstm_c_fwd), _state_t(state_l0_mlstm_c_bwd),
                  _n_rows(state_l0_mlstm_n_fwd, state_l0_mlstm_n_bwd),
                  _m_rows(state_l0_mlstm_m_fwd, state_l0_mlstm_m_bwd))
    tables = _rope_tables(ss)

    results = {}
    for name, p in paths.items():
        x, b, n, tm = p["x"], p["b"], p["n"], p["tm"]
        first_row, groups = p["first_row"], p["groups"]
        is_ctx = name == "ctx"
        ffn_tm = tm if is_ctx else n
        g_pre, g_post, g_pre2, g_post2 = norms[0]

        za, zb, zg = _proj_l0(x, g_pre, mods[0], first_row, groups, w_in0, tm)
        za, zb, zg = (t.reshape(b, n, -1) for t in (za, zb, zg))
        oa, sa_f, sa_b = _gla(za, zg, wgf, wgb, row(l0_gla_b_gate_f), row(l0_gla_b_gate_b),
                              row(l0_gla_g_norm), None if is_ctx else gla_init)
        om, c_f, c_b, n_rows, m_rows = _mlstm(zb, zg, bg, row(l0_mlstm_g_norm),
                                              None if is_ctx else mlstm_init)
        x = _mix_ffn([oa.reshape(b * n, HV), om.reshape(b * n, HV)], w_out0, x, mods[0], first_row,
                     groups, g_post, g_pre2, *ffn[0], g_post2, ffn_tm, n)

        g_pre, g_post, g_pre2, g_post2 = norms[1]
        q, ckv, kr = _proj_l1(x, g_pre, mods[1], first_row, groups, w_in1, row(l1_g_q_norm), w_q1,
                              row(l1_g_kv_norm), None if is_ctx else tables, tm)
        ckv3 = ckv.reshape(b, n, C_KV_RANK)
        kr3 = kr.reshape(b, n, C_ROPE)
        kv_parts = [(ckv3, kr3)] if is_ctx else [(cache_l1_ckv, cache_l1_krope), (ckv3, kr3)]
        o = _attention(q.reshape(b, n, -1), kv_parts, w_kvb1, 256)
        x = _mix_ffn([o.reshape(b * n, C_HEADS * C_DV)], w_out1, x, mods[1], first_row,
                     groups, g_post, g_pre2, *ffn[1], g_post2, ffn_tm, n)

        results[name] = dict(y=x.reshape(b, n, d), gla=(sa_f, sa_b), c=(c_f, c_b),
                             n=n_rows, m=m_rows, ckv=ckv3, kr=kr3)

    r = results["ctx"]
    n_fwd = r["n"][:, 0, :].reshape(bp, B_HEADS, B_DK)
    n_bwd = r["n"][:, 1, :].reshape(bp, B_HEADS, B_DK)
    m_fwd = r["m"][:, 0:B_HEADS, 0]
    m_bwd = r["m"][:, B_HEADS:2 * B_HEADS, 0]
    return (r["y"], results["lat"]["y"], r["gla"][0], r["gla"][1],
            r["c"][0], n_fwd, m_fwd, r["c"][1], n_bwd, m_bwd, r["ckv"], r["kr"])
```

```python
import functools

import numpy as np
import jax
import jax.numpy as jnp
from jax import lax
from jax.experimental import pallas as pl
from jax.experimental.pallas import tpu as pltpu

F32 = jnp.float32
BF16 = jnp.bfloat16

D_MODEL = 1024
EPS = 1e-6
N_MOD = 6
GRID_W = 64
ROPE_THETA = 10000.0

A_HEADS, A_DK, A_DV, A_GATE_RANK, A_GATE_TEMP = 4, 64, 128, 16, 16.0
B_HEADS, B_DK, B_DV = 4, 64, 128
C_HEADS, C_Q_RANK, C_KV_RANK, C_NOPE, C_ROPE, C_DV = 8, 384, 256, 128, 64, 128
D_FF = 2816
CONV_W = 3

HEADS = 4
HK = HEADS * 64
HV = HEADS * 128
ZA_W = 1536
ZG_W = 128
GATE_COL = 32
RCHUNK = 128
L1_IN_PAD = 768
QCAT = 256

VMEM_LIMIT = 56 * 1024 * 1024


def _cparams(sem, vmem=VMEM_LIMIT):
    return pltpu.CompilerParams(dimension_semantics=sem, vmem_limit_bytes=vmem)


def _dot(a, b):
    return jnp.dot(a, b, preferred_element_type=F32)


def _dot_nt(a, b):
    return lax.dot_general(a, b, (((1,), (1,)), ((), ())), preferred_element_type=F32)


def _dot_tn(a, b):
    return lax.dot_general(a, b, (((0,), (0,)), ((), ())), preferred_element_type=F32)


def _split3(x):
    hi = x.astype(BF16)
    r1 = x - hi.astype(F32)
    mid = r1.astype(BF16)
    lo = (r1 - mid.astype(F32)).astype(BF16)
    return hi, mid, lo


def _tri_dot(tri, x):
    hi, mid, lo = _split3(x)
    return _dot(tri, hi) + _dot(tri, mid) + _dot(tri, lo)


def _dot_tri(x, tri):
    hi, mid, lo = _split3(x)
    return _dot(hi, tri) + _dot(mid, tri) + _dot(lo, tri)


def _dot_hl(a, b):
    ah = a.astype(BF16)
    al = (a - ah.astype(F32)).astype(BF16)
    bh = b.astype(BF16)
    bl = (b - bh.astype(F32)).astype(BF16)
    return _dot(ah, bh) + _dot(ah, bl) + _dot(al, bh)


def _rms(x, g):
    return x * lax.rsqrt(jnp.mean(x * x, axis=-1, keepdims=True) + EPS) * g


def _sigmoid(x):
    return 1.0 / (1.0 + jnp.exp(-x))


def _silu(x):
    return x * _sigmoid(x)


def _log_sigmoid(x):
    return jnp.minimum(x, 0.0) - jnp.log(1.0 + jnp.exp(-jnp.abs(x)))


def _tri_masks(n):
    r = lax.broadcasted_iota(jnp.int32, (n, n), 0)
    c = lax.broadcasted_iota(jnp.int32, (n, n), 1)
    return c <= r, c >= r


def _mod_kernel(cond_ref, w_ref, b_ref, o_ref):
    s = _silu(cond_ref[...])
    o_ref[...] = _dot(s.astype(BF16), w_ref[...].astype(BF16)) + b_ref[...]


def _modulation(cond8, w_mod, b_mod):
    d = D_MODEL
    out = pl.pallas_call(
        _mod_kernel,
        out_shape=jax.ShapeDtypeStruct((N_MOD, 8, d), F32),
        grid=(N_MOD,),
        in_specs=[pl.BlockSpec((8, d), lambda j: (0, 0)),
                  pl.BlockSpec((d, d), lambda j: (0, j)),
                  pl.BlockSpec((1, d), lambda j: (0, j))],
        out_specs=pl.BlockSpec((None, 8, d), lambda j: (j, 0, 0)),
        compiler_params=_cparams(("arbitrary",)),
        name="modulation",
    )(cond8, w_mod, b_mod.reshape(1, N_MOD * d))
    return out.reshape(N_MOD, 8, 1, d)


MOD_SHIFT1, MOD_SCALE1, MOD_GATE1, MOD_SHIFT2, MOD_SCALE2, MOD_GATE2 = range(N_MOD)


def _mod_spec(which, first_row, per):
    return pl.BlockSpec((None, None, 1, D_MODEL), lambda i: (which, first_row + i // per, 0, 0))


def _p0_kernel(x_ref, g_ref, sc_ref, sh_ref, w_ref, za_ref, zb_ref, zg_ref):
    h = _rms(x_ref[...], g_ref[...]) * (1.0 + sc_ref[...]) + sh_ref[...]
    hb = h.astype(BF16)
    za_ref[...] = _dot(hb, w_ref[:, 0:ZA_W])
    zb_ref[...] = _dot(hb, w_ref[:, ZA_W:2 * ZA_W])
    zg_ref[...] = _dot(hb, w_ref[:, 2 * ZA_W:2 * ZA_W + ZG_W])


def _proj_l0(x, g, mod, first_row, groups, w, tm):
    t = x.shape[0]
    per = (t // tm) // groups
    nw = w.shape[1]
    row = lambda i: (i, 0)
    fixed = lambda i: (0, 0)
    return pl.pallas_call(
        _p0_kernel,
        out_shape=(jax.ShapeDtypeStruct((t, ZA_W), F32),
                   jax.ShapeDtypeStruct((t, ZA_W), F32),
                   jax.ShapeDtypeStruct((t, ZG_W), F32)),
        grid=(t // tm,),
        in_specs=[pl.BlockSpec((tm, D_MODEL), row),
                  pl.BlockSpec((1, D_MODEL), fixed),
                  _mod_spec(MOD_SCALE1, first_row, per),
                  _mod_spec(MOD_SHIFT1, first_row, per),
                  pl.BlockSpec((D_MODEL, nw), fixed)],
        out_specs=(pl.BlockSpec((tm, ZA_W), row),
                   pl.BlockSpec((tm, ZA_W), row),
                   pl.BlockSpec((tm, ZG_W), row)),
        compiler_params=_cparams(("arbitrary",)),
        name="proj_l0",
    )(x, g, mod, mod, w)


def _head_lane_mask(shape, h):
    lane = lax.broadcasted_iota(jnp.int32, shape, len(shape) - 1)
    return (lane >= h * 64) & (lane < (h + 1) * 64)


def _head_row(h):
    return jnp.where(_head_lane_mask((1, HK), h), 1.0, 0.0).astype(BF16)


def _gla_prepare(za_ref, la_ref, r0, rev, tri):
    L = RCHUNK
    rows = pl.ds(r0, L)
    b = _tri_dot(tri, la_ref[rows, :])
    mid = L // 2
    ref = b[mid:mid + 1, :]
    b_end = b[0:1, :] if rev else b[L - 1:L, :]
    q = za_ref[rows, 0:HK]
    k = za_ref[rows, HK:2 * HK]
    v = za_ref[rows, 2 * HK:2 * HK + HV].astype(BF16)
    qe = (q * (jnp.exp(b - ref) * (A_DK ** -0.5))).astype(BF16)
    ke = (k * jnp.exp(ref - b)).astype(BF16)
    qm = jnp.concatenate([qe * _head_row(h) for h in range(HEADS)], axis=0)
    rt = _dot_tn(v, ke)
    upd = None
    for h in range(HEADS):
        blk = rt[h * 128:(h + 1) * 128, :]
        blk = jnp.where(_head_lane_mask(blk.shape, h), blk, 0.0)
        upd = blk if upd is None else upd + blk
    return dict(rows=rows, qm=qm, ke=ke, v=v, upd=upd * jnp.exp(b_end - ref),
                decay=jnp.exp(b_end), eref=jnp.exp(ref))


def _gla_apply(items):
    L = RCHUNK
    sts = [st_ref[...] for _, st_ref, _, _ in items]
    o1s = []
    for (p, _, _, _), st in zip(items, sts):
        rhs = jnp.concatenate([p["ke"], (st * p["eref"]).astype(BF16)], axis=0)
        o1s.append(_dot_nt(p["qm"], rhs))
    for h in range(HEADS):
        for (p, _, o_ref, mask), o1 in zip(items, o1s):
            blk = o1[h * L:(h + 1) * L, :]
            sc = jnp.where(mask, blk[:, 0:L], 0.0).astype(BF16)
            o_ref[p["rows"], h * 128:(h + 1) * 128] = (
                blk[:, L:L + 128] + _dot(sc, p["v"][:, h * 128:(h + 1) * 128]))
    for (p, st_ref, _, _), st in zip(items, sts):
        st_ref[...] = st * p["decay"] + p["upd"]


def _gla_kernel(*refs, n, has_init):
    if has_init:
        (za_ref, zg_ref, wgf_ref, wgb_ref, bgf_ref, bgb_ref, gn_ref, sf0_ref, sb0_ref,
         o_ref, sf_ref, sb_ref, laf_ref, lab_ref, of_ref, ob_ref, stf_ref, stb_ref) = refs
    else:
        (za_ref, zg_ref, wgf_ref, wgb_ref, bgf_ref, bgb_ref, gn_ref,
         o_ref, sf_ref, sb_ref, laf_ref, lab_ref, of_ref, ob_ref, stf_ref, stb_ref) = refs
    L = RCHUNK
    nc = n // L
    zg = zg_ref[...]
    xf = _dot_hl(zg, wgf_ref[...]) + bgf_ref[...]
    xb = _dot_hl(zg, wgb_ref[...]) + bgb_ref[...]
    laf_ref[...] = _log_sigmoid(xf) / A_GATE_TEMP
    lab_ref[...] = _log_sigmoid(xb) / A_GATE_TEMP
    if has_init:
        stf_ref[...] = sf0_ref[...]
        stb_ref[...] = sb0_ref[...]
    else:
        stf_ref[...] = jnp.zeros_like(stf_ref)
        stb_ref[...] = jnp.zeros_like(stb_ref)

    m_lo, m_up = _tri_masks(L)
    tri_lo = jnp.where(m_lo, 1.0, 0.0).astype(BF16)
    tri_up = jnp.where(m_up, 1.0, 0.0).astype(BF16)

    def scan_body(g, carry):
        waves = []
        for w in range(2):
            c = 2 * g + w
            rf = pl.multiple_of(c * L, L)
            rb = pl.multiple_of((nc - 1 - c) * L, L)
            waves.append((_gla_prepare(za_ref, laf_ref, rf, False, tri_lo),
                          _gla_prepare(za_ref, lab_ref, rb, True, tri_up)))
        for pf, pb in waves:
            _gla_apply([(pf, stf_ref, of_ref, m_lo), (pb, stb_ref, ob_ref, m_up)])
        return carry

    if nc == 2:
        scan_body(0, 0)
    else:
        lax.fori_loop(0, nc // 2, scan_body, 0)

    gn = gn_ref[...]

    def out_body(c, carry):
        rows = pl.ds(pl.multiple_of(c * L, L), L)
        for h in range(HEADS):
            cols = slice(h * 128, (h + 1) * 128)
            o = of_ref[rows, cols] + ob_ref[rows, cols]
            gate = za_ref[rows, 2 * HK + HV + h * 128:2 * HK + HV + (h + 1) * 128]
            o_ref[rows, cols] = (_rms(o, gn) * _silu(gate)).astype(o_ref.dtype)
        return carry

    if nc == 2:
        for c in range(nc):
            out_body(c, 0)
    else:
        lax.fori_loop(0, nc, out_body, 0)

    stf_t = stf_ref[...].T
    stb_t = stb_ref[...].T
    for h in range(HEADS):
        sf_ref[h] = stf_t[h * 64:(h + 1) * 64, :]
        sb_ref[h] = stb_t[h * 64:(h + 1) * 64, :]


def _gla(za, zg, wgf, wgb, bgf, bgb, gn, init):
    bsz, n, _ = za.shape
    has_init = init is not None
    per_b3 = lambda b: (b, 0, 0)
    per_b4 = lambda b: (b, 0, 0, 0)
    fixed = lambda b: (0, 0)
    in_specs = [pl.BlockSpec((None, n, ZA_W), per_b3),
                pl.BlockSpec((None, n, ZG_W), per_b3),
                pl.BlockSpec((ZG_W, HK), fixed),
                pl.BlockSpec((ZG_W, HK), fixed),
                pl.BlockSpec((1, HK), fixed),
                pl.BlockSpec((1, HK), fixed),
                pl.BlockSpec((1, 128), fixed)]
    args = [za, zg, wgf, wgb, bgf, bgb, gn]
    if has_init:
        in_specs += [pl.BlockSpec((None, 128, HK), per_b3)] * 2
        args += list(init)
    return pl.pallas_call(
        functools.partial(_gla_kernel, n=n, has_init=has_init),
        out_shape=(jax.ShapeDtypeStruct((bsz, n, HV), BF16),
                   jax.ShapeDtypeStruct((bsz, HEADS, 64, 128), F32),
                   jax.ShapeDtypeStruct((bsz, HEADS, 64, 128), F32)),
        grid=(bsz,),
        in_specs=in_specs,
        out_specs=(pl.BlockSpec((None, n, HV), per_b3),
                   pl.BlockSpec((None, HEADS, 64, 128), per_b4),
                   pl.BlockSpec((None, HEADS, 64, 128), per_b4)),
        scratch_shapes=[pltpu.VMEM((n, HK), F32), pltpu.VMEM((n, HK), F32),
                        pltpu.VMEM((n, HV), F32), pltpu.VMEM((n, HV), F32),
                        pltpu.VMEM((128, HK), F32), pltpu.VMEM((128, HK), F32)],
        compiler_params=_cparams(("arbitrary",)),
        name="gla",
    )(*args)


T_M, T_MOUT, T_G, T_SRCW, T_MPREV, T_CARRY, T_USED = 0, 8, 16, 24, 32, 40, 48


def _lane_scan(x, op, fill, is_fwd, pos, n):
    k = 1
    while k < RCHUNK:
        pf = jnp.where(pos >= k, pltpu.roll(x, k, axis=1), fill)
        sf = jnp.where(pos < RCHUNK - k, pltpu.roll(x, n - k, axis=1), fill)
        x = op(x, jnp.where(is_fwd, pf, sf))
        k *= 2
    return x


def _mlstm_gate_tables(g, m0, gt_ref, tab_ref, col_ref, n):
    L = RCHUNK
    nc = n // L
    for c in range(nc):
        gt_ref[:, c * L:(c + 1) * L] = g[c * L:(c + 1) * L, :].T
        tab_ref[c, T_USED:, :] = jnp.zeros((ZG_W - T_USED, L), F32)
    gi = gt_ref[GATE_COL:GATE_COL + 8, :]
    lf = gt_ref[GATE_COL + 8:GATE_COL + 16, :]
    is_fwd = lax.broadcasted_iota(jnp.int32, (8, n), 0) < HEADS
    pos = lax.broadcasted_iota(jnp.int32, (8, n), 1) % L
    b = _lane_scan(lf, jnp.add, 0.0, is_fwd, pos, n)
    gg = gi - b
    cmax = _lane_scan(gg, jnp.maximum, -jnp.inf, is_fwd, pos, n)
    fwd_l = lax.broadcasted_iota(jnp.int32, (8, L), 0) < HEADS
    mp = m0
    for j in range(nc):
        cf, cb = j, nc - 1 - j

        def pick(x):
            return jnp.where(fwd_l, x[:, cf * L:(cf + 1) * L], x[:, cb * L:(cb + 1) * L])

        gj, bj, cj, lj = pick(gg), pick(b), pick(cmax), pick(lf)
        mx = jnp.maximum(mp, jnp.max(gj, axis=1, keepdims=True))
        mj = jnp.maximum(cj, mp)
        blocks = (mj, bj + mj, gj, jnp.exp(gj - mx),
                  jnp.broadcast_to(mp, (8, L)), jnp.broadcast_to(jnp.exp(mp - mx), (8, L)))
        for t, val in enumerate(blocks):
            tab_ref[cf, 8 * t:8 * t + HEADS, :] = val[0:HEADS]
            tab_ref[cb, 8 * t + HEADS:8 * t + 8, :] = val[HEADS:8]
        mp = jnp.sum(lj, axis=1, keepdims=True) + mx
    for c in range(nc):
        col_ref[c * L:(c + 1) * L, :] = tab_ref[c].T
    return mp


def _mlstm_prepare(zb_ref, tab_ref, col_ref, c, d, mask):
    L = RCHUNK
    rows = pl.ds(pl.multiple_of(c * L, L), L)
    qb = zb_ref[rows, 0:HK].astype(BF16)
    kb = (zb_ref[rows, HK:2 * HK] * (B_DK ** -0.5)).astype(BF16)
    v = zb_ref[rows, 2 * HK:2 * HK + HV]
    vb = v.astype(BF16)
    vt = v.T
    qm = jnp.concatenate([qb * _head_row(h) for h in range(HEADS)], axis=0)
    s = _dot_nt(qm, kb)
    ones = jnp.ones((L, 128), BF16)
    pv, w_inter, e_negm = [], [], []
    upd = None
    carry = None
    for h in range(HEADS):
        r = 4 * d + h
        m_rep = jnp.broadcast_to(col_ref[rows, T_M + r:T_M + r + 1], (L, 128))
        mout_rep = jnp.broadcast_to(col_ref[rows, T_MOUT + r:T_MOUT + r + 1], (L, 128))
        g_row = tab_ref[c, T_G + r:T_G + r + 1, :]
        w = jnp.exp(jnp.where(mask, g_row - m_rep, -jnp.inf))
        qk = (s[h * L:(h + 1) * L, :] * w).astype(BF16)
        v1 = jnp.concatenate([vb[:, h * 128:(h + 1) * 128], ones], axis=1)
        pv.append(_dot(qk, v1))
        w_inter.append(jnp.exp(tab_ref[c, T_MPREV + r:T_MPREV + r + 1, :] - m_rep))
        e_negm.append(jnp.exp(-mout_rep))
        sw = tab_ref[c, T_SRCW + r:T_SRCW + r + 1, :]
        lhs = jnp.concatenate([vt[h * 128:(h + 1) * 128, :] * sw, jnp.broadcast_to(sw, (128, L))],
                              axis=0).astype(BF16)
        blk = _dot(lhs, kb)
        blk = jnp.where(_head_lane_mask(blk.shape, h), blk, 0.0)
        upd = blk if upd is None else upd + blk
        cr = tab_ref[c, T_CARRY + r:T_CARRY + r + 1, :]
        cr = jnp.where(_head_lane_mask((1, HK), h), jnp.concatenate([cr, cr], axis=1), 0.0)
        carry = cr if carry is None else carry + cr
    return dict(rows=rows, qm=qm, pv=pv, w_inter=w_inter, e_negm=e_negm, upd=upd, carry=carry)


def _mlstm_apply(items):
    L = RCHUNK
    sts = [st_ref[...] for _, st_ref, _ in items]
    a_all = [_dot_nt(p["qm"], st.astype(BF16)) for (p, _, _), st in zip(items, sts)]
    for h in range(HEADS):
        for (p, _, h_ref), a in zip(items, a_all):
            blk = a[h * L:(h + 1) * L, :]
            pv, w, e = p["pv"][h], p["w_inter"][h], p["e_negm"][h]
            num = w * blk[:, 0:128] + pv[:, 0:128]
            den = w * blk[:, 128:256] + pv[:, 128:256]
            h_ref[p["rows"], h * 128:(h + 1) * 128] = num / jnp.maximum(jnp.abs(den), e)
    for (p, st_ref, _), st in zip(items, sts):
        st_ref[...] = p["carry"] * st + p["upd"]


def _mlstm_kernel(*refs, n, has_init):
    if has_init:
        (zb_ref, zg_ref, zgn_ref, bg_ref, gn_ref, cf0_ref, cb0_ref, n0_ref, m0_ref, m0n_ref,
         o_ref, cf_ref, cb_ref, n_out_ref, m_out_ref,
         gt_ref, tab_a, tab_b, col_a, col_b, mf_a, mf_b, hf_ref, hb_ref, stf_ref, stb_ref) = refs
    else:
        (zb_ref, zg_ref, zgn_ref, bg_ref, gn_ref,
         o_ref, cf_ref, cb_ref, n_out_ref, m_out_ref,
         gt_ref, tab_a, tab_b, col_a, col_b, mf_a, mf_b, hf_ref, hb_ref, stf_ref, stb_ref) = refs
    L = RCHUNK
    nc = n // L
    step = pl.program_id(0)

    def fill_tables(gate_ref, m0, tab_ref, col_ref, mf_ref):
        g = gate_ref[...] + bg_ref[...]
        lane = lax.broadcasted_iota(jnp.int32, g.shape, 1)
        is_forget = (lane >= GATE_COL + 8) & (lane < GATE_COL + 16)
        g = jnp.where(is_forget, _log_sigmoid(g), g)
        m_fin = _mlstm_gate_tables(g, m0, gt_ref, tab_ref, col_ref, n)
        mf_ref[...] = jnp.broadcast_to(m_fin, (8, 128))

    if has_init:
        m0, m0_next = m0_ref[:, 0:1], m0n_ref[:, 0:1]
        stf_ref[...] = jnp.concatenate(
            [cf0_ref[...], jnp.broadcast_to(n0_ref[0:1, :], (128, HK))], axis=0)
        stb_ref[...] = jnp.concatenate(
            [cb0_ref[...], jnp.broadcast_to(n0_ref[1:2, :], (128, HK))], axis=0)
    else:
        m0 = m0_next = jnp.zeros((8, 1), F32)
        stf_ref[...] = jnp.zeros_like(stf_ref)
        stb_ref[...] = jnp.zeros_like(stb_ref)

    @pl.when(step == 0)
    def _():
        fill_tables(zg_ref, m0, tab_a, col_a, mf_a)

    m_lo, m_up = _tri_masks(L)

    def run(tab_ref, col_ref, mf_ref, tab_next, col_next, mf_next):
        fill_tables(zgn_ref, m0_next, tab_next, col_next, mf_next)

        def scan_body(gidx, carry):
            waves = []
            for w in range(2):
                c = 2 * gidx + w
                waves.append((_mlstm_prepare(zb_ref, tab_ref, col_ref, c, 0, m_lo),
                              _mlstm_prepare(zb_ref, tab_ref, col_ref, nc - 1 - c, 1, m_up)))
            for pf, pb in waves:
                _mlstm_apply([(pf, stf_ref, hf_ref), (pb, stb_ref, hb_ref)])
            return carry

        if nc == 2:
            scan_body(0, 0)
            for c in range(nc):
                out_body(c, 0)
        else:
            lax.fori_loop(0, nc // 2, scan_body, 0)
            lax.fori_loop(0, nc, out_body, 0)
        m_out_ref[...] = mf_ref[...]

    gn = gn_ref[...]

    def out_body(c, carry):
        rows = pl.ds(pl.multiple_of(c * L, L), L)
        for h in range(HEADS):
            cols = slice(h * 128, (h + 1) * 128)
            hh = hf_ref[rows, cols] + hb_ref[rows, cols]
            gate = zb_ref[rows, 2 * HK + HV + h * 128:2 * HK + HV + (h + 1) * 128]
            o_ref[rows, cols] = (_sigmoid(gate) * _rms(hh, gn)).astype(o_ref.dtype)
        return carry

    @pl.when(step % 2 == 0)
    def _():
        run(tab_a, col_a, mf_a, tab_b, col_b, mf_b)

    @pl.when(step % 2 == 1)
    def _():
        run(tab_b, col_b, mf_b, tab_a, col_a, mf_a)

    stf = stf_ref[...]
    stb = stb_ref[...]
    ctf_t = stf[0:128, :].T
    ctb_t = stb[0:128, :].T
    for h in range(HEADS):
        cf_ref[h] = ctf_t[h * 64:(h + 1) * 64, :]
        cb_ref[h] = ctb_t[h * 64:(h + 1) * 64, :]
    n_out_ref[...] = jnp.concatenate(
        [stf[128:129, :], stb[128:129, :], jnp.zeros((6, HK), F32)], axis=0)


def _mlstm(zb, zg, bg, gn, init):
    bsz, n, _ = zb.shape
    has_init = init is not None
    nc = n // RCHUNK
    per_b3 = lambda b: (b, 0, 0)
    next_b3 = lambda b: (jnp.minimum(b + 1, bsz - 1), 0, 0)
    per_b4 = lambda b: (b, 0, 0, 0)
    fixed = lambda b: (0, 0)
    in_specs = [pl.BlockSpec((None, n, ZA_W), per_b3),
                pl.BlockSpec((None, n, ZG_W), per_b3),
                pl.BlockSpec((None, n, ZG_W), next_b3),
                pl.BlockSpec((1, ZG_W), fixed),
                pl.BlockSpec((1, 128), fixed)]
    args = [zb, zg, zg, bg, gn]
    if has_init:
        cf0, cb0, n0, m0 = init
        in_specs += [pl.BlockSpec((None, 128, HK), per_b3)] * 2
        in_specs += [pl.BlockSpec((None, 8, HK), per_b3), pl.BlockSpec((None, 8, 128), per_b3),
                     pl.BlockSpec((None, 8, 128), next_b3)]
        args += [cf0, cb0, n0, m0, m0]
    return pl.pallas_call(
        functools.partial(_mlstm_kernel, n=n, has_init=has_init),
        out_shape=(jax.ShapeDtypeStruct((bsz, n, HV), BF16),
                   jax.ShapeDtypeStruct((bsz, HEADS, 64, 128), F32),
                   jax.ShapeDtypeStruct((bsz, HEADS, 64, 128), F32),
                   jax.ShapeDtypeStruct((bsz, 8, HK), F32),
                   jax.ShapeDtypeStruct((bsz, 8, 128), F32)),
        grid=(bsz,),
        in_specs=in_specs,
        out_specs=(pl.BlockSpec((None, n, HV), per_b3),
                   pl.BlockSpec((None, HEADS, 64, 128), per_b4),
                   pl.BlockSpec((None, HEADS, 64, 128), per_b4),
                   pl.BlockSpec((None, 8, HK), per_b3),
                   pl.BlockSpec((None, 8, 128), per_b3)),
        scratch_shapes=[pltpu.VMEM((ZG_W, n), F32),
                        pltpu.VMEM((nc, ZG_W, RCHUNK), F32), pltpu.VMEM((nc, ZG_W, RCHUNK), F32),
                        pltpu.VMEM((n, ZG_W), F32), pltpu.VMEM((n, ZG_W), F32),
                        pltpu.VMEM((8, 128), F32), pltpu.VMEM((8, 128), F32),
                        pltpu.VMEM((n, HV), F32), pltpu.VMEM((n, HV), F32),
                        pltpu.VMEM((2 * 128, HK), F32), pltpu.VMEM((2 * 128, HK), F32)],
        compiler_params=_cparams(("arbitrary",)),
        name="mlstm",
    )(*args)


def _mix_ffn_kernel(*refs, n_parts, seq, tf):
    parts = refs[:n_parts]
    (wo_ref, x_ref, gt1_ref, gpost1_ref, gpre2_ref, sc2_ref, sh2_ref,
     wu_ref, cw_ref, cb_ref, wd_ref, gt2_ref, gpost2_ref, o_ref, act_ref) = refs[n_parts:]
    tm = x_ref.shape[0]
    m = None
    off = 0
    for p in parts:
        kp = p.shape[1]
        t = _dot(p[...], wo_ref[off:off + kp, :])
        m = t if m is None else m + t
        off += kp
    x1 = x_ref[...] + gt1_ref[...] * _rms(m, gpost1_ref[...])
    o_ref[...] = x1
    hb = (_rms(x1, gpre2_ref[...]) * (1.0 + sc2_ref[...]) + sh2_ref[...]).astype(BF16)

    pos = lax.broadcasted_iota(jnp.int32, (tm, 1), 0) % seq
    first = pos == 0
    last = pos == seq - 1

    def conv(u, cols):
        prev = jnp.where(first, 0.0, pltpu.roll(u, 1, axis=0))
        nxt = jnp.where(last, 0.0, pltpu.roll(u, tm - 1, axis=0))
        return (cw_ref[0:1, cols] * prev + cw_ref[1:2, cols] * u + cw_ref[2:3, cols] * nxt
                + cb_ref[:, cols])

    for j in range(D_FF // tf):
        ca = slice(j * tf, (j + 1) * tf)
        cg = slice(D_FF + j * tf, D_FF + (j + 1) * tf)
        a = conv(_dot(hb, wu_ref[:, ca]), ca)
        g = conv(_dot(hb, wu_ref[:, cg]), cg)
        act_ref[:, ca] = (_silu(g) * a).astype(BF16)
    y = _dot(act_ref[...], wd_ref[...])
    o_ref[...] = o_ref[...] + gt2_ref[...] * _rms(y, gpost2_ref[...])


def _resident(shape):
    return pl.BlockSpec(shape, lambda *_: (0,) * len(shape), pipeline_mode=pl.Buffered(1))


def _mix_ffn(parts, w_out, x, mod, first_row, groups, gpost1, gpre2, w_up, conv_w, conv_b, w_down,
             gpost2, tm, seq, tf=256):
    t = x.shape[0]
    per = (t // tm) // groups
    row = lambda i: (i, 0)
    vec = _resident((1, D_MODEL))
    in_specs = [pl.BlockSpec((tm, p.shape[1]), row) for p in parts]
    in_specs += [_resident(w_out.shape), pl.BlockSpec((tm, D_MODEL), row),
                 _mod_spec(MOD_GATE1, first_row, per), vec, vec,
                 _mod_spec(MOD_SCALE2, first_row, per), _mod_spec(MOD_SHIFT2, first_row, per),
                 _resident(w_up.shape), _resident(conv_w.shape), _resident(conv_b.shape),
                 _resident(w_down.shape), _mod_spec(MOD_GATE2, first_row, per), vec]
    return pl.pallas_call(
        functools.partial(_mix_ffn_kernel, n_parts=len(parts), seq=seq, tf=tf),
        out_shape=jax.ShapeDtypeStruct((t, D_MODEL), F32),
        grid=(t // tm,),
        in_specs=in_specs,
        out_specs=pl.BlockSpec((tm, D_MODEL), row),
        scratch_shapes=[pltpu.VMEM((tm, D_FF), BF16)],
        compiler_params=_cparams(("arbitrary",)),
        name="mix_ffn",
    )(*parts, w_out, x, mod, gpost1, gpre2, mod, mod, w_up, conv_w, conv_b, w_down, mod, gpost2)


def _rope(x, cos, sin_signed):
    lane = lax.broadcasted_iota(jnp.int32, x.shape, 1)
    lower = (lane % 32) < 16
    partner = jnp.where(lower, pltpu.roll(x, 128 - 16, axis=1), pltpu.roll(x, 16, axis=1))
    return x * cos + partner * sin_signed


def _p1_kernel(*refs, rope):
    if rope:
        (x_ref, g_ref, sc_ref, sh_ref, w_ref, gq_ref, wq_ref, gkv_ref, cos_ref, sin_ref,
         q_ref, ckv_ref, kr_ref) = refs
    else:
        (x_ref, g_ref, sc_ref, sh_ref, w_ref, gq_ref, wq_ref, gkv_ref,
         q_ref, ckv_ref, kr_ref) = refs
    h = _rms(x_ref[...], g_ref[...]) * (1.0 + sc_ref[...]) + sh_ref[...]
    z = _dot(h.astype(BF16), w_ref[...])
    qa = _rms(z[:, 0:C_Q_RANK], gq_ref[...])
    q = _dot(qa.astype(BF16), wq_ref[...])
    ckv_ref[...] = _rms(z[:, C_Q_RANK:C_Q_RANK + C_KV_RANK], gkv_ref[...])
    kr = z[:, C_Q_RANK + C_KV_RANK:L1_IN_PAD]
    if rope:
        cos = cos_ref[...]
        sin = sin_ref[...]
        kr = _rope(kr, cos, sin)
    kr_ref[...] = kr[:, 0:C_ROPE]
    for hd in range(C_HEADS):
        c0 = hd * QCAT
        q_ref[:, c0:c0 + 128] = q[:, c0:c0 + 128].astype(q_ref.dtype)
        qr = q[:, c0 + 128:c0 + 256]
        if rope:
            qr = _rope(qr, cos, sin)
        q_ref[:, c0 + 128:c0 + 256] = qr.astype(q_ref.dtype)


def _proj_l1(x, g, mod, first_row, groups, w, gq, wq, gkv, tables, tm):
    t = x.shape[0]
    per = (t // tm) // groups
    row = lambda i: (i, 0)
    fixed = lambda i: (0, 0)
    in_specs = [pl.BlockSpec((tm, D_MODEL), row),
                pl.BlockSpec((1, D_MODEL), fixed),
                _mod_spec(MOD_SCALE1, first_row, per),
                _mod_spec(MOD_SHIFT1, first_row, per),
                pl.BlockSpec(w.shape, fixed),
                pl.BlockSpec((1, C_Q_RANK), fixed),
                pl.BlockSpec(wq.shape, fixed),
                pl.BlockSpec((1, C_KV_RANK), fixed)]
    args = [x, g, mod, mod, w, gq, wq, gkv]
    rope = tables is not None
    if rope:
        n = tables[0].shape[0]
        pos = lambda i: (i % (n // tm), 0)
        in_specs += [pl.BlockSpec((tm, 128), pos)] * 2
        args += list(tables)
    return pl.pallas_call(
        functools.partial(_p1_kernel, rope=rope),
        out_shape=(jax.ShapeDtypeStruct((t, C_HEADS * QCAT), BF16),
                   jax.ShapeDtypeStruct((t, C_KV_RANK), F32),
                   jax.ShapeDtypeStruct((t, C_ROPE), F32)),
        grid=(t // tm,),
        in_specs=in_specs,
        out_specs=(pl.BlockSpec((tm, C_HEADS * QCAT), row),
                   pl.BlockSpec((tm, C_KV_RANK), row),
                   pl.BlockSpec((tm, C_ROPE), row)),
        compiler_params=_cparams(("arbitrary",)),
        name="proj_l1",
    )(*args)


def _attn_kernel(*refs, n_parts, scale):
    q_ref = refs[0]
    kv_parts = refs[1:1 + 2 * n_parts]
    w_ref, o_ref, k_sc, v_sc = refs[1 + 2 * n_parts:]

    @pl.when(pl.program_id(1) == 0)
    def _():
        off = 0
        for p in range(n_parts):
            ckv_ref, kr_ref = kv_parts[2 * p], kv_parts[2 * p + 1]
            nk = ckv_ref.shape[0]
            kv = _dot(ckv_ref[...].astype(BF16), w_ref[...])
            kr = kr_ref[...].astype(BF16)
            kr2 = jnp.concatenate([kr, kr], axis=1)
            for hd in range(C_HEADS):
                k_sc[hd, off:off + nk, 0:128] = kv[:, hd * 128:(hd + 1) * 128].astype(BF16)
                k_sc[hd, off:off + nk, 128:256] = kr2
            v_sc[off:off + nk, :] = kv[:, C_HEADS * C_NOPE:].astype(BF16)
            off += nk

    for hd in range(C_HEADS):
        s = _dot_nt(q_ref[:, hd * QCAT:(hd + 1) * QCAT], k_sc[hd]) * scale
        p = jnp.exp(s - jnp.max(s, axis=1, keepdims=True))
        l = jnp.sum(p, axis=1, keepdims=True)
        o = _dot(p.astype(BF16), v_sc[:, hd * C_DV:(hd + 1) * C_DV])
        o_ref[:, hd * C_DV:(hd + 1) * C_DV] = (o / l).astype(o_ref.dtype)


def _attention(q, kv_parts, w_kvb, tq):
    bsz, n, _ = q.shape
    nk = sum(p[0].shape[1] for p in kv_parts)
    in_specs = [pl.BlockSpec((None, tq, C_HEADS * QCAT), lambda b, i: (b, i, 0))]
    args = [q]
    for ckv, kr in kv_parts:
        in_specs.append(pl.BlockSpec((None, ckv.shape[1], C_KV_RANK), lambda b, i: (b, 0, 0)))
        in_specs.append(pl.BlockSpec((None, kr.shape[1], C_ROPE), lambda b, i: (b, 0, 0)))
        args += [ckv, kr]
    in_specs.append(pl.BlockSpec(w_kvb.shape, lambda b, i: (0, 0)))
    args.append(w_kvb)
    return pl.pallas_call(
        functools.partial(_attn_kernel, n_parts=len(kv_parts),
                          scale=(C_NOPE + C_ROPE) ** -0.5),
        out_shape=jax.ShapeDtypeStruct((bsz, n, C_HEADS * C_DV), BF16),
        grid=(bsz, n // tq),
        in_specs=in_specs,
        out_specs=pl.BlockSpec((None, tq, C_HEADS * C_DV), lambda b, i: (b, i, 0)),
        scratch_shapes=[pltpu.VMEM((C_HEADS, nk, QCAT), BF16),
                        pltpu.VMEM((nk, C_HEADS * C_DV), BF16)],
        compiler_params=_cparams(("arbitrary", "arbitrary")),
        name="mla_attention",
    )(*args)


def _l0_w_in_layout(w_in):
    sizes = (256, 256, 512, 512, 32, 256, 256, 512, 512, 16)
    offs = np.concatenate([[0], np.cumsum(sizes)])
    seg = [w_in[:, offs[i]:offs[i + 1]] for i in range(len(sizes))]
    pad = jnp.zeros((w_in.shape[0], ZG_W - 48), w_in.dtype)
    cols = seg[0:4] + seg[5:9] + [seg[4], seg[9], pad]
    return jnp.concatenate(cols, axis=1).astype(BF16)


def _l1_wq_layout(w_qb):
    blocks = []
    zero = jnp.zeros((w_qb.shape[0], C_ROPE), w_qb.dtype)
    for hd in range(C_HEADS):
        base = hd * (C_NOPE + C_ROPE)
        nope = w_qb[:, base:base + C_NOPE]
        rope = w_qb[:, base + C_NOPE:base + C_NOPE + C_ROPE]
        blocks += [nope, rope, zero] if hd % 2 == 0 else [nope, zero, rope]
    return jnp.concatenate(blocks, axis=1).astype(BF16)


def _l1_wkvb_layout(w_kvb):
    w = w_kvb.reshape(C_KV_RANK, C_HEADS, C_NOPE + C_DV)
    k = w[:, :, :C_NOPE].reshape(C_KV_RANK, C_HEADS * C_NOPE)
    v = w[:, :, C_NOPE:].reshape(C_KV_RANK, C_HEADS * C_DV)
    return jnp.concatenate([k, v], axis=1).astype(BF16)


def _rope_tables(n):
    pos = np.arange(n)
    row = (pos // GRID_W).astype(np.float64)
    col = (pos % GRID_W).astype(np.float64)
    half = C_ROPE // 2
    inv = 1.0 / (ROPE_THETA ** (np.arange(0, half, 2, dtype=np.float64) / half))
    ang_r = row[:, None] * inv[None, :]
    ang_c = col[:, None] * inv[None, :]
    cos64 = np.concatenate([np.cos(ang_r)] * 2 + [np.cos(ang_c)] * 2, axis=1)
    sin64 = np.concatenate([-np.sin(ang_r), np.sin(ang_r), -np.sin(ang_c), np.sin(ang_c)], axis=1)
    return (jnp.asarray(np.concatenate([cos64] * 2, axis=1), F32),
            jnp.asarray(np.concatenate([sin64] * 2, axis=1), F32))


def _gate_weight(w, first_row):
    return jnp.pad(w, ((first_row, ZG_W - first_row - A_GATE_RANK), (0, 0)))


def _state_t(s):
    b = s.shape[0]
    return s.transpose(0, 3, 1, 2).reshape(b, 128, HK)


def _n_rows(n_fwd, n_bwd):
    b = n_fwd.shape[0]
    rows = jnp.zeros((b, 8, HK), F32)
    return rows.at[:, 0, :].set(n_fwd.reshape(b, HK)).at[:, 1, :].set(n_bwd.reshape(b, HK))


def _m_rows(m_fwd, m_bwd):
    m = jnp.concatenate([m_fwd, m_bwd], axis=1)
    return jnp.broadcast_to(m[:, :, None], (m.shape[0], 8, 128))


def kernel(x_prompt, x_sample, state_l0_gla_fwd, state_l0_gla_bwd, state_l0_mlstm_c_fwd, state_l0_mlstm_n_fwd, state_l0_mlstm_m_fwd, state_l0_mlstm_c_bwd, state_l0_mlstm_n_bwd, state_l0_mlstm_m_bwd, cache_l1_ckv, cache_l1_krope, c, c_ctx, l0_w_mod, l0_b_mod, l0_g_pre_mix, l0_g_post_mix, l0_g_pre_ffn, l0_g_post_ffn, l0_w_in, l0_gla_w_gate_f, l0_gla_b_gate_f, l0_gla_w_gate_b, l0_gla_b_gate_b, l0_gla_g_norm, l0_mlstm_b_gates, l0_mlstm_g_norm, l0_w_out, l0_ffn_w_up, l0_ffn_conv_w, l0_ffn_conv_b, l0_ffn_w_down, l1_w_mod, l1_b_mod, l1_g_pre_mix, l1_g_post_mix, l1_g_pre_ffn, l1_g_post_ffn, l1_w_in, l1_g_q_norm, l1_w_qb, l1_g_kv_norm, l1_w_kvb, l1_w_out, l1_ffn_w_up, l1_ffn_conv_w, l1_ffn_conv_b, l1_ffn_w_down):
    bp, sp, d = x_prompt.shape
    bs, ss, _ = x_sample.shape
    row = lambda v: v.reshape(1, -1)

    cond8 = jnp.concatenate([c_ctx[None, :], c, jnp.zeros((8 - 1 - bs, d), F32)], axis=0)
    mods = (_modulation(cond8, l0_w_mod, l0_b_mod), _modulation(cond8, l1_w_mod, l1_b_mod))

    paths = {
        "ctx": dict(x=x_prompt.reshape(bp * sp, d), b=bp, n=sp, first_row=0, groups=1, tm=512),
        "lat": dict(x=x_sample.reshape(bs * ss, d), b=bs, n=ss, first_row=1, groups=bs, tm=512),
    }

    w_in0 = _l0_w_in_layout(l0_w_in)
    wgf = _gate_weight(l0_gla_w_gate_f, 0)
    wgb = _gate_weight(l0_gla_w_gate_b, A_GATE_RANK)
    bg = jnp.pad(l0_mlstm_b_gates, (GATE_COL, ZG_W - GATE_COL - 16)).reshape(1, ZG_W)
    w_out0 = l0_w_out.astype(BF16)
    w_in1 = jnp.pad(l1_w_in, ((0, 0), (0, L1_IN_PAD - l1_w_in.shape[1]))).astype(BF16)
    w_q1 = _l1_wq_layout(l1_w_qb)
    w_kvb1 = _l1_wkvb_layout(l1_w_kvb)
    w_out1 = l1_w_out.astype(BF16)
    ffn = ((l0_ffn_w_up.astype(BF16), l0_ffn_conv_w, row(l0_ffn_conv_b), l0_ffn_w_down.astype(BF16)),
           (l1_ffn_w_up.astype(BF16), l1_ffn_conv_w, row(l1_ffn_conv_b), l1_ffn_w_down.astype(BF16)))
    norms = ((row(l0_g_pre_mix), row(l0_g_post_mix), row(l0_g_pre_ffn), row(l0_g_post_ffn)),
             (row(l1_g_pre_mix), row(l1_g_post_mix), row(l1_g_pre_ffn), row(l1_g_post_ffn)))

    gla_init = (_state_t(state_l0_gla_fwd), _state_t(state_l0_gla_bwd))
    mlstm_init = (_state_t(state_l0_mlstm_c_fwd), _state_t(state_l0_mlstm_c_bwd),
                  _n_rows(state_l0_mlstm_n_fwd, state_l0_mlstm_n_bwd),
                  _m_rows(state_l0_mlstm_m_fwd, state_l0_mlstm_m_bwd))
    tables = _rope_tables(ss)

    results = {}
    for name, p in paths.items():
        x, b, n, tm = p["x"], p["b"], p["n"], p["tm"]
        first_row, groups = p["first_row"], p["groups"]
        is_ctx = name == "ctx"
        ffn_tm = tm if is_ctx else n
        g_pre, g_post, g_pre2, g_post2 = norms[0]

        za, zb, zg = _proj_l0(x, g_pre, mods[0], first_row, groups, w_in0, tm)
        za, zb, zg = (t.reshape(b, n, -1) for t in (za, zb, zg))
        oa, sa_f, sa_b = _gla(za, zg, wgf, wgb, row(l0_gla_b_gate_f), row(l0_gla_b_gate_b),
                              row(l0_gla_g_norm), None if is_ctx else gla_init)
        om, c_f, c_b, n_rows, m_rows = _mlstm(zb, zg, bg, row(l0_mlstm_g_norm),
                                              None if is_ctx else mlstm_init)
        x = _mix_ffn([oa.reshape(b * n, HV), om.reshape(b * n, HV)], w_out0, x, mods[0], first_row,
                     groups, g_post, g_pre2, *ffn[0], g_post2, ffn_tm, n)

        g_pre, g_post, g_pre2, g_post2 = norms[1]
        q, ckv, kr = _proj_l1(x, g_pre, mods[1], first_row, groups, w_in1, row(l1_g_q_norm), w_q1,
                              row(l1_g_kv_norm), None if is_ctx else tables, tm)
        ckv3 = ckv.reshape(b, n, C_KV_RANK)
        kr3 = kr.reshape(b, n, C_ROPE)
        kv_parts = [(ckv3, kr3)] if is_ctx else [(cache_l1_ckv, cache_l1_krope), (ckv3, kr3)]
        o = _attention(q.reshape(b, n, -1), kv_parts, w_kvb1, 256)
        x = _mix_ffn([o.reshape(b * n, C_HEADS * C_DV)], w_out1, x, mods[1], first_row,
                     groups, g_post, g_pre2, *ffn[1], g_post2, ffn_tm, n)

        results[name] = dict(y=x.reshape(b, n, d), gla=(sa_f, sa_b), c=(c_f, c_b),
                             n=n_rows, m=m_rows, ckv=ckv3, kr=kr3)

    r = results["ctx"]
    n_fwd = r["n"][:, 0, :].reshape(bp, B_HEADS, B_DK)
    n_bwd = r["n"][:, 1, :].reshape(bp, B_HEADS, B_DK)
    m_fwd = r["m"][:, 0:B_HEADS, 0]
    m_bwd = r["m"][:, B_HEADS:2 * B_HEADS, 0]
    return (r["y"], results["lat"]["y"], r["gla"][0], r["gla"][1],
            r["c"][0], n_fwd, m_fwd, r["c"][1], n_bwd, m_bwd, r["ckv"], r["kr"])
```

```python
import functools

import numpy as np
import jax
import jax.numpy as jnp
from jax import lax
from jax.experimental import pallas as pl
from jax.experimental.pallas import tpu as pltpu

F32 = jnp.float32
BF16 = jnp.bfloat16

D_MODEL = 1024
EPS = 1e-6
LOG2E = 1.4426950408889634
N_MOD = 6
GRID_W = 64
ROPE_THETA = 10000.0

A_HEADS, A_DK, A_DV, A_GATE_RANK, A_GATE_TEMP = 4, 64, 128, 16, 16.0
B_HEADS, B_DK, B_DV = 4, 64, 128
C_HEADS, C_Q_RANK, C_KV_RANK, C_NOPE, C_ROPE, C_DV = 8, 384, 256, 128, 64, 128
D_FF = 2816
CONV_W = 3

HEADS = 4
HK = HEADS * 64
HV = HEADS * 128
ZA_W = 1536
ZG_W = 128
GATE_COL = 32
RCHUNK = 128
L1_IN_PAD = 768
QCAT = 256

VMEM_LIMIT = 56 * 1024 * 1024


def _cparams(sem, vmem=VMEM_LIMIT):
    return pltpu.CompilerParams(dimension_semantics=sem, vmem_limit_bytes=vmem)


def _dot(a, b):
    return jnp.dot(a, b, preferred_element_type=F32)


def _dot_nt(a, b):
    return lax.dot_general(a, b, (((1,), (1,)), ((), ())), preferred_element_type=F32)


def _dot_tn(a, b):
    return lax.dot_general(a, b, (((0,), (0,)), ((), ())), preferred_element_type=F32)


def _split3(x):
    hi = x.astype(BF16)
    r1 = x - hi.astype(F32)
    mid = r1.astype(BF16)
    lo = (r1 - mid.astype(F32)).astype(BF16)
    return hi, mid, lo


def _tri_dot(tri, x):
    hi, mid, lo = _split3(x)
    return _dot(tri, hi) + _dot(tri, mid) + _dot(tri, lo)


def _dot_tri(x, tri):
    hi, mid, lo = _split3(x)
    return _dot(hi, tri) + _dot(mid, tri) + _dot(lo, tri)


def _dot_hl(a, b):
    ah = a.astype(BF16)
    al = (a - ah.astype(F32)).astype(BF16)
    bh = b.astype(BF16)
    bl = (b - bh.astype(F32)).astype(BF16)
    return _dot(ah, bh) + _dot(ah, bl) + _dot(al, bh)


def _rms(x, g):
    return x * lax.rsqrt(jnp.mean(x * x, axis=-1, keepdims=True) + EPS) * g


def _sigmoid(x):
    return 1.0 / (1.0 + jnp.exp(-x))


def _silu(x):
    return x * _sigmoid(x)


def _log_sigmoid(x):
    return jnp.minimum(x, 0.0) - jnp.log(1.0 + jnp.exp(-jnp.abs(x)))


def _tri_masks(n):
    r = lax.broadcasted_iota(jnp.int32, (n, n), 0)
    c = lax.broadcasted_iota(jnp.int32, (n, n), 1)
    return c <= r, c >= r


def _mod_kernel(cond_ref, w_ref, b_ref, o_ref):
    s = _silu(cond_ref[...])
    o_ref[...] = _dot(s.astype(BF16), w_ref[...].astype(BF16)) + b_ref[...]


def _modulation(cond8, w_mod, b_mod):
    d = D_MODEL
    out = pl.pallas_call(
        _mod_kernel,
        out_shape=jax.ShapeDtypeStruct((N_MOD, 8, d), F32),
        grid=(N_MOD,),
        in_specs=[pl.BlockSpec((8, d), lambda j: (0, 0)),
                  pl.BlockSpec((d, d), lambda j: (0, j)),
                  pl.BlockSpec((1, d), lambda j: (0, j))],
        out_specs=pl.BlockSpec((None, 8, d), lambda j: (j, 0, 0)),
        compiler_params=_cparams(("arbitrary",)),
        name="modulation",
    )(cond8, w_mod, b_mod.reshape(1, N_MOD * d))
    return out.reshape(N_MOD, 8, 1, d)


MOD_SHIFT1, MOD_SCALE1, MOD_GATE1, MOD_SHIFT2, MOD_SCALE2, MOD_GATE2 = range(N_MOD)


def _mod_spec(which, first_row, per):
    return pl.BlockSpec((None, None, 1, D_MODEL), lambda i: (which, first_row + i // per, 0, 0))


L0_GROUP_ROWS = ((0, ZA_W), (ZA_W + 32, 2 * ZA_W + 32))
L0_LR_ROWS = (ZA_W, ZA_W + 32)
L0_GATE_ROWS = (2 * ZA_W + 32, 2 * ZA_W + 48)


def _p0_kernel(x_ref, g_ref, sc_ref, sh_ref, wt_ref, za_ref, zb_ref, zg_ref, w_ref):
    @pl.when(pl.program_id(0) == 0)
    def _():
        blk = 256
        for gi, (r0, r1) in enumerate(L0_GROUP_ROWS):
            for j in range((r1 - r0) // blk):
                rows = wt_ref[r0 + j * blk:r0 + (j + 1) * blk, :]
                w_ref[:, gi * ZA_W + j * blk:gi * ZA_W + (j + 1) * blk] = rows.T.astype(BF16)
        tail = jnp.concatenate([wt_ref[L0_LR_ROWS[0]:L0_LR_ROWS[1], :],
                                wt_ref[L0_GATE_ROWS[0]:L0_GATE_ROWS[1], :],
                                jnp.zeros((ZG_W - 48, D_MODEL), F32)], axis=0)
        w_ref[:, 2 * ZA_W:2 * ZA_W + ZG_W] = tail.T.astype(BF16)

    h = _rms(x_ref[...], g_ref[...]) * (1.0 + sc_ref[...]) + sh_ref[...]
    hb = h.astype(BF16)
    za_ref[...] = _dot(hb, w_ref[:, 0:ZA_W])
    zb_ref[...] = _dot(hb, w_ref[:, ZA_W:2 * ZA_W])
    zg_ref[...] = _dot(hb, w_ref[:, 2 * ZA_W:2 * ZA_W + ZG_W])


def _proj_l0(x, g, mod, first_row, groups, w_t, tm):
    t = x.shape[0]
    per = (t // tm) // groups
    row = lambda i: (i, 0)
    fixed = lambda i: (0, 0)
    return pl.pallas_call(
        _p0_kernel,
        out_shape=(jax.ShapeDtypeStruct((t, ZA_W), F32),
                   jax.ShapeDtypeStruct((t, ZA_W), F32),
                   jax.ShapeDtypeStruct((t, ZG_W), F32)),
        grid=(t // tm,),
        in_specs=[pl.BlockSpec((tm, D_MODEL), row),
                  pl.BlockSpec((1, D_MODEL), fixed),
                  _mod_spec(MOD_SCALE1, first_row, per),
                  _mod_spec(MOD_SHIFT1, first_row, per),
                  _resident(w_t.shape)],
        out_specs=(pl.BlockSpec((tm, ZA_W), row),
                   pl.BlockSpec((tm, ZA_W), row),
                   pl.BlockSpec((tm, ZG_W), row)),
        scratch_shapes=[pltpu.VMEM((D_MODEL, 2 * ZA_W + ZG_W), BF16)],
        compiler_params=_cparams(("arbitrary",)),
        name="proj_l0",
    )(x, g, mod, mod, w_t)


def _head_lane_mask(shape, h):
    lane = lax.broadcasted_iota(jnp.int32, shape, len(shape) - 1)
    return (lane >= h * 64) & (lane < (h + 1) * 64)


def _head_row(h):
    return jnp.where(_head_lane_mask((1, HK), h), 1.0, 0.0).astype(BF16)


def _gla_prepare(za_ref, la_ref, r0, rev, tri):
    L = RCHUNK
    rows = pl.ds(r0, L)
    b = _tri_dot(tri, la_ref[rows, :])
    mid = L // 2
    ref = b[mid:mid + 1, :]
    b_end = b[0:1, :] if rev else b[L - 1:L, :]
    q = za_ref[rows, 0:HK]
    k = za_ref[rows, HK:2 * HK]
    v = za_ref[rows, 2 * HK:2 * HK + HV].astype(BF16)
    qe = (q * (jnp.exp(b - ref) * (A_DK ** -0.5))).astype(BF16)
    ke = (k * jnp.exp(ref - b)).astype(BF16)
    qm = jnp.concatenate([qe * _head_row(h) for h in range(HEADS)], axis=0)
    rt = _dot_tn(v, ke)
    upd = None
    for h in range(HEADS):
        blk = rt[h * 128:(h + 1) * 128, :]
        blk = jnp.where(_head_lane_mask(blk.shape, h), blk, 0.0)
        upd = blk if upd is None else upd + blk
    return dict(rows=rows, qm=qm, ke=ke, v=v, upd=upd * jnp.exp(b_end - ref),
                decay=jnp.exp(b_end), eref=jnp.exp(ref))


def _gla_apply(items):
    L = RCHUNK
    sts = [st_ref[...] for _, st_ref, _, _ in items]
    o1s = []
    for (p, _, _, _), st in zip(items, sts):
        rhs = jnp.concatenate([p["ke"], (st * p["eref"]).astype(BF16)], axis=0)
        o1s.append(_dot_nt(p["qm"], rhs))
    for h in range(HEADS):
        for (p, _, o_ref, mask), o1 in zip(items, o1s):
            blk = o1[h * L:(h + 1) * L, :]
            sc = jnp.where(mask, blk[:, 0:L], 0.0).astype(BF16)
            o_ref[p["rows"], h * 128:(h + 1) * 128] = (
                blk[:, L:L + 128] + _dot(sc, p["v"][:, h * 128:(h + 1) * 128]))
    for (p, st_ref, _, _), st in zip(items, sts):
        st_ref[...] = st * p["decay"] + p["upd"]


def _gla_kernel(*refs, n, has_init):
    if has_init:
        (za_ref, zg_ref, wgf_ref, wgb_ref, bgf_ref, bgb_ref, gn_ref, sf0_ref, sb0_ref,
         o_ref, sf_ref, sb_ref, laf_ref, lab_ref, of_ref, ob_ref, stf_ref, stb_ref) = refs
    else:
        (za_ref, zg_ref, wgf_ref, wgb_ref, bgf_ref, bgb_ref, gn_ref,
         o_ref, sf_ref, sb_ref, laf_ref, lab_ref, of_ref, ob_ref, stf_ref, stb_ref) = refs
    L = RCHUNK
    nc = n // L
    zg = zg_ref[...]
    xf = _dot_hl(zg, wgf_ref[...]) + bgf_ref[...]
    xb = _dot_hl(zg, wgb_ref[...]) + bgb_ref[...]
    laf_ref[...] = _log_sigmoid(xf) / A_GATE_TEMP
    lab_ref[...] = _log_sigmoid(xb) / A_GATE_TEMP
    if has_init:
        stf_ref[...] = sf0_ref[...]
        stb_ref[...] = sb0_ref[...]
    else:
        stf_ref[...] = jnp.zeros_like(stf_ref)
        stb_ref[...] = jnp.zeros_like(stb_ref)

    m_lo, m_up = _tri_masks(L)
    tri_lo = jnp.where(m_lo, 1.0, 0.0).astype(BF16)
    tri_up = jnp.where(m_up, 1.0, 0.0).astype(BF16)

    def scan_body(g, carry):
        waves = []
        for w in range(2):
            c = 2 * g + w
            rf = pl.multiple_of(c * L, L)
            rb = pl.multiple_of((nc - 1 - c) * L, L)
            waves.append((_gla_prepare(za_ref, laf_ref, rf, False, tri_lo),
                          _gla_prepare(za_ref, lab_ref, rb, True, tri_up)))
        for pf, pb in waves:
            _gla_apply([(pf, stf_ref, of_ref, m_lo), (pb, stb_ref, ob_ref, m_up)])
        return carry

    if nc == 2:
        scan_body(0, 0)
    else:
        lax.fori_loop(0, nc // 2, scan_body, 0)

    gn = gn_ref[...]

    def out_body(c, carry):
        rows = pl.ds(pl.multiple_of(c * L, L), L)
        for h in range(HEADS):
            cols = slice(h * 128, (h + 1) * 128)
            o = of_ref[rows, cols] + ob_ref[rows, cols]
            gate = za_ref[rows, 2 * HK + HV + h * 128:2 * HK + HV + (h + 1) * 128]
            o_ref[rows, cols] = (_rms(o, gn) * _silu(gate)).astype(o_ref.dtype)
        return carry

    if nc == 2:
        for c in range(nc):
            out_body(c, 0)
    else:
        lax.fori_loop(0, nc, out_body, 0)

    stf_t = stf_ref[...].T
    stb_t = stb_ref[...].T
    for h in range(HEADS):
        sf_ref[h] = stf_t[h * 64:(h + 1) * 64, :]
        sb_ref[h] = stb_t[h * 64:(h + 1) * 64, :]


def _gla(za, zg, wgf, wgb, bgf, bgb, gn, init):
    bsz, n, _ = za.shape
    has_init = init is not None
    per_b3 = lambda b: (b, 0, 0)
    per_b4 = lambda b: (b, 0, 0, 0)
    fixed = lambda b: (0, 0)
    in_specs = [pl.BlockSpec((None, n, ZA_W), per_b3),
                pl.BlockSpec((None, n, ZG_W), per_b3),
                pl.BlockSpec((ZG_W, HK), fixed),
                pl.BlockSpec((ZG_W, HK), fixed),
                pl.BlockSpec((1, HK), fixed),
                pl.BlockSpec((1, HK), fixed),
                pl.BlockSpec((1, 128), fixed)]
    args = [za, zg, wgf, wgb, bgf, bgb, gn]
    if has_init:
        in_specs += [pl.BlockSpec((None, 128, HK), per_b3)] * 2
        args += list(init)
    return pl.pallas_call(
        functools.partial(_gla_kernel, n=n, has_init=has_init),
        out_shape=(jax.ShapeDtypeStruct((bsz, n, HV), BF16),
                   jax.ShapeDtypeStruct((bsz, HEADS, 64, 128), F32),
                   jax.ShapeDtypeStruct((bsz, HEADS, 64, 128), F32)),
        grid=(bsz,),
        in_specs=in_specs,
        out_specs=(pl.BlockSpec((None, n, HV), per_b3),
                   pl.BlockSpec((None, HEADS, 64, 128), per_b4),
                   pl.BlockSpec((None, HEADS, 64, 128), per_b4)),
        scratch_shapes=[pltpu.VMEM((n, HK), F32), pltpu.VMEM((n, HK), F32),
                        pltpu.VMEM((n, HV), F32), pltpu.VMEM((n, HV), F32),
                        pltpu.VMEM((128, HK), F32), pltpu.VMEM((128, HK), F32)],
        compiler_params=_cparams(("arbitrary",)),
        name="gla",
    )(*args)


T_M, T_MOUT, T_G, T_SRCW, T_MPREV, T_CARRY, T_USED = 0, 8, 16, 24, 32, 40, 48


def _lane_scan(x, op, fill, is_fwd, pos, n):
    k = 1
    while k < RCHUNK:
        pf = jnp.where(pos >= k, pltpu.roll(x, k, axis=1), fill)
        sf = jnp.where(pos < RCHUNK - k, pltpu.roll(x, n - k, axis=1), fill)
        x = op(x, jnp.where(is_fwd, pf, sf))
        k *= 2
    return x


def _mlstm_gate_tables(g, m0, gt_ref, tab_ref, col_ref, n):
    L = RCHUNK
    nc = n // L
    for c in range(nc):
        gt_ref[:, c * L:(c + 1) * L] = g[c * L:(c + 1) * L, :].T
        tab_ref[c, T_USED:, :] = jnp.zeros((ZG_W - T_USED, L), F32)
    gi = gt_ref[GATE_COL:GATE_COL + 8, :]
    lf = gt_ref[GATE_COL + 8:GATE_COL + 16, :]
    is_fwd = lax.broadcasted_iota(jnp.int32, (8, n), 0) < HEADS
    pos = lax.broadcasted_iota(jnp.int32, (8, n), 1) % L
    b = _lane_scan(lf, jnp.add, 0.0, is_fwd, pos, n)
    gg = gi - b
    cmax = _lane_scan(gg, jnp.maximum, -jnp.inf, is_fwd, pos, n)
    fwd_l = lax.broadcasted_iota(jnp.int32, (8, L), 0) < HEADS
    mp = m0
    for j in range(nc):
        cf, cb = j, nc - 1 - j

        def pick(x):
            return jnp.where(fwd_l, x[:, cf * L:(cf + 1) * L], x[:, cb * L:(cb + 1) * L])

        gj, bj, cj, lj = pick(gg), pick(b), pick(cmax), pick(lf)
        mx = jnp.maximum(mp, jnp.max(gj, axis=1, keepdims=True))
        mj = jnp.maximum(cj, mp)
        blocks = (mj, bj + mj, gj, jnp.exp(gj - mx),
                  jnp.broadcast_to(mp, (8, L)), jnp.broadcast_to(jnp.exp(mp - mx), (8, L)))
        for t, val in enumerate(blocks):
            tab_ref[cf, 8 * t:8 * t + HEADS, :] = val[0:HEADS]
            tab_ref[cb, 8 * t + HEADS:8 * t + 8, :] = val[HEADS:8]
        mp = jnp.sum(lj, axis=1, keepdims=True) + mx
    for c in range(nc):
        col_ref[c * L:(c + 1) * L, :] = tab_ref[c].T
    return mp


def _mlstm_prepare(zb_ref, tab_ref, col_ref, c, d, mask):
    L = RCHUNK
    rows = pl.ds(pl.multiple_of(c * L, L), L)
    qb = zb_ref[rows, 0:HK].astype(BF16)
    kb = (zb_ref[rows, HK:2 * HK] * (B_DK ** -0.5)).astype(BF16)
    v = zb_ref[rows, 2 * HK:2 * HK + HV]
    vb = v.astype(BF16)
    vt = v.T
    qm = jnp.concatenate([qb * _head_row(h) for h in range(HEADS)], axis=0)
    s = _dot_nt(qm, kb)
    ones = jnp.ones((L, 128), BF16)
    pv, w_inter, e_negm = [], [], []
    upd = None
    carry = None
    for h in range(HEADS):
        r = 4 * d + h
        m_rep = jnp.broadcast_to(col_ref[rows, T_M + r:T_M + r + 1], (L, 128))
        mout_rep = jnp.broadcast_to(col_ref[rows, T_MOUT + r:T_MOUT + r + 1], (L, 128))
        g_row = tab_ref[c, T_G + r:T_G + r + 1, :]
        w = jnp.exp(jnp.where(mask, g_row - m_rep, -jnp.inf))
        qk = (s[h * L:(h + 1) * L, :] * w).astype(BF16)
        v1 = jnp.concatenate([vb[:, h * 128:(h + 1) * 128], ones], axis=1)
        pv.append(_dot(qk, v1))
        w_inter.append(jnp.exp(tab_ref[c, T_MPREV + r:T_MPREV + r + 1, :] - m_rep))
        e_negm.append(jnp.exp(-mout_rep))
        sw = tab_ref[c, T_SRCW + r:T_SRCW + r + 1, :]
        lhs = jnp.concatenate([vt[h * 128:(h + 1) * 128, :] * sw, jnp.broadcast_to(sw, (128, L))],
                              axis=0).astype(BF16)
        blk = _dot(lhs, kb)
        blk = jnp.where(_head_lane_mask(blk.shape, h), blk, 0.0)
        upd = blk if upd is None else upd + blk
        cr = tab_ref[c, T_CARRY + r:T_CARRY + r + 1, :]
        cr = jnp.where(_head_lane_mask((1, HK), h), jnp.concatenate([cr, cr], axis=1), 0.0)
        carry = cr if carry is None else carry + cr
    return dict(rows=rows, qm=qm, pv=pv, w_inter=w_inter, e_negm=e_negm, upd=upd, carry=carry)


def _mlstm_apply(items):
    L = RCHUNK
    sts = [st_ref[...] for _, st_ref, _ in items]
    a_all = [_dot_nt(p["qm"], st.astype(BF16)) for (p, _, _), st in zip(items, sts)]
    for h in range(HEADS):
        for (p, _, h_ref), a in zip(items, a_all):
            blk = a[h * L:(h + 1) * L, :]
            pv, w, e = p["pv"][h], p["w_inter"][h], p["e_negm"][h]
            num = w * blk[:, 0:128] + pv[:, 0:128]
            den = w * blk[:, 128:256] + pv[:, 128:256]
            h_ref[p["rows"], h * 128:(h + 1) * 128] = num / jnp.maximum(jnp.abs(den), e)
    for (p, st_ref, _), st in zip(items, sts):
        st_ref[...] = p["carry"] * st + p["upd"]


def _mlstm_kernel(*refs, n, has_init):
    if has_init:
        (zb_ref, zg_ref, zgn_ref, bg_ref, gn_ref, cf0_ref, cb0_ref, n0_ref, m0_ref, m0n_ref,
         o_ref, cf_ref, cb_ref, n_out_ref, m_out_ref,
         gt_ref, tab_a, tab_b, col_a, col_b, mf_a, mf_b, hf_ref, hb_ref, stf_ref, stb_ref) = refs
    else:
        (zb_ref, zg_ref, zgn_ref, bg_ref, gn_ref,
         o_ref, cf_ref, cb_ref, n_out_ref, m_out_ref,
         gt_ref, tab_a, tab_b, col_a, col_b, mf_a, mf_b, hf_ref, hb_ref, stf_ref, stb_ref) = refs
    L = RCHUNK
    nc = n // L
    step = pl.program_id(0)

    def fill_tables(gate_ref, m0, tab_ref, col_ref, mf_ref):
        g = gate_ref[...] + bg_ref[...]
        lane = lax.broadcasted_iota(jnp.int32, g.shape, 1)
        is_forget = (lane >= GATE_COL + 8) & (lane < GATE_COL + 16)
        g = jnp.where(is_forget, _log_sigmoid(g), g)
        m_fin = _mlstm_gate_tables(g, m0, gt_ref, tab_ref, col_ref, n)
        mf_ref[...] = jnp.broadcast_to(m_fin, (8, 128))

    if has_init:
        m0, m0_next = m0_ref[:, 0:1], m0n_ref[:, 0:1]
        stf_ref[...] = jnp.concatenate(
            [cf0_ref[...], jnp.broadcast_to(n0_ref[0:1, :], (128, HK))], axis=0)
        stb_ref[...] = jnp.concatenate(
            [cb0_ref[...], jnp.broadcast_to(n0_ref[1:2, :], (128, HK))], axis=0)
    else:
        m0 = m0_next = jnp.zeros((8, 1), F32)
        stf_ref[...] = jnp.zeros_like(stf_ref)
        stb_ref[...] = jnp.zeros_like(stb_ref)

    @pl.when(step == 0)
    def _():
        fill_tables(zg_ref, m0, tab_a, col_a, mf_a)

    m_lo, m_up = _tri_masks(L)

    def run(tab_ref, col_ref, mf_ref, tab_next, col_next, mf_next):
        fill_tables(zgn_ref, m0_next, tab_next, col_next, mf_next)

        def scan_body(gidx, carry):
            waves = []
            for w in range(2):
                c = 2 * gidx + w
                waves.append((_mlstm_prepare(zb_ref, tab_ref, col_ref, c, 0, m_lo),
                              _mlstm_prepare(zb_ref, tab_ref, col_ref, nc - 1 - c, 1, m_up)))
            for pf, pb in waves:
                _mlstm_apply([(pf, stf_ref, hf_ref), (pb, stb_ref, hb_ref)])
            return carry

        if nc == 2:
            scan_body(0, 0)
            for c in range(nc):
                out_body(c, 0)
        else:
            lax.fori_loop(0, nc // 2, scan_body, 0)
            lax.fori_loop(0, nc, out_body, 0)
        m_out_ref[...] = mf_ref[...]

    gn = gn_ref[...]

    def out_body(c, carry):
        rows = pl.ds(pl.multiple_of(c * L, L), L)
        for h in range(HEADS):
            cols = slice(h * 128, (h + 1) * 128)
            hh = hf_ref[rows, cols] + hb_ref[rows, cols]
            gate = zb_ref[rows, 2 * HK + HV + h * 128:2 * HK + HV + (h + 1) * 128]
            o_ref[rows, cols] = (_sigmoid(gate) * _rms(hh, gn)).astype(o_ref.dtype)
        return carry

    @pl.when(step % 2 == 0)
    def _():
        run(tab_a, col_a, mf_a, tab_b, col_b, mf_b)

    @pl.when(step % 2 == 1)
    def _():
        run(tab_b, col_b, mf_b, tab_a, col_a, mf_a)

    stf = stf_ref[...]
    stb = stb_ref[...]
    ctf_t = stf[0:128, :].T
    ctb_t = stb[0:128, :].T
    for h in range(HEADS):
        cf_ref[h] = ctf_t[h * 64:(h + 1) * 64, :]
        cb_ref[h] = ctb_t[h * 64:(h + 1) * 64, :]
    n_out_ref[...] = jnp.concatenate(
        [stf[128:129, :], stb[128:129, :], jnp.zeros((6, HK), F32)], axis=0)


def _mlstm(zb, zg, bg, gn, init):
    bsz, n, _ = zb.shape
    has_init = init is not None
    nc = n // RCHUNK
    per_b3 = lambda b: (b, 0, 0)
    next_b3 = lambda b: (jnp.minimum(b + 1, bsz - 1), 0, 0)
    per_b4 = lambda b: (b, 0, 0, 0)
    fixed = lambda b: (0, 0)
    in_specs = [pl.BlockSpec((None, n, ZA_W), per_b3),
                pl.BlockSpec((None, n, ZG_W), per_b3),
                pl.BlockSpec((None, n, ZG_W), next_b3),
                pl.BlockSpec((1, ZG_W), fixed),
                pl.BlockSpec((1, 128), fixed)]
    args = [zb, zg, zg, bg, gn]
    if has_init:
        cf0, cb0, n0, m0 = init
        in_specs += [pl.BlockSpec((None, 128, HK), per_b3)] * 2
        in_specs += [pl.BlockSpec((None, 8, HK), per_b3), pl.BlockSpec((None, 8, 128), per_b3),
                     pl.BlockSpec((None, 8, 128), next_b3)]
        args += [cf0, cb0, n0, m0, m0]
    return pl.pallas_call(
        functools.partial(_mlstm_kernel, n=n, has_init=has_init),
        out_shape=(jax.ShapeDtypeStruct((bsz, n, HV), BF16),
                   jax.ShapeDtypeStruct((bsz, HEADS, 64, 128), F32),
                   jax.ShapeDtypeStruct((bsz, HEADS, 64, 128), F32),
                   jax.ShapeDtypeStruct((bsz, 8, HK), F32),
                   jax.ShapeDtypeStruct((bsz, 8, 128), F32)),
        grid=(bsz,),
        in_specs=in_specs,
        out_specs=(pl.BlockSpec((None, n, HV), per_b3),
                   pl.BlockSpec((None, HEADS, 64, 128), per_b4),
                   pl.BlockSpec((None, HEADS, 64, 128), per_b4),
                   pl.BlockSpec((None, 8, HK), per_b3),
                   pl.BlockSpec((None, 8, 128), per_b3)),
        scratch_shapes=[pltpu.VMEM((ZG_W, n), F32),
                        pltpu.VMEM((nc, ZG_W, RCHUNK), F32), pltpu.VMEM((nc, ZG_W, RCHUNK), F32),
                        pltpu.VMEM((n, ZG_W), F32), pltpu.VMEM((n, ZG_W), F32),
                        pltpu.VMEM((8, 128), F32), pltpu.VMEM((8, 128), F32),
                        pltpu.VMEM((n, HV), F32), pltpu.VMEM((n, HV), F32),
                        pltpu.VMEM((2 * 128, HK), F32), pltpu.VMEM((2 * 128, HK), F32)],
        compiler_params=_cparams(("arbitrary",)),
        name="mlstm",
    )(*args)


def _mix_ffn_kernel(*refs, n_parts, seq, tf):
    parts = refs[:n_parts]
    (wo_ref, x_ref, gt1_ref, gpost1_ref, gpre2_ref, sc2_ref, sh2_ref,
     wu_ref, cw_ref, cb_ref, wd_ref, gt2_ref, gpost2_ref, o_ref, act_ref) = refs[n_parts:]
    tm = x_ref.shape[0]
    m = None
    off = 0
    for p in parts:
        kp = p.shape[1]
        t = _dot(p[...], wo_ref[off:off + kp, :])
        m = t if m is None else m + t
        off += kp
    x1 = x_ref[...] + gt1_ref[...] * _rms(m, gpost1_ref[...])
    o_ref[...] = x1
    hb = (_rms(x1, gpre2_ref[...]) * (1.0 + sc2_ref[...]) + sh2_ref[...]).astype(BF16)

    pos = lax.broadcasted_iota(jnp.int32, (tm, 1), 0) % seq
    first = pos == 0
    last = pos == seq - 1

    def conv(u, cols):
        prev = jnp.where(first, 0.0, pltpu.roll(u, 1, axis=0))
        nxt = jnp.where(last, 0.0, pltpu.roll(u, tm - 1, axis=0))
        return (cw_ref[0:1, cols] * prev + cw_ref[1:2, cols] * u + cw_ref[2:3, cols] * nxt
                + cb_ref[:, cols])

    for j in range(D_FF // tf):
        ca = slice(j * tf, (j + 1) * tf)
        cg = slice(D_FF + j * tf, D_FF + (j + 1) * tf)
        a = conv(_dot(hb, wu_ref[:, ca]), ca)
        g = conv(_dot(hb, wu_ref[:, cg]), cg)
        act_ref[:, ca] = (_silu(g) * a).astype(BF16)
    y = _dot(act_ref[...], wd_ref[...])
    o_ref[...] = o_ref[...] + gt2_ref[...] * _rms(y, gpost2_ref[...])


def _resident(shape):
    return pl.BlockSpec(shape, lambda *_: (0,) * len(shape), pipeline_mode=pl.Buffered(1))


def _mix_ffn(parts, w_out, x, mod, first_row, groups, gpost1, gpre2, w_up, conv_w, conv_b, w_down,
             gpost2, tm, seq, tf=256):
    t = x.shape[0]
    per = (t // tm) // groups
    row = lambda i: (i, 0)
    vec = _resident((1, D_MODEL))
    in_specs = [pl.BlockSpec((tm, p.shape[1]), row) for p in parts]
    in_specs += [_resident(w_out.shape), pl.BlockSpec((tm, D_MODEL), row),
                 _mod_spec(MOD_GATE1, first_row, per), vec, vec,
                 _mod_spec(MOD_SCALE2, first_row, per), _mod_spec(MOD_SHIFT2, first_row, per),
                 _resident(w_up.shape), _resident(conv_w.shape), _resident(conv_b.shape),
                 _resident(w_down.shape), _mod_spec(MOD_GATE2, first_row, per), vec]
    return pl.pallas_call(
        functools.partial(_mix_ffn_kernel, n_parts=len(parts), seq=seq, tf=tf),
        out_shape=jax.ShapeDtypeStruct((t, D_MODEL), F32),
        grid=(t // tm,),
        in_specs=in_specs,
        out_specs=pl.BlockSpec((tm, D_MODEL), row),
        scratch_shapes=[pltpu.VMEM((tm, D_FF), BF16)],
        compiler_params=_cparams(("arbitrary",)),
        name="mix_ffn",
    )(*parts, w_out, x, mod, gpost1, gpre2, mod, mod, w_up, conv_w, conv_b, w_down, mod, gpost2)


def _rope(x, cos, sin_signed):
    lane = lax.broadcasted_iota(jnp.int32, x.shape, 1)
    lower = (lane % 32) < 16
    partner = jnp.where(lower, pltpu.roll(x, 128 - 16, axis=1), pltpu.roll(x, 16, axis=1))
    return x * cos + partner * sin_signed


def _p1_kernel(*refs, rope):
    if rope:
        (x_ref, g_ref, sc_ref, sh_ref, w_ref, gq_ref, wq_ref, gkv_ref, cos_ref, sin_ref,
         q_ref, ckv_ref, kr_ref) = refs
    else:
        (x_ref, g_ref, sc_ref, sh_ref, w_ref, gq_ref, wq_ref, gkv_ref,
         q_ref, ckv_ref, kr_ref) = refs
    h = _rms(x_ref[...], g_ref[...]) * (1.0 + sc_ref[...]) + sh_ref[...]
    z = _dot(h.astype(BF16), w_ref[...])
    qa = _rms(z[:, 0:C_Q_RANK], gq_ref[...])
    q = _dot(qa.astype(BF16), wq_ref[...])
    ckv_ref[...] = _rms(z[:, C_Q_RANK:C_Q_RANK + C_KV_RANK], gkv_ref[...])
    kr = z[:, C_Q_RANK + C_KV_RANK:L1_IN_PAD]
    if rope:
        cos = cos_ref[...]
        sin = sin_ref[...]
        kr = _rope(kr, cos, sin)
    kr_ref[...] = kr[:, 0:C_ROPE]
    for hd in range(C_HEADS):
        c0 = hd * QCAT
        q_ref[:, c0:c0 + 128] = q[:, c0:c0 + 128].astype(q_ref.dtype)
        qr = q[:, c0 + 128:c0 + 256]
        if rope:
            qr = _rope(qr, cos, sin)
        q_ref[:, c0 + 128:c0 + 256] = qr.astype(q_ref.dtype)


def _proj_l1(x, g, mod, first_row, groups, w, gq, wq, gkv, tables, tm):
    t = x.shape[0]
    per = (t // tm) // groups
    row = lambda i: (i, 0)
    fixed = lambda i: (0, 0)
    in_specs = [pl.BlockSpec((tm, D_MODEL), row),
                pl.BlockSpec((1, D_MODEL), fixed),
                _mod_spec(MOD_SCALE1, first_row, per),
                _mod_spec(MOD_SHIFT1, first_row, per),
                pl.BlockSpec(w.shape, fixed),
                pl.BlockSpec((1, C_Q_RANK), fixed),
                pl.BlockSpec(wq.shape, fixed),
                pl.BlockSpec((1, C_KV_RANK), fixed)]
    args = [x, g, mod, mod, w, gq, wq, gkv]
    rope = tables is not None
    if rope:
        n = tables[0].shape[0]
        pos = lambda i: (i % (n // tm), 0)
        in_specs += [pl.BlockSpec((tm, 128), pos)] * 2
        args += list(tables)
    return pl.pallas_call(
        functools.partial(_p1_kernel, rope=rope),
        out_shape=(jax.ShapeDtypeStruct((t, C_HEADS * QCAT), BF16),
                   jax.ShapeDtypeStruct((t, C_KV_RANK), F32),
                   jax.ShapeDtypeStruct((t, C_ROPE), F32)),
        grid=(t // tm,),
        in_specs=in_specs,
        out_specs=(pl.BlockSpec((tm, C_HEADS * QCAT), row),
                   pl.BlockSpec((tm, C_KV_RANK), row),
                   pl.BlockSpec((tm, C_ROPE), row)),
        compiler_params=_cparams(("arbitrary",)),
        name="proj_l1",
    )(*args)


def _attn_kernel(*refs, n_parts, scale):
    q_ref = refs[0]
    kv_parts = refs[1:1 + 2 * n_parts]
    w_ref, o_ref, k_sc, v_sc = refs[1 + 2 * n_parts:]

    @pl.when(pl.program_id(1) == 0)
    def _():
        off = 0
        for p in range(n_parts):
            ckv_ref, kr_ref = kv_parts[2 * p], kv_parts[2 * p + 1]
            nk = ckv_ref.shape[0]
            kv = _dot(ckv_ref[...].astype(BF16), w_ref[...])
            kr = kr_ref[...].astype(BF16)
            kr2 = jnp.concatenate([kr, kr], axis=1)
            ones = jnp.ones((nk, C_DV), BF16)
            for hd in range(C_HEADS):
                v0 = C_HEADS * C_NOPE + hd * C_DV
                k_sc[hd, off:off + nk, 0:128] = kv[:, hd * 128:(hd + 1) * 128].astype(BF16)
                k_sc[hd, off:off + nk, 128:256] = kr2
                v_sc[hd, off:off + nk, 0:C_DV] = kv[:, v0:v0 + C_DV].astype(BF16)
                v_sc[hd, off:off + nk, C_DV:2 * C_DV] = ones
            off += nk

    for hd in range(C_HEADS):
        t = _dot_nt(q_ref[:, hd * QCAT:(hd + 1) * QCAT], k_sc[hd]) * (scale * LOG2E)
        p = jnp.exp2(t - jnp.max(t, axis=1, keepdims=True))
        ov = _dot(p.astype(BF16), v_sc[hd])
        o_ref[:, hd * C_DV:(hd + 1) * C_DV] = (ov[:, 0:C_DV] / ov[:, C_DV:2 * C_DV]).astype(o_ref.dtype)


def _attention(q, kv_parts, w_kvb, tq):
    bsz, n, _ = q.shape
    nk = sum(p[0].shape[1] for p in kv_parts)
    in_specs = [pl.BlockSpec((None, tq, C_HEADS * QCAT), lambda b, i: (b, i, 0))]
    args = [q]
    for ckv, kr in kv_parts:
        in_specs.append(pl.BlockSpec((None, ckv.shape[1], C_KV_RANK), lambda b, i: (b, 0, 0)))
        in_specs.append(pl.BlockSpec((None, kr.shape[1], C_ROPE), lambda b, i: (b, 0, 0)))
        args += [ckv, kr]
    in_specs.append(pl.BlockSpec(w_kvb.shape, lambda b, i: (0, 0)))
    args.append(w_kvb)
    return pl.pallas_call(
        functools.partial(_attn_kernel, n_parts=len(kv_parts),
                          scale=(C_NOPE + C_ROPE) ** -0.5),
        out_shape=jax.ShapeDtypeStruct((bsz, n, C_HEADS * C_DV), BF16),
        grid=(bsz, n // tq),
        in_specs=in_specs,
        out_specs=pl.BlockSpec((None, tq, C_HEADS * C_DV), lambda b, i: (b, i, 0)),
        scratch_shapes=[pltpu.VMEM((C_HEADS, nk, QCAT), BF16),
                        pltpu.VMEM((C_HEADS, nk, 2 * C_DV), BF16)],
        compiler_params=_cparams(("arbitrary", "arbitrary")),
        name="mla_attention",
    )(*args)


def _l1_wq_layout(w_qb):
    blocks = []
    zero = jnp.zeros((w_qb.shape[0], C_ROPE), w_qb.dtype)
    for hd in range(C_HEADS):
        base = hd * (C_NOPE + C_ROPE)
        nope = w_qb[:, base:base + C_NOPE]
        rope = w_qb[:, base + C_NOPE:base + C_NOPE + C_ROPE]
        blocks += [nope, rope, zero] if hd % 2 == 0 else [nope, zero, rope]
    return jnp.concatenate(blocks, axis=1).astype(BF16)


def _l1_wkvb_layout(w_kvb):
    w = w_kvb.reshape(C_KV_RANK, C_HEADS, C_NOPE + C_DV)
    k = w[:, :, :C_NOPE].reshape(C_KV_RANK, C_HEADS * C_NOPE)
    v = w[:, :, C_NOPE:].reshape(C_KV_RANK, C_HEADS * C_DV)
    return jnp.concatenate([k, v], axis=1).astype(BF16)


def _rope_tables(n):
    pos = np.arange(n)
    row = (pos // GRID_W).astype(np.float64)
    col = (pos % GRID_W).astype(np.float64)
    half = C_ROPE // 2
    inv = 1.0 / (ROPE_THETA ** (np.arange(0, half, 2, dtype=np.float64) / half))
    ang_r = row[:, None] * inv[None, :]
    ang_c = col[:, None] * inv[None, :]
    cos64 = np.concatenate([np.cos(ang_r)] * 2 + [np.cos(ang_c)] * 2, axis=1)
    sin64 = np.concatenate([-np.sin(ang_r), np.sin(ang_r), -np.sin(ang_c), np.sin(ang_c)], axis=1)
    return (jnp.asarray(np.concatenate([cos64] * 2, axis=1), F32),
            jnp.asarray(np.concatenate([sin64] * 2, axis=1), F32))


def _gate_weight(w, first_row):
    return jnp.pad(w, ((first_row, ZG_W - first_row - A_GATE_RANK), (0, 0)))


def _state_t(s):
    b = s.shape[0]
    return s.transpose(0, 3, 1, 2).reshape(b, 128, HK)


def _n_rows(n_fwd, n_bwd):
    b = n_fwd.shape[0]
    rows = jnp.zeros((b, 8, HK), F32)
    return rows.at[:, 0, :].set(n_fwd.reshape(b, HK)).at[:, 1, :].set(n_bwd.reshape(b, HK))


def _m_rows(m_fwd, m_bwd):
    m = jnp.concatenate([m_fwd, m_bwd], axis=1)
    return jnp.broadcast_to(m[:, :, None], (m.shape[0], 8, 128))


def kernel(x_prompt, x_sample, state_l0_gla_fwd, state_l0_gla_bwd, state_l0_mlstm_c_fwd, state_l0_mlstm_n_fwd, state_l0_mlstm_m_fwd, state_l0_mlstm_c_bwd, state_l0_mlstm_n_bwd, state_l0_mlstm_m_bwd, cache_l1_ckv, cache_l1_krope, c, c_ctx, l0_w_mod, l0_b_mod, l0_g_pre_mix, l0_g_post_mix, l0_g_pre_ffn, l0_g_post_ffn, l0_w_in, l0_gla_w_gate_f, l0_gla_b_gate_f, l0_gla_w_gate_b, l0_gla_b_gate_b, l0_gla_g_norm, l0_mlstm_b_gates, l0_mlstm_g_norm, l0_w_out, l0_ffn_w_up, l0_ffn_conv_w, l0_ffn_conv_b, l0_ffn_w_down, l1_w_mod, l1_b_mod, l1_g_pre_mix, l1_g_post_mix, l1_g_pre_ffn, l1_g_post_ffn, l1_w_in, l1_g_q_norm, l1_w_qb, l1_g_kv_norm, l1_w_kvb, l1_w_out, l1_ffn_w_up, l1_ffn_conv_w, l1_ffn_conv_b, l1_ffn_w_down):
    bp, sp, d = x_prompt.shape
    bs, ss, _ = x_sample.shape
    row = lambda v: v.reshape(1, -1)

    cond8 = jnp.concatenate([c_ctx[None, :], c, jnp.zeros((8 - 1 - bs, d), F32)], axis=0)
    mods = (_modulation(cond8, l0_w_mod, l0_b_mod), _modulation(cond8, l1_w_mod, l1_b_mod))

    paths = {
        "ctx": dict(x=x_prompt.reshape(bp * sp, d), b=bp, n=sp, first_row=0, groups=1, tm=512),
        "lat": dict(x=x_sample.reshape(bs * ss, d), b=bs, n=ss, first_row=1, groups=bs, tm=512),
    }

    w_in0 = l0_w_in.T
    wgf = _gate_weight(l0_gla_w_gate_f, 0)
    wgb = _gate_weight(l0_gla_w_gate_b, A_GATE_RANK)
    bg = jnp.pad(l0_mlstm_b_gates, (GATE_COL, ZG_W - GATE_COL - 16)).reshape(1, ZG_W)
    w_out0 = l0_w_out.astype(BF16)
    w_in1 = jnp.pad(l1_w_in, ((0, 0), (0, L1_IN_PAD - l1_w_in.shape[1]))).astype(BF16)
    w_q1 = _l1_wq_layout(l1_w_qb)
    w_kvb1 = _l1_wkvb_layout(l1_w_kvb)
    w_out1 = l1_w_out.astype(BF16)
    ffn = ((l0_ffn_w_up.astype(BF16), l0_ffn_conv_w, row(l0_ffn_conv_b), l0_ffn_w_down.astype(BF16)),
           (l1_ffn_w_up.astype(BF16), l1_ffn_conv_w, row(l1_ffn_conv_b), l1_ffn_w_down.astype(BF16)))
    norms = ((row(l0_g_pre_mix), row(l0_g_post_mix), row(l0_g_pre_ffn), row(l0_g_post_ffn)),
             (row(l1_g_pre_mix), row(l1_g_post_mix), row(l1_g_pre_ffn), row(l1_g_post_ffn)))

    gla_init = (_state_t(state_l0_gla_fwd), _state_t(state_l0_gla_bwd))
    mlstm_init = (_state_t(state_l0_mlstm_c_fwd), _state_t(state_l0_mlstm_c_bwd),
                  _n_rows(state_l0_mlstm_n_fwd, state_l0_mlstm_n_bwd),
                  _m_rows(state_l0_mlstm_m_fwd, state_l0_mlstm_m_bwd))
    tables = _rope_tables(ss)

    results = {}
    for name, p in paths.items():
        x, b, n, tm = p["x"], p["b"], p["n"], p["tm"]
        first_row, groups = p["first_row"], p["groups"]
        is_ctx = name == "ctx"
        ffn_tm = tm if is_ctx else n
        g_pre, g_post, g_pre2, g_post2 = norms[0]

        za, zb, zg = _proj_l0(x, g_pre, mods[0], first_row, groups, w_in0, tm)
        za, zb, zg = (t.reshape(b, n, -1) for t in (za, zb, zg))
        oa, sa_f, sa_b = _gla(za, zg, wgf, wgb, row(l0_gla_b_gate_f), row(l0_gla_b_gate_b),
                              row(l0_gla_g_norm), None if is_ctx else gla_init)
        om, c_f, c_b, n_rows, m_rows = _mlstm(zb, zg, bg, row(l0_mlstm_g_norm),
                                              None if is_ctx else mlstm_init)
        x = _mix_ffn([oa.reshape(b * n, HV), om.reshape(b * n, HV)], w_out0, x, mods[0], first_row,
                     groups, g_post, g_pre2, *ffn[0], g_post2, ffn_tm, n)

        g_pre, g_post, g_pre2, g_post2 = norms[1]
        q, ckv, kr = _proj_l1(x, g_pre, mods[1], first_row, groups, w_in1, row(l1_g_q_norm), w_q1,
                              row(l1_g_kv_norm), None if is_ctx else tables, tm)
        ckv3 = ckv.reshape(b, n, C_KV_RANK)
        kr3 = kr.reshape(b, n, C_ROPE)
        kv_parts = [(ckv3, kr3)] if is_ctx else [(cache_l1_ckv, cache_l1_krope), (ckv3, kr3)]
        o = _attention(q.reshape(b, n, -1), kv_parts, w_kvb1, 256)
        x = _mix_ffn([o.reshape(b * n, C_HEADS * C_DV)], w_out1, x, mods[1], first_row,
                     groups, g_post, g_pre2, *ffn[1], g_post2, ffn_tm, n)

        results[name] = dict(y=x.reshape(b, n, d), gla=(sa_f, sa_b), c=(c_f, c_b),
                             n=n_rows, m=m_rows, ckv=ckv3, kr=kr3)

    r = results["ctx"]
    n_fwd = r["n"][:, 0, :].reshape(bp, B_HEADS, B_DK)
    n_bwd = r["n"][:, 1, :].reshape(bp, B_HEADS, B_DK)
    m_fwd = r["m"][:, 0:B_HEADS, 0]
    m_bwd = r["m"][:, B_HEADS:2 * B_HEADS, 0]
    return (r["y"], results["lat"]["y"], r["gla"][0], r["gla"][1],
            r["c"][0], n_fwd, m_fwd, r["c"][1], n_bwd, m_bwd, r["ckv"], r["kr"])
```

```python
import functools

import numpy as np
import jax
import jax.numpy as jnp
from jax import lax
from jax.experimental import pallas as pl
from jax.experimental.pallas import tpu as pltpu

F32 = jnp.float32
BF16 = jnp.bfloat16

D_MODEL = 1024
EPS = 1e-6
LOG2E = 1.4426950408889634
N_MOD = 6
GRID_W = 64
ROPE_THETA = 10000.0

A_HEADS, A_DK, A_DV, A_GATE_RANK, A_GATE_TEMP = 4, 64, 128, 16, 16.0
B_HEADS, B_DK, B_DV = 4, 64, 128
C_HEADS, C_Q_RANK, C_KV_RANK, C_NOPE, C_ROPE, C_DV = 8, 384, 256, 128, 64, 128
D_FF = 2816
CONV_W = 3

HEADS = 4
HK = HEADS * 64
HV = HEADS * 128
ZA_W = 1536
ZG_W = 128
GATE_COL = 32
RCHUNK = 128
L1_IN_PAD = 768
QCAT = 256

VMEM_LIMIT = 56 * 1024 * 1024


def _cparams(sem, vmem=VMEM_LIMIT):
    return pltpu.CompilerParams(dimension_semantics=sem, vmem_limit_bytes=vmem)


def _dot(a, b):
    return jnp.dot(a, b, preferred_element_type=F32)


def _dot_nt(a, b):
    return lax.dot_general(a, b, (((1,), (1,)), ((), ())), preferred_element_type=F32)


def _dot_tn(a, b):
    return lax.dot_general(a, b, (((0,), (0,)), ((), ())), preferred_element_type=F32)


def _split3(x):
    hi = x.astype(BF16)
    r1 = x - hi.astype(F32)
    mid = r1.astype(BF16)
    lo = (r1 - mid.astype(F32)).astype(BF16)
    return hi, mid, lo


def _tri_dot(tri, x):
    hi, mid, lo = _split3(x)
    return _dot(tri, hi) + _dot(tri, mid) + _dot(tri, lo)


def _dot_tri(x, tri):
    hi, mid, lo = _split3(x)
    return _dot(hi, tri) + _dot(mid, tri) + _dot(lo, tri)


def _dot_hl(a, b):
    ah = a.astype(BF16)
    al = (a - ah.astype(F32)).astype(BF16)
    bh = b.astype(BF16)
    bl = (b - bh.astype(F32)).astype(BF16)
    return _dot(ah, bh) + _dot(ah, bl) + _dot(al, bh)


def _rms(x, g):
    return x * lax.rsqrt(jnp.mean(x * x, axis=-1, keepdims=True) + EPS) * g


def _sigmoid(x):
    return 1.0 / (1.0 + jnp.exp(-x))


def _silu(x):
    return x * _sigmoid(x)


def _log_sigmoid(x):
    return jnp.minimum(x, 0.0) - jnp.log(1.0 + jnp.exp(-jnp.abs(x)))


def _tri_masks(n):
    r = lax.broadcasted_iota(jnp.int32, (n, n), 0)
    c = lax.broadcasted_iota(jnp.int32, (n, n), 1)
    return c <= r, c >= r


def _mod_kernel(cond_ref, w_ref, b_ref, o_ref):
    s = _silu(cond_ref[...])
    o_ref[...] = _dot(s.astype(BF16), w_ref[...].astype(BF16)) + b_ref[...]


def _modulation(cond8, w_mod, b_mod):
    d = D_MODEL
    out = pl.pallas_call(
        _mod_kernel,
        out_shape=jax.ShapeDtypeStruct((N_MOD, 8, d), F32),
        grid=(N_MOD,),
        in_specs=[pl.BlockSpec((8, d), lambda j: (0, 0)),
                  pl.BlockSpec((d, d), lambda j: (0, j)),
                  pl.BlockSpec((1, d), lambda j: (0, j))],
        out_specs=pl.BlockSpec((None, 8, d), lambda j: (j, 0, 0)),
        compiler_params=_cparams(("arbitrary",)),
        name="modulation",
    )(cond8, w_mod, b_mod.reshape(1, N_MOD * d))
    return out.reshape(N_MOD, 8, 1, d)


MOD_SHIFT1, MOD_SCALE1, MOD_GATE1, MOD_SHIFT2, MOD_SCALE2, MOD_GATE2 = range(N_MOD)


def _mod_spec(which, first_row, per):
    return pl.BlockSpec((None, None, 1, D_MODEL), lambda i: (which, first_row + i // per, 0, 0))


L0_GROUP_ROWS = ((0, ZA_W), (ZA_W + 32, 2 * ZA_W + 32))
L0_LR_ROWS = (ZA_W, ZA_W + 32)
L0_GATE_ROWS = (2 * ZA_W + 32, 2 * ZA_W + 48)


def _p0_kernel(x_ref, g_ref, sc_ref, sh_ref, wt_ref, za_ref, zb_ref, zg_ref, w_ref):
    @pl.when(pl.program_id(0) == 0)
    def _():
        blk = 256
        for gi, (r0, r1) in enumerate(L0_GROUP_ROWS):
            for j in range((r1 - r0) // blk):
                rows = wt_ref[r0 + j * blk:r0 + (j + 1) * blk, :]
                w_ref[:, gi * ZA_W + j * blk:gi * ZA_W + (j + 1) * blk] = rows.T.astype(BF16)
        tail = jnp.concatenate([wt_ref[L0_LR_ROWS[0]:L0_LR_ROWS[1], :],
                                wt_ref[L0_GATE_ROWS[0]:L0_GATE_ROWS[1], :],
                                jnp.zeros((ZG_W - 48, D_MODEL), F32)], axis=0)
        w_ref[:, 2 * ZA_W:2 * ZA_W + ZG_W] = tail.T.astype(BF16)

    h = _rms(x_ref[...], g_ref[...]) * (1.0 + sc_ref[...]) + sh_ref[...]
    hb = h.astype(BF16)
    za_ref[...] = _dot(hb, w_ref[:, 0:ZA_W])
    zb_ref[...] = _dot(hb, w_ref[:, ZA_W:2 * ZA_W])
    zg_ref[...] = _dot(hb, w_ref[:, 2 * ZA_W:2 * ZA_W + ZG_W])


def _proj_l0(x, g, mod, first_row, groups, w_t, tm):
    t = x.shape[0]
    per = (t // tm) // groups
    row = lambda i: (i, 0)
    fixed = lambda i: (0, 0)
    return pl.pallas_call(
        _p0_kernel,
        out_shape=(jax.ShapeDtypeStruct((t, ZA_W), F32),
                   jax.ShapeDtypeStruct((t, ZA_W), F32),
                   jax.ShapeDtypeStruct((t, ZG_W), F32)),
        grid=(t // tm,),
        in_specs=[pl.BlockSpec((tm, D_MODEL), row),
                  pl.BlockSpec((1, D_MODEL), fixed),
                  _mod_spec(MOD_SCALE1, first_row, per),
                  _mod_spec(MOD_SHIFT1, first_row, per),
                  _resident(w_t.shape)],
        out_specs=(pl.BlockSpec((tm, ZA_W), row),
                   pl.BlockSpec((tm, ZA_W), row),
                   pl.BlockSpec((tm, ZG_W), row)),
        scratch_shapes=[pltpu.VMEM((D_MODEL, 2 * ZA_W + ZG_W), BF16)],
        compiler_params=_cparams(("arbitrary",)),
        name="proj_l0",
    )(x, g, mod, mod, w_t)


def _head_lane_mask(shape, h):
    lane = lax.broadcasted_iota(jnp.int32, shape, len(shape) - 1)
    return (lane >= h * 64) & (lane < (h + 1) * 64)


def _head_row(h):
    return jnp.where(_head_lane_mask((1, HK), h), 1.0, 0.0).astype(BF16)


def _gla_prepare(za_ref, la_ref, r0, rev, tri):
    L = RCHUNK
    rows = pl.ds(r0, L)
    b = _tri_dot(tri, la_ref[rows, :])
    mid = L // 2
    ref = b[mid:mid + 1, :]
    b_end = b[0:1, :] if rev else b[L - 1:L, :]
    q = za_ref[rows, 0:HK]
    k = za_ref[rows, HK:2 * HK]
    v = za_ref[rows, 2 * HK:2 * HK + HV].astype(BF16)
    qe = (q * (jnp.exp(b - ref) * (A_DK ** -0.5))).astype(BF16)
    ke = (k * jnp.exp(ref - b)).astype(BF16)
    qm = jnp.concatenate([qe * _head_row(h) for h in range(HEADS)], axis=0)
    rt = _dot_tn(v, ke)
    upd = None
    for h in range(HEADS):
        blk = rt[h * 128:(h + 1) * 128, :]
        blk = jnp.where(_head_lane_mask(blk.shape, h), blk, 0.0)
        upd = blk if upd is None else upd + blk
    return dict(rows=rows, qm=qm, ke=ke, v=v, upd=upd * jnp.exp(b_end - ref),
                decay=jnp.exp(b_end), eref=jnp.exp(ref))


def _gla_apply(items):
    L = RCHUNK
    sts = [st_ref[...] for _, st_ref, _, _ in items]
    o1s = []
    for (p, _, _, _), st in zip(items, sts):
        rhs = jnp.concatenate([p["ke"], (st * p["eref"]).astype(BF16)], axis=0)
        o1s.append(_dot_nt(p["qm"], rhs))
    for h in range(HEADS):
        for (p, _, o_ref, mask), o1 in zip(items, o1s):
            blk = o1[h * L:(h + 1) * L, :]
            sc = jnp.where(mask, blk[:, 0:L], 0.0).astype(BF16)
            o_ref[p["rows"], h * 128:(h + 1) * 128] = (
                blk[:, L:L + 128] + _dot(sc, p["v"][:, h * 128:(h + 1) * 128]))
    for (p, st_ref, _, _), st in zip(items, sts):
        st_ref[...] = st * p["decay"] + p["upd"]


T_M, T_MOUT, T_G, T_SRCW, T_MPREV, T_CARRY, T_USED = 0, 8, 16, 24, 32, 40, 48


def _lane_scan(x, op, fill, is_fwd, pos, n):
    k = 1
    while k < RCHUNK:
        pf = jnp.where(pos >= k, pltpu.roll(x, k, axis=1), fill)
        sf = jnp.where(pos < RCHUNK - k, pltpu.roll(x, n - k, axis=1), fill)
        x = op(x, jnp.where(is_fwd, pf, sf))
        k *= 2
    return x


def _mlstm_gate_tables(g, m0, gt_ref, tab_ref, col_ref, n):
    L = RCHUNK
    nc = n // L
    for c in range(nc):
        gt_ref[:, c * L:(c + 1) * L] = g[c * L:(c + 1) * L, :].T
        tab_ref[c, T_USED:, :] = jnp.zeros((ZG_W - T_USED, L), F32)
    gi = gt_ref[GATE_COL:GATE_COL + 8, :]
    lf = gt_ref[GATE_COL + 8:GATE_COL + 16, :]
    is_fwd = lax.broadcasted_iota(jnp.int32, (8, n), 0) < HEADS
    pos = lax.broadcasted_iota(jnp.int32, (8, n), 1) % L
    b = _lane_scan(lf, jnp.add, 0.0, is_fwd, pos, n)
    gg = gi - b
    cmax = _lane_scan(gg, jnp.maximum, -jnp.inf, is_fwd, pos, n)
    fwd_l = lax.broadcasted_iota(jnp.int32, (8, L), 0) < HEADS
    mp = m0
    for j in range(nc):
        cf, cb = j, nc - 1 - j

        def pick(x):
            return jnp.where(fwd_l, x[:, cf * L:(cf + 1) * L], x[:, cb * L:(cb + 1) * L])

        gj, bj, cj, lj = pick(gg), pick(b), pick(cmax), pick(lf)
        mx = jnp.maximum(mp, jnp.max(gj, axis=1, keepdims=True))
        mj = jnp.maximum(cj, mp)
        blocks = (mj, bj + mj, gj, jnp.exp(gj - mx),
                  jnp.broadcast_to(mp, (8, L)), jnp.broadcast_to(jnp.exp(mp - mx), (8, L)))
        for t, val in enumerate(blocks):
            tab_ref[cf, 8 * t:8 * t + HEADS, :] = val[0:HEADS]
            tab_ref[cb, 8 * t + HEADS:8 * t + 8, :] = val[HEADS:8]
        mp = jnp.sum(lj, axis=1, keepdims=True) + mx
    for c in range(nc):
        col_ref[c * L:(c + 1) * L, :] = tab_ref[c].T
    return mp


def _mlstm_prepare(zb_ref, tab_ref, col_ref, c, d, mask):
    L = RCHUNK
    rows = pl.ds(pl.multiple_of(c * L, L), L)
    qb = zb_ref[rows, 0:HK].astype(BF16)
    kb = (zb_ref[rows, HK:2 * HK] * (B_DK ** -0.5)).astype(BF16)
    v = zb_ref[rows, 2 * HK:2 * HK + HV]
    vb = v.astype(BF16)
    vt = v.T
    qm = jnp.concatenate([qb * _head_row(h) for h in range(HEADS)], axis=0)
    s = _dot_nt(qm, kb)
    ones = jnp.ones((L, 128), BF16)
    pv, w_inter, e_negm = [], [], []
    upd = None
    carry = None
    for h in range(HEADS):
        r = 4 * d + h
        m_rep = jnp.broadcast_to(col_ref[rows, T_M + r:T_M + r + 1], (L, 128))
        mout_rep = jnp.broadcast_to(col_ref[rows, T_MOUT + r:T_MOUT + r + 1], (L, 128))
        g_row = tab_ref[c, T_G + r:T_G + r + 1, :]
        w = jnp.exp(jnp.where(mask, g_row - m_rep, -jnp.inf))
        qk = (s[h * L:(h + 1) * L, :] * w).astype(BF16)
        v1 = jnp.concatenate([vb[:, h * 128:(h + 1) * 128], ones], axis=1)
        pv.append(_dot(qk, v1))
        w_inter.append(jnp.exp(tab_ref[c, T_MPREV + r:T_MPREV + r + 1, :] - m_rep))
        e_negm.append(jnp.exp(-mout_rep))
        sw = tab_ref[c, T_SRCW + r:T_SRCW + r + 1, :]
        lhs = jnp.concatenate([vt[h * 128:(h + 1) * 128, :] * sw, jnp.broadcast_to(sw, (128, L))],
                              axis=0).astype(BF16)
        blk = _dot(lhs, kb)
        blk = jnp.where(_head_lane_mask(blk.shape, h), blk, 0.0)
        upd = blk if upd is None else upd + blk
        cr = tab_ref[c, T_CARRY + r:T_CARRY + r + 1, :]
        cr = jnp.where(_head_lane_mask((1, HK), h), jnp.concatenate([cr, cr], axis=1), 0.0)
        carry = cr if carry is None else carry + cr
    return dict(rows=rows, qm=qm, pv=pv, w_inter=w_inter, e_negm=e_negm, upd=upd, carry=carry)


def _mlstm_apply(items):
    L = RCHUNK
    sts = [st_ref[...] for _, st_ref, _ in items]
    a_all = [_dot_nt(p["qm"], st.astype(BF16)) for (p, _, _), st in zip(items, sts)]
    for h in range(HEADS):
        for (p, _, h_ref), a in zip(items, a_all):
            blk = a[h * L:(h + 1) * L, :]
            pv, w, e = p["pv"][h], p["w_inter"][h], p["e_negm"][h]
            num = w * blk[:, 0:128] + pv[:, 0:128]
            den = w * blk[:, 128:256] + pv[:, 128:256]
            h_ref[p["rows"], h * 128:(h + 1) * 128] = num / jnp.maximum(jnp.abs(den), e)
    for (p, st_ref, _), st in zip(items, sts):
        st_ref[...] = p["carry"] * st + p["upd"]


def _rec_kernel(*refs, n, has_init):
    it = iter(refs)

    def take(k):
        return [next(it) for _ in range(k)]

    za_ref, zb_ref, zg_ref, zgn_ref = take(4)
    wgf_ref, wgb_ref, bgf_ref, bgb_ref, gna_ref, bg_ref, gnb_ref = take(7)
    if has_init:
        sf0_ref, sb0_ref, cf0_ref, cb0_ref, n0_ref, m0_ref, m0n_ref = take(7)
    oa_ref, sf_ref, sb_ref, om_ref, cf_ref, cb_ref, n_out_ref, m_out_ref = take(8)
    (gt_ref,) = take(1)
    set_a = take(5)
    set_b = take(5)
    of_ref, ob_ref, hf_ref, hb_ref, gsf_ref, gsb_ref, msf_ref, msb_ref = take(8)
    L = RCHUNK
    nc = n // L
    step = pl.program_id(0)

    def fill(gate_ref, m0, bufs):
        laf_ref, lab_ref, tab_ref, col_ref, mf_ref = bufs
        zg = gate_ref[...]
        laf_ref[...] = _log_sigmoid(_dot_hl(zg, wgf_ref[...]) + bgf_ref[...]) / A_GATE_TEMP
        lab_ref[...] = _log_sigmoid(_dot_hl(zg, wgb_ref[...]) + bgb_ref[...]) / A_GATE_TEMP
        g = zg + bg_ref[...]
        lane = lax.broadcasted_iota(jnp.int32, g.shape, 1)
        is_forget = (lane >= GATE_COL + 8) & (lane < GATE_COL + 16)
        g = jnp.where(is_forget, _log_sigmoid(g), g)
        m_fin = _mlstm_gate_tables(g, m0, gt_ref, tab_ref, col_ref, n)
        mf_ref[...] = jnp.broadcast_to(m_fin, (8, 128))

    if has_init:
        m0, m0_next = m0_ref[:, 0:1], m0n_ref[:, 0:1]
        gsf_ref[...] = sf0_ref[...]
        gsb_ref[...] = sb0_ref[...]
        msf_ref[...] = jnp.concatenate(
            [cf0_ref[...], jnp.broadcast_to(n0_ref[0:1, :], (128, HK))], axis=0)
        msb_ref[...] = jnp.concatenate(
            [cb0_ref[...], jnp.broadcast_to(n0_ref[1:2, :], (128, HK))], axis=0)
    else:
        m0 = m0_next = jnp.zeros((8, 1), F32)
        for ref in (gsf_ref, gsb_ref, msf_ref, msb_ref):
            ref[...] = jnp.zeros_like(ref)

    @pl.when(step == 0)
    def _():
        fill(zg_ref, m0, set_a)

    m_lo, m_up = _tri_masks(L)
    tri_lo = jnp.where(m_lo, 1.0, 0.0).astype(BF16)
    tri_up = jnp.where(m_up, 1.0, 0.0).astype(BF16)
    gna = gna_ref[...]
    gnb = gnb_ref[...]

    def out_body(c, carry):
        rows = pl.ds(pl.multiple_of(c * L, L), L)
        for h in range(HEADS):
            cols = slice(h * 128, (h + 1) * 128)
            gcols = slice(2 * HK + HV + h * 128, 2 * HK + HV + (h + 1) * 128)
            o = of_ref[rows, cols] + ob_ref[rows, cols]
            oa_ref[rows, cols] = (_rms(o, gna) * _silu(za_ref[rows, gcols])).astype(oa_ref.dtype)
            hh = hf_ref[rows, cols] + hb_ref[rows, cols]
            om_ref[rows, cols] = (_sigmoid(zb_ref[rows, gcols]) * _rms(hh, gnb)).astype(om_ref.dtype)
        return carry

    def run(cur, nxt):
        laf_ref, lab_ref, tab_ref, col_ref, mf_ref = cur
        fill(zgn_ref, m0_next, nxt)

        def scan_body(gidx, carry):
            gla_waves, mlstm_waves = [], []
            for w in range(2):
                c = 2 * gidx + w
                cb = nc - 1 - c
                gla_waves.append(
                    (_gla_prepare(za_ref, laf_ref, pl.multiple_of(c * L, L), False, tri_lo),
                     _gla_prepare(za_ref, lab_ref, pl.multiple_of(cb * L, L), True, tri_up)))
                mlstm_waves.append((_mlstm_prepare(zb_ref, tab_ref, col_ref, c, 0, m_lo),
                                    _mlstm_prepare(zb_ref, tab_ref, col_ref, cb, 1, m_up)))
            for (gf, gb), (mf, mb) in zip(gla_waves, mlstm_waves):
                _gla_apply([(gf, gsf_ref, of_ref, m_lo), (gb, gsb_ref, ob_ref, m_up)])
                _mlstm_apply([(mf, msf_ref, hf_ref), (mb, msb_ref, hb_ref)])
            return carry

        if nc == 2:
            scan_body(0, 0)
            for c in range(nc):
                out_body(c, 0)
        else:
            lax.fori_loop(0, nc // 2, scan_body, 0)
            lax.fori_loop(0, nc, out_body, 0)
        m_out_ref[...] = mf_ref[...]

    @pl.when(step % 2 == 0)
    def _():
        run(set_a, set_b)

    @pl.when(step % 2 == 1)
    def _():
        run(set_b, set_a)

    gsf_t = gsf_ref[...].T
    gsb_t = gsb_ref[...].T
    msf = msf_ref[...]
    msb = msb_ref[...]
    ctf_t = msf[0:128, :].T
    ctb_t = msb[0:128, :].T
    for h in range(HEADS):
        rows = slice(h * 64, (h + 1) * 64)
        sf_ref[h] = gsf_t[rows, :]
        sb_ref[h] = gsb_t[rows, :]
        cf_ref[h] = ctf_t[rows, :]
        cb_ref[h] = ctb_t[rows, :]
    n_out_ref[...] = jnp.concatenate(
        [msf[128:129, :], msb[128:129, :], jnp.zeros((6, HK), F32)], axis=0)


def _recurrent(za, zb, zg, wgf, wgb, bgf, bgb, gna, bg, gnb, init):
    bsz, n, _ = za.shape
    has_init = init is not None
    nc = n // RCHUNK
    per_b3 = lambda b: (b, 0, 0)
    next_b3 = lambda b: (jnp.minimum(b + 1, bsz - 1), 0, 0)
    per_b4 = lambda b: (b, 0, 0, 0)
    fixed = lambda b: (0, 0)
    in_specs = [pl.BlockSpec((None, n, ZA_W), per_b3),
                pl.BlockSpec((None, n, ZA_W), per_b3),
                pl.BlockSpec((None, n, ZG_W), per_b3),
                pl.BlockSpec((None, n, ZG_W), next_b3),
                pl.BlockSpec((ZG_W, HK), fixed),
                pl.BlockSpec((ZG_W, HK), fixed),
                pl.BlockSpec((1, HK), fixed),
                pl.BlockSpec((1, HK), fixed),
                pl.BlockSpec((1, 128), fixed),
                pl.BlockSpec((1, ZG_W), fixed),
                pl.BlockSpec((1, 128), fixed)]
    args = [za, zb, zg, zg, wgf, wgb, bgf, bgb, gna, bg, gnb]
    if has_init:
        sf0, sb0, cf0, cb0, n0, m0 = init
        in_specs += [pl.BlockSpec((None, 128, HK), per_b3)] * 4
        in_specs += [pl.BlockSpec((None, 8, HK), per_b3), pl.BlockSpec((None, 8, 128), per_b3),
                     pl.BlockSpec((None, 8, 128), next_b3)]
        args += [sf0, sb0, cf0, cb0, n0, m0, m0]
    state = jax.ShapeDtypeStruct((bsz, HEADS, 64, 128), F32)
    state_spec = pl.BlockSpec((None, HEADS, 64, 128), per_b4)
    gate_set = [pltpu.VMEM((n, HK), F32), pltpu.VMEM((n, HK), F32),
                pltpu.VMEM((nc, ZG_W, RCHUNK), F32), pltpu.VMEM((n, ZG_W), F32),
                pltpu.VMEM((8, 128), F32)]
    return pl.pallas_call(
        functools.partial(_rec_kernel, n=n, has_init=has_init),
        out_shape=(jax.ShapeDtypeStruct((bsz, n, HV), BF16), state, state,
                   jax.ShapeDtypeStruct((bsz, n, HV), BF16), state, state,
                   jax.ShapeDtypeStruct((bsz, 8, HK), F32),
                   jax.ShapeDtypeStruct((bsz, 8, 128), F32)),
        grid=(bsz,),
        in_specs=in_specs,
        out_specs=(pl.BlockSpec((None, n, HV), per_b3), state_spec, state_spec,
                   pl.BlockSpec((None, n, HV), per_b3), state_spec, state_spec,
                   pl.BlockSpec((None, 8, HK), per_b3),
                   pl.BlockSpec((None, 8, 128), per_b3)),
        scratch_shapes=([pltpu.VMEM((ZG_W, n), F32)] + gate_set + gate_set
                        + [pltpu.VMEM((n, HV), F32)] * 4
                        + [pltpu.VMEM((128, HK), F32)] * 2 + [pltpu.VMEM((2 * 128, HK), F32)] * 2),
        compiler_params=_cparams(("arbitrary",)),
        name="recurrent",
    )(*args)


def _mix_ffn_kernel(*refs, n_parts, seq, tf):
    parts = refs[:n_parts]
    (wo_ref, x_ref, gt1_ref, gpost1_ref, gpre2_ref, sc2_ref, sh2_ref,
     wu_ref, cw_ref, cb_ref, wd_ref, gt2_ref, gpost2_ref, o_ref, act_ref) = refs[n_parts:]
    tm = x_ref.shape[0]
    m = None
    off = 0
    for p in parts:
        kp = p.shape[1]
        t = _dot(p[...], wo_ref[off:off + kp, :])
        m = t if m is None else m + t
        off += kp
    x1 = x_ref[...] + gt1_ref[...] * _rms(m, gpost1_ref[...])
    o_ref[...] = x1
    hb = (_rms(x1, gpre2_ref[...]) * (1.0 + sc2_ref[...]) + sh2_ref[...]).astype(BF16)

    pos = lax.broadcasted_iota(jnp.int32, (tm, 1), 0) % seq
    first = pos == 0
    last = pos == seq - 1

    def conv(u, cols):
        prev = jnp.where(first, 0.0, pltpu.roll(u, 1, axis=0))
        nxt = jnp.where(last, 0.0, pltpu.roll(u, tm - 1, axis=0))
        return (cw_ref[0:1, cols] * prev + cw_ref[1:2, cols] * u + cw_ref[2:3, cols] * nxt
                + cb_ref[:, cols])

    for j in range(D_FF // tf):
        ca = slice(j * tf, (j + 1) * tf)
        cg = slice(D_FF + j * tf, D_FF + (j + 1) * tf)
        a = conv(_dot(hb, wu_ref[:, ca]), ca)
        g = conv(_dot(hb, wu_ref[:, cg]), cg)
        act_ref[:, ca] = (_silu(g) * a).astype(BF16)
    y = _dot(act_ref[...], wd_ref[...])
    o_ref[...] = o_ref[...] + gt2_ref[...] * _rms(y, gpost2_ref[...])


def _resident(shape):
    return pl.BlockSpec(shape, lambda *_: (0,) * len(shape), pipeline_mode=pl.Buffered(1))


def _mix_ffn(parts, w_out, x, mod, first_row, groups, gpost1, gpre2, w_up, conv_w, conv_b, w_down,
             gpost2, tm, seq, tf=256):
    t = x.shape[0]
    per = (t // tm) // groups
    row = lambda i: (i, 0)
    vec = _resident((1, D_MODEL))
    in_specs = [pl.BlockSpec((tm, p.shape[1]), row) for p in parts]
    in_specs += [_resident(w_out.shape), pl.BlockSpec((tm, D_MODEL), row),
                 _mod_spec(MOD_GATE1, first_row, per), vec, vec,
                 _mod_spec(MOD_SCALE2, first_row, per), _mod_spec(MOD_SHIFT2, first_row, per),
                 _resident(w_up.shape), _resident(conv_w.shape), _resident(conv_b.shape),
                 _resident(w_down.shape), _mod_spec(MOD_GATE2, first_row, per), vec]
    return pl.pallas_call(
        functools.partial(_mix_ffn_kernel, n_parts=len(parts), seq=seq, tf=tf),
        out_shape=jax.ShapeDtypeStruct((t, D_MODEL), F32),
        grid=(t // tm,),
        in_specs=in_specs,
        out_specs=pl.BlockSpec((tm, D_MODEL), row),
        scratch_shapes=[pltpu.VMEM((tm, D_FF), BF16)],
        compiler_params=_cparams(("arbitrary",)),
        name="mix_ffn",
    )(*parts, w_out, x, mod, gpost1, gpre2, mod, mod, w_up, conv_w, conv_b, w_down, mod, gpost2)


def _rope(x, cos, sin_signed):
    lane = lax.broadcasted_iota(jnp.int32, x.shape, 1)
    lower = (lane % 32) < 16
    partner = jnp.where(lower, pltpu.roll(x, 128 - 16, axis=1), pltpu.roll(x, 16, axis=1))
    return x * cos + partner * sin_signed


def _p1_kernel(*refs, rope):
    if rope:
        (x_ref, g_ref, sc_ref, sh_ref, w_ref, gq_ref, wq_ref, gkv_ref, cos_ref, sin_ref,
         q_ref, ckv_ref, kr_ref) = refs
    else:
        (x_ref, g_ref, sc_ref, sh_ref, w_ref, gq_ref, wq_ref, gkv_ref,
         q_ref, ckv_ref, kr_ref) = refs
    h = _rms(x_ref[...], g_ref[...]) * (1.0 + sc_ref[...]) + sh_ref[...]
    z = _dot(h.astype(BF16), w_ref[...])
    qa = _rms(z[:, 0:C_Q_RANK], gq_ref[...])
    q = _dot(qa.astype(BF16), wq_ref[...])
    ckv_ref[...] = _rms(z[:, C_Q_RANK:C_Q_RANK + C_KV_RANK], gkv_ref[...])
    kr = z[:, C_Q_RANK + C_KV_RANK:L1_IN_PAD]
    if rope:
        cos = cos_ref[...]
        sin = sin_ref[...]
        kr = _rope(kr, cos, sin)
    kr_ref[...] = kr[:, 0:C_ROPE]
    for hd in range(C_HEADS):
        c0 = hd * QCAT
        q_ref[:, c0:c0 + 128] = q[:, c0:c0 + 128].astype(q_ref.dtype)
        qr = q[:, c0 + 128:c0 + 256]
        if rope:
            qr = _rope(qr, cos, sin)
        q_ref[:, c0 + 128:c0 + 256] = qr.astype(q_ref.dtype)


def _proj_l1(x, g, mod, first_row, groups, w, gq, wq, gkv, tables, tm):
    t = x.shape[0]
    per = (t // tm) // groups
    row = lambda i: (i, 0)
    fixed = lambda i: (0, 0)
    in_specs = [pl.BlockSpec((tm, D_MODEL), row),
                pl.BlockSpec((1, D_MODEL), fixed),
                _mod_spec(MOD_SCALE1, first_row, per),
                _mod_spec(MOD_SHIFT1, first_row, per),
                pl.BlockSpec(w.shape, fixed),
                pl.BlockSpec((1, C_Q_RANK), fixed),
                pl.BlockSpec(wq.shape, fixed),
                pl.BlockSpec((1, C_KV_RANK), fixed)]
    args = [x, g, mod, mod, w, gq, wq, gkv]
    rope = tables is not None
    if rope:
        n = tables[0].shape[0]
        pos = lambda i: (i % (n // tm), 0)
        in_specs += [pl.BlockSpec((tm, 128), pos)] * 2
        args += list(tables)
    return pl.pallas_call(
        functools.partial(_p1_kernel, rope=rope),
        out_shape=(jax.ShapeDtypeStruct((t, C_HEADS * QCAT), BF16),
                   jax.ShapeDtypeStruct((t, C_KV_RANK), F32),
                   jax.ShapeDtypeStruct((t, C_ROPE), F32)),
        grid=(t // tm,),
        in_specs=in_specs,
        out_specs=(pl.BlockSpec((tm, C_HEADS * QCAT), row),
                   pl.BlockSpec((tm, C_KV_RANK), row),
                   pl.BlockSpec((tm, C_ROPE), row)),
        compiler_params=_cparams(("arbitrary",)),
        name="proj_l1",
    )(*args)


def _attn_kernel(*refs, n_parts, scale):
    q_ref = refs[0]
    kv_parts = refs[1:1 + 2 * n_parts]
    w_ref, o_ref, k_sc, v_sc = refs[1 + 2 * n_parts:]

    @pl.when(pl.program_id(1) == 0)
    def _():
        off = 0
        for p in range(n_parts):
            ckv_ref, kr_ref = kv_parts[2 * p], kv_parts[2 * p + 1]
            nk = ckv_ref.shape[0]
            kv = _dot(ckv_ref[...].astype(BF16), w_ref[...])
            kr = kr_ref[...].astype(BF16)
            kr2 = jnp.concatenate([kr, kr], axis=1)
            ones = jnp.ones((nk, C_DV), BF16)
            for hd in range(C_HEADS):
                v0 = C_HEADS * C_NOPE + hd * C_DV
                k_sc[hd, off:off + nk, 0:128] = kv[:, hd * 128:(hd + 1) * 128].astype(BF16)
                k_sc[hd, off:off + nk, 128:256] = kr2
                v_sc[hd, off:off + nk, 0:C_DV] = kv[:, v0:v0 + C_DV].astype(BF16)
                v_sc[hd, off:off + nk, C_DV:2 * C_DV] = ones
            off += nk

    for hd in range(C_HEADS):
        t = _dot_nt(q_ref[:, hd * QCAT:(hd + 1) * QCAT], k_sc[hd]) * (scale * LOG2E)
        p = jnp.exp2(t - jnp.max(t, axis=1, keepdims=True))
        ov = _dot(p.astype(BF16), v_sc[hd])
        o_ref[:, hd * C_DV:(hd + 1) * C_DV] = (ov[:, 0:C_DV] / ov[:, C_DV:2 * C_DV]).astype(o_ref.dtype)


def _attention(q, kv_parts, w_kvb, tq):
    bsz, n, _ = q.shape
    nk = sum(p[0].shape[1] for p in kv_parts)
    in_specs = [pl.BlockSpec((None, tq, C_HEADS * QCAT), lambda b, i: (b, i, 0))]
    args = [q]
    for ckv, kr in kv_parts:
        in_specs.append(pl.BlockSpec((None, ckv.shape[1], C_KV_RANK), lambda b, i: (b, 0, 0)))
        in_specs.append(pl.BlockSpec((None, kr.shape[1], C_ROPE), lambda b, i: (b, 0, 0)))
        args += [ckv, kr]
    in_specs.append(pl.BlockSpec(w_kvb.shape, lambda b, i: (0, 0)))
    args.append(w_kvb)
    return pl.pallas_call(
        functools.partial(_attn_kernel, n_parts=len(kv_parts),
                          scale=(C_NOPE + C_ROPE) ** -0.5),
        out_shape=jax.ShapeDtypeStruct((bsz, n, C_HEADS * C_DV), BF16),
        grid=(bsz, n // tq),
        in_specs=in_specs,
        out_specs=pl.BlockSpec((None, tq, C_HEADS * C_DV), lambda b, i: (b, i, 0)),
        scratch_shapes=[pltpu.VMEM((C_HEADS, nk, QCAT), BF16),
                        pltpu.VMEM((C_HEADS, nk, 2 * C_DV), BF16)],
        compiler_params=_cparams(("arbitrary", "arbitrary")),
        name="mla_attention",
    )(*args)


def _l1_wq_layout(w_qb):
    blocks = []
    zero = jnp.zeros((w_qb.shape[0], C_ROPE), w_qb.dtype)
    for hd in range(C_HEADS):
        base = hd * (C_NOPE + C_ROPE)
        nope = w_qb[:, base:base + C_NOPE]
        rope = w_qb[:, base + C_NOPE:base + C_NOPE + C_ROPE]
        blocks += [nope, rope, zero] if hd % 2 == 0 else [nope, zero, rope]
    return jnp.concatenate(blocks, axis=1).astype(BF16)


def _l1_wkvb_layout(w_kvb):
    w = w_kvb.reshape(C_KV_RANK, C_HEADS, C_NOPE + C_DV)
    k = w[:, :, :C_NOPE].reshape(C_KV_RANK, C_HEADS * C_NOPE)
    v = w[:, :, C_NOPE:].reshape(C_KV_RANK, C_HEADS * C_DV)
    return jnp.concatenate([k, v], axis=1).astype(BF16)


def _rope_tables(n):
    pos = np.arange(n)
    row = (pos // GRID_W).astype(np.float64)
    col = (pos % GRID_W).astype(np.float64)
    half = C_ROPE // 2
    inv = 1.0 / (ROPE_THETA ** (np.arange(0, half, 2, dtype=np.float64) / half))
    ang_r = row[:, None] * inv[None, :]
    ang_c = col[:, None] * inv[None, :]
    cos64 = np.concatenate([np.cos(ang_r)] * 2 + [np.cos(ang_c)] * 2, axis=1)
    sin64 = np.concatenate([-np.sin(ang_r), np.sin(ang_r), -np.sin(ang_c), np.sin(ang_c)], axis=1)
    return (jnp.asarray(np.concatenate([cos64] * 2, axis=1), F32),
            jnp.asarray(np.concatenate([sin64] * 2, axis=1), F32))


def _gate_weight(w, first_row):
    return jnp.pad(w, ((first_row, ZG_W - first_row - A_GATE_RANK), (0, 0)))


def _state_t(s):
    b = s.shape[0]
    return s.transpose(0, 3, 1, 2).reshape(b, 128, HK)


def _n_rows(n_fwd, n_bwd):
    b = n_fwd.shape[0]
    rows = jnp.zeros((b, 8, HK), F32)
    return rows.at[:, 0, :].set(n_fwd.reshape(b, HK)).at[:, 1, :].set(n_bwd.reshape(b, HK))


def _m_rows(m_fwd, m_bwd):
    m = jnp.concatenate([m_fwd, m_bwd], axis=1)
    return jnp.broadcast_to(m[:, :, None], (m.shape[0], 8, 128))


def kernel(x_prompt, x_sample, state_l0_gla_fwd, state_l0_gla_bwd, state_l0_mlstm_c_fwd, state_l0_mlstm_n_fwd, state_l0_mlstm_m_fwd, state_l0_mlstm_c_bwd, state_l0_mlstm_n_bwd, state_l0_mlstm_m_bwd, cache_l1_ckv, cache_l1_krope, c, c_ctx, l0_w_mod, l0_b_mod, l0_g_pre_mix, l0_g_post_mix, l0_g_pre_ffn, l0_g_post_ffn, l0_w_in, l0_gla_w_gate_f, l0_gla_b_gate_f, l0_gla_w_gate_b, l0_gla_b_gate_b, l0_gla_g_norm, l0_mlstm_b_gates, l0_mlstm_g_norm, l0_w_out, l0_ffn_w_up, l0_ffn_conv_w, l0_ffn_conv_b, l0_ffn_w_down, l1_w_mod, l1_b_mod, l1_g_pre_mix, l1_g_post_mix, l1_g_pre_ffn, l1_g_post_ffn, l1_w_in, l1_g_q_norm, l1_w_qb, l1_g_kv_norm, l1_w_kvb, l1_w_out, l1_ffn_w_up, l1_ffn_conv_w, l1_ffn_conv_b, l1_ffn_w_down):
    bp, sp, d = x_prompt.shape
    bs, ss, _ = x_sample.shape
    row = lambda v: v.reshape(1, -1)

    cond8 = jnp.concatenate([c_ctx[None, :], c, jnp.zeros((8 - 1 - bs, d), F32)], axis=0)
    mods = (_modulation(cond8, l0_w_mod, l0_b_mod), _modulation(cond8, l1_w_mod, l1_b_mod))

    paths = {
        "ctx": dict(x=x_prompt.reshape(bp * sp, d), b=bp, n=sp, first_row=0, groups=1, tm=512),
        "lat": dict(x=x_sample.reshape(bs * ss, d), b=bs, n=ss, first_row=1, groups=bs, tm=512),
    }

    w_in0 = l0_w_in.T
    wgf = _gate_weight(l0_gla_w_gate_f, 0)
    wgb = _gate_weight(l0_gla_w_gate_b, A_GATE_RANK)
    bg = jnp.pad(l0_mlstm_b_gates, (GATE_COL, ZG_W - GATE_COL - 16)).reshape(1, ZG_W)
    w_out0 = l0_w_out.astype(BF16)
    w_in1 = jnp.pad(l1_w_in, ((0, 0), (0, L1_IN_PAD - l1_w_in.shape[1]))).astype(BF16)
    w_q1 = _l1_wq_layout(l1_w_qb)
    w_kvb1 = _l1_wkvb_layout(l1_w_kvb)
    w_out1 = l1_w_out.astype(BF16)
    ffn = ((l0_ffn_w_up.astype(BF16), l0_ffn_conv_w, row(l0_ffn_conv_b), l0_ffn_w_down.astype(BF16)),
           (l1_ffn_w_up.astype(BF16), l1_ffn_conv_w, row(l1_ffn_conv_b), l1_ffn_w_down.astype(BF16)))
    norms = ((row(l0_g_pre_mix), row(l0_g_post_mix), row(l0_g_pre_ffn), row(l0_g_post_ffn)),
             (row(l1_g_pre_mix), row(l1_g_post_mix), row(l1_g_pre_ffn), row(l1_g_post_ffn)))

    gla_init = (_state_t(state_l0_gla_fwd), _state_t(state_l0_gla_bwd))
    mlstm_init = (_state_t(state_l0_mlstm_c_fwd), _state_t(state_l0_mlstm_c_bwd),
                  _n_rows(state_l0_mlstm_n_fwd, state_l0_mlstm_n_bwd),
                  _m_rows(state_l0_mlstm_m_fwd, state_l0_mlstm_m_bwd))
    tables = _rope_tables(ss)

    results = {}
    for name, p in paths.items():
        x, b, n, tm = p["x"], p["b"], p["n"], p["tm"]
        first_row, groups = p["first_row"], p["groups"]
        is_ctx = name == "ctx"
        ffn_tm = tm if is_ctx else n
        g_pre, g_post, g_pre2, g_post2 = norms[0]

        za, zb, zg = _proj_l0(x, g_pre, mods[0], first_row, groups, w_in0, tm)
        za, zb, zg = (t.reshape(b, n, -1) for t in (za, zb, zg))
        oa, sa_f, sa_b, om, c_f, c_b, n_rows, m_rows = _recurrent(
            za, zb, zg, wgf, wgb, row(l0_gla_b_gate_f), row(l0_gla_b_gate_b), row(l0_gla_g_norm),
            bg, row(l0_mlstm_g_norm), None if is_ctx else gla_init + mlstm_init)
        x = _mix_ffn([oa.reshape(b * n, HV), om.reshape(b * n, HV)], w_out0, x, mods[0], first_row,
                     groups, g_post, g_pre2, *ffn[0], g_post2, ffn_tm, n)

        g_pre, g_post, g_pre2, g_post2 = norms[1]
        q, ckv, kr = _proj_l1(x, g_pre, mods[1], first_row, groups, w_in1, row(l1_g_q_norm), w_q1,
                              row(l1_g_kv_norm), None if is_ctx else tables, tm)
        ckv3 = ckv.reshape(b, n, C_KV_RANK)
        kr3 = kr.reshape(b, n, C_ROPE)
        kv_parts = [(ckv3, kr3)] if is_ctx else [(cache_l1_ckv, cache_l1_krope), (ckv3, kr3)]
        o = _attention(q.reshape(b, n, -1), kv_parts, w_kvb1, 256)
        x = _mix_ffn([o.reshape(b * n, C_HEADS * C_DV)], w_out1, x, mods[1], first_row,
                     groups, g_post, g_pre2, *ffn[1], g_post2, ffn_tm, n)

        results[name] = dict(y=x.reshape(b, n, d), gla=(sa_f, sa_b), c=(c_f, c_b),
                             n=n_rows, m=m_rows, ckv=ckv3, kr=kr3)

    r = results["ctx"]
    n_fwd = r["n"][:, 0, :].reshape(bp, B_HEADS, B_DK)
    n_bwd = r["n"][:, 1, :].reshape(bp, B_HEADS, B_DK)
    m_fwd = r["m"][:, 0:B_HEADS, 0]
    m_bwd = r["m"][:, B_HEADS:2 * B_HEADS, 0]
    return (r["y"], results["lat"]["y"], r["gla"][0], r["gla"][1],
            r["c"][0], n_fwd, m_fwd, r["c"][1], n_bwd, m_bwd, r["ckv"], r["kr"])
```

```python
import functools

import numpy as np
import jax
import jax.numpy as jnp
from jax import lax
from jax.experimental import pallas as pl
from jax.experimental.pallas import tpu as pltpu

F32 = jnp.float32
BF16 = jnp.bfloat16

D_MODEL = 1024
EPS = 1e-6
LOG2E = 1.4426950408889634
N_MOD = 6
GRID_W = 64
ROPE_THETA = 10000.0

A_HEADS, A_DK, A_DV, A_GATE_RANK, A_GATE_TEMP = 4, 64, 128, 16, 16.0
B_HEADS, B_DK, B_DV = 4, 64, 128
C_HEADS, C_Q_RANK, C_KV_RANK, C_NOPE, C_ROPE, C_DV = 8, 384, 256, 128, 64, 128
D_FF = 2816
CONV_W = 3

HEADS = 4
HK = HEADS * 64
HV = HEADS * 128
ZA_W = 1536
ZG_W = 128
GATE_COL = 32
RCHUNK = 128
L1_IN_PAD = 768
QCAT = 256

VMEM_LIMIT = 56 * 1024 * 1024


def _cparams(sem, vmem=VMEM_LIMIT):
    return pltpu.CompilerParams(dimension_semantics=sem, vmem_limit_bytes=vmem)


def _dot(a, b):
    return jnp.dot(a, b, preferred_element_type=F32)


def _dot_nt(a, b):
    return lax.dot_general(a, b, (((1,), (1,)), ((), ())), preferred_element_type=F32)


def _dot_tn(a, b):
    return lax.dot_general(a, b, (((0,), (0,)), ((), ())), preferred_element_type=F32)


def _split3(x):
    hi = x.astype(BF16)
    r1 = x - hi.astype(F32)
    mid = r1.astype(BF16)
    lo = (r1 - mid.astype(F32)).astype(BF16)
    return hi, mid, lo


def _tri_dot(tri, x):
    hi, mid, lo = _split3(x)
    return _dot(tri, hi) + _dot(tri, mid) + _dot(tri, lo)


def _dot_tri(x, tri):
    hi, mid, lo = _split3(x)
    return _dot(hi, tri) + _dot(mid, tri) + _dot(lo, tri)


def _dot_hl(a, b):
    ah = a.astype(BF16)
    al = (a - ah.astype(F32)).astype(BF16)
    bh = b.astype(BF16)
    bl = (b - bh.astype(F32)).astype(BF16)
    return _dot(ah, bh) + _dot(ah, bl) + _dot(al, bh)


def _rms(x, g):
    return x * lax.rsqrt(jnp.mean(x * x, axis=-1, keepdims=True) + EPS) * g


def _sigmoid(x):
    return 1.0 / (1.0 + jnp.exp(-x))


def _silu(x):
    return x * _sigmoid(x)


def _log_sigmoid(x):
    return jnp.minimum(x, 0.0) - jnp.log(1.0 + jnp.exp(-jnp.abs(x)))


def _tri_masks(n):
    r = lax.broadcasted_iota(jnp.int32, (n, n), 0)
    c = lax.broadcasted_iota(jnp.int32, (n, n), 1)
    return c <= r, c >= r


def _mod_kernel(cond_ref, w_ref, b_ref, o_ref):
    s = _silu(cond_ref[...])
    o_ref[...] = _dot(s.astype(BF16), w_ref[...].astype(BF16)) + b_ref[...]


def _modulation(cond8, w_mod, b_mod):
    d = D_MODEL
    out = pl.pallas_call(
        _mod_kernel,
        out_shape=jax.ShapeDtypeStruct((N_MOD, 8, d), F32),
        grid=(N_MOD,),
        in_specs=[pl.BlockSpec((8, d), lambda j: (0, 0)),
                  pl.BlockSpec((d, d), lambda j: (0, j)),
                  pl.BlockSpec((1, d), lambda j: (0, j))],
        out_specs=pl.BlockSpec((None, 8, d), lambda j: (j, 0, 0)),
        compiler_params=_cparams(("arbitrary",)),
        name="modulation",
    )(cond8, w_mod, b_mod.reshape(1, N_MOD * d))
    return out.reshape(N_MOD, 8, 1, d)


MOD_SHIFT1, MOD_SCALE1, MOD_GATE1, MOD_SHIFT2, MOD_SCALE2, MOD_GATE2 = range(N_MOD)


def _mod_spec(which, first_row, per):
    return pl.BlockSpec((None, None, 1, D_MODEL), lambda i: (which, first_row + i // per, 0, 0))


L0_GROUP_ROWS = ((0, ZA_W), (ZA_W + 32, 2 * ZA_W + 32))
L0_LR_ROWS = (ZA_W, ZA_W + 32)
L0_GATE_ROWS = (2 * ZA_W + 32, 2 * ZA_W + 48)


def _p0_kernel(x_ref, g_ref, sc_ref, sh_ref, wt_ref, za_ref, zb_ref, zg_ref, w_ref):
    @pl.when(pl.program_id(0) == 0)
    def _():
        blk = 256
        for gi, (r0, r1) in enumerate(L0_GROUP_ROWS):
            for j in range((r1 - r0) // blk):
                rows = wt_ref[r0 + j * blk:r0 + (j + 1) * blk, :]
                w_ref[:, gi * ZA_W + j * blk:gi * ZA_W + (j + 1) * blk] = rows.T.astype(BF16)
        tail = jnp.concatenate([wt_ref[L0_LR_ROWS[0]:L0_LR_ROWS[1], :],
                                wt_ref[L0_GATE_ROWS[0]:L0_GATE_ROWS[1], :],
                                jnp.zeros((ZG_W - 48, D_MODEL), F32)], axis=0)
        w_ref[:, 2 * ZA_W:2 * ZA_W + ZG_W] = tail.T.astype(BF16)

    h = _rms(x_ref[...], g_ref[...]) * (1.0 + sc_ref[...]) + sh_ref[...]
    hb = h.astype(BF16)
    za_ref[...] = _dot(hb, w_ref[:, 0:ZA_W])
    zb_ref[...] = _dot(hb, w_ref[:, ZA_W:2 * ZA_W])
    zg_ref[...] = _dot(hb, w_ref[:, 2 * ZA_W:2 * ZA_W + ZG_W])


def _proj_l0(x, g, mod, first_row, groups, w_t, tm):
    t = x.shape[0]
    per = (t // tm) // groups
    row = lambda i: (i, 0)
    fixed = lambda i: (0, 0)
    return pl.pallas_call(
        _p0_kernel,
        out_shape=(jax.ShapeDtypeStruct((t, ZA_W), F32),
                   jax.ShapeDtypeStruct((t, ZA_W), F32),
                   jax.ShapeDtypeStruct((t, ZG_W), F32)),
        grid=(t // tm,),
        in_specs=[pl.BlockSpec((tm, D_MODEL), row),
                  pl.BlockSpec((1, D_MODEL), fixed),
                  _mod_spec(MOD_SCALE1, first_row, per),
                  _mod_spec(MOD_SHIFT1, first_row, per),
                  _resident(w_t.shape)],
        out_specs=(pl.BlockSpec((tm, ZA_W), row),
                   pl.BlockSpec((tm, ZA_W), row),
                   pl.BlockSpec((tm, ZG_W), row)),
        scratch_shapes=[pltpu.VMEM((D_MODEL, 2 * ZA_W + ZG_W), BF16)],
        compiler_params=_cparams(("arbitrary",)),
        name="proj_l0",
    )(x, g, mod, mod, w_t)


def _head_lane_mask(shape, h):
    lane = lax.broadcasted_iota(jnp.int32, shape, len(shape) - 1)
    return (lane >= h * 64) & (lane < (h + 1) * 64)


def _head_row(h):
    return jnp.where(_head_lane_mask((1, HK), h), 1.0, 0.0).astype(BF16)


def _gla_prepare(za_ref, la_ref, r0, rev, tri):
    L = RCHUNK
    rows = pl.ds(r0, L)
    b = _tri_dot(tri, la_ref[rows, :])
    mid = L // 2
    ref = b[mid:mid + 1, :]
    b_end = b[0:1, :] if rev else b[L - 1:L, :]
    q = za_ref[rows, 0:HK]
    k = za_ref[rows, HK:2 * HK]
    v = za_ref[rows, 2 * HK:2 * HK + HV].astype(BF16)
    qe = (q * (jnp.exp(b - ref) * (A_DK ** -0.5))).astype(BF16)
    ke = (k * jnp.exp(ref - b)).astype(BF16)
    qm = jnp.concatenate([qe * _head_row(h) for h in range(HEADS)], axis=0)
    rt = _dot_tn(v, ke)
    upd = None
    for h in range(HEADS):
        blk = rt[h * 128:(h + 1) * 128, :]
        blk = jnp.where(_head_lane_mask(blk.shape, h), blk, 0.0)
        upd = blk if upd is None else upd + blk
    return dict(rows=rows, qm=qm, ke=ke, v=v, upd=upd * jnp.exp(b_end - ref),
                decay=jnp.exp(b_end), eref=jnp.exp(ref))


def _gla_apply(items):
    L = RCHUNK
    sts = [st_ref[...] for _, st_ref, _, _ in items]
    o1s = []
    for (p, _, _, _), st in zip(items, sts):
        rhs = jnp.concatenate([p["ke"], (st * p["eref"]).astype(BF16)], axis=0)
        o1s.append(_dot_nt(p["qm"], rhs))
    for h in range(HEADS):
        for (p, _, o_ref, mask), o1 in zip(items, o1s):
            blk = o1[h * L:(h + 1) * L, :]
            sc = jnp.where(mask, blk[:, 0:L], 0.0).astype(BF16)
            o_ref[p["rows"], h * 128:(h + 1) * 128] = (
                blk[:, L:L + 128] + _dot(sc, p["v"][:, h * 128:(h + 1) * 128]))
    for (p, st_ref, _, _), st in zip(items, sts):
        st_ref[...] = st * p["decay"] + p["upd"]


T_M, T_MOUT, T_G, T_SRCW, T_MPREV, T_CARRY, T_USED = 0, 8, 16, 24, 32, 40, 48


def _lane_scan(x, op, fill, is_fwd, pos, n):
    k = 1
    while k < RCHUNK:
        pf = jnp.where(pos >= k, pltpu.roll(x, k, axis=1), fill)
        sf = jnp.where(pos < RCHUNK - k, pltpu.roll(x, n - k, axis=1), fill)
        x = op(x, jnp.where(is_fwd, pf, sf))
        k *= 2
    return x


def _mlstm_gate_tables(g, m0, gt_ref, tab_ref, col_ref, n):
    L = RCHUNK
    nc = n // L
    for c in range(nc):
        gt_ref[:, c * L:(c + 1) * L] = g[c * L:(c + 1) * L, :].T
        tab_ref[c, T_USED:, :] = jnp.zeros((ZG_W - T_USED, L), F32)
    gi = gt_ref[GATE_COL:GATE_COL + 8, :]
    lf = gt_ref[GATE_COL + 8:GATE_COL + 16, :]
    is_fwd = lax.broadcasted_iota(jnp.int32, (8, n), 0) < HEADS
    pos = lax.broadcasted_iota(jnp.int32, (8, n), 1) % L
    b = _lane_scan(lf, jnp.add, 0.0, is_fwd, pos, n)
    gg = gi - b
    cmax = _lane_scan(gg, jnp.maximum, -jnp.inf, is_fwd, pos, n)
    fwd_l = lax.broadcasted_iota(jnp.int32, (8, L), 0) < HEADS
    mp = m0
    for j in range(nc):
        cf, cb = j, nc - 1 - j

        def pick(x):
            return jnp.where(fwd_l, x[:, cf * L:(cf + 1) * L], x[:, cb * L:(cb + 1) * L])

        gj, bj, cj, lj = pick(gg), pick(b), pick(cmax), pick(lf)
        mx = jnp.maximum(mp, jnp.max(gj, axis=1, keepdims=True))
        mj = jnp.maximum(cj, mp)
        blocks = (mj, bj + mj, gj, jnp.exp(gj - mx),
                  jnp.broadcast_to(mp, (8, L)), jnp.broadcast_to(jnp.exp(mp - mx), (8, L)))
        for t, val in enumerate(blocks):
            tab_ref[cf, 8 * t:8 * t + HEADS, :] = val[0:HEADS]
            tab_ref[cb, 8 * t + HEADS:8 * t + 8, :] = val[HEADS:8]
        mp = jnp.sum(lj, axis=1, keepdims=True) + mx
    for c in range(nc):
        col_ref[c * L:(c + 1) * L, :] = tab_ref[c].T
    return mp


def _mlstm_prepare(zb_ref, tab_ref, col_ref, c, d, mask):
    L = RCHUNK
    rows = pl.ds(pl.multiple_of(c * L, L), L)
    qb = zb_ref[rows, 0:HK].astype(BF16)
    kb = (zb_ref[rows, HK:2 * HK] * (B_DK ** -0.5)).astype(BF16)
    v = zb_ref[rows, 2 * HK:2 * HK + HV]
    vb = v.astype(BF16)
    vt = v.T
    qm = jnp.concatenate([qb * _head_row(h) for h in range(HEADS)], axis=0)
    s = _dot_nt(qm, kb)
    ones = jnp.ones((L, 128), BF16)
    pv, w_inter, e_negm = [], [], []
    upd = None
    carry = None
    for h in range(HEADS):
        r = 4 * d + h
        m_rep = jnp.broadcast_to(col_ref[rows, T_M + r:T_M + r + 1], (L, 128))
        mout_rep = jnp.broadcast_to(col_ref[rows, T_MOUT + r:T_MOUT + r + 1], (L, 128))
        g_row = tab_ref[c, T_G + r:T_G + r + 1, :]
        w = jnp.exp(jnp.where(mask, g_row - m_rep, -jnp.inf))
        qk = (s[h * L:(h + 1) * L, :] * w).astype(BF16)
        v1 = jnp.concatenate([vb[:, h * 128:(h + 1) * 128], ones], axis=1)
        pv.append(_dot(qk, v1))
        w_inter.append(jnp.exp(tab_ref[c, T_MPREV + r:T_MPREV + r + 1, :] - m_rep))
        e_negm.append(jnp.exp(-mout_rep))
        sw = tab_ref[c, T_SRCW + r:T_SRCW + r + 1, :]
        lhs = jnp.concatenate([vt[h * 128:(h + 1) * 128, :] * sw, jnp.broadcast_to(sw, (128, L))],
                              axis=0).astype(BF16)
        blk = _dot(lhs, kb)
        blk = jnp.where(_head_lane_mask(blk.shape, h), blk, 0.0)
        upd = blk if upd is None else upd + blk
        cr = tab_ref[c, T_CARRY + r:T_CARRY + r + 1, :]
        cr = jnp.where(_head_lane_mask((1, HK), h), jnp.concatenate([cr, cr], axis=1), 0.0)
        carry = cr if carry is None else carry + cr
    return dict(rows=rows, qm=qm, pv=pv, w_inter=w_inter, e_negm=e_negm, upd=upd, carry=carry)


def _mlstm_apply(items):
    L = RCHUNK
    sts = [st_ref[...] for _, st_ref, _ in items]
    a_all = [_dot_nt(p["qm"], st.astype(BF16)) for (p, _, _), st in zip(items, sts)]
    for h in range(HEADS):
        for (p, _, h_ref), a in zip(items, a_all):
            blk = a[h * L:(h + 1) * L, :]
            pv, w, e = p["pv"][h], p["w_inter"][h], p["e_negm"][h]
            num = w * blk[:, 0:128] + pv[:, 0:128]
            den = w * blk[:, 128:256] + pv[:, 128:256]
            h_ref[p["rows"], h * 128:(h + 1) * 128] = num / jnp.maximum(jnp.abs(den), e)
    for (p, st_ref, _), st in zip(items, sts):
        st_ref[...] = p["carry"] * st + p["upd"]


def _rec_kernel(*refs, n, has_init):
    it = iter(refs)

    def take(k):
        return [next(it) for _ in range(k)]

    za_ref, zb_ref, zg_ref, zgn_ref = take(4)
    wgf_ref, wgb_ref, bgf_ref, bgb_ref, gna_ref, bg_ref, gnb_ref = take(7)
    if has_init:
        sf0_ref, sb0_ref, cf0_ref, cb0_ref, n0_ref, m0_ref, m0n_ref = take(7)
    oa_ref, sf_ref, sb_ref, om_ref, cf_ref, cb_ref, n_out_ref, m_out_ref = take(8)
    (gt_ref,) = take(1)
    set_a = take(5)
    set_b = take(5)
    of_ref, ob_ref, hf_ref, hb_ref, gsf_ref, gsb_ref, msf_ref, msb_ref = take(8)
    L = RCHUNK
    nc = n // L
    step = pl.program_id(0)

    def fill(gate_ref, m0, bufs):
        laf_ref, lab_ref, tab_ref, col_ref, mf_ref = bufs
        zg = gate_ref[...]
        laf_ref[...] = _log_sigmoid(_dot_hl(zg, wgf_ref[...]) + bgf_ref[...]) / A_GATE_TEMP
        lab_ref[...] = _log_sigmoid(_dot_hl(zg, wgb_ref[...]) + bgb_ref[...]) / A_GATE_TEMP
        g = zg + bg_ref[...]
        lane = lax.broadcasted_iota(jnp.int32, g.shape, 1)
        is_forget = (lane >= GATE_COL + 8) & (lane < GATE_COL + 16)
        g = jnp.where(is_forget, _log_sigmoid(g), g)
        m_fin = _mlstm_gate_tables(g, m0, gt_ref, tab_ref, col_ref, n)
        mf_ref[...] = jnp.broadcast_to(m_fin, (8, 128))

    if has_init:
        m0, m0_next = m0_ref[:, 0:1], m0n_ref[:, 0:1]
        gsf_ref[...] = sf0_ref[...]
        gsb_ref[...] = sb0_ref[...]
        msf_ref[...] = jnp.concatenate(
            [cf0_ref[...], jnp.broadcast_to(n0_ref[0:1, :], (128, HK))], axis=0)
        msb_ref[...] = jnp.concatenate(
            [cb0_ref[...], jnp.broadcast_to(n0_ref[1:2, :], (128, HK))], axis=0)
    else:
        m0 = m0_next = jnp.zeros((8, 1), F32)
        for ref in (gsf_ref, gsb_ref, msf_ref, msb_ref):
            ref[...] = jnp.zeros_like(ref)

    @pl.when(step == 0)
    def _():
        fill(zg_ref, m0, set_a)

    m_lo, m_up = _tri_masks(L)
    tri_lo = jnp.where(m_lo, 1.0, 0.0).astype(BF16)
    tri_up = jnp.where(m_up, 1.0, 0.0).astype(BF16)
    gna = gna_ref[...]
    gnb = gnb_ref[...]

    def out_body(c, carry):
        rows = pl.ds(pl.multiple_of(c * L, L), L)
        for h in range(HEADS):
            cols = slice(h * 128, (h + 1) * 128)
            gcols = slice(2 * HK + HV + h * 128, 2 * HK + HV + (h + 1) * 128)
            o = of_ref[rows, cols] + ob_ref[rows, cols]
            oa_ref[rows, cols] = (_rms(o, gna) * _silu(za_ref[rows, gcols])).astype(oa_ref.dtype)
            hh = hf_ref[rows, cols] + hb_ref[rows, cols]
            om_ref[rows, cols] = (_sigmoid(zb_ref[rows, gcols]) * _rms(hh, gnb)).astype(om_ref.dtype)
        return carry

    def run(cur, nxt):
        laf_ref, lab_ref, tab_ref, col_ref, mf_ref = cur
        fill(zgn_ref, m0_next, nxt)

        def scan_body(gidx, carry):
            gla_waves, mlstm_waves = [], []
            for w in range(2):
                c = 2 * gidx + w
                cb = nc - 1 - c
                gla_waves.append(
                    (_gla_prepare(za_ref, laf_ref, pl.multiple_of(c * L, L), False, tri_lo),
                     _gla_prepare(za_ref, lab_ref, pl.multiple_of(cb * L, L), True, tri_up)))
                mlstm_waves.append((_mlstm_prepare(zb_ref, tab_ref, col_ref, c, 0, m_lo),
                                    _mlstm_prepare(zb_ref, tab_ref, col_ref, cb, 1, m_up)))
            for (gf, gb), (mf, mb) in zip(gla_waves, mlstm_waves):
                _gla_apply([(gf, gsf_ref, of_ref, m_lo), (gb, gsb_ref, ob_ref, m_up)])
                _mlstm_apply([(mf, msf_ref, hf_ref), (mb, msb_ref, hb_ref)])
            return carry

        if nc == 2:
            scan_body(0, 0)
            for c in range(nc):
                out_body(c, 0)
        else:
            lax.fori_loop(0, nc // 2, scan_body, 0)
            lax.fori_loop(0, nc, out_body, 0)
        m_out_ref[...] = mf_ref[...]

    @pl.when(step % 2 == 0)
    def _():
        run(set_a, set_b)

    @pl.when(step % 2 == 1)
    def _():
        run(set_b, set_a)

    gsf_t = gsf_ref[...].T
    gsb_t = gsb_ref[...].T
    msf = msf_ref[...]
    msb = msb_ref[...]
    ctf_t = msf[0:128, :].T
    ctb_t = msb[0:128, :].T
    for h in range(HEADS):
        rows = slice(h * 64, (h + 1) * 64)
        sf_ref[h] = gsf_t[rows, :]
        sb_ref[h] = gsb_t[rows, :]
        cf_ref[h] = ctf_t[rows, :]
        cb_ref[h] = ctb_t[rows, :]
    n_out_ref[...] = jnp.concatenate(
        [msf[128:129, :], msb[128:129, :], jnp.zeros((6, HK), F32)], axis=0)


def _recurrent(za, zb, zg, wgf, wgb, bgf, bgb, gna, bg, gnb, init):
    bsz, n, _ = za.shape
    has_init = init is not None
    nc = n // RCHUNK
    per_b3 = lambda b: (b, 0, 0)
    next_b3 = lambda b: (jnp.minimum(b + 1, bsz - 1), 0, 0)
    per_b4 = lambda b: (b, 0, 0, 0)
    fixed = lambda b: (0, 0)
    in_specs = [pl.BlockSpec((None, n, ZA_W), per_b3),
                pl.BlockSpec((None, n, ZA_W), per_b3),
                pl.BlockSpec((None, n, ZG_W), per_b3),
                pl.BlockSpec((None, n, ZG_W), next_b3),
                pl.BlockSpec((ZG_W, HK), fixed),
                pl.BlockSpec((ZG_W, HK), fixed),
                pl.BlockSpec((1, HK), fixed),
                pl.BlockSpec((1, HK), fixed),
                pl.BlockSpec((1, 128), fixed),
                pl.BlockSpec((1, ZG_W), fixed),
                pl.BlockSpec((1, 128), fixed)]
    args = [za, zb, zg, zg, wgf, wgb, bgf, bgb, gna, bg, gnb]
    if has_init:
        sf0, sb0, cf0, cb0, n0, m0 = init
        in_specs += [pl.BlockSpec((None, 128, HK), per_b3)] * 4
        in_specs += [pl.BlockSpec((None, 8, HK), per_b3), pl.BlockSpec((None, 8, 128), per_b3),
                     pl.BlockSpec((None, 8, 128), next_b3)]
        args += [sf0, sb0, cf0, cb0, n0, m0, m0]
    state = jax.ShapeDtypeStruct((bsz, HEADS, 64, 128), F32)
    state_spec = pl.BlockSpec((None, HEADS, 64, 128), per_b4)
    gate_set = [pltpu.VMEM((n, HK), F32), pltpu.VMEM((n, HK), F32),
                pltpu.VMEM((nc, ZG_W, RCHUNK), F32), pltpu.VMEM((n, ZG_W), F32),
                pltpu.VMEM((8, 128), F32)]
    return pl.pallas_call(
        functools.partial(_rec_kernel, n=n, has_init=has_init),
        out_shape=(jax.ShapeDtypeStruct((bsz, n, HV), BF16), state, state,
                   jax.ShapeDtypeStruct((bsz, n, HV), BF16), state, state,
                   jax.ShapeDtypeStruct((bsz, 8, HK), F32),
                   jax.ShapeDtypeStruct((bsz, 8, 128), F32)),
        grid=(bsz,),
        in_specs=in_specs,
        out_specs=(pl.BlockSpec((None, n, HV), per_b3), state_spec, state_spec,
                   pl.BlockSpec((None, n, HV), per_b3), state_spec, state_spec,
                   pl.BlockSpec((None, 8, HK), per_b3),
                   pl.BlockSpec((None, 8, 128), per_b3)),
        scratch_shapes=([pltpu.VMEM((ZG_W, n), F32)] + gate_set + gate_set
                        + [pltpu.VMEM((n, HV), F32)] * 4
                        + [pltpu.VMEM((128, HK), F32)] * 2 + [pltpu.VMEM((2 * 128, HK), F32)] * 2),
        compiler_params=_cparams(("arbitrary",)),
        name="recurrent",
    )(*args)


def _mix_residual(parts, wo_ref, x_ref, gt1_ref, gpost1_ref, gpre2_ref, sc2_ref, sh2_ref):
    m = None
    off = 0
    for p in parts:
        kp = p.shape[1]
        t = _dot(p[...], wo_ref[off:off + kp, :])
        m = t if m is None else m + t
        off += kp
    x1 = x_ref[...] + gt1_ref[...] * _rms(m, gpost1_ref[...])
    hb = (_rms(x1, gpre2_ref[...]) * (1.0 + sc2_ref[...]) + sh2_ref[...]).astype(BF16)
    return x1, hb


def _ffn_gated_up(hb, wu_ref, cw_ref, cb_ref, act_ref, seq, tf):
    tm = hb.shape[0]
    pos = lax.broadcasted_iota(jnp.int32, (tm, 1), 0) % seq
    first = pos == 0
    last = pos == seq - 1

    def conv(u, cols):
        prev = jnp.where(first, 0.0, pltpu.roll(u, 1, axis=0))
        nxt = jnp.where(last, 0.0, pltpu.roll(u, tm - 1, axis=0))
        return (cw_ref[0:1, cols] * prev + cw_ref[1:2, cols] * u + cw_ref[2:3, cols] * nxt
                + cb_ref[:, cols])

    for j in range(D_FF // tf):
        ca = slice(j * tf, (j + 1) * tf)
        cg = slice(D_FF + j * tf, D_FF + (j + 1) * tf)
        a = conv(_dot(hb, wu_ref[:, ca]), ca)
        g = conv(_dot(hb, wu_ref[:, cg]), cg)
        act_ref[:, ca] = (_silu(g) * a).astype(BF16)


def _mix_ffn_kernel(*refs, n_parts, seq, tf):
    parts = refs[:n_parts]
    (wo_ref, x_ref, gt1_ref, gpost1_ref, gpre2_ref, sc2_ref, sh2_ref,
     wu_ref, cw_ref, cb_ref, wd_ref, gt2_ref, gpost2_ref, o_ref, act_ref) = refs[n_parts:]
    x1, hb = _mix_residual(parts, wo_ref, x_ref, gt1_ref, gpost1_ref, gpre2_ref, sc2_ref, sh2_ref)
    o_ref[...] = x1
    _ffn_gated_up(hb, wu_ref, cw_ref, cb_ref, act_ref, seq, tf)
    y = _dot(act_ref[...], wd_ref[...])
    o_ref[...] = o_ref[...] + gt2_ref[...] * _rms(y, gpost2_ref[...])


def _resident(shape):
    return pl.BlockSpec(shape, lambda *_: (0,) * len(shape), pipeline_mode=pl.Buffered(1))


def _mix_ffn(parts, w_out, x, mod, first_row, groups, gpost1, gpre2, w_up, conv_w, conv_b, w_down,
             gpost2, tm, seq, tf=256):
    t = x.shape[0]
    per = (t // tm) // groups
    row = lambda i: (i, 0)
    vec = _resident((1, D_MODEL))
    in_specs = [pl.BlockSpec((tm, p.shape[1]), row) for p in parts]
    in_specs += [_resident(w_out.shape), pl.BlockSpec((tm, D_MODEL), row),
                 _mod_spec(MOD_GATE1, first_row, per), vec, vec,
                 _mod_spec(MOD_SCALE2, first_row, per), _mod_spec(MOD_SHIFT2, first_row, per),
                 _resident(w_up.shape), _resident(conv_w.shape), _resident(conv_b.shape),
                 _resident(w_down.shape), _mod_spec(MOD_GATE2, first_row, per), vec]
    return pl.pallas_call(
        functools.partial(_mix_ffn_kernel, n_parts=len(parts), seq=seq, tf=tf),
        out_shape=jax.ShapeDtypeStruct((t, D_MODEL), F32),
        grid=(t // tm,),
        in_specs=in_specs,
        out_specs=pl.BlockSpec((tm, D_MODEL), row),
        scratch_shapes=[pltpu.VMEM((tm, D_FF), BF16)],
        compiler_params=_cparams(("arbitrary",)),
        name="mix_ffn",
    )(*parts, w_out, x, mod, gpost1, gpre2, mod, mod, w_up, conv_w, conv_b, w_down, mod, gpost2)


def _rope(x, cos, sin_signed):
    lane = lax.broadcasted_iota(jnp.int32, x.shape, 1)
    lower = (lane % 32) < 16
    partner = jnp.where(lower, pltpu.roll(x, 128 - 16, axis=1), pltpu.roll(x, 16, axis=1))
    return x * cos + partner * sin_signed


def _p1_kernel(*refs, rope):
    if rope:
        (x_ref, g_ref, sc_ref, sh_ref, w_ref, gq_ref, wq_ref, gkv_ref, cos_ref, sin_ref,
         q_ref, ckv_ref, kr_ref) = refs
    else:
        (x_ref, g_ref, sc_ref, sh_ref, w_ref, gq_ref, wq_ref, gkv_ref,
         q_ref, ckv_ref, kr_ref) = refs
    h = _rms(x_ref[...], g_ref[...]) * (1.0 + sc_ref[...]) + sh_ref[...]
    z = _dot(h.astype(BF16), w_ref[...])
    qa = _rms(z[:, 0:C_Q_RANK], gq_ref[...])
    q = _dot(qa.astype(BF16), wq_ref[...])
    ckv_ref[...] = _rms(z[:, C_Q_RANK:C_Q_RANK + C_KV_RANK], gkv_ref[...])
    kr = z[:, C_Q_RANK + C_KV_RANK:L1_IN_PAD]
    if rope:
        cos = cos_ref[...]
        sin = sin_ref[...]
        kr = _rope(kr, cos, sin)
    kr_ref[...] = kr[:, 0:C_ROPE]
    for hd in range(C_HEADS):
        c0 = hd * QCAT
        q_ref[:, c0:c0 + 128] = q[:, c0:c0 + 128].astype(q_ref.dtype)
        qr = q[:, c0 + 128:c0 + 256]
        if rope:
            qr = _rope(qr, cos, sin)
        q_ref[:, c0 + 128:c0 + 256] = qr.astype(q_ref.dtype)


def _proj_l1(x, g, mod, first_row, groups, w, gq, wq, gkv, tables, tm):
    t = x.shape[0]
    per = (t // tm) // groups
    row = lambda i: (i, 0)
    fixed = lambda i: (0, 0)
    in_specs = [pl.BlockSpec((tm, D_MODEL), row),
                pl.BlockSpec((1, D_MODEL), fixed),
                _mod_spec(MOD_SCALE1, first_row, per),
                _mod_spec(MOD_SHIFT1, first_row, per),
                pl.BlockSpec(w.shape, fixed),
                pl.BlockSpec((1, C_Q_RANK), fixed),
                pl.BlockSpec(wq.shape, fixed),
                pl.BlockSpec((1, C_KV_RANK), fixed)]
    args = [x, g, mod, mod, w, gq, wq, gkv]
    rope = tables is not None
    if rope:
        n = tables[0].shape[0]
        pos = lambda i: (i % (n // tm), 0)
        in_specs += [pl.BlockSpec((tm, 128), pos)] * 2
        args += list(tables)
    return pl.pallas_call(
        functools.partial(_p1_kernel, rope=rope),
        out_shape=(jax.ShapeDtypeStruct((t, C_HEADS * QCAT), BF16),
                   jax.ShapeDtypeStruct((t, C_KV_RANK), F32),
                   jax.ShapeDtypeStruct((t, C_ROPE), F32)),
        grid=(t // tm,),
        in_specs=in_specs,
        out_specs=(pl.BlockSpec((tm, C_HEADS * QCAT), row),
                   pl.BlockSpec((tm, C_KV_RANK), row),
                   pl.BlockSpec((tm, C_ROPE), row)),
        compiler_params=_cparams(("arbitrary",)),
        name="proj_l1",
    )(*args)


def _attn_kernel(*refs, n_parts, scale):
    q_ref = refs[0]
    kv_parts = refs[1:1 + 2 * n_parts]
    w_ref, o_ref, k_sc, v_sc = refs[1 + 2 * n_parts:]
    bt = q_ref.shape[0]

    @pl.when(pl.program_id(1) == 0)
    def _():
        off = 0
        for p in range(n_parts):
            ckv_ref, kr_ref = kv_parts[2 * p], kv_parts[2 * p + 1]
            nk = ckv_ref.shape[1]
            ones = jnp.ones((nk, C_DV), BF16)
            for e in range(bt):
                kv = _dot(ckv_ref[e].astype(BF16), w_ref[...])
                kr = kr_ref[e].astype(BF16)
                kr2 = jnp.concatenate([kr, kr], axis=1)
                for hd in range(C_HEADS):
                    v0 = C_HEADS * C_NOPE + hd * C_DV
                    k_sc[e, hd, off:off + nk, 0:128] = kv[:, hd * 128:(hd + 1) * 128].astype(BF16)
                    k_sc[e, hd, off:off + nk, 128:256] = kr2
                    v_sc[e, hd, off:off + nk, 0:C_DV] = kv[:, v0:v0 + C_DV].astype(BF16)
                    v_sc[e, hd, off:off + nk, C_DV:2 * C_DV] = ones
            off += nk

    for hd in range(C_HEADS):
        for e in range(bt):
            t = _dot_nt(q_ref[e, :, hd * QCAT:(hd + 1) * QCAT], k_sc[e, hd]) * (scale * LOG2E)
            p = jnp.exp2(t - jnp.max(t, axis=1, keepdims=True))
            ov = _dot(p.astype(BF16), v_sc[e, hd])
            o_ref[e, :, hd * C_DV:(hd + 1) * C_DV] = (
                ov[:, 0:C_DV] / ov[:, C_DV:2 * C_DV]).astype(o_ref.dtype)


def _attention(q, kv_parts, w_kvb, tq, bt):
    bsz, n, _ = q.shape
    nk = sum(p[0].shape[1] for p in kv_parts)
    in_specs = [pl.BlockSpec((bt, tq, C_HEADS * QCAT), lambda b, i: (b, i, 0))]
    args = [q]
    for ckv, kr in kv_parts:
        in_specs.append(pl.BlockSpec((bt, ckv.shape[1], C_KV_RANK), lambda b, i: (b, 0, 0)))
        in_specs.append(pl.BlockSpec((bt, kr.shape[1], C_ROPE), lambda b, i: (b, 0, 0)))
        args += [ckv, kr]
    in_specs.append(pl.BlockSpec(w_kvb.shape, lambda b, i: (0, 0)))
    args.append(w_kvb)
    return pl.pallas_call(
        functools.partial(_attn_kernel, n_parts=len(kv_parts),
                          scale=(C_NOPE + C_ROPE) ** -0.5),
        out_shape=jax.ShapeDtypeStruct((bsz, n, C_HEADS * C_DV), BF16),
        grid=(bsz // bt, n // tq),
        in_specs=in_specs,
        out_specs=pl.BlockSpec((bt, tq, C_HEADS * C_DV), lambda b, i: (b, i, 0)),
        scratch_shapes=[pltpu.VMEM((bt, C_HEADS, nk, QCAT), BF16),
                        pltpu.VMEM((bt, C_HEADS, nk, 2 * C_DV), BF16)],
        compiler_params=_cparams(("arbitrary", "arbitrary")),
        name="mla_attention",
    )(*args)


def _l1_wq_layout(w_qb):
    blocks = []
    zero = jnp.zeros((w_qb.shape[0], C_ROPE), w_qb.dtype)
    for hd in range(C_HEADS):
        base = hd * (C_NOPE + C_ROPE)
        nope = w_qb[:, base:base + C_NOPE]
        rope = w_qb[:, base + C_NOPE:base + C_NOPE + C_ROPE]
        blocks += [nope, rope, zero] if hd % 2 == 0 else [nope, zero, rope]
    return jnp.concatenate(blocks, axis=1).astype(BF16)


def _l1_wkvb_layout(w_kvb):
    w = w_kvb.reshape(C_KV_RANK, C_HEADS, C_NOPE + C_DV)
    k = w[:, :, :C_NOPE].reshape(C_KV_RANK, C_HEADS * C_NOPE)
    v = w[:, :, C_NOPE:].reshape(C_KV_RANK, C_HEADS * C_DV)
    return jnp.concatenate([k, v], axis=1).astype(BF16)


def _rope_tables(n):
    pos = np.arange(n)
    row = (pos // GRID_W).astype(np.float64)
    col = (pos % GRID_W).astype(np.float64)
    half = C_ROPE // 2
    inv = 1.0 / (ROPE_THETA ** (np.arange(0, half, 2, dtype=np.float64) / half))
    ang_r = row[:, None] * inv[None, :]
    ang_c = col[:, None] * inv[None, :]
    cos64 = np.concatenate([np.cos(ang_r)] * 2 + [np.cos(ang_c)] * 2, axis=1)
    sin64 = np.concatenate([-np.sin(ang_r), np.sin(ang_r), -np.sin(ang_c), np.sin(ang_c)], axis=1)
    return (jnp.asarray(np.concatenate([cos64] * 2, axis=1), F32),
            jnp.asarray(np.concatenate([sin64] * 2, axis=1), F32))


def _gate_weight(w, first_row):
    return jnp.pad(w, ((first_row, ZG_W - first_row - A_GATE_RANK), (0, 0)))


def _state_t(s):
    b = s.shape[0]
    return s.transpose(0, 3, 1, 2).reshape(b, 128, HK)


def _n_rows(n_fwd, n_bwd):
    b = n_fwd.shape[0]
    rows = jnp.zeros((b, 8, HK), F32)
    return rows.at[:, 0, :].set(n_fwd.reshape(b, HK)).at[:, 1, :].set(n_bwd.reshape(b, HK))


def _m_rows(m_fwd, m_bwd):
    m = jnp.concatenate([m_fwd, m_bwd], axis=1)
    return jnp.broadcast_to(m[:, :, None], (m.shape[0], 8, 128))


def kernel(x_prompt, x_sample, state_l0_gla_fwd, state_l0_gla_bwd, state_l0_mlstm_c_fwd, state_l0_mlstm_n_fwd, state_l0_mlstm_m_fwd, state_l0_mlstm_c_bwd, state_l0_mlstm_n_bwd, state_l0_mlstm_m_bwd, cache_l1_ckv, cache_l1_krope, c, c_ctx, l0_w_mod, l0_b_mod, l0_g_pre_mix, l0_g_post_mix, l0_g_pre_ffn, l0_g_post_ffn, l0_w_in, l0_gla_w_gate_f, l0_gla_b_gate_f, l0_gla_w_gate_b, l0_gla_b_gate_b, l0_gla_g_norm, l0_mlstm_b_gates, l0_mlstm_g_norm, l0_w_out, l0_ffn_w_up, l0_ffn_conv_w, l0_ffn_conv_b, l0_ffn_w_down, l1_w_mod, l1_b_mod, l1_g_pre_mix, l1_g_post_mix, l1_g_pre_ffn, l1_g_post_ffn, l1_w_in, l1_g_q_norm, l1_w_qb, l1_g_kv_norm, l1_w_kvb, l1_w_out, l1_ffn_w_up, l1_ffn_conv_w, l1_ffn_conv_b, l1_ffn_w_down):
    bp, sp, d = x_prompt.shape
    bs, ss, _ = x_sample.shape
    row = lambda v: v.reshape(1, -1)

    cond8 = jnp.concatenate([c_ctx[None, :], c, jnp.zeros((8 - 1 - bs, d), F32)], axis=0)
    mods = (_modulation(cond8, l0_w_mod, l0_b_mod), _modulation(cond8, l1_w_mod, l1_b_mod))

    paths = {
        "ctx": dict(x=x_prompt.reshape(bp * sp, d), b=bp, n=sp, first_row=0, groups=1, tm=512),
        "lat": dict(x=x_sample.reshape(bs * ss, d), b=bs, n=ss, first_row=1, groups=bs, tm=512),
    }

    w_in0 = l0_w_in.T
    wgf = _gate_weight(l0_gla_w_gate_f, 0)
    wgb = _gate_weight(l0_gla_w_gate_b, A_GATE_RANK)
    bg = jnp.pad(l0_mlstm_b_gates, (GATE_COL, ZG_W - GATE_COL - 16)).reshape(1, ZG_W)
    w_out0 = l0_w_out.astype(BF16)
    w_in1 = jnp.pad(l1_w_in, ((0, 0), (0, L1_IN_PAD - l1_w_in.shape[1]))).astype(BF16)
    w_q1 = _l1_wq_layout(l1_w_qb)
    w_kvb1 = _l1_wkvb_layout(l1_w_kvb)
    w_out1 = l1_w_out.astype(BF16)
    ffn = ((l0_ffn_w_up.astype(BF16), l0_ffn_conv_w, row(l0_ffn_conv_b), l0_ffn_w_down.astype(BF16)),
           (l1_ffn_w_up.astype(BF16), l1_ffn_conv_w, row(l1_ffn_conv_b), l1_ffn_w_down.astype(BF16)))
    norms = ((row(l0_g_pre_mix), row(l0_g_post_mix), row(l0_g_pre_ffn), row(l0_g_post_ffn)),
             (row(l1_g_pre_mix), row(l1_g_post_mix), row(l1_g_pre_ffn), row(l1_g_post_ffn)))

    gla_init = (_state_t(state_l0_gla_fwd), _state_t(state_l0_gla_bwd))
    mlstm_init = (_state_t(state_l0_mlstm_c_fwd), _state_t(state_l0_mlstm_c_bwd),
                  _n_rows(state_l0_mlstm_n_fwd, state_l0_mlstm_n_bwd),
                  _m_rows(state_l0_mlstm_m_fwd, state_l0_mlstm_m_bwd))
    tables = _rope_tables(ss)

    results = {}
    for name, p in paths.items():
        x, b, n, tm = p["x"], p["b"], p["n"], p["tm"]
        first_row, groups = p["first_row"], p["groups"]
        is_ctx = name == "ctx"
        ffn_tm = tm if is_ctx else n
        g_pre, g_post, g_pre2, g_post2 = norms[0]

        za, zb, zg = _proj_l0(x, g_pre, mods[0], first_row, groups, w_in0, tm)
        za, zb, zg = (t.reshape(b, n, -1) for t in (za, zb, zg))
        oa, sa_f, sa_b, om, c_f, c_b, n_rows, m_rows = _recurrent(
            za, zb, zg, wgf, wgb, row(l0_gla_b_gate_f), row(l0_gla_b_gate_b), row(l0_gla_g_norm),
            bg, row(l0_mlstm_g_norm), None if is_ctx else gla_init + mlstm_init)
        x = _mix_ffn([oa.reshape(b * n, HV), om.reshape(b * n, HV)], w_out0, x, mods[0], first_row,
                     groups, g_post, g_pre2, *ffn[0], g_post2, ffn_tm, n)

        g_pre, g_post, g_pre2, g_post2 = norms[1]
        q, ckv, kr = _proj_l1(x, g_pre, mods[1], first_row, groups, w_in1, row(l1_g_q_norm), w_q1,
                              row(l1_g_kv_norm), None if is_ctx else tables, tm)
        ckv3 = ckv.reshape(b, n, C_KV_RANK)
        kr3 = kr.reshape(b, n, C_ROPE)
        kv_parts = [(ckv3, kr3)] if is_ctx else [(cache_l1_ckv, cache_l1_krope), (ckv3, kr3)]
        o = _attention(q.reshape(b, n, -1), kv_parts, w_kvb1, 256, 4 if is_ctx else 1)
        x = _mix_ffn([o.reshape(b * n, C_HEADS * C_DV)], w_out1, x, mods[1], first_row,
                     groups, g_post, g_pre2, *ffn[1], g_post2, ffn_tm, n)

        results[name] = dict(y=x.reshape(b, n, d), gla=(sa_f, sa_b), c=(c_f, c_b),
                             n=n_rows, m=m_rows, ckv=ckv3, kr=kr3)

    r = results["ctx"]
    n_fwd = r["n"][:, 0, :].reshape(bp, B_HEADS, B_DK)
    n_bwd = r["n"][:, 1, :].reshape(bp, B_HEADS, B_DK)
    m_fwd = r["m"][:, 0:B_HEADS, 0]
    m_bwd = r["m"][:, B_HEADS:2 * B_HEADS, 0]
    return (r["y"], results["lat"]["y"], r["gla"][0], r["gla"][1],
            r["c"][0], n_fwd, m_fwd, r["c"][1], n_bwd, m_bwd, r["ckv"], r["kr"])
```

```python
import functools

import numpy as np
import jax
import jax.numpy as jnp
from jax import lax
from jax.experimental import pallas as pl
from jax.experimental.pallas import tpu as pltpu

F32 = jnp.float32
BF16 = jnp.bfloat16

D_MODEL = 1024
EPS = 1e-6
LOG2E = 1.4426950408889634
N_MOD = 6
GRID_W = 64
ROPE_THETA = 10000.0

A_HEADS, A_DK, A_DV, A_GATE_RANK, A_GATE_TEMP = 4, 64, 128, 16, 16.0
B_HEADS, B_DK, B_DV = 4, 64, 128
C_HEADS, C_Q_RANK, C_KV_RANK, C_NOPE, C_ROPE, C_DV = 8, 384, 256, 128, 64, 128
D_FF = 2816
CONV_W = 3

HEADS = 4
HK = HEADS * 64
HV = HEADS * 128
ZA_W = 1536
ZG_W = 128
GATE_COL = 32
RCHUNK = 128
L1_IN_PAD = 768
QCAT = 256

VMEM_LIMIT = 56 * 1024 * 1024


def _cparams(sem, vmem=VMEM_LIMIT):
    return pltpu.CompilerParams(dimension_semantics=sem, vmem_limit_bytes=vmem)


def _dot(a, b):
    return jnp.dot(a, b, preferred_element_type=F32)


def _dot_nt(a, b):
    return lax.dot_general(a, b, (((1,), (1,)), ((), ())), preferred_element_type=F32)


def _dot_tn(a, b):
    return lax.dot_general(a, b, (((0,), (0,)), ((), ())), preferred_element_type=F32)


def _split3(x):
    hi = x.astype(BF16)
    r1 = x - hi.astype(F32)
    mid = r1.astype(BF16)
    lo = (r1 - mid.astype(F32)).astype(BF16)
    return hi, mid, lo


def _tri_dot(tri, x):
    hi, mid, lo = _split3(x)
    return _dot(tri, hi) + _dot(tri, mid) + _dot(tri, lo)


def _dot_tri(x, tri):
    hi, mid, lo = _split3(x)
    return _dot(hi, tri) + _dot(mid, tri) + _dot(lo, tri)


def _dot_hl(a, b):
    ah = a.astype(BF16)
    al = (a - ah.astype(F32)).astype(BF16)
    bh = b.astype(BF16)
    bl = (b - bh.astype(F32)).astype(BF16)
    return _dot(ah, bh) + _dot(ah, bl) + _dot(al, bh)


def _rms(x, g):
    return x * lax.rsqrt(jnp.mean(x * x, axis=-1, keepdims=True) + EPS) * g


def _sigmoid(x):
    return 1.0 / (1.0 + jnp.exp(-x))


def _silu(x):
    return x * _sigmoid(x)


def _log_sigmoid(x):
    return jnp.minimum(x, 0.0) - jnp.log(1.0 + jnp.exp(-jnp.abs(x)))


def _tri_masks(n):
    r = lax.broadcasted_iota(jnp.int32, (n, n), 0)
    c = lax.broadcasted_iota(jnp.int32, (n, n), 1)
    return c <= r, c >= r


def _mod_kernel(cond_ref, w_ref, b_ref, o_ref):
    s = _silu(cond_ref[...])
    o_ref[...] = _dot(s.astype(BF16), w_ref[...].astype(BF16)) + b_ref[...]


def _modulation(cond8, w_mod, b_mod):
    d = D_MODEL
    out = pl.pallas_call(
        _mod_kernel,
        out_shape=jax.ShapeDtypeStruct((N_MOD, 8, d), F32),
        grid=(N_MOD,),
        in_specs=[pl.BlockSpec((8, d), lambda j: (0, 0)),
                  pl.BlockSpec((d, d), lambda j: (0, j)),
                  pl.BlockSpec((1, d), lambda j: (0, j))],
        out_specs=pl.BlockSpec((None, 8, d), lambda j: (j, 0, 0)),
        compiler_params=_cparams(("arbitrary",)),
        name="modulation",
    )(cond8, w_mod, b_mod.reshape(1, N_MOD * d))
    return out.reshape(N_MOD, 8, 1, d)


MOD_SHIFT1, MOD_SCALE1, MOD_GATE1, MOD_SHIFT2, MOD_SCALE2, MOD_GATE2 = range(N_MOD)


def _mod_spec(which, first_row, per):
    return pl.BlockSpec((None, None, 1, D_MODEL), lambda i: (which, first_row + i // per, 0, 0))


L0_GROUP_ROWS = ((0, ZA_W), (ZA_W + 32, 2 * ZA_W + 32))
L0_LR_ROWS = (ZA_W, ZA_W + 32)
L0_GATE_ROWS = (2 * ZA_W + 32, 2 * ZA_W + 48)


def _p0_kernel(x_ref, g_ref, sc_ref, sh_ref, wt_ref, za_ref, zb_ref, zg_ref, w_ref):
    @pl.when(pl.program_id(0) == 0)
    def _():
        blk = 256
        for gi, (r0, r1) in enumerate(L0_GROUP_ROWS):
            for j in range((r1 - r0) // blk):
                rows = wt_ref[r0 + j * blk:r0 + (j + 1) * blk, :]
                w_ref[:, gi * ZA_W + j * blk:gi * ZA_W + (j + 1) * blk] = rows.T.astype(BF16)
        tail = jnp.concatenate([wt_ref[L0_LR_ROWS[0]:L0_LR_ROWS[1], :],
                                wt_ref[L0_GATE_ROWS[0]:L0_GATE_ROWS[1], :],
                                jnp.zeros((ZG_W - 48, D_MODEL), F32)], axis=0)
        w_ref[:, 2 * ZA_W:2 * ZA_W + ZG_W] = tail.T.astype(BF16)

    h = _rms(x_ref[...], g_ref[...]) * (1.0 + sc_ref[...]) + sh_ref[...]
    hb = h.astype(BF16)
    za_ref[...] = _dot(hb, w_ref[:, 0:ZA_W])
    zb_ref[...] = _dot(hb, w_ref[:, ZA_W:2 * ZA_W])
    zg_ref[...] = _dot(hb, w_ref[:, 2 * ZA_W:2 * ZA_W + ZG_W])


def _proj_l0(x, g, mod, first_row, groups, w_t, tm):
    t = x.shape[0]
    per = (t // tm) // groups
    row = lambda i: (i, 0)
    fixed = lambda i: (0, 0)
    return pl.pallas_call(
        _p0_kernel,
        out_shape=(jax.ShapeDtypeStruct((t, ZA_W), F32),
                   jax.ShapeDtypeStruct((t, ZA_W), F32),
                   jax.ShapeDtypeStruct((t, ZG_W), F32)),
        grid=(t // tm,),
        in_specs=[pl.BlockSpec((tm, D_MODEL), row),
                  pl.BlockSpec((1, D_MODEL), fixed),
                  _mod_spec(MOD_SCALE1, first_row, per),
                  _mod_spec(MOD_SHIFT1, first_row, per),
                  _resident(w_t.shape)],
        out_specs=(pl.BlockSpec((tm, ZA_W), row),
                   pl.BlockSpec((tm, ZA_W), row),
                   pl.BlockSpec((tm, ZG_W), row)),
        scratch_shapes=[pltpu.VMEM((D_MODEL, 2 * ZA_W + ZG_W), BF16)],
        compiler_params=_cparams(("arbitrary",)),
        name="proj_l0",
    )(x, g, mod, mod, w_t)


def _head_lane_mask(shape, h):
    lane = lax.broadcasted_iota(jnp.int32, shape, len(shape) - 1)
    return (lane >= h * 64) & (lane < (h + 1) * 64)


def _head_row(h):
    return jnp.where(_head_lane_mask((1, HK), h), 1.0, 0.0).astype(BF16)


def _gla_prepare(za_ref, la_ref, r0, rev, tri):
    L = RCHUNK
    rows = pl.ds(r0, L)
    b = _tri_dot(tri, la_ref[rows, :])
    mid = L // 2
    ref = b[mid:mid + 1, :]
    b_end = b[0:1, :] if rev else b[L - 1:L, :]
    q = za_ref[rows, 0:HK]
    k = za_ref[rows, HK:2 * HK]
    v = za_ref[rows, 2 * HK:2 * HK + HV].astype(BF16)
    qe = (q * (jnp.exp(b - ref) * (A_DK ** -0.5))).astype(BF16)
    ke = (k * jnp.exp(ref - b)).astype(BF16)
    qm = jnp.concatenate([qe * _head_row(h) for h in range(HEADS)], axis=0)
    rt = _dot_tn(v, ke)
    upd = None
    for h in range(HEADS):
        blk = rt[h * 128:(h + 1) * 128, :]
        blk = jnp.where(_head_lane_mask(blk.shape, h), blk, 0.0)
        upd = blk if upd is None else upd + blk
    return dict(rows=rows, qm=qm, ke=ke, v=v, upd=upd * jnp.exp(b_end - ref),
                decay=jnp.exp(b_end), eref=jnp.exp(ref))


def _gla_apply(items):
    L = RCHUNK
    sts = [st_ref[...] for _, st_ref, _, _ in items]
    o1s = []
    for (p, _, _, _), st in zip(items, sts):
        rhs = jnp.concatenate([p["ke"], (st * p["eref"]).astype(BF16)], axis=0)
        o1s.append(_dot_nt(p["qm"], rhs))
    for h in range(HEADS):
        for (p, _, o_ref, mask), o1 in zip(items, o1s):
            blk = o1[h * L:(h + 1) * L, :]
            sc = jnp.where(mask, blk[:, 0:L], 0.0).astype(BF16)
            o_ref[p["rows"], h * 128:(h + 1) * 128] = (
                blk[:, L:L + 128] + _dot(sc, p["v"][:, h * 128:(h + 1) * 128]))
    for (p, st_ref, _, _), st in zip(items, sts):
        st_ref[...] = st * p["decay"] + p["upd"]


T_M, T_MOUT, T_G, T_SRCW, T_MPREV, T_CARRY, T_USED = 0, 8, 16, 24, 32, 40, 48


def _lane_scan(x, op, fill, is_fwd, pos, n):
    k = 1
    while k < RCHUNK:
        pf = jnp.where(pos >= k, pltpu.roll(x, k, axis=1), fill)
        sf = jnp.where(pos < RCHUNK - k, pltpu.roll(x, n - k, axis=1), fill)
        x = op(x, jnp.where(is_fwd, pf, sf))
        k *= 2
    return x


def _mlstm_gate_tables(g, m0, gt_ref, tab_ref, col_ref, n):
    L = RCHUNK
    nc = n // L
    for c in range(nc):
        gt_ref[:, c * L:(c + 1) * L] = g[c * L:(c + 1) * L, :].T
        tab_ref[c, T_USED:, :] = jnp.zeros((ZG_W - T_USED, L), F32)
    gi = gt_ref[GATE_COL:GATE_COL + 8, :]
    lf = gt_ref[GATE_COL + 8:GATE_COL + 16, :]
    is_fwd = lax.broadcasted_iota(jnp.int32, (8, n), 0) < HEADS
    pos = lax.broadcasted_iota(jnp.int32, (8, n), 1) % L
    b = _lane_scan(lf, jnp.add, 0.0, is_fwd, pos, n)
    gg = gi - b
    cmax = _lane_scan(gg, jnp.maximum, -jnp.inf, is_fwd, pos, n)
    fwd_l = lax.broadcasted_iota(jnp.int32, (8, L), 0) < HEADS
    mp = m0
    for j in range(nc):
        cf, cb = j, nc - 1 - j

        def pick(x):
            return jnp.where(fwd_l, x[:, cf * L:(cf + 1) * L], x[:, cb * L:(cb + 1) * L])

        gj, bj, cj, lj = pick(gg), pick(b), pick(cmax), pick(lf)
        mx = jnp.maximum(mp, jnp.max(gj, axis=1, keepdims=True))
        mj = jnp.maximum(cj, mp)
        blocks = (mj, bj + mj, gj, jnp.exp(gj - mx),
                  jnp.broadcast_to(mp, (8, L)), jnp.broadcast_to(jnp.exp(mp - mx), (8, L)))
        for t, val in enumerate(blocks):
            tab_ref[cf, 8 * t:8 * t + HEADS, :] = val[0:HEADS]
            tab_ref[cb, 8 * t + HEADS:8 * t + 8, :] = val[HEADS:8]
        mp = jnp.sum(lj, axis=1, keepdims=True) + mx
    for c in range(nc):
        col_ref[c * L:(c + 1) * L, :] = tab_ref[c].T
    return mp


def _mlstm_prepare(zb_ref, tab_ref, col_ref, c, d, mask):
    L = RCHUNK
    rows = pl.ds(pl.multiple_of(c * L, L), L)
    qb = zb_ref[rows, 0:HK].astype(BF16)
    kb = (zb_ref[rows, HK:2 * HK] * (B_DK ** -0.5)).astype(BF16)
    v = zb_ref[rows, 2 * HK:2 * HK + HV]
    vb = v.astype(BF16)
    vt = v.T
    qm = jnp.concatenate([qb * _head_row(h) for h in range(HEADS)], axis=0)
    s = _dot_nt(qm, kb)
    ones = jnp.ones((L, 128), BF16)
    pv, w_inter, e_negm = [], [], []
    upd = None
    carry = None
    for h in range(HEADS):
        r = 4 * d + h
        m_rep = jnp.broadcast_to(col_ref[rows, T_M + r:T_M + r + 1], (L, 128))
        mout_rep = jnp.broadcast_to(col_ref[rows, T_MOUT + r:T_MOUT + r + 1], (L, 128))
        g_row = tab_ref[c, T_G + r:T_G + r + 1, :]
        w = jnp.exp(jnp.where(mask, g_row - m_rep, -jnp.inf))
        qk = (s[h * L:(h + 1) * L, :] * w).astype(BF16)
        v1 = jnp.concatenate([vb[:, h * 128:(h + 1) * 128], ones], axis=1)
        pv.append(_dot(qk, v1))
        w_inter.append(jnp.exp(tab_ref[c, T_MPREV + r:T_MPREV + r + 1, :] - m_rep))
        e_negm.append(jnp.exp(-mout_rep))
        sw = tab_ref[c, T_SRCW + r:T_SRCW + r + 1, :]
        lhs = jnp.concatenate([vt[h * 128:(h + 1) * 128, :] * sw, jnp.broadcast_to(sw, (128, L))],
                              axis=0).astype(BF16)
        blk = _dot(lhs, kb)
        blk = jnp.where(_head_lane_mask(blk.shape, h), blk, 0.0)
        upd = blk if upd is None else upd + blk
        cr = tab_ref[c, T_CARRY + r:T_CARRY + r + 1, :]
        cr = jnp.where(_head_lane_mask((1, HK), h), jnp.concatenate([cr, cr], axis=1), 0.0)
        carry = cr if carry is None else carry + cr
    return dict(rows=rows, qm=qm, pv=pv, w_inter=w_inter, e_negm=e_negm, upd=upd, carry=carry)


def _mlstm_apply(items):
    L = RCHUNK
    sts = [st_ref[...] for _, st_ref, _ in items]
    a_all = [_dot_nt(p["qm"], st.astype(BF16)) for (p, _, _), st in zip(items, sts)]
    for h in range(HEADS):
        for (p, _, h_ref), a in zip(items, a_all):
            blk = a[h * L:(h + 1) * L, :]
            pv, w, e = p["pv"][h], p["w_inter"][h], p["e_negm"][h]
            num = w * blk[:, 0:128] + pv[:, 0:128]
            den = w * blk[:, 128:256] + pv[:, 128:256]
            h_ref[p["rows"], h * 128:(h + 1) * 128] = num / jnp.maximum(jnp.abs(den), e)
    for (p, st_ref, _), st in zip(items, sts):
        st_ref[...] = p["carry"] * st + p["upd"]


def _rec_kernel(*refs, n, has_init):
    it = iter(refs)

    def take(k):
        return [next(it) for _ in range(k)]

    za_all, zb_all, zg_all, zgn_all = take(4)
    wgf_ref, wgb_ref, bgf_ref, bgb_ref, gna_ref, bg_ref, gnb_ref = take(7)
    if has_init:
        init_all = take(7)
    out_all = take(8)
    (gt_all,) = take(1)
    set_a_all = take(5)
    set_b_all = take(5)
    work_all = take(8)
    L = RCHUNK
    nc = n // L
    step = pl.program_id(0)
    elems = []
    for e in range(za_all.shape[0]):
        el = dict(za=za_all.at[e], zb=zb_all.at[e], zg=zg_all.at[e], zgn=zgn_all.at[e],
                  gt=gt_all.at[e], set_a=[r.at[e] for r in set_a_all],
                  set_b=[r.at[e] for r in set_b_all])
        (el["oa"], el["sf"], el["sb"], el["om"], el["cf"], el["cb"], el["n_out"],
         el["m_out"]) = [r.at[e] for r in out_all]
        (el["of"], el["ob"], el["hf"], el["hb"], el["gsf"], el["gsb"], el["msf"],
         el["msb"]) = [r.at[e] for r in work_all]
        if has_init:
            el["init"] = [r.at[e] for r in init_all]
        elems.append(el)

    def fill(gate_ref, m0, bufs, gt_ref):
        laf_ref, lab_ref, tab_ref, col_ref, mf_ref = bufs
        zg = gate_ref[...]
        laf_ref[...] = _log_sigmoid(_dot_hl(zg, wgf_ref[...]) + bgf_ref[...]) / A_GATE_TEMP
        lab_ref[...] = _log_sigmoid(_dot_hl(zg, wgb_ref[...]) + bgb_ref[...]) / A_GATE_TEMP
        g = zg + bg_ref[...]
        lane = lax.broadcasted_iota(jnp.int32, g.shape, 1)
        is_forget = (lane >= GATE_COL + 8) & (lane < GATE_COL + 16)
        g = jnp.where(is_forget, _log_sigmoid(g), g)
        m_fin = _mlstm_gate_tables(g, m0, gt_ref, tab_ref, col_ref, n)
        mf_ref[...] = jnp.broadcast_to(m_fin, (8, 128))

    for el in elems:
        if has_init:
            sf0_ref, sb0_ref, cf0_ref, cb0_ref, n0_ref, m0_ref, m0n_ref = el["init"]
            el["m0"], el["m0_next"] = m0_ref[:, 0:1], m0n_ref[:, 0:1]
            el["gsf"][...] = sf0_ref[...]
            el["gsb"][...] = sb0_ref[...]
            el["msf"][...] = jnp.concatenate(
                [cf0_ref[...], jnp.broadcast_to(n0_ref[0:1, :], (128, HK))], axis=0)
            el["msb"][...] = jnp.concatenate(
                [cb0_ref[...], jnp.broadcast_to(n0_ref[1:2, :], (128, HK))], axis=0)
        else:
            el["m0"] = el["m0_next"] = jnp.zeros((8, 1), F32)
            for name in ("gsf", "gsb", "msf", "msb"):
                el[name][...] = jnp.zeros(el[name].shape, F32)

    @pl.when(step == 0)
    def _():
        for el in elems:
            fill(el["zg"], el["m0"], el["set_a"], el["gt"])

    m_lo, m_up = _tri_masks(L)
    tri_lo = jnp.where(m_lo, 1.0, 0.0).astype(BF16)
    tri_up = jnp.where(m_up, 1.0, 0.0).astype(BF16)
    gna = gna_ref[...]
    gnb = gnb_ref[...]

    def out_body(c, carry):
        rows = pl.ds(pl.multiple_of(c * L, L), L)
        for h in range(HEADS):
            cols = slice(h * 128, (h + 1) * 128)
            gcols = slice(2 * HK + HV + h * 128, 2 * HK + HV + (h + 1) * 128)
            for el in elems:
                o = el["of"][rows, cols] + el["ob"][rows, cols]
                el["oa"][rows, cols] = (
                    _rms(o, gna) * _silu(el["za"][rows, gcols])).astype(BF16)
                hh = el["hf"][rows, cols] + el["hb"][rows, cols]
                el["om"][rows, cols] = (
                    _sigmoid(el["zb"][rows, gcols]) * _rms(hh, gnb)).astype(BF16)
        return carry

    def run(cur, nxt):
        for el in elems:
            fill(el["zgn"], el["m0_next"], el[nxt], el["gt"])

        def scan_body(gidx, carry):
            waves = []
            for w in range(2):
                c = 2 * gidx + w
                cb = nc - 1 - c
                gla_items, mlstm_items = [], []
                for el in elems:
                    laf_ref, lab_ref, tab_ref, col_ref, _ = el[cur]
                    gla_items += [
                        (_gla_prepare(el["za"], laf_ref, pl.multiple_of(c * L, L), False, tri_lo),
                         el["gsf"], el["of"], m_lo),
                        (_gla_prepare(el["za"], lab_ref, pl.multiple_of(cb * L, L), True, tri_up),
                         el["gsb"], el["ob"], m_up)]
                    mlstm_items += [
                        (_mlstm_prepare(el["zb"], tab_ref, col_ref, c, 0, m_lo), el["msf"], el["hf"]),
                        (_mlstm_prepare(el["zb"], tab_ref, col_ref, cb, 1, m_up), el["msb"], el["hb"])]
                waves.append((gla_items, mlstm_items))
            for gla_items, mlstm_items in waves:
                _gla_apply(gla_items)
                _mlstm_apply(mlstm_items)
            return carry

        if nc == 2:
            scan_body(0, 0)
            for c in range(nc):
                out_body(c, 0)
        else:
            lax.fori_loop(0, nc // 2, scan_body, 0)
            lax.fori_loop(0, nc, out_body, 0)
        for el in elems:
            el["m_out"][...] = el[cur][4][...]

    @pl.when(step % 2 == 0)
    def _():
        run("set_a", "set_b")

    @pl.when(step % 2 == 1)
    def _():
        run("set_b", "set_a")

    for el in elems:
        gsf_t = el["gsf"][...].T
        gsb_t = el["gsb"][...].T
        msf = el["msf"][...]
        msb = el["msb"][...]
        ctf_t = msf[0:128, :].T
        ctb_t = msb[0:128, :].T
        for h in range(HEADS):
            rows = slice(h * 64, (h + 1) * 64)
            el["sf"][h] = gsf_t[rows, :]
            el["sb"][h] = gsb_t[rows, :]
            el["cf"][h] = ctf_t[rows, :]
            el["cb"][h] = ctb_t[rows, :]
        el["n_out"][...] = jnp.concatenate(
            [msf[128:129, :], msb[128:129, :], jnp.zeros((6, HK), F32)], axis=0)


def _recurrent(za, zb, zg, wgf, wgb, bgf, bgb, gna, bg, gnb, init, bt):
    bsz, n, _ = za.shape
    has_init = init is not None
    nc = n // RCHUNK
    steps = bsz // bt
    per_b3 = lambda b: (b, 0, 0)
    next_b3 = lambda b: (jnp.minimum(b + 1, steps - 1), 0, 0)
    per_b4 = lambda b: (b, 0, 0, 0)
    fixed = lambda b: (0, 0)
    in_specs = [pl.BlockSpec((bt, n, ZA_W), per_b3),
                pl.BlockSpec((bt, n, ZA_W), per_b3),
                pl.BlockSpec((bt, n, ZG_W), per_b3),
                pl.BlockSpec((bt, n, ZG_W), next_b3),
                pl.BlockSpec((ZG_W, HK), fixed),
                pl.BlockSpec((ZG_W, HK), fixed),
                pl.BlockSpec((1, HK), fixed),
                pl.BlockSpec((1, HK), fixed),
                pl.BlockSpec((1, 128), fixed),
                pl.BlockSpec((1, ZG_W), fixed),
                pl.BlockSpec((1, 128), fixed)]
    args = [za, zb, zg, zg, wgf, wgb, bgf, bgb, gna, bg, gnb]
    if has_init:
        sf0, sb0, cf0, cb0, n0, m0 = init
        in_specs += [pl.BlockSpec((bt, 128, HK), per_b3)] * 4
        in_specs += [pl.BlockSpec((bt, 8, HK), per_b3), pl.BlockSpec((bt, 8, 128), per_b3),
                     pl.BlockSpec((bt, 8, 128), next_b3)]
        args += [sf0, sb0, cf0, cb0, n0, m0, m0]
    state = jax.ShapeDtypeStruct((bsz, HEADS, 64, 128), F32)
    state_spec = pl.BlockSpec((bt, HEADS, 64, 128), per_b4)

    def vmem(*shape):
        return pltpu.VMEM((bt,) + shape, F32)

    gate_set = [vmem(n, HK), vmem(n, HK), vmem(nc, ZG_W, RCHUNK), vmem(n, ZG_W), vmem(8, 128)]
    return pl.pallas_call(
        functools.partial(_rec_kernel, n=n, has_init=has_init),
        out_shape=(jax.ShapeDtypeStruct((bsz, n, HV), BF16), state, state,
                   jax.ShapeDtypeStruct((bsz, n, HV), BF16), state, state,
                   jax.ShapeDtypeStruct((bsz, 8, HK), F32),
                   jax.ShapeDtypeStruct((bsz, 8, 128), F32)),
        grid=(steps,),
        in_specs=in_specs,
        out_specs=(pl.BlockSpec((bt, n, HV), per_b3), state_spec, state_spec,
                   pl.BlockSpec((bt, n, HV), per_b3), state_spec, state_spec,
                   pl.BlockSpec((bt, 8, HK), per_b3),
                   pl.BlockSpec((bt, 8, 128), per_b3)),
        scratch_shapes=([vmem(ZG_W, n)] + gate_set + gate_set + [vmem(n, HV)] * 4
                        + [vmem(128, HK)] * 2 + [vmem(2 * 128, HK)] * 2),
        compiler_params=_cparams(("arbitrary",)),
        name="recurrent",
    )(*args)


def _mix_residual(parts, wo_ref, x_ref, gt1_ref, gpost1_ref, gpre2_ref, sc2_ref, sh2_ref):
    m = None
    off = 0
    for p in parts:
        kp = p.shape[1]
        t = _dot(p[...], wo_ref[off:off + kp, :])
        m = t if m is None else m + t
        off += kp
    x1 = x_ref[...] + gt1_ref[...] * _rms(m, gpost1_ref[...])
    hb = (_rms(x1, gpre2_ref[...]) * (1.0 + sc2_ref[...]) + sh2_ref[...]).astype(BF16)
    return x1, hb


def _ffn_gated_up(hb, wu_ref, cw_ref, cb_ref, act_ref, seq, tf):
    tm = hb.shape[0]
    pos = lax.broadcasted_iota(jnp.int32, (tm, 1), 0) % seq
    first = pos == 0
    last = pos == seq - 1

    def conv(u, cols):
        prev = jnp.where(first, 0.0, pltpu.roll(u, 1, axis=0))
        nxt = jnp.where(last, 0.0, pltpu.roll(u, tm - 1, axis=0))
        return (cw_ref[0:1, cols] * prev + cw_ref[1:2, cols] * u + cw_ref[2:3, cols] * nxt
                + cb_ref[:, cols])

    for j in range(D_FF // tf):
        ca = slice(j * tf, (j + 1) * tf)
        cg = slice(D_FF + j * tf, D_FF + (j + 1) * tf)
        a = conv(_dot(hb, wu_ref[:, ca]), ca)
        g = conv(_dot(hb, wu_ref[:, cg]), cg)
        act_ref[:, ca] = (_silu(g) * a).astype(BF16)


def _mix_ffn_kernel(*refs, n_parts, seq, tf):
    parts = refs[:n_parts]
    (wo_ref, x_ref, gt1_ref, gpost1_ref, gpre2_ref, sc2_ref, sh2_ref,
     wu_ref, cw_ref, cb_ref, wd_ref, gt2_ref, gpost2_ref, o_ref, act_ref) = refs[n_parts:]
    x1, hb = _mix_residual(parts, wo_ref, x_ref, gt1_ref, gpost1_ref, gpre2_ref, sc2_ref, sh2_ref)
    o_ref[...] = x1
    _ffn_gated_up(hb, wu_ref, cw_ref, cb_ref, act_ref, seq, tf)
    y = _dot(act_ref[...], wd_ref[...])
    o_ref[...] = o_ref[...] + gt2_ref[...] * _rms(y, gpost2_ref[...])


def _resident(shape):
    return pl.BlockSpec(shape, lambda *_: (0,) * len(shape), pipeline_mode=pl.Buffered(1))


def _mix_ffn(parts, w_out, x, mod, first_row, groups, gpost1, gpre2, w_up, conv_w, conv_b, w_down,
             gpost2, tm, seq, tf=256):
    t = x.shape[0]
    per = (t // tm) // groups
    row = lambda i: (i, 0)
    vec = _resident((1, D_MODEL))
    in_specs = [pl.BlockSpec((tm, p.shape[1]), row) for p in parts]
    in_specs += [_resident(w_out.shape), pl.BlockSpec((tm, D_MODEL), row),
                 _mod_spec(MOD_GATE1, first_row, per), vec, vec,
                 _mod_spec(MOD_SCALE2, first_row, per), _mod_spec(MOD_SHIFT2, first_row, per),
                 _resident(w_up.shape), _resident(conv_w.shape), _resident(conv_b.shape),
                 _resident(w_down.shape), _mod_spec(MOD_GATE2, first_row, per), vec]
    return pl.pallas_call(
        functools.partial(_mix_ffn_kernel, n_parts=len(parts), seq=seq, tf=tf),
        out_shape=jax.ShapeDtypeStruct((t, D_MODEL), F32),
        grid=(t // tm,),
        in_specs=in_specs,
        out_specs=pl.BlockSpec((tm, D_MODEL), row),
        scratch_shapes=[pltpu.VMEM((tm, D_FF), BF16)],
        compiler_params=_cparams(("arbitrary",)),
        name="mix_ffn",
    )(*parts, w_out, x, mod, gpost1, gpre2, mod, mod, w_up, conv_w, conv_b, w_down, mod, gpost2)


def _rope(x, cos, sin_signed):
    lane = lax.broadcasted_iota(jnp.int32, x.shape, 1)
    lower = (lane % 32) < 16
    partner = jnp.where(lower, pltpu.roll(x, 128 - 16, axis=1), pltpu.roll(x, 16, axis=1))
    return x * cos + partner * sin_signed


def _p1_kernel(*refs, rope):
    if rope:
        (x_ref, g_ref, sc_ref, sh_ref, w_ref, gq_ref, wq_ref, gkv_ref, cos_ref, sin_ref,
         q_ref, ckv_ref, kr_ref) = refs
    else:
        (x_ref, g_ref, sc_ref, sh_ref, w_ref, gq_ref, wq_ref, gkv_ref,
         q_ref, ckv_ref, kr_ref) = refs
    h = _rms(x_ref[...], g_ref[...]) * (1.0 + sc_ref[...]) + sh_ref[...]
    z = _dot(h.astype(BF16), w_ref[...])
    qa = _rms(z[:, 0:C_Q_RANK], gq_ref[...])
    q = _dot(qa.astype(BF16), wq_ref[...])
    ckv_ref[...] = _rms(z[:, C_Q_RANK:C_Q_RANK + C_KV_RANK], gkv_ref[...])
    kr = z[:, C_Q_RANK + C_KV_RANK:L1_IN_PAD]
    if rope:
        cos = cos_ref[...]
        sin = sin_ref[...]
        kr = _rope(kr, cos, sin)
    kr_ref[...] = kr[:, 0:C_ROPE]
    for hd in range(C_HEADS):
        c0 = hd * QCAT
        q_ref[:, c0:c0 + 128] = q[:, c0:c0 + 128].astype(q_ref.dtype)
        qr = q[:, c0 + 128:c0 + 256]
        if rope:
            qr = _rope(qr, cos, sin)
        q_ref[:, c0 + 128:c0 + 256] = qr.astype(q_ref.dtype)


def _proj_l1(x, g, mod, first_row, groups, w, gq, wq, gkv, tables, tm):
    t = x.shape[0]
    per = (t // tm) // groups
    row = lambda i: (i, 0)
    fixed = lambda i: (0, 0)
    in_specs = [pl.BlockSpec((tm, D_MODEL), row),
                pl.BlockSpec((1, D_MODEL), fixed),
                _mod_spec(MOD_SCALE1, first_row, per),
                _mod_spec(MOD_SHIFT1, first_row, per),
                pl.BlockSpec(w.shape, fixed),
                pl.BlockSpec((1, C_Q_RANK), fixed),
                pl.BlockSpec(wq.shape, fixed),
                pl.BlockSpec((1, C_KV_RANK), fixed)]
    args = [x, g, mod, mod, w, gq, wq, gkv]
    rope = tables is not None
    if rope:
        n = tables[0].shape[0]
        pos = lambda i: (i % (n // tm), 0)
        in_specs += [pl.BlockSpec((tm, 128), pos)] * 2
        args += list(tables)
    return pl.pallas_call(
        functools.partial(_p1_kernel, rope=rope),
        out_shape=(jax.ShapeDtypeStruct((t, C_HEADS * QCAT), BF16),
                   jax.ShapeDtypeStruct((t, C_KV_RANK), F32),
                   jax.ShapeDtypeStruct((t, C_ROPE), F32)),
        grid=(t // tm,),
        in_specs=in_specs,
        out_specs=(pl.BlockSpec((tm, C_HEADS * QCAT), row),
                   pl.BlockSpec((tm, C_KV_RANK), row),
                   pl.BlockSpec((tm, C_ROPE), row)),
        compiler_params=_cparams(("arbitrary",)),
        name="proj_l1",
    )(*args)


def _attn_kernel(*refs, n_parts, scale):
    q_ref = refs[0]
    kv_parts = refs[1:1 + 2 * n_parts]
    w_ref, o_ref, k_sc, v_sc = refs[1 + 2 * n_parts:]
    bt = q_ref.shape[0]

    @pl.when(pl.program_id(1) == 0)
    def _():
        off = 0
        for p in range(n_parts):
            ckv_ref, kr_ref = kv_parts[2 * p], kv_parts[2 * p + 1]
            nk = ckv_ref.shape[1]
            ones = jnp.ones((nk, C_DV), BF16)
            for e in range(bt):
                kv = _dot(ckv_ref[e].astype(BF16), w_ref[...])
                kr = kr_ref[e].astype(BF16)
                kr2 = jnp.concatenate([kr, kr], axis=1)
                for hd in range(C_HEADS):
                    v0 = C_HEADS * C_NOPE + hd * C_DV
                    k_sc[e, hd, off:off + nk, 0:128] = kv[:, hd * 128:(hd + 1) * 128].astype(BF16)
                    k_sc[e, hd, off:off + nk, 128:256] = kr2
                    v_sc[e, hd, off:off + nk, 0:C_DV] = kv[:, v0:v0 + C_DV].astype(BF16)
                    v_sc[e, hd, off:off + nk, C_DV:2 * C_DV] = ones
            off += nk

    for hd in range(C_HEADS):
        for e in range(bt):
            t = _dot_nt(q_ref[e, :, hd * QCAT:(hd + 1) * QCAT], k_sc[e, hd]) * (scale * LOG2E)
            p = jnp.exp2(t - jnp.max(t, axis=1, keepdims=True))
            ov = _dot(p.astype(BF16), v_sc[e, hd])
            o_ref[e, :, hd * C_DV:(hd + 1) * C_DV] = (
                ov[:, 0:C_DV] / ov[:, C_DV:2 * C_DV]).astype(o_ref.dtype)


def _attention(q, kv_parts, w_kvb, tq, bt):
    bsz, n, _ = q.shape
    nk = sum(p[0].shape[1] for p in kv_parts)
    in_specs = [pl.BlockSpec((bt, tq, C_HEADS * QCAT), lambda b, i: (b, i, 0))]
    args = [q]
    for ckv, kr in kv_parts:
        in_specs.append(pl.BlockSpec((bt, ckv.shape[1], C_KV_RANK), lambda b, i: (b, 0, 0)))
        in_specs.append(pl.BlockSpec((bt, kr.shape[1], C_ROPE), lambda b, i: (b, 0, 0)))
        args += [ckv, kr]
    in_specs.append(pl.BlockSpec(w_kvb.shape, lambda b, i: (0, 0)))
    args.append(w_kvb)
    return pl.pallas_call(
        functools.partial(_attn_kernel, n_parts=len(kv_parts),
                          scale=(C_NOPE + C_ROPE) ** -0.5),
        out_shape=jax.ShapeDtypeStruct((bsz, n, C_HEADS * C_DV), BF16),
        grid=(bsz // bt, n // tq),
        in_specs=in_specs,
        out_specs=pl.BlockSpec((bt, tq, C_HEADS * C_DV), lambda b, i: (b, i, 0)),
        scratch_shapes=[pltpu.VMEM((bt, C_HEADS, nk, QCAT), BF16),
                        pltpu.VMEM((bt, C_HEADS, nk, 2 * C_DV), BF16)],
        compiler_params=_cparams(("arbitrary", "arbitrary")),
        name="mla_attention",
    )(*args)


def _l1_wq_layout(w_qb):
    blocks = []
    zero = jnp.zeros((w_qb.shape[0], C_ROPE), w_qb.dtype)
    for hd in range(C_HEADS):
        base = hd * (C_NOPE + C_ROPE)
        nope = w_qb[:, base:base + C_NOPE]
        rope = w_qb[:, base + C_NOPE:base + C_NOPE + C_ROPE]
        blocks += [nope, rope, zero] if hd % 2 == 0 else [nope, zero, rope]
    return jnp.concatenate(blocks, axis=1).astype(BF16)


def _l1_wkvb_layout(w_kvb):
    w = w_kvb.reshape(C_KV_RANK, C_HEADS, C_NOPE + C_DV)
    k = w[:, :, :C_NOPE].reshape(C_KV_RANK, C_HEADS * C_NOPE)
    v = w[:, :, C_NOPE:].reshape(C_KV_RANK, C_HEADS * C_DV)
    return jnp.concatenate([k, v], axis=1).astype(BF16)


def _rope_tables(n):
    pos = np.arange(n)
    row = (pos // GRID_W).astype(np.float64)
    col = (pos % GRID_W).astype(np.float64)
    half = C_ROPE // 2
    inv = 1.0 / (ROPE_THETA ** (np.arange(0, half, 2, dtype=np.float64) / half))
    ang_r = row[:, None] * inv[None, :]
    ang_c = col[:, None] * inv[None, :]
    cos64 = np.concatenate([np.cos(ang_r)] * 2 + [np.cos(ang_c)] * 2, axis=1)
    sin64 = np.concatenate([-np.sin(ang_r), np.sin(ang_r), -np.sin(ang_c), np.sin(ang_c)], axis=1)
    return (jnp.asarray(np.concatenate([cos64] * 2, axis=1), F32),
            jnp.asarray(np.concatenate([sin64] * 2, axis=1), F32))


def _gate_weight(w, first_row):
    return jnp.pad(w, ((first_row, ZG_W - first_row - A_GATE_RANK), (0, 0)))


def _state_t(s):
    b = s.shape[0]
    return s.transpose(0, 3, 1, 2).reshape(b, 128, HK)


def _n_rows(n_fwd, n_bwd):
    b = n_fwd.shape[0]
    rows = jnp.zeros((b, 8, HK), F32)
    return rows.at[:, 0, :].set(n_fwd.reshape(b, HK)).at[:, 1, :].set(n_bwd.reshape(b, HK))


def _m_rows(m_fwd, m_bwd):
    m = jnp.concatenate([m_fwd, m_bwd], axis=1)
    return jnp.broadcast_to(m[:, :, None], (m.shape[0], 8, 128))


def kernel(x_prompt, x_sample, state_l0_gla_fwd, state_l0_gla_bwd, state_l0_mlstm_c_fwd, state_l0_mlstm_n_fwd, state_l0_mlstm_m_fwd, state_l0_mlstm_c_bwd, state_l0_mlstm_n_bwd, state_l0_mlstm_m_bwd, cache_l1_ckv, cache_l1_krope, c, c_ctx, l0_w_mod, l0_b_mod, l0_g_pre_mix, l0_g_post_mix, l0_g_pre_ffn, l0_g_post_ffn, l0_w_in, l0_gla_w_gate_f, l0_gla_b_gate_f, l0_gla_w_gate_b, l0_gla_b_gate_b, l0_gla_g_norm, l0_mlstm_b_gates, l0_mlstm_g_norm, l0_w_out, l0_ffn_w_up, l0_ffn_conv_w, l0_ffn_conv_b, l0_ffn_w_down, l1_w_mod, l1_b_mod, l1_g_pre_mix, l1_g_post_mix, l1_g_pre_ffn, l1_g_post_ffn, l1_w_in, l1_g_q_norm, l1_w_qb, l1_g_kv_norm, l1_w_kvb, l1_w_out, l1_ffn_w_up, l1_ffn_conv_w, l1_ffn_conv_b, l1_ffn_w_down):
    bp, sp, d = x_prompt.shape
    bs, ss, _ = x_sample.shape
    row = lambda v: v.reshape(1, -1)

    cond8 = jnp.concatenate([c_ctx[None, :], c, jnp.zeros((8 - 1 - bs, d), F32)], axis=0)
    mods = (_modulation(cond8, l0_w_mod, l0_b_mod), _modulation(cond8, l1_w_mod, l1_b_mod))

    paths = {
        "ctx": dict(x=x_prompt.reshape(bp * sp, d), b=bp, n=sp, first_row=0, groups=1, tm=512),
        "lat": dict(x=x_sample.reshape(bs * ss, d), b=bs, n=ss, first_row=1, groups=bs, tm=512),
    }

    w_in0 = l0_w_in.T
    wgf = _gate_weight(l0_gla_w_gate_f, 0)
    wgb = _gate_weight(l0_gla_w_gate_b, A_GATE_RANK)
    bg = jnp.pad(l0_mlstm_b_gates, (GATE_COL, ZG_W - GATE_COL - 16)).reshape(1, ZG_W)
    w_out0 = l0_w_out.astype(BF16)
    w_in1 = jnp.pad(l1_w_in, ((0, 0), (0, L1_IN_PAD - l1_w_in.shape[1]))).astype(BF16)
    w_q1 = _l1_wq_layout(l1_w_qb)
    w_kvb1 = _l1_wkvb_layout(l1_w_kvb)
    w_out1 = l1_w_out.astype(BF16)
    ffn = ((l0_ffn_w_up.astype(BF16), l0_ffn_conv_w, row(l0_ffn_conv_b), l0_ffn_w_down.astype(BF16)),
           (l1_ffn_w_up.astype(BF16), l1_ffn_conv_w, row(l1_ffn_conv_b), l1_ffn_w_down.astype(BF16)))
    norms = ((row(l0_g_pre_mix), row(l0_g_post_mix), row(l0_g_pre_ffn), row(l0_g_post_ffn)),
             (row(l1_g_pre_mix), row(l1_g_post_mix), row(l1_g_pre_ffn), row(l1_g_post_ffn)))

    gla_init = (_state_t(state_l0_gla_fwd), _state_t(state_l0_gla_bwd))
    mlstm_init = (_state_t(state_l0_mlstm_c_fwd), _state_t(state_l0_mlstm_c_bwd),
                  _n_rows(state_l0_mlstm_n_fwd, state_l0_mlstm_n_bwd),
                  _m_rows(state_l0_mlstm_m_fwd, state_l0_mlstm_m_bwd))
    tables = _rope_tables(ss)

    results = {}
    for name, p in paths.items():
        x, b, n, tm = p["x"], p["b"], p["n"], p["tm"]
        first_row, groups = p["first_row"], p["groups"]
        is_ctx = name == "ctx"
        ffn_tm = tm if is_ctx else n
        g_pre, g_post, g_pre2, g_post2 = norms[0]

        za, zb, zg = _proj_l0(x, g_pre, mods[0], first_row, groups, w_in0, tm)
        za, zb, zg = (t.reshape(b, n, -1) for t in (za, zb, zg))
        oa, sa_f, sa_b, om, c_f, c_b, n_rows, m_rows = _recurrent(
            za, zb, zg, wgf, wgb, row(l0_gla_b_gate_f), row(l0_gla_b_gate_b), row(l0_gla_g_norm),
            bg, row(l0_mlstm_g_norm), None if is_ctx else gla_init + mlstm_init,
            2 if is_ctx else 1)
        x = _mix_ffn([oa.reshape(b * n, HV), om.reshape(b * n, HV)], w_out0, x, mods[0], first_row,
                     groups, g_post, g_pre2, *ffn[0], g_post2, ffn_tm, n)

        g_pre, g_post, g_pre2, g_post2 = norms[1]
        q, ckv, kr = _proj_l1(x, g_pre, mods[1], first_row, groups, w_in1, row(l1_g_q_norm), w_q1,
                              row(l1_g_kv_norm), None if is_ctx else tables, tm)
        ckv3 = ckv.reshape(b, n, C_KV_RANK)
        kr3 = kr.reshape(b, n, C_ROPE)
        kv_parts = [(ckv3, kr3)] if is_ctx else [(cache_l1_ckv, cache_l1_krope), (ckv3, kr3)]
        o = _attention(q.reshape(b, n, -1), kv_parts, w_kvb1, 256, 4 if is_ctx else 1)
        x = _mix_ffn([o.reshape(b * n, C_HEADS * C_DV)], w_out1, x, mods[1], first_row,
                     groups, g_post, g_pre2, *ffn[1], g_post2, ffn_tm, n)

        results[name] = dict(y=x.reshape(b, n, d), gla=(sa_f, sa_b), c=(c_f, c_b),
                             n=n_rows, m=m_rows, ckv=ckv3, kr=kr3)

    r = results["ctx"]
    n_fwd = r["n"][:, 0, :].reshape(bp, B_HEADS, B_DK)
    n_bwd = r["n"][:, 1, :].reshape(bp, B_HEADS, B_DK)
    m_fwd = r["m"][:, 0:B_HEADS, 0]
    m_bwd = r["m"][:, B_HEADS:2 * B_HEADS, 0]
    return (r["y"], results["lat"]["y"], r["gla"][0], r["gla"][1],
            r["c"][0], n_fwd, m_fwd, r["c"][1], n_bwd, m_bwd, r["ckv"], r["kr"])
```

```python
import functools

import numpy as np
import jax
import jax.numpy as jnp
from jax import lax
from jax.experimental import pallas as pl
from jax.experimental.pallas import tpu as pltpu

F32 = jnp.float32
BF16 = jnp.bfloat16

D_MODEL = 1024
EPS = 1e-6
LOG2E = 1.4426950408889634
N_MOD = 6
GRID_W = 64
ROPE_THETA = 10000.0

A_HEADS, A_DK, A_DV, A_GATE_RANK, A_GATE_TEMP = 4, 64, 128, 16, 16.0
B_HEADS, B_DK, B_DV = 4, 64, 128
C_HEADS, C_Q_RANK, C_KV_RANK, C_NOPE, C_ROPE, C_DV = 8, 384, 256, 128, 64, 128
D_FF = 2816
CONV_W = 3

HEADS = 4
HK = HEADS * 64
HV = HEADS * 128
ZA_W = 1536
ZG_W = 128
GATE_COL = 32
RCHUNK = 128
GLA_SAFE_SPAN = 60.0
L1_IN_PAD = 768
QCAT = 256

V7X_VMEM_BYTES = 64 * 1024 * 1024
VMEM_LIMIT = V7X_VMEM_BYTES - 8 * 1024 * 1024


def _cparams(sem):
    return pltpu.CompilerParams(dimension_semantics=sem, vmem_limit_bytes=VMEM_LIMIT)


def _dot(a, b):
    return jnp.dot(a, b, preferred_element_type=F32)


def _dot_nt(a, b):
    return lax.dot_general(a, b, (((1,), (1,)), ((), ())), preferred_element_type=F32)


def _dot_tn(a, b):
    return lax.dot_general(a, b, (((0,), (0,)), ((), ())), preferred_element_type=F32)


def _split3(x):
    hi = x.astype(BF16)
    r1 = x - hi.astype(F32)
    mid = r1.astype(BF16)
    lo = (r1 - mid.astype(F32)).astype(BF16)
    return hi, mid, lo


def _tri_dot(tri, x):
    hi, mid, lo = _split3(x)
    return _dot(tri, hi) + _dot(tri, mid) + _dot(tri, lo)


def _dot_hl(a, b):
    ah = a.astype(BF16)
    al = (a - ah.astype(F32)).astype(BF16)
    bh = b.astype(BF16)
    bl = (b - bh.astype(F32)).astype(BF16)
    return _dot(ah, bh) + _dot(ah, bl) + _dot(al, bh)


def _rms(x, g):
    return x * lax.rsqrt(jnp.mean(x * x, axis=-1, keepdims=True) + EPS) * g


def _sigmoid(x):
    return 1.0 / (1.0 + jnp.exp(-x))


def _silu(x):
    return x * _sigmoid(x)


def _log_sigmoid(x):
    return jnp.minimum(x, 0.0) - jnp.log(1.0 + jnp.exp(-jnp.abs(x)))


def _tri_masks(n):
    r = lax.broadcasted_iota(jnp.int32, (n, n), 0)
    c = lax.broadcasted_iota(jnp.int32, (n, n), 1)
    return c <= r, c >= r


def _mod_kernel(cond_ref, w_ref, b_ref, o_ref):
    s = _silu(cond_ref[...])
    o_ref[...] = _dot(s.astype(BF16), w_ref[...].astype(BF16)) + b_ref[...]


def _modulation(cond8, w_mod, b_mod):
    d = D_MODEL
    out = pl.pallas_call(
        _mod_kernel,
        out_shape=jax.ShapeDtypeStruct((N_MOD, 8, d), F32),
        grid=(N_MOD,),
        in_specs=[pl.BlockSpec((8, d), lambda j: (0, 0)),
                  pl.BlockSpec((d, d), lambda j: (0, j)),
                  pl.BlockSpec((1, d), lambda j: (0, j))],
        out_specs=pl.BlockSpec((None, 8, d), lambda j: (j, 0, 0)),
        compiler_params=_cparams(("arbitrary",)),
        name="modulation",
    )(cond8, w_mod, b_mod.reshape(1, N_MOD * d))
    return out.reshape(N_MOD, 8, 1, d)


MOD_SHIFT1, MOD_SCALE1, MOD_GATE1, MOD_SHIFT2, MOD_SCALE2, MOD_GATE2 = range(N_MOD)


def _mod_spec(which, first_row, per):
    return pl.BlockSpec((None, None, 1, D_MODEL), lambda i: (which, first_row + i // per, 0, 0))


L0_GROUP_ROWS = ((0, ZA_W), (ZA_W + 32, 2 * ZA_W + 32))
L0_LR_ROWS = (ZA_W, ZA_W + 32)
L0_GATE_ROWS = (2 * ZA_W + 32, 2 * ZA_W + 48)


def _p0_kernel(x_ref, g_ref, sc_ref, sh_ref, wt_ref, za_ref, zb_ref, zg_ref, w_ref):
    @pl.when(pl.program_id(0) == 0)
    def _():
        blk = 256
        for gi, (r0, r1) in enumerate(L0_GROUP_ROWS):
            for j in range((r1 - r0) // blk):
                rows = wt_ref[r0 + j * blk:r0 + (j + 1) * blk, :]
                w_ref[:, gi * ZA_W + j * blk:gi * ZA_W + (j + 1) * blk] = rows.T.astype(BF16)
        tail = jnp.concatenate([wt_ref[L0_LR_ROWS[0]:L0_LR_ROWS[1], :],
                                wt_ref[L0_GATE_ROWS[0]:L0_GATE_ROWS[1], :],
                                jnp.zeros((ZG_W - 48, D_MODEL), F32)], axis=0)
        w_ref[:, 2 * ZA_W:2 * ZA_W + ZG_W] = tail.T.astype(BF16)

    h = _rms(x_ref[...], g_ref[...]) * (1.0 + sc_ref[...]) + sh_ref[...]
    hb = h.astype(BF16)
    za_ref[...] = _dot(hb, w_ref[:, 0:ZA_W])
    zb_ref[...] = _dot(hb, w_ref[:, ZA_W:2 * ZA_W])
    zg_ref[...] = _dot(hb, w_ref[:, 2 * ZA_W:2 * ZA_W + ZG_W])


def _proj_l0(x, g, mod, first_row, groups, w_t, tm):
    t = x.shape[0]
    per = (t // tm) // groups
    row = lambda i: (i, 0)
    fixed = lambda i: (0, 0)
    return pl.pallas_call(
        _p0_kernel,
        out_shape=(jax.ShapeDtypeStruct((t, ZA_W), F32),
                   jax.ShapeDtypeStruct((t, ZA_W), F32),
                   jax.ShapeDtypeStruct((t, ZG_W), F32)),
        grid=(t // tm,),
        in_specs=[pl.BlockSpec((tm, D_MODEL), row),
                  pl.BlockSpec((1, D_MODEL), fixed),
                  _mod_spec(MOD_SCALE1, first_row, per),
                  _mod_spec(MOD_SHIFT1, first_row, per),
                  _resident(w_t.shape)],
        out_specs=(pl.BlockSpec((tm, ZA_W), row),
                   pl.BlockSpec((tm, ZA_W), row),
                   pl.BlockSpec((tm, ZG_W), row)),
        scratch_shapes=[pltpu.VMEM((D_MODEL, 2 * ZA_W + ZG_W), BF16)],
        compiler_params=_cparams(("arbitrary",)),
        name="proj_l0",
    )(x, g, mod, mod, w_t)


def _head_lane_mask(shape, h):
    lane = lax.broadcasted_iota(jnp.int32, shape, len(shape) - 1)
    return (lane >= h * 64) & (lane < (h + 1) * 64)


def _head_row(h):
    return jnp.where(_head_lane_mask((1, HK), h), 1.0, 0.0).astype(BF16)


def _gla_prepare(za_ref, la_ref, r0, rev, tri):
    L = RCHUNK
    rows = pl.ds(r0, L)
    b = _tri_dot(tri, la_ref[rows, :])
    mid = L // 2
    ref = b[mid:mid + 1, :]
    b_end = b[0:1, :] if rev else b[L - 1:L, :]
    q = za_ref[rows, 0:HK]
    k = za_ref[rows, HK:2 * HK]
    v = za_ref[rows, 2 * HK:2 * HK + HV].astype(BF16)
    qe = (q * (jnp.exp(b - ref) * (A_DK ** -0.5))).astype(BF16)
    ke = (k * jnp.exp(ref - b)).astype(BF16)
    qm = jnp.concatenate([qe * _head_row(h) for h in range(HEADS)], axis=0)
    rt = _dot_tn(v, ke)
    upd = None
    for h in range(HEADS):
        blk = rt[h * 128:(h + 1) * 128, :]
        blk = jnp.where(_head_lane_mask(blk.shape, h), blk, 0.0)
        upd = blk if upd is None else upd + blk
    return dict(rows=rows, qm=qm, ke=ke, v=v, upd=upd * jnp.exp(b_end - ref),
                decay=jnp.exp(b_end), eref=jnp.exp(ref))


def _gla_apply(items):
    L = RCHUNK
    sts = [st_ref[...] for _, st_ref, _, _ in items]
    o1s = []
    for (p, _, _, _), st in zip(items, sts):
        rhs = jnp.concatenate([p["ke"], (st * p["eref"]).astype(BF16)], axis=0)
        o1s.append(_dot_nt(p["qm"], rhs))
    for h in range(HEADS):
        for (p, _, o_ref, mask), o1 in zip(items, o1s):
            blk = o1[h * L:(h + 1) * L, :]
            sc = jnp.where(mask, blk[:, 0:L], 0.0).astype(BF16)
            o_ref[p["rows"], h * 128:(h + 1) * 128] = (
                blk[:, L:L + 128] + _dot(sc, p["v"][:, h * 128:(h + 1) * 128]))
    for (p, st_ref, _, _), st in zip(items, sts):
        st_ref[...] = st * p["decay"] + p["upd"]


def _gla_exact_scan(za_ref, la_ref, st_ref, o_ref, b_ref, n, rev, st_init):
    L = RCHUNK
    nc = n // L
    m_lo, m_up = _tri_masks(L)
    tri = jnp.where(m_up if rev else m_lo, 1.0, 0.0).astype(BF16)
    s_idx = lax.broadcasted_iota(jnp.int32, (L, 1), 0)
    krow = lax.broadcasted_iota(jnp.int32, (HK, 128), 0) // 64
    kcol = lax.broadcasted_iota(jnp.int32, (HK, 128), 1)
    head_sum = jnp.where(krow == kcol, 1.0, 0.0)
    hsub = lax.broadcasted_iota(jnp.int32, (8, HK), 0)
    hlane = lax.broadcasted_iota(jnp.int32, (8, HK), 1) // 64
    head_rows = jnp.where(hsub == hlane, 1.0, 0.0)
    st_ref[...] = st_init

    def chunk_body(ci, carry):
        c = nc - 1 - ci if rev else ci
        r0 = pl.multiple_of(c * L, L)
        rows = pl.ds(r0, L)
        b = _tri_dot(tri, la_ref[rows, :])
        b_ref[...] = b
        b_end = b[0:1, :] if rev else b[L - 1:L, :]
        k = za_ref[rows, HK:2 * HK]
        vb = za_ref[rows, 2 * HK:2 * HK + HV].astype(BF16)
        st = st_ref[...]
        stb = st.astype(BF16)

        def row_body(t, carry2):
            b_t = b_ref[pl.ds(t, 1), :]
            q_t = za_ref[pl.ds(r0 + t, 1), 0:HK] * (A_DK ** -0.5)
            valid = (s_idx >= t) if rev else (s_idx <= t)
            w = jnp.exp(jnp.minimum(b_t - b_ref[...], 0.0))
            p = jnp.where(valid, (q_t * k) * w, 0.0)
            sc = jnp.dot(p, head_sum, precision=lax.Precision.HIGHEST,
                         preferred_element_type=F32)
            pv = _dot_tn(sc.astype(BF16), vb)
            qm = (head_rows * (q_t * jnp.exp(b_t))).astype(BF16)
            inter = _dot_nt(qm, stb)
            o_ref[pl.ds(r0 + t, 1), :] = jnp.concatenate(
                [pv[h:h + 1, h * 128:(h + 1) * 128] + inter[h:h + 1, :] for h in range(HEADS)],
                axis=1)
            return carry2

        lax.fori_loop(0, L, row_body, 0)
        kd = (k * jnp.exp(b_end - b)).astype(BF16)
        rt = _dot_tn(vb, kd)
        upd = None
        for h in range(HEADS):
            blk = rt[h * 128:(h + 1) * 128, :]
            blk = jnp.where(_head_lane_mask(blk.shape, h), blk, 0.0)
            upd = blk if upd is None else upd + blk
        st_ref[...] = st * jnp.exp(b_end) + upd
        return carry

    lax.fori_loop(0, nc, chunk_body, 0)


T_M, T_MOUT, T_G, T_SRCW, T_MPREV, T_CARRY, T_USED = 0, 8, 16, 24, 32, 40, 48


def _lane_scan(x, op, fill, is_fwd, pos, n):
    k = 1
    while k < RCHUNK:
        pf = jnp.where(pos >= k, pltpu.roll(x, k, axis=1), fill)
        sf = jnp.where(pos < RCHUNK - k, pltpu.roll(x, n - k, axis=1), fill)
        x = op(x, jnp.where(is_fwd, pf, sf))
        k *= 2
    return x


def _mlstm_gate_tables(g, m0, gt_ref, tab_ref, col_ref, n):
    L = RCHUNK
    nc = n // L
    for c in range(nc):
        gt_ref[:, c * L:(c + 1) * L] = g[c * L:(c + 1) * L, :].T
        tab_ref[c, T_USED:, :] = jnp.zeros((ZG_W - T_USED, L), F32)
    gi = gt_ref[GATE_COL:GATE_COL + 8, :]
    lf = gt_ref[GATE_COL + 8:GATE_COL + 16, :]
    is_fwd = lax.broadcasted_iota(jnp.int32, (8, n), 0) < HEADS
    pos = lax.broadcasted_iota(jnp.int32, (8, n), 1) % L
    b = _lane_scan(lf, jnp.add, 0.0, is_fwd, pos, n)
    gg = gi - b
    cmax = _lane_scan(gg, jnp.maximum, -jnp.inf, is_fwd, pos, n)
    fwd_l = lax.broadcasted_iota(jnp.int32, (8, L), 0) < HEADS
    mp = m0
    for j in range(nc):
        cf, cb = j, nc - 1 - j

        def pick(x):
            return jnp.where(fwd_l, x[:, cf * L:(cf + 1) * L], x[:, cb * L:(cb + 1) * L])

        gj, bj, cj, lj = pick(gg), pick(b), pick(cmax), pick(lf)
        mx = jnp.maximum(mp, jnp.max(gj, axis=1, keepdims=True))
        mj = jnp.maximum(cj, mp)
        blocks = (mj, bj + mj, gj, jnp.exp(gj - mx),
                  jnp.broadcast_to(mp, (8, L)), jnp.broadcast_to(jnp.exp(mp - mx), (8, L)))
        for t, val in enumerate(blocks):
            tab_ref[cf, 8 * t:8 * t + HEADS, :] = val[0:HEADS]
            tab_ref[cb, 8 * t + HEADS:8 * t + 8, :] = val[HEADS:8]
        mp = jnp.sum(lj, axis=1, keepdims=True) + mx
    for c in range(nc):
        col_ref[c * L:(c + 1) * L, :] = tab_ref[c].T
    return mp


def _mlstm_prepare(zb_ref, tab_ref, col_ref, c, d, mask):
    L = RCHUNK
    rows = pl.ds(pl.multiple_of(c * L, L), L)
    qb = zb_ref[rows, 0:HK].astype(BF16)
    kb = (zb_ref[rows, HK:2 * HK] * (B_DK ** -0.5)).astype(BF16)
    v = zb_ref[rows, 2 * HK:2 * HK + HV]
    vb = v.astype(BF16)
    vt = v.T
    qm = jnp.concatenate([qb * _head_row(h) for h in range(HEADS)], axis=0)
    s = _dot_nt(qm, kb)
    ones = jnp.ones((L, 128), BF16)
    pv, w_inter, e_negm = [], [], []
    upd = None
    carry = None
    for h in range(HEADS):
        r = 4 * d + h
        m_rep = jnp.broadcast_to(col_ref[rows, T_M + r:T_M + r + 1], (L, 128))
        mout_rep = jnp.broadcast_to(col_ref[rows, T_MOUT + r:T_MOUT + r + 1], (L, 128))
        g_row = tab_ref[c, T_G + r:T_G + r + 1, :]
        w = jnp.exp(jnp.where(mask, g_row - m_rep, -jnp.inf))
        qk = (s[h * L:(h + 1) * L, :] * w).astype(BF16)
        v1 = jnp.concatenate([vb[:, h * 128:(h + 1) * 128], ones], axis=1)
        pv.append(_dot(qk, v1))
        w_inter.append(jnp.exp(tab_ref[c, T_MPREV + r:T_MPREV + r + 1, :] - m_rep))
        e_negm.append(jnp.exp(-mout_rep))
        sw = tab_ref[c, T_SRCW + r:T_SRCW + r + 1, :]
        lhs = jnp.concatenate([vt[h * 128:(h + 1) * 128, :] * sw, jnp.broadcast_to(sw, (128, L))],
                              axis=0).astype(BF16)
        blk = _dot(lhs, kb)
        blk = jnp.where(_head_lane_mask(blk.shape, h), blk, 0.0)
        upd = blk if upd is None else upd + blk
        cr = tab_ref[c, T_CARRY + r:T_CARRY + r + 1, :]
        cr = jnp.where(_head_lane_mask((1, HK), h), jnp.concatenate([cr, cr], axis=1), 0.0)
        carry = cr if carry is None else carry + cr
    return dict(rows=rows, qm=qm, pv=pv, w_inter=w_inter, e_negm=e_negm, upd=upd, carry=carry)


def _mlstm_apply(items):
    L = RCHUNK
    sts = [st_ref[...] for _, st_ref, _ in items]
    a_all = [_dot_nt(p["qm"], st.astype(BF16)) for (p, _, _), st in zip(items, sts)]
    for h in range(HEADS):
        for (p, _, h_ref), a in zip(items, a_all):
            blk = a[h * L:(h + 1) * L, :]
            pv, w, e = p["pv"][h], p["w_inter"][h], p["e_negm"][h]
            num = w * blk[:, 0:128] + pv[:, 0:128]
            den = w * blk[:, 128:256] + pv[:, 128:256]
            h_ref[p["rows"], h * 128:(h + 1) * 128] = num / jnp.maximum(jnp.abs(den), e)
    for (p, st_ref, _), st in zip(items, sts):
        st_ref[...] = p["carry"] * st + p["upd"]


def _rec_kernel(*refs, n, has_init):
    it = iter(refs)

    def take(k):
        return [next(it) for _ in range(k)]

    za_all, zb_all, zg_all, zgn_all = take(4)
    wgf_ref, wgb_ref, bgf_ref, bgb_ref, gna_ref, bg_ref, gnb_ref = take(7)
    if has_init:
        init_all = take(7)
    out_all = take(8)
    (gt_all,) = take(1)
    set_a_all = take(6)
    set_b_all = take(6)
    work_all = take(9)
    L = RCHUNK
    nc = n // L
    step = pl.program_id(0)
    elems = []
    for e in range(za_all.shape[0]):
        el = dict(za=za_all.at[e], zb=zb_all.at[e], zg=zg_all.at[e], zgn=zgn_all.at[e],
                  gt=gt_all.at[e], set_a=[r.at[e] for r in set_a_all],
                  set_b=[r.at[e] for r in set_b_all])
        (el["oa"], el["sf"], el["sb"], el["om"], el["cf"], el["cb"], el["n_out"],
         el["m_out"]) = [r.at[e] for r in out_all]
        (el["of"], el["ob"], el["hf"], el["hb"], el["gsf"], el["gsb"], el["msf"],
         el["msb"], el["b_exact"]) = [r.at[e] for r in work_all]
        if has_init:
            el["init"] = [r.at[e] for r in init_all]
        elems.append(el)

    def fill(gate_ref, m0, bufs, gt_ref):
        laf_ref, lab_ref, tab_ref, col_ref, mf_ref, span_ref = bufs
        zg = gate_ref[...]
        laf = _log_sigmoid(_dot_hl(zg, wgf_ref[...]) + bgf_ref[...]) / A_GATE_TEMP
        lab = _log_sigmoid(_dot_hl(zg, wgb_ref[...]) + bgb_ref[...]) / A_GATE_TEMP
        laf_ref[...] = laf
        lab_ref[...] = lab
        span = None
        for la in (laf, lab):
            for c in range(nc):
                tot = -jnp.sum(la[c * L:(c + 1) * L, :], axis=0, keepdims=True)
                span = tot if span is None else jnp.maximum(span, tot)
        span_ref[...] = jnp.broadcast_to(jnp.max(span, axis=1, keepdims=True), (8, 128))
        g = zg + bg_ref[...]
        lane = lax.broadcasted_iota(jnp.int32, g.shape, 1)
        is_forget = (lane >= GATE_COL + 8) & (lane < GATE_COL + 16)
        g = jnp.where(is_forget, _log_sigmoid(g), g)
        m_fin = _mlstm_gate_tables(g, m0, gt_ref, tab_ref, col_ref, n)
        mf_ref[...] = jnp.broadcast_to(m_fin, (8, 128))

    for el in elems:
        if has_init:
            sf0_ref, sb0_ref, cf0_ref, cb0_ref, n0_ref, m0_ref, m0n_ref = el["init"]
            el["m0"], el["m0_next"] = m0_ref[:, 0:1], m0n_ref[:, 0:1]
            el["gsf"][...] = sf0_ref[...]
            el["gsb"][...] = sb0_ref[...]
            el["msf"][...] = jnp.concatenate(
                [cf0_ref[...], jnp.broadcast_to(n0_ref[0:1, :], (128, HK))], axis=0)
            el["msb"][...] = jnp.concatenate(
                [cb0_ref[...], jnp.broadcast_to(n0_ref[1:2, :], (128, HK))], axis=0)
        else:
            el["m0"] = el["m0_next"] = jnp.zeros((8, 1), F32)
            for name in ("gsf", "gsb", "msf", "msb"):
                el[name][...] = jnp.zeros(el[name].shape, F32)

    @pl.when(step == 0)
    def _():
        for el in elems:
            fill(el["zg"], el["m0"], el["set_a"], el["gt"])

    m_lo, m_up = _tri_masks(L)
    tri_lo = jnp.where(m_lo, 1.0, 0.0).astype(BF16)
    tri_up = jnp.where(m_up, 1.0, 0.0).astype(BF16)
    gna = gna_ref[...]
    gnb = gnb_ref[...]

    def out_body(c, carry):
        rows = pl.ds(pl.multiple_of(c * L, L), L)
        for h in range(HEADS):
            cols = slice(h * 128, (h + 1) * 128)
            gcols = slice(2 * HK + HV + h * 128, 2 * HK + HV + (h + 1) * 128)
            for el in elems:
                o = el["of"][rows, cols] + el["ob"][rows, cols]
                el["oa"][rows, cols] = (
                    _rms(o, gna) * _silu(el["za"][rows, gcols])).astype(BF16)
                hh = el["hf"][rows, cols] + el["hb"][rows, cols]
                el["om"][rows, cols] = (
                    _sigmoid(el["zb"][rows, gcols]) * _rms(hh, gnb)).astype(BF16)
        return carry

    def run(cur, nxt):
        for el in elems:
            fill(el["zgn"], el["m0_next"], el[nxt], el["gt"])

        def scan_body(gidx, carry):
            waves = []
            for w in range(2):
                c = 2 * gidx + w
                cb = nc - 1 - c
                gla_items, mlstm_items = [], []
                for el in elems:
                    laf_ref, lab_ref, tab_ref, col_ref = el[cur][0:4]
                    gla_items += [
                        (_gla_prepare(el["za"], laf_ref, pl.multiple_of(c * L, L), False, tri_lo),
                         el["gsf"], el["of"], m_lo),
                        (_gla_prepare(el["za"], lab_ref, pl.multiple_of(cb * L, L), True, tri_up),
                         el["gsb"], el["ob"], m_up)]
                    mlstm_items += [
                        (_mlstm_prepare(el["zb"], tab_ref, col_ref, c, 0, m_lo), el["msf"], el["hf"]),
                        (_mlstm_prepare(el["zb"], tab_ref, col_ref, cb, 1, m_up), el["msb"], el["hb"])]
                waves.append((gla_items, mlstm_items))
            for gla_items, mlstm_items in waves:
                _gla_apply(gla_items)
                _mlstm_apply(mlstm_items)
            return carry

        if nc == 2:
            scan_body(0, 0)
        else:
            lax.fori_loop(0, nc // 2, scan_body, 0)
        for el in elems:
            laf_ref, lab_ref = el[cur][0:2]
            too_fast = el[cur][5][0, 0] > GLA_SAFE_SPAN

            @pl.when(too_fast)
            def _(el=el, laf_ref=laf_ref, lab_ref=lab_ref):
                zero = jnp.zeros((128, HK), F32)
                init_f = el["init"][0][...] if has_init else zero
                init_b = el["init"][1][...] if has_init else zero
                _gla_exact_scan(el["za"], laf_ref, el["gsf"], el["of"], el["b_exact"], n, False, init_f)
                _gla_exact_scan(el["za"], lab_ref, el["gsb"], el["ob"], el["b_exact"], n, True, init_b)

        if nc == 2:
            for c in range(nc):
                out_body(c, 0)
        else:
            lax.fori_loop(0, nc, out_body, 0)
        for el in elems:
            el["m_out"][...] = el[cur][4][...]

    @pl.when(step % 2 == 0)
    def _():
        run("set_a", "set_b")

    @pl.when(step % 2 == 1)
    def _():
        run("set_b", "set_a")

    for el in elems:
        gsf_t = el["gsf"][...].T
        gsb_t = el["gsb"][...].T
        msf = el["msf"][...]
        msb = el["msb"][...]
        ctf_t = msf[0:128, :].T
        ctb_t = msb[0:128, :].T
        for h in range(HEADS):
            rows = slice(h * 64, (h + 1) * 64)
            el["sf"][h] = gsf_t[rows, :]
            el["sb"][h] = gsb_t[rows, :]
            el["cf"][h] = ctf_t[rows, :]
            el["cb"][h] = ctb_t[rows, :]
        el["n_out"][...] = jnp.concatenate(
            [msf[128:129, :], msb[128:129, :], jnp.zeros((6, HK), F32)], axis=0)


def _recurrent(za, zb, zg, wgf, wgb, bgf, bgb, gna, bg, gnb, init, bt):
    bsz, n, _ = za.shape
    has_init = init is not None
    nc = n // RCHUNK
    steps = bsz // bt
    per_b3 = lambda b: (b, 0, 0)
    next_b3 = lambda b: (jnp.minimum(b + 1, steps - 1), 0, 0)
    per_b4 = lambda b: (b, 0, 0, 0)
    fixed = lambda b: (0, 0)
    in_specs = [pl.BlockSpec((bt, n, ZA_W), per_b3),
                pl.BlockSpec((bt, n, ZA_W), per_b3),
                pl.BlockSpec((bt, n, ZG_W), per_b3),
                pl.BlockSpec((bt, n, ZG_W), next_b3),
                pl.BlockSpec((ZG_W, HK), fixed),
                pl.BlockSpec((ZG_W, HK), fixed),
                pl.BlockSpec((1, HK), fixed),
                pl.BlockSpec((1, HK), fixed),
                pl.BlockSpec((1, 128), fixed),
                pl.BlockSpec((1, ZG_W), fixed),
                pl.BlockSpec((1, 128), fixed)]
    args = [za, zb, zg, zg, wgf, wgb, bgf, bgb, gna, bg, gnb]
    if has_init:
        sf0, sb0, cf0, cb0, n0, m0 = init
        in_specs += [pl.BlockSpec((bt, 128, HK), per_b3)] * 4
        in_specs += [pl.BlockSpec((bt, 8, HK), per_b3), pl.BlockSpec((bt, 8, 128), per_b3),
                     pl.BlockSpec((bt, 8, 128), next_b3)]
        args += [sf0, sb0, cf0, cb0, n0, m0, m0]
    state = jax.ShapeDtypeStruct((bsz, HEADS, 64, 128), F32)
    state_spec = pl.BlockSpec((bt, HEADS, 64, 128), per_b4)

    def vmem(*shape):
        return pltpu.VMEM((bt,) + shape, F32)

    gate_set = [vmem(n, HK), vmem(n, HK), vmem(nc, ZG_W, RCHUNK), vmem(n, ZG_W), vmem(8, 128),
                vmem(8, 128)]
    return pl.pallas_call(
        functools.partial(_rec_kernel, n=n, has_init=has_init),
        out_shape=(jax.ShapeDtypeStruct((bsz, n, HV), BF16), state, state,
                   jax.ShapeDtypeStruct((bsz, n, HV), BF16), state, state,
                   jax.ShapeDtypeStruct((bsz, 8, HK), F32),
                   jax.ShapeDtypeStruct((bsz, 8, 128), F32)),
        grid=(steps,),
        in_specs=in_specs,
        out_specs=(pl.BlockSpec((bt, n, HV), per_b3), state_spec, state_spec,
                   pl.BlockSpec((bt, n, HV), per_b3), state_spec, state_spec,
                   pl.BlockSpec((bt, 8, HK), per_b3),
                   pl.BlockSpec((bt, 8, 128), per_b3)),
        scratch_shapes=([vmem(ZG_W, n)] + gate_set + gate_set + [vmem(n, HV)] * 4
                        + [vmem(128, HK)] * 2 + [vmem(2 * 128, HK)] * 2 + [vmem(RCHUNK, HK)]),
        compiler_params=_cparams(("arbitrary",)),
        name="recurrent",
    )(*args)


def _mix_residual(parts, wo_ref, x_ref, gt1_ref, gpost1_ref, gpre2_ref, sc2_ref, sh2_ref):
    m = None
    off = 0
    for p in parts:
        kp = p.shape[1]
        t = _dot(p[...], wo_ref[off:off + kp, :])
        m = t if m is None else m + t
        off += kp
    x1 = x_ref[...] + gt1_ref[...] * _rms(m, gpost1_ref[...])
    hb = (_rms(x1, gpre2_ref[...]) * (1.0 + sc2_ref[...]) + sh2_ref[...]).astype(BF16)
    return x1, hb


def _ffn_gated_up(hb, wu_ref, cw_ref, cb_ref, act_ref, seq, tf):
    tm = hb.shape[0]
    pos = lax.broadcasted_iota(jnp.int32, (tm, 1), 0) % seq
    first = pos == 0
    last = pos == seq - 1

    def conv(u, cols):
        prev = jnp.where(first, 0.0, pltpu.roll(u, 1, axis=0))
        nxt = jnp.where(last, 0.0, pltpu.roll(u, tm - 1, axis=0))
        return (cw_ref[0:1, cols] * prev + cw_ref[1:2, cols] * u + cw_ref[2:3, cols] * nxt
                + cb_ref[:, cols])

    for j in range(D_FF // tf):
        ca = slice(j * tf, (j + 1) * tf)
        cg = slice(D_FF + j * tf, D_FF + (j + 1) * tf)
        a = conv(_dot(hb, wu_ref[:, ca]), ca)
        g = conv(_dot(hb, wu_ref[:, cg]), cg)
        act_ref[:, ca] = (_silu(g) * a).astype(BF16)


def _mix_ffn_kernel(*refs, n_parts, seq, tf):
    parts = refs[:n_parts]
    (wo_ref, x_ref, gt1_ref, gpost1_ref, gpre2_ref, sc2_ref, sh2_ref,
     wu_ref, cw_ref, cb_ref, wd_ref, gt2_ref, gpost2_ref, o_ref, act_ref) = refs[n_parts:]
    x1, hb = _mix_residual(parts, wo_ref, x_ref, gt1_ref, gpost1_ref, gpre2_ref, sc2_ref, sh2_ref)
    o_ref[...] = x1
    _ffn_gated_up(hb, wu_ref, cw_ref, cb_ref, act_ref, seq, tf)
    y = _dot(act_ref[...], wd_ref[...])
    o_ref[...] = o_ref[...] + gt2_ref[...] * _rms(y, gpost2_ref[...])


def _resident(shape):
    return pl.BlockSpec(shape, lambda *_: (0,) * len(shape), pipeline_mode=pl.Buffered(1))


def _mix_ffn(parts, w_out, x, mod, first_row, groups, gpost1, gpre2, w_up, conv_w, conv_b, w_down,
             gpost2, tm, seq, tf=256):
    t = x.shape[0]
    per = (t // tm) // groups
    row = lambda i: (i, 0)
    vec = _resident((1, D_MODEL))
    in_specs = [pl.BlockSpec((tm, p.shape[1]), row) for p in parts]
    in_specs += [_resident(w_out.shape), pl.BlockSpec((tm, D_MODEL), row),
                 _mod_spec(MOD_GATE1, first_row, per), vec, vec,
                 _mod_spec(MOD_SCALE2, first_row, per), _mod_spec(MOD_SHIFT2, first_row, per),
                 _resident(w_up.shape), _resident(conv_w.shape), _resident(conv_b.shape),
                 _resident(w_down.shape), _mod_spec(MOD_GATE2, first_row, per), vec]
    return pl.pallas_call(
        functools.partial(_mix_ffn_kernel, n_parts=len(parts), seq=seq, tf=tf),
        out_shape=jax.ShapeDtypeStruct((t, D_MODEL), F32),
        grid=(t // tm,),
        in_specs=in_specs,
        out_specs=pl.BlockSpec((tm, D_MODEL), row),
        scratch_shapes=[pltpu.VMEM((tm, D_FF), BF16)],
        compiler_params=_cparams(("arbitrary",)),
        name="mix_ffn",
    )(*parts, w_out, x, mod, gpost1, gpre2, mod, mod, w_up, conv_w, conv_b, w_down, mod, gpost2)


def _rope(x, cos, sin_signed):
    lane = lax.broadcasted_iota(jnp.int32, x.shape, 1)
    lower = (lane % 32) < 16
    partner = jnp.where(lower, pltpu.roll(x, 128 - 16, axis=1), pltpu.roll(x, 16, axis=1))
    return x * cos + partner * sin_signed


def _p1_kernel(*refs, rope):
    if rope:
        (x_ref, g_ref, sc_ref, sh_ref, w_ref, gq_ref, wq_ref, gkv_ref, cos_ref, sin_ref,
         q_ref, ckv_ref, kr_ref) = refs
    else:
        (x_ref, g_ref, sc_ref, sh_ref, w_ref, gq_ref, wq_ref, gkv_ref,
         q_ref, ckv_ref, kr_ref) = refs
    h = _rms(x_ref[...], g_ref[...]) * (1.0 + sc_ref[...]) + sh_ref[...]
    z = _dot(h.astype(BF16), w_ref[...])
    qa = _rms(z[:, 0:C_Q_RANK], gq_ref[...])
    q = _dot(qa.astype(BF16), wq_ref[...])
    ckv_ref[...] = _rms(z[:, C_Q_RANK:C_Q_RANK + C_KV_RANK], gkv_ref[...])
    kr = z[:, C_Q_RANK + C_KV_RANK:L1_IN_PAD]
    if rope:
        cos = cos_ref[...]
        sin = sin_ref[...]
        kr = _rope(kr, cos, sin)
    kr_ref[...] = kr[:, 0:C_ROPE]
    for hd in range(C_HEADS):
        c0 = hd * QCAT
        q_ref[:, c0:c0 + 128] = q[:, c0:c0 + 128].astype(q_ref.dtype)
        qr = q[:, c0 + 128:c0 + 256]
        if rope:
            qr = _rope(qr, cos, sin)
        q_ref[:, c0 + 128:c0 + 256] = qr.astype(q_ref.dtype)


def _proj_l1(x, g, mod, first_row, groups, w, gq, wq, gkv, tables, tm):
    t = x.shape[0]
    per = (t // tm) // groups
    row = lambda i: (i, 0)
    fixed = lambda i: (0, 0)
    in_specs = [pl.BlockSpec((tm, D_MODEL), row),
                pl.BlockSpec((1, D_MODEL), fixed),
                _mod_spec(MOD_SCALE1, first_row, per),
                _mod_spec(MOD_SHIFT1, first_row, per),
                pl.BlockSpec(w.shape, fixed),
                pl.BlockSpec((1, C_Q_RANK), fixed),
                pl.BlockSpec(wq.shape, fixed),
                pl.BlockSpec((1, C_KV_RANK), fixed)]
    args = [x, g, mod, mod, w, gq, wq, gkv]
    rope = tables is not None
    if rope:
        n = tables[0].shape[0]
        pos = lambda i: (i % (n // tm), 0)
        in_specs += [pl.BlockSpec((tm, 128), pos)] * 2
        args += list(tables)
    return pl.pallas_call(
        functools.partial(_p1_kernel, rope=rope),
        out_shape=(jax.ShapeDtypeStruct((t, C_HEADS * QCAT), BF16),
                   jax.ShapeDtypeStruct((t, C_KV_RANK), F32),
                   jax.ShapeDtypeStruct((t, C_ROPE), F32)),
        grid=(t // tm,),
        in_specs=in_specs,
        out_specs=(pl.BlockSpec((tm, C_HEADS * QCAT), row),
                   pl.BlockSpec((tm, C_KV_RANK), row),
                   pl.BlockSpec((tm, C_ROPE), row)),
        compiler_params=_cparams(("arbitrary",)),
        name="proj_l1",
    )(*args)


def _attn_kernel(*refs, n_parts, scale):
    q_ref = refs[0]
    kv_parts = refs[1:1 + 2 * n_parts]
    w_ref, o_ref, k_sc, v_sc = refs[1 + 2 * n_parts:]
    bt = q_ref.shape[0]

    @pl.when(pl.program_id(1) == 0)
    def _():
        off = 0
        for p in range(n_parts):
            ckv_ref, kr_ref = kv_parts[2 * p], kv_parts[2 * p + 1]
            nk = ckv_ref.shape[1]
            ones = jnp.ones((nk, C_DV), BF16)
            for e in range(bt):
                kv = _dot(ckv_ref[e].astype(BF16), w_ref[...])
                kr = kr_ref[e].astype(BF16)
                kr2 = jnp.concatenate([kr, kr], axis=1)
                for hd in range(C_HEADS):
                    v0 = C_HEADS * C_NOPE + hd * C_DV
                    k_sc[e, hd, off:off + nk, 0:128] = kv[:, hd * 128:(hd + 1) * 128].astype(BF16)
                    k_sc[e, hd, off:off + nk, 128:256] = kr2
                    v_sc[e, hd, off:off + nk, 0:C_DV] = kv[:, v0:v0 + C_DV].astype(BF16)
                    v_sc[e, hd, off:off + nk, C_DV:2 * C_DV] = ones
            off += nk

    for hd in range(C_HEADS):
        for e in range(bt):
            t = _dot_nt(q_ref[e, :, hd * QCAT:(hd + 1) * QCAT], k_sc[e, hd]) * (scale * LOG2E)
            p = jnp.exp2(t - jnp.max(t, axis=1, keepdims=True))
            ov = _dot(p.astype(BF16), v_sc[e, hd])
            o_ref[e, :, hd * C_DV:(hd + 1) * C_DV] = (
                ov[:, 0:C_DV] / ov[:, C_DV:2 * C_DV]).astype(o_ref.dtype)


def _attention(q, kv_parts, w_kvb, tq, bt):
    bsz, n, _ = q.shape
    nk = sum(p[0].shape[1] for p in kv_parts)
    in_specs = [pl.BlockSpec((bt, tq, C_HEADS * QCAT), lambda b, i: (b, i, 0))]
    args = [q]
    for ckv, kr in kv_parts:
        in_specs.append(pl.BlockSpec((bt, ckv.shape[1], C_KV_RANK), lambda b, i: (b, 0, 0)))
        in_specs.append(pl.BlockSpec((bt, kr.shape[1], C_ROPE), lambda b, i: (b, 0, 0)))
        args += [ckv, kr]
    in_specs.append(pl.BlockSpec(w_kvb.shape, lambda b, i: (0, 0)))
    args.append(w_kvb)
    return pl.pallas_call(
        functools.partial(_attn_kernel, n_parts=len(kv_parts),
                          scale=(C_NOPE + C_ROPE) ** -0.5),
        out_shape=jax.ShapeDtypeStruct((bsz, n, C_HEADS * C_DV), BF16),
        grid=(bsz // bt, n // tq),
        in_specs=in_specs,
        out_specs=pl.BlockSpec((bt, tq, C_HEADS * C_DV), lambda b, i: (b, i, 0)),
        scratch_shapes=[pltpu.VMEM((bt, C_HEADS, nk, QCAT), BF16),
                        pltpu.VMEM((bt, C_HEADS, nk, 2 * C_DV), BF16)],
        compiler_params=_cparams(("arbitrary", "arbitrary")),
        name="mla_attention",
    )(*args)


def _l1_wq_layout(w_qb):
    blocks = []
    zero = jnp.zeros((w_qb.shape[0], C_ROPE), w_qb.dtype)
    for hd in range(C_HEADS):
        base = hd * (C_NOPE + C_ROPE)
        nope = w_qb[:, base:base + C_NOPE]
        rope = w_qb[:, base + C_NOPE:base + C_NOPE + C_ROPE]
        blocks += [nope, rope, zero] if hd % 2 == 0 else [nope, zero, rope]
    return jnp.concatenate(blocks, axis=1).astype(BF16)


def _l1_wkvb_layout(w_kvb):
    w = w_kvb.reshape(C_KV_RANK, C_HEADS, C_NOPE + C_DV)
    k = w[:, :, :C_NOPE].reshape(C_KV_RANK, C_HEADS * C_NOPE)
    v = w[:, :, C_NOPE:].reshape(C_KV_RANK, C_HEADS * C_DV)
    return jnp.concatenate([k, v], axis=1).astype(BF16)


def _rope_tables(n):
    pos = np.arange(n)
    row = (pos // GRID_W).astype(np.float64)
    col = (pos % GRID_W).astype(np.float64)
    half = C_ROPE // 2
    inv = 1.0 / (ROPE_THETA ** (np.arange(0, half, 2, dtype=np.float64) / half))
    ang_r = row[:, None] * inv[None, :]
    ang_c = col[:, None] * inv[None, :]
    cos64 = np.concatenate([np.cos(ang_r)] * 2 + [np.cos(ang_c)] * 2, axis=1)
    sin64 = np.concatenate([-np.sin(ang_r), np.sin(ang_r), -np.sin(ang_c), np.sin(ang_c)], axis=1)
    return (jnp.asarray(np.concatenate([cos64] * 2, axis=1), F32),
            jnp.asarray(np.concatenate([sin64] * 2, axis=1), F32))


def _gate_weight(w, first_row):
    return jnp.pad(w, ((first_row, ZG_W - first_row - A_GATE_RANK), (0, 0)))


def _state_t(s):
    b = s.shape[0]
    return s.transpose(0, 3, 1, 2).reshape(b, 128, HK)


def _n_rows(n_fwd, n_bwd):
    b = n_fwd.shape[0]
    rows = jnp.zeros((b, 8, HK), F32)
    return rows.at[:, 0, :].set(n_fwd.reshape(b, HK)).at[:, 1, :].set(n_bwd.reshape(b, HK))


def _m_rows(m_fwd, m_bwd):
    m = jnp.concatenate([m_fwd, m_bwd], axis=1)
    return jnp.broadcast_to(m[:, :, None], (m.shape[0], 8, 128))


def kernel(x_prompt, x_sample, state_l0_gla_fwd, state_l0_gla_bwd, state_l0_mlstm_c_fwd, state_l0_mlstm_n_fwd, state_l0_mlstm_m_fwd, state_l0_mlstm_c_bwd, state_l0_mlstm_n_bwd, state_l0_mlstm_m_bwd, cache_l1_ckv, cache_l1_krope, c, c_ctx, l0_w_mod, l0_b_mod, l0_g_pre_mix, l0_g_post_mix, l0_g_pre_ffn, l0_g_post_ffn, l0_w_in, l0_gla_w_gate_f, l0_gla_b_gate_f, l0_gla_w_gate_b, l0_gla_b_gate_b, l0_gla_g_norm, l0_mlstm_b_gates, l0_mlstm_g_norm, l0_w_out, l0_ffn_w_up, l0_ffn_conv_w, l0_ffn_conv_b, l0_ffn_w_down, l1_w_mod, l1_b_mod, l1_g_pre_mix, l1_g_post_mix, l1_g_pre_ffn, l1_g_post_ffn, l1_w_in, l1_g_q_norm, l1_w_qb, l1_g_kv_norm, l1_w_kvb, l1_w_out, l1_ffn_w_up, l1_ffn_conv_w, l1_ffn_conv_b, l1_ffn_w_down):
    bp, sp, d = x_prompt.shape
    bs, ss, _ = x_sample.shape
    row = lambda v: v.reshape(1, -1)

    cond8 = jnp.concatenate([c_ctx[None, :], c, jnp.zeros((8 - 1 - bs, d), F32)], axis=0)
    mods = (_modulation(cond8, l0_w_mod, l0_b_mod), _modulation(cond8, l1_w_mod, l1_b_mod))

    paths = {
        "ctx": dict(x=x_prompt.reshape(bp * sp, d), b=bp, n=sp, first_row=0, groups=1, tm=512),
        "lat": dict(x=x_sample.reshape(bs * ss, d), b=bs, n=ss, first_row=1, groups=bs, tm=512),
    }

    w_in0 = l0_w_in.T
    wgf = _gate_weight(l0_gla_w_gate_f, 0)
    wgb = _gate_weight(l0_gla_w_gate_b, A_GATE_RANK)
    bg = jnp.pad(l0_mlstm_b_gates, (GATE_COL, ZG_W - GATE_COL - 16)).reshape(1, ZG_W)
    w_out0 = l0_w_out.astype(BF16)
    w_in1 = jnp.pad(l1_w_in, ((0, 0), (0, L1_IN_PAD - l1_w_in.shape[1]))).astype(BF16)
    w_q1 = _l1_wq_layout(l1_w_qb)
    w_kvb1 = _l1_wkvb_layout(l1_w_kvb)
    w_out1 = l1_w_out.astype(BF16)
    ffn = ((l0_ffn_w_up.astype(BF16), l0_ffn_conv_w, row(l0_ffn_conv_b), l0_ffn_w_down.astype(BF16)),
           (l1_ffn_w_up.astype(BF16), l1_ffn_conv_w, row(l1_ffn_conv_b), l1_ffn_w_down.astype(BF16)))
    norms = ((row(l0_g_pre_mix), row(l0_g_post_mix), row(l0_g_pre_ffn), row(l0_g_post_ffn)),
             (row(l1_g_pre_mix), row(l1_g_post_mix), row(l1_g_pre_ffn), row(l1_g_post_ffn)))

    gla_init = (_state_t(state_l0_gla_fwd), _state_t(state_l0_gla_bwd))
    mlstm_init = (_state_t(state_l0_mlstm_c_fwd), _state_t(state_l0_mlstm_c_bwd),
                  _n_rows(state_l0_mlstm_n_fwd, state_l0_mlstm_n_bwd),
                  _m_rows(state_l0_mlstm_m_fwd, state_l0_mlstm_m_bwd))
    tables = _rope_tables(ss)

    results = {}
    for name, p in paths.items():
        x, b, n, tm = p["x"], p["b"], p["n"], p["tm"]
        first_row, groups = p["first_row"], p["groups"]
        is_ctx = name == "ctx"
        ffn_tm = tm if is_ctx else n
        g_pre, g_post, g_pre2, g_post2 = norms[0]

        za, zb, zg = _proj_l0(x, g_pre, mods[0], first_row, groups, w_in0, tm)
        za, zb, zg = (t.reshape(b, n, -1) for t in (za, zb, zg))
        oa, sa_f, sa_b, om, c_f, c_b, n_rows, m_rows = _recurrent(
            za, zb, zg, wgf, wgb, row(l0_gla_b_gate_f), row(l0_gla_b_gate_b), row(l0_gla_g_norm),
            bg, row(l0_mlstm_g_norm), None if is_ctx else gla_init + mlstm_init,
            2 if is_ctx else 1)
        x = _mix_ffn([oa.reshape(b * n, HV), om.reshape(b * n, HV)], w_out0, x, mods[0], first_row,
                     groups, g_post, g_pre2, *ffn[0], g_post2, ffn_tm, n)

        g_pre, g_post, g_pre2, g_post2 = norms[1]
        q, ckv, kr = _proj_l1(x, g_pre, mods[1], first_row, groups, w_in1, row(l1_g_q_norm), w_q1,
                              row(l1_g_kv_norm), None if is_ctx else tables, tm)
        ckv3 = ckv.reshape(b, n, C_KV_RANK)
        kr3 = kr.reshape(b, n, C_ROPE)
        kv_parts = [(ckv3, kr3)] if is_ctx else [(cache_l1_ckv, cache_l1_krope), (ckv3, kr3)]
        o = _attention(q.reshape(b, n, -1), kv_parts, w_kvb1, 256, 4 if is_ctx else 1)
        x = _mix_ffn([o.reshape(b * n, C_HEADS * C_DV)], w_out1, x, mods[1], first_row,
                     groups, g_post, g_pre2, *ffn[1], g_post2, ffn_tm, n)

        results[name] = dict(y=x.reshape(b, n, d), gla=(sa_f, sa_b), c=(c_f, c_b),
                             n=n_rows, m=m_rows, ckv=ckv3, kr=kr3)

    r = results["ctx"]
    n_fwd = r["n"][:, 0, :].reshape(bp, B_HEADS, B_DK)
    n_bwd = r["n"][:, 1, :].reshape(bp, B_HEADS, B_DK)
    m_fwd = r["m"][:, 0:B_HEADS, 0]
    m_bwd = r["m"][:, B_HEADS:2 * B_HEADS, 0]
    return (r["y"], results["lat"]["y"], r["gla"][0], r["gla"][1],
            r["c"][0], n_fwd, m_fwd, r["c"][1], n_bwd, m_bwd, r["ckv"], r["kr"])
```

```python
import functools

import numpy as np
import jax
import jax.numpy as jnp
from jax import lax
from jax.experimental import pallas as pl
from jax.experimental.pallas import tpu as pltpu

F32 = jnp.float32
BF16 = jnp.bfloat16

D_MODEL = 1024
EPS = 1e-6
LOG2E = 1.4426950408889634
N_MOD = 6
GRID_W = 64
ROPE_THETA = 10000.0

A_HEADS, A_DK, A_DV, A_GATE_RANK, A_GATE_TEMP = 4, 64, 128, 16, 16.0
B_HEADS, B_DK, B_DV = 4, 64, 128
C_HEADS, C_Q_RANK, C_KV_RANK, C_NOPE, C_ROPE, C_DV = 8, 384, 256, 128, 64, 128
D_FF = 2816
CONV_W = 3

HEADS = 4
HK = HEADS * 64
HV = HEADS * 128
ZA_W = 1536
ZG_W = 128
GATE_COL = 32
RCHUNK = 128
GLA_SAFE_SPAN = 60.0
L1_IN_PAD = 768
QCAT = 256

V7X_VMEM_BYTES = 64 * 1024 * 1024
VMEM_LIMIT = V7X_VMEM_BYTES - 8 * 1024 * 1024


def _cparams(sem):
    return pltpu.CompilerParams(dimension_semantics=sem, vmem_limit_bytes=VMEM_LIMIT)


def _dot(a, b):
    return jnp.dot(a, b, preferred_element_type=F32)


def _dot_nt(a, b):
    return lax.dot_general(a, b, (((1,), (1,)), ((), ())), preferred_element_type=F32)


def _dot_tn(a, b):
    return lax.dot_general(a, b, (((0,), (0,)), ((), ())), preferred_element_type=F32)


def _split3(x):
    hi = x.astype(BF16)
    r1 = x - hi.astype(F32)
    mid = r1.astype(BF16)
    lo = (r1 - mid.astype(F32)).astype(BF16)
    return hi, mid, lo


def _tri_dot(tri, x):
    hi, mid, lo = _split3(x)
    return _dot(tri, hi) + _dot(tri, mid) + _dot(tri, lo)


def _dot_hl(a, b):
    ah = a.astype(BF16)
    al = (a - ah.astype(F32)).astype(BF16)
    bh = b.astype(BF16)
    bl = (b - bh.astype(F32)).astype(BF16)
    return _dot(ah, bh) + _dot(ah, bl) + _dot(al, bh)


def _rms(x, g):
    return x * lax.rsqrt(jnp.mean(x * x, axis=-1, keepdims=True) + EPS) * g


def _sigmoid(x):
    return 1.0 / (1.0 + jnp.exp(-x))


def _silu(x):
    return x * _sigmoid(x)


def _log_sigmoid(x):
    return jnp.minimum(x, 0.0) - jnp.log(1.0 + jnp.exp(-jnp.abs(x)))


def _tri_masks(n):
    r = lax.broadcasted_iota(jnp.int32, (n, n), 0)
    c = lax.broadcasted_iota(jnp.int32, (n, n), 1)
    return c <= r, c >= r


def _mod_kernel(cond_ref, w_ref, b_ref, o_ref):
    s = _silu(cond_ref[...])
    o_ref[...] = _dot(s.astype(BF16), w_ref[...].astype(BF16)) + b_ref[...]


def _modulation(cond8, w_mod, b_mod):
    d = D_MODEL
    out = pl.pallas_call(
        _mod_kernel,
        out_shape=jax.ShapeDtypeStruct((N_MOD, 8, d), F32),
        grid=(N_MOD,),
        in_specs=[pl.BlockSpec((8, d), lambda j: (0, 0)),
                  pl.BlockSpec((d, d), lambda j: (0, j)),
                  pl.BlockSpec((1, d), lambda j: (0, j))],
        out_specs=pl.BlockSpec((None, 8, d), lambda j: (j, 0, 0)),
        compiler_params=_cparams(("arbitrary",)),
        name="modulation",
    )(cond8, w_mod, b_mod.reshape(1, N_MOD * d))
    return out.reshape(N_MOD, 8, 1, d)


MOD_SHIFT1, MOD_SCALE1, MOD_GATE1, MOD_SHIFT2, MOD_SCALE2, MOD_GATE2 = range(N_MOD)


def _mod_spec(which, first_row, per):
    return pl.BlockSpec((None, None, 1, D_MODEL), lambda i: (which, first_row + i // per, 0, 0))


L0_GROUP_ROWS = ((0, ZA_W), (ZA_W + 32, 2 * ZA_W + 32))
L0_LR_ROWS = (ZA_W, ZA_W + 32)
L0_GATE_ROWS = (2 * ZA_W + 32, 2 * ZA_W + 48)


def _p0_kernel(x_ref, g_ref, sc_ref, sh_ref, wt_ref, za_ref, zb_ref, zg_ref, w_ref):
    @pl.when(pl.program_id(0) == 0)
    def _():
        blk = 256
        for gi, (r0, r1) in enumerate(L0_GROUP_ROWS):
            for j in range((r1 - r0) // blk):
                rows = wt_ref[r0 + j * blk:r0 + (j + 1) * blk, :]
                w_ref[:, gi * ZA_W + j * blk:gi * ZA_W + (j + 1) * blk] = rows.T.astype(BF16)
        tail = jnp.concatenate([wt_ref[L0_LR_ROWS[0]:L0_LR_ROWS[1], :],
                                wt_ref[L0_GATE_ROWS[0]:L0_GATE_ROWS[1], :],
                                jnp.zeros((ZG_W - 48, D_MODEL), F32)], axis=0)
        w_ref[:, 2 * ZA_W:2 * ZA_W + ZG_W] = tail.T.astype(BF16)

    h = _rms(x_ref[...], g_ref[...]) * (1.0 + sc_ref[...]) + sh_ref[...]
    hb = h.astype(BF16)
    za_ref[...] = _dot(hb, w_ref[:, 0:ZA_W])
    zb_ref[...] = _dot(hb, w_ref[:, ZA_W:2 * ZA_W])
    zg_ref[...] = _dot(hb, w_ref[:, 2 * ZA_W:2 * ZA_W + ZG_W])


def _proj_l0(x, g, mod, first_row, groups, w_t, tm):
    t = x.shape[0]
    per = (t // tm) // groups
    row = lambda i: (i, 0)
    fixed = lambda i: (0, 0)
    return pl.pallas_call(
        _p0_kernel,
        out_shape=(jax.ShapeDtypeStruct((t, ZA_W), F32),
                   jax.ShapeDtypeStruct((t, ZA_W), F32),
                   jax.ShapeDtypeStruct((t, ZG_W), F32)),
        grid=(t // tm,),
        in_specs=[pl.BlockSpec((tm, D_MODEL), row),
                  pl.BlockSpec((1, D_MODEL), fixed),
                  _mod_spec(MOD_SCALE1, first_row, per),
                  _mod_spec(MOD_SHIFT1, first_row, per),
                  _resident(w_t.shape)],
        out_specs=(pl.BlockSpec((tm, ZA_W), row),
                   pl.BlockSpec((tm, ZA_W), row),
                   pl.BlockSpec((tm, ZG_W), row)),
        scratch_shapes=[pltpu.VMEM((D_MODEL, 2 * ZA_W + ZG_W), BF16)],
        compiler_params=_cparams(("arbitrary",)),
        name="proj_l0",
    )(x, g, mod, mod, w_t)


def _head_lane_mask(shape, h):
    lane = lax.broadcasted_iota(jnp.int32, shape, len(shape) - 1)
    return (lane >= h * 64) & (lane < (h + 1) * 64)


def _head_row(h):
    return jnp.where(_head_lane_mask((1, HK), h), 1.0, 0.0).astype(BF16)


def _gla_prepare(za_ref, la_ref, r0, rev, tri):
    L = RCHUNK
    rows = pl.ds(r0, L)
    b = _tri_dot(tri, la_ref[rows, :])
    mid = L // 2
    ref = b[mid:mid + 1, :]
    b_end = b[0:1, :] if rev else b[L - 1:L, :]
    q = za_ref[rows, 0:HK]
    k = za_ref[rows, HK:2 * HK]
    v = za_ref[rows, 2 * HK:2 * HK + HV].astype(BF16)
    qe = (q * (jnp.exp(b - ref) * (A_DK ** -0.5))).astype(BF16)
    ke = (k * jnp.exp(ref - b)).astype(BF16)
    qm = jnp.concatenate([qe * _head_row(h) for h in range(HEADS)], axis=0)
    rt = _dot_tn(v, ke)
    upd = None
    for h in range(HEADS):
        blk = rt[h * 128:(h + 1) * 128, :]
        blk = jnp.where(_head_lane_mask(blk.shape, h), blk, 0.0)
        upd = blk if upd is None else upd + blk
    return dict(rows=rows, qm=qm, ke=ke, v=v, upd=upd * jnp.exp(b_end - ref),
                decay=jnp.exp(b_end), eref=jnp.exp(ref))


def _gla_apply(items):
    L = RCHUNK
    sts = [st_ref[...] for _, st_ref, _, _ in items]
    o1s = []
    for (p, _, _, _), st in zip(items, sts):
        rhs = jnp.concatenate([p["ke"], (st * p["eref"]).astype(BF16)], axis=0)
        o1s.append(_dot_nt(p["qm"], rhs))
    for h in range(HEADS):
        for (p, _, o_ref, mask), o1 in zip(items, o1s):
            blk = o1[h * L:(h + 1) * L, :]
            sc = jnp.where(mask, blk[:, 0:L], 0.0).astype(BF16)
            o_ref[p["rows"], h * 128:(h + 1) * 128] = (
                blk[:, L:L + 128] + _dot(sc, p["v"][:, h * 128:(h + 1) * 128]))
    for (p, st_ref, _, _), st in zip(items, sts):
        st_ref[...] = st * p["decay"] + p["upd"]


def _gla_exact_scan(za_ref, la_ref, st_ref, o_ref, b_ref, n, rev, st_init):
    L = RCHUNK
    nc = n // L
    m_lo, m_up = _tri_masks(L)
    tri = jnp.where(m_up if rev else m_lo, 1.0, 0.0).astype(BF16)
    s_idx = lax.broadcasted_iota(jnp.int32, (L, 1), 0)
    krow = lax.broadcasted_iota(jnp.int32, (HK, 128), 0) // 64
    kcol = lax.broadcasted_iota(jnp.int32, (HK, 128), 1)
    head_sum = jnp.where(krow == kcol, 1.0, 0.0)
    hsub = lax.broadcasted_iota(jnp.int32, (8, HK), 0)
    hlane = lax.broadcasted_iota(jnp.int32, (8, HK), 1) // 64
    head_rows = jnp.where(hsub == hlane, 1.0, 0.0)
    st_ref[...] = st_init

    def chunk_body(ci, carry):
        c = nc - 1 - ci if rev else ci
        r0 = pl.multiple_of(c * L, L)
        rows = pl.ds(r0, L)
        b = _tri_dot(tri, la_ref[rows, :])
        b_ref[...] = b
        b_end = b[0:1, :] if rev else b[L - 1:L, :]
        k = za_ref[rows, HK:2 * HK]
        vb = za_ref[rows, 2 * HK:2 * HK + HV].astype(BF16)
        st = st_ref[...]
        stb = st.astype(BF16)

        def row_body(t, carry2):
            b_t = b_ref[pl.ds(t, 1), :]
            q_t = za_ref[pl.ds(r0 + t, 1), 0:HK] * (A_DK ** -0.5)
            valid = (s_idx >= t) if rev else (s_idx <= t)
            w = jnp.exp(jnp.minimum(b_t - b_ref[...], 0.0))
            p = jnp.where(valid, (q_t * k) * w, 0.0)
            sc = jnp.dot(p, head_sum, precision=lax.Precision.HIGHEST,
                         preferred_element_type=F32)
            pv = _dot_tn(sc.astype(BF16), vb)
            qm = (head_rows * (q_t * jnp.exp(b_t))).astype(BF16)
            inter = _dot_nt(qm, stb)
            o_ref[pl.ds(r0 + t, 1), :] = jnp.concatenate(
                [pv[h:h + 1, h * 128:(h + 1) * 128] + inter[h:h + 1, :] for h in range(HEADS)],
                axis=1)
            return carry2

        lax.fori_loop(0, L, row_body, 0)
        kd = (k * jnp.exp(b_end - b)).astype(BF16)
        rt = _dot_tn(vb, kd)
        upd = None
        for h in range(HEADS):
            blk = rt[h * 128:(h + 1) * 128, :]
            blk = jnp.where(_head_lane_mask(blk.shape, h), blk, 0.0)
            upd = blk if upd is None else upd + blk
        st_ref[...] = st * jnp.exp(b_end) + upd
        return carry

    lax.fori_loop(0, nc, chunk_body, 0)


T_M, T_MOUT, T_G, T_SRCW, T_MPREV, T_CARRY, T_USED = 0, 8, 16, 24, 32, 40, 48


def _lane_scan(x, op, fill, is_fwd, pos, n):
    k = 1
    while k < RCHUNK:
        pf = jnp.where(pos >= k, pltpu.roll(x, k, axis=1), fill)
        sf = jnp.where(pos < RCHUNK - k, pltpu.roll(x, n - k, axis=1), fill)
        x = op(x, jnp.where(is_fwd, pf, sf))
        k *= 2
    return x


def _mlstm_gate_tables(g, m0, gt_ref, tab_ref, col_ref, n):
    L = RCHUNK
    nc = n // L
    for c in range(nc):
        gt_ref[:, c * L:(c + 1) * L] = g[c * L:(c + 1) * L, :].T
        tab_ref[c, T_USED:, :] = jnp.zeros((ZG_W - T_USED, L), F32)
    gi = gt_ref[GATE_COL:GATE_COL + 8, :]
    lf = gt_ref[GATE_COL + 8:GATE_COL + 16, :]
    is_fwd = lax.broadcasted_iota(jnp.int32, (8, n), 0) < HEADS
    pos = lax.broadcasted_iota(jnp.int32, (8, n), 1) % L
    b = _lane_scan(lf, jnp.add, 0.0, is_fwd, pos, n)
    gg = gi - b
    cmax = _lane_scan(gg, jnp.maximum, -jnp.inf, is_fwd, pos, n)
    fwd_l = lax.broadcasted_iota(jnp.int32, (8, L), 0) < HEADS
    mp = m0
    for j in range(nc):
        cf, cb = j, nc - 1 - j

        def pick(x):
            return jnp.where(fwd_l, x[:, cf * L:(cf + 1) * L], x[:, cb * L:(cb + 1) * L])

        gj, bj, cj, lj = pick(gg), pick(b), pick(cmax), pick(lf)
        mx = jnp.maximum(mp, jnp.max(gj, axis=1, keepdims=True))
        mj = jnp.maximum(cj, mp)
        blocks = (mj, bj + mj, gj, jnp.exp(gj - mx),
                  jnp.broadcast_to(mp, (8, L)), jnp.broadcast_to(jnp.exp(mp - mx), (8, L)))
        for t, val in enumerate(blocks):
            tab_ref[cf, 8 * t:8 * t + HEADS, :] = val[0:HEADS]
            tab_ref[cb, 8 * t + HEADS:8 * t + 8, :] = val[HEADS:8]
        mp = jnp.sum(lj, axis=1, keepdims=True) + mx
    for c in range(nc):
        col_ref[c * L:(c + 1) * L, :] = tab_ref[c].T
    return mp


def _mlstm_prepare(zb_ref, tab_ref, col_ref, c, d, mask):
    L = RCHUNK
    rows = pl.ds(pl.multiple_of(c * L, L), L)
    qb = zb_ref[rows, 0:HK].astype(BF16)
    kb = (zb_ref[rows, HK:2 * HK] * (B_DK ** -0.5)).astype(BF16)
    v = zb_ref[rows, 2 * HK:2 * HK + HV]
    vb = v.astype(BF16)
    vt = v.T
    qm = jnp.concatenate([qb * _head_row(h) for h in range(HEADS)], axis=0)
    s = _dot_nt(qm, kb)
    ones = jnp.ones((L, 128), BF16)
    pv, w_inter, e_negm = [], [], []
    upd = None
    carry = None
    for h in range(HEADS):
        r = 4 * d + h
        m_rep = jnp.broadcast_to(col_ref[rows, T_M + r:T_M + r + 1], (L, 128))
        mout_rep = jnp.broadcast_to(col_ref[rows, T_MOUT + r:T_MOUT + r + 1], (L, 128))
        g_row = tab_ref[c, T_G + r:T_G + r + 1, :]
        w = jnp.exp(jnp.where(mask, g_row - m_rep, -jnp.inf))
        qk = (s[h * L:(h + 1) * L, :] * w).astype(BF16)
        v1 = jnp.concatenate([vb[:, h * 128:(h + 1) * 128], ones], axis=1)
        pv.append(_dot(qk, v1))
        w_inter.append(jnp.exp(tab_ref[c, T_MPREV + r:T_MPREV + r + 1, :] - m_rep))
        e_negm.append(jnp.exp(-mout_rep))
        sw = tab_ref[c, T_SRCW + r:T_SRCW + r + 1, :]
        lhs = jnp.concatenate([vt[h * 128:(h + 1) * 128, :] * sw, jnp.broadcast_to(sw, (128, L))],
                              axis=0).astype(BF16)
        blk = _dot(lhs, kb)
        blk = jnp.where(_head_lane_mask(blk.shape, h), blk, 0.0)
        upd = blk if upd is None else upd + blk
        cr = tab_ref[c, T_CARRY + r:T_CARRY + r + 1, :]
        cr = jnp.where(_head_lane_mask((1, HK), h), jnp.concatenate([cr, cr], axis=1), 0.0)
        carry = cr if carry is None else carry + cr
    return dict(rows=rows, qm=qm, pv=pv, w_inter=w_inter, e_negm=e_negm, upd=upd, carry=carry)


def _mlstm_apply(items):
    L = RCHUNK
    sts = [st_ref[...] for _, st_ref, _ in items]
    a_all = [_dot_nt(p["qm"], st.astype(BF16)) for (p, _, _), st in zip(items, sts)]
    for h in range(HEADS):
        for (p, _, h_ref), a in zip(items, a_all):
            blk = a[h * L:(h + 1) * L, :]
            pv, w, e = p["pv"][h], p["w_inter"][h], p["e_negm"][h]
            num = w * blk[:, 0:128] + pv[:, 0:128]
            den = w * blk[:, 128:256] + pv[:, 128:256]
            h_ref[p["rows"], h * 128:(h + 1) * 128] = num / jnp.maximum(jnp.abs(den), e)
    for (p, st_ref, _), st in zip(items, sts):
        st_ref[...] = p["carry"] * st + p["upd"]


def _rec_kernel(*refs, n, has_init):
    it = iter(refs)

    def take(k):
        return [next(it) for _ in range(k)]

    za_all, zb_all, zg_all, zgn_all = take(4)
    wgf_ref, wgb_ref, bgf_ref, bgb_ref, gna_ref, bg_ref, gnb_ref = take(7)
    if has_init:
        init_all = take(7)
    out_all = take(8)
    (gt_all,) = take(1)
    set_a_all = take(6)
    set_b_all = take(6)
    work_all = take(9)
    L = RCHUNK
    nc = n // L
    step = pl.program_id(0)
    elems = []
    for e in range(za_all.shape[0]):
        el = dict(za=za_all.at[e], zb=zb_all.at[e], zg=zg_all.at[e], zgn=zgn_all.at[e],
                  gt=gt_all.at[e], set_a=[r.at[e] for r in set_a_all],
                  set_b=[r.at[e] for r in set_b_all])
        (el["oa"], el["sf"], el["sb"], el["om"], el["cf"], el["cb"], el["n_out"],
         el["m_out"]) = [r.at[e] for r in out_all]
        (el["of"], el["ob"], el["hf"], el["hb"], el["gsf"], el["gsb"], el["msf"],
         el["msb"], el["b_exact"]) = [r.at[e] for r in work_all]
        if has_init:
            el["init"] = [r.at[e] for r in init_all]
        elems.append(el)

    def fill(gate_ref, m0, bufs, gt_ref):
        laf_ref, lab_ref, tab_ref, col_ref, mf_ref, span_ref = bufs
        zg = gate_ref[...]
        laf = _log_sigmoid(_dot_hl(zg, wgf_ref[...]) + bgf_ref[...]) / A_GATE_TEMP
        lab = _log_sigmoid(_dot_hl(zg, wgb_ref[...]) + bgb_ref[...]) / A_GATE_TEMP
        laf_ref[...] = laf
        lab_ref[...] = lab
        span = None
        for la in (laf, lab):
            for c in range(nc):
                tot = -jnp.sum(la[c * L:(c + 1) * L, :], axis=0, keepdims=True)
                span = tot if span is None else jnp.maximum(span, tot)
        span_ref[...] = jnp.broadcast_to(jnp.max(span, axis=1, keepdims=True), (8, 128))
        g = zg + bg_ref[...]
        lane = lax.broadcasted_iota(jnp.int32, g.shape, 1)
        is_forget = (lane >= GATE_COL + 8) & (lane < GATE_COL + 16)
        g = jnp.where(is_forget, _log_sigmoid(g), g)
        m_fin = _mlstm_gate_tables(g, m0, gt_ref, tab_ref, col_ref, n)
        mf_ref[...] = jnp.broadcast_to(m_fin, (8, 128))

    for el in elems:
        if has_init:
            sf0_ref, sb0_ref, cf0_ref, cb0_ref, n0_ref, m0_ref, m0n_ref = el["init"]
            el["m0"], el["m0_next"] = m0_ref[:, 0:1], m0n_ref[:, 0:1]
            el["gsf"][...] = sf0_ref[...]
            el["gsb"][...] = sb0_ref[...]
            el["msf"][...] = jnp.concatenate(
                [cf0_ref[...], jnp.broadcast_to(n0_ref[0:1, :], (128, HK))], axis=0)
            el["msb"][...] = jnp.concatenate(
                [cb0_ref[...], jnp.broadcast_to(n0_ref[1:2, :], (128, HK))], axis=0)
        else:
            el["m0"] = el["m0_next"] = jnp.zeros((8, 1), F32)
            for name in ("gsf", "gsb", "msf", "msb"):
                el[name][...] = jnp.zeros(el[name].shape, F32)

    @pl.when(step == 0)
    def _():
        for el in elems:
            fill(el["zg"], el["m0"], el["set_a"], el["gt"])

    m_lo, m_up = _tri_masks(L)
    tri_lo = jnp.where(m_lo, 1.0, 0.0).astype(BF16)
    tri_up = jnp.where(m_up, 1.0, 0.0).astype(BF16)
    gna = gna_ref[...]
    gnb = gnb_ref[...]

    def gla_out(el, rows, h):
        cols = slice(h * 128, (h + 1) * 128)
        gcols = slice(2 * HK + HV + h * 128, 2 * HK + HV + (h + 1) * 128)
        o = el["of"][rows, cols] + el["ob"][rows, cols]
        el["oa"][rows, cols] = (_rms(o, gna) * _silu(el["za"][rows, gcols])).astype(BF16)

    def out_body(c, carry):
        rows = pl.ds(pl.multiple_of(c * L, L), L)
        for h in range(HEADS):
            cols = slice(h * 128, (h + 1) * 128)
            gcols = slice(2 * HK + HV + h * 128, 2 * HK + HV + (h + 1) * 128)
            for el in elems:
                gla_out(el, rows, h)
                hh = el["hf"][rows, cols] + el["hb"][rows, cols]
                el["om"][rows, cols] = (
                    _sigmoid(el["zb"][rows, gcols]) * _rms(hh, gnb)).astype(BF16)
        return carry

    def run(cur, nxt):
        for el in elems:
            fill(el["zgn"], el["m0_next"], el[nxt], el["gt"])

        def scan_body(gidx, carry):
            waves = []
            for w in range(2):
                c = 2 * gidx + w
                cb = nc - 1 - c
                gla_items, mlstm_items = [], []
                for el in elems:
                    laf_ref, lab_ref, tab_ref, col_ref = el[cur][0:4]
                    gla_items += [
                        (_gla_prepare(el["za"], laf_ref, pl.multiple_of(c * L, L), False, tri_lo),
                         el["gsf"], el["of"], m_lo),
                        (_gla_prepare(el["za"], lab_ref, pl.multiple_of(cb * L, L), True, tri_up),
                         el["gsb"], el["ob"], m_up)]
                    mlstm_items += [
                        (_mlstm_prepare(el["zb"], tab_ref, col_ref, c, 0, m_lo), el["msf"], el["hf"]),
                        (_mlstm_prepare(el["zb"], tab_ref, col_ref, cb, 1, m_up), el["msb"], el["hb"])]
                waves.append((gla_items, mlstm_items))
            for gla_items, mlstm_items in waves:
                _gla_apply(gla_items)
                _mlstm_apply(mlstm_items)
            return carry

        if nc == 2:
            scan_body(0, 0)
        else:
            lax.fori_loop(0, nc // 2, scan_body, 0)
        if nc == 2:
            for c in range(nc):
                out_body(c, 0)
        else:
            lax.fori_loop(0, nc, out_body, 0)
        for el in elems:
            el["m_out"][...] = el[cur][4][...]
        for el in elems:
            laf_ref, lab_ref = el[cur][0:2]
            too_fast = el[cur][5][0, 0] > GLA_SAFE_SPAN

            @pl.when(too_fast)
            def _(el=el, laf_ref=laf_ref, lab_ref=lab_ref):
                zero = jnp.zeros((128, HK), F32)
                init_f = el["init"][0][...] if has_init else zero
                init_b = el["init"][1][...] if has_init else zero
                _gla_exact_scan(el["za"], laf_ref, el["gsf"], el["of"], el["b_exact"], n, False, init_f)
                _gla_exact_scan(el["za"], lab_ref, el["gsb"], el["ob"], el["b_exact"], n, True, init_b)

                def redo_out(c, carry):
                    rows = pl.ds(pl.multiple_of(c * L, L), L)
                    for h in range(HEADS):
                        gla_out(el, rows, h)
                    return carry

                lax.fori_loop(0, nc, redo_out, 0)

    @pl.when(step % 2 == 0)
    def _():
        run("set_a", "set_b")

    @pl.when(step % 2 == 1)
    def _():
        run("set_b", "set_a")

    for el in elems:
        gsf_t = el["gsf"][...].T
        gsb_t = el["gsb"][...].T
        msf = el["msf"][...]
        msb = el["msb"][...]
        ctf_t = msf[0:128, :].T
        ctb_t = msb[0:128, :].T
        for h in range(HEADS):
            rows = slice(h * 64, (h + 1) * 64)
            el["sf"][h] = gsf_t[rows, :]
            el["sb"][h] = gsb_t[rows, :]
            el["cf"][h] = ctf_t[rows, :]
            el["cb"][h] = ctb_t[rows, :]
        el["n_out"][...] = jnp.concatenate(
            [msf[128:129, :], msb[128:129, :], jnp.zeros((6, HK), F32)], axis=0)


def _recurrent(za, zb, zg, wgf, wgb, bgf, bgb, gna, bg, gnb, init, bt):
    bsz, n, _ = za.shape
    has_init = init is not None
    nc = n // RCHUNK
    steps = bsz // bt
    per_b3 = lambda b: (b, 0, 0)
    next_b3 = lambda b: (jnp.minimum(b + 1, steps - 1), 0, 0)
    per_b4 = lambda b: (b, 0, 0, 0)
    fixed = lambda b: (0, 0)
    in_specs = [pl.BlockSpec((bt, n, ZA_W), per_b3),
                pl.BlockSpec((bt, n, ZA_W), per_b3),
                pl.BlockSpec((bt, n, ZG_W), per_b3),
                pl.BlockSpec((bt, n, ZG_W), next_b3),
                pl.BlockSpec((ZG_W, HK), fixed),
                pl.BlockSpec((ZG_W, HK), fixed),
                pl.BlockSpec((1, HK), fixed),
                pl.BlockSpec((1, HK), fixed),
                pl.BlockSpec((1, 128), fixed),
                pl.BlockSpec((1, ZG_W), fixed),
                pl.BlockSpec((1, 128), fixed)]
    args = [za, zb, zg, zg, wgf, wgb, bgf, bgb, gna, bg, gnb]
    if has_init:
        sf0, sb0, cf0, cb0, n0, m0 = init
        in_specs += [pl.BlockSpec((bt, 128, HK), per_b3)] * 4
        in_specs += [pl.BlockSpec((bt, 8, HK), per_b3), pl.BlockSpec((bt, 8, 128), per_b3),
                     pl.BlockSpec((bt, 8, 128), next_b3)]
        args += [sf0, sb0, cf0, cb0, n0, m0, m0]
    state = jax.ShapeDtypeStruct((bsz, HEADS, 64, 128), F32)
    state_spec = pl.BlockSpec((bt, HEADS, 64, 128), per_b4)

    def vmem(*shape):
        return pltpu.VMEM((bt,) + shape, F32)

    gate_set = [vmem(n, HK), vmem(n, HK), vmem(nc, ZG_W, RCHUNK), vmem(n, ZG_W), vmem(8, 128),
                vmem(8, 128)]
    return pl.pallas_call(
        functools.partial(_rec_kernel, n=n, has_init=has_init),
        out_shape=(jax.ShapeDtypeStruct((bsz, n, HV), BF16), state, state,
                   jax.ShapeDtypeStruct((bsz, n, HV), BF16), state, state,
                   jax.ShapeDtypeStruct((bsz, 8, HK), F32),
                   jax.ShapeDtypeStruct((bsz, 8, 128), F32)),
        grid=(steps,),
        in_specs=in_specs,
        out_specs=(pl.BlockSpec((bt, n, HV), per_b3), state_spec, state_spec,
                   pl.BlockSpec((bt, n, HV), per_b3), state_spec, state_spec,
                   pl.BlockSpec((bt, 8, HK), per_b3),
                   pl.BlockSpec((bt, 8, 128), per_b3)),
        scratch_shapes=([vmem(ZG_W, n)] + gate_set + gate_set + [vmem(n, HV)] * 4
                        + [vmem(128, HK)] * 2 + [vmem(2 * 128, HK)] * 2 + [vmem(RCHUNK, HK)]),
        compiler_params=_cparams(("arbitrary",)),
        name="recurrent",
    )(*args)


def _mix_residual(parts, wo_ref, x_ref, gt1_ref, gpost1_ref, gpre2_ref, sc2_ref, sh2_ref):
    m = None
    off = 0
    for p in parts:
        kp = p.shape[1]
        t = _dot(p[...], wo_ref[off:off + kp, :])
        m = t if m is None else m + t
        off += kp
    x1 = x_ref[...] + gt1_ref[...] * _rms(m, gpost1_ref[...])
    hb = (_rms(x1, gpre2_ref[...]) * (1.0 + sc2_ref[...]) + sh2_ref[...]).astype(BF16)
    return x1, hb


def _ffn_gated_up(hb, wu_ref, cw_ref, cb_ref, act_ref, seq, tf):
    tm = hb.shape[0]
    pos = lax.broadcasted_iota(jnp.int32, (tm, 1), 0) % seq
    first = pos == 0
    last = pos == seq - 1

    def conv(u, cols):
        prev = jnp.where(first, 0.0, pltpu.roll(u, 1, axis=0))
        nxt = jnp.where(last, 0.0, pltpu.roll(u, tm - 1, axis=0))
        return (cw_ref[0:1, cols] * prev + cw_ref[1:2, cols] * u + cw_ref[2:3, cols] * nxt
                + cb_ref[:, cols])

    for j in range(D_FF // tf):
        ca = slice(j * tf, (j + 1) * tf)
        cg = slice(D_FF + j * tf, D_FF + (j + 1) * tf)
        a = conv(_dot(hb, wu_ref[:, ca]), ca)
        g = conv(_dot(hb, wu_ref[:, cg]), cg)
        act_ref[:, ca] = (_silu(g) * a).astype(BF16)


def _mix_ffn_kernel(*refs, n_parts, seq, tf):
    parts = refs[:n_parts]
    (wo_ref, x_ref, gt1_ref, gpost1_ref, gpre2_ref, sc2_ref, sh2_ref,
     wu_ref, cw_ref, cb_ref, wd_ref, gt2_ref, gpost2_ref, o_ref, act_ref) = refs[n_parts:]
    x1, hb = _mix_residual(parts, wo_ref, x_ref, gt1_ref, gpost1_ref, gpre2_ref, sc2_ref, sh2_ref)
    o_ref[...] = x1
    _ffn_gated_up(hb, wu_ref, cw_ref, cb_ref, act_ref, seq, tf)
    y = _dot(act_ref[...], wd_ref[...])
    o_ref[...] = o_ref[...] + gt2_ref[...] * _rms(y, gpost2_ref[...])


def _resident(shape):
    return pl.BlockSpec(shape, lambda *_: (0,) * len(shape), pipeline_mode=pl.Buffered(1))


def _mix_ffn(parts, w_out, x, mod, first_row, groups, gpost1, gpre2, w_up, conv_w, conv_b, w_down,
             gpost2, tm, seq, tf=256):
    t = x.shape[0]
    per = (t // tm) // groups
    row = lambda i: (i, 0)
    vec = _resident((1, D_MODEL))
    in_specs = [pl.BlockSpec((tm, p.shape[1]), row) for p in parts]
    in_specs += [_resident(w_out.shape), pl.BlockSpec((tm, D_MODEL), row),
                 _mod_spec(MOD_GATE1, first_row, per), vec, vec,
                 _mod_spec(MOD_SCALE2, first_row, per), _mod_spec(MOD_SHIFT2, first_row, per),
                 _resident(w_up.shape), _resident(conv_w.shape), _resident(conv_b.shape),
                 _resident(w_down.shape), _mod_spec(MOD_GATE2, first_row, per), vec]
    return pl.pallas_call(
        functools.partial(_mix_ffn_kernel, n_parts=len(parts), seq=seq, tf=tf),
        out_shape=jax.ShapeDtypeStruct((t, D_MODEL), F32),
        grid=(t // tm,),
        in_specs=in_specs,
        out_specs=pl.BlockSpec((tm, D_MODEL), row),
        scratch_shapes=[pltpu.VMEM((tm, D_FF), BF16)],
        compiler_params=_cparams(("arbitrary",)),
        name="mix_ffn",
    )(*parts, w_out, x, mod, gpost1, gpre2, mod, mod, w_up, conv_w, conv_b, w_down, mod, gpost2)


def _rope(x, cos, sin_signed):
    lane = lax.broadcasted_iota(jnp.int32, x.shape, 1)
    lower = (lane % 32) < 16
    partner = jnp.where(lower, pltpu.roll(x, 128 - 16, axis=1), pltpu.roll(x, 16, axis=1))
    return x * cos + partner * sin_signed


def _p1_kernel(*refs, rope):
    if rope:
        (x_ref, g_ref, sc_ref, sh_ref, w_ref, gq_ref, wq_ref, gkv_ref, cos_ref, sin_ref,
         q_ref, ckv_ref, kr_ref) = refs
    else:
        (x_ref, g_ref, sc_ref, sh_ref, w_ref, gq_ref, wq_ref, gkv_ref,
         q_ref, ckv_ref, kr_ref) = refs
    h = _rms(x_ref[...], g_ref[...]) * (1.0 + sc_ref[...]) + sh_ref[...]
    z = _dot(h.astype(BF16), w_ref[...])
    qa = _rms(z[:, 0:C_Q_RANK], gq_ref[...])
    q = _dot(qa.astype(BF16), wq_ref[...])
    ckv_ref[...] = _rms(z[:, C_Q_RANK:C_Q_RANK + C_KV_RANK], gkv_ref[...])
    kr = z[:, C_Q_RANK + C_KV_RANK:L1_IN_PAD]
    if rope:
        cos = cos_ref[...]
        sin = sin_ref[...]
        kr = _rope(kr, cos, sin)
    kr_ref[...] = kr[:, 0:C_ROPE]
    for hd in range(C_HEADS):
        c0 = hd * QCAT
        q_ref[:, c0:c0 + 128] = q[:, c0:c0 + 128].astype(q_ref.dtype)
        qr = q[:, c0 + 128:c0 + 256]
        if rope:
            qr = _rope(qr, cos, sin)
        q_ref[:, c0 + 128:c0 + 256] = qr.astype(q_ref.dtype)


def _proj_l1(x, g, mod, first_row, groups, w, gq, wq, gkv, tables, tm):
    t = x.shape[0]
    per = (t // tm) // groups
    row = lambda i: (i, 0)
    fixed = lambda i: (0, 0)
    in_specs = [pl.BlockSpec((tm, D_MODEL), row),
                pl.BlockSpec((1, D_MODEL), fixed),
                _mod_spec(MOD_SCALE1, first_row, per),
                _mod_spec(MOD_SHIFT1, first_row, per),
                pl.BlockSpec(w.shape, fixed),
                pl.BlockSpec((1, C_Q_RANK), fixed),
                pl.BlockSpec(wq.shape, fixed),
                pl.BlockSpec((1, C_KV_RANK), fixed)]
    args = [x, g, mod, mod, w, gq, wq, gkv]
    rope = tables is not None
    if rope:
        n = tables[0].shape[0]
        pos = lambda i: (i % (n // tm), 0)
        in_specs += [pl.BlockSpec((tm, 128), pos)] * 2
        args += list(tables)
    return pl.pallas_call(
        functools.partial(_p1_kernel, rope=rope),
        out_shape=(jax.ShapeDtypeStruct((t, C_HEADS * QCAT), BF16),
                   jax.ShapeDtypeStruct((t, C_KV_RANK), F32),
                   jax.ShapeDtypeStruct((t, C_ROPE), F32)),
        grid=(t // tm,),
        in_specs=in_specs,
        out_specs=(pl.BlockSpec((tm, C_HEADS * QCAT), row),
                   pl.BlockSpec((tm, C_KV_RANK), row),
                   pl.BlockSpec((tm, C_ROPE), row)),
        compiler_params=_cparams(("arbitrary",)),
        name="proj_l1",
    )(*args)


def _attn_kernel(*refs, n_parts, scale):
    q_ref = refs[0]
    kv_parts = refs[1:1 + 2 * n_parts]
    w_ref, o_ref, k_sc, v_sc = refs[1 + 2 * n_parts:]
    bt = q_ref.shape[0]

    @pl.when(pl.program_id(1) == 0)
    def _():
        off = 0
        for p in range(n_parts):
            ckv_ref, kr_ref = kv_parts[2 * p], kv_parts[2 * p + 1]
            nk = ckv_ref.shape[1]
            ones = jnp.ones((nk, C_DV), BF16)
            for e in range(bt):
                kv = _dot(ckv_ref[e].astype(BF16), w_ref[...])
                kr = kr_ref[e].astype(BF16)
                kr2 = jnp.concatenate([kr, kr], axis=1)
                for hd in range(C_HEADS):
                    v0 = C_HEADS * C_NOPE + hd * C_DV
                    k_sc[e, hd, off:off + nk, 0:128] = kv[:, hd * 128:(hd + 1) * 128].astype(BF16)
                    k_sc[e, hd, off:off + nk, 128:256] = kr2
                    v_sc[e, hd, off:off + nk, 0:C_DV] = kv[:, v0:v0 + C_DV].astype(BF16)
                    v_sc[e, hd, off:off + nk, C_DV:2 * C_DV] = ones
            off += nk

    for hd in range(C_HEADS):
        for e in range(bt):
            t = _dot_nt(q_ref[e, :, hd * QCAT:(hd + 1) * QCAT], k_sc[e, hd]) * (scale * LOG2E)
            p = jnp.exp2(t - jnp.max(t, axis=1, keepdims=True))
            ov = _dot(p.astype(BF16), v_sc[e, hd])
            o_ref[e, :, hd * C_DV:(hd + 1) * C_DV] = (
                ov[:, 0:C_DV] / ov[:, C_DV:2 * C_DV]).astype(o_ref.dtype)


def _attention(q, kv_parts, w_kvb, tq, bt):
    bsz, n, _ = q.shape
    nk = sum(p[0].shape[1] for p in kv_parts)
    in_specs = [pl.BlockSpec((bt, tq, C_HEADS * QCAT), lambda b, i: (b, i, 0))]
    args = [q]
    for ckv, kr in kv_parts:
        in_specs.append(pl.BlockSpec((bt, ckv.shape[1], C_KV_RANK), lambda b, i: (b, 0, 0)))
        in_specs.append(pl.BlockSpec((bt, kr.shape[1], C_ROPE), lambda b, i: (b, 0, 0)))
        args += [ckv, kr]
    in_specs.append(pl.BlockSpec(w_kvb.shape, lambda b, i: (0, 0)))
    args.append(w_kvb)
    return pl.pallas_call(
        functools.partial(_attn_kernel, n_parts=len(kv_parts),
                          scale=(C_NOPE + C_ROPE) ** -0.5),
        out_shape=jax.ShapeDtypeStruct((bsz, n, C_HEADS * C_DV), BF16),
        grid=(bsz // bt, n // tq),
        in_specs=in_specs,
        out_specs=pl.BlockSpec((bt, tq, C_HEADS * C_DV), lambda b, i: (b, i, 0)),
        scratch_shapes=[pltpu.VMEM((bt, C_HEADS, nk, QCAT), BF16),
                        pltpu.VMEM((bt, C_HEADS, nk, 2 * C_DV), BF16)],
        compiler_params=_cparams(("arbitrary", "arbitrary")),
        name="mla_attention",
    )(*args)


def _l1_wq_layout(w_qb):
    blocks = []
    zero = jnp.zeros((w_qb.shape[0], C_ROPE), w_qb.dtype)
    for hd in range(C_HEADS):
        base = hd * (C_NOPE + C_ROPE)
        nope = w_qb[:, base:base + C_NOPE]
        rope = w_qb[:, base + C_NOPE:base + C_NOPE + C_ROPE]
        blocks += [nope, rope, zero] if hd % 2 == 0 else [nope, zero, rope]
    return jnp.concatenate(blocks, axis=1).astype(BF16)


def _l1_wkvb_layout(w_kvb):
    w = w_kvb.reshape(C_KV_RANK, C_HEADS, C_NOPE + C_DV)
    k = w[:, :, :C_NOPE].reshape(C_KV_RANK, C_HEADS * C_NOPE)
    v = w[:, :, C_NOPE:].reshape(C_KV_RANK, C_HEADS * C_DV)
    return jnp.concatenate([k, v], axis=1).astype(BF16)


def _rope_tables(n):
    pos = np.arange(n)
    row = (pos // GRID_W).astype(np.float64)
    col = (pos % GRID_W).astype(np.float64)
    half = C_ROPE // 2
    inv = 1.0 / (ROPE_THETA ** (np.arange(0, half, 2, dtype=np.float64) / half))
    ang_r = row[:, None] * inv[None, :]
    ang_c = col[:, None] * inv[None, :]
    cos64 = np.concatenate([np.cos(ang_r)] * 2 + [np.cos(ang_c)] * 2, axis=1)
    sin64 = np.concatenate([-np.sin(ang_r), np.sin(ang_r), -np.sin(ang_c), np.sin(ang_c)], axis=1)
    return (jnp.asarray(np.concatenate([cos64] * 2, axis=1), F32),
            jnp.asarray(np.concatenate([sin64] * 2, axis=1), F32))


def _gate_weight(w, first_row):
    return jnp.pad(w, ((first_row, ZG_W - first_row - A_GATE_RANK), (0, 0)))


def _state_t(s):
    b = s.shape[0]
    return s.transpose(0, 3, 1, 2).reshape(b, 128, HK)


def _n_rows(n_fwd, n_bwd):
    b = n_fwd.shape[0]
    rows = jnp.zeros((b, 8, HK), F32)
    return rows.at[:, 0, :].set(n_fwd.reshape(b, HK)).at[:, 1, :].set(n_bwd.reshape(b, HK))


def _m_rows(m_fwd, m_bwd):
    m = jnp.concatenate([m_fwd, m_bwd], axis=1)
    return jnp.broadcast_to(m[:, :, None], (m.shape[0], 8, 128))


def kernel(x_prompt, x_sample, state_l0_gla_fwd, state_l0_gla_bwd, state_l0_mlstm_c_fwd, state_l0_mlstm_n_fwd, state_l0_mlstm_m_fwd, state_l0_mlstm_c_bwd, state_l0_mlstm_n_bwd, state_l0_mlstm_m_bwd, cache_l1_ckv, cache_l1_krope, c, c_ctx, l0_w_mod, l0_b_mod, l0_g_pre_mix, l0_g_post_mix, l0_g_pre_ffn, l0_g_post_ffn, l0_w_in, l0_gla_w_gate_f, l0_gla_b_gate_f, l0_gla_w_gate_b, l0_gla_b_gate_b, l0_gla_g_norm, l0_mlstm_b_gates, l0_mlstm_g_norm, l0_w_out, l0_ffn_w_up, l0_ffn_conv_w, l0_ffn_conv_b, l0_ffn_w_down, l1_w_mod, l1_b_mod, l1_g_pre_mix, l1_g_post_mix, l1_g_pre_ffn, l1_g_post_ffn, l1_w_in, l1_g_q_norm, l1_w_qb, l1_g_kv_norm, l1_w_kvb, l1_w_out, l1_ffn_w_up, l1_ffn_conv_w, l1_ffn_conv_b, l1_ffn_w_down):
    bp, sp, d = x_prompt.shape
    bs, ss, _ = x_sample.shape
    row = lambda v: v.reshape(1, -1)

    cond8 = jnp.concatenate([c_ctx[None, :], c, jnp.zeros((8 - 1 - bs, d), F32)], axis=0)
    mods = (_modulation(cond8, l0_w_mod, l0_b_mod), _modulation(cond8, l1_w_mod, l1_b_mod))

    paths = {
        "ctx": dict(x=x_prompt.reshape(bp * sp, d), b=bp, n=sp, first_row=0, groups=1, tm=512),
        "lat": dict(x=x_sample.reshape(bs * ss, d), b=bs, n=ss, first_row=1, groups=bs, tm=512),
    }

    w_in0 = l0_w_in.T
    wgf = _gate_weight(l0_gla_w_gate_f, 0)
    wgb = _gate_weight(l0_gla_w_gate_b, A_GATE_RANK)
    bg = jnp.pad(l0_mlstm_b_gates, (GATE_COL, ZG_W - GATE_COL - 16)).reshape(1, ZG_W)
    w_out0 = l0_w_out.astype(BF16)
    w_in1 = jnp.pad(l1_w_in, ((0, 0), (0, L1_IN_PAD - l1_w_in.shape[1]))).astype(BF16)
    w_q1 = _l1_wq_layout(l1_w_qb)
    w_kvb1 = _l1_wkvb_layout(l1_w_kvb)
    w_out1 = l1_w_out.astype(BF16)
    ffn = ((l0_ffn_w_up.astype(BF16), l0_ffn_conv_w, row(l0_ffn_conv_b), l0_ffn_w_down.astype(BF16)),
           (l1_ffn_w_up.astype(BF16), l1_ffn_conv_w, row(l1_ffn_conv_b), l1_ffn_w_down.astype(BF16)))
    norms = ((row(l0_g_pre_mix), row(l0_g_post_mix), row(l0_g_pre_ffn), row(l0_g_post_ffn)),
             (row(l1_g_pre_mix), row(l1_g_post_mix), row(l1_g_pre_ffn), row(l1_g_post_ffn)))

    gla_init = (_state_t(state_l0_gla_fwd), _state_t(state_l0_gla_bwd))
    mlstm_init = (_state_t(state_l0_mlstm_c_fwd), _state_t(state_l0_mlstm_c_bwd),
                  _n_rows(state_l0_mlstm_n_fwd, state_l0_mlstm_n_bwd),
                  _m_rows(state_l0_mlstm_m_fwd, state_l0_mlstm_m_bwd))
    tables = _rope_tables(ss)

    results = {}
    for name, p in paths.items():
        x, b, n, tm = p["x"], p["b"], p["n"], p["tm"]
        first_row, groups = p["first_row"], p["groups"]
        is_ctx = name == "ctx"
        ffn_tm = tm if is_ctx else n
        g_pre, g_post, g_pre2, g_post2 = norms[0]

        za, zb, zg = _proj_l0(x, g_pre, mods[0], first_row, groups, w_in0, tm)
        za, zb, zg = (t.reshape(b, n, -1) for t in (za, zb, zg))
        oa, sa_f, sa_b, om, c_f, c_b, n_rows, m_rows = _recurrent(
            za, zb, zg, wgf, wgb, row(l0_gla_b_gate_f), row(l0_gla_b_gate_b), row(l0_gla_g_norm),
            bg, row(l0_mlstm_g_norm), None if is_ctx else gla_init + mlstm_init,
            2 if is_ctx else 1)
        x = _mix_ffn([oa.reshape(b * n, HV), om.reshape(b * n, HV)], w_out0, x, mods[0], first_row,
                     groups, g_post, g_pre2, *ffn[0], g_post2, ffn_tm, n)

        g_pre, g_post, g_pre2, g_post2 = norms[1]
        q, ckv, kr = _proj_l1(x, g_pre, mods[1], first_row, groups, w_in1, row(l1_g_q_norm), w_q1,
                              row(l1_g_kv_norm), None if is_ctx else tables, tm)
        ckv3 = ckv.reshape(b, n, C_KV_RANK)
        kr3 = kr.reshape(b, n, C_ROPE)
        kv_parts = [(ckv3, kr3)] if is_ctx else [(cache_l1_ckv, cache_l1_krope), (ckv3, kr3)]
        o = _attention(q.reshape(b, n, -1), kv_parts, w_kvb1, 256, 4 if is_ctx else 1)
        x = _mix_ffn([o.reshape(b * n, C_HEADS * C_DV)], w_out1, x, mods[1], first_row,
                     groups, g_post, g_pre2, *ffn[1], g_post2, ffn_tm, n)

        results[name] = dict(y=x.reshape(b, n, d), gla=(sa_f, sa_b), c=(c_f, c_b),
                             n=n_rows, m=m_rows, ckv=ckv3, kr=kr3)

    r = results["ctx"]
    n_fwd = r["n"][:, 0, :].reshape(bp, B_HEADS, B_DK)
    n_bwd = r["n"][:, 1, :].reshape(bp, B_HEADS, B_DK)
    m_fwd = r["m"][:, 0:B_HEADS, 0]
    m_bwd = r["m"][:, B_HEADS:2 * B_HEADS, 0]
    return (r["y"], results["lat"]["y"], r["gla"][0], r["gla"][1],
            r["c"][0], n_fwd, m_fwd, r["c"][1], n_bwd, m_bwd, r["ckv"], r["kr"])
```

```python
import functools

import numpy as np
import jax
import jax.numpy as jnp
from jax import lax
from jax.experimental import pallas as pl
from jax.experimental.pallas import tpu as pltpu

F32 = jnp.float32
BF16 = jnp.bfloat16

D_MODEL = 1024
EPS = 1e-6
LOG2E = 1.4426950408889634
N_MOD = 6
GRID_W = 64
ROPE_THETA = 10000.0

A_HEADS, A_DK, A_DV, A_GATE_RANK, A_GATE_TEMP = 4, 64, 128, 16, 16.0
B_HEADS, B_DK, B_DV = 4, 64, 128
C_HEADS, C_Q_RANK, C_KV_RANK, C_NOPE, C_ROPE, C_DV = 8, 384, 256, 128, 64, 128
D_FF = 2816
CONV_W = 3

HEADS = 4
HK = HEADS * 64
HV = HEADS * 128
ZA_W = 1536
ZG_W = 128
GATE_COL = 32
RCHUNK = 128
GLA_SAFE_SPAN = 60.0
L1_IN_PAD = 768
QCAT = 256

V7X_VMEM_BYTES = 64 * 1024 * 1024
VMEM_LIMIT = V7X_VMEM_BYTES - 8 * 1024 * 1024


def _cparams(sem):
    return pltpu.CompilerParams(dimension_semantics=sem, vmem_limit_bytes=VMEM_LIMIT)


def _dot(a, b):
    return jnp.dot(a, b, preferred_element_type=F32)


def _dot_nt(a, b):
    return lax.dot_general(a, b, (((1,), (1,)), ((), ())), preferred_element_type=F32)


def _dot_tn(a, b):
    return lax.dot_general(a, b, (((0,), (0,)), ((), ())), preferred_element_type=F32)


def _split3(x):
    hi = x.astype(BF16)
    r1 = x - hi.astype(F32)
    mid = r1.astype(BF16)
    lo = (r1 - mid.astype(F32)).astype(BF16)
    return hi, mid, lo


def _tri_dot(tri, x):
    hi, mid, lo = _split3(x)
    return _dot(tri, hi) + _dot(tri, mid) + _dot(tri, lo)


def _dot_hl(a, b):
    ah = a.astype(BF16)
    al = (a - ah.astype(F32)).astype(BF16)
    bh = b.astype(BF16)
    bl = (b - bh.astype(F32)).astype(BF16)
    return _dot(ah, bh) + _dot(ah, bl) + _dot(al, bh)


def _rms(x, g):
    return x * lax.rsqrt(jnp.mean(x * x, axis=-1, keepdims=True) + EPS) * g


def _sigmoid(x):
    return 1.0 / (1.0 + jnp.exp(-x))


def _silu(x):
    return x * _sigmoid(x)


def _log_sigmoid(x):
    return jnp.minimum(x, 0.0) - jnp.log(1.0 + jnp.exp(-jnp.abs(x)))


def _tri_masks(n):
    r = lax.broadcasted_iota(jnp.int32, (n, n), 0)
    c = lax.broadcasted_iota(jnp.int32, (n, n), 1)
    return c <= r, c >= r


def _mod_kernel(cond_ref, w_ref, b_ref, o_ref):
    s = _silu(cond_ref[...])
    o_ref[...] = _dot(s.astype(BF16), w_ref[...].astype(BF16)) + b_ref[...]


def _modulation(cond8, w_mod, b_mod):
    d = D_MODEL
    out = pl.pallas_call(
        _mod_kernel,
        out_shape=jax.ShapeDtypeStruct((N_MOD, 8, d), F32),
        grid=(N_MOD,),
        in_specs=[pl.BlockSpec((8, d), lambda j: (0, 0)),
                  pl.BlockSpec((d, d), lambda j: (0, j)),
                  pl.BlockSpec((1, d), lambda j: (0, j))],
        out_specs=pl.BlockSpec((None, 8, d), lambda j: (j, 0, 0)),
        compiler_params=_cparams(("arbitrary",)),
        name="modulation",
    )(cond8, w_mod, b_mod.reshape(1, N_MOD * d))
    return out.reshape(N_MOD, 8, 1, d)


MOD_SHIFT1, MOD_SCALE1, MOD_GATE1, MOD_SHIFT2, MOD_SCALE2, MOD_GATE2 = range(N_MOD)


def _mod_spec(which, first_row, per):
    return pl.BlockSpec((None, None, 1, D_MODEL), lambda i: (which, first_row + i // per, 0, 0))


L0_GROUP_ROWS = ((0, ZA_W), (ZA_W + 32, 2 * ZA_W + 32))
L0_LR_ROWS = (ZA_W, ZA_W + 32)
L0_GATE_ROWS = (2 * ZA_W + 32, 2 * ZA_W + 48)


def _p0_kernel(x_ref, g_ref, sc_ref, sh_ref, wt_ref, za_ref, zb_ref, zg_ref, w_ref):
    @pl.when(pl.program_id(0) == 0)
    def _():
        blk = 256
        for gi, (r0, r1) in enumerate(L0_GROUP_ROWS):
            for j in range((r1 - r0) // blk):
                rows = wt_ref[r0 + j * blk:r0 + (j + 1) * blk, :]
                w_ref[:, gi * ZA_W + j * blk:gi * ZA_W + (j + 1) * blk] = rows.T.astype(BF16)
        tail = jnp.concatenate([wt_ref[L0_LR_ROWS[0]:L0_LR_ROWS[1], :],
                                wt_ref[L0_GATE_ROWS[0]:L0_GATE_ROWS[1], :],
                                jnp.zeros((ZG_W - 48, D_MODEL), F32)], axis=0)
        w_ref[:, 2 * ZA_W:2 * ZA_W + ZG_W] = tail.T.astype(BF16)

    h = _rms(x_ref[...], g_ref[...]) * (1.0 + sc_ref[...]) + sh_ref[...]
    hb = h.astype(BF16)
    za_ref[...] = _dot(hb, w_ref[:, 0:ZA_W])
    zb_ref[...] = _dot(hb, w_ref[:, ZA_W:2 * ZA_W])
    zg_ref[...] = _dot(hb, w_ref[:, 2 * ZA_W:2 * ZA_W + ZG_W])


def _proj_l0(x, g, mod, first_row, groups, w_t, tm):
    t = x.shape[0]
    per = (t // tm) // groups
    row = lambda i: (i, 0)
    fixed = lambda i: (0, 0)
    return pl.pallas_call(
        _p0_kernel,
        out_shape=(jax.ShapeDtypeStruct((t, ZA_W), F32),
                   jax.ShapeDtypeStruct((t, ZA_W), F32),
                   jax.ShapeDtypeStruct((t, ZG_W), F32)),
        grid=(t // tm,),
        in_specs=[pl.BlockSpec((tm, D_MODEL), row),
                  pl.BlockSpec((1, D_MODEL), fixed),
                  _mod_spec(MOD_SCALE1, first_row, per),
                  _mod_spec(MOD_SHIFT1, first_row, per),
                  _resident(w_t.shape)],
        out_specs=(pl.BlockSpec((tm, ZA_W), row),
                   pl.BlockSpec((tm, ZA_W), row),
                   pl.BlockSpec((tm, ZG_W), row)),
        scratch_shapes=[pltpu.VMEM((D_MODEL, 2 * ZA_W + ZG_W), BF16)],
        compiler_params=_cparams(("arbitrary",)),
        name="proj_l0",
    )(x, g, mod, mod, w_t)


def _head_lane_mask(shape, h):
    lane = lax.broadcasted_iota(jnp.int32, shape, len(shape) - 1)
    return (lane >= h * 64) & (lane < (h + 1) * 64)


def _head_row(h):
    return jnp.where(_head_lane_mask((1, HK), h), 1.0, 0.0).astype(BF16)


def _gla_prepare(za_ref, la_ref, r0, rev, tri):
    L = RCHUNK
    rows = pl.ds(r0, L)
    b = _tri_dot(tri, la_ref[rows, :])
    mid = L // 2
    ref = b[mid:mid + 1, :]
    b_end = b[0:1, :] if rev else b[L - 1:L, :]
    q = za_ref[rows, 0:HK]
    k = za_ref[rows, HK:2 * HK]
    v = za_ref[rows, 2 * HK:2 * HK + HV].astype(BF16)
    qe = (q * (jnp.exp(b - ref) * (A_DK ** -0.5))).astype(BF16)
    ke = (k * jnp.exp(ref - b)).astype(BF16)
    qm = jnp.concatenate([qe * _head_row(h) for h in range(HEADS)], axis=0)
    rt = _dot_tn(v, ke)
    upd = None
    for h in range(HEADS):
        blk = rt[h * 128:(h + 1) * 128, :]
        blk = jnp.where(_head_lane_mask(blk.shape, h), blk, 0.0)
        upd = blk if upd is None else upd + blk
    return dict(rows=rows, qm=qm, ke=ke, v=v, upd=upd * jnp.exp(b_end - ref),
                decay=jnp.exp(b_end), eref=jnp.exp(ref))


def _gla_apply(items):
    L = RCHUNK
    sts = [st_ref[...] for _, st_ref, _, _ in items]
    o1s = []
    for (p, _, _, _), st in zip(items, sts):
        rhs = jnp.concatenate([p["ke"], (st * p["eref"]).astype(BF16)], axis=0)
        o1s.append(_dot_nt(p["qm"], rhs))
    for h in range(HEADS):
        for (p, _, o_ref, mask), o1 in zip(items, o1s):
            blk = o1[h * L:(h + 1) * L, :]
            sc = jnp.where(mask, blk[:, 0:L], 0.0).astype(BF16)
            o_ref[p["rows"], h * 128:(h + 1) * 128] = (
                blk[:, L:L + 128] + _dot(sc, p["v"][:, h * 128:(h + 1) * 128]))
    for (p, st_ref, _, _), st in zip(items, sts):
        st_ref[...] = st * p["decay"] + p["upd"]


def _gla_exact_scan(za_ref, la_ref, st_ref, o_ref, b_ref, n, rev, st_init):
    L = RCHUNK
    nc = n // L
    m_lo, m_up = _tri_masks(L)
    tri = jnp.where(m_up if rev else m_lo, 1.0, 0.0).astype(BF16)
    s_idx = lax.broadcasted_iota(jnp.int32, (L, 1), 0)
    krow = lax.broadcasted_iota(jnp.int32, (HK, 128), 0) // 64
    kcol = lax.broadcasted_iota(jnp.int32, (HK, 128), 1)
    head_sum = jnp.where(krow == kcol, 1.0, 0.0)
    hsub = lax.broadcasted_iota(jnp.int32, (8, HK), 0)
    hlane = lax.broadcasted_iota(jnp.int32, (8, HK), 1) // 64
    head_rows = jnp.where(hsub == hlane, 1.0, 0.0)
    st_ref[...] = st_init

    def chunk_body(ci, carry):
        c = nc - 1 - ci if rev else ci
        r0 = pl.multiple_of(c * L, L)
        rows = pl.ds(r0, L)
        b = _tri_dot(tri, la_ref[rows, :])
        b_ref[...] = b
        b_end = b[0:1, :] if rev else b[L - 1:L, :]
        k = za_ref[rows, HK:2 * HK]
        vb = za_ref[rows, 2 * HK:2 * HK + HV].astype(BF16)
        st = st_ref[...]
        stb = st.astype(BF16)

        def row_body(t, carry2):
            b_t = b_ref[pl.ds(t, 1), :]
            q_t = za_ref[pl.ds(r0 + t, 1), 0:HK] * (A_DK ** -0.5)
            valid = (s_idx >= t) if rev else (s_idx <= t)
            w = jnp.exp(jnp.minimum(b_t - b_ref[...], 0.0))
            p = jnp.where(valid, (q_t * k) * w, 0.0)
            sc = jnp.dot(p, head_sum, precision=lax.Precision.HIGHEST,
                         preferred_element_type=F32)
            pv = _dot_tn(sc.astype(BF16), vb)
            qm = (head_rows * (q_t * jnp.exp(b_t))).astype(BF16)
            inter = _dot_nt(qm, stb)
            o_ref[pl.ds(r0 + t, 1), :] = jnp.concatenate(
                [pv[h:h + 1, h * 128:(h + 1) * 128] + inter[h:h + 1, :] for h in range(HEADS)],
                axis=1)
            return carry2

        lax.fori_loop(0, L, row_body, 0)
        kd = (k * jnp.exp(b_end - b)).astype(BF16)
        rt = _dot_tn(vb, kd)
        upd = None
        for h in range(HEADS):
            blk = rt[h * 128:(h + 1) * 128, :]
            blk = jnp.where(_head_lane_mask(blk.shape, h), blk, 0.0)
            upd = blk if upd is None else upd + blk
        st_ref[...] = st * jnp.exp(b_end) + upd
        return carry

    lax.fori_loop(0, nc, chunk_body, 0)


T_M, T_MOUT, T_G, T_SRCW, T_MPREV, T_CARRY, T_USED = 0, 8, 16, 24, 32, 40, 48


def _lane_scan(x, op, fill, is_fwd, pos, n):
    k = 1
    while k < RCHUNK:
        pf = jnp.where(pos >= k, pltpu.roll(x, k, axis=1), fill)
        sf = jnp.where(pos < RCHUNK - k, pltpu.roll(x, n - k, axis=1), fill)
        x = op(x, jnp.where(is_fwd, pf, sf))
        k *= 2
    return x


def _mlstm_gate_tables(g, m0, gt_ref, tab_ref, col_ref, n):
    L = RCHUNK
    nc = n // L
    for c in range(nc):
        gt_ref[:, c * L:(c + 1) * L] = g[c * L:(c + 1) * L, :].T
        tab_ref[c, T_USED:, :] = jnp.zeros((ZG_W - T_USED, L), F32)
    gi = gt_ref[GATE_COL:GATE_COL + 8, :]
    lf = gt_ref[GATE_COL + 8:GATE_COL + 16, :]
    is_fwd = lax.broadcasted_iota(jnp.int32, (8, n), 0) < HEADS
    pos = lax.broadcasted_iota(jnp.int32, (8, n), 1) % L
    b = _lane_scan(lf, jnp.add, 0.0, is_fwd, pos, n)
    gg = gi - b
    cmax = _lane_scan(gg, jnp.maximum, -jnp.inf, is_fwd, pos, n)
    fwd_l = lax.broadcasted_iota(jnp.int32, (8, L), 0) < HEADS
    mp = m0
    for j in range(nc):
        cf, cb = j, nc - 1 - j

        def pick(x):
            return jnp.where(fwd_l, x[:, cf * L:(cf + 1) * L], x[:, cb * L:(cb + 1) * L])

        gj, bj, cj, lj = pick(gg), pick(b), pick(cmax), pick(lf)
        mx = jnp.maximum(mp, jnp.max(gj, axis=1, keepdims=True))
        mj = jnp.maximum(cj, mp)
        blocks = (mj, bj + mj, gj, jnp.exp(gj - mx),
                  jnp.broadcast_to(mp, (8, L)), jnp.broadcast_to(jnp.exp(mp - mx), (8, L)))
        for t, val in enumerate(blocks):
            tab_ref[cf, 8 * t:8 * t + HEADS, :] = val[0:HEADS]
            tab_ref[cb, 8 * t + HEADS:8 * t + 8, :] = val[HEADS:8]
        mp = jnp.sum(lj, axis=1, keepdims=True) + mx
    for c in range(nc):
        col_ref[c * L:(c + 1) * L, :] = tab_ref[c].T
    return mp


def _mlstm_prepare(zb_ref, tab_ref, col_ref, c, d, mask):
    L = RCHUNK
    rows = pl.ds(pl.multiple_of(c * L, L), L)
    qb = zb_ref[rows, 0:HK].astype(BF16)
    kb = (zb_ref[rows, HK:2 * HK] * (B_DK ** -0.5)).astype(BF16)
    v = zb_ref[rows, 2 * HK:2 * HK + HV]
    vb = v.astype(BF16)
    vt = v.T
    qm = jnp.concatenate([qb * _head_row(h) for h in range(HEADS)], axis=0)
    s = _dot_nt(qm, kb)
    ones = jnp.ones((L, 128), BF16)
    pv, w_inter, e_negm = [], [], []
    upd = None
    carry = None
    for h in range(HEADS):
        r = 4 * d + h
        m_rep = jnp.broadcast_to(col_ref[rows, T_M + r:T_M + r + 1], (L, 128))
        mout_rep = jnp.broadcast_to(col_ref[rows, T_MOUT + r:T_MOUT + r + 1], (L, 128))
        g_row = tab_ref[c, T_G + r:T_G + r + 1, :]
        w = jnp.exp(jnp.where(mask, g_row - m_rep, -jnp.inf))
        qk = (s[h * L:(h + 1) * L, :] * w).astype(BF16)
        v1 = jnp.concatenate([vb[:, h * 128:(h + 1) * 128], ones], axis=1)
        pv.append(_dot(qk, v1))
        w_inter.append(jnp.exp(tab_ref[c, T_MPREV + r:T_MPREV + r + 1, :] - m_rep))
        e_negm.append(jnp.exp(-mout_rep))
        sw = tab_ref[c, T_SRCW + r:T_SRCW + r + 1, :]
        lhs = jnp.concatenate([vt[h * 128:(h + 1) * 128, :] * sw, jnp.broadcast_to(sw, (128, L))],
                              axis=0).astype(BF16)
        blk = _dot(lhs, kb)
        blk = jnp.where(_head_lane_mask(blk.shape, h), blk, 0.0)
        upd = blk if upd is None else upd + blk
        cr = tab_ref[c, T_CARRY + r:T_CARRY + r + 1, :]
        cr = jnp.where(_head_lane_mask((1, HK), h), jnp.concatenate([cr, cr], axis=1), 0.0)
        carry = cr if carry is None else carry + cr
    return dict(rows=rows, qm=qm, pv=pv, w_inter=w_inter, e_negm=e_negm, upd=upd, carry=carry)


def _mlstm_apply(items):
    L = RCHUNK
    sts = [st_ref[...] for _, st_ref, _ in items]
    a_all = [_dot_nt(p["qm"], st.astype(BF16)) for (p, _, _), st in zip(items, sts)]
    for h in range(HEADS):
        for (p, _, h_ref), a in zip(items, a_all):
            blk = a[h * L:(h + 1) * L, :]
            pv, w, e = p["pv"][h], p["w_inter"][h], p["e_negm"][h]
            num = w * blk[:, 0:128] + pv[:, 0:128]
            den = w * blk[:, 128:256] + pv[:, 128:256]
            h_ref[p["rows"], h * 128:(h + 1) * 128] = num / jnp.maximum(jnp.abs(den), e)
    for (p, st_ref, _), st in zip(items, sts):
        st_ref[...] = p["carry"] * st + p["upd"]


def _rec_kernel(*refs, n, has_init):
    it = iter(refs)

    def take(k):
        return [next(it) for _ in range(k)]

    za_all, zb_all, zg_all, zgn_all = take(4)
    wgf_ref, wgb_ref, bgf_ref, bgb_ref, gna_ref, bg_ref, gnb_ref = take(7)
    if has_init:
        init_all = take(7)
    out_all = take(8)
    (gt_all,) = take(1)
    set_a_all = take(6)
    set_b_all = take(6)
    work_all = take(9)
    L = RCHUNK
    nc = n // L
    step = pl.program_id(0)
    elems = []
    for e in range(za_all.shape[0]):
        el = dict(za=za_all.at[e], zb=zb_all.at[e], zg=zg_all.at[e], zgn=zgn_all.at[e],
                  gt=gt_all.at[e], set_a=[r.at[e] for r in set_a_all],
                  set_b=[r.at[e] for r in set_b_all])
        (el["oa"], el["sf"], el["sb"], el["om"], el["cf"], el["cb"], el["n_out"],
         el["m_out"]) = [r.at[e] for r in out_all]
        (el["of"], el["ob"], el["hf"], el["hb"], el["gsf"], el["gsb"], el["msf"],
         el["msb"], el["b_exact"]) = [r.at[e] for r in work_all]
        if has_init:
            el["init"] = [r.at[e] for r in init_all]
        elems.append(el)

    def fill(gate_ref, m0, bufs, gt_ref):
        laf_ref, lab_ref, tab_ref, col_ref, mf_ref, span_ref = bufs
        zg = gate_ref[...]
        laf = _log_sigmoid(_dot_hl(zg, wgf_ref[...]) + bgf_ref[...]) / A_GATE_TEMP
        lab = _log_sigmoid(_dot_hl(zg, wgb_ref[...]) + bgb_ref[...]) / A_GATE_TEMP
        laf_ref[...] = laf
        lab_ref[...] = lab
        span = None
        for la in (laf, lab):
            for c in range(nc):
                tot = -jnp.sum(la[c * L:(c + 1) * L, :], axis=0, keepdims=True)
                span = tot if span is None else jnp.maximum(span, tot)
        span_ref[...] = jnp.broadcast_to(jnp.max(span, axis=1, keepdims=True), (8, 128))
        g = zg + bg_ref[...]
        lane = lax.broadcasted_iota(jnp.int32, g.shape, 1)
        is_forget = (lane >= GATE_COL + 8) & (lane < GATE_COL + 16)
        g = jnp.where(is_forget, _log_sigmoid(g), g)
        m_fin = _mlstm_gate_tables(g, m0, gt_ref, tab_ref, col_ref, n)
        mf_ref[...] = jnp.broadcast_to(m_fin, (8, 128))

    for el in elems:
        if has_init:
            sf0_ref, sb0_ref, cf0_ref, cb0_ref, n0_ref, m0_ref, m0n_ref = el["init"]
            el["m0"], el["m0_next"] = m0_ref[:, 0:1], m0n_ref[:, 0:1]
            el["gsf"][...] = sf0_ref[...]
            el["gsb"][...] = sb0_ref[...]
            el["msf"][...] = jnp.concatenate(
                [cf0_ref[...], jnp.broadcast_to(n0_ref[0:1, :], (128, HK))], axis=0)
            el["msb"][...] = jnp.concatenate(
                [cb0_ref[...], jnp.broadcast_to(n0_ref[1:2, :], (128, HK))], axis=0)
        else:
            el["m0"] = el["m0_next"] = jnp.zeros((8, 1), F32)
            for name in ("gsf", "gsb", "msf", "msb"):
                el[name][...] = jnp.zeros(el[name].shape, F32)

    @pl.when(step == 0)
    def _():
        for el in elems:
            fill(el["zg"], el["m0"], el["set_a"], el["gt"])

    m_lo, m_up = _tri_masks(L)
    tri_lo = jnp.where(m_lo, 1.0, 0.0).astype(BF16)
    tri_up = jnp.where(m_up, 1.0, 0.0).astype(BF16)
    gna = gna_ref[...]
    gnb = gnb_ref[...]

    def gla_out(el, rows, h):
        cols = slice(h * 128, (h + 1) * 128)
        gcols = slice(2 * HK + HV + h * 128, 2 * HK + HV + (h + 1) * 128)
        o = el["of"][rows, cols] + el["ob"][rows, cols]
        el["oa"][rows, cols] = (_rms(o, gna) * _silu(el["za"][rows, gcols])).astype(BF16)

    def out_body(c, carry):
        rows = pl.ds(pl.multiple_of(c * L, L), L)
        for h in range(HEADS):
            cols = slice(h * 128, (h + 1) * 128)
            gcols = slice(2 * HK + HV + h * 128, 2 * HK + HV + (h + 1) * 128)
            for el in elems:
                gla_out(el, rows, h)
                hh = el["hf"][rows, cols] + el["hb"][rows, cols]
                el["om"][rows, cols] = (
                    _sigmoid(el["zb"][rows, gcols]) * _rms(hh, gnb)).astype(BF16)
        return carry

    def run(cur, nxt):
        for el in elems:
            fill(el["zgn"], el["m0_next"], el[nxt], el["gt"])

        def scan_body(gidx, carry):
            waves = []
            for w in range(2):
                c = 2 * gidx + w
                cb = nc - 1 - c
                gla_items, mlstm_items = [], []
                for el in elems:
                    laf_ref, lab_ref, tab_ref, col_ref = el[cur][0:4]
                    gla_items += [
                        (_gla_prepare(el["za"], laf_ref, pl.multiple_of(c * L, L), False, tri_lo),
                         el["gsf"], el["of"], m_lo),
                        (_gla_prepare(el["za"], lab_ref, pl.multiple_of(cb * L, L), True, tri_up),
                         el["gsb"], el["ob"], m_up)]
                    mlstm_items += [
                        (_mlstm_prepare(el["zb"], tab_ref, col_ref, c, 0, m_lo), el["msf"], el["hf"]),
                        (_mlstm_prepare(el["zb"], tab_ref, col_ref, cb, 1, m_up), el["msb"], el["hb"])]
                waves.append((gla_items, mlstm_items))
            for gla_items, mlstm_items in waves:
                _gla_apply(gla_items)
                _mlstm_apply(mlstm_items)
            return carry

        if nc == 2:
            scan_body(0, 0)
        else:
            lax.fori_loop(0, nc // 2, scan_body, 0)
        if nc == 2:
            for c in range(nc):
                out_body(c, 0)
        else:
            lax.fori_loop(0, nc, out_body, 0)
        for el in elems:
            el["m_out"][...] = el[cur][4][...]
        for el in elems:
            laf_ref, lab_ref = el[cur][0:2]
            too_fast = el[cur][5][0, 0] > GLA_SAFE_SPAN

            @pl.when(too_fast)
            def _(el=el, laf_ref=laf_ref, lab_ref=lab_ref):
                zero = jnp.zeros((128, HK), F32)
                init_f = el["init"][0][...] if has_init else zero
                init_b = el["init"][1][...] if has_init else zero
                _gla_exact_scan(el["za"], laf_ref, el["gsf"], el["of"], el["b_exact"], n, False, init_f)
                _gla_exact_scan(el["za"], lab_ref, el["gsb"], el["ob"], el["b_exact"], n, True, init_b)

                def redo_out(c, carry):
                    rows = pl.ds(pl.multiple_of(c * L, L), L)
                    for h in range(HEADS):
                        gla_out(el, rows, h)
                    return carry

                lax.fori_loop(0, nc, redo_out, 0)

    @pl.when(step % 2 == 0)
    def _():
        run("set_a", "set_b")

    @pl.when(step % 2 == 1)
    def _():
        run("set_b", "set_a")

    for el in elems:
        gsf_t = el["gsf"][...].T
        gsb_t = el["gsb"][...].T
        msf = el["msf"][...]
        msb = el["msb"][...]
        ctf_t = msf[0:128, :].T
        ctb_t = msb[0:128, :].T
        for h in range(HEADS):
            rows = slice(h * 64, (h + 1) * 64)
            el["sf"][h] = gsf_t[rows, :]
            el["sb"][h] = gsb_t[rows, :]
            el["cf"][h] = ctf_t[rows, :]
            el["cb"][h] = ctb_t[rows, :]
        el["n_out"][...] = jnp.concatenate(
            [msf[128:129, :], msb[128:129, :], jnp.zeros((6, HK), F32)], axis=0)


def _recurrent(za, zb, zg, wgf, wgb, bgf, bgb, gna, bg, gnb, init, bt):
    bsz, n, _ = za.shape
    has_init = init is not None
    nc = n // RCHUNK
    steps = bsz // bt
    per_b3 = lambda b: (b, 0, 0)
    next_b3 = lambda b: (jnp.minimum(b + 1, steps - 1), 0, 0)
    per_b4 = lambda b: (b, 0, 0, 0)
    fixed = lambda b: (0, 0)
    in_specs = [pl.BlockSpec((bt, n, ZA_W), per_b3),
                pl.BlockSpec((bt, n, ZA_W), per_b3),
                pl.BlockSpec((bt, n, ZG_W), per_b3),
                pl.BlockSpec((bt, n, ZG_W), next_b3),
                pl.BlockSpec((ZG_W, HK), fixed),
                pl.BlockSpec((ZG_W, HK), fixed),
                pl.BlockSpec((1, HK), fixed),
                pl.BlockSpec((1, HK), fixed),
                pl.BlockSpec((1, 128), fixed),
                pl.BlockSpec((1, ZG_W), fixed),
                pl.BlockSpec((1, 128), fixed)]
    args = [za, zb, zg, zg, wgf, wgb, bgf, bgb, gna, bg, gnb]
    if has_init:
        sf0, sb0, cf0, cb0, n0, m0 = init
        in_specs += [pl.BlockSpec((bt, 128, HK), per_b3)] * 4
        in_specs += [pl.BlockSpec((bt, 8, HK), per_b3), pl.BlockSpec((bt, 8, 128), per_b3),
                     pl.BlockSpec((bt, 8, 128), next_b3)]
        args += [sf0, sb0, cf0, cb0, n0, m0, m0]
    state = jax.ShapeDtypeStruct((bsz, HEADS, 64, 128), F32)
    state_spec = pl.BlockSpec((bt, HEADS, 64, 128), per_b4)

    def vmem(*shape):
        return pltpu.VMEM((bt,) + shape, F32)

    gate_set = [vmem(n, HK), vmem(n, HK), vmem(nc, ZG_W, RCHUNK), vmem(n, ZG_W), vmem(8, 128),
                vmem(8, 128)]
    return pl.pallas_call(
        functools.partial(_rec_kernel, n=n, has_init=has_init),
        out_shape=(jax.ShapeDtypeStruct((bsz, n, HV), BF16), state, state,
                   jax.ShapeDtypeStruct((bsz, n, HV), BF16), state, state,
                   jax.ShapeDtypeStruct((bsz, 8, HK), F32),
                   jax.ShapeDtypeStruct((bsz, 8, 128), F32)),
        grid=(steps,),
        in_specs=in_specs,
        out_specs=(pl.BlockSpec((bt, n, HV), per_b3), state_spec, state_spec,
                   pl.BlockSpec((bt, n, HV), per_b3), state_spec, state_spec,
                   pl.BlockSpec((bt, 8, HK), per_b3),
                   pl.BlockSpec((bt, 8, 128), per_b3)),
        scratch_shapes=([vmem(ZG_W, n)] + gate_set + gate_set + [vmem(n, HV)] * 4
                        + [vmem(128, HK)] * 2 + [vmem(2 * 128, HK)] * 2 + [vmem(RCHUNK, HK)]),
        compiler_params=_cparams(("arbitrary",)),
        name="recurrent",
    )(*args)


def _mix_residual(parts, wo_ref, x_ref, gt1_ref, gpost1_ref, gpre2_ref, sc2_ref, sh2_ref):
    m = None
    off = 0
    for p in parts:
        kp = p.shape[1]
        t = _dot(p[...], wo_ref[off:off + kp, :])
        m = t if m is None else m + t
        off += kp
    x1 = x_ref[...] + gt1_ref[...] * _rms(m, gpost1_ref[...])
    hb = (_rms(x1, gpre2_ref[...]) * (1.0 + sc2_ref[...]) + sh2_ref[...]).astype(BF16)
    return x1, hb


def _ffn_gated_up(hb, wu_ref, cw_ref, cb_ref, act_ref, seq, tf):
    tm = hb.shape[0]
    pos = lax.broadcasted_iota(jnp.int32, (tm, 1), 0) % seq
    first = pos == 0
    last = pos == seq - 1

    def conv(u, cols):
        prev = jnp.where(first, 0.0, pltpu.roll(u, 1, axis=0))
        nxt = jnp.where(last, 0.0, pltpu.roll(u, tm - 1, axis=0))
        return (cw_ref[0:1, cols] * prev + cw_ref[1:2, cols] * u + cw_ref[2:3, cols] * nxt
                + cb_ref[:, cols])

    for j in range(D_FF // tf):
        ca = slice(j * tf, (j + 1) * tf)
        cg = slice(D_FF + j * tf, D_FF + (j + 1) * tf)
        a = conv(_dot(hb, wu_ref[:, ca]), ca)
        g = conv(_dot(hb, wu_ref[:, cg]), cg)
        act_ref[:, ca] = (_silu(g) * a).astype(BF16)


def _mix_ffn_kernel(*refs, n_parts, seq, tf):
    parts = refs[:n_parts]
    (wo_ref, x_ref, gt1_ref, gpost1_ref, gpre2_ref, sc2_ref, sh2_ref,
     wu_ref, cw_ref, cb_ref, wd_ref, gt2_ref, gpost2_ref, o_ref, act_ref) = refs[n_parts:]
    x1, hb = _mix_residual(parts, wo_ref, x_ref, gt1_ref, gpost1_ref, gpre2_ref, sc2_ref, sh2_ref)
    o_ref[...] = x1
    _ffn_gated_up(hb, wu_ref, cw_ref, cb_ref, act_ref, seq, tf)
    y = _dot(act_ref[...], wd_ref[...])
    o_ref[...] = o_ref[...] + gt2_ref[...] * _rms(y, gpost2_ref[...])


def _resident(shape):
    return pl.BlockSpec(shape, lambda *_: (0,) * len(shape), pipeline_mode=pl.Buffered(1))


def _mix_ffn(parts, w_out, x, mod, first_row, groups, gpost1, gpre2, w_up, conv_w, conv_b, w_down,
             gpost2, tm, seq, tf=256):
    t = x.shape[0]
    per = (t // tm) // groups
    row = lambda i: (i, 0)
    vec = _resident((1, D_MODEL))
    in_specs = [pl.BlockSpec((tm, p.shape[1]), row) for p in parts]
    in_specs += [_resident(w_out.shape), pl.BlockSpec((tm, D_MODEL), row),
                 _mod_spec(MOD_GATE1, first_row, per), vec, vec,
                 _mod_spec(MOD_SCALE2, first_row, per), _mod_spec(MOD_SHIFT2, first_row, per),
                 _resident(w_up.shape), _resident(conv_w.shape), _resident(conv_b.shape),
                 _resident(w_down.shape), _mod_spec(MOD_GATE2, first_row, per), vec]
    return pl.pallas_call(
        functools.partial(_mix_ffn_kernel, n_parts=len(parts), seq=seq, tf=tf),
        out_shape=jax.ShapeDtypeStruct((t, D_MODEL), F32),
        grid=(t // tm,),
        in_specs=in_specs,
        out_specs=pl.BlockSpec((tm, D_MODEL), row),
        scratch_shapes=[pltpu.VMEM((tm, D_FF), BF16)],
        compiler_params=_cparams(("arbitrary",)),
        name="mix_ffn",
    )(*parts, w_out, x, mod, gpost1, gpre2, mod, mod, w_up, conv_w, conv_b, w_down, mod, gpost2)


def _rope(x, cos, sin_signed):
    lane = lax.broadcasted_iota(jnp.int32, x.shape, 1)
    lower = (lane % 32) < 16
    partner = jnp.where(lower, pltpu.roll(x, 128 - 16, axis=1), pltpu.roll(x, 16, axis=1))
    return x * cos + partner * sin_signed


def _p1_kernel(*refs, rope):
    if rope:
        (x_ref, g_ref, sc_ref, sh_ref, w_ref, gq_ref, wq_ref, gkv_ref, cos_ref, sin_ref,
         q_ref, ckv_ref, kr_ref) = refs
    else:
        (x_ref, g_ref, sc_ref, sh_ref, w_ref, gq_ref, wq_ref, gkv_ref,
         q_ref, ckv_ref, kr_ref) = refs
    h = _rms(x_ref[...], g_ref[...]) * (1.0 + sc_ref[...]) + sh_ref[...]
    z = _dot(h.astype(BF16), w_ref[...])
    qa = _rms(z[:, 0:C_Q_RANK], gq_ref[...])
    q = _dot(qa.astype(BF16), wq_ref[...])
    ckv_ref[...] = _rms(z[:, C_Q_RANK:C_Q_RANK + C_KV_RANK], gkv_ref[...])
    kr = z[:, C_Q_RANK + C_KV_RANK:L1_IN_PAD]
    if rope:
        cos = cos_ref[...]
        sin = sin_ref[...]
        kr = _rope(kr, cos, sin)
    kr_ref[...] = kr[:, 0:C_ROPE]
    for hd in range(C_HEADS):
        c0 = hd * QCAT
        q_ref[:, c0:c0 + 128] = q[:, c0:c0 + 128].astype(q_ref.dtype)
        qr = q[:, c0 + 128:c0 + 256]
        if rope:
            qr = _rope(qr, cos, sin)
        q_ref[:, c0 + 128:c0 + 256] = qr.astype(q_ref.dtype)


def _proj_l1(x, g, mod, first_row, groups, w, gq, wq, gkv, tables, tm):
    t = x.shape[0]
    per = (t // tm) // groups
    row = lambda i: (i, 0)
    fixed = lambda i: (0, 0)
    in_specs = [pl.BlockSpec((tm, D_MODEL), row),
                pl.BlockSpec((1, D_MODEL), fixed),
                _mod_spec(MOD_SCALE1, first_row, per),
                _mod_spec(MOD_SHIFT1, first_row, per),
                pl.BlockSpec(w.shape, fixed),
                pl.BlockSpec((1, C_Q_RANK), fixed),
                pl.BlockSpec(wq.shape, fixed),
                pl.BlockSpec((1, C_KV_RANK), fixed)]
    args = [x, g, mod, mod, w, gq, wq, gkv]
    rope = tables is not None
    if rope:
        n = tables[0].shape[0]
        pos = lambda i: (i % (n // tm), 0)
        in_specs += [pl.BlockSpec((tm, 128), pos)] * 2
        args += list(tables)
    return pl.pallas_call(
        functools.partial(_p1_kernel, rope=rope),
        out_shape=(jax.ShapeDtypeStruct((t, C_HEADS * QCAT), BF16),
                   jax.ShapeDtypeStruct((t, C_KV_RANK), F32),
                   jax.ShapeDtypeStruct((t, C_ROPE), F32)),
        grid=(t // tm,),
        in_specs=in_specs,
        out_specs=(pl.BlockSpec((tm, C_HEADS * QCAT), row),
                   pl.BlockSpec((tm, C_KV_RANK), row),
                   pl.BlockSpec((tm, C_ROPE), row)),
        compiler_params=_cparams(("arbitrary",)),
        name="proj_l1",
    )(*args)


def _attn_kernel(*refs, n_parts, scale):
    q_ref = refs[0]
    kv_parts = refs[1:1 + 2 * n_parts]
    w_ref, o_ref, k_sc, v_sc = refs[1 + 2 * n_parts:]
    bt = q_ref.shape[0]

    @pl.when(pl.program_id(1) == 0)
    def _():
        off = 0
        for p in range(n_parts):
            ckv_ref, kr_ref = kv_parts[2 * p], kv_parts[2 * p + 1]
            nk = ckv_ref.shape[1]
            ones = jnp.ones((nk, C_DV), BF16)
            for e in range(bt):
                kv = _dot(ckv_ref[e].astype(BF16), w_ref[...])
                kr = kr_ref[e].astype(BF16)
                kr2 = jnp.concatenate([kr, kr], axis=1)
                for hd in range(C_HEADS):
                    v0 = C_HEADS * C_NOPE + hd * C_DV
                    k_sc[e, hd, off:off + nk, 0:128] = kv[:, hd * 128:(hd + 1) * 128].astype(BF16)
                    k_sc[e, hd, off:off + nk, 128:256] = kr2
                    v_sc[e, hd, off:off + nk, 0:C_DV] = kv[:, v0:v0 + C_DV].astype(BF16)
                    v_sc[e, hd, off:off + nk, C_DV:2 * C_DV] = ones
            off += nk

    for hd in range(C_HEADS):
        for e in range(bt):
            t = _dot_nt(q_ref[e, :, hd * QCAT:(hd + 1) * QCAT], k_sc[e, hd]) * (scale * LOG2E)
            p = jnp.exp2(t - jnp.max(t, axis=1, keepdims=True))
            ov = _dot(p.astype(BF16), v_sc[e, hd])
            o_ref[e, :, hd * C_DV:(hd + 1) * C_DV] = (
                ov[:, 0:C_DV] / ov[:, C_DV:2 * C_DV]).astype(o_ref.dtype)


def _attention(q, kv_parts, w_kvb, tq, bt):
    bsz, n, _ = q.shape
    nk = sum(p[0].shape[1] for p in kv_parts)
    in_specs = [pl.BlockSpec((bt, tq, C_HEADS * QCAT), lambda b, i: (b, i, 0))]
    args = [q]
    for ckv, kr in kv_parts:
        in_specs.append(pl.BlockSpec((bt, ckv.shape[1], C_KV_RANK), lambda b, i: (b, 0, 0)))
        in_specs.append(pl.BlockSpec((bt, kr.shape[1], C_ROPE), lambda b, i: (b, 0, 0)))
        args += [ckv, kr]
    in_specs.append(pl.BlockSpec(w_kvb.shape, lambda b, i: (0, 0)))
    args.append(w_kvb)
    return pl.pallas_call(
        functools.partial(_attn_kernel, n_parts=len(kv_parts),
                          scale=(C_NOPE + C_ROPE) ** -0.5),
        out_shape=jax.ShapeDtypeStruct((bsz, n, C_HEADS * C_DV), BF16),
        grid=(bsz // bt, n // tq),
        in_specs=in_specs,
        out_specs=pl.BlockSpec((bt, tq, C_HEADS * C_DV), lambda b, i: (b, i, 0)),
        scratch_shapes=[pltpu.VMEM((bt, C_HEADS, nk, QCAT), BF16),
                        pltpu.VMEM((bt, C_HEADS, nk, 2 * C_DV), BF16)],
        compiler_params=_cparams(("arbitrary", "arbitrary")),
        name="mla_attention",
    )(*args)


def _l1_wq_layout(w_qb):
    blocks = []
    zero = jnp.zeros((w_qb.shape[0], C_ROPE), w_qb.dtype)
    for hd in range(C_HEADS):
        base = hd * (C_NOPE + C_ROPE)
        nope = w_qb[:, base:base + C_NOPE]
        rope = w_qb[:, base + C_NOPE:base + C_NOPE + C_ROPE]
        blocks += [nope, rope, zero] if hd % 2 == 0 else [nope, zero, rope]
    return jnp.concatenate(blocks, axis=1).astype(BF16)


def _l1_wkvb_layout(w_kvb):
    w = w_kvb.reshape(C_KV_RANK, C_HEADS, C_NOPE + C_DV)
    k = w[:, :, :C_NOPE].reshape(C_KV_RANK, C_HEADS * C_NOPE)
    v = w[:, :, C_NOPE:].reshape(C_KV_RANK, C_HEADS * C_DV)
    return jnp.concatenate([k, v], axis=1).astype(BF16)


def _rope_tables(n):
    pos = np.arange(n)
    row = (pos // GRID_W).astype(np.float64)
    col = (pos % GRID_W).astype(np.float64)
    half = C_ROPE // 2
    inv = 1.0 / (ROPE_THETA ** (np.arange(0, half, 2, dtype=np.float64) / half))
    ang_r = row[:, None] * inv[None, :]
    ang_c = col[:, None] * inv[None, :]
    cos64 = np.concatenate([np.cos(ang_r)] * 2 + [np.cos(ang_c)] * 2, axis=1)
    sin64 = np.concatenate([-np.sin(ang_r), np.sin(ang_r), -np.sin(ang_c), np.sin(ang_c)], axis=1)
    return (jnp.asarray(np.concatenate([cos64] * 2, axis=1), F32),
            jnp.asarray(np.concatenate([sin64] * 2, axis=1), F32))


def _gate_weight(w, first_row):
    return jnp.pad(w, ((first_row, ZG_W - first_row - A_GATE_RANK), (0, 0)))


def _state_t(s):
    b = s.shape[0]
    return s.transpose(0, 3, 1, 2).reshape(b, 128, HK)


def _n_rows(n_fwd, n_bwd):
    b = n_fwd.shape[0]
    rows = jnp.zeros((b, 8, HK), F32)
    return rows.at[:, 0, :].set(n_fwd.reshape(b, HK)).at[:, 1, :].set(n_bwd.reshape(b, HK))


def _m_rows(m_fwd, m_bwd):
    m = jnp.concatenate([m_fwd, m_bwd], axis=1)
    return jnp.broadcast_to(m[:, :, None], (m.shape[0], 8, 128))


def kernel(x_prompt, x_sample, state_l0_gla_fwd, state_l0_gla_bwd, state_l0_mlstm_c_fwd, state_l0_mlstm_n_fwd, state_l0_mlstm_m_fwd, state_l0_mlstm_c_bwd, state_l0_mlstm_n_bwd, state_l0_mlstm_m_bwd, cache_l1_ckv, cache_l1_krope, c, c_ctx, l0_w_mod, l0_b_mod, l0_g_pre_mix, l0_g_post_mix, l0_g_pre_ffn, l0_g_post_ffn, l0_w_in, l0_gla_w_gate_f, l0_gla_b_gate_f, l0_gla_w_gate_b, l0_gla_b_gate_b, l0_gla_g_norm, l0_mlstm_b_gates, l0_mlstm_g_norm, l0_w_out, l0_ffn_w_up, l0_ffn_conv_w, l0_ffn_conv_b, l0_ffn_w_down, l1_w_mod, l1_b_mod, l1_g_pre_mix, l1_g_post_mix, l1_g_pre_ffn, l1_g_post_ffn, l1_w_in, l1_g_q_norm, l1_w_qb, l1_g_kv_norm, l1_w_kvb, l1_w_out, l1_ffn_w_up, l1_ffn_conv_w, l1_ffn_conv_b, l1_ffn_w_down):
    bp, sp, d = x_prompt.shape
    bs, ss, _ = x_sample.shape
    row = lambda v: v.reshape(1, -1)

    cond8 = jnp.concatenate([c_ctx[None, :], c, jnp.zeros((8 - 1 - bs, d), F32)], axis=0)
    mods = (_modulation(cond8, l0_w_mod, l0_b_mod), _modulation(cond8, l1_w_mod, l1_b_mod))

    paths = {
        "ctx": dict(x=x_prompt.reshape(bp * sp, d), b=bp, n=sp, first_row=0, groups=1, tm=512),
        "lat": dict(x=x_sample.reshape(bs * ss, d), b=bs, n=ss, first_row=1, groups=bs, tm=512),
    }

    w_in0 = l0_w_in.T
    wgf = _gate_weight(l0_gla_w_gate_f, 0)
    wgb = _gate_weight(l0_gla_w_gate_b, A_GATE_RANK)
    bg = jnp.pad(l0_mlstm_b_gates, (GATE_COL, ZG_W - GATE_COL - 16)).reshape(1, ZG_W)
    w_out0 = l0_w_out.astype(BF16)
    w_in1 = jnp.pad(l1_w_in, ((0, 0), (0, L1_IN_PAD - l1_w_in.shape[1]))).astype(BF16)
    w_q1 = _l1_wq_layout(l1_w_qb)
    w_kvb1 = _l1_wkvb_layout(l1_w_kvb)
    w_out1 = l1_w_out.astype(BF16)
    ffn = ((l0_ffn_w_up.astype(BF16), l0_ffn_conv_w, row(l0_ffn_conv_b), l0_ffn_w_down.astype(BF16)),
           (l1_ffn_w_up.astype(BF16), l1_ffn_conv_w, row(l1_ffn_conv_b), l1_ffn_w_down.astype(BF16)))
    norms = ((row(l0_g_pre_mix), row(l0_g_post_mix), row(l0_g_pre_ffn), row(l0_g_post_ffn)),
             (row(l1_g_pre_mix), row(l1_g_post_mix), row(l1_g_pre_ffn), row(l1_g_post_ffn)))

    gla_init = (_state_t(state_l0_gla_fwd), _state_t(state_l0_gla_bwd))
    mlstm_init = (_state_t(state_l0_mlstm_c_fwd), _state_t(state_l0_mlstm_c_bwd),
                  _n_rows(state_l0_mlstm_n_fwd, state_l0_mlstm_n_bwd),
                  _m_rows(state_l0_mlstm_m_fwd, state_l0_mlstm_m_bwd))
    tables = _rope_tables(ss)

    results = {}
    for name, p in paths.items():
        x, b, n, tm = p["x"], p["b"], p["n"], p["tm"]
        first_row, groups = p["first_row"], p["groups"]
        is_ctx = name == "ctx"
        ffn_tm = tm if is_ctx else n
        g_pre, g_post, g_pre2, g_post2 = norms[0]

        za, zb, zg = _proj_l0(x, g_pre, mods[0], first_row, groups, w_in0, tm)
        za, zb, zg = (t.reshape(b, n, -1) for t in (za, zb, zg))
        oa, sa_f, sa_b, om, c_f, c_b, n_rows, m_rows = _recurrent(
            za, zb, zg, wgf, wgb, row(l0_gla_b_gate_f), row(l0_gla_b_gate_b), row(l0_gla_g_norm),
            bg, row(l0_mlstm_g_norm), None if is_ctx else gla_init + mlstm_init,
            2 if is_ctx else 1)
        x = _mix_ffn([oa.reshape(b * n, HV), om.reshape(b * n, HV)], w_out0, x, mods[0], first_row,
                     groups, g_post, g_pre2, *ffn[0], g_post2, ffn_tm, n)

        g_pre, g_post, g_pre2, g_post2 = norms[1]
        q, ckv, kr = _proj_l1(x, g_pre, mods[1], first_row, groups, w_in1, row(l1_g_q_norm), w_q1,
                              row(l1_g_kv_norm), None if is_ctx else tables, tm)
        ckv3 = ckv.reshape(b, n, C_KV_RANK)
        kr3 = kr.reshape(b, n, C_ROPE)
        kv_parts = [(ckv3, kr3)] if is_ctx else [(cache_l1_ckv, cache_l1_krope), (ckv3, kr3)]
        o = _attention(q.reshape(b, n, -1), kv_parts, w_kvb1, 256 if is_ctx else 512,
                       4 if is_ctx else 1)
        x = _mix_ffn([o.reshape(b * n, C_HEADS * C_DV)], w_out1, x, mods[1], first_row,
                     groups, g_post, g_pre2, *ffn[1], g_post2, ffn_tm, n)

        results[name] = dict(y=x.reshape(b, n, d), gla=(sa_f, sa_b), c=(c_f, c_b),
                             n=n_rows, m=m_rows, ckv=ckv3, kr=kr3)

    r = results["ctx"]
    n_fwd = r["n"][:, 0, :].reshape(bp, B_HEADS, B_DK)
    n_bwd = r["n"][:, 1, :].reshape(bp, B_HEADS, B_DK)
    m_fwd = r["m"][:, 0:B_HEADS, 0]
    m_bwd = r["m"][:, B_HEADS:2 * B_HEADS, 0]
    return (r["y"], results["lat"]["y"], r["gla"][0], r["gla"][1],
            r["c"][0], n_fwd, m_fwd, r["c"][1], n_bwd, m_bwd, r["ckv"], r["kr"])
```

```python
import functools

import numpy as np
import jax
import jax.numpy as jnp
from jax import lax
from jax.experimental import pallas as pl
from jax.experimental.pallas import tpu as pltpu

F32 = jnp.float32
BF16 = jnp.bfloat16

D_MODEL = 1024
EPS = 1e-6
LOG2E = 1.4426950408889634
N_MOD = 6
GRID_W = 64
ROPE_THETA = 10000.0

A_HEADS, A_DK, A_DV, A_GATE_RANK, A_GATE_TEMP = 4, 64, 128, 16, 16.0
B_HEADS, B_DK, B_DV = 4, 64, 128
C_HEADS, C_Q_RANK, C_KV_RANK, C_NOPE, C_ROPE, C_DV = 8, 384, 256, 128, 64, 128
D_FF = 2816
CONV_W = 3

HEADS = 4
HK = HEADS * 64
HV = HEADS * 128
ZA_W = 1536
ZG_W = 128
GATE_COL = 32
RCHUNK = 128
GLA_SAFE_SPAN = 60.0
L1_IN_PAD = 768
QCAT = 256

V7X_VMEM_BYTES = 64 * 1024 * 1024
VMEM_LIMIT = V7X_VMEM_BYTES - 8 * 1024 * 1024


def _cparams(sem):
    return pltpu.CompilerParams(dimension_semantics=sem, vmem_limit_bytes=VMEM_LIMIT)


def _dot(a, b):
    return jnp.dot(a, b, preferred_element_type=F32)


def _dot_nt(a, b):
    return lax.dot_general(a, b, (((1,), (1,)), ((), ())), preferred_element_type=F32)


def _dot_tn(a, b):
    return lax.dot_general(a, b, (((0,), (0,)), ((), ())), preferred_element_type=F32)


def _split3(x):
    hi = x.astype(BF16)
    r1 = x - hi.astype(F32)
    mid = r1.astype(BF16)
    lo = (r1 - mid.astype(F32)).astype(BF16)
    return hi, mid, lo


def _tri_dot(tri, x):
    hi, mid, lo = _split3(x)
    return _dot(tri, hi) + _dot(tri, mid) + _dot(tri, lo)


def _dot_hl(a, b):
    ah = a.astype(BF16)
    al = (a - ah.astype(F32)).astype(BF16)
    bh = b.astype(BF16)
    bl = (b - bh.astype(F32)).astype(BF16)
    return _dot(ah, bh) + _dot(ah, bl) + _dot(al, bh)


def _rms(x, g):
    return x * lax.rsqrt(jnp.mean(x * x, axis=-1, keepdims=True) + EPS) * g


def _sigmoid(x):
    return 1.0 / (1.0 + jnp.exp(-x))


def _silu(x):
    return x * _sigmoid(x)


def _log_sigmoid(x):
    return jnp.minimum(x, 0.0) - jnp.log(1.0 + jnp.exp(-jnp.abs(x)))


def _tri_masks(n):
    r = lax.broadcasted_iota(jnp.int32, (n, n), 0)
    c = lax.broadcasted_iota(jnp.int32, (n, n), 1)
    return c <= r, c >= r


def _mod_kernel(cond_ref, w_ref, b_ref, o_ref):
    s = _silu(cond_ref[...])
    o_ref[...] = _dot(s.astype(BF16), w_ref[...].astype(BF16)) + b_ref[...]


def _modulation(cond8, w_mod, b_mod):
    d = D_MODEL
    out = pl.pallas_call(
        _mod_kernel,
        out_shape=jax.ShapeDtypeStruct((N_MOD, 8, d), F32),
        grid=(N_MOD,),
        in_specs=[pl.BlockSpec((8, d), lambda j: (0, 0)),
                  pl.BlockSpec((d, d), lambda j: (0, j)),
                  pl.BlockSpec((1, d), lambda j: (0, j))],
        out_specs=pl.BlockSpec((None, 8, d), lambda j: (j, 0, 0)),
        compiler_params=_cparams(("arbitrary",)),
        name="modulation",
    )(cond8, w_mod, b_mod.reshape(1, N_MOD * d))
    return out.reshape(N_MOD, 8, 1, d)


MOD_SHIFT1, MOD_SCALE1, MOD_GATE1, MOD_SHIFT2, MOD_SCALE2, MOD_GATE2 = range(N_MOD)


def _mod_spec(which, first_row, per):
    return pl.BlockSpec((None, None, 1, D_MODEL), lambda i: (which, first_row + i // per, 0, 0))


L0_GROUP_ROWS = ((0, ZA_W), (ZA_W + 32, 2 * ZA_W + 32))
L0_LR_ROWS = (ZA_W, ZA_W + 32)
L0_GATE_ROWS = (2 * ZA_W + 32, 2 * ZA_W + 48)


def _p0_kernel(x_ref, g_ref, sc_ref, sh_ref, wt_ref, za_ref, zb_ref, zg_ref, w_ref):
    @pl.when(pl.program_id(0) == 0)
    def _():
        blk = 256
        for gi, (r0, r1) in enumerate(L0_GROUP_ROWS):
            for j in range((r1 - r0) // blk):
                rows = wt_ref[r0 + j * blk:r0 + (j + 1) * blk, :]
                w_ref[:, gi * ZA_W + j * blk:gi * ZA_W + (j + 1) * blk] = rows.T.astype(BF16)
        tail = jnp.concatenate([wt_ref[L0_LR_ROWS[0]:L0_LR_ROWS[1], :],
                                wt_ref[L0_GATE_ROWS[0]:L0_GATE_ROWS[1], :],
                                jnp.zeros((ZG_W - 48, D_MODEL), F32)], axis=0)
        w_ref[:, 2 * ZA_W:2 * ZA_W + ZG_W] = tail.T.astype(BF16)

    h = _rms(x_ref[...], g_ref[...]) * (1.0 + sc_ref[...]) + sh_ref[...]
    hb = h.astype(BF16)
    za_ref[...] = _dot(hb, w_ref[:, 0:ZA_W])
    zb_ref[...] = _dot(hb, w_ref[:, ZA_W:2 * ZA_W])
    zg_ref[...] = _dot(hb, w_ref[:, 2 * ZA_W:2 * ZA_W + ZG_W])


def _proj_l0(x, g, mod, first_row, groups, w_t, tm):
    t = x.shape[0]
    per = (t // tm) // groups
    row = lambda i: (i, 0)
    fixed = lambda i: (0, 0)
    return pl.pallas_call(
        _p0_kernel,
        out_shape=(jax.ShapeDtypeStruct((t, ZA_W), F32),
                   jax.ShapeDtypeStruct((t, ZA_W), F32),
                   jax.ShapeDtypeStruct((t, ZG_W), F32)),
        grid=(t // tm,),
        in_specs=[pl.BlockSpec((tm, D_MODEL), row),
                  pl.BlockSpec((1, D_MODEL), fixed),
                  _mod_spec(MOD_SCALE1, first_row, per),
                  _mod_spec(MOD_SHIFT1, first_row, per),
                  _resident(w_t.shape)],
        out_specs=(pl.BlockSpec((tm, ZA_W), row),
                   pl.BlockSpec((tm, ZA_W), row),
                   pl.BlockSpec((tm, ZG_W), row)),
        scratch_shapes=[pltpu.VMEM((D_MODEL, 2 * ZA_W + ZG_W), BF16)],
        compiler_params=_cparams(("arbitrary",)),
        name="proj_l0",
    )(x, g, mod, mod, w_t)


def _head_lane_mask(shape, h):
    lane = lax.broadcasted_iota(jnp.int32, shape, len(shape) - 1)
    return (lane >= h * 64) & (lane < (h + 1) * 64)


def _head_row(h):
    return jnp.where(_head_lane_mask((1, HK), h), 1.0, 0.0).astype(BF16)


def _gla_prepare(za_ref, la_ref, r0, rev, tri):
    L = RCHUNK
    rows = pl.ds(r0, L)
    b = _tri_dot(tri, la_ref[rows, :])
    mid = L // 2
    ref = b[mid:mid + 1, :]
    b_end = b[0:1, :] if rev else b[L - 1:L, :]
    q = za_ref[rows, 0:HK]
    k = za_ref[rows, HK:2 * HK]
    v = za_ref[rows, 2 * HK:2 * HK + HV].astype(BF16)
    qe = (q * (jnp.exp(b - ref) * (A_DK ** -0.5))).astype(BF16)
    ke = (k * jnp.exp(ref - b)).astype(BF16)
    qm = jnp.concatenate([qe * _head_row(h) for h in range(HEADS)], axis=0)
    rt = _dot_tn(v, ke)
    upd = None
    for h in range(HEADS):
        blk = rt[h * 128:(h + 1) * 128, :]
        blk = jnp.where(_head_lane_mask(blk.shape, h), blk, 0.0)
        upd = blk if upd is None else upd + blk
    return dict(rows=rows, qm=qm, ke=ke, v=v, upd=upd * jnp.exp(b_end - ref),
                decay=jnp.exp(b_end), eref=jnp.exp(ref))


def _gla_apply(items):
    L = RCHUNK
    sts = [st_ref[...] for _, st_ref, _, _ in items]
    o1s = []
    for (p, _, _, _), st in zip(items, sts):
        rhs = jnp.concatenate([p["ke"], (st * p["eref"]).astype(BF16)], axis=0)
        o1s.append(_dot_nt(p["qm"], rhs))
    for h in range(HEADS):
        for (p, _, o_ref, mask), o1 in zip(items, o1s):
            blk = o1[h * L:(h + 1) * L, :]
            sc = jnp.where(mask, blk[:, 0:L], 0.0).astype(BF16)
            o_ref[p["rows"], h * 128:(h + 1) * 128] = (
                blk[:, L:L + 128] + _dot(sc, p["v"][:, h * 128:(h + 1) * 128]))
    for (p, st_ref, _, _), st in zip(items, sts):
        st_ref[...] = st * p["decay"] + p["upd"]


def _gla_exact_scan(za_ref, la_ref, st_ref, o_ref, b_ref, n, rev, st_init):
    L = RCHUNK
    nc = n // L
    m_lo, m_up = _tri_masks(L)
    tri = jnp.where(m_up if rev else m_lo, 1.0, 0.0).astype(BF16)
    s_idx = lax.broadcasted_iota(jnp.int32, (L, 1), 0)
    krow = lax.broadcasted_iota(jnp.int32, (HK, 128), 0) // 64
    kcol = lax.broadcasted_iota(jnp.int32, (HK, 128), 1)
    head_sum = jnp.where(krow == kcol, 1.0, 0.0)
    hsub = lax.broadcasted_iota(jnp.int32, (8, HK), 0)
    hlane = lax.broadcasted_iota(jnp.int32, (8, HK), 1) // 64
    head_rows = jnp.where(hsub == hlane, 1.0, 0.0)
    st_ref[...] = st_init

    def chunk_body(ci, carry):
        c = nc - 1 - ci if rev else ci
        r0 = pl.multiple_of(c * L, L)
        rows = pl.ds(r0, L)
        b = _tri_dot(tri, la_ref[rows, :])
        b_ref[...] = b
        b_end = b[0:1, :] if rev else b[L - 1:L, :]
        k = za_ref[rows, HK:2 * HK]
        vb = za_ref[rows, 2 * HK:2 * HK + HV].astype(BF16)
        st = st_ref[...]
        stb = st.astype(BF16)

        def row_body(t, carry2):
            b_t = b_ref[pl.ds(t, 1), :]
            q_t = za_ref[pl.ds(r0 + t, 1), 0:HK] * (A_DK ** -0.5)
            valid = (s_idx >= t) if rev else (s_idx <= t)
            w = jnp.exp(jnp.minimum(b_t - b_ref[...], 0.0))
            p = jnp.where(valid, (q_t * k) * w, 0.0)
            sc = jnp.dot(p, head_sum, precision=lax.Precision.HIGHEST,
                         preferred_element_type=F32)
            pv = _dot_tn(sc.astype(BF16), vb)
            qm = (head_rows * (q_t * jnp.exp(b_t))).astype(BF16)
            inter = _dot_nt(qm, stb)
            o_ref[pl.ds(r0 + t, 1), :] = jnp.concatenate(
                [pv[h:h + 1, h * 128:(h + 1) * 128] + inter[h:h + 1, :] for h in range(HEADS)],
                axis=1)
            return carry2

        lax.fori_loop(0, L, row_body, 0)
        kd = (k * jnp.exp(b_end - b)).astype(BF16)
        rt = _dot_tn(vb, kd)
        upd = None
        for h in range(HEADS):
            blk = rt[h * 128:(h + 1) * 128, :]
            blk = jnp.where(_head_lane_mask(blk.shape, h), blk, 0.0)
            upd = blk if upd is None else upd + blk
        st_ref[...] = st * jnp.exp(b_end) + upd
        return carry

    lax.fori_loop(0, nc, chunk_body, 0)


T_M, T_MOUT, T_G, T_SRCW, T_MPREV, T_CARRY, T_USED = 0, 8, 16, 24, 32, 40, 48


def _lane_scan(x, op, fill, is_fwd, pos, n):
    k = 1
    while k < RCHUNK:
        pf = jnp.where(pos >= k, pltpu.roll(x, k, axis=1), fill)
        sf = jnp.where(pos < RCHUNK - k, pltpu.roll(x, n - k, axis=1), fill)
        x = op(x, jnp.where(is_fwd, pf, sf))
        k *= 2
    return x


def _mlstm_gate_tables(g, m0, gt_ref, tab_ref, col_ref, n):
    L = RCHUNK
    nc = n // L
    for c in range(nc):
        gt_ref[:, c * L:(c + 1) * L] = g[c * L:(c + 1) * L, :].T
        tab_ref[c, T_USED:, :] = jnp.zeros((ZG_W - T_USED, L), F32)
    gi = gt_ref[GATE_COL:GATE_COL + 8, :]
    lf = gt_ref[GATE_COL + 8:GATE_COL + 16, :]
    is_fwd = lax.broadcasted_iota(jnp.int32, (8, n), 0) < HEADS
    pos = lax.broadcasted_iota(jnp.int32, (8, n), 1) % L
    b = _lane_scan(lf, jnp.add, 0.0, is_fwd, pos, n)
    gg = gi - b
    cmax = _lane_scan(gg, jnp.maximum, -jnp.inf, is_fwd, pos, n)
    fwd_l = lax.broadcasted_iota(jnp.int32, (8, L), 0) < HEADS
    mp = m0
    for j in range(nc):
        cf, cb = j, nc - 1 - j

        def pick(x):
            return jnp.where(fwd_l, x[:, cf * L:(cf + 1) * L], x[:, cb * L:(cb + 1) * L])

        gj, bj, cj, lj = pick(gg), pick(b), pick(cmax), pick(lf)
        mx = jnp.maximum(mp, jnp.max(gj, axis=1, keepdims=True))
        mj = jnp.maximum(cj, mp)
        blocks = (mj, bj + mj, gj, jnp.exp(gj - mx),
                  jnp.broadcast_to(mp, (8, L)), jnp.broadcast_to(jnp.exp(mp - mx), (8, L)))
        for t, val in enumerate(blocks):
            tab_ref[cf, 8 * t:8 * t + HEADS, :] = val[0:HEADS]
            tab_ref[cb, 8 * t + HEADS:8 * t + 8, :] = val[HEADS:8]
        mp = jnp.sum(lj, axis=1, keepdims=True) + mx
    for c in range(nc):
        col_ref[c * L:(c + 1) * L, :] = tab_ref[c].T
    return mp


def _mlstm_prepare(zb_ref, tab_ref, col_ref, c, d, mask):
    L = RCHUNK
    rows = pl.ds(pl.multiple_of(c * L, L), L)
    qb = zb_ref[rows, 0:HK].astype(BF16)
    kb = (zb_ref[rows, HK:2 * HK] * (B_DK ** -0.5)).astype(BF16)
    v = zb_ref[rows, 2 * HK:2 * HK + HV]
    vb = v.astype(BF16)
    vt = v.T
    qm = jnp.concatenate([qb * _head_row(h) for h in range(HEADS)], axis=0)
    s = _dot_nt(qm, kb)
    ones = jnp.ones((L, 128), BF16)
    pv, w_inter, e_negm = [], [], []
    upd = None
    carry = None
    for h in range(HEADS):
        r = 4 * d + h
        m_rep = jnp.broadcast_to(col_ref[rows, T_M + r:T_M + r + 1], (L, 128))
        mout_rep = jnp.broadcast_to(col_ref[rows, T_MOUT + r:T_MOUT + r + 1], (L, 128))
        g_row = tab_ref[c, T_G + r:T_G + r + 1, :]
        w = jnp.exp(jnp.where(mask, g_row - m_rep, -jnp.inf))
        qk = (s[h * L:(h + 1) * L, :] * w).astype(BF16)
        v1 = jnp.concatenate([vb[:, h * 128:(h + 1) * 128], ones], axis=1)
        pv.append(_dot(qk, v1))
        w_inter.append(jnp.exp(tab_ref[c, T_MPREV + r:T_MPREV + r + 1, :] - m_rep))
        e_negm.append(jnp.exp(-mout_rep))
        sw = tab_ref[c, T_SRCW + r:T_SRCW + r + 1, :]
        lhs = jnp.concatenate([vt[h * 128:(h + 1) * 128, :] * sw, jnp.broadcast_to(sw, (128, L))],
                              axis=0).astype(BF16)
        blk = _dot(lhs, kb)
        blk = jnp.where(_head_lane_mask(blk.shape, h), blk, 0.0)
        upd = blk if upd is None else upd + blk
        cr = tab_ref[c, T_CARRY + r:T_CARRY + r + 1, :]
        cr = jnp.where(_head_lane_mask((1, HK), h), jnp.concatenate([cr, cr], axis=1), 0.0)
        carry = cr if carry is None else carry + cr
    return dict(rows=rows, qm=qm, pv=pv, w_inter=w_inter, e_negm=e_negm, upd=upd, carry=carry)


def _mlstm_apply(items):
    L = RCHUNK
    sts = [st_ref[...] for _, st_ref, _ in items]
    a_all = [_dot_nt(p["qm"], st.astype(BF16)) for (p, _, _), st in zip(items, sts)]
    for h in range(HEADS):
        for (p, _, h_ref), a in zip(items, a_all):
            blk = a[h * L:(h + 1) * L, :]
            pv, w, e = p["pv"][h], p["w_inter"][h], p["e_negm"][h]
            num = w * blk[:, 0:128] + pv[:, 0:128]
            den = w * blk[:, 128:256] + pv[:, 128:256]
            h_ref[p["rows"], h * 128:(h + 1) * 128] = num / jnp.maximum(jnp.abs(den), e)
    for (p, st_ref, _), st in zip(items, sts):
        st_ref[...] = p["carry"] * st + p["upd"]


def _rec_kernel(*refs, n, has_init):
    it = iter(refs)

    def take(k):
        return [next(it) for _ in range(k)]

    za_all, zb_all, zg_all, zgn_all = take(4)
    wgf_ref, wgb_ref, bgf_ref, bgb_ref, gna_ref, bg_ref, gnb_ref = take(7)
    if has_init:
        init_all = take(7)
    out_all = take(8)
    (gt_all,) = take(1)
    set_a_all = take(6)
    set_b_all = take(6)
    work_all = take(9)
    L = RCHUNK
    nc = n // L
    step = pl.program_id(0)
    elems = []
    for e in range(za_all.shape[0]):
        el = dict(za=za_all.at[e], zb=zb_all.at[e], zg=zg_all.at[e], zgn=zgn_all.at[e],
                  gt=gt_all.at[e], set_a=[r.at[e] for r in set_a_all],
                  set_b=[r.at[e] for r in set_b_all])
        (el["oa"], el["sf"], el["sb"], el["om"], el["cf"], el["cb"], el["n_out"],
         el["m_out"]) = [r.at[e] for r in out_all]
        (el["of"], el["ob"], el["hf"], el["hb"], el["gsf"], el["gsb"], el["msf"],
         el["msb"], el["b_exact"]) = [r.at[e] for r in work_all]
        if has_init:
            el["init"] = [r.at[e] for r in init_all]
        elems.append(el)

    def fill(gate_ref, m0, bufs, gt_ref):
        laf_ref, lab_ref, tab_ref, col_ref, mf_ref, span_ref = bufs
        zg = gate_ref[...]
        laf = _log_sigmoid(_dot_hl(zg, wgf_ref[...]) + bgf_ref[...]) / A_GATE_TEMP
        lab = _log_sigmoid(_dot_hl(zg, wgb_ref[...]) + bgb_ref[...]) / A_GATE_TEMP
        laf_ref[...] = laf
        lab_ref[...] = lab
        span = None
        for la in (laf, lab):
            for c in range(nc):
                tot = -jnp.sum(la[c * L:(c + 1) * L, :], axis=0, keepdims=True)
                span = tot if span is None else jnp.maximum(span, tot)
        span_ref[...] = jnp.broadcast_to(jnp.max(span, axis=1, keepdims=True), (8, 128))
        g = zg + bg_ref[...]
        lane = lax.broadcasted_iota(jnp.int32, g.shape, 1)
        is_forget = (lane >= GATE_COL + 8) & (lane < GATE_COL + 16)
        g = jnp.where(is_forget, _log_sigmoid(g), g)
        m_fin = _mlstm_gate_tables(g, m0, gt_ref, tab_ref, col_ref, n)
        mf_ref[...] = jnp.broadcast_to(m_fin, (8, 128))

    for el in elems:
        if has_init:
            sf0_ref, sb0_ref, cf0_ref, cb0_ref, n0_ref, m0_ref, m0n_ref = el["init"]
            el["m0"], el["m0_next"] = m0_ref[:, 0:1], m0n_ref[:, 0:1]
            el["gsf"][...] = sf0_ref[...]
            el["gsb"][...] = sb0_ref[...]
            el["msf"][...] = jnp.concatenate(
                [cf0_ref[...], jnp.broadcast_to(n0_ref[0:1, :], (128, HK))], axis=0)
            el["msb"][...] = jnp.concatenate(
                [cb0_ref[...], jnp.broadcast_to(n0_ref[1:2, :], (128, HK))], axis=0)
        else:
            el["m0"] = el["m0_next"] = jnp.zeros((8, 1), F32)
            for name in ("gsf", "gsb", "msf", "msb"):
                el[name][...] = jnp.zeros(el[name].shape, F32)

    @pl.when(step == 0)
    def _():
        for el in elems:
            fill(el["zg"], el["m0"], el["set_a"], el["gt"])

    m_lo, m_up = _tri_masks(L)
    tri_lo = jnp.where(m_lo, 1.0, 0.0).astype(BF16)
    tri_up = jnp.where(m_up, 1.0, 0.0).astype(BF16)
    gna = gna_ref[...]
    gnb = gnb_ref[...]

    def gla_out(el, rows, h):
        cols = slice(h * 128, (h + 1) * 128)
        gcols = slice(2 * HK + HV + h * 128, 2 * HK + HV + (h + 1) * 128)
        o = el["of"][rows, cols] + el["ob"][rows, cols]
        el["oa"][rows, cols] = (_rms(o, gna) * _silu(el["za"][rows, gcols])).astype(BF16)

    def out_body(c, carry):
        rows = pl.ds(pl.multiple_of(c * L, L), L)
        for h in range(HEADS):
            cols = slice(h * 128, (h + 1) * 128)
            gcols = slice(2 * HK + HV + h * 128, 2 * HK + HV + (h + 1) * 128)
            for el in elems:
                gla_out(el, rows, h)
                hh = el["hf"][rows, cols] + el["hb"][rows, cols]
                el["om"][rows, cols] = (
                    _sigmoid(el["zb"][rows, gcols]) * _rms(hh, gnb)).astype(BF16)
        return carry

    def run(cur, nxt):
        for el in elems:
            fill(el["zgn"], el["m0_next"], el[nxt], el["gt"])

        def scan_body(gidx, carry):
            waves = []
            for w in range(2):
                c = 2 * gidx + w
                cb = nc - 1 - c
                gla_items, mlstm_items = [], []
                for el in elems:
                    laf_ref, lab_ref, tab_ref, col_ref = el[cur][0:4]
                    gla_items += [
                        (_gla_prepare(el["za"], laf_ref, pl.multiple_of(c * L, L), False, tri_lo),
                         el["gsf"], el["of"], m_lo),
                        (_gla_prepare(el["za"], lab_ref, pl.multiple_of(cb * L, L), True, tri_up),
                         el["gsb"], el["ob"], m_up)]
                    mlstm_items += [
                        (_mlstm_prepare(el["zb"], tab_ref, col_ref, c, 0, m_lo), el["msf"], el["hf"]),
                        (_mlstm_prepare(el["zb"], tab_ref, col_ref, cb, 1, m_up), el["msb"], el["hb"])]
                waves.append((gla_items, mlstm_items))
            for gla_items, mlstm_items in waves:
                _gla_apply(gla_items)
                _mlstm_apply(mlstm_items)
            return carry

        if nc == 2:
            scan_body(0, 0)
        else:
            lax.fori_loop(0, nc // 2, scan_body, 0)
        if nc == 2:
            for c in range(nc):
                out_body(c, 0)
        else:
            lax.fori_loop(0, nc, out_body, 0)
        for el in elems:
            el["m_out"][...] = el[cur][4][...]
        for el in elems:
            laf_ref, lab_ref = el[cur][0:2]
            too_fast = el[cur][5][0, 0] > GLA_SAFE_SPAN

            @pl.when(too_fast)
            def _(el=el, laf_ref=laf_ref, lab_ref=lab_ref):
                zero = jnp.zeros((128, HK), F32)
                init_f = el["init"][0][...] if has_init else zero
                init_b = el["init"][1][...] if has_init else zero
                _gla_exact_scan(el["za"], laf_ref, el["gsf"], el["of"], el["b_exact"], n, False, init_f)
                _gla_exact_scan(el["za"], lab_ref, el["gsb"], el["ob"], el["b_exact"], n, True, init_b)

                def redo_out(c, carry):
                    rows = pl.ds(pl.multiple_of(c * L, L), L)
                    for h in range(HEADS):
                        gla_out(el, rows, h)
                    return carry

                lax.fori_loop(0, nc, redo_out, 0)

    @pl.when(step % 2 == 0)
    def _():
        run("set_a", "set_b")

    @pl.when(step % 2 == 1)
    def _():
        run("set_b", "set_a")

    for el in elems:
        gsf_t = el["gsf"][...].T
        gsb_t = el["gsb"][...].T
        msf = el["msf"][...]
        msb = el["msb"][...]
        ctf_t = msf[0:128, :].T
        ctb_t = msb[0:128, :].T
        for h in range(HEADS):
            rows = slice(h * 64, (h + 1) * 64)
            el["sf"][h] = gsf_t[rows, :]
            el["sb"][h] = gsb_t[rows, :]
            el["cf"][h] = ctf_t[rows, :]
            el["cb"][h] = ctb_t[rows, :]
        el["n_out"][...] = jnp.concatenate(
            [msf[128:129, :], msb[128:129, :], jnp.zeros((6, HK), F32)], axis=0)


def _recurrent(za, zb, zg, wgf, wgb, bgf, bgb, gna, bg, gnb, init, bt):
    bsz, n, _ = za.shape
    has_init = init is not None
    nc = n // RCHUNK
    steps = bsz // bt
    per_b3 = lambda b: (b, 0, 0)
    next_b3 = lambda b: (jnp.minimum(b + 1, steps - 1), 0, 0)
    per_b4 = lambda b: (b, 0, 0, 0)
    fixed = lambda b: (0, 0)
    in_specs = [pl.BlockSpec((bt, n, ZA_W), per_b3),
                pl.BlockSpec((bt, n, ZA_W), per_b3),
                pl.BlockSpec((bt, n, ZG_W), per_b3),
                pl.BlockSpec((bt, n, ZG_W), next_b3),
                pl.BlockSpec((ZG_W, HK), fixed),
                pl.BlockSpec((ZG_W, HK), fixed),
                pl.BlockSpec((1, HK), fixed),
                pl.BlockSpec((1, HK), fixed),
                pl.BlockSpec((1, 128), fixed),
                pl.BlockSpec((1, ZG_W), fixed),
                pl.BlockSpec((1, 128), fixed)]
    args = [za, zb, zg, zg, wgf, wgb, bgf, bgb, gna, bg, gnb]
    if has_init:
        sf0, sb0, cf0, cb0, n0, m0 = init
        in_specs += [pl.BlockSpec((bt, 128, HK), per_b3)] * 4
        in_specs += [pl.BlockSpec((bt, 8, HK), per_b3), pl.BlockSpec((bt, 8, 128), per_b3),
                     pl.BlockSpec((bt, 8, 128), next_b3)]
        args += [sf0, sb0, cf0, cb0, n0, m0, m0]
    state = jax.ShapeDtypeStruct((bsz, HEADS, 64, 128), F32)
    state_spec = pl.BlockSpec((bt, HEADS, 64, 128), per_b4)

    def vmem(*shape):
        return pltpu.VMEM((bt,) + shape, F32)

    gate_set = [vmem(n, HK), vmem(n, HK), vmem(nc, ZG_W, RCHUNK), vmem(n, ZG_W), vmem(8, 128),
                vmem(8, 128)]
    return pl.pallas_call(
        functools.partial(_rec_kernel, n=n, has_init=has_init),
        out_shape=(jax.ShapeDtypeStruct((bsz, n, HV), BF16), state, state,
                   jax.ShapeDtypeStruct((bsz, n, HV), BF16), state, state,
                   jax.ShapeDtypeStruct((bsz, 8, HK), F32),
                   jax.ShapeDtypeStruct((bsz, 8, 128), F32)),
        grid=(steps,),
        in_specs=in_specs,
        out_specs=(pl.BlockSpec((bt, n, HV), per_b3), state_spec, state_spec,
                   pl.BlockSpec((bt, n, HV), per_b3), state_spec, state_spec,
                   pl.BlockSpec((bt, 8, HK), per_b3),
                   pl.BlockSpec((bt, 8, 128), per_b3)),
        scratch_shapes=([vmem(ZG_W, n)] + gate_set + gate_set + [vmem(n, HV)] * 4
                        + [vmem(128, HK)] * 2 + [vmem(2 * 128, HK)] * 2 + [vmem(RCHUNK, HK)]),
        compiler_params=_cparams(("arbitrary",)),
        name="recurrent",
    )(*args)


def _mix_residual(parts, wo_ref, x_ref, gt1_ref, gpost1_ref, gpre2_ref, sc2_ref, sh2_ref):
    m = None
    off = 0
    for p in parts:
        kp = p.shape[1]
        t = _dot(p[...], wo_ref[off:off + kp, :])
        m = t if m is None else m + t
        off += kp
    x1 = x_ref[...] + gt1_ref[...] * _rms(m, gpost1_ref[...])
    hb = (_rms(x1, gpre2_ref[...]) * (1.0 + sc2_ref[...]) + sh2_ref[...]).astype(BF16)
    return x1, hb


def _ffn_gated_up(hb, wu_ref, cw_ref, cb_ref, act_ref, seq, tf):
    tm = hb.shape[0]
    pos = lax.broadcasted_iota(jnp.int32, (tm, 1), 0) % seq
    first = pos == 0
    last = pos == seq - 1

    def conv(u, cols):
        prev = jnp.where(first, 0.0, pltpu.roll(u, 1, axis=0))
        nxt = jnp.where(last, 0.0, pltpu.roll(u, tm - 1, axis=0))
        return (cw_ref[0:1, cols] * prev + cw_ref[1:2, cols] * u + cw_ref[2:3, cols] * nxt
                + cb_ref[:, cols])

    for j in range(D_FF // tf):
        ca = slice(j * tf, (j + 1) * tf)
        cg = slice(D_FF + j * tf, D_FF + (j + 1) * tf)
        a = conv(_dot(hb, wu_ref[:, ca]), ca)
        g = conv(_dot(hb, wu_ref[:, cg]), cg)
        act_ref[:, ca] = (_silu(g) * a).astype(BF16)


def _mix_ffn_kernel(*refs, n_parts, seq, tf):
    parts = refs[:n_parts]
    (wo_ref, x_ref, gt1_ref, gpost1_ref, gpre2_ref, sc2_ref, sh2_ref,
     wu_ref, cw_ref, cb_ref, wd_ref, gt2_ref, gpost2_ref, o_ref, act_ref) = refs[n_parts:]
    x1, hb = _mix_residual(parts, wo_ref, x_ref, gt1_ref, gpost1_ref, gpre2_ref, sc2_ref, sh2_ref)
    o_ref[...] = x1
    _ffn_gated_up(hb, wu_ref, cw_ref, cb_ref, act_ref, seq, tf)
    y = _dot(act_ref[...], wd_ref[...])
    o_ref[...] = o_ref[...] + gt2_ref[...] * _rms(y, gpost2_ref[...])


def _resident(shape):
    return pl.BlockSpec(shape, lambda *_: (0,) * len(shape), pipeline_mode=pl.Buffered(1))


def _mix_ffn(parts, w_out, x, mod, first_row, groups, gpost1, gpre2, w_up, conv_w, conv_b, w_down,
             gpost2, tm, seq, tf=256):
    t = x.shape[0]
    per = (t // tm) // groups
    row = lambda i: (i, 0)
    vec = _resident((1, D_MODEL))
    in_specs = [pl.BlockSpec((tm, p.shape[1]), row) for p in parts]
    in_specs += [_resident(w_out.shape), pl.BlockSpec((tm, D_MODEL), row),
                 _mod_spec(MOD_GATE1, first_row, per), vec, vec,
                 _mod_spec(MOD_SCALE2, first_row, per), _mod_spec(MOD_SHIFT2, first_row, per),
                 _resident(w_up.shape), _resident(conv_w.shape), _resident(conv_b.shape),
                 _resident(w_down.shape), _mod_spec(MOD_GATE2, first_row, per), vec]
    return pl.pallas_call(
        functools.partial(_mix_ffn_kernel, n_parts=len(parts), seq=seq, tf=tf),
        out_shape=jax.ShapeDtypeStruct((t, D_MODEL), F32),
        grid=(t // tm,),
        in_specs=in_specs,
        out_specs=pl.BlockSpec((tm, D_MODEL), row),
        scratch_shapes=[pltpu.VMEM((tm, D_FF), BF16)],
        compiler_params=_cparams(("arbitrary",)),
        name="mix_ffn",
    )(*parts, w_out, x, mod, gpost1, gpre2, mod, mod, w_up, conv_w, conv_b, w_down, mod, gpost2)


def _rope(x, cos, sin_signed):
    lane = lax.broadcasted_iota(jnp.int32, x.shape, 1)
    lower = (lane % 32) < 16
    partner = jnp.where(lower, pltpu.roll(x, 128 - 16, axis=1), pltpu.roll(x, 16, axis=1))
    return x * cos + partner * sin_signed


def _p1_kernel(*refs, rope):
    if rope:
        (x_ref, g_ref, sc_ref, sh_ref, w_ref, gq_ref, wq_ref, gkv_ref, cos_ref, sin_ref,
         q_ref, ckv_ref, kr_ref) = refs
    else:
        (x_ref, g_ref, sc_ref, sh_ref, w_ref, gq_ref, wq_ref, gkv_ref,
         q_ref, ckv_ref, kr_ref) = refs
    h = _rms(x_ref[...], g_ref[...]) * (1.0 + sc_ref[...]) + sh_ref[...]
    z = _dot(h.astype(BF16), w_ref[...])
    qa = _rms(z[:, 0:C_Q_RANK], gq_ref[...])
    q = _dot(qa.astype(BF16), wq_ref[...])
    ckv_ref[...] = _rms(z[:, C_Q_RANK:C_Q_RANK + C_KV_RANK], gkv_ref[...])
    kr = z[:, C_Q_RANK + C_KV_RANK:L1_IN_PAD]
    if rope:
        cos = cos_ref[...]
        sin = sin_ref[...]
        kr = _rope(kr, cos, sin)
    kr_ref[...] = kr[:, 0:C_ROPE]
    for hd in range(C_HEADS):
        c0 = hd * QCAT
        q_ref[:, c0:c0 + 128] = q[:, c0:c0 + 128].astype(q_ref.dtype)
        qr = q[:, c0 + 128:c0 + 256]
        if rope:
            qr = _rope(qr, cos, sin)
        q_ref[:, c0 + 128:c0 + 256] = qr.astype(q_ref.dtype)


def _proj_l1(x, g, mod, first_row, groups, w, gq, wq, gkv, tables, tm):
    t = x.shape[0]
    per = (t // tm) // groups
    row = lambda i: (i, 0)
    fixed = lambda i: (0, 0)
    in_specs = [pl.BlockSpec((tm, D_MODEL), row),
                pl.BlockSpec((1, D_MODEL), fixed),
                _mod_spec(MOD_SCALE1, first_row, per),
                _mod_spec(MOD_SHIFT1, first_row, per),
                pl.BlockSpec(w.shape, fixed),
                pl.BlockSpec((1, C_Q_RANK), fixed),
                pl.BlockSpec(wq.shape, fixed),
                pl.BlockSpec((1, C_KV_RANK), fixed)]
    args = [x, g, mod, mod, w, gq, wq, gkv]
    rope = tables is not None
    if rope:
        n = tables[0].shape[0]
        pos = lambda i: (i % (n // tm), 0)
        in_specs += [pl.BlockSpec((tm, 128), pos)] * 2
        args += list(tables)
    return pl.pallas_call(
        functools.partial(_p1_kernel, rope=rope),
        out_shape=(jax.ShapeDtypeStruct((t, C_HEADS * QCAT), BF16),
                   jax.ShapeDtypeStruct((t, C_KV_RANK), F32),
                   jax.ShapeDtypeStruct((t, C_ROPE), F32)),
        grid=(t // tm,),
        in_specs=in_specs,
        out_specs=(pl.BlockSpec((tm, C_HEADS * QCAT), row),
                   pl.BlockSpec((tm, C_KV_RANK), row),
                   pl.BlockSpec((tm, C_ROPE), row)),
        compiler_params=_cparams(("arbitrary",)),
        name="proj_l1",
    )(*args)


def _attn_kernel(*refs, n_parts, scale):
    q_ref = refs[0]
    kv_parts = refs[1:1 + 2 * n_parts]
    w_ref, o_ref, k_sc, v_sc = refs[1 + 2 * n_parts:]
    bt = q_ref.shape[0]

    @pl.when(pl.program_id(1) == 0)
    def _():
        off = 0
        for p in range(n_parts):
            ckv_ref, kr_ref = kv_parts[2 * p], kv_parts[2 * p + 1]
            nk = ckv_ref.shape[1]
            ones = jnp.ones((nk, C_DV), BF16)
            for e in range(bt):
                kv = _dot(ckv_ref[e].astype(BF16), w_ref[...])
                kr = kr_ref[e].astype(BF16)
                kr2 = jnp.concatenate([kr, kr], axis=1)
                for hd in range(C_HEADS):
                    v0 = C_HEADS * C_NOPE + hd * C_DV
                    k_sc[e, hd, off:off + nk, 0:128] = kv[:, hd * 128:(hd + 1) * 128].astype(BF16)
                    k_sc[e, hd, off:off + nk, 128:256] = kr2
                    v_sc[e, hd, off:off + nk, 0:C_DV] = kv[:, v0:v0 + C_DV].astype(BF16)
                    v_sc[e, hd, off:off + nk, C_DV:2 * C_DV] = ones
            off += nk

    for hd in range(C_HEADS):
        for e in range(bt):
            t = _dot_nt(q_ref[e, :, hd * QCAT:(hd + 1) * QCAT], k_sc[e, hd]) * (scale * LOG2E)
            p = jnp.exp2(t - jnp.max(t, axis=1, keepdims=True))
            ov = _dot(p.astype(BF16), v_sc[e, hd])
            o_ref[e, :, hd * C_DV:(hd + 1) * C_DV] = (
                ov[:, 0:C_DV] / ov[:, C_DV:2 * C_DV]).astype(o_ref.dtype)


def _attention(q, kv_parts, w_kvb, tq, bt):
    bsz, n, _ = q.shape
    nk = sum(p[0].shape[1] for p in kv_parts)
    in_specs = [pl.BlockSpec((bt, tq, C_HEADS * QCAT), lambda b, i: (b, i, 0))]
    args = [q]
    for ckv, kr in kv_parts:
        in_specs.append(pl.BlockSpec((bt, ckv.shape[1], C_KV_RANK), lambda b, i: (b, 0, 0)))
        in_specs.append(pl.BlockSpec((bt, kr.shape[1], C_ROPE), lambda b, i: (b, 0, 0)))
        args += [ckv, kr]
    in_specs.append(pl.BlockSpec(w_kvb.shape, lambda b, i: (0, 0)))
    args.append(w_kvb)
    return pl.pallas_call(
        functools.partial(_attn_kernel, n_parts=len(kv_parts),
                          scale=(C_NOPE + C_ROPE) ** -0.5),
        out_shape=jax.ShapeDtypeStruct((bsz, n, C_HEADS * C_DV), BF16),
        grid=(bsz // bt, n // tq),
        in_specs=in_specs,
        out_specs=pl.BlockSpec((bt, tq, C_HEADS * C_DV), lambda b, i: (b, i, 0)),
        scratch_shapes=[pltpu.VMEM((bt, C_HEADS, nk, QCAT), BF16),
                        pltpu.VMEM((bt, C_HEADS, nk, 2 * C_DV), BF16)],
        compiler_params=_cparams(("arbitrary", "arbitrary")),
        name="mla_attention",
    )(*args)


def _l1_wq_layout(w_qb):
    blocks = []
    zero = jnp.zeros((w_qb.shape[0], C_ROPE), w_qb.dtype)
    for hd in range(C_HEADS):
        base = hd * (C_NOPE + C_ROPE)
        nope = w_qb[:, base:base + C_NOPE]
        rope = w_qb[:, base + C_NOPE:base + C_NOPE + C_ROPE]
        blocks += [nope, rope, zero] if hd % 2 == 0 else [nope, zero, rope]
    return jnp.concatenate(blocks, axis=1).astype(BF16)


def _l1_wkvb_layout(w_kvb):
    w = w_kvb.reshape(C_KV_RANK, C_HEADS, C_NOPE + C_DV)
    k = w[:, :, :C_NOPE].reshape(C_KV_RANK, C_HEADS * C_NOPE)
    v = w[:, :, C_NOPE:].reshape(C_KV_RANK, C_HEADS * C_DV)
    return jnp.concatenate([k, v], axis=1).astype(BF16)


def _rope_tables(n):
    pos = np.arange(n)
    row = (pos // GRID_W).astype(np.float64)
    col = (pos % GRID_W).astype(np.float64)
    half = C_ROPE // 2
    inv = 1.0 / (ROPE_THETA ** (np.arange(0, half, 2, dtype=np.float64) / half))
    ang_r = row[:, None] * inv[None, :]
    ang_c = col[:, None] * inv[None, :]
    cos64 = np.concatenate([np.cos(ang_r)] * 2 + [np.cos(ang_c)] * 2, axis=1)
    sin64 = np.concatenate([-np.sin(ang_r), np.sin(ang_r), -np.sin(ang_c), np.sin(ang_c)], axis=1)
    return (jnp.asarray(np.concatenate([cos64] * 2, axis=1), F32),
            jnp.asarray(np.concatenate([sin64] * 2, axis=1), F32))


def _gate_weight(w, first_row):
    return jnp.pad(w, ((first_row, ZG_W - first_row - A_GATE_RANK), (0, 0)))


def _state_t(s):
    b = s.shape[0]
    return s.transpose(0, 3, 1, 2).reshape(b, 128, HK)


def _n_rows(n_fwd, n_bwd):
    b = n_fwd.shape[0]
    rows = jnp.zeros((b, 8, HK), F32)
    return rows.at[:, 0, :].set(n_fwd.reshape(b, HK)).at[:, 1, :].set(n_bwd.reshape(b, HK))


def _m_rows(m_fwd, m_bwd):
    m = jnp.concatenate([m_fwd, m_bwd], axis=1)
    return jnp.broadcast_to(m[:, :, None], (m.shape[0], 8, 128))


def kernel(x_prompt, x_sample, state_l0_gla_fwd, state_l0_gla_bwd, state_l0_mlstm_c_fwd, state_l0_mlstm_n_fwd, state_l0_mlstm_m_fwd, state_l0_mlstm_c_bwd, state_l0_mlstm_n_bwd, state_l0_mlstm_m_bwd, cache_l1_ckv, cache_l1_krope, c, c_ctx, l0_w_mod, l0_b_mod, l0_g_pre_mix, l0_g_post_mix, l0_g_pre_ffn, l0_g_post_ffn, l0_w_in, l0_gla_w_gate_f, l0_gla_b_gate_f, l0_gla_w_gate_b, l0_gla_b_gate_b, l0_gla_g_norm, l0_mlstm_b_gates, l0_mlstm_g_norm, l0_w_out, l0_ffn_w_up, l0_ffn_conv_w, l0_ffn_conv_b, l0_ffn_w_down, l1_w_mod, l1_b_mod, l1_g_pre_mix, l1_g_post_mix, l1_g_pre_ffn, l1_g_post_ffn, l1_w_in, l1_g_q_norm, l1_w_qb, l1_g_kv_norm, l1_w_kvb, l1_w_out, l1_ffn_w_up, l1_ffn_conv_w, l1_ffn_conv_b, l1_ffn_w_down):
    bp, sp, d = x_prompt.shape
    bs, ss, _ = x_sample.shape
    row = lambda v: v.reshape(1, -1)

    cond8 = jnp.concatenate([c_ctx[None, :], c, jnp.zeros((8 - 1 - bs, d), F32)], axis=0)
    mods = (_modulation(cond8, l0_w_mod, l0_b_mod), _modulation(cond8, l1_w_mod, l1_b_mod))

    paths = {
        "ctx": dict(x=x_prompt.reshape(bp * sp, d), b=bp, n=sp, first_row=0, groups=1, tm=512),
        "lat": dict(x=x_sample.reshape(bs * ss, d), b=bs, n=ss, first_row=1, groups=bs, tm=512),
    }

    w_in0 = l0_w_in.T
    wgf = _gate_weight(l0_gla_w_gate_f, 0)
    wgb = _gate_weight(l0_gla_w_gate_b, A_GATE_RANK)
    bg = jnp.pad(l0_mlstm_b_gates, (GATE_COL, ZG_W - GATE_COL - 16)).reshape(1, ZG_W)
    w_out0 = l0_w_out.astype(BF16)
    w_in1 = jnp.pad(l1_w_in, ((0, 0), (0, L1_IN_PAD - l1_w_in.shape[1]))).astype(BF16)
    w_q1 = _l1_wq_layout(l1_w_qb)
    w_kvb1 = _l1_wkvb_layout(l1_w_kvb)
    w_out1 = l1_w_out.astype(BF16)
    ffn = ((l0_ffn_w_up.astype(BF16), l0_ffn_conv_w, row(l0_ffn_conv_b), l0_ffn_w_down.astype(BF16)),
           (l1_ffn_w_up.astype(BF16), l1_ffn_conv_w, row(l1_ffn_conv_b), l1_ffn_w_down.astype(BF16)))
    norms = ((row(l0_g_pre_mix), row(l0_g_post_mix), row(l0_g_pre_ffn), row(l0_g_post_ffn)),
             (row(l1_g_pre_mix), row(l1_g_post_mix), row(l1_g_pre_ffn), row(l1_g_post_ffn)))

    gla_init = (_state_t(state_l0_gla_fwd), _state_t(state_l0_gla_bwd))
    mlstm_init = (_state_t(state_l0_mlstm_c_fwd), _state_t(state_l0_mlstm_c_bwd),
                  _n_rows(state_l0_mlstm_n_fwd, state_l0_mlstm_n_bwd),
                  _m_rows(state_l0_mlstm_m_fwd, state_l0_mlstm_m_bwd))
    tables = _rope_tables(ss)

    results = {}
    for name, p in paths.items():
        x, b, n, tm = p["x"], p["b"], p["n"], p["tm"]
        first_row, groups = p["first_row"], p["groups"]
        is_ctx = name == "ctx"
        ffn_tm = tm if is_ctx else n
        g_pre, g_post, g_pre2, g_post2 = norms[0]

        za, zb, zg = _proj_l0(x, g_pre, mods[0], first_row, groups, w_in0, tm)
        za, zb, zg = (t.reshape(b, n, -1) for t in (za, zb, zg))
        oa, sa_f, sa_b, om, c_f, c_b, n_rows, m_rows = _recurrent(
            za, zb, zg, wgf, wgb, row(l0_gla_b_gate_f), row(l0_gla_b_gate_b), row(l0_gla_g_norm),
            bg, row(l0_mlstm_g_norm), None if is_ctx else gla_init + mlstm_init,
            2 if is_ctx else 1)
        x = _mix_ffn([oa.reshape(b * n, HV), om.reshape(b * n, HV)], w_out0, x, mods[0], first_row,
                     groups, g_post, g_pre2, *ffn[0], g_post2, ffn_tm, n)

        g_pre, g_post, g_pre2, g_post2 = norms[1]
        q, ckv, kr = _proj_l1(x, g_pre, mods[1], first_row, groups, w_in1, row(l1_g_q_norm), w_q1,
                              row(l1_g_kv_norm), None if is_ctx else tables,
                              2 * tm if is_ctx else tm)
        ckv3 = ckv.reshape(b, n, C_KV_RANK)
        kr3 = kr.reshape(b, n, C_ROPE)
        kv_parts = [(ckv3, kr3)] if is_ctx else [(cache_l1_ckv, cache_l1_krope), (ckv3, kr3)]
        o = _attention(q.reshape(b, n, -1), kv_parts, w_kvb1, 256 if is_ctx else 512,
                       4 if is_ctx else 1)
        x = _mix_ffn([o.reshape(b * n, C_HEADS * C_DV)], w_out1, x, mods[1], first_row,
                     groups, g_post, g_pre2, *ffn[1], g_post2, ffn_tm, n)

        results[name] = dict(y=x.reshape(b, n, d), gla=(sa_f, sa_b), c=(c_f, c_b),
                             n=n_rows, m=m_rows, ckv=ckv3, kr=kr3)

    r = results["ctx"]
    n_fwd = r["n"][:, 0, :].reshape(bp, B_HEADS, B_DK)
    n_bwd = r["n"][:, 1, :].reshape(bp, B_HEADS, B_DK)
    m_fwd = r["m"][:, 0:B_HEADS, 0]
    m_bwd = r["m"][:, B_HEADS:2 * B_HEADS, 0]
    return (r["y"], results["lat"]["y"], r["gla"][0], r["gla"][1],
            r["c"][0], n_fwd, m_fwd, r["c"][1], n_bwd, m_bwd, r["ckv"], r["kr"])
```

```python
import functools

import numpy as np
import jax
import jax.numpy as jnp
from jax import lax
from jax.experimental import pallas as pl
from jax.experimental.pallas import tpu as pltpu

F32 = jnp.float32
BF16 = jnp.bfloat16

D_MODEL = 1024
EPS = 1e-6
LOG2E = 1.4426950408889634
N_MOD = 6
GRID_W = 64
ROPE_THETA = 10000.0

A_HEADS, A_DK, A_DV, A_GATE_RANK, A_GATE_TEMP = 4, 64, 128, 16, 16.0
B_HEADS, B_DK, B_DV = 4, 64, 128
C_HEADS, C_Q_RANK, C_KV_RANK, C_NOPE, C_ROPE, C_DV = 8, 384, 256, 128, 64, 128
D_FF = 2816
CONV_W = 3

HEADS = 4
HK = HEADS * 64
HV = HEADS * 128
ZA_W = 1536
ZG_W = 128
GATE_COL = 32
RCHUNK = 128
GLA_SAFE_SPAN = 60.0
L1_IN_PAD = 768
QCAT = 256

V7X_VMEM_BYTES = 64 * 1024 * 1024
VMEM_LIMIT = V7X_VMEM_BYTES - 8 * 1024 * 1024


def _cparams(sem):
    return pltpu.CompilerParams(dimension_semantics=sem, vmem_limit_bytes=VMEM_LIMIT)


def _dot(a, b):
    return jnp.dot(a, b, preferred_element_type=F32)


def _dot_nt(a, b):
    return lax.dot_general(a, b, (((1,), (1,)), ((), ())), preferred_element_type=F32)


def _dot_tn(a, b):
    return lax.dot_general(a, b, (((0,), (0,)), ((), ())), preferred_element_type=F32)


def _split3(x):
    hi = x.astype(BF16)
    r1 = x - hi.astype(F32)
    mid = r1.astype(BF16)
    lo = (r1 - mid.astype(F32)).astype(BF16)
    return hi, mid, lo


def _tri_dot(tri, x):
    hi, mid, lo = _split3(x)
    return _dot(tri, hi) + _dot(tri, mid) + _dot(tri, lo)


def _dot_hl(a, b):
    ah = a.astype(BF16)
    al = (a - ah.astype(F32)).astype(BF16)
    bh = b.astype(BF16)
    bl = (b - bh.astype(F32)).astype(BF16)
    return _dot(ah, bh) + _dot(ah, bl) + _dot(al, bh)


def _rms(x, g):
    return x * lax.rsqrt(jnp.mean(x * x, axis=-1, keepdims=True) + EPS) * g


def _sigmoid(x):
    return 1.0 / (1.0 + jnp.exp(-x))


def _silu(x):
    return x * _sigmoid(x)


def _log_sigmoid(x):
    return jnp.minimum(x, 0.0) - jnp.log(1.0 + jnp.exp(-jnp.abs(x)))


def _tri_masks(n):
    r = lax.broadcasted_iota(jnp.int32, (n, n), 0)
    c = lax.broadcasted_iota(jnp.int32, (n, n), 1)
    return c <= r, c >= r


def _mod_kernel(cond_ref, w_ref, b_ref, o_ref):
    s = _silu(cond_ref[...])
    o_ref[...] = _dot(s.astype(BF16), w_ref[...].astype(BF16)) + b_ref[...]


def _modulation(cond8, w_mod, b_mod):
    d = D_MODEL
    out = pl.pallas_call(
        _mod_kernel,
        out_shape=jax.ShapeDtypeStruct((N_MOD, 8, d), F32),
        grid=(N_MOD,),
        in_specs=[pl.BlockSpec((8, d), lambda j: (0, 0)),
                  pl.BlockSpec((d, d), lambda j: (0, j)),
                  pl.BlockSpec((1, d), lambda j: (0, j))],
        out_specs=pl.BlockSpec((None, 8, d), lambda j: (j, 0, 0)),
        compiler_params=_cparams(("arbitrary",)),
        name="modulation",
    )(cond8, w_mod, b_mod.reshape(1, N_MOD * d))
    return out.reshape(N_MOD, 8, 1, d)


MOD_SHIFT1, MOD_SCALE1, MOD_GATE1, MOD_SHIFT2, MOD_SCALE2, MOD_GATE2 = range(N_MOD)


def _mod_spec(which, first_row, per):
    return pl.BlockSpec((None, None, 1, D_MODEL), lambda i: (which, first_row + i // per, 0, 0))


L0_GROUP_ROWS = ((0, ZA_W), (ZA_W + 32, 2 * ZA_W + 32))
L0_LR_ROWS = (ZA_W, ZA_W + 32)
L0_GATE_ROWS = (2 * ZA_W + 32, 2 * ZA_W + 48)


def _p0_kernel(x_ref, g_ref, sc_ref, sh_ref, wt_ref, za_ref, zb_ref, zg_ref, w_ref):
    @pl.when(pl.program_id(0) == 0)
    def _():
        blk = 256
        for gi, (r0, r1) in enumerate(L0_GROUP_ROWS):
            for j in range((r1 - r0) // blk):
                rows = wt_ref[r0 + j * blk:r0 + (j + 1) * blk, :]
                w_ref[:, gi * ZA_W + j * blk:gi * ZA_W + (j + 1) * blk] = rows.T.astype(BF16)
        tail = jnp.concatenate([wt_ref[L0_LR_ROWS[0]:L0_LR_ROWS[1], :],
                                wt_ref[L0_GATE_ROWS[0]:L0_GATE_ROWS[1], :],
                                jnp.zeros((ZG_W - 48, D_MODEL), F32)], axis=0)
        w_ref[:, 2 * ZA_W:2 * ZA_W + ZG_W] = tail.T.astype(BF16)

    h = _rms(x_ref[...], g_ref[...]) * (1.0 + sc_ref[...]) + sh_ref[...]
    hb = h.astype(BF16)
    za_ref[...] = _dot(hb, w_ref[:, 0:ZA_W])
    zb_ref[...] = _dot(hb, w_ref[:, ZA_W:2 * ZA_W])
    zg_ref[...] = _dot(hb, w_ref[:, 2 * ZA_W:2 * ZA_W + ZG_W])


def _proj_l0(x, g, mod, first_row, groups, w_t, tm):
    t = x.shape[0]
    per = (t // tm) // groups
    row = lambda i: (i, 0)
    fixed = lambda i: (0, 0)
    return pl.pallas_call(
        _p0_kernel,
        out_shape=(jax.ShapeDtypeStruct((t, ZA_W), F32),
                   jax.ShapeDtypeStruct((t, ZA_W), F32),
                   jax.ShapeDtypeStruct((t, ZG_W), F32)),
        grid=(t // tm,),
        in_specs=[pl.BlockSpec((tm, D_MODEL), row),
                  pl.BlockSpec((1, D_MODEL), fixed),
                  _mod_spec(MOD_SCALE1, first_row, per),
                  _mod_spec(MOD_SHIFT1, first_row, per),
                  _resident(w_t.shape)],
        out_specs=(pl.BlockSpec((tm, ZA_W), row),
                   pl.BlockSpec((tm, ZA_W), row),
                   pl.BlockSpec((tm, ZG_W), row)),
        scratch_shapes=[pltpu.VMEM((D_MODEL, 2 * ZA_W + ZG_W), BF16)],
        compiler_params=_cparams(("arbitrary",)),
        name="proj_l0",
    )(x, g, mod, mod, w_t)


def _head_lane_mask(shape, h):
    lane = lax.broadcasted_iota(jnp.int32, shape, len(shape) - 1)
    return (lane >= h * 64) & (lane < (h + 1) * 64)


def _head_row(h):
    return jnp.where(_head_lane_mask((1, HK), h), 1.0, 0.0).astype(BF16)


def _gla_prepare(za_ref, la_ref, r0, rev, tri):
    L = RCHUNK
    rows = pl.ds(r0, L)
    b = _tri_dot(tri, la_ref[rows, :])
    mid = L // 2
    ref = b[mid:mid + 1, :]
    b_end = b[0:1, :] if rev else b[L - 1:L, :]
    q = za_ref[rows, 0:HK]
    k = za_ref[rows, HK:2 * HK]
    v = za_ref[rows, 2 * HK:2 * HK + HV].astype(BF16)
    qe = (q * (jnp.exp(b - ref) * (A_DK ** -0.5))).astype(BF16)
    ke = (k * jnp.exp(ref - b)).astype(BF16)
    qm = jnp.concatenate([qe * _head_row(h) for h in range(HEADS)], axis=0)
    rt = _dot_tn(v, ke)
    upd = None
    for h in range(HEADS):
        blk = rt[h * 128:(h + 1) * 128, :]
        blk = jnp.where(_head_lane_mask(blk.shape, h), blk, 0.0)
        upd = blk if upd is None else upd + blk
    return dict(rows=rows, qm=qm, ke=ke, v=v, upd=upd * jnp.exp(b_end - ref),
                decay=jnp.exp(b_end), eref=jnp.exp(ref))


def _gla_apply(items):
    L = RCHUNK
    sts = [st_ref[...] for _, st_ref, _, _ in items]
    o1s = []
    for (p, _, _, _), st in zip(items, sts):
        rhs = jnp.concatenate([p["ke"], (st * p["eref"]).astype(BF16)], axis=0)
        o1s.append(_dot_nt(p["qm"], rhs))
    for h in range(HEADS):
        for (p, _, o_ref, mask), o1 in zip(items, o1s):
            blk = o1[h * L:(h + 1) * L, :]
            sc = jnp.where(mask, blk[:, 0:L], 0.0).astype(BF16)
            o_ref[p["rows"], h * 128:(h + 1) * 128] = (
                blk[:, L:L + 128] + _dot(sc, p["v"][:, h * 128:(h + 1) * 128]))
    for (p, st_ref, _, _), st in zip(items, sts):
        st_ref[...] = st * p["decay"] + p["upd"]


def _gla_exact_scan(za_ref, la_ref, st_ref, o_ref, b_ref, n, rev, st_init):
    L = RCHUNK
    nc = n // L
    m_lo, m_up = _tri_masks(L)
    tri = jnp.where(m_up if rev else m_lo, 1.0, 0.0).astype(BF16)
    s_idx = lax.broadcasted_iota(jnp.int32, (L, 1), 0)
    krow = lax.broadcasted_iota(jnp.int32, (HK, 128), 0) // 64
    kcol = lax.broadcasted_iota(jnp.int32, (HK, 128), 1)
    head_sum = jnp.where(krow == kcol, 1.0, 0.0)
    hsub = lax.broadcasted_iota(jnp.int32, (8, HK), 0)
    hlane = lax.broadcasted_iota(jnp.int32, (8, HK), 1) // 64
    head_rows = jnp.where(hsub == hlane, 1.0, 0.0)
    st_ref[...] = st_init

    def chunk_body(ci, carry):
        c = nc - 1 - ci if rev else ci
        r0 = pl.multiple_of(c * L, L)
        rows = pl.ds(r0, L)
        b = _tri_dot(tri, la_ref[rows, :])
        b_ref[...] = b
        b_end = b[0:1, :] if rev else b[L - 1:L, :]
        k = za_ref[rows, HK:2 * HK]
        vb = za_ref[rows, 2 * HK:2 * HK + HV].astype(BF16)
        st = st_ref[...]
        stb = st.astype(BF16)

        def row_body(t, carry2):
            b_t = b_ref[pl.ds(t, 1), :]
            q_t = za_ref[pl.ds(r0 + t, 1), 0:HK] * (A_DK ** -0.5)
            valid = (s_idx >= t) if rev else (s_idx <= t)
            w = jnp.exp(jnp.minimum(b_t - b_ref[...], 0.0))
            p = jnp.where(valid, (q_t * k) * w, 0.0)
            sc = jnp.dot(p, head_sum, precision=lax.Precision.HIGHEST,
                         preferred_element_type=F32)
            pv = _dot_tn(sc.astype(BF16), vb)
            qm = (head_rows * (q_t * jnp.exp(b_t))).astype(BF16)
            inter = _dot_nt(qm, stb)
            o_ref[pl.ds(r0 + t, 1), :] = jnp.concatenate(
                [pv[h:h + 1, h * 128:(h + 1) * 128] + inter[h:h + 1, :] for h in range(HEADS)],
                axis=1)
            return carry2

        lax.fori_loop(0, L, row_body, 0)
        kd = (k * jnp.exp(b_end - b)).astype(BF16)
        rt = _dot_tn(vb, kd)
        upd = None
        for h in range(HEADS):
            blk = rt[h * 128:(h + 1) * 128, :]
            blk = jnp.where(_head_lane_mask(blk.shape, h), blk, 0.0)
            upd = blk if upd is None else upd + blk
        st_ref[...] = st * jnp.exp(b_end) + upd
        return carry

    lax.fori_loop(0, nc, chunk_body, 0)


T_M, T_MOUT, T_G, T_SRCW, T_MPREV, T_CARRY, T_USED = 0, 8, 16, 24, 32, 40, 48


def _lane_scan(x, op, fill, is_fwd, pos, n):
    k = 1
    while k < RCHUNK:
        pf = jnp.where(pos >= k, pltpu.roll(x, k, axis=1), fill)
        sf = jnp.where(pos < RCHUNK - k, pltpu.roll(x, n - k, axis=1), fill)
        x = op(x, jnp.where(is_fwd, pf, sf))
        k *= 2
    return x


def _mlstm_gate_tables(g, m0, gt_ref, tab_ref, col_ref, n):
    L = RCHUNK
    nc = n // L
    for c in range(nc):
        gt_ref[:, c * L:(c + 1) * L] = g[c * L:(c + 1) * L, :].T
        tab_ref[c, T_USED:, :] = jnp.zeros((ZG_W - T_USED, L), F32)
    gi = gt_ref[GATE_COL:GATE_COL + 8, :]
    lf = gt_ref[GATE_COL + 8:GATE_COL + 16, :]
    is_fwd = lax.broadcasted_iota(jnp.int32, (8, n), 0) < HEADS
    pos = lax.broadcasted_iota(jnp.int32, (8, n), 1) % L
    b = _lane_scan(lf, jnp.add, 0.0, is_fwd, pos, n)
    gg = gi - b
    cmax = _lane_scan(gg, jnp.maximum, -jnp.inf, is_fwd, pos, n)
    fwd_l = lax.broadcasted_iota(jnp.int32, (8, L), 0) < HEADS
    mp = m0
    for j in range(nc):
        cf, cb = j, nc - 1 - j

        def pick(x):
            return jnp.where(fwd_l, x[:, cf * L:(cf + 1) * L], x[:, cb * L:(cb + 1) * L])

        gj, bj, cj, lj = pick(gg), pick(b), pick(cmax), pick(lf)
        mx = jnp.maximum(mp, jnp.max(gj, axis=1, keepdims=True))
        mj = jnp.maximum(cj, mp)
        blocks = (mj, bj + mj, gj, jnp.exp(gj - mx),
                  jnp.broadcast_to(mp, (8, L)), jnp.broadcast_to(jnp.exp(mp - mx), (8, L)))
        for t, val in enumerate(blocks):
            tab_ref[cf, 8 * t:8 * t + HEADS, :] = val[0:HEADS]
            tab_ref[cb, 8 * t + HEADS:8 * t + 8, :] = val[HEADS:8]
        mp = jnp.sum(lj, axis=1, keepdims=True) + mx
    for c in range(nc):
        col_ref[c * L:(c + 1) * L, :] = tab_ref[c].T
    return mp


def _mlstm_prepare(zb_ref, tab_ref, col_ref, c, d, mask):
    L = RCHUNK
    rows = pl.ds(pl.multiple_of(c * L, L), L)
    qb = zb_ref[rows, 0:HK].astype(BF16)
    kb = (zb_ref[rows, HK:2 * HK] * (B_DK ** -0.5)).astype(BF16)
    v = zb_ref[rows, 2 * HK:2 * HK + HV]
    vb = v.astype(BF16)
    vt = v.T
    qm = jnp.concatenate([qb * _head_row(h) for h in range(HEADS)], axis=0)
    s = _dot_nt(qm, kb)
    ones = jnp.ones((L, 128), BF16)
    pv, w_inter, e_negm = [], [], []
    upd = None
    carry = None
    for h in range(HEADS):
        r = 4 * d + h
        m_rep = jnp.broadcast_to(col_ref[rows, T_M + r:T_M + r + 1], (L, 128))
        mout_rep = jnp.broadcast_to(col_ref[rows, T_MOUT + r:T_MOUT + r + 1], (L, 128))
        g_row = tab_ref[c, T_G + r:T_G + r + 1, :]
        w = jnp.exp(jnp.where(mask, g_row - m_rep, -jnp.inf))
        qk = (s[h * L:(h + 1) * L, :] * w).astype(BF16)
        v1 = jnp.concatenate([vb[:, h * 128:(h + 1) * 128], ones], axis=1)
        pv.append(_dot(qk, v1))
        w_inter.append(jnp.exp(tab_ref[c, T_MPREV + r:T_MPREV + r + 1, :] - m_rep))
        e_negm.append(jnp.exp(-mout_rep))
        sw = tab_ref[c, T_SRCW + r:T_SRCW + r + 1, :]
        lhs = jnp.concatenate([vt[h * 128:(h + 1) * 128, :] * sw, jnp.broadcast_to(sw, (128, L))],
                              axis=0).astype(BF16)
        blk = _dot(lhs, kb)
        blk = jnp.where(_head_lane_mask(blk.shape, h), blk, 0.0)
        upd = blk if upd is None else upd + blk
        cr = tab_ref[c, T_CARRY + r:T_CARRY + r + 1, :]
        cr = jnp.where(_head_lane_mask((1, HK), h), jnp.concatenate([cr, cr], axis=1), 0.0)
        carry = cr if carry is None else carry + cr
    return dict(rows=rows, qm=qm, pv=pv, w_inter=w_inter, e_negm=e_negm, upd=upd, carry=carry)


def _mlstm_apply(items):
    L = RCHUNK
    sts = [st_ref[...] for _, st_ref, _ in items]
    a_all = [_dot_nt(p["qm"], st.astype(BF16)) for (p, _, _), st in zip(items, sts)]
    for h in range(HEADS):
        for (p, _, h_ref), a in zip(items, a_all):
            blk = a[h * L:(h + 1) * L, :]
            pv, w, e = p["pv"][h], p["w_inter"][h], p["e_negm"][h]
            num = w * blk[:, 0:128] + pv[:, 0:128]
            den = w * blk[:, 128:256] + pv[:, 128:256]
            h_ref[p["rows"], h * 128:(h + 1) * 128] = num / jnp.maximum(jnp.abs(den), e)
    for (p, st_ref, _), st in zip(items, sts):
        st_ref[...] = p["carry"] * st + p["upd"]


def _rec_kernel(*refs, n, has_init):
    it = iter(refs)

    def take(k):
        return [next(it) for _ in range(k)]

    za_all, zb_all, zg_all, zgn_all = take(4)
    wgf_ref, wgb_ref, bgf_ref, bgb_ref, gna_ref, bg_ref, gnb_ref = take(7)
    if has_init:
        init_all = take(7)
    out_all = take(8)
    (gt_all,) = take(1)
    set_a_all = take(6)
    set_b_all = take(6)
    work_all = take(9)
    L = RCHUNK
    nc = n // L
    step = pl.program_id(0)
    elems = []
    for e in range(za_all.shape[0]):
        el = dict(za=za_all.at[e], zb=zb_all.at[e], zg=zg_all.at[e], zgn=zgn_all.at[e],
                  gt=gt_all.at[e], set_a=[r.at[e] for r in set_a_all],
                  set_b=[r.at[e] for r in set_b_all])
        (el["oa"], el["sf"], el["sb"], el["om"], el["cf"], el["cb"], el["n_out"],
         el["m_out"]) = [r.at[e] for r in out_all]
        (el["of"], el["ob"], el["hf"], el["hb"], el["gsf"], el["gsb"], el["msf"],
         el["msb"], el["b_exact"]) = [r.at[e] for r in work_all]
        if has_init:
            el["init"] = [r.at[e] for r in init_all]
        elems.append(el)

    def fill(gate_ref, m0, bufs, gt_ref):
        laf_ref, lab_ref, tab_ref, col_ref, mf_ref, span_ref = bufs
        zg = gate_ref[...]
        laf = _log_sigmoid(_dot_hl(zg, wgf_ref[...]) + bgf_ref[...]) / A_GATE_TEMP
        lab = _log_sigmoid(_dot_hl(zg, wgb_ref[...]) + bgb_ref[...]) / A_GATE_TEMP
        laf_ref[...] = laf
        lab_ref[...] = lab
        span = None
        for la in (laf, lab):
            for c in range(nc):
                tot = -jnp.sum(la[c * L:(c + 1) * L, :], axis=0, keepdims=True)
                span = tot if span is None else jnp.maximum(span, tot)
        span_ref[...] = jnp.broadcast_to(jnp.max(span, axis=1, keepdims=True), (8, 128))
        g = zg + bg_ref[...]
        lane = lax.broadcasted_iota(jnp.int32, g.shape, 1)
        is_forget = (lane >= GATE_COL + 8) & (lane < GATE_COL + 16)
        g = jnp.where(is_forget, _log_sigmoid(g), g)
        m_fin = _mlstm_gate_tables(g, m0, gt_ref, tab_ref, col_ref, n)
        mf_ref[...] = jnp.broadcast_to(m_fin, (8, 128))

    for el in elems:
        if has_init:
            sf0_ref, sb0_ref, cf0_ref, cb0_ref, n0_ref, m0_ref, m0n_ref = el["init"]
            el["m0"], el["m0_next"] = m0_ref[:, 0:1], m0n_ref[:, 0:1]
            el["gsf"][...] = sf0_ref[...]
            el["gsb"][...] = sb0_ref[...]
            el["msf"][...] = jnp.concatenate(
                [cf0_ref[...], jnp.broadcast_to(n0_ref[0:1, :], (128, HK))], axis=0)
            el["msb"][...] = jnp.concatenate(
                [cb0_ref[...], jnp.broadcast_to(n0_ref[1:2, :], (128, HK))], axis=0)
        else:
            el["m0"] = el["m0_next"] = jnp.zeros((8, 1), F32)
            for name in ("gsf", "gsb", "msf", "msb"):
                el[name][...] = jnp.zeros(el[name].shape, F32)

    @pl.when(step == 0)
    def _():
        for el in elems:
            fill(el["zg"], el["m0"], el["set_a"], el["gt"])

    m_lo, m_up = _tri_masks(L)
    tri_lo = jnp.where(m_lo, 1.0, 0.0).astype(BF16)
    tri_up = jnp.where(m_up, 1.0, 0.0).astype(BF16)
    gna = gna_ref[...]
    gnb = gnb_ref[...]

    def out_body(c, carry):
        rows = pl.ds(pl.multiple_of(c * L, L), L)
        for h in range(HEADS):
            cols = slice(h * 128, (h + 1) * 128)
            gcols = slice(2 * HK + HV + h * 128, 2 * HK + HV + (h + 1) * 128)
            for el in elems:
                o = el["of"][rows, cols] + el["ob"][rows, cols]
                el["oa"][rows, cols] = (
                    _rms(o, gna) * _silu(el["za"][rows, gcols])).astype(BF16)
                hh = el["hf"][rows, cols] + el["hb"][rows, cols]
                el["om"][rows, cols] = (
                    _sigmoid(el["zb"][rows, gcols]) * _rms(hh, gnb)).astype(BF16)
        return carry

    def run(cur, nxt):
        for el in elems:
            fill(el["zgn"], el["m0_next"], el[nxt], el["gt"])

        def scan_body(gidx, carry):
            waves = []
            for w in range(2):
                c = 2 * gidx + w
                cb = nc - 1 - c
                gla_items, mlstm_items = [], []
                for el in elems:
                    laf_ref, lab_ref, tab_ref, col_ref = el[cur][0:4]
                    gla_items += [
                        (_gla_prepare(el["za"], laf_ref, pl.multiple_of(c * L, L), False, tri_lo),
                         el["gsf"], el["of"], m_lo),
                        (_gla_prepare(el["za"], lab_ref, pl.multiple_of(cb * L, L), True, tri_up),
                         el["gsb"], el["ob"], m_up)]
                    mlstm_items += [
                        (_mlstm_prepare(el["zb"], tab_ref, col_ref, c, 0, m_lo), el["msf"], el["hf"]),
                        (_mlstm_prepare(el["zb"], tab_ref, col_ref, cb, 1, m_up), el["msb"], el["hb"])]
                waves.append((gla_items, mlstm_items))
            for gla_items, mlstm_items in waves:
                _gla_apply(gla_items)
                _mlstm_apply(mlstm_items)
            return carry

        if nc == 2:
            scan_body(0, 0)
        else:
            lax.fori_loop(0, nc // 2, scan_body, 0)
        for el in elems:
            laf_ref, lab_ref = el[cur][0:2]
            too_fast = el[cur][5][0, 0] > GLA_SAFE_SPAN

            @pl.when(too_fast)
            def _(el=el, laf_ref=laf_ref, lab_ref=lab_ref):
                zero = jnp.zeros((128, HK), F32)
                init_f = el["init"][0][...] if has_init else zero
                init_b = el["init"][1][...] if has_init else zero
                _gla_exact_scan(el["za"], laf_ref, el["gsf"], el["of"], el["b_exact"], n, False, init_f)
                _gla_exact_scan(el["za"], lab_ref, el["gsb"], el["ob"], el["b_exact"], n, True, init_b)

        if nc == 2:
            for c in range(nc):
                out_body(c, 0)
        else:
            lax.fori_loop(0, nc, out_body, 0)
        for el in elems:
            el["m_out"][...] = el[cur][4][...]

    @pl.when(step % 2 == 0)
    def _():
        run("set_a", "set_b")

    @pl.when(step % 2 == 1)
    def _():
        run("set_b", "set_a")

    for el in elems:
        gsf_t = el["gsf"][...].T
        gsb_t = el["gsb"][...].T
        msf = el["msf"][...]
        msb = el["msb"][...]
        ctf_t = msf[0:128, :].T
        ctb_t = msb[0:128, :].T
        for h in range(HEADS):
            rows = slice(h * 64, (h + 1) * 64)
            el["sf"][h] = gsf_t[rows, :]
            el["sb"][h] = gsb_t[rows, :]
            el["cf"][h] = ctf_t[rows, :]
            el["cb"][h] = ctb_t[rows, :]
        el["n_out"][...] = jnp.concatenate(
            [msf[128:129, :], msb[128:129, :], jnp.zeros((6, HK), F32)], axis=0)


def _recurrent(za, zb, zg, wgf, wgb, bgf, bgb, gna, bg, gnb, init, bt):
    bsz, n, _ = za.shape
    has_init = init is not None
    nc = n // RCHUNK
    steps = bsz // bt
    per_b3 = lambda b: (b, 0, 0)
    next_b3 = lambda b: (jnp.minimum(b + 1, steps - 1), 0, 0)
    per_b4 = lambda b: (b, 0, 0, 0)
    fixed = lambda b: (0, 0)
    in_specs = [pl.BlockSpec((bt, n, ZA_W), per_b3),
                pl.BlockSpec((bt, n, ZA_W), per_b3),
                pl.BlockSpec((bt, n, ZG_W), per_b3),
                pl.BlockSpec((bt, n, ZG_W), next_b3),
                pl.BlockSpec((ZG_W, HK), fixed),
                pl.BlockSpec((ZG_W, HK), fixed),
                pl.BlockSpec((1, HK), fixed),
                pl.BlockSpec((1, HK), fixed),
                pl.BlockSpec((1, 128), fixed),
                pl.BlockSpec((1, ZG_W), fixed),
                pl.BlockSpec((1, 128), fixed)]
    args = [za, zb, zg, zg, wgf, wgb, bgf, bgb, gna, bg, gnb]
    if has_init:
        sf0, sb0, cf0, cb0, n0, m0 = init
        in_specs += [pl.BlockSpec((bt, 128, HK), per_b3)] * 4
        in_specs += [pl.BlockSpec((bt, 8, HK), per_b3), pl.BlockSpec((bt, 8, 128), per_b3),
                     pl.BlockSpec((bt, 8, 128), next_b3)]
        args += [sf0, sb0, cf0, cb0, n0, m0, m0]
    state = jax.ShapeDtypeStruct((bsz, HEADS, 64, 128), F32)
    state_spec = pl.BlockSpec((bt, HEADS, 64, 128), per_b4)

    def vmem(*shape):
        return pltpu.VMEM((bt,) + shape, F32)

    gate_set = [vmem(n, HK), vmem(n, HK), vmem(nc, ZG_W, RCHUNK), vmem(n, ZG_W), vmem(8, 128),
                vmem(8, 128)]
    return pl.pallas_call(
        functools.partial(_rec_kernel, n=n, has_init=has_init),
        out_shape=(jax.ShapeDtypeStruct((bsz, n, HV), BF16), state, state,
                   jax.ShapeDtypeStruct((bsz, n, HV), BF16), state, state,
                   jax.ShapeDtypeStruct((bsz, 8, HK), F32),
                   jax.ShapeDtypeStruct((bsz, 8, 128), F32)),
        grid=(steps,),
        in_specs=in_specs,
        out_specs=(pl.BlockSpec((bt, n, HV), per_b3), state_spec, state_spec,
                   pl.BlockSpec((bt, n, HV), per_b3), state_spec, state_spec,
                   pl.BlockSpec((bt, 8, HK), per_b3),
                   pl.BlockSpec((bt, 8, 128), per_b3)),
        scratch_shapes=([vmem(ZG_W, n)] + gate_set + gate_set + [vmem(n, HV)] * 4
                        + [vmem(128, HK)] * 2 + [vmem(2 * 128, HK)] * 2 + [vmem(RCHUNK, HK)]),
        compiler_params=_cparams(("arbitrary",)),
        name="recurrent",
    )(*args)


def _mix_residual(parts, wo_ref, x_ref, gt1_ref, gpost1_ref, gpre2_ref, sc2_ref, sh2_ref):
    m = None
    off = 0
    for p in parts:
        kp = p.shape[1]
        t = _dot(p[...], wo_ref[off:off + kp, :])
        m = t if m is None else m + t
        off += kp
    x1 = x_ref[...] + gt1_ref[...] * _rms(m, gpost1_ref[...])
    hb = (_rms(x1, gpre2_ref[...]) * (1.0 + sc2_ref[...]) + sh2_ref[...]).astype(BF16)
    return x1, hb


def _ffn_gated_up(hb, wu_ref, cw_ref, cb_ref, act_ref, seq, tf):
    tm = hb.shape[0]
    pos = lax.broadcasted_iota(jnp.int32, (tm, 1), 0) % seq
    first = pos == 0
    last = pos == seq - 1

    def conv(u, cols):
        prev = jnp.where(first, 0.0, pltpu.roll(u, 1, axis=0))
        nxt = jnp.where(last, 0.0, pltpu.roll(u, tm - 1, axis=0))
        return (cw_ref[0:1, cols] * prev + cw_ref[1:2, cols] * u + cw_ref[2:3, cols] * nxt
                + cb_ref[:, cols])

    for j in range(D_FF // tf):
        ca = slice(j * tf, (j + 1) * tf)
        cg = slice(D_FF + j * tf, D_FF + (j + 1) * tf)
        a = conv(_dot(hb, wu_ref[:, ca]), ca)
        g = conv(_dot(hb, wu_ref[:, cg]), cg)
        act_ref[:, ca] = (_silu(g) * a).astype(BF16)


def _mix_ffn_kernel(*refs, n_parts, seq, tf):
    parts = refs[:n_parts]
    (wo_ref, x_ref, gt1_ref, gpost1_ref, gpre2_ref, sc2_ref, sh2_ref,
     wu_ref, cw_ref, cb_ref, wd_ref, gt2_ref, gpost2_ref, o_ref, act_ref) = refs[n_parts:]
    x1, hb = _mix_residual(parts, wo_ref, x_ref, gt1_ref, gpost1_ref, gpre2_ref, sc2_ref, sh2_ref)
    o_ref[...] = x1
    _ffn_gated_up(hb, wu_ref, cw_ref, cb_ref, act_ref, seq, tf)
    y = _dot(act_ref[...], wd_ref[...])
    o_ref[...] = o_ref[...] + gt2_ref[...] * _rms(y, gpost2_ref[...])


def _resident(shape):
    return pl.BlockSpec(shape, lambda *_: (0,) * len(shape), pipeline_mode=pl.Buffered(1))


def _mix_ffn(parts, w_out, x, mod, first_row, groups, gpost1, gpre2, w_up, conv_w, conv_b, w_down,
             gpost2, tm, seq, tf=256):
    t = x.shape[0]
    per = (t // tm) // groups
    row = lambda i: (i, 0)
    vec = _resident((1, D_MODEL))
    in_specs = [pl.BlockSpec((tm, p.shape[1]), row) for p in parts]
    in_specs += [_resident(w_out.shape), pl.BlockSpec((tm, D_MODEL), row),
                 _mod_spec(MOD_GATE1, first_row, per), vec, vec,
                 _mod_spec(MOD_SCALE2, first_row, per), _mod_spec(MOD_SHIFT2, first_row, per),
                 _resident(w_up.shape), _resident(conv_w.shape), _resident(conv_b.shape),
                 _resident(w_down.shape), _mod_spec(MOD_GATE2, first_row, per), vec]
    return pl.pallas_call(
        functools.partial(_mix_ffn_kernel, n_parts=len(parts), seq=seq, tf=tf),
        out_shape=jax.ShapeDtypeStruct((t, D_MODEL), F32),
        grid=(t // tm,),
        in_specs=in_specs,
        out_specs=pl.BlockSpec((tm, D_MODEL), row),
        scratch_shapes=[pltpu.VMEM((tm, D_FF), BF16)],
        compiler_params=_cparams(("arbitrary",)),
        name="mix_ffn",
    )(*parts, w_out, x, mod, gpost1, gpre2, mod, mod, w_up, conv_w, conv_b, w_down, mod, gpost2)


def _rope(x, cos, sin_signed):
    lane = lax.broadcasted_iota(jnp.int32, x.shape, 1)
    lower = (lane % 32) < 16
    partner = jnp.where(lower, pltpu.roll(x, 128 - 16, axis=1), pltpu.roll(x, 16, axis=1))
    return x * cos + partner * sin_signed


def _p1_kernel(*refs, rope):
    if rope:
        (x_ref, g_ref, sc_ref, sh_ref, w_ref, gq_ref, wq_ref, gkv_ref, cos_ref, sin_ref,
         q_ref, ckv_ref, kr_ref) = refs
    else:
        (x_ref, g_ref, sc_ref, sh_ref, w_ref, gq_ref, wq_ref, gkv_ref,
         q_ref, ckv_ref, kr_ref) = refs
    h = _rms(x_ref[...], g_ref[...]) * (1.0 + sc_ref[...]) + sh_ref[...]
    z = _dot(h.astype(BF16), w_ref[...])
    qa = _rms(z[:, 0:C_Q_RANK], gq_ref[...])
    q = _dot(qa.astype(BF16), wq_ref[...])
    ckv_ref[...] = _rms(z[:, C_Q_RANK:C_Q_RANK + C_KV_RANK], gkv_ref[...])
    kr = z[:, C_Q_RANK + C_KV_RANK:L1_IN_PAD]
    if rope:
        cos = cos_ref[...]
        sin = sin_ref[...]
        kr = _rope(kr, cos, sin)
    kr_ref[...] = kr[:, 0:C_ROPE]
    for hd in range(C_HEADS):
        c0 = hd * QCAT
        q_ref[:, c0:c0 + 128] = q[:, c0:c0 + 128].astype(q_ref.dtype)
        qr = q[:, c0 + 128:c0 + 256]
        if rope:
            qr = _rope(qr, cos, sin)
        q_ref[:, c0 + 128:c0 + 256] = qr.astype(q_ref.dtype)


def _proj_l1(x, g, mod, first_row, groups, w, gq, wq, gkv, tables, tm):
    t = x.shape[0]
    per = (t // tm) // groups
    row = lambda i: (i, 0)
    fixed = lambda i: (0, 0)
    in_specs = [pl.BlockSpec((tm, D_MODEL), row),
                pl.BlockSpec((1, D_MODEL), fixed),
                _mod_spec(MOD_SCALE1, first_row, per),
                _mod_spec(MOD_SHIFT1, first_row, per),
                pl.BlockSpec(w.shape, fixed),
                pl.BlockSpec((1, C_Q_RANK), fixed),
                pl.BlockSpec(wq.shape, fixed),
                pl.BlockSpec((1, C_KV_RANK), fixed)]
    args = [x, g, mod, mod, w, gq, wq, gkv]
    rope = tables is not None
    if rope:
        n = tables[0].shape[0]
        pos = lambda i: (i % (n // tm), 0)
        in_specs += [pl.BlockSpec((tm, 128), pos)] * 2
        args += list(tables)
    return pl.pallas_call(
        functools.partial(_p1_kernel, rope=rope),
        out_shape=(jax.ShapeDtypeStruct((t, C_HEADS * QCAT), BF16),
                   jax.ShapeDtypeStruct((t, C_KV_RANK), F32),
                   jax.ShapeDtypeStruct((t, C_ROPE), F32)),
        grid=(t // tm,),
        in_specs=in_specs,
        out_specs=(pl.BlockSpec((tm, C_HEADS * QCAT), row),
                   pl.BlockSpec((tm, C_KV_RANK), row),
                   pl.BlockSpec((tm, C_ROPE), row)),
        compiler_params=_cparams(("arbitrary",)),
        name="proj_l1",
    )(*args)


def _attn_kernel(*refs, n_parts, scale):
    q_ref = refs[0]
    kv_parts = refs[1:1 + 2 * n_parts]
    w_ref, o_ref, k_sc, v_sc = refs[1 + 2 * n_parts:]
    bt = q_ref.shape[0]

    @pl.when(pl.program_id(1) == 0)
    def _():
        off = 0
        for p in range(n_parts):
            ckv_ref, kr_ref = kv_parts[2 * p], kv_parts[2 * p + 1]
            nk = ckv_ref.shape[1]
            ones = jnp.ones((nk, C_DV), BF16)
            for e in range(bt):
                kv = _dot(ckv_ref[e].astype(BF16), w_ref[...])
                kr = kr_ref[e].astype(BF16)
                kr2 = jnp.concatenate([kr, kr], axis=1)
                for hd in range(C_HEADS):
                    v0 = C_HEADS * C_NOPE + hd * C_DV
                    k_sc[e, hd, off:off + nk, 0:128] = kv[:, hd * 128:(hd + 1) * 128].astype(BF16)
                    k_sc[e, hd, off:off + nk, 128:256] = kr2
                    v_sc[e, hd, off:off + nk, 0:C_DV] = kv[:, v0:v0 + C_DV].astype(BF16)
                    v_sc[e, hd, off:off + nk, C_DV:2 * C_DV] = ones
            off += nk

    for hd in range(C_HEADS):
        for e in range(bt):
            t = _dot_nt(q_ref[e, :, hd * QCAT:(hd + 1) * QCAT], k_sc[e, hd]) * (scale * LOG2E)
            p = jnp.exp2(t - jnp.max(t, axis=1, keepdims=True))
            ov = _dot(p.astype(BF16), v_sc[e, hd])
            o_ref[e, :, hd * C_DV:(hd + 1) * C_DV] = (
                ov[:, 0:C_DV] / ov[:, C_DV:2 * C_DV]).astype(o_ref.dtype)


def _attention(q, kv_parts, w_kvb, tq, bt):
    bsz, n, _ = q.shape
    nk = sum(p[0].shape[1] for p in kv_parts)
    in_specs = [pl.BlockSpec((bt, tq, C_HEADS * QCAT), lambda b, i: (b, i, 0))]
    args = [q]
    for ckv, kr in kv_parts:
        in_specs.append(pl.BlockSpec((bt, ckv.shape[1], C_KV_RANK), lambda b, i: (b, 0, 0)))
        in_specs.append(pl.BlockSpec((bt, kr.shape[1], C_ROPE), lambda b, i: (b, 0, 0)))
        args += [ckv, kr]
    in_specs.append(pl.BlockSpec(w_kvb.shape, lambda b, i: (0, 0)))
    args.append(w_kvb)
    return pl.pallas_call(
        functools.partial(_attn_kernel, n_parts=len(kv_parts),
                          scale=(C_NOPE + C_ROPE) ** -0.5),
        out_shape=jax.ShapeDtypeStruct((bsz, n, C_HEADS * C_DV), BF16),
        grid=(bsz // bt, n // tq),
        in_specs=in_specs,
        out_specs=pl.BlockSpec((bt, tq, C_HEADS * C_DV), lambda b, i: (b, i, 0)),
        scratch_shapes=[pltpu.VMEM((bt, C_HEADS, nk, QCAT), BF16),
                        pltpu.VMEM((bt, C_HEADS, nk, 2 * C_DV), BF16)],
        compiler_params=_cparams(("arbitrary", "arbitrary")),
        name="mla_attention",
    )(*args)


def _l1_wq_layout(w_qb):
    blocks = []
    zero = jnp.zeros((w_qb.shape[0], C_ROPE), w_qb.dtype)
    for hd in range(C_HEADS):
        base = hd * (C_NOPE + C_ROPE)
        nope = w_qb[:, base:base + C_NOPE]
        rope = w_qb[:, base + C_NOPE:base + C_NOPE + C_ROPE]
        blocks += [nope, rope, zero] if hd % 2 == 0 else [nope, zero, rope]
    return jnp.concatenate(blocks, axis=1).astype(BF16)


def _l1_wkvb_layout(w_kvb):
    w = w_kvb.reshape(C_KV_RANK, C_HEADS, C_NOPE + C_DV)
    k = w[:, :, :C_NOPE].reshape(C_KV_RANK, C_HEADS * C_NOPE)
    v = w[:, :, C_NOPE:].reshape(C_KV_RANK, C_HEADS * C_DV)
    return jnp.concatenate([k, v], axis=1).astype(BF16)


def _rope_tables(n):
    pos = np.arange(n)
    row = (pos // GRID_W).astype(np.float64)
    col = (pos % GRID_W).astype(np.float64)
    half = C_ROPE // 2
    inv = 1.0 / (ROPE_THETA ** (np.arange(0, half, 2, dtype=np.float64) / half))
    ang_r = row[:, None] * inv[None, :]
    ang_c = col[:, None] * inv[None, :]
    cos64 = np.concatenate([np.cos(ang_r)] * 2 + [np.cos(ang_c)] * 2, axis=1)
    sin64 = np.concatenate([-np.sin(ang_r), np.sin(ang_r), -np.sin(ang_c), np.sin(ang_c)], axis=1)
    return (jnp.asarray(np.concatenate([cos64] * 2, axis=1), F32),
            jnp.asarray(np.concatenate([sin64] * 2, axis=1), F32))


def _gate_weight(w, first_row):
    return jnp.pad(w, ((first_row, ZG_W - first_row - A_GATE_RANK), (0, 0)))


def _state_t(s):
    b = s.shape[0]
    return s.transpose(0, 3, 1, 2).reshape(b, 128, HK)


def _n_rows(n_fwd, n_bwd):
    b = n_fwd.shape[0]
    rows = jnp.zeros((b, 8, HK), F32)
    return rows.at[:, 0, :].set(n_fwd.reshape(b, HK)).at[:, 1, :].set(n_bwd.reshape(b, HK))


def _m_rows(m_fwd, m_bwd):
    m = jnp.concatenate([m_fwd, m_bwd], axis=1)
    return jnp.broadcast_to(m[:, :, None], (m.shape[0], 8, 128))


def kernel(x_prompt, x_sample, state_l0_gla_fwd, state_l0_gla_bwd, state_l0_mlstm_c_fwd, state_l0_mlstm_n_fwd, state_l0_mlstm_m_fwd, state_l0_mlstm_c_bwd, state_l0_mlstm_n_bwd, state_l0_mlstm_m_bwd, cache_l1_ckv, cache_l1_krope, c, c_ctx, l0_w_mod, l0_b_mod, l0_g_pre_mix, l0_g_post_mix, l0_g_pre_ffn, l0_g_post_ffn, l0_w_in, l0_gla_w_gate_f, l0_gla_b_gate_f, l0_gla_w_gate_b, l0_gla_b_gate_b, l0_gla_g_norm, l0_mlstm_b_gates, l0_mlstm_g_norm, l0_w_out, l0_ffn_w_up, l0_ffn_conv_w, l0_ffn_conv_b, l0_ffn_w_down, l1_w_mod, l1_b_mod, l1_g_pre_mix, l1_g_post_mix, l1_g_pre_ffn, l1_g_post_ffn, l1_w_in, l1_g_q_norm, l1_w_qb, l1_g_kv_norm, l1_w_kvb, l1_w_out, l1_ffn_w_up, l1_ffn_conv_w, l1_ffn_conv_b, l1_ffn_w_down):
    bp, sp, d = x_prompt.shape
    bs, ss, _ = x_sample.shape
    row = lambda v: v.reshape(1, -1)

    cond8 = jnp.concatenate([c_ctx[None, :], c, jnp.zeros((8 - 1 - bs, d), F32)], axis=0)
    mods = (_modulation(cond8, l0_w_mod, l0_b_mod), _modulation(cond8, l1_w_mod, l1_b_mod))

    paths = {
        "ctx": dict(x=x_prompt.reshape(bp * sp, d), b=bp, n=sp, first_row=0, groups=1, tm=512),
        "lat": dict(x=x_sample.reshape(bs * ss, d), b=bs, n=ss, first_row=1, groups=bs, tm=512),
    }

    w_in0 = l0_w_in.T
    wgf = _gate_weight(l0_gla_w_gate_f, 0)
    wgb = _gate_weight(l0_gla_w_gate_b, A_GATE_RANK)
    bg = jnp.pad(l0_mlstm_b_gates, (GATE_COL, ZG_W - GATE_COL - 16)).reshape(1, ZG_W)
    w_out0 = l0_w_out.astype(BF16)
    w_in1 = jnp.pad(l1_w_in, ((0, 0), (0, L1_IN_PAD - l1_w_in.shape[1]))).astype(BF16)
    w_q1 = _l1_wq_layout(l1_w_qb)
    w_kvb1 = _l1_wkvb_layout(l1_w_kvb)
    w_out1 = l1_w_out.astype(BF16)
    ffn = ((l0_ffn_w_up.astype(BF16), l0_ffn_conv_w, row(l0_ffn_conv_b), l0_ffn_w_down.astype(BF16)),
           (l1_ffn_w_up.astype(BF16), l1_ffn_conv_w, row(l1_ffn_conv_b), l1_ffn_w_down.astype(BF16)))
    norms = ((row(l0_g_pre_mix), row(l0_g_post_mix), row(l0_g_pre_ffn), row(l0_g_post_ffn)),
             (row(l1_g_pre_mix), row(l1_g_post_mix), row(l1_g_pre_ffn), row(l1_g_post_ffn)))

    gla_init = (_state_t(state_l0_gla_fwd), _state_t(state_l0_gla_bwd))
    mlstm_init = (_state_t(state_l0_mlstm_c_fwd), _state_t(state_l0_mlstm_c_bwd),
                  _n_rows(state_l0_mlstm_n_fwd, state_l0_mlstm_n_bwd),
                  _m_rows(state_l0_mlstm_m_fwd, state_l0_mlstm_m_bwd))
    tables = _rope_tables(ss)

    results = {}
    for name, p in paths.items():
        x, b, n, tm = p["x"], p["b"], p["n"], p["tm"]
        first_row, groups = p["first_row"], p["groups"]
        is_ctx = name == "ctx"
        ffn_tm = tm if is_ctx else n
        g_pre, g_post, g_pre2, g_post2 = norms[0]

        za, zb, zg = _proj_l0(x, g_pre, mods[0], first_row, groups, w_in0, tm)
        za, zb, zg = (t.reshape(b, n, -1) for t in (za, zb, zg))
        oa, sa_f, sa_b, om, c_f, c_b, n_rows, m_rows = _recurrent(
            za, zb, zg, wgf, wgb, row(l0_gla_b_gate_f), row(l0_gla_b_gate_b), row(l0_gla_g_norm),
            bg, row(l0_mlstm_g_norm), None if is_ctx else gla_init + mlstm_init,
            2 if is_ctx else 1)
        x = _mix_ffn([oa.reshape(b * n, HV), om.reshape(b * n, HV)], w_out0, x, mods[0], first_row,
                     groups, g_post, g_pre2, *ffn[0], g_post2, ffn_tm, n)

        g_pre, g_post, g_pre2, g_post2 = norms[1]
        q, ckv, kr = _proj_l1(x, g_pre, mods[1], first_row, groups, w_in1, row(l1_g_q_norm), w_q1,
                              row(l1_g_kv_norm), None if is_ctx else tables,
                              2 * tm if is_ctx else tm)
        ckv3 = ckv.reshape(b, n, C_KV_RANK)
        kr3 = kr.reshape(b, n, C_ROPE)
        kv_parts = [(ckv3, kr3)] if is_ctx else [(cache_l1_ckv, cache_l1_krope), (ckv3, kr3)]
        o = _attention(q.reshape(b, n, -1), kv_parts, w_kvb1, 256 if is_ctx else 512,
                       4 if is_ctx else 1)
        x = _mix_ffn([o.reshape(b * n, C_HEADS * C_DV)], w_out1, x, mods[1], first_row,
                     groups, g_post, g_pre2, *ffn[1], g_post2, ffn_tm, n)

        results[name] = dict(y=x.reshape(b, n, d), gla=(sa_f, sa_b), c=(c_f, c_b),
                             n=n_rows, m=m_rows, ckv=ckv3, kr=kr3)

    r = results["ctx"]
    n_fwd = r["n"][:, 0, :].reshape(bp, B_HEADS, B_DK)
    n_bwd = r["n"][:, 1, :].reshape(bp, B_HEADS, B_DK)
    m_fwd = r["m"][:, 0:B_HEADS, 0]
    m_bwd = r["m"][:, B_HEADS:2 * B_HEADS, 0]
    return (r["y"], results["lat"]["y"], r["gla"][0], r["gla"][1],
            r["c"][0], n_fwd, m_fwd, r["c"][1], n_bwd, m_bwd, r["ckv"], r["kr"])
```

```python
import functools

import numpy as np
import jax
import jax.numpy as jnp
from jax import lax
from jax.experimental import pallas as pl
from jax.experimental.pallas import tpu as pltpu

F32 = jnp.float32
BF16 = jnp.bfloat16

D_MODEL = 1024
EPS = 1e-6
LOG2E = 1.4426950408889634
N_MOD = 6
GRID_W = 64
ROPE_THETA = 10000.0

A_HEADS, A_DK, A_DV, A_GATE_RANK, A_GATE_TEMP = 4, 64, 128, 16, 16.0
B_HEADS, B_DK, B_DV = 4, 64, 128
C_HEADS, C_Q_RANK, C_KV_RANK, C_NOPE, C_ROPE, C_DV = 8, 384, 256, 128, 64, 128
D_FF = 2816
CONV_W = 3

HEADS = 4
HK = HEADS * 64
HV = HEADS * 128
ZA_W = 1536
ZG_W = 128
GATE_COL = 32
RCHUNK = 128
GLA_SAFE_SPAN = 60.0
L1_IN_PAD = 768
QCAT = 256

V7X_VMEM_BYTES = 64 * 1024 * 1024
VMEM_LIMIT = V7X_VMEM_BYTES - 8 * 1024 * 1024


def _cparams(sem):
    return pltpu.CompilerParams(dimension_semantics=sem, vmem_limit_bytes=VMEM_LIMIT)


def _dot(a, b):
    return jnp.dot(a, b, preferred_element_type=F32)


def _dot_nt(a, b):
    return lax.dot_general(a, b, (((1,), (1,)), ((), ())), preferred_element_type=F32)


def _dot_tn(a, b):
    return lax.dot_general(a, b, (((0,), (0,)), ((), ())), preferred_element_type=F32)


def _split3(x):
    hi = x.astype(BF16)
    r1 = x - hi.astype(F32)
    mid = r1.astype(BF16)
    lo = (r1 - mid.astype(F32)).astype(BF16)
    return hi, mid, lo


def _tri_dot(tri, x):
    hi, mid, lo = _split3(x)
    return _dot(tri, hi) + _dot(tri, mid) + _dot(tri, lo)


def _dot_hl(a, b):
    ah = a.astype(BF16)
    al = (a - ah.astype(F32)).astype(BF16)
    bh = b.astype(BF16)
    bl = (b - bh.astype(F32)).astype(BF16)
    return _dot(ah, bh) + _dot(ah, bl) + _dot(al, bh)


def _rms(x, g):
    return x * lax.rsqrt(jnp.mean(x * x, axis=-1, keepdims=True) + EPS) * g


def _sigmoid(x):
    return 1.0 / (1.0 + jnp.exp(-x))


def _silu(x):
    return x * _sigmoid(x)


def _log_sigmoid(x):
    return jnp.minimum(x, 0.0) - jnp.log(1.0 + jnp.exp(-jnp.abs(x)))


def _tri_masks(n):
    r = lax.broadcasted_iota(jnp.int32, (n, n), 0)
    c = lax.broadcasted_iota(jnp.int32, (n, n), 1)
    return c <= r, c >= r


def _mod_kernel(cond_ref, w_ref, b_ref, o_ref):
    s = _silu(cond_ref[...])
    o_ref[...] = _dot(s.astype(BF16), w_ref[...].astype(BF16)) + b_ref[...]


def _modulation(cond8, w_mod, b_mod):
    d = D_MODEL
    out = pl.pallas_call(
        _mod_kernel,
        out_shape=jax.ShapeDtypeStruct((N_MOD, 8, d), F32),
        grid=(N_MOD,),
        in_specs=[pl.BlockSpec((8, d), lambda j: (0, 0)),
                  pl.BlockSpec((d, d), lambda j: (0, j)),
                  pl.BlockSpec((1, d), lambda j: (0, j))],
        out_specs=pl.BlockSpec((None, 8, d), lambda j: (j, 0, 0)),
        compiler_params=_cparams(("arbitrary",)),
        name="modulation",
    )(cond8, w_mod, b_mod.reshape(1, N_MOD * d))
    return out.reshape(N_MOD, 8, 1, d)


MOD_SHIFT1, MOD_SCALE1, MOD_GATE1, MOD_SHIFT2, MOD_SCALE2, MOD_GATE2 = range(N_MOD)


def _mod_spec(which, first_row, per):
    return pl.BlockSpec((None, None, 1, D_MODEL), lambda i: (which, first_row + i // per, 0, 0))


L0_GROUP_ROWS = ((0, ZA_W), (ZA_W + 32, 2 * ZA_W + 32))
L0_LR_ROWS = (ZA_W, ZA_W + 32)
L0_GATE_ROWS = (2 * ZA_W + 32, 2 * ZA_W + 48)


def _p0_kernel(x_ref, g_ref, sc_ref, sh_ref, wt_ref, za_ref, zb_ref, zg_ref, w_ref):
    @pl.when(pl.program_id(0) == 0)
    def _():
        blk = 256
        for gi, (r0, r1) in enumerate(L0_GROUP_ROWS):
            for j in range((r1 - r0) // blk):
                rows = wt_ref[r0 + j * blk:r0 + (j + 1) * blk, :]
                w_ref[:, gi * ZA_W + j * blk:gi * ZA_W + (j + 1) * blk] = rows.T.astype(BF16)
        tail = jnp.concatenate([wt_ref[L0_LR_ROWS[0]:L0_LR_ROWS[1], :],
                                wt_ref[L0_GATE_ROWS[0]:L0_GATE_ROWS[1], :],
                                jnp.zeros((ZG_W - 48, D_MODEL), F32)], axis=0)
        w_ref[:, 2 * ZA_W:2 * ZA_W + ZG_W] = tail.T.astype(BF16)

    h = _rms(x_ref[...], g_ref[...]) * (1.0 + sc_ref[...]) + sh_ref[...]
    hb = h.astype(BF16)
    za_ref[...] = _dot(hb, w_ref[:, 0:ZA_W])
    zb_ref[...] = _dot(hb, w_ref[:, ZA_W:2 * ZA_W])
    zg_ref[...] = _dot(hb, w_ref[:, 2 * ZA_W:2 * ZA_W + ZG_W])


def _proj_l0(x, g, mod, first_row, groups, w_t, tm):
    t = x.shape[0]
    per = (t // tm) // groups
    row = lambda i: (i, 0)
    fixed = lambda i: (0, 0)
    return pl.pallas_call(
        _p0_kernel,
        out_shape=(jax.ShapeDtypeStruct((t, ZA_W), F32),
                   jax.ShapeDtypeStruct((t, ZA_W), F32),
                   jax.ShapeDtypeStruct((t, ZG_W), F32)),
        grid=(t // tm,),
        in_specs=[pl.BlockSpec((tm, D_MODEL), row),
                  pl.BlockSpec((1, D_MODEL), fixed),
                  _mod_spec(MOD_SCALE1, first_row, per),
                  _mod_spec(MOD_SHIFT1, first_row, per),
                  _resident(w_t.shape)],
        out_specs=(pl.BlockSpec((tm, ZA_W), row),
                   pl.BlockSpec((tm, ZA_W), row),
                   pl.BlockSpec((tm, ZG_W), row)),
        scratch_shapes=[pltpu.VMEM((D_MODEL, 2 * ZA_W + ZG_W), BF16)],
        compiler_params=_cparams(("arbitrary",)),
        name="proj_l0",
    )(x, g, mod, mod, w_t)


def _head_lane_mask(shape, h):
    lane = lax.broadcasted_iota(jnp.int32, shape, len(shape) - 1)
    return (lane >= h * 64) & (lane < (h + 1) * 64)


def _head_row(h):
    return jnp.where(_head_lane_mask((1, HK), h), 1.0, 0.0).astype(BF16)


def _gla_prepare(za_ref, la_ref, r0, rev, tri):
    L = RCHUNK
    rows = pl.ds(r0, L)
    b = _tri_dot(tri, la_ref[rows, :])
    mid = L // 2
    ref = b[mid:mid + 1, :]
    b_end = b[0:1, :] if rev else b[L - 1:L, :]
    q = za_ref[rows, 0:HK]
    k = za_ref[rows, HK:2 * HK]
    v = za_ref[rows, 2 * HK:2 * HK + HV].astype(BF16)
    qe = (q * (jnp.exp(b - ref) * (A_DK ** -0.5))).astype(BF16)
    ke = (k * jnp.exp(ref - b)).astype(BF16)
    qm = jnp.concatenate([qe * _head_row(h) for h in range(HEADS)], axis=0)
    rt = _dot_tn(v, ke)
    upd = None
    for h in range(HEADS):
        blk = rt[h * 128:(h + 1) * 128, :]
        blk = jnp.where(_head_lane_mask(blk.shape, h), blk, 0.0)
        upd = blk if upd is None else upd + blk
    return dict(rows=rows, qm=qm, ke=ke, v=v, upd=upd * jnp.exp(b_end - ref),
                decay=jnp.exp(b_end), eref=jnp.exp(ref))


def _gla_apply(items):
    L = RCHUNK
    sts = [st_ref[...] for _, st_ref, _, _ in items]
    o1s = []
    for (p, _, _, _), st in zip(items, sts):
        rhs = jnp.concatenate([p["ke"], (st * p["eref"]).astype(BF16)], axis=0)
        o1s.append(_dot_nt(p["qm"], rhs))
    for h in range(HEADS):
        for (p, _, o_ref, mask), o1 in zip(items, o1s):
            blk = o1[h * L:(h + 1) * L, :]
            sc = jnp.where(mask, blk[:, 0:L], 0.0).astype(BF16)
            o_ref[p["rows"], h * 128:(h + 1) * 128] = (
                blk[:, L:L + 128] + _dot(sc, p["v"][:, h * 128:(h + 1) * 128]))
    for (p, st_ref, _, _), st in zip(items, sts):
        st_ref[...] = st * p["decay"] + p["upd"]


def _gla_exact_scan(za_ref, la_ref, st_ref, o_ref, b_ref, n, rev, st_init):
    L = RCHUNK
    nc = n // L
    m_lo, m_up = _tri_masks(L)
    tri = jnp.where(m_up if rev else m_lo, 1.0, 0.0).astype(BF16)
    s_idx = lax.broadcasted_iota(jnp.int32, (L, 1), 0)
    krow = lax.broadcasted_iota(jnp.int32, (HK, 128), 0) // 64
    kcol = lax.broadcasted_iota(jnp.int32, (HK, 128), 1)
    head_sum = jnp.where(krow == kcol, 1.0, 0.0)
    hsub = lax.broadcasted_iota(jnp.int32, (8, HK), 0)
    hlane = lax.broadcasted_iota(jnp.int32, (8, HK), 1) // 64
    head_rows = jnp.where(hsub == hlane, 1.0, 0.0)
    st_ref[...] = st_init

    def chunk_body(ci, carry):
        c = nc - 1 - ci if rev else ci
        r0 = pl.multiple_of(c * L, L)
        rows = pl.ds(r0, L)
        b = _tri_dot(tri, la_ref[rows, :])
        b_ref[...] = b
        b_end = b[0:1, :] if rev else b[L - 1:L, :]
        k = za_ref[rows, HK:2 * HK]
        vb = za_ref[rows, 2 * HK:2 * HK + HV].astype(BF16)
        st = st_ref[...]
        stb = st.astype(BF16)

        def row_body(t, carry2):
            b_t = b_ref[pl.ds(t, 1), :]
            q_t = za_ref[pl.ds(r0 + t, 1), 0:HK] * (A_DK ** -0.5)
            valid = (s_idx >= t) if rev else (s_idx <= t)
            w = jnp.exp(jnp.minimum(b_t - b_ref[...], 0.0))
            p = jnp.where(valid, (q_t * k) * w, 0.0)
            sc = jnp.dot(p, head_sum, precision=lax.Precision.HIGHEST,
                         preferred_element_type=F32)
            pv = _dot_tn(sc.astype(BF16), vb)
            qm = (head_rows * (q_t * jnp.exp(b_t))).astype(BF16)
            inter = _dot_nt(qm, stb)
            o_ref[pl.ds(r0 + t, 1), :] = jnp.concatenate(
                [pv[h:h + 1, h * 128:(h + 1) * 128] + inter[h:h + 1, :] for h in range(HEADS)],
                axis=1)
            return carry2

        lax.fori_loop(0, L, row_body, 0)
        kd = (k * jnp.exp(b_end - b)).astype(BF16)
        rt = _dot_tn(vb, kd)
        upd = None
        for h in range(HEADS):
            blk = rt[h * 128:(h + 1) * 128, :]
            blk = jnp.where(_head_lane_mask(blk.shape, h), blk, 0.0)
            upd = blk if upd is None else upd + blk
        st_ref[...] = st * jnp.exp(b_end) + upd
        return carry

    lax.fori_loop(0, nc, chunk_body, 0)


T_M, T_MOUT, T_G, T_SRCW, T_MPREV, T_CARRY, T_USED = 0, 8, 16, 24, 32, 40, 48


def _lane_scan(x, op, fill, is_fwd, pos, n):
    k = 1
    while k < RCHUNK:
        pf = jnp.where(pos >= k, pltpu.roll(x, k, axis=1), fill)
        sf = jnp.where(pos < RCHUNK - k, pltpu.roll(x, n - k, axis=1), fill)
        x = op(x, jnp.where(is_fwd, pf, sf))
        k *= 2
    return x


def _mlstm_gate_tables(g, m0, gt_ref, tab_ref, col_ref, n):
    L = RCHUNK
    nc = n // L
    for c in range(nc):
        gt_ref[:, c * L:(c + 1) * L] = g[c * L:(c + 1) * L, :].T
        tab_ref[c, T_USED:, :] = jnp.zeros((ZG_W - T_USED, L), F32)
    gi = gt_ref[GATE_COL:GATE_COL + 8, :]
    lf = gt_ref[GATE_COL + 8:GATE_COL + 16, :]
    is_fwd = lax.broadcasted_iota(jnp.int32, (8, n), 0) < HEADS
    pos = lax.broadcasted_iota(jnp.int32, (8, n), 1) % L
    b = _lane_scan(lf, jnp.add, 0.0, is_fwd, pos, n)
    gg = gi - b
    cmax = _lane_scan(gg, jnp.maximum, -jnp.inf, is_fwd, pos, n)
    fwd_l = lax.broadcasted_iota(jnp.int32, (8, L), 0) < HEADS
    mp = m0
    for j in range(nc):
        cf, cb = j, nc - 1 - j

        def pick(x):
            return jnp.where(fwd_l, x[:, cf * L:(cf + 1) * L], x[:, cb * L:(cb + 1) * L])

        gj, bj, cj, lj = pick(gg), pick(b), pick(cmax), pick(lf)
        mx = jnp.maximum(mp, jnp.max(gj, axis=1, keepdims=True))
        mj = jnp.maximum(cj, mp)
        blocks = (mj, bj + mj, gj, jnp.exp(gj - mx),
                  jnp.broadcast_to(mp, (8, L)), jnp.broadcast_to(jnp.exp(mp - mx), (8, L)))
        for t, val in enumerate(blocks):
            tab_ref[cf, 8 * t:8 * t + HEADS, :] = val[0:HEADS]
            tab_ref[cb, 8 * t + HEADS:8 * t + 8, :] = val[HEADS:8]
        mp = jnp.sum(lj, axis=1, keepdims=True) + mx
    for c in range(nc):
        col_ref[c * L:(c + 1) * L, :] = tab_ref[c].T
    return mp


def _mlstm_prepare(zb_ref, tab_ref, col_ref, c, d, mask):
    L = RCHUNK
    rows = pl.ds(pl.multiple_of(c * L, L), L)
    qb = zb_ref[rows, 0:HK].astype(BF16)
    kb = (zb_ref[rows, HK:2 * HK] * (B_DK ** -0.5)).astype(BF16)
    v = zb_ref[rows, 2 * HK:2 * HK + HV]
    vb = v.astype(BF16)
    vt = v.T
    qm = jnp.concatenate([qb * _head_row(h) for h in range(HEADS)], axis=0)
    s = _dot_nt(qm, kb)
    ones = jnp.ones((L, 128), BF16)
    pv, w_inter, e_negm = [], [], []
    upd = None
    carry = None
    for h in range(HEADS):
        r = 4 * d + h
        m_rep = jnp.broadcast_to(col_ref[rows, T_M + r:T_M + r + 1], (L, 128))
        mout_rep = jnp.broadcast_to(col_ref[rows, T_MOUT + r:T_MOUT + r + 1], (L, 128))
        g_row = tab_ref[c, T_G + r:T_G + r + 1, :]
        w = jnp.exp(jnp.where(mask, g_row - m_rep, -jnp.inf))
        qk = (s[h * L:(h + 1) * L, :] * w).astype(BF16)
        v1 = jnp.concatenate([vb[:, h * 128:(h + 1) * 128], ones], axis=1)
        pv.append(_dot(qk, v1))
        w_inter.append(jnp.exp(tab_ref[c, T_MPREV + r:T_MPREV + r + 1, :] - m_rep))
        e_negm.append(jnp.exp(-mout_rep))
        sw = tab_ref[c, T_SRCW + r:T_SRCW + r + 1, :]
        lhs = jnp.concatenate([vt[h * 128:(h + 1) * 128, :] * sw, jnp.broadcast_to(sw, (128, L))],
                              axis=0).astype(BF16)
        blk = _dot(lhs, kb)
        blk = jnp.where(_head_lane_mask(blk.shape, h), blk, 0.0)
        upd = blk if upd is None else upd + blk
        cr = tab_ref[c, T_CARRY + r:T_CARRY + r + 1, :]
        cr = jnp.where(_head_lane_mask((1, HK), h), jnp.concatenate([cr, cr], axis=1), 0.0)
        carry = cr if carry is None else carry + cr
    return dict(rows=rows, qm=qm, pv=pv, w_inter=w_inter, e_negm=e_negm, upd=upd, carry=carry)


def _mlstm_apply(items):
    L = RCHUNK
    sts = [st_ref[...] for _, st_ref, _ in items]
    a_all = [_dot_nt(p["qm"], st.astype(BF16)) for (p, _, _), st in zip(items, sts)]
    for h in range(HEADS):
        for (p, _, h_ref), a in zip(items, a_all):
            blk = a[h * L:(h + 1) * L, :]
            pv, w, e = p["pv"][h], p["w_inter"][h], p["e_negm"][h]
            num = w * blk[:, 0:128] + pv[:, 0:128]
            den = w * blk[:, 128:256] + pv[:, 128:256]
            h_ref[p["rows"], h * 128:(h + 1) * 128] = num / jnp.maximum(jnp.abs(den), e)
    for (p, st_ref, _), st in zip(items, sts):
        st_ref[...] = p["carry"] * st + p["upd"]


def _rec_kernel(*refs, n, has_init):
    it = iter(refs)

    def take(k):
        return [next(it) for _ in range(k)]

    za_all, zb_all, zg_all, zgn_all = take(4)
    wgf_ref, wgb_ref, bgf_ref, bgb_ref, gna_ref, bg_ref, gnb_ref = take(7)
    if has_init:
        init_all = take(7)
    out_all = take(8)
    (gt_all,) = take(1)
    set_a_all = take(6)
    set_b_all = take(6)
    work_all = take(9)
    L = RCHUNK
    nc = n // L
    step = pl.program_id(0)
    elems = []
    for e in range(za_all.shape[0]):
        el = dict(za=za_all.at[e], zb=zb_all.at[e], zg=zg_all.at[e], zgn=zgn_all.at[e],
                  gt=gt_all.at[e], set_a=[r.at[e] for r in set_a_all],
                  set_b=[r.at[e] for r in set_b_all])
        (el["oa"], el["sf"], el["sb"], el["om"], el["cf"], el["cb"], el["n_out"],
         el["m_out"]) = [r.at[e] for r in out_all]
        (el["of"], el["ob"], el["hf"], el["hb"], el["gsf"], el["gsb"], el["msf"],
         el["msb"], el["b_exact"]) = [r.at[e] for r in work_all]
        if has_init:
            el["init"] = [r.at[e] for r in init_all]
        elems.append(el)

    def fill(gate_ref, m0, bufs, gt_ref):
        laf_ref, lab_ref, tab_ref, col_ref, mf_ref, span_ref = bufs
        zg = gate_ref[...]
        laf = _log_sigmoid(_dot_hl(zg, wgf_ref[...]) + bgf_ref[...]) / A_GATE_TEMP
        lab = _log_sigmoid(_dot_hl(zg, wgb_ref[...]) + bgb_ref[...]) / A_GATE_TEMP
        laf_ref[...] = laf
        lab_ref[...] = lab
        span = None
        for la in (laf, lab):
            for c in range(nc):
                tot = -jnp.sum(la[c * L:(c + 1) * L, :], axis=0, keepdims=True)
                span = tot if span is None else jnp.maximum(span, tot)
        span_ref[...] = jnp.broadcast_to(jnp.max(span, axis=1, keepdims=True), (8, 128))
        g = zg + bg_ref[...]
        lane = lax.broadcasted_iota(jnp.int32, g.shape, 1)
        is_forget = (lane >= GATE_COL + 8) & (lane < GATE_COL + 16)
        g = jnp.where(is_forget, _log_sigmoid(g), g)
        m_fin = _mlstm_gate_tables(g, m0, gt_ref, tab_ref, col_ref, n)
        mf_ref[...] = jnp.broadcast_to(m_fin, (8, 128))

    for el in elems:
        if has_init:
            sf0_ref, sb0_ref, cf0_ref, cb0_ref, n0_ref, m0_ref, m0n_ref = el["init"]
            el["m0"], el["m0_next"] = m0_ref[:, 0:1], m0n_ref[:, 0:1]
            el["gsf"][...] = sf0_ref[...]
            el["gsb"][...] = sb0_ref[...]
            el["msf"][...] = jnp.concatenate(
                [cf0_ref[...], jnp.broadcast_to(n0_ref[0:1, :], (128, HK))], axis=0)
            el["msb"][...] = jnp.concatenate(
                [cb0_ref[...], jnp.broadcast_to(n0_ref[1:2, :], (128, HK))], axis=0)
        else:
            el["m0"] = el["m0_next"] = jnp.zeros((8, 1), F32)
            for name in ("gsf", "gsb", "msf", "msb"):
                el[name][...] = jnp.zeros(el[name].shape, F32)

    @pl.when(step == 0)
    def _():
        for el in elems:
            fill(el["zg"], el["m0"], el["set_a"], el["gt"])

    m_lo, m_up = _tri_masks(L)
    tri_lo = jnp.where(m_lo, 1.0, 0.0).astype(BF16)
    tri_up = jnp.where(m_up, 1.0, 0.0).astype(BF16)
    gna = gna_ref[...]
    gnb = gnb_ref[...]

    def out_body(c, carry):
        rows = pl.ds(pl.multiple_of(c * L, L), L)
        for h in range(HEADS):
            cols = slice(h * 128, (h + 1) * 128)
            gcols = slice(2 * HK + HV + h * 128, 2 * HK + HV + (h + 1) * 128)
            for el in elems:
                o = el["of"][rows, cols] + el["ob"][rows, cols]
                el["oa"][rows, cols] = (
                    _rms(o, gna) * _silu(el["za"][rows, gcols])).astype(BF16)
                hh = el["hf"][rows, cols] + el["hb"][rows, cols]
                el["om"][rows, cols] = (
                    _sigmoid(el["zb"][rows, gcols]) * _rms(hh, gnb)).astype(BF16)
        return carry

    def run(cur, nxt):
        for el in elems:
            fill(el["zgn"], el["m0_next"], el[nxt], el["gt"])

        def scan_body(gidx, carry):
            waves = []
            for w in range(2):
                c = 2 * gidx + w
                cb = nc - 1 - c
                gla_items, mlstm_items = [], []
                for el in elems:
                    laf_ref, lab_ref, tab_ref, col_ref = el[cur][0:4]
                    gla_items += [
                        (_gla_prepare(el["za"], laf_ref, pl.multiple_of(c * L, L), False, tri_lo),
                         el["gsf"], el["of"], m_lo),
                        (_gla_prepare(el["za"], lab_ref, pl.multiple_of(cb * L, L), True, tri_up),
                         el["gsb"], el["ob"], m_up)]
                    mlstm_items += [
                        (_mlstm_prepare(el["zb"], tab_ref, col_ref, c, 0, m_lo), el["msf"], el["hf"]),
                        (_mlstm_prepare(el["zb"], tab_ref, col_ref, cb, 1, m_up), el["msb"], el["hb"])]
                waves.append((gla_items, mlstm_items))
            for gla_items, mlstm_items in waves:
                _gla_apply(gla_items)
                _mlstm_apply(mlstm_items)
            return carry

        if nc == 2:
            scan_body(0, 0)
        else:
            lax.fori_loop(0, nc // 2, scan_body, 0)
        for el in elems:
            laf_ref, lab_ref = el[cur][0:2]
            too_fast = el[cur][5][0, 0] > GLA_SAFE_SPAN

            @pl.when(too_fast)
            def _(el=el, laf_ref=laf_ref, lab_ref=lab_ref):
                zero = jnp.zeros((128, HK), F32)
                init_f = el["init"][0][...] if has_init else zero
                init_b = el["init"][1][...] if has_init else zero
                _gla_exact_scan(el["za"], laf_ref, el["gsf"], el["of"], el["b_exact"], n, False, init_f)
                _gla_exact_scan(el["za"], lab_ref, el["gsb"], el["ob"], el["b_exact"], n, True, init_b)

        if nc == 2:
            for c in range(nc):
                out_body(c, 0)
        else:
            lax.fori_loop(0, nc, out_body, 0)
        for el in elems:
            el["m_out"][...] = el[cur][4][...]

    @pl.when(step % 2 == 0)
    def _():
        run("set_a", "set_b")

    @pl.when(step % 2 == 1)
    def _():
        run("set_b", "set_a")

    for el in elems:
        gsf_t = el["gsf"][...].T
        gsb_t = el["gsb"][...].T
        msf = el["msf"][...]
        msb = el["msb"][...]
        ctf_t = msf[0:128, :].T
        ctb_t = msb[0:128, :].T
        for h in range(HEADS):
            rows = slice(h * 64, (h + 1) * 64)
            el["sf"][h] = gsf_t[rows, :]
            el["sb"][h] = gsb_t[rows, :]
            el["cf"][h] = ctf_t[rows, :]
            el["cb"][h] = ctb_t[rows, :]
        el["n_out"][...] = jnp.concatenate(
            [msf[128:129, :], msb[128:129, :], jnp.zeros((6, HK), F32)], axis=0)


def _recurrent(za, zb, zg, wgf, wgb, bgf, bgb, gna, bg, gnb, init, bt):
    bsz, n, _ = za.shape
    has_init = init is not None
    nc = n // RCHUNK
    steps = bsz // bt
    per_b3 = lambda b: (b, 0, 0)
    next_b3 = lambda b: (jnp.minimum(b + 1, steps - 1), 0, 0)
    per_b4 = lambda b: (b, 0, 0, 0)
    fixed = lambda b: (0, 0)
    in_specs = [pl.BlockSpec((bt, n, ZA_W), per_b3),
                pl.BlockSpec((bt, n, ZA_W), per_b3),
                pl.BlockSpec((bt, n, ZG_W), per_b3),
                pl.BlockSpec((bt, n, ZG_W), next_b3),
                pl.BlockSpec((ZG_W, HK), fixed),
                pl.BlockSpec((ZG_W, HK), fixed),
                pl.BlockSpec((1, HK), fixed),
                pl.BlockSpec((1, HK), fixed),
                pl.BlockSpec((1, 128), fixed),
                pl.BlockSpec((1, ZG_W), fixed),
                pl.BlockSpec((1, 128), fixed)]
    args = [za, zb, zg, zg, wgf, wgb, bgf, bgb, gna, bg, gnb]
    if has_init:
        sf0, sb0, cf0, cb0, n0, m0 = init
        in_specs += [pl.BlockSpec((bt, 128, HK), per_b3)] * 4
        in_specs += [pl.BlockSpec((bt, 8, HK), per_b3), pl.BlockSpec((bt, 8, 128), per_b3),
                     pl.BlockSpec((bt, 8, 128), next_b3)]
        args += [sf0, sb0, cf0, cb0, n0, m0, m0]
    state = jax.ShapeDtypeStruct((bsz, HEADS, 64, 128), F32)
    state_spec = pl.BlockSpec((bt, HEADS, 64, 128), per_b4)

    def vmem(*shape):
        return pltpu.VMEM((bt,) + shape, F32)

    gate_set = [vmem(n, HK), vmem(n, HK), vmem(nc, ZG_W, RCHUNK), vmem(n, ZG_W), vmem(8, 128),
                vmem(8, 128)]
    return pl.pallas_call(
        functools.partial(_rec_kernel, n=n, has_init=has_init),
        out_shape=(jax.ShapeDtypeStruct((bsz, n, HV), BF16), state, state,
                   jax.ShapeDtypeStruct((bsz, n, HV), BF16), state, state,
                   jax.ShapeDtypeStruct((bsz, 8, HK), F32),
                   jax.ShapeDtypeStruct((bsz, 8, 128), F32)),
        grid=(steps,),
        in_specs=in_specs,
        out_specs=(pl.BlockSpec((bt, n, HV), per_b3), state_spec, state_spec,
                   pl.BlockSpec((bt, n, HV), per_b3), state_spec, state_spec,
                   pl.BlockSpec((bt, 8, HK), per_b3),
                   pl.BlockSpec((bt, 8, 128), per_b3)),
        scratch_shapes=([vmem(ZG_W, n)] + gate_set + gate_set + [vmem(n, HV)] * 4
                        + [vmem(128, HK)] * 2 + [vmem(2 * 128, HK)] * 2 + [vmem(RCHUNK, HK)]),
        compiler_params=_cparams(("arbitrary",)),
        name="recurrent",
    )(*args)


def _mix_residual(parts, wo_ref, x_ref, gt1_ref, gpost1_ref, gpre2_ref, sc2_ref, sh2_ref):
    m = None
    off = 0
    for p in parts:
        kp = p.shape[1]
        t = _dot(p[...], wo_ref[off:off + kp, :])
        m = t if m is None else m + t
        off += kp
    x1 = x_ref[...] + gt1_ref[...] * _rms(m, gpost1_ref[...])
    hb = (_rms(x1, gpre2_ref[...]) * (1.0 + sc2_ref[...]) + sh2_ref[...]).astype(BF16)
    return x1, hb


def _ffn_gated_up(hb, wu_ref, cw_ref, cb_ref, act_ref, seq, tf):
    tm = hb.shape[0]
    pos = lax.broadcasted_iota(jnp.int32, (tm, 1), 0) % seq
    first = pos == 0
    last = pos == seq - 1

    def conv(u, cols):
        prev = jnp.where(first, 0.0, pltpu.roll(u, 1, axis=0))
        nxt = jnp.where(last, 0.0, pltpu.roll(u, tm - 1, axis=0))
        return (cw_ref[0:1, cols] * prev + cw_ref[1:2, cols] * u + cw_ref[2:3, cols] * nxt
                + cb_ref[:, cols])

    for j in range(D_FF // tf):
        ca = slice(j * tf, (j + 1) * tf)
        cg = slice(D_FF + j * tf, D_FF + (j + 1) * tf)
        a = conv(_dot(hb, wu_ref[:, ca]), ca)
        g = conv(_dot(hb, wu_ref[:, cg]), cg)
        act_ref[:, ca] = (_silu(g) * a).astype(BF16)


def _mix_ffn_kernel(*refs, n_parts, seq, tf):
    parts = refs[:n_parts]
    (wo_ref, x_ref, gt1_ref, gpost1_ref, gpre2_ref, sc2_ref, sh2_ref,
     wu_ref, cw_ref, cb_ref, wd_ref, gt2_ref, gpost2_ref, o_ref, act_ref) = refs[n_parts:]
    x1, hb = _mix_residual(parts, wo_ref, x_ref, gt1_ref, gpost1_ref, gpre2_ref, sc2_ref, sh2_ref)
    o_ref[...] = x1
    _ffn_gated_up(hb, wu_ref, cw_ref, cb_ref, act_ref, seq, tf)
    y = _dot(act_ref[...], wd_ref[...])
    o_ref[...] = o_ref[...] + gt2_ref[...] * _rms(y, gpost2_ref[...])


def _resident(shape):
    return pl.BlockSpec(shape, lambda *_: (0,) * len(shape), pipeline_mode=pl.Buffered(1))


def _mix_ffn(parts, w_out, x, mod, first_row, groups, gpost1, gpre2, w_up, conv_w, conv_b, w_down,
             gpost2, tm, seq, in_place, tf=256):
    t = x.shape[0]
    per = (t // tm) // groups
    row = lambda i: (i, 0)
    vec = _resident((1, D_MODEL))
    in_specs = [pl.BlockSpec((tm, p.shape[1]), row) for p in parts]
    in_specs += [_resident(w_out.shape), pl.BlockSpec((tm, D_MODEL), row),
                 _mod_spec(MOD_GATE1, first_row, per), vec, vec,
                 _mod_spec(MOD_SCALE2, first_row, per), _mod_spec(MOD_SHIFT2, first_row, per),
                 _resident(w_up.shape), _resident(conv_w.shape), _resident(conv_b.shape),
                 _resident(w_down.shape), _mod_spec(MOD_GATE2, first_row, per), vec]
    return pl.pallas_call(
        functools.partial(_mix_ffn_kernel, n_parts=len(parts), seq=seq, tf=tf),
        out_shape=jax.ShapeDtypeStruct((t, D_MODEL), F32),
        grid=(t // tm,),
        in_specs=in_specs,
        out_specs=pl.BlockSpec((tm, D_MODEL), row),
        scratch_shapes=[pltpu.VMEM((tm, D_FF), BF16)],
        input_output_aliases={len(parts) + 1: 0} if in_place else {},
        compiler_params=_cparams(("arbitrary",)),
        name="mix_ffn",
    )(*parts, w_out, x, mod, gpost1, gpre2, mod, mod, w_up, conv_w, conv_b, w_down, mod, gpost2)


def _rope(x, cos, sin_signed):
    lane = lax.broadcasted_iota(jnp.int32, x.shape, 1)
    lower = (lane % 32) < 16
    partner = jnp.where(lower, pltpu.roll(x, 128 - 16, axis=1), pltpu.roll(x, 16, axis=1))
    return x * cos + partner * sin_signed


def _p1_kernel(*refs, rope):
    if rope:
        (x_ref, g_ref, sc_ref, sh_ref, w_ref, gq_ref, wq_ref, gkv_ref, cos_ref, sin_ref,
         q_ref, ckv_ref, kr_ref) = refs
    else:
        (x_ref, g_ref, sc_ref, sh_ref, w_ref, gq_ref, wq_ref, gkv_ref,
         q_ref, ckv_ref, kr_ref) = refs
    h = _rms(x_ref[...], g_ref[...]) * (1.0 + sc_ref[...]) + sh_ref[...]
    z = _dot(h.astype(BF16), w_ref[...])
    qa = _rms(z[:, 0:C_Q_RANK], gq_ref[...])
    q = _dot(qa.astype(BF16), wq_ref[...])
    ckv_ref[...] = _rms(z[:, C_Q_RANK:C_Q_RANK + C_KV_RANK], gkv_ref[...])
    kr = z[:, C_Q_RANK + C_KV_RANK:L1_IN_PAD]
    if rope:
        cos = cos_ref[...]
        sin = sin_ref[...]
        kr = _rope(kr, cos, sin)
    kr_ref[...] = kr[:, 0:C_ROPE]
    for hd in range(C_HEADS):
        c0 = hd * QCAT
        q_ref[:, c0:c0 + 128] = q[:, c0:c0 + 128].astype(q_ref.dtype)
        qr = q[:, c0 + 128:c0 + 256]
        if rope:
            qr = _rope(qr, cos, sin)
        q_ref[:, c0 + 128:c0 + 256] = qr.astype(q_ref.dtype)


def _proj_l1(x, g, mod, first_row, groups, w, gq, wq, gkv, tables, tm):
    t = x.shape[0]
    per = (t // tm) // groups
    row = lambda i: (i, 0)
    fixed = lambda i: (0, 0)
    in_specs = [pl.BlockSpec((tm, D_MODEL), row),
                pl.BlockSpec((1, D_MODEL), fixed),
                _mod_spec(MOD_SCALE1, first_row, per),
                _mod_spec(MOD_SHIFT1, first_row, per),
                pl.BlockSpec(w.shape, fixed),
                pl.BlockSpec((1, C_Q_RANK), fixed),
                pl.BlockSpec(wq.shape, fixed),
                pl.BlockSpec((1, C_KV_RANK), fixed)]
    args = [x, g, mod, mod, w, gq, wq, gkv]
    rope = tables is not None
    if rope:
        n = tables[0].shape[0]
        pos = lambda i: (i % (n // tm), 0)
        in_specs += [pl.BlockSpec((tm, 128), pos)] * 2
        args += list(tables)
    return pl.pallas_call(
        functools.partial(_p1_kernel, rope=rope),
        out_shape=(jax.ShapeDtypeStruct((t, C_HEADS * QCAT), BF16),
                   jax.ShapeDtypeStruct((t, C_KV_RANK), F32),
                   jax.ShapeDtypeStruct((t, C_ROPE), F32)),
        grid=(t // tm,),
        in_specs=in_specs,
        out_specs=(pl.BlockSpec((tm, C_HEADS * QCAT), row),
                   pl.BlockSpec((tm, C_KV_RANK), row),
                   pl.BlockSpec((tm, C_ROPE), row)),
        compiler_params=_cparams(("arbitrary",)),
        name="proj_l1",
    )(*args)


def _attn_kernel(*refs, n_parts, scale):
    q_ref = refs[0]
    kv_parts = refs[1:1 + 2 * n_parts]
    w_ref, o_ref, k_sc, v_sc = refs[1 + 2 * n_parts:]
    bt = q_ref.shape[0]

    @pl.when(pl.program_id(1) == 0)
    def _():
        off = 0
        for p in range(n_parts):
            ckv_ref, kr_ref = kv_parts[2 * p], kv_parts[2 * p + 1]
            nk = ckv_ref.shape[1]
            ones = jnp.ones((nk, C_DV), BF16)
            for e in range(bt):
                kv = _dot(ckv_ref[e].astype(BF16), w_ref[...])
                kr = kr_ref[e].astype(BF16)
                kr2 = jnp.concatenate([kr, kr], axis=1)
                for hd in range(C_HEADS):
                    v0 = C_HEADS * C_NOPE + hd * C_DV
                    k_sc[e, hd, off:off + nk, 0:128] = kv[:, hd * 128:(hd + 1) * 128].astype(BF16)
                    k_sc[e, hd, off:off + nk, 128:256] = kr2
                    v_sc[e, hd, off:off + nk, 0:C_DV] = kv[:, v0:v0 + C_DV].astype(BF16)
                    v_sc[e, hd, off:off + nk, C_DV:2 * C_DV] = ones
            off += nk

    for hd in range(C_HEADS):
        for e in range(bt):
            t = _dot_nt(q_ref[e, :, hd * QCAT:(hd + 1) * QCAT], k_sc[e, hd]) * (scale * LOG2E)
            p = jnp.exp2(t - jnp.max(t, axis=1, keepdims=True))
            ov = _dot(p.astype(BF16), v_sc[e, hd])
            o_ref[e, :, hd * C_DV:(hd + 1) * C_DV] = (
                ov[:, 0:C_DV] / ov[:, C_DV:2 * C_DV]).astype(o_ref.dtype)


def _attention(q, kv_parts, w_kvb, tq, bt):
    bsz, n, _ = q.shape
    nk = sum(p[0].shape[1] for p in kv_parts)
    in_specs = [pl.BlockSpec((bt, tq, C_HEADS * QCAT), lambda b, i: (b, i, 0))]
    args = [q]
    for ckv, kr in kv_parts:
        in_specs.append(pl.BlockSpec((bt, ckv.shape[1], C_KV_RANK), lambda b, i: (b, 0, 0)))
        in_specs.append(pl.BlockSpec((bt, kr.shape[1], C_ROPE), lambda b, i: (b, 0, 0)))
        args += [ckv, kr]
    in_specs.append(pl.BlockSpec(w_kvb.shape, lambda b, i: (0, 0)))
    args.append(w_kvb)
    return pl.pallas_call(
        functools.partial(_attn_kernel, n_parts=len(kv_parts),
                          scale=(C_NOPE + C_ROPE) ** -0.5),
        out_shape=jax.ShapeDtypeStruct((bsz, n, C_HEADS * C_DV), BF16),
        grid=(bsz // bt, n // tq),
        in_specs=in_specs,
        out_specs=pl.BlockSpec((bt, tq, C_HEADS * C_DV), lambda b, i: (b, i, 0)),
        scratch_shapes=[pltpu.VMEM((bt, C_HEADS, nk, QCAT), BF16),
                        pltpu.VMEM((bt, C_HEADS, nk, 2 * C_DV), BF16)],
        compiler_params=_cparams(("arbitrary", "arbitrary")),
        name="mla_attention",
    )(*args)


def _l1_wq_layout(w_qb):
    blocks = []
    zero = jnp.zeros((w_qb.shape[0], C_ROPE), w_qb.dtype)
    for hd in range(C_HEADS):
        base = hd * (C_NOPE + C_ROPE)
        nope = w_qb[:, base:base + C_NOPE]
        rope = w_qb[:, base + C_NOPE:base + C_NOPE + C_ROPE]
        blocks += [nope, rope, zero] if hd % 2 == 0 else [nope, zero, rope]
    return jnp.concatenate(blocks, axis=1).astype(BF16)


def _l1_wkvb_layout(w_kvb):
    w = w_kvb.reshape(C_KV_RANK, C_HEADS, C_NOPE + C_DV)
    k = w[:, :, :C_NOPE].reshape(C_KV_RANK, C_HEADS * C_NOPE)
    v = w[:, :, C_NOPE:].reshape(C_KV_RANK, C_HEADS * C_DV)
    return jnp.concatenate([k, v], axis=1).astype(BF16)


def _rope_tables(n):
    pos = np.arange(n)
    row = (pos // GRID_W).astype(np.float64)
    col = (pos % GRID_W).astype(np.float64)
    half = C_ROPE // 2
    inv = 1.0 / (ROPE_THETA ** (np.arange(0, half, 2, dtype=np.float64) / half))
    ang_r = row[:, None] * inv[None, :]
    ang_c = col[:, None] * inv[None, :]
    cos64 = np.concatenate([np.cos(ang_r)] * 2 + [np.cos(ang_c)] * 2, axis=1)
    sin64 = np.concatenate([-np.sin(ang_r), np.sin(ang_r), -np.sin(ang_c), np.sin(ang_c)], axis=1)
    return (jnp.asarray(np.concatenate([cos64] * 2, axis=1), F32),
            jnp.asarray(np.concatenate([sin64] * 2, axis=1), F32))


def _gate_weight(w, first_row):
    return jnp.pad(w, ((first_row, ZG_W - first_row - A_GATE_RANK), (0, 0)))


def _state_t(s):
    b = s.shape[0]
    return s.transpose(0, 3, 1, 2).reshape(b, 128, HK)


def _n_rows(n_fwd, n_bwd):
    b = n_fwd.shape[0]
    rows = jnp.zeros((b, 8, HK), F32)
    return rows.at[:, 0, :].set(n_fwd.reshape(b, HK)).at[:, 1, :].set(n_bwd.reshape(b, HK))


def _m_rows(m_fwd, m_bwd):
    m = jnp.concatenate([m_fwd, m_bwd], axis=1)
    return jnp.broadcast_to(m[:, :, None], (m.shape[0], 8, 128))


def kernel(x_prompt, x_sample, state_l0_gla_fwd, state_l0_gla_bwd, state_l0_mlstm_c_fwd, state_l0_mlstm_n_fwd, state_l0_mlstm_m_fwd, state_l0_mlstm_c_bwd, state_l0_mlstm_n_bwd, state_l0_mlstm_m_bwd, cache_l1_ckv, cache_l1_krope, c, c_ctx, l0_w_mod, l0_b_mod, l0_g_pre_mix, l0_g_post_mix, l0_g_pre_ffn, l0_g_post_ffn, l0_w_in, l0_gla_w_gate_f, l0_gla_b_gate_f, l0_gla_w_gate_b, l0_gla_b_gate_b, l0_gla_g_norm, l0_mlstm_b_gates, l0_mlstm_g_norm, l0_w_out, l0_ffn_w_up, l0_ffn_conv_w, l0_ffn_conv_b, l0_ffn_w_down, l1_w_mod, l1_b_mod, l1_g_pre_mix, l1_g_post_mix, l1_g_pre_ffn, l1_g_post_ffn, l1_w_in, l1_g_q_norm, l1_w_qb, l1_g_kv_norm, l1_w_kvb, l1_w_out, l1_ffn_w_up, l1_ffn_conv_w, l1_ffn_conv_b, l1_ffn_w_down):
    bp, sp, d = x_prompt.shape
    bs, ss, _ = x_sample.shape
    row = lambda v: v.reshape(1, -1)

    cond8 = jnp.concatenate([c_ctx[None, :], c, jnp.zeros((8 - 1 - bs, d), F32)], axis=0)
    mods = (_modulation(cond8, l0_w_mod, l0_b_mod), _modulation(cond8, l1_w_mod, l1_b_mod))

    paths = {
        "ctx": dict(x=x_prompt.reshape(bp * sp, d), b=bp, n=sp, first_row=0, groups=1, tm=512),
        "lat": dict(x=x_sample.reshape(bs * ss, d), b=bs, n=ss, first_row=1, groups=bs, tm=512),
    }

    w_in0 = l0_w_in.T
    wgf = _gate_weight(l0_gla_w_gate_f, 0)
    wgb = _gate_weight(l0_gla_w_gate_b, A_GATE_RANK)
    bg = jnp.pad(l0_mlstm_b_gates, (GATE_COL, ZG_W - GATE_COL - 16)).reshape(1, ZG_W)
    w_out0 = l0_w_out.astype(BF16)
    w_in1 = jnp.pad(l1_w_in, ((0, 0), (0, L1_IN_PAD - l1_w_in.shape[1]))).astype(BF16)
    w_q1 = _l1_wq_layout(l1_w_qb)
    w_kvb1 = _l1_wkvb_layout(l1_w_kvb)
    w_out1 = l1_w_out.astype(BF16)
    ffn = ((l0_ffn_w_up.astype(BF16), l0_ffn_conv_w, row(l0_ffn_conv_b), l0_ffn_w_down.astype(BF16)),
           (l1_ffn_w_up.astype(BF16), l1_ffn_conv_w, row(l1_ffn_conv_b), l1_ffn_w_down.astype(BF16)))
    norms = ((row(l0_g_pre_mix), row(l0_g_post_mix), row(l0_g_pre_ffn), row(l0_g_post_ffn)),
             (row(l1_g_pre_mix), row(l1_g_post_mix), row(l1_g_pre_ffn), row(l1_g_post_ffn)))

    gla_init = (_state_t(state_l0_gla_fwd), _state_t(state_l0_gla_bwd))
    mlstm_init = (_state_t(state_l0_mlstm_c_fwd), _state_t(state_l0_mlstm_c_bwd),
                  _n_rows(state_l0_mlstm_n_fwd, state_l0_mlstm_n_bwd),
                  _m_rows(state_l0_mlstm_m_fwd, state_l0_mlstm_m_bwd))
    tables = _rope_tables(ss)

    results = {}
    for name, p in paths.items():
        x, b, n, tm = p["x"], p["b"], p["n"], p["tm"]
        first_row, groups = p["first_row"], p["groups"]
        is_ctx = name == "ctx"
        ffn_tm = tm if is_ctx else n
        g_pre, g_post, g_pre2, g_post2 = norms[0]

        za, zb, zg = _proj_l0(x, g_pre, mods[0], first_row, groups, w_in0, tm)
        za, zb, zg = (t.reshape(b, n, -1) for t in (za, zb, zg))
        oa, sa_f, sa_b, om, c_f, c_b, n_rows, m_rows = _recurrent(
            za, zb, zg, wgf, wgb, row(l0_gla_b_gate_f), row(l0_gla_b_gate_b), row(l0_gla_g_norm),
            bg, row(l0_mlstm_g_norm), None if is_ctx else gla_init + mlstm_init,
            2 if is_ctx else 1)
        x = _mix_ffn([oa.reshape(b * n, HV), om.reshape(b * n, HV)], w_out0, x, mods[0], first_row,
                     groups, g_post, g_pre2, *ffn[0], g_post2, ffn_tm, n, False)

        g_pre, g_post, g_pre2, g_post2 = norms[1]
        q, ckv, kr = _proj_l1(x, g_pre, mods[1], first_row, groups, w_in1, row(l1_g_q_norm), w_q1,
                              row(l1_g_kv_norm), None if is_ctx else tables, tm)
        ckv3 = ckv.reshape(b, n, C_KV_RANK)
        kr3 = kr.reshape(b, n, C_ROPE)
        kv_parts = [(ckv3, kr3)] if is_ctx else [(cache_l1_ckv, cache_l1_krope), (ckv3, kr3)]
        o = _attention(q.reshape(b, n, -1), kv_parts, w_kvb1, 256, 4 if is_ctx else 1)
        x = _mix_ffn([o.reshape(b * n, C_HEADS * C_DV)], w_out1, x, mods[1], first_row,
                     groups, g_post, g_pre2, *ffn[1], g_post2, ffn_tm, n, True)

        results[name] = dict(y=x.reshape(b, n, d), gla=(sa_f, sa_b), c=(c_f, c_b),
                             n=n_rows, m=m_rows, ckv=ckv3, kr=kr3)

    r = results["ctx"]
    n_fwd = r["n"][:, 0, :].reshape(bp, B_HEADS, B_DK)
    n_bwd = r["n"][:, 1, :].reshape(bp, B_HEADS, B_DK)
    m_fwd = r["m"][:, 0:B_HEADS, 0]
    m_bwd = r["m"][:, B_HEADS:2 * B_HEADS, 0]
    return (r["y"], results["lat"]["y"], r["gla"][0], r["gla"][1],
            r["c"][0], n_fwd, m_fwd, r["c"][1], n_bwd, m_bwd, r["ckv"], r["kr"])
```

```python
import functools

import numpy as np
import jax
import jax.numpy as jnp
from jax import lax
from jax.experimental import pallas as pl
from jax.experimental.pallas import tpu as pltpu

F32 = jnp.float32
BF16 = jnp.bfloat16

D_MODEL = 1024
EPS = 1e-6
LOG2E = 1.4426950408889634
N_MOD = 6
GRID_W = 64
ROPE_THETA = 10000.0

A_HEADS, A_DK, A_DV, A_GATE_RANK, A_GATE_TEMP = 4, 64, 128, 16, 16.0
B_HEADS, B_DK, B_DV = 4, 64, 128
C_HEADS, C_Q_RANK, C_KV_RANK, C_NOPE, C_ROPE, C_DV = 8, 384, 256, 128, 64, 128
D_FF = 2816
CONV_W = 3

HEADS = 4
HK = HEADS * 64
HV = HEADS * 128
ZA_W = 1536
ZG_W = 128
GATE_COL = 32
RCHUNK = 128
GLA_SAFE_SPAN = 60.0
L1_IN_PAD = 768
QCAT = 256

V7X_VMEM_BYTES = 64 * 1024 * 1024
VMEM_LIMIT = V7X_VMEM_BYTES - 8 * 1024 * 1024


def _cparams(sem):
    return pltpu.CompilerParams(dimension_semantics=sem, vmem_limit_bytes=VMEM_LIMIT)


def _dot(a, b):
    return jnp.dot(a, b, preferred_element_type=F32)


def _dot_nt(a, b):
    return lax.dot_general(a, b, (((1,), (1,)), ((), ())), preferred_element_type=F32)


def _dot_tn(a, b):
    return lax.dot_general(a, b, (((0,), (0,)), ((), ())), preferred_element_type=F32)


def _split3(x):
    hi = x.astype(BF16)
    r1 = x - hi.astype(F32)
    mid = r1.astype(BF16)
    lo = (r1 - mid.astype(F32)).astype(BF16)
    return hi, mid, lo


def _tri_dot(tri, x):
    hi, mid, lo = _split3(x)
    return _dot(tri, hi) + _dot(tri, mid) + _dot(tri, lo)


def _dot_hl(a, b):
    ah = a.astype(BF16)
    al = (a - ah.astype(F32)).astype(BF16)
    bh = b.astype(BF16)
    bl = (b - bh.astype(F32)).astype(BF16)
    return _dot(ah, bh) + _dot(ah, bl) + _dot(al, bh)


def _rms(x, g):
    return x * lax.rsqrt(jnp.mean(x * x, axis=-1, keepdims=True) + EPS) * g


def _sigmoid(x):
    return 1.0 / (1.0 + jnp.exp(-x))


def _silu(x):
    return x * _sigmoid(x)


def _log_sigmoid(x):
    return jnp.minimum(x, 0.0) - jnp.log(1.0 + jnp.exp(-jnp.abs(x)))


def _tri_masks(n):
    r = lax.broadcasted_iota(jnp.int32, (n, n), 0)
    c = lax.broadcasted_iota(jnp.int32, (n, n), 1)
    return c <= r, c >= r


def _mod_kernel(cond_ref, w_ref, b_ref, o_ref):
    s = _silu(cond_ref[...])
    o_ref[...] = _dot(s.astype(BF16), w_ref[...].astype(BF16)) + b_ref[...]


def _modulation(cond8, w_mod, b_mod):
    d = D_MODEL
    out = pl.pallas_call(
        _mod_kernel,
        out_shape=jax.ShapeDtypeStruct((N_MOD, 8, d), F32),
        grid=(N_MOD,),
        in_specs=[pl.BlockSpec((8, d), lambda j: (0, 0)),
                  pl.BlockSpec((d, d), lambda j: (0, j)),
                  pl.BlockSpec((1, d), lambda j: (0, j))],
        out_specs=pl.BlockSpec((None, 8, d), lambda j: (j, 0, 0)),
        compiler_params=_cparams(("arbitrary",)),
        name="modulation",
    )(cond8, w_mod, b_mod.reshape(1, N_MOD * d))
    return out.reshape(N_MOD, 8, 1, d)


MOD_SHIFT1, MOD_SCALE1, MOD_GATE1, MOD_SHIFT2, MOD_SCALE2, MOD_GATE2 = range(N_MOD)


def _mod_spec(which, first_row, per):
    return pl.BlockSpec((None, None, 1, D_MODEL), lambda i: (which, first_row + i // per, 0, 0))


L0_GROUP_ROWS = ((0, ZA_W), (ZA_W + 32, 2 * ZA_W + 32))
L0_LR_ROWS = (ZA_W, ZA_W + 32)
L0_GATE_ROWS = (2 * ZA_W + 32, 2 * ZA_W + 48)


def _p0_kernel(x_ref, g_ref, sc_ref, sh_ref, wt_ref, za_ref, zb_ref, zg_ref, w_ref):
    @pl.when(pl.program_id(0) == 0)
    def _():
        blk = 256
        for gi, (r0, r1) in enumerate(L0_GROUP_ROWS):
            for j in range((r1 - r0) // blk):
                rows = wt_ref[r0 + j * blk:r0 + (j + 1) * blk, :]
                w_ref[:, gi * ZA_W + j * blk:gi * ZA_W + (j + 1) * blk] = rows.T.astype(BF16)
        tail = jnp.concatenate([wt_ref[L0_LR_ROWS[0]:L0_LR_ROWS[1], :],
                                wt_ref[L0_GATE_ROWS[0]:L0_GATE_ROWS[1], :],
                                jnp.zeros((ZG_W - 48, D_MODEL), F32)], axis=0)
        w_ref[:, 2 * ZA_W:2 * ZA_W + ZG_W] = tail.T.astype(BF16)

    h = _rms(x_ref[...], g_ref[...]) * (1.0 + sc_ref[...]) + sh_ref[...]
    hb = h.astype(BF16)
    za_ref[...] = _dot(hb, w_ref[:, 0:ZA_W])
    zb_ref[...] = _dot(hb, w_ref[:, ZA_W:2 * ZA_W])
    zg_ref[...] = _dot(hb, w_ref[:, 2 * ZA_W:2 * ZA_W + ZG_W])


def _proj_l0(x, g, mod, first_row, groups, w_t, tm):
    t = x.shape[0]
    per = (t // tm) // groups
    row = lambda i: (i, 0)
    fixed = lambda i: (0, 0)
    return pl.pallas_call(
        _p0_kernel,
        out_shape=(jax.ShapeDtypeStruct((t, ZA_W), F32),
                   jax.ShapeDtypeStruct((t, ZA_W), F32),
                   jax.ShapeDtypeStruct((t, ZG_W), F32)),
        grid=(t // tm,),
        in_specs=[pl.BlockSpec((tm, D_MODEL), row),
                  pl.BlockSpec((1, D_MODEL), fixed),
                  _mod_spec(MOD_SCALE1, first_row, per),
                  _mod_spec(MOD_SHIFT1, first_row, per),
                  _resident(w_t.shape)],
        out_specs=(pl.BlockSpec((tm, ZA_W), row),
                   pl.BlockSpec((tm, ZA_W), row),
                   pl.BlockSpec((tm, ZG_W), row)),
        scratch_shapes=[pltpu.VMEM((D_MODEL, 2 * ZA_W + ZG_W), BF16)],
        compiler_params=_cparams(("arbitrary",)),
        name="proj_l0",
    )(x, g, mod, mod, w_t)


def _head_lane_mask(shape, h):
    lane = lax.broadcasted_iota(jnp.int32, shape, len(shape) - 1)
    return (lane >= h * 64) & (lane < (h + 1) * 64)


def _head_row(h):
    return jnp.where(_head_lane_mask((1, HK), h), 1.0, 0.0).astype(BF16)


def _gla_prepare(za_ref, la_ref, r0, rev, tri):
    L = RCHUNK
    rows = pl.ds(r0, L)
    b = _tri_dot(tri, la_ref[rows, :])
    mid = L // 2
    ref = b[mid:mid + 1, :]
    b_end = b[0:1, :] if rev else b[L - 1:L, :]
    q = za_ref[rows, 0:HK]
    k = za_ref[rows, HK:2 * HK]
    v = za_ref[rows, 2 * HK:2 * HK + HV].astype(BF16)
    qe = (q * (jnp.exp(b - ref) * (A_DK ** -0.5))).astype(BF16)
    ke = (k * jnp.exp(ref - b)).astype(BF16)
    qm = jnp.concatenate([qe * _head_row(h) for h in range(HEADS)], axis=0)
    rt = _dot_tn(v, ke)
    upd = None
    for h in range(HEADS):
        blk = rt[h * 128:(h + 1) * 128, :]
        blk = jnp.where(_head_lane_mask(blk.shape, h), blk, 0.0)
        upd = blk if upd is None else upd + blk
    return dict(rows=rows, qm=qm, ke=ke, v=v, upd=upd * jnp.exp(b_end - ref),
                decay=jnp.exp(b_end), eref=jnp.exp(ref))


def _gla_apply(items):
    L = RCHUNK
    sts = [st_ref[...] for _, st_ref, _, _ in items]
    o1s = []
    for (p, _, _, _), st in zip(items, sts):
        rhs = jnp.concatenate([p["ke"], (st * p["eref"]).astype(BF16)], axis=0)
        o1s.append(_dot_nt(p["qm"], rhs))
    for h in range(HEADS):
        for (p, _, o_ref, mask), o1 in zip(items, o1s):
            blk = o1[h * L:(h + 1) * L, :]
            sc = jnp.where(mask, blk[:, 0:L], 0.0).astype(BF16)
            o_ref[p["rows"], h * 128:(h + 1) * 128] = (
                blk[:, L:L + 128] + _dot(sc, p["v"][:, h * 128:(h + 1) * 128]))
    for (p, st_ref, _, _), st in zip(items, sts):
        st_ref[...] = st * p["decay"] + p["upd"]


def _gla_exact_scan(za_ref, la_ref, st_ref, o_ref, b_ref, n, rev, st_init):
    L = RCHUNK
    nc = n // L
    m_lo, m_up = _tri_masks(L)
    tri = jnp.where(m_up if rev else m_lo, 1.0, 0.0).astype(BF16)
    s_idx = lax.broadcasted_iota(jnp.int32, (L, 1), 0)
    krow = lax.broadcasted_iota(jnp.int32, (HK, 128), 0) // 64
    kcol = lax.broadcasted_iota(jnp.int32, (HK, 128), 1)
    head_sum = jnp.where(krow == kcol, 1.0, 0.0)
    hsub = lax.broadcasted_iota(jnp.int32, (8, HK), 0)
    hlane = lax.broadcasted_iota(jnp.int32, (8, HK), 1) // 64
    head_rows = jnp.where(hsub == hlane, 1.0, 0.0)
    st_ref[...] = st_init

    def chunk_body(ci, carry):
        c = nc - 1 - ci if rev else ci
        r0 = pl.multiple_of(c * L, L)
        rows = pl.ds(r0, L)
        b = _tri_dot(tri, la_ref[rows, :])
        b_ref[...] = b
        b_end = b[0:1, :] if rev else b[L - 1:L, :]
        k = za_ref[rows, HK:2 * HK]
        vb = za_ref[rows, 2 * HK:2 * HK + HV].astype(BF16)
        st = st_ref[...]
        stb = st.astype(BF16)

        def row_body(t, carry2):
            b_t = b_ref[pl.ds(t, 1), :]
            q_t = za_ref[pl.ds(r0 + t, 1), 0:HK] * (A_DK ** -0.5)
            valid = (s_idx >= t) if rev else (s_idx <= t)
            w = jnp.exp(jnp.minimum(b_t - b_ref[...], 0.0))
            p = jnp.where(valid, (q_t * k) * w, 0.0)
            sc = jnp.dot(p, head_sum, precision=lax.Precision.HIGHEST,
                         preferred_element_type=F32)
            pv = _dot_tn(sc.astype(BF16), vb)
            qm = (head_rows * (q_t * jnp.exp(b_t))).astype(BF16)
            inter = _dot_nt(qm, stb)
            o_ref[pl.ds(r0 + t, 1), :] = jnp.concatenate(
                [pv[h:h + 1, h * 128:(h + 1) * 128] + inter[h:h + 1, :] for h in range(HEADS)],
                axis=1)
            return carry2

        lax.fori_loop(0, L, row_body, 0)
        kd = (k * jnp.exp(b_end - b)).astype(BF16)
        rt = _dot_tn(vb, kd)
        upd = None
        for h in range(HEADS):
            blk = rt[h * 128:(h + 1) * 128, :]
            blk = jnp.where(_head_lane_mask(blk.shape, h), blk, 0.0)
            upd = blk if upd is None else upd + blk
        st_ref[...] = st * jnp.exp(b_end) + upd
        return carry

    lax.fori_loop(0, nc, chunk_body, 0)


T_M, T_MOUT, T_G, T_SRCW, T_MPREV, T_CARRY, T_USED = 0, 8, 16, 24, 32, 40, 48


def _lane_scan(x, op, fill, is_fwd, pos, n):
    k = 1
    while k < RCHUNK:
        pf = jnp.where(pos >= k, pltpu.roll(x, k, axis=1), fill)
        sf = jnp.where(pos < RCHUNK - k, pltpu.roll(x, n - k, axis=1), fill)
        x = op(x, jnp.where(is_fwd, pf, sf))
        k *= 2
    return x


def _mlstm_gate_tables(g, m0, gt_ref, tab_ref, col_ref, n):
    L = RCHUNK
    nc = n // L
    for c in range(nc):
        gt_ref[:, c * L:(c + 1) * L] = g[c * L:(c + 1) * L, :].T
        tab_ref[c, T_USED:, :] = jnp.zeros((ZG_W - T_USED, L), F32)
    gi = gt_ref[GATE_COL:GATE_COL + 8, :]
    lf = gt_ref[GATE_COL + 8:GATE_COL + 16, :]
    is_fwd = lax.broadcasted_iota(jnp.int32, (8, n), 0) < HEADS
    pos = lax.broadcasted_iota(jnp.int32, (8, n), 1) % L
    b = _lane_scan(lf, jnp.add, 0.0, is_fwd, pos, n)
    gg = gi - b
    cmax = _lane_scan(gg, jnp.maximum, -jnp.inf, is_fwd, pos, n)
    fwd_l = lax.broadcasted_iota(jnp.int32, (8, L), 0) < HEADS
    mp = m0
    for j in range(nc):
        cf, cb = j, nc - 1 - j

        def pick(x):
            return jnp.where(fwd_l, x[:, cf * L:(cf + 1) * L], x[:, cb * L:(cb + 1) * L])

        gj, bj, cj, lj = pick(gg), pick(b), pick(cmax), pick(lf)
        mx = jnp.maximum(mp, jnp.max(gj, axis=1, keepdims=True))
        mj = jnp.maximum(cj, mp)
        blocks = (mj, bj + mj, gj, jnp.exp(gj - mx),
                  jnp.broadcast_to(mp, (8, L)), jnp.broadcast_to(jnp.exp(mp - mx), (8, L)))
        for t, val in enumerate(blocks):
            tab_ref[cf, 8 * t:8 * t + HEADS, :] = val[0:HEADS]
            tab_ref[cb, 8 * t + HEADS:8 * t + 8, :] = val[HEADS:8]
        mp = jnp.sum(lj, axis=1, keepdims=True) + mx
    for c in range(nc):
        col_ref[c * L:(c + 1) * L, :] = tab_ref[c].T
    return mp


def _mlstm_prepare(zb_ref, tab_ref, col_ref, c, d, mask):
    L = RCHUNK
    rows = pl.ds(pl.multiple_of(c * L, L), L)
    qb = zb_ref[rows, 0:HK].astype(BF16)
    kb = (zb_ref[rows, HK:2 * HK] * (B_DK ** -0.5)).astype(BF16)
    v = zb_ref[rows, 2 * HK:2 * HK + HV]
    vb = v.astype(BF16)
    vt = v.T
    qm = jnp.concatenate([qb * _head_row(h) for h in range(HEADS)], axis=0)
    s = _dot_nt(qm, kb)
    ones = jnp.ones((L, 128), BF16)
    pv, w_inter, e_negm = [], [], []
    upd = None
    carry = None
    for h in range(HEADS):
        r = 4 * d + h
        m_rep = jnp.broadcast_to(col_ref[rows, T_M + r:T_M + r + 1], (L, 128))
        mout_rep = jnp.broadcast_to(col_ref[rows, T_MOUT + r:T_MOUT + r + 1], (L, 128))
        g_row = tab_ref[c, T_G + r:T_G + r + 1, :]
        w = jnp.exp(jnp.where(mask, g_row - m_rep, -jnp.inf))
        qk = (s[h * L:(h + 1) * L, :] * w).astype(BF16)
        v1 = jnp.concatenate([vb[:, h * 128:(h + 1) * 128], ones], axis=1)
        pv.append(_dot(qk, v1))
        w_inter.append(jnp.exp(tab_ref[c, T_MPREV + r:T_MPREV + r + 1, :] - m_rep))
        e_negm.append(jnp.exp(-mout_rep))
        sw = tab_ref[c, T_SRCW + r:T_SRCW + r + 1, :]
        lhs = jnp.concatenate([vt[h * 128:(h + 1) * 128, :] * sw, jnp.broadcast_to(sw, (128, L))],
                              axis=0).astype(BF16)
        blk = _dot(lhs, kb)
        blk = jnp.where(_head_lane_mask(blk.shape, h), blk, 0.0)
        upd = blk if upd is None else upd + blk
        cr = tab_ref[c, T_CARRY + r:T_CARRY + r + 1, :]
        cr = jnp.where(_head_lane_mask((1, HK), h), jnp.concatenate([cr, cr], axis=1), 0.0)
        carry = cr if carry is None else carry + cr
    return dict(rows=rows, qm=qm, pv=pv, w_inter=w_inter, e_negm=e_negm, upd=upd, carry=carry)


def _mlstm_apply(items):
    L = RCHUNK
    sts = [st_ref[...] for _, st_ref, _ in items]
    a_all = [_dot_nt(p["qm"], st.astype(BF16)) for (p, _, _), st in zip(items, sts)]
    for h in range(HEADS):
        for (p, _, h_ref), a in zip(items, a_all):
            blk = a[h * L:(h + 1) * L, :]
            pv, w, e = p["pv"][h], p["w_inter"][h], p["e_negm"][h]
            num = w * blk[:, 0:128] + pv[:, 0:128]
            den = w * blk[:, 128:256] + pv[:, 128:256]
            h_ref[p["rows"], h * 128:(h + 1) * 128] = num / jnp.maximum(jnp.abs(den), e)
    for (p, st_ref, _), st in zip(items, sts):
        st_ref[...] = p["carry"] * st + p["upd"]


def _rec_kernel(*refs, n, has_init):
    it = iter(refs)

    def take(k):
        return [next(it) for _ in range(k)]

    za_all, zb_all, zg_all, zgn_all = take(4)
    wgf_ref, wgb_ref, bgf_ref, bgb_ref, gna_ref, bg_ref, gnb_ref = take(7)
    if has_init:
        init_all = take(7)
    out_all = take(8)
    (gt_all,) = take(1)
    set_a_all = take(6)
    set_b_all = take(6)
    work_all = take(9)
    L = RCHUNK
    nc = n // L
    step = pl.program_id(0)
    elems = []
    for e in range(za_all.shape[0]):
        el = dict(za=za_all.at[e], zb=zb_all.at[e], zg=zg_all.at[e], zgn=zgn_all.at[e],
                  gt=gt_all.at[e], set_a=[r.at[e] for r in set_a_all],
                  set_b=[r.at[e] for r in set_b_all])
        (el["oa"], el["sf"], el["sb"], el["om"], el["cf"], el["cb"], el["n_out"],
         el["m_out"]) = [r.at[e] for r in out_all]
        (el["of"], el["ob"], el["hf"], el["hb"], el["gsf"], el["gsb"], el["msf"],
         el["msb"], el["b_exact"]) = [r.at[e] for r in work_all]
        if has_init:
            el["init"] = [r.at[e] for r in init_all]
        elems.append(el)

    def fill(gate_ref, m0, bufs, gt_ref):
        laf_ref, lab_ref, tab_ref, col_ref, mf_ref, span_ref = bufs
        zg = gate_ref[...]
        laf = _log_sigmoid(_dot_hl(zg, wgf_ref[...]) + bgf_ref[...]) / A_GATE_TEMP
        lab = _log_sigmoid(_dot_hl(zg, wgb_ref[...]) + bgb_ref[...]) / A_GATE_TEMP
        laf_ref[...] = laf
        lab_ref[...] = lab
        span = None
        for la in (laf, lab):
            for c in range(nc):
                tot = -jnp.sum(la[c * L:(c + 1) * L, :], axis=0, keepdims=True)
                span = tot if span is None else jnp.maximum(span, tot)
        span_ref[...] = jnp.broadcast_to(jnp.max(span, axis=1, keepdims=True), (8, 128))
        g = zg + bg_ref[...]
        lane = lax.broadcasted_iota(jnp.int32, g.shape, 1)
        is_forget = (lane >= GATE_COL + 8) & (lane < GATE_COL + 16)
        g = jnp.where(is_forget, _log_sigmoid(g), g)
        m_fin = _mlstm_gate_tables(g, m0, gt_ref, tab_ref, col_ref, n)
        mf_ref[...] = jnp.broadcast_to(m_fin, (8, 128))

    for el in elems:
        if has_init:
            sf0_ref, sb0_ref, cf0_ref, cb0_ref, n0_ref, m0_ref, m0n_ref = el["init"]
            el["m0"], el["m0_next"] = m0_ref[:, 0:1], m0n_ref[:, 0:1]
            el["gsf"][...] = sf0_ref[...]
            el["gsb"][...] = sb0_ref[...]
            el["msf"][...] = jnp.concatenate(
                [cf0_ref[...], jnp.broadcast_to(n0_ref[0:1, :], (128, HK))], axis=0)
            el["msb"][...] = jnp.concatenate(
                [cb0_ref[...], jnp.broadcast_to(n0_ref[1:2, :], (128, HK))], axis=0)
        else:
            el["m0"] = el["m0_next"] = jnp.zeros((8, 1), F32)
            for name in ("gsf", "gsb", "msf", "msb"):
                el[name][...] = jnp.zeros(el[name].shape, F32)

    @pl.when(step == 0)
    def _():
        for el in elems:
            fill(el["zg"], el["m0"], el["set_a"], el["gt"])

    m_lo, m_up = _tri_masks(L)
    tri_lo = jnp.where(m_lo, 1.0, 0.0).astype(BF16)
    tri_up = jnp.where(m_up, 1.0, 0.0).astype(BF16)
    gna = gna_ref[...]
    gnb = gnb_ref[...]

    def out_body(c, carry):
        rows = pl.ds(pl.multiple_of(c * L, L), L)
        for h in range(HEADS):
            cols = slice(h * 128, (h + 1) * 128)
            gcols = slice(2 * HK + HV + h * 128, 2 * HK + HV + (h + 1) * 128)
            for el in elems:
                o = el["of"][rows, cols] + el["ob"][rows, cols]
                el["oa"][rows, cols] = (
                    _rms(o, gna) * _silu(el["za"][rows, gcols])).astype(BF16)
                hh = el["hf"][rows, cols] + el["hb"][rows, cols]
                el["om"][rows, cols] = (
                    _sigmoid(el["zb"][rows, gcols]) * _rms(hh, gnb)).astype(BF16)
        return carry

    def run(cur, nxt):
        worst = None
        for el in elems:
            span = el[cur][5][0, 0]
            worst = span if worst is None else jnp.maximum(worst, span)
        safe = worst <= GLA_SAFE_SPAN

        @pl.when(safe)
        def _():
            run_variant(cur, nxt, False)

        @pl.when(jnp.logical_not(safe))
        def _():
            run_variant(cur, nxt, True)

    def run_variant(cur, nxt, exact_gla):
        for el in elems:
            fill(el["zgn"], el["m0_next"], el[nxt], el["gt"])

        def scan_body(gidx, carry):
            waves = []
            for w in range(2):
                c = 2 * gidx + w
                cb = nc - 1 - c
                gla_items, mlstm_items = [], []
                for el in elems:
                    laf_ref, lab_ref, tab_ref, col_ref = el[cur][0:4]
                    gla_items += [
                        (_gla_prepare(el["za"], laf_ref, pl.multiple_of(c * L, L), False, tri_lo),
                         el["gsf"], el["of"], m_lo),
                        (_gla_prepare(el["za"], lab_ref, pl.multiple_of(cb * L, L), True, tri_up),
                         el["gsb"], el["ob"], m_up)]
                    mlstm_items += [
                        (_mlstm_prepare(el["zb"], tab_ref, col_ref, c, 0, m_lo), el["msf"], el["hf"]),
                        (_mlstm_prepare(el["zb"], tab_ref, col_ref, cb, 1, m_up), el["msb"], el["hb"])]
                waves.append((gla_items, mlstm_items))
            for gla_items, mlstm_items in waves:
                _gla_apply(gla_items)
                _mlstm_apply(mlstm_items)
            return carry

        if nc == 2:
            scan_body(0, 0)
        else:
            lax.fori_loop(0, nc // 2, scan_body, 0)
        if exact_gla:
            for el in elems:
                laf_ref, lab_ref = el[cur][0:2]
                zero = jnp.zeros((128, HK), F32)
                init_f = el["init"][0][...] if has_init else zero
                init_b = el["init"][1][...] if has_init else zero
                _gla_exact_scan(el["za"], laf_ref, el["gsf"], el["of"], el["b_exact"], n, False, init_f)
                _gla_exact_scan(el["za"], lab_ref, el["gsb"], el["ob"], el["b_exact"], n, True, init_b)

        if nc == 2:
            for c in range(nc):
                out_body(c, 0)
        else:
            lax.fori_loop(0, nc, out_body, 0)
        for el in elems:
            el["m_out"][...] = el[cur][4][...]

    @pl.when(step % 2 == 0)
    def _():
        run("set_a", "set_b")

    @pl.when(step % 2 == 1)
    def _():
        run("set_b", "set_a")

    for el in elems:
        gsf_t = el["gsf"][...].T
        gsb_t = el["gsb"][...].T
        msf = el["msf"][...]
        msb = el["msb"][...]
        ctf_t = msf[0:128, :].T
        ctb_t = msb[0:128, :].T
        for h in range(HEADS):
            rows = slice(h * 64, (h + 1) * 64)
            el["sf"][h] = gsf_t[rows, :]
            el["sb"][h] = gsb_t[rows, :]
            el["cf"][h] = ctf_t[rows, :]
            el["cb"][h] = ctb_t[rows, :]
        el["n_out"][...] = jnp.concatenate(
            [msf[128:129, :], msb[128:129, :], jnp.zeros((6, HK), F32)], axis=0)


def _recurrent(za, zb, zg, wgf, wgb, bgf, bgb, gna, bg, gnb, init, bt):
    bsz, n, _ = za.shape
    has_init = init is not None
    nc = n // RCHUNK
    steps = bsz // bt
    per_b3 = lambda b: (b, 0, 0)
    next_b3 = lambda b: (jnp.minimum(b + 1, steps - 1), 0, 0)
    per_b4 = lambda b: (b, 0, 0, 0)
    fixed = lambda b: (0, 0)
    in_specs = [pl.BlockSpec((bt, n, ZA_W), per_b3),
                pl.BlockSpec((bt, n, ZA_W), per_b3),
                pl.BlockSpec((bt, n, ZG_W), per_b3),
                pl.BlockSpec((bt, n, ZG_W), next_b3),
                pl.BlockSpec((ZG_W, HK), fixed),
                pl.BlockSpec((ZG_W, HK), fixed),
                pl.BlockSpec((1, HK), fixed),
                pl.BlockSpec((1, HK), fixed),
                pl.BlockSpec((1, 128), fixed),
                pl.BlockSpec((1, ZG_W), fixed),
                pl.BlockSpec((1, 128), fixed)]
    args = [za, zb, zg, zg, wgf, wgb, bgf, bgb, gna, bg, gnb]
    if has_init:
        sf0, sb0, cf0, cb0, n0, m0 = init
        in_specs += [pl.BlockSpec((bt, 128, HK), per_b3)] * 4
        in_specs += [pl.BlockSpec((bt, 8, HK), per_b3), pl.BlockSpec((bt, 8, 128), per_b3),
                     pl.BlockSpec((bt, 8, 128), next_b3)]
        args += [sf0, sb0, cf0, cb0, n0, m0, m0]
    state = jax.ShapeDtypeStruct((bsz, HEADS, 64, 128), F32)
    state_spec = pl.BlockSpec((bt, HEADS, 64, 128), per_b4)

    def vmem(*shape):
        return pltpu.VMEM((bt,) + shape, F32)

    gate_set = [vmem(n, HK), vmem(n, HK), vmem(nc, ZG_W, RCHUNK), vmem(n, ZG_W), vmem(8, 128),
                vmem(8, 128)]
    return pl.pallas_call(
        functools.partial(_rec_kernel, n=n, has_init=has_init),
        out_shape=(jax.ShapeDtypeStruct((bsz, n, HV), BF16), state, state,
                   jax.ShapeDtypeStruct((bsz, n, HV), BF16), state, state,
                   jax.ShapeDtypeStruct((bsz, 8, HK), F32),
                   jax.ShapeDtypeStruct((bsz, 8, 128), F32)),
        grid=(steps,),
        in_specs=in_specs,
        out_specs=(pl.BlockSpec((bt, n, HV), per_b3), state_spec, state_spec,
                   pl.BlockSpec((bt, n, HV), per_b3), state_spec, state_spec,
                   pl.BlockSpec((bt, 8, HK), per_b3),
                   pl.BlockSpec((bt, 8, 128), per_b3)),
        scratch_shapes=([vmem(ZG_W, n)] + gate_set + gate_set + [vmem(n, HV)] * 4
                        + [vmem(128, HK)] * 2 + [vmem(2 * 128, HK)] * 2 + [vmem(RCHUNK, HK)]),
        compiler_params=_cparams(("arbitrary",)),
        name="recurrent",
    )(*args)


def _mix_residual(parts, wo_ref, x_ref, gt1_ref, gpost1_ref, gpre2_ref, sc2_ref, sh2_ref):
    m = None
    off = 0
    for p in parts:
        kp = p.shape[1]
        t = _dot(p[...], wo_ref[off:off + kp, :])
        m = t if m is None else m + t
        off += kp
    x1 = x_ref[...] + gt1_ref[...] * _rms(m, gpost1_ref[...])
    hb = (_rms(x1, gpre2_ref[...]) * (1.0 + sc2_ref[...]) + sh2_ref[...]).astype(BF16)
    return x1, hb


def _ffn_gated_up(hb, wu_ref, cw_ref, cb_ref, act_ref, seq, tf):
    tm = hb.shape[0]
    pos = lax.broadcasted_iota(jnp.int32, (tm, 1), 0) % seq
    first = pos == 0
    last = pos == seq - 1

    def conv(u, cols):
        prev = jnp.where(first, 0.0, pltpu.roll(u, 1, axis=0))
        nxt = jnp.where(last, 0.0, pltpu.roll(u, tm - 1, axis=0))
        return (cw_ref[0:1, cols] * prev + cw_ref[1:2, cols] * u + cw_ref[2:3, cols] * nxt
                + cb_ref[:, cols])

    for j in range(D_FF // tf):
        ca = slice(j * tf, (j + 1) * tf)
        cg = slice(D_FF + j * tf, D_FF + (j + 1) * tf)
        a = conv(_dot(hb, wu_ref[:, ca]), ca)
        g = conv(_dot(hb, wu_ref[:, cg]), cg)
        act_ref[:, ca] = (_silu(g) * a).astype(BF16)


def _mix_ffn_kernel(*refs, n_parts, seq, tf):
    parts = refs[:n_parts]
    (wo_ref, x_ref, gt1_ref, gpost1_ref, gpre2_ref, sc2_ref, sh2_ref,
     wu_ref, cw_ref, cb_ref, wd_ref, gt2_ref, gpost2_ref, o_ref, act_ref) = refs[n_parts:]
    x1, hb = _mix_residual(parts, wo_ref, x_ref, gt1_ref, gpost1_ref, gpre2_ref, sc2_ref, sh2_ref)
    o_ref[...] = x1
    _ffn_gated_up(hb, wu_ref, cw_ref, cb_ref, act_ref, seq, tf)
    y = _dot(act_ref[...], wd_ref[...])
    o_ref[...] = o_ref[...] + gt2_ref[...] * _rms(y, gpost2_ref[...])


def _resident(shape):
    return pl.BlockSpec(shape, lambda *_: (0,) * len(shape), pipeline_mode=pl.Buffered(1))


def _mix_ffn(parts, w_out, x, mod, first_row, groups, gpost1, gpre2, w_up, conv_w, conv_b, w_down,
             gpost2, tm, seq, tf=256):
    t = x.shape[0]
    per = (t // tm) // groups
    row = lambda i: (i, 0)
    vec = _resident((1, D_MODEL))
    in_specs = [pl.BlockSpec((tm, p.shape[1]), row) for p in parts]
    in_specs += [_resident(w_out.shape), pl.BlockSpec((tm, D_MODEL), row),
                 _mod_spec(MOD_GATE1, first_row, per), vec, vec,
                 _mod_spec(MOD_SCALE2, first_row, per), _mod_spec(MOD_SHIFT2, first_row, per),
                 _resident(w_up.shape), _resident(conv_w.shape), _resident(conv_b.shape),
                 _resident(w_down.shape), _mod_spec(MOD_GATE2, first_row, per), vec]
    return pl.pallas_call(
        functools.partial(_mix_ffn_kernel, n_parts=len(parts), seq=seq, tf=tf),
        out_shape=jax.ShapeDtypeStruct((t, D_MODEL), F32),
        grid=(t // tm,),
        in_specs=in_specs,
        out_specs=pl.BlockSpec((tm, D_MODEL), row),
        scratch_shapes=[pltpu.VMEM((tm, D_FF), BF16)],
        compiler_params=_cparams(("arbitrary",)),
        name="mix_ffn",
    )(*parts, w_out, x, mod, gpost1, gpre2, mod, mod, w_up, conv_w, conv_b, w_down, mod, gpost2)


def _rope(x, cos, sin_signed):
    lane = lax.broadcasted_iota(jnp.int32, x.shape, 1)
    lower = (lane % 32) < 16
    partner = jnp.where(lower, pltpu.roll(x, 128 - 16, axis=1), pltpu.roll(x, 16, axis=1))
    return x * cos + partner * sin_signed


def _p1_kernel(*refs, rope):
    if rope:
        (x_ref, g_ref, sc_ref, sh_ref, w_ref, gq_ref, wq_ref, gkv_ref, cos_ref, sin_ref,
         q_ref, ckv_ref, kr_ref) = refs
    else:
        (x_ref, g_ref, sc_ref, sh_ref, w_ref, gq_ref, wq_ref, gkv_ref,
         q_ref, ckv_ref, kr_ref) = refs
    h = _rms(x_ref[...], g_ref[...]) * (1.0 + sc_ref[...]) + sh_ref[...]
    z = _dot(h.astype(BF16), w_ref[...])
    qa = _rms(z[:, 0:C_Q_RANK], gq_ref[...])
    q = _dot(qa.astype(BF16), wq_ref[...])
    ckv_ref[...] = _rms(z[:, C_Q_RANK:C_Q_RANK + C_KV_RANK], gkv_ref[...])
    kr = z[:, C_Q_RANK + C_KV_RANK:L1_IN_PAD]
    if rope:
        cos = cos_ref[...]
        sin = sin_ref[...]
        kr = _rope(kr, cos, sin)
    kr_ref[...] = kr[:, 0:C_ROPE]
    for hd in range(C_HEADS):
        c0 = hd * QCAT
        q_ref[:, c0:c0 + 128] = q[:, c0:c0 + 128].astype(q_ref.dtype)
        qr = q[:, c0 + 128:c0 + 256]
        if rope:
            qr = _rope(qr, cos, sin)
        q_ref[:, c0 + 128:c0 + 256] = qr.astype(q_ref.dtype)


def _proj_l1(x, g, mod, first_row, groups, w, gq, wq, gkv, tables, tm):
    t = x.shape[0]
    per = (t // tm) // groups
    row = lambda i: (i, 0)
    fixed = lambda i: (0, 0)
    in_specs = [pl.BlockSpec((tm, D_MODEL), row),
                pl.BlockSpec((1, D_MODEL), fixed),
                _mod_spec(MOD_SCALE1, first_row, per),
                _mod_spec(MOD_SHIFT1, first_row, per),
                pl.BlockSpec(w.shape, fixed),
                pl.BlockSpec((1, C_Q_RANK), fixed),
                pl.BlockSpec(wq.shape, fixed),
                pl.BlockSpec((1, C_KV_RANK), fixed)]
    args = [x, g, mod, mod, w, gq, wq, gkv]
    rope = tables is not None
    if rope:
        n = tables[0].shape[0]
        pos = lambda i: (i % (n // tm), 0)
        in_specs += [pl.BlockSpec((tm, 128), pos)] * 2
        args += list(tables)
    return pl.pallas_call(
        functools.partial(_p1_kernel, rope=rope),
        out_shape=(jax.ShapeDtypeStruct((t, C_HEADS * QCAT), BF16),
                   jax.ShapeDtypeStruct((t, C_KV_RANK), F32),
                   jax.ShapeDtypeStruct((t, C_ROPE), F32)),
        grid=(t // tm,),
        in_specs=in_specs,
        out_specs=(pl.BlockSpec((tm, C_HEADS * QCAT), row),
                   pl.BlockSpec((tm, C_KV_RANK), row),
                   pl.BlockSpec((tm, C_ROPE), row)),
        compiler_params=_cparams(("arbitrary",)),
        name="proj_l1",
    )(*args)


def _attn_kernel(*refs, n_parts, scale):
    q_ref = refs[0]
    kv_parts = refs[1:1 + 2 * n_parts]
    w_ref, o_ref, k_sc, v_sc = refs[1 + 2 * n_parts:]
    bt = q_ref.shape[0]

    @pl.when(pl.program_id(1) == 0)
    def _():
        off = 0
        for p in range(n_parts):
            ckv_ref, kr_ref = kv_parts[2 * p], kv_parts[2 * p + 1]
            nk = ckv_ref.shape[1]
            ones = jnp.ones((nk, C_DV), BF16)
            for e in range(bt):
                kv = _dot(ckv_ref[e].astype(BF16), w_ref[...])
                kr = kr_ref[e].astype(BF16)
                kr2 = jnp.concatenate([kr, kr], axis=1)
                for hd in range(C_HEADS):
                    v0 = C_HEADS * C_NOPE + hd * C_DV
                    k_sc[e, hd, off:off + nk, 0:128] = kv[:, hd * 128:(hd + 1) * 128].astype(BF16)
                    k_sc[e, hd, off:off + nk, 128:256] = kr2
                    v_sc[e, hd, off:off + nk, 0:C_DV] = kv[:, v0:v0 + C_DV].astype(BF16)
                    v_sc[e, hd, off:off + nk, C_DV:2 * C_DV] = ones
            off += nk

    for hd in range(C_HEADS):
        for e in range(bt):
            t = _dot_nt(q_ref[e, :, hd * QCAT:(hd + 1) * QCAT], k_sc[e, hd]) * (scale * LOG2E)
            p = jnp.exp2(t - jnp.max(t, axis=1, keepdims=True))
            ov = _dot(p.astype(BF16), v_sc[e, hd])
            o_ref[e, :, hd * C_DV:(hd + 1) * C_DV] = (
                ov[:, 0:C_DV] / ov[:, C_DV:2 * C_DV]).astype(o_ref.dtype)


def _attention(q, kv_parts, w_kvb, tq, bt):
    bsz, n, _ = q.shape
    nk = sum(p[0].shape[1] for p in kv_parts)
    in_specs = [pl.BlockSpec((bt, tq, C_HEADS * QCAT), lambda b, i: (b, i, 0))]
    args = [q]
    for ckv, kr in kv_parts:
        in_specs.append(pl.BlockSpec((bt, ckv.shape[1], C_KV_RANK), lambda b, i: (b, 0, 0)))
        in_specs.append(pl.BlockSpec((bt, kr.shape[1], C_ROPE), lambda b, i: (b, 0, 0)))
        args += [ckv, kr]
    in_specs.append(pl.BlockSpec(w_kvb.shape, lambda b, i: (0, 0)))
    args.append(w_kvb)
    return pl.pallas_call(
        functools.partial(_attn_kernel, n_parts=len(kv_parts),
                          scale=(C_NOPE + C_ROPE) ** -0.5),
        out_shape=jax.ShapeDtypeStruct((bsz, n, C_HEADS * C_DV), BF16),
        grid=(bsz // bt, n // tq),
        in_specs=in_specs,
        out_specs=pl.BlockSpec((bt, tq, C_HEADS * C_DV), lambda b, i: (b, i, 0)),
        scratch_shapes=[pltpu.VMEM((bt, C_HEADS, nk, QCAT), BF16),
                        pltpu.VMEM((bt, C_HEADS, nk, 2 * C_DV), BF16)],
        compiler_params=_cparams(("arbitrary", "arbitrary")),
        name="mla_attention",
    )(*args)


def _l1_wq_layout(w_qb):
    blocks = []
    zero = jnp.zeros((w_qb.shape[0], C_ROPE), w_qb.dtype)
    for hd in range(C_HEADS):
        base = hd * (C_NOPE + C_ROPE)
        nope = w_qb[:, base:base + C_NOPE]
        rope = w_qb[:, base + C_NOPE:base + C_NOPE + C_ROPE]
        blocks += [nope, rope, zero] if hd % 2 == 0 else [nope, zero, rope]
    return jnp.concatenate(blocks, axis=1).astype(BF16)


def _l1_wkvb_layout(w_kvb):
    w = w_kvb.reshape(C_KV_RANK, C_HEADS, C_NOPE + C_DV)
    k = w[:, :, :C_NOPE].reshape(C_KV_RANK, C_HEADS * C_NOPE)
    v = w[:, :, C_NOPE:].reshape(C_KV_RANK, C_HEADS * C_DV)
    return jnp.concatenate([k, v], axis=1).astype(BF16)


def _rope_tables(n):
    pos = np.arange(n)
    row = (pos // GRID_W).astype(np.float64)
    col = (pos % GRID_W).astype(np.float64)
    half = C_ROPE // 2
    inv = 1.0 / (ROPE_THETA ** (np.arange(0, half, 2, dtype=np.float64) / half))
    ang_r = row[:, None] * inv[None, :]
    ang_c = col[:, None] * inv[None, :]
    cos64 = np.concatenate([np.cos(ang_r)] * 2 + [np.cos(ang_c)] * 2, axis=1)
    sin64 = np.concatenate([-np.sin(ang_r), np.sin(ang_r), -np.sin(ang_c), np.sin(ang_c)], axis=1)
    return (jnp.asarray(np.concatenate([cos64] * 2, axis=1), F32),
            jnp.asarray(np.concatenate([sin64] * 2, axis=1), F32))


def _gate_weight(w, first_row):
    return jnp.pad(w, ((first_row, ZG_W - first_row - A_GATE_RANK), (0, 0)))


def _state_t(s):
    b = s.shape[0]
    return s.transpose(0, 3, 1, 2).reshape(b, 128, HK)


def _n_rows(n_fwd, n_bwd):
    b = n_fwd.shape[0]
    rows = jnp.zeros((b, 8, HK), F32)
    return rows.at[:, 0, :].set(n_fwd.reshape(b, HK)).at[:, 1, :].set(n_bwd.reshape(b, HK))


def _m_rows(m_fwd, m_bwd):
    m = jnp.concatenate([m_fwd, m_bwd], axis=1)
    return jnp.broadcast_to(m[:, :, None], (m.shape[0], 8, 128))


def kernel(x_prompt, x_sample, state_l0_gla_fwd, state_l0_gla_bwd, state_l0_mlstm_c_fwd, state_l0_mlstm_n_fwd, state_l0_mlstm_m_fwd, state_l0_mlstm_c_bwd, state_l0_mlstm_n_bwd, state_l0_mlstm_m_bwd, cache_l1_ckv, cache_l1_krope, c, c_ctx, l0_w_mod, l0_b_mod, l0_g_pre_mix, l0_g_post_mix, l0_g_pre_ffn, l0_g_post_ffn, l0_w_in, l0_gla_w_gate_f, l0_gla_b_gate_f, l0_gla_w_gate_b, l0_gla_b_gate_b, l0_gla_g_norm, l0_mlstm_b_gates, l0_mlstm_g_norm, l0_w_out, l0_ffn_w_up, l0_ffn_conv_w, l0_ffn_conv_b, l0_ffn_w_down, l1_w_mod, l1_b_mod, l1_g_pre_mix, l1_g_post_mix, l1_g_pre_ffn, l1_g_post_ffn, l1_w_in, l1_g_q_norm, l1_w_qb, l1_g_kv_norm, l1_w_kvb, l1_w_out, l1_ffn_w_up, l1_ffn_conv_w, l1_ffn_conv_b, l1_ffn_w_down):
    bp, sp, d = x_prompt.shape
    bs, ss, _ = x_sample.shape
    row = lambda v: v.reshape(1, -1)

    cond8 = jnp.concatenate([c_ctx[None, :], c, jnp.zeros((8 - 1 - bs, d), F32)], axis=0)
    mods = (_modulation(cond8, l0_w_mod, l0_b_mod), _modulation(cond8, l1_w_mod, l1_b_mod))

    paths = {
        "ctx": dict(x=x_prompt.reshape(bp * sp, d), b=bp, n=sp, first_row=0, groups=1, tm=512),
        "lat": dict(x=x_sample.reshape(bs * ss, d), b=bs, n=ss, first_row=1, groups=bs, tm=512),
    }

    w_in0 = l0_w_in.T
    wgf = _gate_weight(l0_gla_w_gate_f, 0)
    wgb = _gate_weight(l0_gla_w_gate_b, A_GATE_RANK)
    bg = jnp.pad(l0_mlstm_b_gates, (GATE_COL, ZG_W - GATE_COL - 16)).reshape(1, ZG_W)
    w_out0 = l0_w_out.astype(BF16)
    w_in1 = jnp.pad(l1_w_in, ((0, 0), (0, L1_IN_PAD - l1_w_in.shape[1]))).astype(BF16)
    w_q1 = _l1_wq_layout(l1_w_qb)
    w_kvb1 = _l1_wkvb_layout(l1_w_kvb)
    w_out1 = l1_w_out.astype(BF16)
    ffn = ((l0_ffn_w_up.astype(BF16), l0_ffn_conv_w, row(l0_ffn_conv_b), l0_ffn_w_down.astype(BF16)),
           (l1_ffn_w_up.astype(BF16), l1_ffn_conv_w, row(l1_ffn_conv_b), l1_ffn_w_down.astype(BF16)))
    norms = ((row(l0_g_pre_mix), row(l0_g_post_mix), row(l0_g_pre_ffn), row(l0_g_post_ffn)),
             (row(l1_g_pre_mix), row(l1_g_post_mix), row(l1_g_pre_ffn), row(l1_g_post_ffn)))

    gla_init = (_state_t(state_l0_gla_fwd), _state_t(state_l0_gla_bwd))
    mlstm_init = (_state_t(state_l0_mlstm_c_fwd), _state_t(state_l0_mlstm_c_bwd),
                  _n_rows(state_l0_mlstm_n_fwd, state_l0_mlstm_n_bwd),
                  _m_rows(state_l0_mlstm_m_fwd, state_l0_mlstm_m_bwd))
    tables = _rope_tables(ss)

    results = {}
    for name, p in paths.items():
        x, b, n, tm = p["x"], p["b"], p["n"], p["tm"]
        first_row, groups = p["first_row"], p["groups"]
        is_ctx = name == "ctx"
        ffn_tm = tm if is_ctx else n
        g_pre, g_post, g_pre2, g_post2 = norms[0]

        za, zb, zg = _proj_l0(x, g_pre, mods[0], first_row, groups, w_in0, tm)
        za, zb, zg = (t.reshape(b, n, -1) for t in (za, zb, zg))
        oa, sa_f, sa_b, om, c_f, c_b, n_rows, m_rows = _recurrent(
            za, zb, zg, wgf, wgb, row(l0_gla_b_gate_f), row(l0_gla_b_gate_b), row(l0_gla_g_norm),
            bg, row(l0_mlstm_g_norm), None if is_ctx else gla_init + mlstm_init,
            2 if is_ctx else 1)
        x = _mix_ffn([oa.reshape(b * n, HV), om.reshape(b * n, HV)], w_out0, x, mods[0], first_row,
                     groups, g_post, g_pre2, *ffn[0], g_post2, ffn_tm, n)

        g_pre, g_post, g_pre2, g_post2 = norms[1]
        q, ckv, kr = _proj_l1(x, g_pre, mods[1], first_row, groups, w_in1, row(l1_g_q_norm), w_q1,
                              row(l1_g_kv_norm), None if is_ctx else tables, tm)
        ckv3 = ckv.reshape(b, n, C_KV_RANK)
        kr3 = kr.reshape(b, n, C_ROPE)
        kv_parts = [(ckv3, kr3)] if is_ctx else [(cache_l1_ckv, cache_l1_krope), (ckv3, kr3)]
        o = _attention(q.reshape(b, n, -1), kv_parts, w_kvb1, 256, 4 if is_ctx else 1)
        x = _mix_ffn([o.reshape(b * n, C_HEADS * C_DV)], w_out1, x, mods[1], first_row,
                     groups, g_post, g_pre2, *ffn[1], g_post2, ffn_tm, n)

        results[name] = dict(y=x.reshape(b, n, d), gla=(sa_f, sa_b), c=(c_f, c_b),
                             n=n_rows, m=m_rows, ckv=ckv3, kr=kr3)

    r = results["ctx"]
    n_fwd = r["n"][:, 0, :].reshape(bp, B_HEADS, B_DK)
    n_bwd = r["n"][:, 1, :].reshape(bp, B_HEADS, B_DK)
    m_fwd = r["m"][:, 0:B_HEADS, 0]
    m_bwd = r["m"][:, B_HEADS:2 * B_HEADS, 0]
    return (r["y"], results["lat"]["y"], r["gla"][0], r["gla"][1],
            r["c"][0], n_fwd, m_fwd, r["c"][1], n_bwd, m_bwd, r["ckv"], r["kr"])
```

```python
import functools

import numpy as np
import jax
import jax.numpy as jnp
from jax import lax
from jax.experimental import pallas as pl
from jax.experimental.pallas import tpu as pltpu

F32 = jnp.float32
BF16 = jnp.bfloat16

D_MODEL = 1024
EPS = 1e-6
LOG2E = 1.4426950408889634
N_MOD = 6
GRID_W = 64
ROPE_THETA = 10000.0

A_HEADS, A_DK, A_DV, A_GATE_RANK, A_GATE_TEMP = 4, 64, 128, 16, 16.0
B_HEADS, B_DK, B_DV = 4, 64, 128
C_HEADS, C_Q_RANK, C_KV_RANK, C_NOPE, C_ROPE, C_DV = 8, 384, 256, 128, 64, 128
D_FF = 2816
CONV_W = 3

HEADS = 4
HK = HEADS * 64
HV = HEADS * 128
ZA_W = 1536
ZG_W = 128
GATE_COL = 32
RCHUNK = 128
GLA_SAFE_SPAN = 60.0
L1_IN_PAD = 768
QCAT = 256

V7X_VMEM_BYTES = 64 * 1024 * 1024
VMEM_LIMIT = V7X_VMEM_BYTES - 8 * 1024 * 1024


def _cparams(sem):
    return pltpu.CompilerParams(dimension_semantics=sem, vmem_limit_bytes=VMEM_LIMIT)


def _dot(a, b):
    return jnp.dot(a, b, preferred_element_type=F32)


def _dot_nt(a, b):
    return lax.dot_general(a, b, (((1,), (1,)), ((), ())), preferred_element_type=F32)


def _dot_tn(a, b):
    return lax.dot_general(a, b, (((0,), (0,)), ((), ())), preferred_element_type=F32)


def _split3(x):
    hi = x.astype(BF16)
    r1 = x - hi.astype(F32)
    mid = r1.astype(BF16)
    lo = (r1 - mid.astype(F32)).astype(BF16)
    return hi, mid, lo


def _tri_dot(tri, x):
    hi, mid, lo = _split3(x)
    return _dot(tri, hi) + _dot(tri, mid) + _dot(tri, lo)


def _dot_hl(a, b):
    ah = a.astype(BF16)
    al = (a - ah.astype(F32)).astype(BF16)
    bh = b.astype(BF16)
    bl = (b - bh.astype(F32)).astype(BF16)
    return _dot(ah, bh) + _dot(ah, bl) + _dot(al, bh)


def _rms(x, g):
    return x * lax.rsqrt(jnp.mean(x * x, axis=-1, keepdims=True) + EPS) * g


def _sigmoid(x):
    return 1.0 / (1.0 + jnp.exp(-x))


def _silu(x):
    return x * _sigmoid(x)


def _log_sigmoid(x):
    return jnp.minimum(x, 0.0) - jnp.log(1.0 + jnp.exp(-jnp.abs(x)))


def _tri_masks(n):
    r = lax.broadcasted_iota(jnp.int32, (n, n), 0)
    c = lax.broadcasted_iota(jnp.int32, (n, n), 1)
    return c <= r, c >= r


def _mod_kernel(cond_ref, w_ref, b_ref, o_ref):
    s = _silu(cond_ref[...])
    o_ref[...] = _dot(s.astype(BF16), w_ref[...].astype(BF16)) + b_ref[...]


def _modulation(cond8, w_mod, b_mod):
    d = D_MODEL
    out = pl.pallas_call(
        _mod_kernel,
        out_shape=jax.ShapeDtypeStruct((N_MOD, 8, d), F32),
        grid=(N_MOD,),
        in_specs=[pl.BlockSpec((8, d), lambda j: (0, 0)),
                  pl.BlockSpec((d, d), lambda j: (0, j)),
                  pl.BlockSpec((1, d), lambda j: (0, j))],
        out_specs=pl.BlockSpec((None, 8, d), lambda j: (j, 0, 0)),
        compiler_params=_cparams(("arbitrary",)),
        name="modulation",
    )(cond8, w_mod, b_mod.reshape(1, N_MOD * d))
    return out.reshape(N_MOD, 8, 1, d)


MOD_SHIFT1, MOD_SCALE1, MOD_GATE1, MOD_SHIFT2, MOD_SCALE2, MOD_GATE2 = range(N_MOD)


def _mod_spec(which, first_row, per):
    return pl.BlockSpec((None, None, 1, D_MODEL), lambda i: (which, first_row + i // per, 0, 0))


L0_GROUP_ROWS = ((0, ZA_W), (ZA_W + 32, 2 * ZA_W + 32))
L0_LR_ROWS = (ZA_W, ZA_W + 32)
L0_GATE_ROWS = (2 * ZA_W + 32, 2 * ZA_W + 48)


def _p0_kernel(x_ref, g_ref, sc_ref, sh_ref, wt_ref, za_ref, zb_ref, zg_ref, w_ref):
    @pl.when(pl.program_id(0) == 0)
    def _():
        blk = 256
        for gi, (r0, r1) in enumerate(L0_GROUP_ROWS):
            for j in range((r1 - r0) // blk):
                rows = wt_ref[r0 + j * blk:r0 + (j + 1) * blk, :]
                w_ref[:, gi * ZA_W + j * blk:gi * ZA_W + (j + 1) * blk] = rows.T.astype(BF16)
        tail = jnp.concatenate([wt_ref[L0_LR_ROWS[0]:L0_LR_ROWS[1], :],
                                wt_ref[L0_GATE_ROWS[0]:L0_GATE_ROWS[1], :],
                                jnp.zeros((ZG_W - 48, D_MODEL), F32)], axis=0)
        w_ref[:, 2 * ZA_W:2 * ZA_W + ZG_W] = tail.T.astype(BF16)

    h = _rms(x_ref[...], g_ref[...]) * (1.0 + sc_ref[...]) + sh_ref[...]
    hb = h.astype(BF16)
    za_ref[...] = _dot(hb, w_ref[:, 0:ZA_W])
    zb_ref[...] = _dot(hb, w_ref[:, ZA_W:2 * ZA_W])
    zg_ref[...] = _dot(hb, w_ref[:, 2 * ZA_W:2 * ZA_W + ZG_W])


def _proj_l0(x, g, mod, first_row, groups, w_t, tm):
    t = x.shape[0]
    per = (t // tm) // groups
    row = lambda i: (i, 0)
    fixed = lambda i: (0, 0)
    return pl.pallas_call(
        _p0_kernel,
        out_shape=(jax.ShapeDtypeStruct((t, ZA_W), F32),
                   jax.ShapeDtypeStruct((t, ZA_W), F32),
                   jax.ShapeDtypeStruct((t, ZG_W), F32)),
        grid=(t // tm,),
        in_specs=[pl.BlockSpec((tm, D_MODEL), row),
                  pl.BlockSpec((1, D_MODEL), fixed),
                  _mod_spec(MOD_SCALE1, first_row, per),
                  _mod_spec(MOD_SHIFT1, first_row, per),
                  _resident(w_t.shape)],
        out_specs=(pl.BlockSpec((tm, ZA_W), row),
                   pl.BlockSpec((tm, ZA_W), row),
                   pl.BlockSpec((tm, ZG_W), row)),
        scratch_shapes=[pltpu.VMEM((D_MODEL, 2 * ZA_W + ZG_W), BF16)],
        compiler_params=_cparams(("arbitrary",)),
        name="proj_l0",
    )(x, g, mod, mod, w_t)


def _head_lane_mask(shape, h):
    lane = lax.broadcasted_iota(jnp.int32, shape, len(shape) - 1)
    return (lane >= h * 64) & (lane < (h + 1) * 64)


def _head_row(h):
    return jnp.where(_head_lane_mask((1, HK), h), 1.0, 0.0).astype(BF16)


def _gla_prepare(za_ref, la_ref, r0, rev, tri):
    L = RCHUNK
    rows = pl.ds(r0, L)
    b = _tri_dot(tri, la_ref[rows, :])
    mid = L // 2
    ref = b[mid:mid + 1, :]
    b_end = b[0:1, :] if rev else b[L - 1:L, :]
    q = za_ref[rows, 0:HK]
    k = za_ref[rows, HK:2 * HK]
    v = za_ref[rows, 2 * HK:2 * HK + HV].astype(BF16)
    qe = (q * (jnp.exp(b - ref) * (A_DK ** -0.5))).astype(BF16)
    ke = (k * jnp.exp(ref - b)).astype(BF16)
    qm = jnp.concatenate([qe * _head_row(h) for h in range(HEADS)], axis=0)
    rt = _dot_tn(v, ke)
    upd = None
    for h in range(HEADS):
        blk = rt[h * 128:(h + 1) * 128, :]
        blk = jnp.where(_head_lane_mask(blk.shape, h), blk, 0.0)
        upd = blk if upd is None else upd + blk
    return dict(rows=rows, qm=qm, ke=ke, v=v, upd=upd * jnp.exp(b_end - ref),
                decay=jnp.exp(b_end), eref=jnp.exp(ref))


def _gla_apply(items):
    L = RCHUNK
    sts = [st_ref[...] for _, st_ref, _, _ in items]
    o1s = []
    for (p, _, _, _), st in zip(items, sts):
        rhs = jnp.concatenate([p["ke"], (st * p["eref"]).astype(BF16)], axis=0)
        o1s.append(_dot_nt(p["qm"], rhs))
    for h in range(HEADS):
        for (p, _, o_ref, mask), o1 in zip(items, o1s):
            blk = o1[h * L:(h + 1) * L, :]
            sc = jnp.where(mask, blk[:, 0:L], 0.0).astype(BF16)
            o_ref[p["rows"], h * 128:(h + 1) * 128] = (
                blk[:, L:L + 128] + _dot(sc, p["v"][:, h * 128:(h + 1) * 128]))
    for (p, st_ref, _, _), st in zip(items, sts):
        st_ref[...] = st * p["decay"] + p["upd"]


def _gla_exact_scan(za_ref, la_ref, st_ref, o_ref, b_ref, n, rev, st_init):
    L = RCHUNK
    nc = n // L
    m_lo, m_up = _tri_masks(L)
    tri = jnp.where(m_up if rev else m_lo, 1.0, 0.0).astype(BF16)
    s_idx = lax.broadcasted_iota(jnp.int32, (L, 1), 0)
    krow = lax.broadcasted_iota(jnp.int32, (HK, 128), 0) // 64
    kcol = lax.broadcasted_iota(jnp.int32, (HK, 128), 1)
    head_sum = jnp.where(krow == kcol, 1.0, 0.0)
    hsub = lax.broadcasted_iota(jnp.int32, (8, HK), 0)
    hlane = lax.broadcasted_iota(jnp.int32, (8, HK), 1) // 64
    head_rows = jnp.where(hsub == hlane, 1.0, 0.0)
    st_ref[...] = st_init

    def chunk_body(ci, carry):
        c = nc - 1 - ci if rev else ci
        r0 = pl.multiple_of(c * L, L)
        rows = pl.ds(r0, L)
        b = _tri_dot(tri, la_ref[rows, :])
        b_ref[...] = b
        b_end = b[0:1, :] if rev else b[L - 1:L, :]
        k = za_ref[rows, HK:2 * HK]
        vb = za_ref[rows, 2 * HK:2 * HK + HV].astype(BF16)
        st = st_ref[...]
        stb = st.astype(BF16)

        def row_body(t, carry2):
            b_t = b_ref[pl.ds(t, 1), :]
            q_t = za_ref[pl.ds(r0 + t, 1), 0:HK] * (A_DK ** -0.5)
            valid = (s_idx >= t) if rev else (s_idx <= t)
            w = jnp.exp(jnp.minimum(b_t - b_ref[...], 0.0))
            p = jnp.where(valid, (q_t * k) * w, 0.0)
            sc = jnp.dot(p, head_sum, precision=lax.Precision.HIGHEST,
                         preferred_element_type=F32)
            pv = _dot_tn(sc.astype(BF16), vb)
            qm = (head_rows * (q_t * jnp.exp(b_t))).astype(BF16)
            inter = _dot_nt(qm, stb)
            o_ref[pl.ds(r0 + t, 1), :] = jnp.concatenate(
                [pv[h:h + 1, h * 128:(h + 1) * 128] + inter[h:h + 1, :] for h in range(HEADS)],
                axis=1)
            return carry2

        lax.fori_loop(0, L, row_body, 0)
        kd = (k * jnp.exp(b_end - b)).astype(BF16)
        rt = _dot_tn(vb, kd)
        upd = None
        for h in range(HEADS):
            blk = rt[h * 128:(h + 1) * 128, :]
            blk = jnp.where(_head_lane_mask(blk.shape, h), blk, 0.0)
            upd = blk if upd is None else upd + blk
        st_ref[...] = st * jnp.exp(b_end) + upd
        return carry

    lax.fori_loop(0, nc, chunk_body, 0)


T_M, T_MOUT, T_G, T_SRCW, T_MPREV, T_CARRY, T_USED = 0, 8, 16, 24, 32, 40, 48


def _lane_scan(x, op, fill, is_fwd, pos, n):
    k = 1
    while k < RCHUNK:
        pf = jnp.where(pos >= k, pltpu.roll(x, k, axis=1), fill)
        sf = jnp.where(pos < RCHUNK - k, pltpu.roll(x, n - k, axis=1), fill)
        x = op(x, jnp.where(is_fwd, pf, sf))
        k *= 2
    return x


def _mlstm_gate_tables(g, m0, gt_ref, tab_ref, col_ref, n):
    L = RCHUNK
    nc = n // L
    for c in range(nc):
        gt_ref[:, c * L:(c + 1) * L] = g[c * L:(c + 1) * L, :].T
        tab_ref[c, T_USED:, :] = jnp.zeros((ZG_W - T_USED, L), F32)
    gi = gt_ref[GATE_COL:GATE_COL + 8, :]
    lf = gt_ref[GATE_COL + 8:GATE_COL + 16, :]
    is_fwd = lax.broadcasted_iota(jnp.int32, (8, n), 0) < HEADS
    pos = lax.broadcasted_iota(jnp.int32, (8, n), 1) % L
    b = _lane_scan(lf, jnp.add, 0.0, is_fwd, pos, n)
    gg = gi - b
    cmax = _lane_scan(gg, jnp.maximum, -jnp.inf, is_fwd, pos, n)
    fwd_l = lax.broadcasted_iota(jnp.int32, (8, L), 0) < HEADS
    mp = m0
    for j in range(nc):
        cf, cb = j, nc - 1 - j

        def pick(x):
            return jnp.where(fwd_l, x[:, cf * L:(cf + 1) * L], x[:, cb * L:(cb + 1) * L])

        gj, bj, cj, lj = pick(gg), pick(b), pick(cmax), pick(lf)
        mx = jnp.maximum(mp, jnp.max(gj, axis=1, keepdims=True))
        mj = jnp.maximum(cj, mp)
        blocks = (mj, bj + mj, gj, jnp.exp(gj - mx),
                  jnp.broadcast_to(mp, (8, L)), jnp.broadcast_to(jnp.exp(mp - mx), (8, L)))
        for t, val in enumerate(blocks):
            tab_ref[cf, 8 * t:8 * t + HEADS, :] = val[0:HEADS]
            tab_ref[cb, 8 * t + HEADS:8 * t + 8, :] = val[HEADS:8]
        mp = jnp.sum(lj, axis=1, keepdims=True) + mx
    for c in range(nc):
        col_ref[c * L:(c + 1) * L, :] = tab_ref[c].T
    return mp


def _mlstm_prepare(zb_ref, tab_ref, col_ref, c, d, mask):
    L = RCHUNK
    rows = pl.ds(pl.multiple_of(c * L, L), L)
    qb = zb_ref[rows, 0:HK].astype(BF16)
    kb = (zb_ref[rows, HK:2 * HK] * (B_DK ** -0.5)).astype(BF16)
    v = zb_ref[rows, 2 * HK:2 * HK + HV]
    vb = v.astype(BF16)
    vt = v.T
    qm = jnp.concatenate([qb * _head_row(h) for h in range(HEADS)], axis=0)
    s = _dot_nt(qm, kb)
    ones = jnp.ones((L, 128), BF16)
    pv, w_inter, e_negm = [], [], []
    upd = None
    carry = None
    for h in range(HEADS):
        r = 4 * d + h
        m_rep = jnp.broadcast_to(col_ref[rows, T_M + r:T_M + r + 1], (L, 128))
        mout_rep = jnp.broadcast_to(col_ref[rows, T_MOUT + r:T_MOUT + r + 1], (L, 128))
        g_row = tab_ref[c, T_G + r:T_G + r + 1, :]
        w = jnp.exp(jnp.where(mask, g_row - m_rep, -jnp.inf))
        qk = (s[h * L:(h + 1) * L, :] * w).astype(BF16)
        v1 = jnp.concatenate([vb[:, h * 128:(h + 1) * 128], ones], axis=1)
        pv.append(_dot(qk, v1))
        w_inter.append(jnp.exp(tab_ref[c, T_MPREV + r:T_MPREV + r + 1, :] - m_rep))
        e_negm.append(jnp.exp(-mout_rep))
        sw = tab_ref[c, T_SRCW + r:T_SRCW + r + 1, :]
        lhs = jnp.concatenate([vt[h * 128:(h + 1) * 128, :] * sw, jnp.broadcast_to(sw, (128, L))],
                              axis=0).astype(BF16)
        blk = _dot(lhs, kb)
        blk = jnp.where(_head_lane_mask(blk.shape, h), blk, 0.0)
        upd = blk if upd is None else upd + blk
        cr = tab_ref[c, T_CARRY + r:T_CARRY + r + 1, :]
        cr = jnp.where(_head_lane_mask((1, HK), h), jnp.concatenate([cr, cr], axis=1), 0.0)
        carry = cr if carry is None else carry + cr
    return dict(rows=rows, qm=qm, pv=pv, w_inter=w_inter, e_negm=e_negm, upd=upd, carry=carry)


def _mlstm_apply(items):
    L = RCHUNK
    sts = [st_ref[...] for _, st_ref, _ in items]
    a_all = [_dot_nt(p["qm"], st.astype(BF16)) for (p, _, _), st in zip(items, sts)]
    for h in range(HEADS):
        for (p, _, h_ref), a in zip(items, a_all):
            blk = a[h * L:(h + 1) * L, :]
            pv, w, e = p["pv"][h], p["w_inter"][h], p["e_negm"][h]
            num = w * blk[:, 0:128] + pv[:, 0:128]
            den = w * blk[:, 128:256] + pv[:, 128:256]
            h_ref[p["rows"], h * 128:(h + 1) * 128] = num / jnp.maximum(jnp.abs(den), e)
    for (p, st_ref, _), st in zip(items, sts):
        st_ref[...] = p["carry"] * st + p["upd"]


def _rec_kernel(*refs, n, has_init):
    it = iter(refs)

    def take(k):
        return [next(it) for _ in range(k)]

    za_all, zb_all, zg_all, zgn_all = take(4)
    wgf_ref, wgb_ref, bgf_ref, bgb_ref, gna_ref, bg_ref, gnb_ref = take(7)
    if has_init:
        init_all = take(7)
    out_all = take(8)
    (gt_all,) = take(1)
    set_a_all = take(6)
    set_b_all = take(6)
    work_all = take(9)
    L = RCHUNK
    nc = n // L
    step = pl.program_id(0)
    elems = []
    for e in range(za_all.shape[0]):
        el = dict(za=za_all.at[e], zb=zb_all.at[e], zg=zg_all.at[e], zgn=zgn_all.at[e],
                  gt=gt_all.at[e], set_a=[r.at[e] for r in set_a_all],
                  set_b=[r.at[e] for r in set_b_all])
        (el["oa"], el["sf"], el["sb"], el["om"], el["cf"], el["cb"], el["n_out"],
         el["m_out"]) = [r.at[e] for r in out_all]
        (el["of"], el["ob"], el["hf"], el["hb"], el["gsf"], el["gsb"], el["msf"],
         el["msb"], el["b_exact"]) = [r.at[e] for r in work_all]
        if has_init:
            el["init"] = [r.at[e] for r in init_all]
        elems.append(el)

    def fill(gate_ref, m0, bufs, gt_ref):
        laf_ref, lab_ref, tab_ref, col_ref, mf_ref, span_ref = bufs
        zg = gate_ref[...]
        laf = _log_sigmoid(_dot_hl(zg, wgf_ref[...]) + bgf_ref[...]) / A_GATE_TEMP
        lab = _log_sigmoid(_dot_hl(zg, wgb_ref[...]) + bgb_ref[...]) / A_GATE_TEMP
        laf_ref[...] = laf
        lab_ref[...] = lab
        span = None
        for la in (laf, lab):
            for c in range(nc):
                tot = -jnp.sum(la[c * L:(c + 1) * L, :], axis=0, keepdims=True)
                span = tot if span is None else jnp.maximum(span, tot)
        span_ref[...] = jnp.broadcast_to(jnp.max(span, axis=1, keepdims=True), (8, 128))
        g = zg + bg_ref[...]
        lane = lax.broadcasted_iota(jnp.int32, g.shape, 1)
        is_forget = (lane >= GATE_COL + 8) & (lane < GATE_COL + 16)
        g = jnp.where(is_forget, _log_sigmoid(g), g)
        m_fin = _mlstm_gate_tables(g, m0, gt_ref, tab_ref, col_ref, n)
        mf_ref[...] = jnp.broadcast_to(m_fin, (8, 128))

    for el in elems:
        if has_init:
            sf0_ref, sb0_ref, cf0_ref, cb0_ref, n0_ref, m0_ref, m0n_ref = el["init"]
            el["m0"], el["m0_next"] = m0_ref[:, 0:1], m0n_ref[:, 0:1]
            el["gsf"][...] = sf0_ref[...]
            el["gsb"][...] = sb0_ref[...]
            el["msf"][...] = jnp.concatenate(
                [cf0_ref[...], jnp.broadcast_to(n0_ref[0:1, :], (128, HK))], axis=0)
            el["msb"][...] = jnp.concatenate(
                [cb0_ref[...], jnp.broadcast_to(n0_ref[1:2, :], (128, HK))], axis=0)
        else:
            el["m0"] = el["m0_next"] = jnp.zeros((8, 1), F32)
            for name in ("gsf", "gsb", "msf", "msb"):
                el[name][...] = jnp.zeros(el[name].shape, F32)

    @pl.when(step == 0)
    def _():
        for el in elems:
            fill(el["zg"], el["m0"], el["set_a"], el["gt"])

    m_lo, m_up = _tri_masks(L)
    tri_lo = jnp.where(m_lo, 1.0, 0.0).astype(BF16)
    tri_up = jnp.where(m_up, 1.0, 0.0).astype(BF16)
    gna = gna_ref[...]
    gnb = gnb_ref[...]

    def out_body(c, carry):
        rows = pl.ds(pl.multiple_of(c * L, L), L)
        for h in range(HEADS):
            cols = slice(h * 128, (h + 1) * 128)
            gcols = slice(2 * HK + HV + h * 128, 2 * HK + HV + (h + 1) * 128)
            for el in elems:
                o = el["of"][rows, cols] + el["ob"][rows, cols]
                el["oa"][rows, cols] = (
                    _rms(o, gna) * _silu(el["za"][rows, gcols])).astype(BF16)
                hh = el["hf"][rows, cols] + el["hb"][rows, cols]
                el["om"][rows, cols] = (
                    _sigmoid(el["zb"][rows, gcols]) * _rms(hh, gnb)).astype(BF16)
        return carry

    def run(cur, nxt):
        for el in elems:
            fill(el["zgn"], el["m0_next"], el[nxt], el["gt"])

        def scan_body(gidx, carry):
            waves = []
            for w in range(2):
                c = 2 * gidx + w
                cb = nc - 1 - c
                gla_items, mlstm_items = [], []
                for el in elems:
                    laf_ref, lab_ref, tab_ref, col_ref = el[cur][0:4]
                    gla_items += [
                        (_gla_prepare(el["za"], laf_ref, pl.multiple_of(c * L, L), False, tri_lo),
                         el["gsf"], el["of"], m_lo),
                        (_gla_prepare(el["za"], lab_ref, pl.multiple_of(cb * L, L), True, tri_up),
                         el["gsb"], el["ob"], m_up)]
                    mlstm_items += [
                        (_mlstm_prepare(el["zb"], tab_ref, col_ref, c, 0, m_lo), el["msf"], el["hf"]),
                        (_mlstm_prepare(el["zb"], tab_ref, col_ref, cb, 1, m_up), el["msb"], el["hb"])]
                waves.append((gla_items, mlstm_items))
            for gla_items, mlstm_items in waves:
                _gla_apply(gla_items)
                _mlstm_apply(mlstm_items)
            return carry

        if nc == 2:
            scan_body(0, 0)
        else:
            lax.fori_loop(0, nc // 2, scan_body, 0)
        cur_all = set_a_all if cur == "set_a" else set_b_all
        worst = None
        for el in elems:
            span = el[cur][5][0, 0]
            worst = span if worst is None else jnp.maximum(worst, span)

        @pl.when(worst > GLA_SAFE_SPAN)
        def _():
            of_all, ob_all, _, _, gsf_all, gsb_all, _, _, bex_all = work_all

            def redo(e, carry):
                zero = jnp.zeros((128, HK), F32)
                init_f = init_all[0].at[e][...] if has_init else zero
                init_b = init_all[1].at[e][...] if has_init else zero
                _gla_exact_scan(za_all.at[e], cur_all[0].at[e], gsf_all.at[e], of_all.at[e],
                                bex_all.at[e], n, False, init_f)
                _gla_exact_scan(za_all.at[e], cur_all[1].at[e], gsb_all.at[e], ob_all.at[e],
                                bex_all.at[e], n, True, init_b)
                return carry

            lax.fori_loop(0, len(elems), redo, 0)

        if nc == 2:
            for c in range(nc):
                out_body(c, 0)
        else:
            lax.fori_loop(0, nc, out_body, 0)
        for el in elems:
            el["m_out"][...] = el[cur][4][...]

    @pl.when(step % 2 == 0)
    def _():
        run("set_a", "set_b")

    @pl.when(step % 2 == 1)
    def _():
        run("set_b", "set_a")

    for el in elems:
        gsf_t = el["gsf"][...].T
        gsb_t = el["gsb"][...].T
        msf = el["msf"][...]
        msb = el["msb"][...]
        ctf_t = msf[0:128, :].T
        ctb_t = msb[0:128, :].T
        for h in range(HEADS):
            rows = slice(h * 64, (h + 1) * 64)
            el["sf"][h] = gsf_t[rows, :]
            el["sb"][h] = gsb_t[rows, :]
            el["cf"][h] = ctf_t[rows, :]
            el["cb"][h] = ctb_t[rows, :]
        el["n_out"][...] = jnp.concatenate(
            [msf[128:129, :], msb[128:129, :], jnp.zeros((6, HK), F32)], axis=0)


def _recurrent(za, zb, zg, wgf, wgb, bgf, bgb, gna, bg, gnb, init, bt):
    bsz, n, _ = za.shape
    has_init = init is not None
    nc = n // RCHUNK
    steps = bsz // bt
    per_b3 = lambda b: (b, 0, 0)
    next_b3 = lambda b: (jnp.minimum(b + 1, steps - 1), 0, 0)
    per_b4 = lambda b: (b, 0, 0, 0)
    fixed = lambda b: (0, 0)
    in_specs = [pl.BlockSpec((bt, n, ZA_W), per_b3),
                pl.BlockSpec((bt, n, ZA_W), per_b3),
                pl.BlockSpec((bt, n, ZG_W), per_b3),
                pl.BlockSpec((bt, n, ZG_W), next_b3),
                pl.BlockSpec((ZG_W, HK), fixed),
                pl.BlockSpec((ZG_W, HK), fixed),
                pl.BlockSpec((1, HK), fixed),
                pl.BlockSpec((1, HK), fixed),
                pl.BlockSpec((1, 128), fixed),
                pl.BlockSpec((1, ZG_W), fixed),
                pl.BlockSpec((1, 128), fixed)]
    args = [za, zb, zg, zg, wgf, wgb, bgf, bgb, gna, bg, gnb]
    if has_init:
        sf0, sb0, cf0, cb0, n0, m0 = init
        in_specs += [pl.BlockSpec((bt, 128, HK), per_b3)] * 4
        in_specs += [pl.BlockSpec((bt, 8, HK), per_b3), pl.BlockSpec((bt, 8, 128), per_b3),
                     pl.BlockSpec((bt, 8, 128), next_b3)]
        args += [sf0, sb0, cf0, cb0, n0, m0, m0]
    state = jax.ShapeDtypeStruct((bsz, HEADS, 64, 128), F32)
    state_spec = pl.BlockSpec((bt, HEADS, 64, 128), per_b4)

    def vmem(*shape):
        return pltpu.VMEM((bt,) + shape, F32)

    gate_set = [vmem(n, HK), vmem(n, HK), vmem(nc, ZG_W, RCHUNK), vmem(n, ZG_W), vmem(8, 128),
                vmem(8, 128)]
    return pl.pallas_call(
        functools.partial(_rec_kernel, n=n, has_init=has_init),
        out_shape=(jax.ShapeDtypeStruct((bsz, n, HV), BF16), state, state,
                   jax.ShapeDtypeStruct((bsz, n, HV), BF16), state, state,
                   jax.ShapeDtypeStruct((bsz, 8, HK), F32),
                   jax.ShapeDtypeStruct((bsz, 8, 128), F32)),
        grid=(steps,),
        in_specs=in_specs,
        out_specs=(pl.BlockSpec((bt, n, HV), per_b3), state_spec, state_spec,
                   pl.BlockSpec((bt, n, HV), per_b3), state_spec, state_spec,
                   pl.BlockSpec((bt, 8, HK), per_b3),
                   pl.BlockSpec((bt, 8, 128), per_b3)),
        scratch_shapes=([vmem(ZG_W, n)] + gate_set + gate_set + [vmem(n, HV)] * 4
                        + [vmem(128, HK)] * 2 + [vmem(2 * 128, HK)] * 2 + [vmem(RCHUNK, HK)]),
        compiler_params=_cparams(("arbitrary",)),
        name="recurrent",
    )(*args)


def _mix_residual(parts, wo_ref, x_ref, gt1_ref, gpost1_ref, gpre2_ref, sc2_ref, sh2_ref):
    m = None
    off = 0
    for p in parts:
        kp = p.shape[1]
        t = _dot(p[...], wo_ref[off:off + kp, :])
        m = t if m is None else m + t
        off += kp
    x1 = x_ref[...] + gt1_ref[...] * _rms(m, gpost1_ref[...])
    hb = (_rms(x1, gpre2_ref[...]) * (1.0 + sc2_ref[...]) + sh2_ref[...]).astype(BF16)
    return x1, hb


def _ffn_gated_up(hb, wu_ref, cw_ref, cb_ref, act_ref, seq, tf):
    tm = hb.shape[0]
    pos = lax.broadcasted_iota(jnp.int32, (tm, 1), 0) % seq
    first = pos == 0
    last = pos == seq - 1

    def conv(u, cols):
        prev = jnp.where(first, 0.0, pltpu.roll(u, 1, axis=0))
        nxt = jnp.where(last, 0.0, pltpu.roll(u, tm - 1, axis=0))
        return (cw_ref[0:1, cols] * prev + cw_ref[1:2, cols] * u + cw_ref[2:3, cols] * nxt
                + cb_ref[:, cols])

    for j in range(D_FF // tf):
        ca = slice(j * tf, (j + 1) * tf)
        cg = slice(D_FF + j * tf, D_FF + (j + 1) * tf)
        a = conv(_dot(hb, wu_ref[:, ca]), ca)
        g = conv(_dot(hb, wu_ref[:, cg]), cg)
        act_ref[:, ca] = (_silu(g) * a).astype(BF16)


def _mix_ffn_kernel(*refs, n_parts, seq, tf):
    parts = refs[:n_parts]
    (wo_ref, x_ref, gt1_ref, gpost1_ref, gpre2_ref, sc2_ref, sh2_ref,
     wu_ref, cw_ref, cb_ref, wd_ref, gt2_ref, gpost2_ref, o_ref, act_ref) = refs[n_parts:]
    x1, hb = _mix_residual(parts, wo_ref, x_ref, gt1_ref, gpost1_ref, gpre2_ref, sc2_ref, sh2_ref)
    o_ref[...] = x1
    _ffn_gated_up(hb, wu_ref, cw_ref, cb_ref, act_ref, seq, tf)
    y = _dot(act_ref[...], wd_ref[...])
    o_ref[...] = o_ref[...] + gt2_ref[...] * _rms(y, gpost2_ref[...])


def _resident(shape):
    return pl.BlockSpec(shape, lambda *_: (0,) * len(shape), pipeline_mode=pl.Buffered(1))


def _mix_ffn(parts, w_out, x, mod, first_row, groups, gpost1, gpre2, w_up, conv_w, conv_b, w_down,
             gpost2, tm, seq, tf=256):
    t = x.shape[0]
    per = (t // tm) // groups
    row = lambda i: (i, 0)
    vec = _resident((1, D_MODEL))
    in_specs = [pl.BlockSpec((tm, p.shape[1]), row) for p in parts]
    in_specs += [_resident(w_out.shape), pl.BlockSpec((tm, D_MODEL), row),
                 _mod_spec(MOD_GATE1, first_row, per), vec, vec,
                 _mod_spec(MOD_SCALE2, first_row, per), _mod_spec(MOD_SHIFT2, first_row, per),
                 _resident(w_up.shape), _resident(conv_w.shape), _resident(conv_b.shape),
                 _resident(w_down.shape), _mod_spec(MOD_GATE2, first_row, per), vec]
    return pl.pallas_call(
        functools.partial(_mix_ffn_kernel, n_parts=len(parts), seq=seq, tf=tf),
        out_shape=jax.ShapeDtypeStruct((t, D_MODEL), F32),
        grid=(t // tm,),
        in_specs=in_specs,
        out_specs=pl.BlockSpec((tm, D_MODEL), row),
        scratch_shapes=[pltpu.VMEM((tm, D_FF), BF16)],
        compiler_params=_cparams(("arbitrary",)),
        name="mix_ffn",
    )(*parts, w_out, x, mod, gpost1, gpre2, mod, mod, w_up, conv_w, conv_b, w_down, mod, gpost2)


def _rope(x, cos, sin_signed):
    lane = lax.broadcasted_iota(jnp.int32, x.shape, 1)
    lower = (lane % 32) < 16
    partner = jnp.where(lower, pltpu.roll(x, 128 - 16, axis=1), pltpu.roll(x, 16, axis=1))
    return x * cos + partner * sin_signed


def _p1_kernel(*refs, rope):
    if rope:
        (x_ref, g_ref, sc_ref, sh_ref, w_ref, gq_ref, wq_ref, gkv_ref, cos_ref, sin_ref,
         q_ref, ckv_ref, kr_ref) = refs
    else:
        (x_ref, g_ref, sc_ref, sh_ref, w_ref, gq_ref, wq_ref, gkv_ref,
         q_ref, ckv_ref, kr_ref) = refs
    h = _rms(x_ref[...], g_ref[...]) * (1.0 + sc_ref[...]) + sh_ref[...]
    z = _dot(h.astype(BF16), w_ref[...])
    qa = _rms(z[:, 0:C_Q_RANK], gq_ref[...])
    q = _dot(qa.astype(BF16), wq_ref[...])
    ckv_ref[...] = _rms(z[:, C_Q_RANK:C_Q_RANK + C_KV_RANK], gkv_ref[...])
    kr = z[:, C_Q_RANK + C_KV_RANK:L1_IN_PAD]
    if rope:
        cos = cos_ref[...]
        sin = sin_ref[...]
        kr = _rope(kr, cos, sin)
    kr_ref[...] = kr[:, 0:C_ROPE]
    for hd in range(C_HEADS):
        c0 = hd * QCAT
        q_ref[:, c0:c0 + 128] = q[:, c0:c0 + 128].astype(q_ref.dtype)
        qr = q[:, c0 + 128:c0 + 256]
        if rope:
            qr = _rope(qr, cos, sin)
        q_ref[:, c0 + 128:c0 + 256] = qr.astype(q_ref.dtype)


def _proj_l1(x, g, mod, first_row, groups, w, gq, wq, gkv, tables, tm):
    t = x.shape[0]
    per = (t // tm) // groups
    row = lambda i: (i, 0)
    fixed = lambda i: (0, 0)
    in_specs = [pl.BlockSpec((tm, D_MODEL), row),
                pl.BlockSpec((1, D_MODEL), fixed),
                _mod_spec(MOD_SCALE1, first_row, per),
                _mod_spec(MOD_SHIFT1, first_row, per),
                pl.BlockSpec(w.shape, fixed),
                pl.BlockSpec((1, C_Q_RANK), fixed),
                pl.BlockSpec(wq.shape, fixed),
                pl.BlockSpec((1, C_KV_RANK), fixed)]
    args = [x, g, mod, mod, w, gq, wq, gkv]
    rope = tables is not None
    if rope:
        n = tables[0].shape[0]
        pos = lambda i: (i % (n // tm), 0)
        in_specs += [pl.BlockSpec((tm, 128), pos)] * 2
        args += list(tables)
    return pl.pallas_call(
        functools.partial(_p1_kernel, rope=rope),
        out_shape=(jax.ShapeDtypeStruct((t, C_HEADS * QCAT), BF16),
                   jax.ShapeDtypeStruct((t, C_KV_RANK), F32),
                   jax.ShapeDtypeStruct((t, C_ROPE), F32)),
        grid=(t // tm,),
        in_specs=in_specs,
        out_specs=(pl.BlockSpec((tm, C_HEADS * QCAT), row),
                   pl.BlockSpec((tm, C_KV_RANK), row),
                   pl.BlockSpec((tm, C_ROPE), row)),
        compiler_params=_cparams(("arbitrary",)),
        name="proj_l1",
    )(*args)


def _attn_kernel(*refs, n_parts, scale):
    q_ref = refs[0]
    kv_parts = refs[1:1 + 2 * n_parts]
    w_ref, o_ref, k_sc, v_sc = refs[1 + 2 * n_parts:]
    bt = q_ref.shape[0]

    @pl.when(pl.program_id(1) == 0)
    def _():
        off = 0
        for p in range(n_parts):
            ckv_ref, kr_ref = kv_parts[2 * p], kv_parts[2 * p + 1]
            nk = ckv_ref.shape[1]
            ones = jnp.ones((nk, C_DV), BF16)
            for e in range(bt):
                kv = _dot(ckv_ref[e].astype(BF16), w_ref[...])
                kr = kr_ref[e].astype(BF16)
                kr2 = jnp.concatenate([kr, kr], axis=1)
                for hd in range(C_HEADS):
                    v0 = C_HEADS * C_NOPE + hd * C_DV
                    k_sc[e, hd, off:off + nk, 0:128] = kv[:, hd * 128:(hd + 1) * 128].astype(BF16)
                    k_sc[e, hd, off:off + nk, 128:256] = kr2
                    v_sc[e, hd, off:off + nk, 0:C_DV] = kv[:, v0:v0 + C_DV].astype(BF16)
                    v_sc[e, hd, off:off + nk, C_DV:2 * C_DV] = ones
            off += nk

    for hd in range(C_HEADS):
        for e in range(bt):
            t = _dot_nt(q_ref[e, :, hd * QCAT:(hd + 1) * QCAT], k_sc[e, hd]) * (scale * LOG2E)
            p = jnp.exp2(t - jnp.max(t, axis=1, keepdims=True))
            ov = _dot(p.astype(BF16), v_sc[e, hd])
            o_ref[e, :, hd * C_DV:(hd + 1) * C_DV] = (
                ov[:, 0:C_DV] / ov[:, C_DV:2 * C_DV]).astype(o_ref.dtype)


def _attention(q, kv_parts, w_kvb, tq, bt):
    bsz, n, _ = q.shape
    nk = sum(p[0].shape[1] for p in kv_parts)
    in_specs = [pl.BlockSpec((bt, tq, C_HEADS * QCAT), lambda b, i: (b, i, 0))]
    args = [q]
    for ckv, kr in kv_parts:
        in_specs.append(pl.BlockSpec((bt, ckv.shape[1], C_KV_RANK), lambda b, i: (b, 0, 0)))
        in_specs.append(pl.BlockSpec((bt, kr.shape[1], C_ROPE), lambda b, i: (b, 0, 0)))
        args += [ckv, kr]
    in_specs.append(pl.BlockSpec(w_kvb.shape, lambda b, i: (0, 0)))
    args.append(w_kvb)
    return pl.pallas_call(
        functools.partial(_attn_kernel, n_parts=len(kv_parts),
                          scale=(C_NOPE + C_ROPE) ** -0.5),
        out_shape=jax.ShapeDtypeStruct((bsz, n, C_HEADS * C_DV), BF16),
        grid=(bsz // bt, n // tq),
        in_specs=in_specs,
        out_specs=pl.BlockSpec((bt, tq, C_HEADS * C_DV), lambda b, i: (b, i, 0)),
        scratch_shapes=[pltpu.VMEM((bt, C_HEADS, nk, QCAT), BF16),
                        pltpu.VMEM((bt, C_HEADS, nk, 2 * C_DV), BF16)],
        compiler_params=_cparams(("arbitrary", "arbitrary")),
        name="mla_attention",
    )(*args)


def _l1_wq_layout(w_qb):
    blocks = []
    zero = jnp.zeros((w_qb.shape[0], C_ROPE), w_qb.dtype)
    for hd in range(C_HEADS):
        base = hd * (C_NOPE + C_ROPE)
        nope = w_qb[:, base:base + C_NOPE]
        rope = w_qb[:, base + C_NOPE:base + C_NOPE + C_ROPE]
        blocks += [nope, rope, zero] if hd % 2 == 0 else [nope, zero, rope]
    return jnp.concatenate(blocks, axis=1).astype(BF16)


def _l1_wkvb_layout(w_kvb):
    w = w_kvb.reshape(C_KV_RANK, C_HEADS, C_NOPE + C_DV)
    k = w[:, :, :C_NOPE].reshape(C_KV_RANK, C_HEADS * C_NOPE)
    v = w[:, :, C_NOPE:].reshape(C_KV_RANK, C_HEADS * C_DV)
    return jnp.concatenate([k, v], axis=1).astype(BF16)


def _rope_tables(n):
    pos = np.arange(n)
    row = (pos // GRID_W).astype(np.float64)
    col = (pos % GRID_W).astype(np.float64)
    half = C_ROPE // 2
    inv = 1.0 / (ROPE_THETA ** (np.arange(0, half, 2, dtype=np.float64) / half))
    ang_r = row[:, None] * inv[None, :]
    ang_c = col[:, None] * inv[None, :]
    cos64 = np.concatenate([np.cos(ang_r)] * 2 + [np.cos(ang_c)] * 2, axis=1)
    sin64 = np.concatenate([-np.sin(ang_r), np.sin(ang_r), -np.sin(ang_c), np.sin(ang_c)], axis=1)
    return (jnp.asarray(np.concatenate([cos64] * 2, axis=1), F32),
            jnp.asarray(np.concatenate([sin64] * 2, axis=1), F32))


def _gate_weight(w, first_row):
    return jnp.pad(w, ((first_row, ZG_W - first_row - A_GATE_RANK), (0, 0)))


def _state_t(s):
    b = s.shape[0]
    return s.transpose(0, 3, 1, 2).reshape(b, 128, HK)


def _n_rows(n_fwd, n_bwd):
    b = n_fwd.shape[0]
    rows = jnp.zeros((b, 8, HK), F32)
    return rows.at[:, 0, :].set(n_fwd.reshape(b, HK)).at[:, 1, :].set(n_bwd.reshape(b, HK))


def _m_rows(m_fwd, m_bwd):
    m = jnp.concatenate([m_fwd, m_bwd], axis=1)
    return jnp.broadcast_to(m[:, :, None], (m.shape[0], 8, 128))


def kernel(x_prompt, x_sample, state_l0_gla_fwd, state_l0_gla_bwd, state_l0_mlstm_c_fwd, state_l0_mlstm_n_fwd, state_l0_mlstm_m_fwd, state_l0_mlstm_c_bwd, state_l0_mlstm_n_bwd, state_l0_mlstm_m_bwd, cache_l1_ckv, cache_l1_krope, c, c_ctx, l0_w_mod, l0_b_mod, l0_g_pre_mix, l0_g_post_mix, l0_g_pre_ffn, l0_g_post_ffn, l0_w_in, l0_gla_w_gate_f, l0_gla_b_gate_f, l0_gla_w_gate_b, l0_gla_b_gate_b, l0_gla_g_norm, l0_mlstm_b_gates, l0_mlstm_g_norm, l0_w_out, l0_ffn_w_up, l0_ffn_conv_w, l0_ffn_conv_b, l0_ffn_w_down, l1_w_mod, l1_b_mod, l1_g_pre_mix, l1_g_post_mix, l1_g_pre_ffn, l1_g_post_ffn, l1_w_in, l1_g_q_norm, l1_w_qb, l1_g_kv_norm, l1_w_kvb, l1_w_out, l1_ffn_w_up, l1_ffn_conv_w, l1_ffn_conv_b, l1_ffn_w_down):
    bp, sp, d = x_prompt.shape
    bs, ss, _ = x_sample.shape
    row = lambda v: v.reshape(1, -1)

    cond8 = jnp.concatenate([c_ctx[None, :], c, jnp.zeros((8 - 1 - bs, d), F32)], axis=0)
    mods = (_modulation(cond8, l0_w_mod, l0_b_mod), _modulation(cond8, l1_w_mod, l1_b_mod))

    paths = {
        "ctx": dict(x=x_prompt.reshape(bp * sp, d), b=bp, n=sp, first_row=0, groups=1, tm=512),
        "lat": dict(x=x_sample.reshape(bs * ss, d), b=bs, n=ss, first_row=1, groups=bs, tm=512),
    }

    w_in0 = l0_w_in.T
    wgf = _gate_weight(l0_gla_w_gate_f, 0)
    wgb = _gate_weight(l0_gla_w_gate_b, A_GATE_RANK)
    bg = jnp.pad(l0_mlstm_b_gates, (GATE_COL, ZG_W - GATE_COL - 16)).reshape(1, ZG_W)
    w_out0 = l0_w_out.astype(BF16)
    w_in1 = jnp.pad(l1_w_in, ((0, 0), (0, L1_IN_PAD - l1_w_in.shape[1]))).astype(BF16)
    w_q1 = _l1_wq_layout(l1_w_qb)
    w_kvb1 = _l1_wkvb_layout(l1_w_kvb)
    w_out1 = l1_w_out.astype(BF16)
    ffn = ((l0_ffn_w_up.astype(BF16), l0_ffn_conv_w, row(l0_ffn_conv_b), l0_ffn_w_down.astype(BF16)),
           (l1_ffn_w_up.astype(BF16), l1_ffn_conv_w, row(l1_ffn_conv_b), l1_ffn_w_down.astype(BF16)))
    norms = ((row(l0_g_pre_mix), row(l0_g_post_mix), row(l0_g_pre_ffn), row(l0_g_post_ffn)),
             (row(l1_g_pre_mix), row(l1_g_post_mix), row(l1_g_pre_ffn), row(l1_g_post_ffn)))

    gla_init = (_state_t(state_l0_gla_fwd), _state_t(state_l0_gla_bwd))
    mlstm_init = (_state_t(state_l0_mlstm_c_fwd), _state_t(state_l0_mlstm_c_bwd),
                  _n_rows(state_l0_mlstm_n_fwd, state_l0_mlstm_n_bwd),
                  _m_rows(state_l0_mlstm_m_fwd, state_l0_mlstm_m_bwd))
    tables = _rope_tables(ss)

    results = {}
    for name, p in paths.items():
        x, b, n, tm = p["x"], p["b"], p["n"], p["tm"]
        first_row, groups = p["first_row"], p["groups"]
        is_ctx = name == "ctx"
        ffn_tm = tm if is_ctx else n
        g_pre, g_post, g_pre2, g_post2 = norms[0]

        za, zb, zg = _proj_l0(x, g_pre, mods[0], first_row, groups, w_in0, tm)
        za, zb, zg = (t.reshape(b, n, -1) for t in (za, zb, zg))
        oa, sa_f, sa_b, om, c_f, c_b, n_rows, m_rows = _recurrent(
            za, zb, zg, wgf, wgb, row(l0_gla_b_gate_f), row(l0_gla_b_gate_b), row(l0_gla_g_norm),
            bg, row(l0_mlstm_g_norm), None if is_ctx else gla_init + mlstm_init,
            2 if is_ctx else 1)
        x = _mix_ffn([oa.reshape(b * n, HV), om.reshape(b * n, HV)], w_out0, x, mods[0], first_row,
                     groups, g_post, g_pre2, *ffn[0], g_post2, ffn_tm, n)

        g_pre, g_post, g_pre2, g_post2 = norms[1]
        q, ckv, kr = _proj_l1(x, g_pre, mods[1], first_row, groups, w_in1, row(l1_g_q_norm), w_q1,
                              row(l1_g_kv_norm), None if is_ctx else tables, tm)
        ckv3 = ckv.reshape(b, n, C_KV_RANK)
        kr3 = kr.reshape(b, n, C_ROPE)
        kv_parts = [(ckv3, kr3)] if is_ctx else [(cache_l1_ckv, cache_l1_krope), (ckv3, kr3)]
        o = _attention(q.reshape(b, n, -1), kv_parts, w_kvb1, 256, 4 if is_ctx else 1)
        x = _mix_ffn([o.reshape(b * n, C_HEADS * C_DV)], w_out1, x, mods[1], first_row,
                     groups, g_post, g_pre2, *ffn[1], g_post2, ffn_tm, n)

        results[name] = dict(y=x.reshape(b, n, d), gla=(sa_f, sa_b), c=(c_f, c_b),
                             n=n_rows, m=m_rows, ckv=ckv3, kr=kr3)

    r = results["ctx"]
    n_fwd = r["n"][:, 0, :].reshape(bp, B_HEADS, B_DK)
    n_bwd = r["n"][:, 1, :].reshape(bp, B_HEADS, B_DK)
    m_fwd = r["m"][:, 0:B_HEADS, 0]
    m_bwd = r["m"][:, B_HEADS:2 * B_HEADS, 0]
    return (r["y"], results["lat"]["y"], r["gla"][0], r["gla"][1],
            r["c"][0], n_fwd, m_fwd, r["c"][1], n_bwd, m_bwd, r["ckv"], r["kr"])
```

```python
import functools

import numpy as np
import jax
import jax.numpy as jnp
from jax import lax
from jax.experimental import pallas as pl
from jax.experimental.pallas import tpu as pltpu

F32 = jnp.float32
BF16 = jnp.bfloat16

D_MODEL = 1024
EPS = 1e-6
LOG2E = 1.4426950408889634
N_MOD = 6
GRID_W = 64
ROPE_THETA = 10000.0

A_HEADS, A_DK, A_DV, A_GATE_RANK, A_GATE_TEMP = 4, 64, 128, 16, 16.0
B_HEADS, B_DK, B_DV = 4, 64, 128
C_HEADS, C_Q_RANK, C_KV_RANK, C_NOPE, C_ROPE, C_DV = 8, 384, 256, 128, 64, 128
D_FF = 2816
CONV_W = 3

HEADS = 4
HK = HEADS * 64
HV = HEADS * 128
ZA_W = 1536
ZG_W = 128
GATE_COL = 32
RCHUNK = 128
GLA_SAFE_SPAN = 60.0
L1_IN_PAD = 768
QCAT = 256

V7X_VMEM_BYTES = 64 * 1024 * 1024
VMEM_LIMIT = V7X_VMEM_BYTES - 8 * 1024 * 1024


def _cparams(sem):
    return pltpu.CompilerParams(dimension_semantics=sem, vmem_limit_bytes=VMEM_LIMIT)


def _dot(a, b):
    return jnp.dot(a, b, preferred_element_type=F32)


def _dot_nt(a, b):
    return lax.dot_general(a, b, (((1,), (1,)), ((), ())), preferred_element_type=F32)


def _dot_tn(a, b):
    return lax.dot_general(a, b, (((0,), (0,)), ((), ())), preferred_element_type=F32)


def _split3(x):
    hi = x.astype(BF16)
    r1 = x - hi.astype(F32)
    mid = r1.astype(BF16)
    lo = (r1 - mid.astype(F32)).astype(BF16)
    return hi, mid, lo


def _tri_dot(tri, x):
    hi, mid, lo = _split3(x)
    return _dot(tri, hi) + _dot(tri, mid) + _dot(tri, lo)


def _dot_hl(a, b):
    ah = a.astype(BF16)
    al = (a - ah.astype(F32)).astype(BF16)
    bh = b.astype(BF16)
    bl = (b - bh.astype(F32)).astype(BF16)
    return _dot(ah, bh) + _dot(ah, bl) + _dot(al, bh)


def _rms(x, g):
    return x * lax.rsqrt(jnp.mean(x * x, axis=-1, keepdims=True) + EPS) * g


def _sigmoid(x):
    return 1.0 / (1.0 + jnp.exp(-x))


def _silu(x):
    return x * _sigmoid(x)


def _log_sigmoid(x):
    return jnp.minimum(x, 0.0) - jnp.log(1.0 + jnp.exp(-jnp.abs(x)))


def _tri_masks(n):
    r = lax.broadcasted_iota(jnp.int32, (n, n), 0)
    c = lax.broadcasted_iota(jnp.int32, (n, n), 1)
    return c <= r, c >= r


def _mod_kernel(cond_ref, w_ref, b_ref, o_ref):
    s = _silu(cond_ref[...])
    o_ref[...] = _dot(s.astype(BF16), w_ref[...].astype(BF16)) + b_ref[...]


def _modulation(cond8, w_mod, b_mod):
    d = D_MODEL
    out = pl.pallas_call(
        _mod_kernel,
        out_shape=jax.ShapeDtypeStruct((N_MOD, 8, d), F32),
        grid=(N_MOD,),
        in_specs=[pl.BlockSpec((8, d), lambda j: (0, 0)),
                  pl.BlockSpec((d, d), lambda j: (0, j)),
                  pl.BlockSpec((1, d), lambda j: (0, j))],
        out_specs=pl.BlockSpec((None, 8, d), lambda j: (j, 0, 0)),
        compiler_params=_cparams(("arbitrary",)),
        name="modulation",
    )(cond8, w_mod, b_mod.reshape(1, N_MOD * d))
    return out.reshape(N_MOD, 8, 1, d)


MOD_SHIFT1, MOD_SCALE1, MOD_GATE1, MOD_SHIFT2, MOD_SCALE2, MOD_GATE2 = range(N_MOD)


def _mod_spec(which, first_row, per):
    return pl.BlockSpec((None, None, 1, D_MODEL), lambda i: (which, first_row + i // per, 0, 0))


L0_GROUP_ROWS = ((0, ZA_W), (ZA_W + 32, 2 * ZA_W + 32))
L0_LR_ROWS = (ZA_W, ZA_W + 32)
L0_GATE_ROWS = (2 * ZA_W + 32, 2 * ZA_W + 48)


def _p0_kernel(x_ref, g_ref, sc_ref, sh_ref, wt_ref, za_ref, zb_ref, zg_ref, w_ref):
    @pl.when(pl.program_id(0) == 0)
    def _():
        blk = 256
        for gi, (r0, r1) in enumerate(L0_GROUP_ROWS):
            for j in range((r1 - r0) // blk):
                rows = wt_ref[r0 + j * blk:r0 + (j + 1) * blk, :]
                w_ref[:, gi * ZA_W + j * blk:gi * ZA_W + (j + 1) * blk] = rows.T.astype(BF16)
        tail = jnp.concatenate([wt_ref[L0_LR_ROWS[0]:L0_LR_ROWS[1], :],
                                wt_ref[L0_GATE_ROWS[0]:L0_GATE_ROWS[1], :],
                                jnp.zeros((ZG_W - 48, D_MODEL), F32)], axis=0)
        w_ref[:, 2 * ZA_W:2 * ZA_W + ZG_W] = tail.T.astype(BF16)

    h = _rms(x_ref[...], g_ref[...]) * (1.0 + sc_ref[...]) + sh_ref[...]
    hb = h.astype(BF16)
    za_ref[...] = _dot(hb, w_ref[:, 0:ZA_W])
    zb_ref[...] = _dot(hb, w_ref[:, ZA_W:2 * ZA_W])
    zg_ref[...] = _dot(hb, w_ref[:, 2 * ZA_W:2 * ZA_W + ZG_W])


def _proj_l0(x, g, mod, first_row, groups, w_t, tm):
    t = x.shape[0]
    per = (t // tm) // groups
    row = lambda i: (i, 0)
    fixed = lambda i: (0, 0)
    return pl.pallas_call(
        _p0_kernel,
        out_shape=(jax.ShapeDtypeStruct((t, ZA_W), F32),
                   jax.ShapeDtypeStruct((t, ZA_W), F32),
                   jax.ShapeDtypeStruct((t, ZG_W), F32)),
        grid=(t // tm,),
        in_specs=[pl.BlockSpec((tm, D_MODEL), row),
                  pl.BlockSpec((1, D_MODEL), fixed),
                  _mod_spec(MOD_SCALE1, first_row, per),
                  _mod_spec(MOD_SHIFT1, first_row, per),
                  _resident(w_t.shape)],
        out_specs=(pl.BlockSpec((tm, ZA_W), row),
                   pl.BlockSpec((tm, ZA_W), row),
                   pl.BlockSpec((tm, ZG_W), row)),
        scratch_shapes=[pltpu.VMEM((D_MODEL, 2 * ZA_W + ZG_W), BF16)],
        compiler_params=_cparams(("arbitrary",)),
        name="proj_l0",
    )(x, g, mod, mod, w_t)


def _head_lane_mask(shape, h):
    lane = lax.broadcasted_iota(jnp.int32, shape, len(shape) - 1)
    return (lane >= h * 64) & (lane < (h + 1) * 64)


def _head_row(h):
    return jnp.where(_head_lane_mask((1, HK), h), 1.0, 0.0).astype(BF16)


def _gla_prepare(za_ref, la_ref, r0, rev, tri):
    L = RCHUNK
    rows = pl.ds(r0, L)
    b = _tri_dot(tri, la_ref[rows, :])
    mid = L // 2
    ref = b[mid:mid + 1, :]
    b_end = b[0:1, :] if rev else b[L - 1:L, :]
    q = za_ref[rows, 0:HK]
    k = za_ref[rows, HK:2 * HK]
    v = za_ref[rows, 2 * HK:2 * HK + HV].astype(BF16)
    qe = (q * (jnp.exp(b - ref) * (A_DK ** -0.5))).astype(BF16)
    ke = (k * jnp.exp(ref - b)).astype(BF16)
    qm = jnp.concatenate([qe * _head_row(h) for h in range(HEADS)], axis=0)
    rt = _dot_tn(v, ke)
    upd = None
    for h in range(HEADS):
        blk = rt[h * 128:(h + 1) * 128, :]
        blk = jnp.where(_head_lane_mask(blk.shape, h), blk, 0.0)
        upd = blk if upd is None else upd + blk
    return dict(rows=rows, qm=qm, ke=ke, v=v, upd=upd * jnp.exp(b_end - ref),
                decay=jnp.exp(b_end), eref=jnp.exp(ref))


def _gla_apply(items):
    L = RCHUNK
    sts = [st_ref[...] for _, st_ref, _, _ in items]
    o1s = []
    for (p, _, _, _), st in zip(items, sts):
        rhs = jnp.concatenate([p["ke"], (st * p["eref"]).astype(BF16)], axis=0)
        o1s.append(_dot_nt(p["qm"], rhs))
    for h in range(HEADS):
        for (p, _, o_ref, mask), o1 in zip(items, o1s):
            blk = o1[h * L:(h + 1) * L, :]
            sc = jnp.where(mask, blk[:, 0:L], 0.0).astype(BF16)
            o_ref[p["rows"], h * 128:(h + 1) * 128] = (
                blk[:, L:L + 128] + _dot(sc, p["v"][:, h * 128:(h + 1) * 128]))
    for (p, st_ref, _, _), st in zip(items, sts):
        st_ref[...] = st * p["decay"] + p["upd"]


def _gla_exact_scan(za_ref, la_ref, st_ref, o_ref, b_ref, n, rev, st_init):
    L = RCHUNK
    nc = n // L
    m_lo, m_up = _tri_masks(L)
    tri = jnp.where(m_up if rev else m_lo, 1.0, 0.0).astype(BF16)
    s_idx = lax.broadcasted_iota(jnp.int32, (L, 1), 0)
    krow = lax.broadcasted_iota(jnp.int32, (HK, 128), 0) // 64
    kcol = lax.broadcasted_iota(jnp.int32, (HK, 128), 1)
    head_sum = jnp.where(krow == kcol, 1.0, 0.0)
    hsub = lax.broadcasted_iota(jnp.int32, (8, HK), 0)
    hlane = lax.broadcasted_iota(jnp.int32, (8, HK), 1) // 64
    head_rows = jnp.where(hsub == hlane, 1.0, 0.0)
    st_ref[...] = st_init

    def chunk_body(ci, carry):
        c = nc - 1 - ci if rev else ci
        r0 = pl.multiple_of(c * L, L)
        rows = pl.ds(r0, L)
        b = _tri_dot(tri, la_ref[rows, :])
        b_ref[...] = b
        b_end = b[0:1, :] if rev else b[L - 1:L, :]
        k = za_ref[rows, HK:2 * HK]
        vb = za_ref[rows, 2 * HK:2 * HK + HV].astype(BF16)
        st = st_ref[...]
        stb = st.astype(BF16)

        def row_body(t, carry2):
            b_t = b_ref[pl.ds(t, 1), :]
            q_t = za_ref[pl.ds(r0 + t, 1), 0:HK] * (A_DK ** -0.5)
            valid = (s_idx >= t) if rev else (s_idx <= t)
            w = jnp.exp(jnp.minimum(b_t - b_ref[...], 0.0))
            p = jnp.where(valid, (q_t * k) * w, 0.0)
            sc = jnp.dot(p, head_sum, precision=lax.Precision.HIGHEST,
                         preferred_element_type=F32)
            pv = _dot_tn(sc.astype(BF16), vb)
            qm = (head_rows * (q_t * jnp.exp(b_t))).astype(BF16)
            inter = _dot_nt(qm, stb)
            o_ref[pl.ds(r0 + t, 1), :] = jnp.concatenate(
                [pv[h:h + 1, h * 128:(h + 1) * 128] + inter[h:h + 1, :] for h in range(HEADS)],
                axis=1)
            return carry2

        lax.fori_loop(0, L, row_body, 0)
        kd = (k * jnp.exp(b_end - b)).astype(BF16)
        rt = _dot_tn(vb, kd)
        upd = None
        for h in range(HEADS):
            blk = rt[h * 128:(h + 1) * 128, :]
            blk = jnp.where(_head_lane_mask(blk.shape, h), blk, 0.0)
            upd = blk if upd is None else upd + blk
        st_ref[...] = st * jnp.exp(b_end) + upd
        return carry

    lax.fori_loop(0, nc, chunk_body, 0)


T_M, T_MOUT, T_G, T_SRCW, T_MPREV, T_CARRY, T_USED = 0, 8, 16, 24, 32, 40, 48


def _lane_scan(x, op, fill, is_fwd, pos, n):
    k = 1
    while k < RCHUNK:
        pf = jnp.where(pos >= k, pltpu.roll(x, k, axis=1), fill)
        sf = jnp.where(pos < RCHUNK - k, pltpu.roll(x, n - k, axis=1), fill)
        x = op(x, jnp.where(is_fwd, pf, sf))
        k *= 2
    return x


def _mlstm_gate_tables(g, m0, gt_ref, tab_ref, col_ref, n):
    L = RCHUNK
    nc = n // L
    for c in range(nc):
        gt_ref[:, c * L:(c + 1) * L] = g[c * L:(c + 1) * L, :].T
        tab_ref[c, T_USED:, :] = jnp.zeros((ZG_W - T_USED, L), F32)
    gi = gt_ref[GATE_COL:GATE_COL + 8, :]
    lf = gt_ref[GATE_COL + 8:GATE_COL + 16, :]
    is_fwd = lax.broadcasted_iota(jnp.int32, (8, n), 0) < HEADS
    pos = lax.broadcasted_iota(jnp.int32, (8, n), 1) % L
    b = _lane_scan(lf, jnp.add, 0.0, is_fwd, pos, n)
    gg = gi - b
    cmax = _lane_scan(gg, jnp.maximum, -jnp.inf, is_fwd, pos, n)
    fwd_l = lax.broadcasted_iota(jnp.int32, (8, L), 0) < HEADS
    mp = m0
    for j in range(nc):
        cf, cb = j, nc - 1 - j

        def pick(x):
            return jnp.where(fwd_l, x[:, cf * L:(cf + 1) * L], x[:, cb * L:(cb + 1) * L])

        gj, bj, cj, lj = pick(gg), pick(b), pick(cmax), pick(lf)
        mx = jnp.maximum(mp, jnp.max(gj, axis=1, keepdims=True))
        mj = jnp.maximum(cj, mp)
        blocks = (mj, bj + mj, gj, jnp.exp(gj - mx),
                  jnp.broadcast_to(mp, (8, L)), jnp.broadcast_to(jnp.exp(mp - mx), (8, L)))
        for t, val in enumerate(blocks):
            tab_ref[cf, 8 * t:8 * t + HEADS, :] = val[0:HEADS]
            tab_ref[cb, 8 * t + HEADS:8 * t + 8, :] = val[HEADS:8]
        mp = jnp.sum(lj, axis=1, keepdims=True) + mx
    for c in range(nc):
        col_ref[c * L:(c + 1) * L, :] = tab_ref[c].T
    return mp


def _mlstm_prepare(zb_ref, tab_ref, col_ref, c, d, mask):
    L = RCHUNK
    rows = pl.ds(pl.multiple_of(c * L, L), L)
    qb = zb_ref[rows, 0:HK].astype(BF16)
    kb = (zb_ref[rows, HK:2 * HK] * (B_DK ** -0.5)).astype(BF16)
    v = zb_ref[rows, 2 * HK:2 * HK + HV]
    vb = v.astype(BF16)
    vt = v.T
    qm = jnp.concatenate([qb * _head_row(h) for h in range(HEADS)], axis=0)
    s = _dot_nt(qm, kb)
    ones = jnp.ones((L, 128), BF16)
    pv, w_inter, e_negm = [], [], []
    upd = None
    carry = None
    for h in range(HEADS):
        r = 4 * d + h
        m_rep = jnp.broadcast_to(col_ref[rows, T_M + r:T_M + r + 1], (L, 128))
        mout_rep = jnp.broadcast_to(col_ref[rows, T_MOUT + r:T_MOUT + r + 1], (L, 128))
        g_row = tab_ref[c, T_G + r:T_G + r + 1, :]
        w = jnp.exp(jnp.where(mask, g_row - m_rep, -jnp.inf))
        qk = (s[h * L:(h + 1) * L, :] * w).astype(BF16)
        v1 = jnp.concatenate([vb[:, h * 128:(h + 1) * 128], ones], axis=1)
        pv.append(_dot(qk, v1))
        w_inter.append(jnp.exp(tab_ref[c, T_MPREV + r:T_MPREV + r + 1, :] - m_rep))
        e_negm.append(jnp.exp(-mout_rep))
        sw = tab_ref[c, T_SRCW + r:T_SRCW + r + 1, :]
        lhs = jnp.concatenate([vt[h * 128:(h + 1) * 128, :] * sw, jnp.broadcast_to(sw, (128, L))],
                              axis=0).astype(BF16)
        blk = _dot(lhs, kb)
        blk = jnp.where(_head_lane_mask(blk.shape, h), blk, 0.0)
        upd = blk if upd is None else upd + blk
        cr = tab_ref[c, T_CARRY + r:T_CARRY + r + 1, :]
        cr = jnp.where(_head_lane_mask((1, HK), h), jnp.concatenate([cr, cr], axis=1), 0.0)
        carry = cr if carry is None else carry + cr
    return dict(rows=rows, qm=qm, pv=pv, w_inter=w_inter, e_negm=e_negm, upd=upd, carry=carry)


def _mlstm_apply(items):
    L = RCHUNK
    sts = [st_ref[...] for _, st_ref, _ in items]
    a_all = [_dot_nt(p["qm"], st.astype(BF16)) for (p, _, _), st in zip(items, sts)]
    for h in range(HEADS):
        for (p, _, h_ref), a in zip(items, a_all):
            blk = a[h * L:(h + 1) * L, :]
            pv, w, e = p["pv"][h], p["w_inter"][h], p["e_negm"][h]
            num = w * blk[:, 0:128] + pv[:, 0:128]
            den = w * blk[:, 128:256] + pv[:, 128:256]
            h_ref[p["rows"], h * 128:(h + 1) * 128] = num / jnp.maximum(jnp.abs(den), e)
    for (p, st_ref, _), st in zip(items, sts):
        st_ref[...] = p["carry"] * st + p["upd"]


def _rec_kernel(*refs, n, has_init):
    it = iter(refs)

    def take(k):
        return [next(it) for _ in range(k)]

    za_all, zb_all, zg_all, zgn_all = take(4)
    wgf_ref, wgb_ref, bgf_ref, bgb_ref, gna_ref, bg_ref, gnb_ref = take(7)
    if has_init:
        init_all = take(7)
    out_all = take(8)
    (gt_all,) = take(1)
    set_a_all = take(6)
    set_b_all = take(6)
    work_all = take(9)
    L = RCHUNK
    nc = n // L
    step = pl.program_id(0)
    elems = []
    for e in range(za_all.shape[0]):
        el = dict(za=za_all.at[e], zb=zb_all.at[e], zg=zg_all.at[e], zgn=zgn_all.at[e],
                  gt=gt_all.at[e], set_a=[r.at[e] for r in set_a_all],
                  set_b=[r.at[e] for r in set_b_all])
        (el["oa"], el["sf"], el["sb"], el["om"], el["cf"], el["cb"], el["n_out"],
         el["m_out"]) = [r.at[e] for r in out_all]
        (el["of"], el["ob"], el["hf"], el["hb"], el["gsf"], el["gsb"], el["msf"],
         el["msb"], el["b_exact"]) = [r.at[e] for r in work_all]
        if has_init:
            el["init"] = [r.at[e] for r in init_all]
        elems.append(el)

    def fill(gate_ref, m0, bufs, gt_ref):
        laf_ref, lab_ref, tab_ref, col_ref, mf_ref, span_ref = bufs
        zg = gate_ref[...]
        laf = _log_sigmoid(_dot_hl(zg, wgf_ref[...]) + bgf_ref[...]) / A_GATE_TEMP
        lab = _log_sigmoid(_dot_hl(zg, wgb_ref[...]) + bgb_ref[...]) / A_GATE_TEMP
        laf_ref[...] = laf
        lab_ref[...] = lab
        span = None
        for la in (laf, lab):
            for c in range(nc):
                tot = -jnp.sum(la[c * L:(c + 1) * L, :], axis=0, keepdims=True)
                span = tot if span is None else jnp.maximum(span, tot)
        span_ref[...] = jnp.broadcast_to(jnp.max(span, axis=1, keepdims=True), (8, 128))
        g = zg + bg_ref[...]
        lane = lax.broadcasted_iota(jnp.int32, g.shape, 1)
        is_forget = (lane >= GATE_COL + 8) & (lane < GATE_COL + 16)
        g = jnp.where(is_forget, _log_sigmoid(g), g)
        m_fin = _mlstm_gate_tables(g, m0, gt_ref, tab_ref, col_ref, n)
        mf_ref[...] = jnp.broadcast_to(m_fin, (8, 128))

    for el in elems:
        if has_init:
            sf0_ref, sb0_ref, cf0_ref, cb0_ref, n0_ref, m0_ref, m0n_ref = el["init"]
            el["m0"], el["m0_next"] = m0_ref[:, 0:1], m0n_ref[:, 0:1]
            el["gsf"][...] = sf0_ref[...]
            el["gsb"][...] = sb0_ref[...]
            el["msf"][...] = jnp.concatenate(
                [cf0_ref[...], jnp.broadcast_to(n0_ref[0:1, :], (128, HK))], axis=0)
            el["msb"][...] = jnp.concatenate(
                [cb0_ref[...], jnp.broadcast_to(n0_ref[1:2, :], (128, HK))], axis=0)
        else:
            el["m0"] = el["m0_next"] = jnp.zeros((8, 1), F32)
            for name in ("gsf", "gsb", "msf", "msb"):
                el[name][...] = jnp.zeros(el[name].shape, F32)

    @pl.when(step == 0)
    def _():
        for el in elems:
            fill(el["zg"], el["m0"], el["set_a"], el["gt"])

    m_lo, m_up = _tri_masks(L)
    tri_lo = jnp.where(m_lo, 1.0, 0.0).astype(BF16)
    tri_up = jnp.where(m_up, 1.0, 0.0).astype(BF16)
    gna = gna_ref[...]
    gnb = gnb_ref[...]

    def out_body(c, carry):
        rows = pl.ds(pl.multiple_of(c * L, L), L)
        for h in range(HEADS):
            cols = slice(h * 128, (h + 1) * 128)
            gcols = slice(2 * HK + HV + h * 128, 2 * HK + HV + (h + 1) * 128)
            for el in elems:
                o = el["of"][rows, cols] + el["ob"][rows, cols]
                el["oa"][rows, cols] = (
                    _rms(o, gna) * _silu(el["za"][rows, gcols])).astype(BF16)
                hh = el["hf"][rows, cols] + el["hb"][rows, cols]
                el["om"][rows, cols] = (
                    _sigmoid(el["zb"][rows, gcols]) * _rms(hh, gnb)).astype(BF16)
        return carry

    def run(cur, nxt):
        for el in elems:
            fill(el["zgn"], el["m0_next"], el[nxt], el["gt"])

        def scan_body(gidx, carry):
            waves = []
            for w in range(2):
                c = 2 * gidx + w
                cb = nc - 1 - c
                gla_items, mlstm_items = [], []
                for el in elems:
                    laf_ref, lab_ref, tab_ref, col_ref = el[cur][0:4]
                    gla_items += [
                        (_gla_prepare(el["za"], laf_ref, pl.multiple_of(c * L, L), False, tri_lo),
                         el["gsf"], el["of"], m_lo),
                        (_gla_prepare(el["za"], lab_ref, pl.multiple_of(cb * L, L), True, tri_up),
                         el["gsb"], el["ob"], m_up)]
                    mlstm_items += [
                        (_mlstm_prepare(el["zb"], tab_ref, col_ref, c, 0, m_lo), el["msf"], el["hf"]),
                        (_mlstm_prepare(el["zb"], tab_ref, col_ref, cb, 1, m_up), el["msb"], el["hb"])]
                waves.append((gla_items, mlstm_items))
            for gla_items, mlstm_items in waves:
                _gla_apply(gla_items)
                _mlstm_apply(mlstm_items)
            return carry

        if nc == 2:
            scan_body(0, 0)
        else:
            lax.fori_loop(0, nc // 2, scan_body, 0)
        cur_all = set_a_all if cur == "set_a" else set_b_all
        worst = None
        for el in elems:
            span = el[cur][5][0, 0]
            worst = span if worst is None else jnp.maximum(worst, span)

        @pl.when(worst > GLA_SAFE_SPAN)
        def _():
            of_all, ob_all, _, _, gsf_all, gsb_all, _, _, bex_all = work_all

            def redo(e, carry):
                zero = jnp.zeros((128, HK), F32)
                init_f = init_all[0].at[e][...] if has_init else zero
                init_b = init_all[1].at[e][...] if has_init else zero
                _gla_exact_scan(za_all.at[e], cur_all[0].at[e], gsf_all.at[e], of_all.at[e],
                                bex_all.at[e], n, False, init_f)
                _gla_exact_scan(za_all.at[e], cur_all[1].at[e], gsb_all.at[e], ob_all.at[e],
                                bex_all.at[e], n, True, init_b)
                return carry

            lax.fori_loop(0, len(elems), redo, 0)

        if nc == 2:
            for c in range(nc):
                out_body(c, 0)
        else:
            lax.fori_loop(0, nc, out_body, 0)
        for el in elems:
            el["m_out"][...] = el[cur][4][...]

    @pl.when(step % 2 == 0)
    def _():
        run("set_a", "set_b")

    @pl.when(step % 2 == 1)
    def _():
        run("set_b", "set_a")

    for el in elems:
        gsf_t = el["gsf"][...].T
        gsb_t = el["gsb"][...].T
        msf = el["msf"][...]
        msb = el["msb"][...]
        ctf_t = msf[0:128, :].T
        ctb_t = msb[0:128, :].T
        for h in range(HEADS):
            rows = slice(h * 64, (h + 1) * 64)
            el["sf"][h] = gsf_t[rows, :]
            el["sb"][h] = gsb_t[rows, :]
            el["cf"][h] = ctf_t[rows, :]
            el["cb"][h] = ctb_t[rows, :]
        el["n_out"][...] = jnp.concatenate(
            [msf[128:129, :], msb[128:129, :], jnp.zeros((6, HK), F32)], axis=0)


def _recurrent(za, zb, zg, wgf, wgb, bgf, bgb, gna, bg, gnb, init, bt):
    bsz, n, _ = za.shape
    has_init = init is not None
    nc = n // RCHUNK
    steps = bsz // bt
    per_b3 = lambda b: (b, 0, 0)
    next_b3 = lambda b: (jnp.minimum(b + 1, steps - 1), 0, 0)
    per_b4 = lambda b: (b, 0, 0, 0)
    fixed = lambda b: (0, 0)
    in_specs = [pl.BlockSpec((bt, n, ZA_W), per_b3),
                pl.BlockSpec((bt, n, ZA_W), per_b3),
                pl.BlockSpec((bt, n, ZG_W), per_b3),
                pl.BlockSpec((bt, n, ZG_W), next_b3),
                pl.BlockSpec((ZG_W, HK), fixed),
                pl.BlockSpec((ZG_W, HK), fixed),
                pl.BlockSpec((1, HK), fixed),
                pl.BlockSpec((1, HK), fixed),
                pl.BlockSpec((1, 128), fixed),
                pl.BlockSpec((1, ZG_W), fixed),
                pl.BlockSpec((1, 128), fixed)]
    args = [za, zb, zg, zg, wgf, wgb, bgf, bgb, gna, bg, gnb]
    if has_init:
        sf0, sb0, cf0, cb0, n0, m0 = init
        in_specs += [pl.BlockSpec((bt, 128, HK), per_b3)] * 4
        in_specs += [pl.BlockSpec((bt, 8, HK), per_b3), pl.BlockSpec((bt, 8, 128), per_b3),
                     pl.BlockSpec((bt, 8, 128), next_b3)]
        args += [sf0, sb0, cf0, cb0, n0, m0, m0]
    state = jax.ShapeDtypeStruct((bsz, HEADS, 64, 128), F32)
    state_spec = pl.BlockSpec((bt, HEADS, 64, 128), per_b4)

    def vmem(*shape):
        return pltpu.VMEM((bt,) + shape, F32)

    gate_set = [vmem(n, HK), vmem(n, HK), vmem(nc, ZG_W, RCHUNK), vmem(n, ZG_W), vmem(8, 128),
                vmem(8, 128)]
    return pl.pallas_call(
        functools.partial(_rec_kernel, n=n, has_init=has_init),
        out_shape=(jax.ShapeDtypeStruct((bsz, n, HV), BF16), state, state,
                   jax.ShapeDtypeStruct((bsz, n, HV), BF16), state, state,
                   jax.ShapeDtypeStruct((bsz, 8, HK), F32),
                   jax.ShapeDtypeStruct((bsz, 8, 128), F32)),
        grid=(steps,),
        in_specs=in_specs,
        out_specs=(pl.BlockSpec((bt, n, HV), per_b3), state_spec, state_spec,
                   pl.BlockSpec((bt, n, HV), per_b3), state_spec, state_spec,
                   pl.BlockSpec((bt, 8, HK), per_b3),
                   pl.BlockSpec((bt, 8, 128), per_b3)),
        scratch_shapes=([vmem(ZG_W, n)] + gate_set + gate_set + [vmem(n, HV)] * 4
                        + [vmem(128, HK)] * 2 + [vmem(2 * 128, HK)] * 2 + [vmem(RCHUNK, HK)]),
        compiler_params=_cparams(("arbitrary",)),
        name="recurrent",
    )(*args)


def _mix_residual(parts, wo_ref, x_ref, gt1_ref, gpost1_ref, gpre2_ref, sc2_ref, sh2_ref):
    m = None
    off = 0
    for p in parts:
        kp = p.shape[1]
        t = _dot(p[...], wo_ref[off:off + kp, :])
        m = t if m is None else m + t
        off += kp
    x1 = x_ref[...] + gt1_ref[...] * _rms(m, gpost1_ref[...])
    hb = (_rms(x1, gpre2_ref[...]) * (1.0 + sc2_ref[...]) + sh2_ref[...]).astype(BF16)
    return x1, hb


def _ffn_gated_up(hb, wu_ref, cw_ref, cb_ref, act_ref, seq, tf):
    tm = hb.shape[0]
    pos = lax.broadcasted_iota(jnp.int32, (tm, 1), 0) % seq
    first = pos == 0
    last = pos == seq - 1

    def conv(u, cols):
        prev = jnp.where(first, 0.0, pltpu.roll(u, 1, axis=0))
        nxt = jnp.where(last, 0.0, pltpu.roll(u, tm - 1, axis=0))
        return (cw_ref[0:1, cols] * prev + cw_ref[1:2, cols] * u + cw_ref[2:3, cols] * nxt
                + cb_ref[:, cols])

    for j in range(D_FF // tf):
        ca = slice(j * tf, (j + 1) * tf)
        cg = slice(D_FF + j * tf, D_FF + (j + 1) * tf)
        a = conv(_dot(hb, wu_ref[:, ca]), ca)
        g = conv(_dot(hb, wu_ref[:, cg]), cg)
        act_ref[:, ca] = (_silu(g) * a).astype(BF16)


def _mix_ffn_kernel(*refs, n_parts, seq, tf):
    parts = refs[:n_parts]
    (wo_ref, x_ref, gt1_ref, gpost1_ref, gpre2_ref, sc2_ref, sh2_ref,
     wu_ref, cw_ref, cb_ref, wd_ref, gt2_ref, gpost2_ref, o_ref, act_ref) = refs[n_parts:]
    x1, hb = _mix_residual(parts, wo_ref, x_ref, gt1_ref, gpost1_ref, gpre2_ref, sc2_ref, sh2_ref)
    o_ref[...] = x1
    _ffn_gated_up(hb, wu_ref, cw_ref, cb_ref, act_ref, seq, tf)
    y = _dot(act_ref[...], wd_ref[...])
    o_ref[...] = o_ref[...] + gt2_ref[...] * _rms(y, gpost2_ref[...])


def _resident(shape):
    return pl.BlockSpec(shape, lambda *_: (0,) * len(shape), pipeline_mode=pl.Buffered(1))


def _mix_ffn(parts, w_out, x, mod, first_row, groups, gpost1, gpre2, w_up, conv_w, conv_b, w_down,
             gpost2, tm, seq, tf=256):
    t = x.shape[0]
    per = (t // tm) // groups
    row = lambda i: (i, 0)
    vec = _resident((1, D_MODEL))
    in_specs = [pl.BlockSpec((tm, p.shape[1]), row) for p in parts]
    in_specs += [_resident(w_out.shape), pl.BlockSpec((tm, D_MODEL), row),
                 _mod_spec(MOD_GATE1, first_row, per), vec, vec,
                 _mod_spec(MOD_SCALE2, first_row, per), _mod_spec(MOD_SHIFT2, first_row, per),
                 _resident(w_up.shape), _resident(conv_w.shape), _resident(conv_b.shape),
                 _resident(w_down.shape), _mod_spec(MOD_GATE2, first_row, per), vec]
    return pl.pallas_call(
        functools.partial(_mix_ffn_kernel, n_parts=len(parts), seq=seq, tf=tf),
        out_shape=jax.ShapeDtypeStruct((t, D_MODEL), F32),
        grid=(t // tm,),
        in_specs=in_specs,
        out_specs=pl.BlockSpec((tm, D_MODEL), row),
        scratch_shapes=[pltpu.VMEM((tm, D_FF), BF16)],
        compiler_params=_cparams(("arbitrary",)),
        name="mix_ffn",
    )(*parts, w_out, x, mod, gpost1, gpre2, mod, mod, w_up, conv_w, conv_b, w_down, mod, gpost2)


def _rope(x, cos, sin_signed):
    lane = lax.broadcasted_iota(jnp.int32, x.shape, 1)
    lower = (lane % 32) < 16
    partner = jnp.where(lower, pltpu.roll(x, 128 - 16, axis=1), pltpu.roll(x, 16, axis=1))
    return x * cos + partner * sin_signed


def _p1_kernel(*refs, rope):
    if rope:
        (x_ref, g_ref, sc_ref, sh_ref, w_ref, gq_ref, wq_ref, gkv_ref, cos_ref, sin_ref,
         q_ref, ckv_ref, kr_ref) = refs
    else:
        (x_ref, g_ref, sc_ref, sh_ref, w_ref, gq_ref, wq_ref, gkv_ref,
         q_ref, ckv_ref, kr_ref) = refs
    h = _rms(x_ref[...], g_ref[...]) * (1.0 + sc_ref[...]) + sh_ref[...]
    z = _dot(h.astype(BF16), w_ref[...])
    qa = _rms(z[:, 0:C_Q_RANK], gq_ref[...])
    q = _dot(qa.astype(BF16), wq_ref[...])
    ckv_ref[...] = _rms(z[:, C_Q_RANK:C_Q_RANK + C_KV_RANK], gkv_ref[...])
    kr = z[:, C_Q_RANK + C_KV_RANK:L1_IN_PAD]
    if rope:
        cos = cos_ref[...]
        sin = sin_ref[...]
        kr = _rope(kr, cos, sin)
    kr_ref[...] = kr[:, 0:C_ROPE]
    for hd in range(C_HEADS):
        c0 = hd * QCAT
        q_ref[:, c0:c0 + 128] = q[:, c0:c0 + 128].astype(q_ref.dtype)
        qr = q[:, c0 + 128:c0 + 256]
        if rope:
            qr = _rope(qr, cos, sin)
        q_ref[:, c0 + 128:c0 + 256] = qr.astype(q_ref.dtype)


def _proj_l1(x, g, mod, first_row, groups, w, gq, wq, gkv, tables, tm):
    t = x.shape[0]
    per = (t // tm) // groups
    row = lambda i: (i, 0)
    fixed = lambda i: (0, 0)
    in_specs = [pl.BlockSpec((tm, D_MODEL), row),
                pl.BlockSpec((1, D_MODEL), fixed),
                _mod_spec(MOD_SCALE1, first_row, per),
                _mod_spec(MOD_SHIFT1, first_row, per),
                pl.BlockSpec(w.shape, fixed),
                pl.BlockSpec((1, C_Q_RANK), fixed),
                pl.BlockSpec(wq.shape, fixed),
                pl.BlockSpec((1, C_KV_RANK), fixed)]
    args = [x, g, mod, mod, w, gq, wq, gkv]
    rope = tables is not None
    if rope:
        n = tables[0].shape[0]
        pos = lambda i: (i % (n // tm), 0)
        in_specs += [pl.BlockSpec((tm, 128), pos)] * 2
        args += list(tables)
    return pl.pallas_call(
        functools.partial(_p1_kernel, rope=rope),
        out_shape=(jax.ShapeDtypeStruct((t, C_HEADS * QCAT), BF16),
                   jax.ShapeDtypeStruct((t, C_KV_RANK), F32),
                   jax.ShapeDtypeStruct((t, C_ROPE), F32)),
        grid=(t // tm,),
        in_specs=in_specs,
        out_specs=(pl.BlockSpec((tm, C_HEADS * QCAT), row),
                   pl.BlockSpec((tm, C_KV_RANK), row),
                   pl.BlockSpec((tm, C_ROPE), row)),
        compiler_params=_cparams(("arbitrary",)),
        name="proj_l1",
    )(*args)


def _attn_kernel(*refs, n_parts, scale):
    q_ref = refs[0]
    kv_parts = refs[1:1 + 2 * n_parts]
    w_ref, o_ref, k_sc, v_sc = refs[1 + 2 * n_parts:]
    bt = q_ref.shape[0]

    @pl.when(pl.program_id(1) == 0)
    def _():
        off = 0
        for p in range(n_parts):
            ckv_ref, kr_ref = kv_parts[2 * p], kv_parts[2 * p + 1]
            nk = ckv_ref.shape[1]
            ones = jnp.ones((nk, C_DV), BF16)
            for e in range(bt):
                kv = _dot(ckv_ref[e].astype(BF16), w_ref[...])
                kr = kr_ref[e].astype(BF16)
                kr2 = jnp.concatenate([kr, kr], axis=1)
                for hd in range(C_HEADS):
                    v0 = C_HEADS * C_NOPE + hd * C_DV
                    k_sc[e, hd, off:off + nk, 0:128] = kv[:, hd * 128:(hd + 1) * 128].astype(BF16)
                    k_sc[e, hd, off:off + nk, 128:256] = kr2
                    v_sc[e, hd, off:off + nk, 0:C_DV] = kv[:, v0:v0 + C_DV].astype(BF16)
                    v_sc[e, hd, off:off + nk, C_DV:2 * C_DV] = ones
            off += nk

    for hd in range(C_HEADS):
        for e in range(bt):
            t = _dot_nt(q_ref[e, :, hd * QCAT:(hd + 1) * QCAT], k_sc[e, hd]) * (scale * LOG2E)
            p = jnp.exp2(t - jnp.max(t, axis=1, keepdims=True))
            ov = _dot(p.astype(BF16), v_sc[e, hd])
            o_ref[e, :, hd * C_DV:(hd + 1) * C_DV] = (
                ov[:, 0:C_DV] / ov[:, C_DV:2 * C_DV]).astype(o_ref.dtype)


def _attention(q, kv_parts, w_kvb, tq, bt):
    bsz, n, _ = q.shape
    nk = sum(p[0].shape[1] for p in kv_parts)
    in_specs = [pl.BlockSpec((bt, tq, C_HEADS * QCAT), lambda b, i: (b, i, 0))]
    args = [q]
    for ckv, kr in kv_parts:
        in_specs.append(pl.BlockSpec((bt, ckv.shape[1], C_KV_RANK), lambda b, i: (b, 0, 0)))
        in_specs.append(pl.BlockSpec((bt, kr.shape[1], C_ROPE), lambda b, i: (b, 0, 0)))
        args += [ckv, kr]
    in_specs.append(pl.BlockSpec(w_kvb.shape, lambda b, i: (0, 0)))
    args.append(w_kvb)
    return pl.pallas_call(
        functools.partial(_attn_kernel, n_parts=len(kv_parts),
                          scale=(C_NOPE + C_ROPE) ** -0.5),
        out_shape=jax.ShapeDtypeStruct((bsz, n, C_HEADS * C_DV), BF16),
        grid=(bsz // bt, n // tq),
        in_specs=in_specs,
        out_specs=pl.BlockSpec((bt, tq, C_HEADS * C_DV), lambda b, i: (b, i, 0)),
        scratch_shapes=[pltpu.VMEM((bt, C_HEADS, nk, QCAT), BF16),
                        pltpu.VMEM((bt, C_HEADS, nk, 2 * C_DV), BF16)],
        compiler_params=_cparams(("arbitrary", "arbitrary")),
        name="mla_attention",
    )(*args)


def _l1_wq_layout(w_qb):
    blocks = []
    zero = jnp.zeros((w_qb.shape[0], C_ROPE), w_qb.dtype)
    for hd in range(C_HEADS):
        base = hd * (C_NOPE + C_ROPE)
        nope = w_qb[:, base:base + C_NOPE]
        rope = w_qb[:, base + C_NOPE:base + C_NOPE + C_ROPE]
        blocks += [nope, rope, zero] if hd % 2 == 0 else [nope, zero, rope]
    return jnp.concatenate(blocks, axis=1).astype(BF16)


def _l1_wkvb_layout(w_kvb):
    w = w_kvb.reshape(C_KV_RANK, C_HEADS, C_NOPE + C_DV)
    k = w[:, :, :C_NOPE].reshape(C_KV_RANK, C_HEADS * C_NOPE)
    v = w[:, :, C_NOPE:].reshape(C_KV_RANK, C_HEADS * C_DV)
    return jnp.concatenate([k, v], axis=1).astype(BF16)


def _rope_tables(n):
    pos = np.arange(n)
    row = (pos // GRID_W).astype(np.float64)
    col = (pos % GRID_W).astype(np.float64)
    half = C_ROPE // 2
    inv = 1.0 / (ROPE_THETA ** (np.arange(0, half, 2, dtype=np.float64) / half))
    ang_r = row[:, None] * inv[None, :]
    ang_c = col[:, None] * inv[None, :]
    cos64 = np.concatenate([np.cos(ang_r)] * 2 + [np.cos(ang_c)] * 2, axis=1)
    sin64 = np.concatenate([-np.sin(ang_r), np.sin(ang_r), -np.sin(ang_c), np.sin(ang_c)], axis=1)
    return (jnp.asarray(np.concatenate([cos64] * 2, axis=1), F32),
            jnp.asarray(np.concatenate([sin64] * 2, axis=1), F32))


def _gate_weight(w, first_row):
    return jnp.pad(w, ((first_row, ZG_W - first_row - A_GATE_RANK), (0, 0)))


def _state_t(s):
    b = s.shape[0]
    return s.transpose(0, 3, 1, 2).reshape(b, 128, HK)


def _n_rows(n_fwd, n_bwd):
    b = n_fwd.shape[0]
    rows = jnp.zeros((b, 8, HK), F32)
    return rows.at[:, 0, :].set(n_fwd.reshape(b, HK)).at[:, 1, :].set(n_bwd.reshape(b, HK))


def _m_rows(m_fwd, m_bwd):
    m = jnp.concatenate([m_fwd, m_bwd], axis=1)
    return jnp.broadcast_to(m[:, :, None], (m.shape[0], 8, 128))


def kernel(x_prompt, x_sample, state_l0_gla_fwd, state_l0_gla_bwd, state_l0_mlstm_c_fwd, state_l0_mlstm_n_fwd, state_l0_mlstm_m_fwd, state_l0_mlstm_c_bwd, state_l0_mlstm_n_bwd, state_l0_mlstm_m_bwd, cache_l1_ckv, cache_l1_krope, c, c_ctx, l0_w_mod, l0_b_mod, l0_g_pre_mix, l0_g_post_mix, l0_g_pre_ffn, l0_g_post_ffn, l0_w_in, l0_gla_w_gate_f, l0_gla_b_gate_f, l0_gla_w_gate_b, l0_gla_b_gate_b, l0_gla_g_norm, l0_mlstm_b_gates, l0_mlstm_g_norm, l0_w_out, l0_ffn_w_up, l0_ffn_conv_w, l0_ffn_conv_b, l0_ffn_w_down, l1_w_mod, l1_b_mod, l1_g_pre_mix, l1_g_post_mix, l1_g_pre_ffn, l1_g_post_ffn, l1_w_in, l1_g_q_norm, l1_w_qb, l1_g_kv_norm, l1_w_kvb, l1_w_out, l1_ffn_w_up, l1_ffn_conv_w, l1_ffn_conv_b, l1_ffn_w_down):
    bp, sp, d = x_prompt.shape
    bs, ss, _ = x_sample.shape
    row = lambda v: v.reshape(1, -1)

    cond8 = jnp.concatenate([c_ctx[None, :], c, jnp.zeros((8 - 1 - bs, d), F32)], axis=0)
    mods = (_modulation(cond8, l0_w_mod, l0_b_mod), _modulation(cond8, l1_w_mod, l1_b_mod))

    paths = {
        "ctx": dict(x=x_prompt.reshape(bp * sp, d), b=bp, n=sp, first_row=0, groups=1, tm=512),
        "lat": dict(x=x_sample.reshape(bs * ss, d), b=bs, n=ss, first_row=1, groups=bs, tm=512),
    }

    w_in0 = l0_w_in.T
    wgf = _gate_weight(l0_gla_w_gate_f, 0)
    wgb = _gate_weight(l0_gla_w_gate_b, A_GATE_RANK)
    bg = jnp.pad(l0_mlstm_b_gates, (GATE_COL, ZG_W - GATE_COL - 16)).reshape(1, ZG_W)
    w_out0 = l0_w_out.astype(BF16)
    w_in1 = jnp.pad(l1_w_in, ((0, 0), (0, L1_IN_PAD - l1_w_in.shape[1]))).astype(BF16)
    w_q1 = _l1_wq_layout(l1_w_qb)
    w_kvb1 = _l1_wkvb_layout(l1_w_kvb)
    w_out1 = l1_w_out.astype(BF16)
    ffn = ((l0_ffn_w_up.astype(BF16), l0_ffn_conv_w, row(l0_ffn_conv_b), l0_ffn_w_down.astype(BF16)),
           (l1_ffn_w_up.astype(BF16), l1_ffn_conv_w, row(l1_ffn_conv_b), l1_ffn_w_down.astype(BF16)))
    norms = ((row(l0_g_pre_mix), row(l0_g_post_mix), row(l0_g_pre_ffn), row(l0_g_post_ffn)),
             (row(l1_g_pre_mix), row(l1_g_post_mix), row(l1_g_pre_ffn), row(l1_g_post_ffn)))

    gla_init = (_state_t(state_l0_gla_fwd), _state_t(state_l0_gla_bwd))
    mlstm_init = (_state_t(state_l0_mlstm_c_fwd), _state_t(state_l0_mlstm_c_bwd),
                  _n_rows(state_l0_mlstm_n_fwd, state_l0_mlstm_n_bwd),
                  _m_rows(state_l0_mlstm_m_fwd, state_l0_mlstm_m_bwd))
    tables = _rope_tables(ss)

    results = {}
    for name, p in paths.items():
        x, b, n, tm = p["x"], p["b"], p["n"], p["tm"]
        first_row, groups = p["first_row"], p["groups"]
        is_ctx = name == "ctx"
        ffn_tm = tm if is_ctx else n
        g_pre, g_post, g_pre2, g_post2 = norms[0]

        za, zb, zg = _proj_l0(x, g_pre, mods[0], first_row, groups, w_in0, tm)
        za, zb, zg = (t.reshape(b, n, -1) for t in (za, zb, zg))
        oa, sa_f, sa_b, om, c_f, c_b, n_rows, m_rows = _recurrent(
            za, zb, zg, wgf, wgb, row(l0_gla_b_gate_f), row(l0_gla_b_gate_b), row(l0_gla_g_norm),
            bg, row(l0_mlstm_g_norm), None if is_ctx else gla_init + mlstm_init,
            2 if is_ctx else 1)
        x = _mix_ffn([oa.reshape(b * n, HV), om.reshape(b * n, HV)], w_out0, x, mods[0], first_row,
                     groups, g_post, g_pre2, *ffn[0], g_post2, ffn_tm, n)

        g_pre, g_post, g_pre2, g_post2 = norms[1]
        q, ckv, kr = _proj_l1(x, g_pre, mods[1], first_row, groups, w_in1, row(l1_g_q_norm), w_q1,
                              row(l1_g_kv_norm), None if is_ctx else tables,
                              2 * tm if is_ctx else tm)
        ckv3 = ckv.reshape(b, n, C_KV_RANK)
        kr3 = kr.reshape(b, n, C_ROPE)
        kv_parts = [(ckv3, kr3)] if is_ctx else [(cache_l1_ckv, cache_l1_krope), (ckv3, kr3)]
        o = _attention(q.reshape(b, n, -1), kv_parts, w_kvb1, 256 if is_ctx else 512,
                       4 if is_ctx else 1)
        x = _mix_ffn([o.reshape(b * n, C_HEADS * C_DV)], w_out1, x, mods[1], first_row,
                     groups, g_post, g_pre2, *ffn[1], g_post2, ffn_tm, n)

        results[name] = dict(y=x.reshape(b, n, d), gla=(sa_f, sa_b), c=(c_f, c_b),
                             n=n_rows, m=m_rows, ckv=ckv3, kr=kr3)

    r = results["ctx"]
    n_fwd = r["n"][:, 0, :].reshape(bp, B_HEADS, B_DK)
    n_bwd = r["n"][:, 1, :].reshape(bp, B_HEADS, B_DK)
    m_fwd = r["m"][:, 0:B_HEADS, 0]
    m_bwd = r["m"][:, B_HEADS:2 * B_HEADS, 0]
    return (r["y"], results["lat"]["y"], r["gla"][0], r["gla"][1],
            r["c"][0], n_fwd, m_fwd, r["c"][1], n_bwd, m_bwd, r["ckv"], r["kr"])
```
